```python
import jax, jax.numpy as jnp
from jax import lax
import numpy as np

D_MODEL = 1024
BATCH = 8
SEQ = 8192
DEPTH = 1

GRID_W = 64
CTX_LEN = 256
CONF_WIDTH = D_MODEL
CONF_KERNEL = 31
LRU_WIDTH = 1280
LRU_BLOCKS = 10
LRU_BLOCK_W = LRU_WIDTH // LRU_BLOCKS
LRU_CONV = 4
LRU_PAD = (2, 1)
LRU_C = 8.0
FFN_HIDDEN = 2816
N_MOD = 9
EPS = 1e-6
COL_LRU_X = 2 * CONF_WIDTH
COL_LRU_G = COL_LRU_X + LRU_WIDTH
COL_GATE = COL_LRU_G + LRU_WIDTH
D_IN = COL_GATE + 2 * D_MODEL

kernel_name = "hybrid_conformer_rglru_prefix_block"


def rmsnorm(x, g):
    x32 = x.astype(jnp.float32)
    y = x32 * lax.rsqrt(jnp.mean(x32 * x32, axis=-1, keepdims=True) + EPS)
    return y.astype(x.dtype) * g


def layernorm(x, g, b):
    x32 = x.astype(jnp.float32)
    mu = jnp.mean(x32, axis=-1, keepdims=True)
    xc = x32 - mu
    y = xc * lax.rsqrt(jnp.mean(xc * xc, axis=-1, keepdims=True) + EPS)
    return y.astype(x.dtype) * g + b


def modulate(x, shift, scale):
    return x * (1 + scale) + shift


def depthwise_conv(x, w, b, pad):
    y = lax.conv_general_dilated(x, w[:, None, :], window_strides=(1,), padding=[pad],
                                 dimension_numbers=("NWC", "WIO", "NWC"),
                                 feature_group_count=x.shape[-1])
    return y + b


def swiglu(x, w_up, w_down):
    g, u = jnp.split(x @ w_up, 2, axis=-1)
    return (jax.nn.silu(g) * u) @ w_down


def grid_pos_embedding(seq_len, dim):
    rows = seq_len // GRID_W
    t = jnp.arange(rows * GRID_W)
    row = (t // GRID_W).astype(jnp.float32)
    col = (t % GRID_W).astype(jnp.float32)
    q = dim // 4
    omega = 1.0 / (10000.0 ** (jnp.arange(q, dtype=jnp.float32) / q))
    er = row[:, None] * omega
    ec = col[:, None] * omega
    return jnp.concatenate([jnp.sin(er), jnp.cos(er), jnp.sin(ec), jnp.cos(ec)], axis=-1)


def _combine(left, right):
    a1, b1 = left
    a2, b2 = right
    return a1 * a2, a2 * b1 + b2


def linear_scan(a, b, h0, reverse):
    first = -1 if reverse else 0
    b = b.at[:, first].add(a[:, first] * h0)
    _, h = lax.associative_scan(_combine, (a, b), axis=1, reverse=reverse)
    return h


def rglru_direction(xr, w_a, b_a, w_x, b_x, lam, h0, reverse):
    B, T, _ = xr.shape
    xb = xr.reshape(B, T, LRU_BLOCKS, LRU_BLOCK_W)
    r = jax.nn.sigmoid(jnp.einsum("bthi,hij->bthj", xb, w_a).reshape(B, T, LRU_WIDTH) + b_a)
    ig = jax.nn.sigmoid(jnp.einsum("bthi,hij->bthj", xb, w_x).reshape(B, T, LRU_WIDTH) + b_x)
    log_a = -LRU_C * r.astype(jnp.float32) * jax.nn.softplus(-lam.astype(jnp.float32))
    a = jnp.exp(log_a)
    bb = jnp.sqrt(-jnp.expm1(2.0 * log_a)) * (ig * xr).astype(jnp.float32)
    return linear_scan(a, bb, h0, reverse)


def rglru_scans(u_x, lp, h0f, h0b):
    xr = depthwise_conv(u_x, lp["w_lru_conv"], lp["b_lru_conv"], LRU_PAD)
    hf = rglru_direction(xr, lp["w_rec_gate"][0], lp["b_rec_gate"][0], lp["w_in_gate"][0],
                         lp["b_in_gate"][0], lp["lru_lambda"][0], h0f, False)
    hb = rglru_direction(xr, lp["w_rec_gate"][1], lp["b_rec_gate"][1], lp["w_in_gate"][1],
                         lp["b_in_gate"][1], lp["lru_lambda"][1], h0b, True)
    return xr, hf, hb


def conformer_branch(u_glu, lp):
    v, gt = jnp.split(u_glu, 2, axis=-1)
    u = v * jax.nn.sigmoid(gt)
    u = depthwise_conv(u, lp["w_dw"], lp["b_dw"], (CONF_KERNEL // 2, CONF_KERNEL // 2))
    u = jax.nn.silu(layernorm(u, lp["g_ln"], lp["b_ln"]))
    return u @ lp["w_conf_out"]


def mixer(h, lp, h0f, h0b):
    proj = h @ lp["w_in"] + lp["b_in"]
    y_conf = conformer_branch(proj[..., :COL_LRU_X], lp)
    xr, hf, hb = rglru_scans(proj[..., COL_LRU_X:COL_LRU_G], lp, h0f, h0b)
    y_lru = ((hf + hb).astype(xr.dtype) * jax.nn.gelu(proj[..., COL_LRU_G:COL_GATE])) @ lp["w_lru_out"]
    g_conf, g_lru = jnp.split(jax.nn.sigmoid(proj[..., COL_GATE:]), 2, axis=-1)
    y = (g_conf * y_conf + g_lru * y_lru) @ lp["w_out"]
    return y, hf[:, -1], hb[:, 0]


def context_lru_states(hc, lp):
    u_x = hc @ lp["w_in"][:, COL_LRU_X:COL_LRU_G] + lp["b_in"][COL_LRU_X:COL_LRU_G]
    h0 = jnp.zeros((hc.shape[0], LRU_WIDTH), jnp.float32)
    _, hf, hb = rglru_scans(u_x, lp, h0, h0)
    return hf[:, -1], hb[:, 0]


def layer(x, xc, c, c_ctx, lp, update_context):
    m = jnp.split((jax.nn.silu(c) @ lp["w_mod"] + lp["b_mod"])[:, None, :], N_MOD, axis=-1)
    mc = jnp.split((jax.nn.silu(c_ctx) @ lp["w_mod"] + lp["b_mod"])[None, None, :], N_MOD, axis=-1)
    x = x + 0.5 * m[2] * swiglu(modulate(rmsnorm(x, lp["g_n1"]), m[0], m[1]), lp["w_ffn1_up"], lp["w_ffn1_down"])
    xc = xc + 0.5 * mc[2] * swiglu(modulate(rmsnorm(xc, lp["g_n1"]), mc[0], mc[1]), lp["w_ffn1_up"], lp["w_ffn1_down"])
    hc = modulate(rmsnorm(xc, lp["g_n2"]), mc[3], mc[4])
    if update_context:
        h0 = jnp.zeros((hc.shape[0], LRU_WIDTH), jnp.float32)
        yc, hf_c, hb_c = mixer(hc, lp, h0, h0)
        xc = xc + mc[5] * yc
    else:
        hf_c, hb_c = context_lru_states(hc, lp)
    h = modulate(rmsnorm(x, lp["g_n2"]), m[3], m[4])
    y, _, _ = mixer(h, lp, hf_c, hb_c)
    x = x + m[5] * y
    x = x + 0.5 * m[8] * swiglu(modulate(rmsnorm(x, lp["g_n3"]), m[6], m[7]), lp["w_ffn2_up"], lp["w_ffn2_down"])
    if update_context:
        xc = xc + 0.5 * mc[8] * swiglu(modulate(rmsnorm(xc, lp["g_n3"]), mc[6], mc[7]), lp["w_ffn2_up"], lp["w_ffn2_down"])
    return x, xc


def _fwd_setup_inputs(seed: int = 0) -> dict:
    key = jax.random.key(seed)
    ks = jax.random.split(key, 32)
    L, D, F = DEPTH, D_MODEL, FFN_HIDDEN

    def nrm(k, shape, scale=1.0):
        return scale * jax.random.normal(k, shape, jnp.float32)

    u = jax.random.uniform(ks[23], (L, 2, LRU_WIDTH), jnp.float32, minval=0.9, maxval=0.999)
    a = u ** (1.0 / LRU_C)
    lam = jnp.log(a) - jnp.log1p(-a)
    return {
        "x": nrm(ks[0], (BATCH, SEQ, D)),
        "c": nrm(ks[1], (BATCH, D)),
        "ctx": nrm(ks[2], (BATCH, CTX_LEN, D)),
        "c_ctx": nrm(ks[3], (D,)),
        "w_mod": nrm(ks[4], (L, D, N_MOD * D), 0.5 * D ** -0.5),
        "b_mod": nrm(ks[5], (L, N_MOD * D), 0.02),
        "g_n1": 1.0 + nrm(ks[6], (L, D), 0.1),
        "w_ffn1_up": nrm(ks[7], (L, D, 2 * F), D ** -0.5),
        "w_ffn1_down": nrm(ks[8], (L, F, D), F ** -0.5),
        "g_n2": 1.0 + nrm(ks[9], (L, D), 0.1),
        "w_in": nrm(ks[10], (L, D, D_IN), D ** -0.5),
        "b_in": nrm(ks[11], (L, D_IN), 0.02),
        "w_dw": nrm(ks[12], (L, CONF_KERNEL, CONF_WIDTH), CONF_KERNEL ** -0.5),
        "b_dw": nrm(ks[13], (L, CONF_WIDTH), 0.02),
        "g_ln": 1.0 + nrm(ks[14], (L, CONF_WIDTH), 0.1),
        "b_ln": nrm(ks[15], (L, CONF_WIDTH), 0.02),
        "w_conf_out": nrm(ks[16], (L, CONF_WIDTH, D), CONF_WIDTH ** -0.5),
        "w_lru_conv": nrm(ks[17], (L, LRU_CONV, LRU_WIDTH), LRU_CONV ** -0.5),
        "b_lru_conv": nrm(ks[18], (L, LRU_WIDTH), 0.02),
        "w_rec_gate": nrm(ks[19], (L, 2, LRU_BLOCKS, LRU_BLOCK_W, LRU_BLOCK_W), LRU_BLOCK_W ** -0.5),
        "b_rec_gate": nrm(ks[20], (L, 2, LRU_WIDTH), 0.02),
        "w_in_gate": nrm(ks[21], (L, 2, LRU_BLOCKS, LRU_BLOCK_W, LRU_BLOCK_W), LRU_BLOCK_W ** -0.5),
        "b_in_gate": nrm(ks[22], (L, 2, LRU_WIDTH), 0.02),
        "lru_lambda": lam,
        "w_lru_out": nrm(ks[24], (L, LRU_WIDTH, D), LRU_WIDTH ** -0.5),
        "w_out": nrm(ks[25], (L, D, D), D ** -0.5),
        "g_n3": 1.0 + nrm(ks[26], (L, D), 0.1),
        "w_ffn2_up": nrm(ks[27], (L, D, 2 * F), D ** -0.5),
        "w_ffn2_down": nrm(ks[28], (L, F, D), F ** -0.5),
        "g_final": 1.0 + nrm(ks[29], (D,), 0.1),
    }


def _fwd_reference(x, c, ctx, c_ctx, w_mod, b_mod, g_n1, w_ffn1_up, w_ffn1_down, g_n2, w_in, b_in,
              w_dw, b_dw, g_ln, b_ln, w_conf_out, w_lru_conv, b_lru_conv, w_rec_gate, b_rec_gate,
              w_in_gate, b_in_gate, lru_lambda, w_lru_out, w_out, g_n3, w_ffn2_up, w_ffn2_down,
              g_final):
    x = x + grid_pos_embedding(x.shape[1], x.shape[2]).astype(x.dtype)[None]
    xc = ctx
    for i in range(DEPTH):
        lp = dict(w_mod=w_mod[i], b_mod=b_mod[i], g_n1=g_n1[i], w_ffn1_up=w_ffn1_up[i],
                  w_ffn1_down=w_ffn1_down[i], g_n2=g_n2[i], w_in=w_in[i], b_in=b_in[i],
                  w_dw=w_dw[i], b_dw=b_dw[i], g_ln=g_ln[i], b_ln=b_ln[i], w_conf_out=w_conf_out[i],
                  w_lru_conv=w_lru_conv[i], b_lru_conv=b_lru_conv[i], w_rec_gate=w_rec_gate[i],
                  b_rec_gate=b_rec_gate[i], w_in_gate=w_in_gate[i], b_in_gate=b_in_gate[i],
                  lru_lambda=lru_lambda[i], w_lru_out=w_lru_out[i], w_out=w_out[i], g_n3=g_n3[i],
                  w_ffn2_up=w_ffn2_up[i], w_ffn2_down=w_ffn2_down[i])
        x, xc = layer(x, xc, c, c_ctx, lp, i < DEPTH - 1)
    y = rmsnorm(x, g_final)
    return y


import jax as _jax
import jax.numpy as _jnp

TWIN_FORMAT = 'train_step'
FWD_PARAMS = ['x', 'c', 'ctx', 'c_ctx', 'w_mod', 'b_mod', 'g_n1', 'w_ffn1_up', 'w_ffn1_down', 'g_n2', 'w_in', 'b_in', 'w_dw', 'b_dw', 'g_ln', 'b_ln', 'w_conf_out', 'w_lru_conv', 'b_lru_conv', 'w_rec_gate', 'b_rec_gate', 'w_in_gate', 'b_in_gate', 'lru_lambda', 'w_lru_out', 'w_out', 'g_n3', 'w_ffn2_up', 'w_ffn2_down', 'g_final']
TWIN_WEIGHTS = ['c_ctx', 'w_mod', 'b_mod', 'g_n1', 'w_ffn1_up', 'w_ffn1_down', 'g_n2', 'w_in', 'b_in', 'w_dw', 'b_dw', 'g_ln', 'b_ln', 'w_conf_out', 'w_lru_conv', 'b_lru_conv', 'w_rec_gate', 'b_rec_gate', 'w_in_gate', 'b_in_gate', 'lru_lambda', 'w_lru_out', 'w_out', 'g_n3', 'w_ffn2_up', 'w_ffn2_down', 'g_final']
TWIN_DIFF_INPUT = 'x'
TWIN_INPUTS = ['x', 'c', 'ctx', 'c_ctx', 'w_mod', 'b_mod', 'g_n1', 'w_ffn1_up', 'w_ffn1_down', 'g_n2', 'w_in', 'b_in', 'w_dw', 'b_dw', 'g_ln', 'b_ln', 'w_conf_out', 'w_lru_conv', 'b_lru_conv', 'w_rec_gate', 'b_rec_gate', 'w_in_gate', 'b_in_gate', 'lru_lambda', 'w_lru_out', 'w_out', 'g_n3', 'w_ffn2_up', 'w_ffn2_down', 'g_final', 'loss_target', 'm_c_ctx', 'm_w_mod', 'm_b_mod', 'm_g_n1', 'm_w_ffn1_up', 'm_w_ffn1_down', 'm_g_n2', 'm_w_in', 'm_b_in', 'm_w_dw', 'm_b_dw', 'm_g_ln', 'm_b_ln', 'm_w_conf_out', 'm_w_lru_conv', 'm_b_lru_conv', 'm_w_rec_gate', 'm_b_rec_gate', 'm_w_in_gate', 'm_b_in_gate', 'm_lru_lambda', 'm_w_lru_out', 'm_w_out', 'm_g_n3', 'm_w_ffn2_up', 'm_w_ffn2_down', 'm_g_final', 'v_c_ctx', 'v_w_mod', 'v_b_mod', 'v_g_n1', 'v_w_ffn1_up', 'v_w_ffn1_down', 'v_g_n2', 'v_w_in', 'v_b_in', 'v_w_dw', 'v_b_dw', 'v_g_ln', 'v_b_ln', 'v_w_conf_out', 'v_w_lru_conv', 'v_b_lru_conv', 'v_w_rec_gate', 'v_b_rec_gate', 'v_w_in_gate', 'v_b_in_gate', 'v_lru_lambda', 'v_w_lru_out', 'v_w_out', 'v_g_n3', 'v_w_ffn2_up', 'v_w_ffn2_down', 'v_g_final']
TWIN_OUTPUTS = ['loss', 'grad_x', 'grad_c_ctx', 'grad_w_mod', 'grad_b_mod', 'grad_g_n1', 'grad_w_ffn1_up', 'grad_w_ffn1_down', 'grad_g_n2', 'grad_w_in', 'grad_b_in', 'grad_w_dw', 'grad_b_dw', 'grad_g_ln', 'grad_b_ln', 'grad_w_conf_out', 'grad_w_lru_conv', 'grad_b_lru_conv', 'grad_w_rec_gate', 'grad_b_rec_gate', 'grad_w_in_gate', 'grad_b_in_gate', 'grad_lru_lambda', 'grad_w_lru_out', 'grad_w_out', 'grad_g_n3', 'grad_w_ffn2_up', 'grad_w_ffn2_down', 'grad_g_final', 'delta_c_ctx', 'delta_w_mod', 'delta_b_mod', 'delta_g_n1', 'delta_w_ffn1_up', 'delta_w_ffn1_down', 'delta_g_n2', 'delta_w_in', 'delta_b_in', 'delta_w_dw', 'delta_b_dw', 'delta_g_ln', 'delta_b_ln', 'delta_w_conf_out', 'delta_w_lru_conv', 'delta_b_lru_conv', 'delta_w_rec_gate', 'delta_b_rec_gate', 'delta_w_in_gate', 'delta_b_in_gate', 'delta_lru_lambda', 'delta_w_lru_out', 'delta_w_out', 'delta_g_n3', 'delta_w_ffn2_up', 'delta_w_ffn2_down', 'delta_g_final', 'new_m_c_ctx', 'new_m_w_mod', 'new_m_b_mod', 'new_m_g_n1', 'new_m_w_ffn1_up', 'new_m_w_ffn1_down', 'new_m_g_n2', 'new_m_w_in', 'new_m_b_in', 'new_m_w_dw', 'new_m_b_dw', 'new_m_g_ln', 'new_m_b_ln', 'new_m_w_conf_out', 'new_m_w_lru_conv', 'new_m_b_lru_conv', 'new_m_w_rec_gate', 'new_m_b_rec_gate', 'new_m_w_in_gate', 'new_m_b_in_gate', 'new_m_lru_lambda', 'new_m_w_lru_out', 'new_m_w_out', 'new_m_g_n3', 'new_m_w_ffn2_up', 'new_m_w_ffn2_down', 'new_m_g_final', 'new_v_c_ctx', 'new_v_w_mod', 'new_v_b_mod', 'new_v_g_n1', 'new_v_w_ffn1_up', 'new_v_w_ffn1_down', 'new_v_g_n2', 'new_v_w_in', 'new_v_b_in', 'new_v_w_dw', 'new_v_b_dw', 'new_v_g_ln', 'new_v_b_ln', 'new_v_w_conf_out', 'new_v_w_lru_conv', 'new_v_b_lru_conv', 'new_v_w_rec_gate', 'new_v_b_rec_gate', 'new_v_w_in_gate', 'new_v_b_in_gate', 'new_v_lru_lambda', 'new_v_w_lru_out', 'new_v_w_out', 'new_v_g_n3', 'new_v_w_ffn2_up', 'new_v_w_ffn2_down', 'new_v_g_final']
TWIN_LEAF_KINDS = {'loss': 'loss', 'grad_x': 'grad_x', 'grad_c_ctx': 'grad_w', 'grad_w_mod': 'grad_w', 'grad_b_mod': 'grad_w', 'grad_g_n1': 'grad_w', 'grad_w_ffn1_up': 'grad_w', 'grad_w_ffn1_down': 'grad_w', 'grad_g_n2': 'grad_w', 'grad_w_in': 'grad_w', 'grad_b_in': 'grad_w', 'grad_w_dw': 'grad_w', 'grad_b_dw': 'grad_w', 'grad_g_ln': 'grad_w', 'grad_b_ln': 'grad_w', 'grad_w_conf_out': 'grad_w', 'grad_w_lru_conv': 'grad_w', 'grad_b_lru_conv': 'grad_w', 'grad_w_rec_gate': 'grad_w', 'grad_b_rec_gate': 'grad_w', 'grad_w_in_gate': 'grad_w', 'grad_b_in_gate': 'grad_w', 'grad_lru_lambda': 'grad_w', 'grad_w_lru_out': 'grad_w', 'grad_w_out': 'grad_w', 'grad_g_n3': 'grad_w', 'grad_w_ffn2_up': 'grad_w', 'grad_w_ffn2_down': 'grad_w', 'grad_g_final': 'grad_w', 'delta_c_ctx': 'delta_w', 'delta_w_mod': 'delta_w', 'delta_b_mod': 'delta_w', 'delta_g_n1': 'delta_w', 'delta_w_ffn1_up': 'delta_w', 'delta_w_ffn1_down': 'delta_w', 'delta_g_n2': 'delta_w', 'delta_w_in': 'delta_w', 'delta_b_in': 'delta_w', 'delta_w_dw': 'delta_w', 'delta_b_dw': 'delta_w', 'delta_g_ln': 'delta_w', 'delta_b_ln': 'delta_w', 'delta_w_conf_out': 'delta_w', 'delta_w_lru_conv': 'delta_w', 'delta_b_lru_conv': 'delta_w', 'delta_w_rec_gate': 'delta_w', 'delta_b_rec_gate': 'delta_w', 'delta_w_in_gate': 'delta_w', 'delta_b_in_gate': 'delta_w', 'delta_lru_lambda': 'delta_w', 'delta_w_lru_out': 'delta_w', 'delta_w_out': 'delta_w', 'delta_g_n3': 'delta_w', 'delta_w_ffn2_up': 'delta_w', 'delta_w_ffn2_down': 'delta_w', 'delta_g_final': 'delta_w', 'new_m_c_ctx': 'new_m', 'new_m_w_mod': 'new_m', 'new_m_b_mod': 'new_m', 'new_m_g_n1': 'new_m', 'new_m_w_ffn1_up': 'new_m', 'new_m_w_ffn1_down': 'new_m', 'new_m_g_n2': 'new_m', 'new_m_w_in': 'new_m', 'new_m_b_in': 'new_m', 'new_m_w_dw': 'new_m', 'new_m_b_dw': 'new_m', 'new_m_g_ln': 'new_m', 'new_m_b_ln': 'new_m', 'new_m_w_conf_out': 'new_m', 'new_m_w_lru_conv': 'new_m', 'new_m_b_lru_conv': 'new_m', 'new_m_w_rec_gate': 'new_m', 'new_m_b_rec_gate': 'new_m', 'new_m_w_in_gate': 'new_m', 'new_m_b_in_gate': 'new_m', 'new_m_lru_lambda': 'new_m', 'new_m_w_lru_out': 'new_m', 'new_m_w_out': 'new_m', 'new_m_g_n3': 'new_m', 'new_m_w_ffn2_up': 'new_m', 'new_m_w_ffn2_down': 'new_m', 'new_m_g_final': 'new_m', 'new_v_c_ctx': 'new_v', 'new_v_w_mod': 'new_v', 'new_v_b_mod': 'new_v', 'new_v_g_n1': 'new_v', 'new_v_w_ffn1_up': 'new_v', 'new_v_w_ffn1_down': 'new_v', 'new_v_g_n2': 'new_v', 'new_v_w_in': 'new_v', 'new_v_b_in': 'new_v', 'new_v_w_dw': 'new_v', 'new_v_b_dw': 'new_v', 'new_v_g_ln': 'new_v', 'new_v_b_ln': 'new_v', 'new_v_w_conf_out': 'new_v', 'new_v_w_lru_conv': 'new_v', 'new_v_b_lru_conv': 'new_v', 'new_v_w_rec_gate': 'new_v', 'new_v_b_rec_gate': 'new_v', 'new_v_w_in_gate': 'new_v', 'new_v_b_in_gate': 'new_v', 'new_v_lru_lambda': 'new_v', 'new_v_w_lru_out': 'new_v', 'new_v_w_out': 'new_v', 'new_v_g_n3': 'new_v', 'new_v_w_ffn2_up': 'new_v', 'new_v_w_ffn2_down': 'new_v', 'new_v_g_final': 'new_v'}


def _forward(args):
    return _fwd_reference(*[args[k] for k in FWD_PARAMS])


def _output_shape():
    def fwd():
        inp = _fwd_setup_inputs(0)
        return _fwd_reference(*[inp[k] for k in FWD_PARAMS])
    out = _jax.eval_shape(fwd)
    return out.shape, out.dtype

N_MICROBATCH = 1
ADAM_LR = 0.001
ADAM_B1 = 0.9
ADAM_B2 = 0.999
ADAM_EPS = 1e-08
ADAM_WD = 0.01
ADAM_STEP = 10
PER_EXAMPLE_BATCH_AXIS = {'x': 0, 'c': 0, 'ctx': 0, 'loss_target': 0}
SHARED_INPUTS = []
_WEIGHT_DTYPES = {'c_ctx': _jnp.float32, 'w_mod': _jnp.float32, 'b_mod': _jnp.float32, 'g_n1': _jnp.float32, 'w_ffn1_up': _jnp.float32, 'w_ffn1_down': _jnp.float32, 'g_n2': _jnp.float32, 'w_in': _jnp.float32, 'b_in': _jnp.float32, 'w_dw': _jnp.float32, 'b_dw': _jnp.float32, 'g_ln': _jnp.float32, 'b_ln': _jnp.float32, 'w_conf_out': _jnp.float32, 'w_lru_conv': _jnp.float32, 'b_lru_conv': _jnp.float32, 'w_rec_gate': _jnp.float32, 'b_rec_gate': _jnp.float32, 'w_in_gate': _jnp.float32, 'b_in_gate': _jnp.float32, 'lru_lambda': _jnp.float32, 'w_lru_out': _jnp.float32, 'w_out': _jnp.float32, 'g_n3': _jnp.float32, 'w_ffn2_up': _jnp.float32, 'w_ffn2_down': _jnp.float32, 'g_final': _jnp.float32}
MOMENT_SCALE = {'c_ctx': 3.589352e-02, 'w_mod': 9.235300e-01, 'b_mod': 2.009965e+00, 'g_n1': 6.004295e-02, 'w_ffn1_up': 2.973082e-02, 'w_ffn1_down': 5.745139e-02, 'g_n2': 1.049434e+00, 'w_in': 4.491870e-01, 'b_in': 7.541745e-01, 'w_dw': 7.906299e-02, 'b_dw': 2.979758e-01, 'g_ln': 1.413522e-01, 'b_ln': 1.789220e-01, 'w_conf_out': 1.050922e-01, 'w_lru_conv': 8.979519e-01, 'b_lru_conv': 1.350193e+00, 'w_rec_gate': 3.820435e-02, 'b_rec_gate': 6.759710e-02, 'w_in_gate': 9.243162e-02, 'b_in_gate': 1.568874e-01, 'lru_lambda': 1.800273e-01, 'w_lru_out': 9.447583e-01, 'w_out': 9.550966e-01, 'g_n3': 4.390128e-02, 'w_ffn2_up': 2.225313e-02, 'w_ffn2_down': 4.277603e-02, 'g_final': 6.576394e+01}


def _to_microbatches(a, axis):
    t = _jnp.moveaxis(a, axis, 0)
    t = t.reshape((N_MICROBATCH, t.shape[0] // N_MICROBATCH) + t.shape[1:])
    return _jnp.moveaxis(t, 1, axis + 1)


def setup_inputs(seed: int = 0) -> dict:
    inp = _fwd_setup_inputs(seed)
    key = _jax.random.fold_in(_jax.random.key(seed), 7919)
    shape, _ = _output_shape()
    out = dict(inp)
    out["loss_target"] = _jax.random.normal(_jax.random.fold_in(key, 0), shape, _jnp.float32)
    for i, name in enumerate(TWIN_WEIGHTS):
        w = inp[name].astype(_jnp.float32)
        if MOMENT_SCALE is None:
            s = _jnp.sqrt(_jnp.mean(_jnp.square(w)) + 1e-30)
        else:
            s = MOMENT_SCALE[name]
        km, kv = _jax.random.split(_jax.random.fold_in(key, i + 1))
        out[name] = w
        out["m_" + name] = s * _jax.random.normal(km, w.shape, _jnp.float32)
        out["v_" + name] = (s * s) * _jax.random.uniform(kv, w.shape, _jnp.float32, 0.5, 1.5)
    if N_MICROBATCH > 1:
        for name, axis in PER_EXAMPLE_BATCH_AXIS.items():
            out[name] = _to_microbatches(out[name], axis)
    return {'x': out['x'], 'c': out['c'], 'ctx': out['ctx'], 'c_ctx': out['c_ctx'], 'w_mod': out['w_mod'], 'b_mod': out['b_mod'], 'g_n1': out['g_n1'], 'w_ffn1_up': out['w_ffn1_up'], 'w_ffn1_down': out['w_ffn1_down'], 'g_n2': out['g_n2'], 'w_in': out['w_in'], 'b_in': out['b_in'], 'w_dw': out['w_dw'], 'b_dw': out['b_dw'], 'g_ln': out['g_ln'], 'b_ln': out['b_ln'], 'w_conf_out': out['w_conf_out'], 'w_lru_conv': out['w_lru_conv'], 'b_lru_conv': out['b_lru_conv'], 'w_rec_gate': out['w_rec_gate'], 'b_rec_gate': out['b_rec_gate'], 'w_in_gate': out['w_in_gate'], 'b_in_gate': out['b_in_gate'], 'lru_lambda': out['lru_lambda'], 'w_lru_out': out['w_lru_out'], 'w_out': out['w_out'], 'g_n3': out['g_n3'], 'w_ffn2_up': out['w_ffn2_up'], 'w_ffn2_down': out['w_ffn2_down'], 'g_final': out['g_final'], 'loss_target': out['loss_target'], 'm_c_ctx': out['m_c_ctx'], 'm_w_mod': out['m_w_mod'], 'm_b_mod': out['m_b_mod'], 'm_g_n1': out['m_g_n1'], 'm_w_ffn1_up': out['m_w_ffn1_up'], 'm_w_ffn1_down': out['m_w_ffn1_down'], 'm_g_n2': out['m_g_n2'], 'm_w_in': out['m_w_in'], 'm_b_in': out['m_b_in'], 'm_w_dw': out['m_w_dw'], 'm_b_dw': out['m_b_dw'], 'm_g_ln': out['m_g_ln'], 'm_b_ln': out['m_b_ln'], 'm_w_conf_out': out['m_w_conf_out'], 'm_w_lru_conv': out['m_w_lru_conv'], 'm_b_lru_conv': out['m_b_lru_conv'], 'm_w_rec_gate': out['m_w_rec_gate'], 'm_b_rec_gate': out['m_b_rec_gate'], 'm_w_in_gate': out['m_w_in_gate'], 'm_b_in_gate': out['m_b_in_gate'], 'm_lru_lambda': out['m_lru_lambda'], 'm_w_lru_out': out['m_w_lru_out'], 'm_w_out': out['m_w_out'], 'm_g_n3': out['m_g_n3'], 'm_w_ffn2_up': out['m_w_ffn2_up'], 'm_w_ffn2_down': out['m_w_ffn2_down'], 'm_g_final': out['m_g_final'], 'v_c_ctx': out['v_c_ctx'], 'v_w_mod': out['v_w_mod'], 'v_b_mod': out['v_b_mod'], 'v_g_n1': out['v_g_n1'], 'v_w_ffn1_up': out['v_w_ffn1_up'], 'v_w_ffn1_down': out['v_w_ffn1_down'], 'v_g_n2': out['v_g_n2'], 'v_w_in': out['v_w_in'], 'v_b_in': out['v_b_in'], 'v_w_dw': out['v_w_dw'], 'v_b_dw': out['v_b_dw'], 'v_g_ln': out['v_g_ln'], 'v_b_ln': out['v_b_ln'], 'v_w_conf_out': out['v_w_conf_out'], 'v_w_lru_conv': out['v_w_lru_conv'], 'v_b_lru_conv': out['v_b_lru_conv'], 'v_w_rec_gate': out['v_w_rec_gate'], 'v_b_rec_gate': out['v_b_rec_gate'], 'v_w_in_gate': out['v_w_in_gate'], 'v_b_in_gate': out['v_b_in_gate'], 'v_lru_lambda': out['v_lru_lambda'], 'v_w_lru_out': out['v_w_lru_out'], 'v_w_out': out['v_w_out'], 'v_g_n3': out['v_g_n3'], 'v_w_ffn2_up': out['v_w_ffn2_up'], 'v_w_ffn2_down': out['v_w_ffn2_down'], 'v_g_final': out['v_g_final']}


def _loss(weights, diff, rest, loss_target):
    with _jax.named_scope("forward"):
        args = {**rest, TWIN_DIFF_INPUT: diff, **{k: w.astype(_WEIGHT_DTYPES[k]) for k, w in weights.items()}}
        y = _forward(args)
    with _jax.named_scope("loss_head"):
        err = _jnp.square(y.astype(_jnp.float32) - loss_target)
        return 0.5 * _jnp.sum(_jnp.mean(err, axis=-1)) if err.ndim else 0.5 * err


def _adamw(w, g, m, v):
    m = ADAM_B1 * m + (1.0 - ADAM_B1) * g
    v = ADAM_B2 * v + (1.0 - ADAM_B2) * _jnp.square(g)
    m_hat = m / (1.0 - ADAM_B1 ** ADAM_STEP)
    v_hat = v / (1.0 - ADAM_B2 ** ADAM_STEP)
    delta = -ADAM_LR * (m_hat / (_jnp.sqrt(v_hat) + ADAM_EPS) + ADAM_WD * w)
    return delta, m, v


def reference(x, c, ctx, c_ctx, w_mod, b_mod, g_n1, w_ffn1_up, w_ffn1_down, g_n2, w_in, b_in, w_dw, b_dw, g_ln, b_ln, w_conf_out, w_lru_conv, b_lru_conv, w_rec_gate, b_rec_gate, w_in_gate, b_in_gate, lru_lambda, w_lru_out, w_out, g_n3, w_ffn2_up, w_ffn2_down, g_final, loss_target, m_c_ctx, m_w_mod, m_b_mod, m_g_n1, m_w_ffn1_up, m_w_ffn1_down, m_g_n2, m_w_in, m_b_in, m_w_dw, m_b_dw, m_g_ln, m_b_ln, m_w_conf_out, m_w_lru_conv, m_b_lru_conv, m_w_rec_gate, m_b_rec_gate, m_w_in_gate, m_b_in_gate, m_lru_lambda, m_w_lru_out, m_w_out, m_g_n3, m_w_ffn2_up, m_w_ffn2_down, m_g_final, v_c_ctx, v_w_mod, v_b_mod, v_g_n1, v_w_ffn1_up, v_w_ffn1_down, v_g_n2, v_w_in, v_b_in, v_w_dw, v_b_dw, v_g_ln, v_b_ln, v_w_conf_out, v_w_lru_conv, v_b_lru_conv, v_w_rec_gate, v_b_rec_gate, v_w_in_gate, v_b_in_gate, v_lru_lambda, v_w_lru_out, v_w_out, v_g_n3, v_w_ffn2_up, v_w_ffn2_down, v_g_final):
    given = dict(x=x, c=c, ctx=ctx, c_ctx=c_ctx, w_mod=w_mod, b_mod=b_mod, g_n1=g_n1, w_ffn1_up=w_ffn1_up, w_ffn1_down=w_ffn1_down, g_n2=g_n2, w_in=w_in, b_in=b_in, w_dw=w_dw, b_dw=b_dw, g_ln=g_ln, b_ln=b_ln, w_conf_out=w_conf_out, w_lru_conv=w_lru_conv, b_lru_conv=b_lru_conv, w_rec_gate=w_rec_gate, b_rec_gate=b_rec_gate, w_in_gate=w_in_gate, b_in_gate=b_in_gate, lru_lambda=lru_lambda, w_lru_out=w_lru_out, w_out=w_out, g_n3=g_n3, w_ffn2_up=w_ffn2_up, w_ffn2_down=w_ffn2_down, g_final=g_final, loss_target=loss_target, m_c_ctx=m_c_ctx, m_w_mod=m_w_mod, m_b_mod=m_b_mod, m_g_n1=m_g_n1, m_w_ffn1_up=m_w_ffn1_up, m_w_ffn1_down=m_w_ffn1_down, m_g_n2=m_g_n2, m_w_in=m_w_in, m_b_in=m_b_in, m_w_dw=m_w_dw, m_b_dw=m_b_dw, m_g_ln=m_g_ln, m_b_ln=m_b_ln, m_w_conf_out=m_w_conf_out, m_w_lru_conv=m_w_lru_conv, m_b_lru_conv=m_b_lru_conv, m_w_rec_gate=m_w_rec_gate, m_b_rec_gate=m_b_rec_gate, m_w_in_gate=m_w_in_gate, m_b_in_gate=m_b_in_gate, m_lru_lambda=m_lru_lambda, m_w_lru_out=m_w_lru_out, m_w_out=m_w_out, m_g_n3=m_g_n3, m_w_ffn2_up=m_w_ffn2_up, m_w_ffn2_down=m_w_ffn2_down, m_g_final=m_g_final, v_c_ctx=v_c_ctx, v_w_mod=v_w_mod, v_b_mod=v_b_mod, v_g_n1=v_g_n1, v_w_ffn1_up=v_w_ffn1_up, v_w_ffn1_down=v_w_ffn1_down, v_g_n2=v_g_n2, v_w_in=v_w_in, v_b_in=v_b_in, v_w_dw=v_w_dw, v_b_dw=v_b_dw, v_g_ln=v_g_ln, v_b_ln=v_b_ln, v_w_conf_out=v_w_conf_out, v_w_lru_conv=v_w_lru_conv, v_b_lru_conv=v_b_lru_conv, v_w_rec_gate=v_w_rec_gate, v_b_rec_gate=v_b_rec_gate, v_w_in_gate=v_w_in_gate, v_b_in_gate=v_b_in_gate, v_lru_lambda=v_lru_lambda, v_w_lru_out=v_w_lru_out, v_w_out=v_w_out, v_g_n3=v_g_n3, v_w_ffn2_up=v_w_ffn2_up, v_w_ffn2_down=v_w_ffn2_down, v_g_final=v_g_final)
    weights = {n: given[n] for n in TWIN_WEIGHTS}
    shared = {n: given[n] for n in SHARED_INPUTS}
    per_example = {n: given[n] for n in ['x', 'c', 'ctx']}
    grad_fn = _jax.value_and_grad(_loss, argnums=(0, 1))

    def one_microbatch(ex, loss_target):
        ex = dict(ex)
        diff = ex.pop(TWIN_DIFF_INPUT)
        return grad_fn(weights, diff, {**shared, **ex}, loss_target)

    if N_MICROBATCH == 1:
        loss, (grad_w, grad_x) = one_microbatch(per_example, given["loss_target"])
    else:
        def body(carry, xs):
            loss_sum, grad_sum = carry
            l_k, (gw_k, gx_k) = one_microbatch(xs[0], xs[1])
            with _jax.named_scope("update"):
                return (loss_sum + l_k, _jax.tree.map(_jnp.add, grad_sum, gw_k)), gx_k

        init = (_jnp.zeros((), _jnp.float32), _jax.tree.map(_jnp.zeros_like, weights))
        (loss, grad_w), grad_x = _jax.lax.scan(body, init, (per_example, given["loss_target"]))
    with _jax.named_scope("update"):
        delta_w, new_m, new_v = {}, {}, {}
        for n in TWIN_WEIGHTS:
            delta_w[n], new_m[n], new_v[n] = _adamw(weights[n], grad_w[n], given["m_" + n], given["v_" + n])
    return (loss, grad_x, *[grad_w[n] for n in TWIN_WEIGHTS], *[delta_w[n] for n in TWIN_WEIGHTS],
            *[new_m[n] for n in TWIN_WEIGHTS], *[new_v[n] for n in TWIN_WEIGHTS])
```

```python
import functools

import jax
import jax.numpy as jnp
from jax import lax
from jax.experimental import pallas as pl
from jax.experimental.pallas import tpu as pltpu

F32 = jnp.float32
BF16 = jnp.bfloat16
MESH = pl.DeviceIdType.MESH

EPS = 1e-6
GRID_W = 64
N_MOD = 9
LRU_C = 8.0
LRU_PAD_LEFT = 2
ADAM_LR, ADAM_B1, ADAM_B2, ADAM_EPS, ADAM_WD, ADAM_STEP = 0.001, 0.9, 0.999, 1e-08, 0.01, 10

LANES = 128
SUBLANES = 8
HALO = 16
VMEM_LIMIT = 56 * 1024 * 1024

W_NAMES = ['c_ctx', 'w_mod', 'b_mod', 'g_n1', 'w_ffn1_up', 'w_ffn1_down', 'g_n2', 'w_in', 'b_in', 'w_dw', 'b_dw',
           'g_ln', 'b_ln', 'w_conf_out', 'w_lru_conv', 'b_lru_conv', 'w_rec_gate', 'b_rec_gate', 'w_in_gate',
           'b_in_gate', 'lru_lambda', 'w_lru_out', 'w_out', 'g_n3', 'w_ffn2_up', 'w_ffn2_down', 'g_final']
COL_SHARDED = ['w_ffn1_up', 'w_in', 'w_ffn2_up']
ROW_SHARDED = ['w_ffn1_down', 'w_conf_out', 'w_lru_out', 'w_out', 'w_ffn2_down']
SMALL_SHARDED = ['w_dw', 'w_lru_conv', 'b_rec_gate', 'b_in_gate', 'lru_lambda']


def _params(*semantics):
    return pltpu.CompilerParams(dimension_semantics=semantics, vmem_limit_bytes=VMEM_LIMIT)


def _pick(n, target, mult=LANES):
    best = None
    for t in range(mult, min(n, target) + 1, mult):
        if n % t == 0:
            best = t
    return best or n


def _chunks(width, target=512):
    w = _pick(width, target)
    return [slice(s, s + w) for s in range(0, width, w)]


def _sigmoid(x):
    return jax.nn.sigmoid(x)


def _silu_and_grad(x):
    s = _sigmoid(x)
    return x * s, s * (1.0 + x * (1.0 - s))


_GELU_K = 0.7978845608028654


def _gelu_and_grad(x):
    t = jnp.tanh(_GELU_K * (x + 0.044715 * (x * x * x)))
    return 0.5 * x * (1.0 + t), 0.5 * (1.0 + t) + 0.5 * x * (1.0 - t * t) * _GELU_K * (1.0 + 3 * 0.044715 * x * x)


def _neg_expm1(z):
    series = -(z * (1.0 + z * (0.5 + z * (1.0 / 6.0 + z * (1.0 / 24.0)))))
    return jnp.where(z > -0.01, series, 1.0 - jnp.exp(z))


def _softplus(x):
    return jnp.maximum(x, 0.0) + jnp.log1p(jnp.exp(-jnp.abs(x)))


def _accumulate(ref, value, first):
    @pl.when(first)
    def _():
        ref[...] = value

    @pl.when(jnp.logical_not(first))
    def _():
        ref[...] += value


def _colsum(x):
    return jnp.sum(x, axis=0, keepdims=True)


class Rows:
    def __init__(self, t_lat, t_ctx, d_model):
        self.tm = _pick(t_ctx, 256, HALO)
        assert t_lat % self.tm == 0 and t_ctx % self.tm == 0
        self.n_lat = t_lat // self.tm
        self.n_all = (t_lat + t_ctx) // self.tm
        self.t_all = t_lat + t_ctx
        self.d = d_model

    def seg(self, i):
        return jnp.where(i >= self.n_lat, 1, 0)

    def seg_first(self, i):
        return jnp.logical_or(i == 0, i == self.n_lat)

    def seg_last(self, i):
        return jnp.logical_or(i == self.n_lat - 1, i == self.n_all - 1)

    def row(self, width, col=0):
        return pl.BlockSpec((self.tm, width), lambda i: (i, col))

    def vec(self, width):
        return pl.BlockSpec((1, width), lambda i: (0, 0))

    def full(self, shape):
        return pl.BlockSpec(shape, lambda i: (0,) * len(shape))

    def mod(self):
        return pl.BlockSpec((None, N_MOD, self.d), lambda i: (self.seg(i), 0, 0))

    def segsum(self):
        return pl.BlockSpec((None, 1, self.d), lambda i: (self.seg(i), 0, 0))

    def segsum_shape(self):
        return jax.ShapeDtypeStruct((2, 1, self.d), F32)

    def halo(self, width, side):
        per = self.tm // HALO
        last = self.t_all // HALO - 1
        if side < 0:
            return pl.BlockSpec((HALO, width), lambda i: (jnp.maximum(i * per - 1, 0), 0))
        return pl.BlockSpec((HALO, width), lambda i: (jnp.minimum((i + 1) * per, last), 0))

    def call(self, body, name, in_specs, out_specs, out_shape, scratch=()):
        return pl.pallas_call(body, name=name, grid=(self.n_all,), in_specs=in_specs, out_specs=out_specs,
                              out_shape=out_shape, scratch_shapes=list(scratch), compiler_params=_params("arbitrary"))


_PEER_FLIPS = {
    "chips": [(1, 0, 0), (0, 1, 0), (1, 1, 0)],
    "sibling": [(0, 0, 1)],
    "all": [(0, 0, 1), (0, 1, 0), (0, 1, 1), (1, 0, 0), (1, 0, 1), (1, 1, 0), (1, 1, 1)],
}


def _slot(kind, x, y, c):
    return {"chips": 2 * x + y, "sibling": c, "all": 4 * x + 2 * y + c}[kind]


def exchange(kind, srcs, indexed, name):
    flips = _PEER_FLIPS[kind]
    n_slots = len(flips) + 1
    n_t = len(srcs)
    out_shapes = [jax.ShapeDtypeStruct(s.shape if indexed else (n_slots,) + s.shape, s.dtype) for s in srcs]

    def body(*refs):
        src_refs, dst_refs = refs[:n_t], refs[n_t:2 * n_t]
        send_sems, recv_sems, local_sems = refs[2 * n_t:]
        x, y, c = lax.axis_index("x"), lax.axis_index("y"), lax.axis_index("c")
        me = _slot(kind, x, y, c)

        def piece(t, k):
            return src_refs[t].at[k] if indexed else src_refs[t]

        peers = [(1 - x if fx else x, 1 - y if fy else y, 1 - c if fc else c) for fx, fy, fc in flips]
        local = [pltpu.make_async_copy(piece(t, me), dst_refs[t].at[me], local_sems.at[t]) for t in range(n_t)]
        for cp in local:
            cp.start()
        sends = []
        for t in range(n_t):
            for p, peer in enumerate(peers):
                sends.append(pltpu.make_async_remote_copy(
                    src_ref=piece(t, _slot(kind, *peer)), dst_ref=dst_refs[t].at[me],
                    send_sem=send_sems.at[t, p], recv_sem=recv_sems.at[t, p], device_id=peer, device_id_type=MESH))
        for cp in sends:
            cp.start()
        for t in range(n_t):
            for p, peer in enumerate(peers):
                theirs = _slot(kind, *peer)
                pltpu.make_async_remote_copy(
                    src_ref=piece(t, me), dst_ref=dst_refs[t].at[theirs],
                    send_sem=send_sems.at[t, p], recv_sem=recv_sems.at[t, p], device_id=peer, device_id_type=MESH).wait_recv()
        for cp in sends:
            cp.wait_send()
        for cp in local:
            cp.wait()

    any_spec = pl.BlockSpec(memory_space=pl.ANY)
    outs = pl.pallas_call(
        body, name=name, out_shape=out_shapes, in_specs=[any_spec] * n_t, out_specs=[any_spec] * n_t,
        scratch_shapes=[pltpu.SemaphoreType.DMA((n_t, len(flips))), pltpu.SemaphoreType.DMA((n_t, len(flips))),
                        pltpu.SemaphoreType.DMA((n_t,))],
        compiler_params=pltpu.CompilerParams(has_side_effects=True),
    )(*srcs)
    return list(outs)


def matmul(a, b, mode, out_dtype, name, bias=None):
    if mode == "nn":
        (n_i, n_r), (_, n_j) = a.shape, b.shape
    elif mode == "nt":
        (n_i, n_r), (n_j, _) = a.shape, b.shape
    else:
        (n_r, n_i), (_, n_j) = a.shape, b.shape
    ti = _pick(n_i, 1408 if mode == "tn" else 768)
    tj = _pick(n_j, 1664)
    tr = n_r if (mode == "nn" and n_r <= 2816) else _pick(n_r, 768 if mode == "tn" else 1664)
    steps = n_r // tr
    a_spec = {"nn": pl.BlockSpec((ti, tr), lambda i, j, r: (i, r)), "nt": pl.BlockSpec((ti, tr), lambda i, j, r: (i, r)),
              "tn": pl.BlockSpec((tr, ti), lambda i, j, r: (r, i))}[mode]
    b_spec = {"nn": pl.BlockSpec((tr, tj), lambda i, j, r: (r, j)), "nt": pl.BlockSpec((tj, tr), lambda i, j, r: (j, r)),
              "tn": pl.BlockSpec((tr, tj), lambda i, j, r: (r, j))}[mode]
    dims = {"nn": (((1,), (0,)), ((), ())), "nt": (((1,), (1,)), ((), ())), "tn": (((0,), (0,)), ((), ()))}[mode]

    def body(*refs):
        a_ref, b_ref = refs[:2]
        bias_ref = refs[2] if bias is not None else None
        o_ref = refs[3] if bias is not None else refs[2]
        prod = lax.dot_general(a_ref[...].astype(BF16), b_ref[...].astype(BF16), dims, preferred_element_type=F32)

        def finish(total):
            if bias_ref is not None:
                total = total + bias_ref[...]
            o_ref[...] = total.astype(out_dtype)

        if steps == 1:
            finish(prod)
        else:
            acc = refs[-1]
            r = pl.program_id(2)

            @pl.when(r == 0)
            def _():
                acc[...] = prod

            @pl.when(r > 0)
            def _():
                acc[...] += prod

            @pl.when(r == steps - 1)
            def _():
                finish(acc[...])

    in_specs = [a_spec, b_spec] + ([pl.BlockSpec((1, tj), lambda i, j, r: (0, j))] if bias is not None else [])
    args = (a, b) + ((bias,) if bias is not None else ())
    return pl.pallas_call(
        body, name=name, grid=(n_i // ti, n_j // tj, steps), in_specs=in_specs,
        out_specs=pl.BlockSpec((ti, tj), lambda i, j, r: (i, j)), out_shape=jax.ShapeDtypeStruct((n_i, n_j), out_dtype),
        scratch_shapes=[pltpu.VMEM((ti, tj), F32)] if steps > 1 else [],
        compiler_params=_params("arbitrary", "arbitrary", "arbitrary"),
    )(*args)


def norm_mod(rows, x, g, mv, shift_k, scale_k, name, resid=None, pos=None):
    d = rows.d
    changed = resid is not None or pos is not None

    def body(*refs):
        refs = list(refs)
        x_ref, g_ref, mv_ref = refs[:3]
        rest = refs[3:]
        xv = x_ref[...]
        if pos is not None:
            xv = xv + rest.pop(0)[...]
        if resid is not None:
            xv = xv + resid[2] * mv_ref[resid[1]:resid[1] + 1, :] * rest.pop(0)[...]
        if changed:
            rest.pop(0)[...] = xv
        r = lax.rsqrt(jnp.mean(xv * xv, axis=-1, keepdims=True) + EPS)
        h = (xv * r) * g_ref[...]
        if shift_k is not None:
            h = h * (1.0 + mv_ref[scale_k:scale_k + 1, :]) + mv_ref[shift_k:shift_k + 1, :]
        rest.pop(0)[...] = h.astype(BF16)

    ins = [x, g, mv] + ([pos] if pos is not None else []) + ([resid[0]] if resid is not None else [])
    in_specs = [rows.row(d), rows.vec(d), rows.mod()] + [rows.row(d)] * (len(ins) - 3)
    out_shape = ([jax.ShapeDtypeStruct((rows.t_all, d), F32)] if changed else []) + [jax.ShapeDtypeStruct((rows.t_all, d), BF16)]
    outs = rows.call(body, name, in_specs, [rows.row(d)] * len(out_shape), out_shape)(*ins)
    return (outs[0], outs[1]) if changed else (None, outs[0])


def norm_mod_bwd(rows, x, dh, dxn, g, mv, shift_k, scale_k, name):
    d = rows.d

    def body(x_ref, dh_ref, dxn_ref, g_ref, mv_ref, dx_ref, dg_ref, dsh_ref, dsc_ref):
        i = pl.program_id(0)
        xv, dhv, gv = x_ref[...], dh_ref[...], g_ref[...]
        r = lax.rsqrt(jnp.mean(xv * xv, axis=-1, keepdims=True) + EPS)
        n = xv * r
        dy = dhv * (1.0 + mv_ref[scale_k:scale_k + 1, :])
        dn = dy * gv
        dx_ref[...] = dxn_ref[...] + r * (dn - n * jnp.mean(dn * n, axis=-1, keepdims=True))
        _accumulate(dg_ref, _colsum(dy * n), i == 0)
        _accumulate(dsh_ref, _colsum(dhv), rows.seg_first(i))
        _accumulate(dsc_ref, _colsum(dhv * (n * gv)), rows.seg_first(i))

    return rows.call(
        body, name, [rows.row(d), rows.row(d), rows.row(d), rows.vec(d), rows.mod()],
        [rows.row(d), rows.vec(d), rows.segsum(), rows.segsum()],
        [jax.ShapeDtypeStruct((rows.t_all, d), F32), jax.ShapeDtypeStruct((1, d), F32), rows.segsum_shape(), rows.segsum_shape()],
    )(x, dh, dxn, g, mv)


def gate_bwd(rows, dxn, f, mv, gate_k, scale, name):
    d = rows.d

    def body(dxn_ref, f_ref, mv_ref, df_ref, dgate_ref):
        i = pl.program_id(0)
        dv = dxn_ref[...]
        df_ref[...] = (scale * mv_ref[gate_k:gate_k + 1, :] * dv).astype(BF16)
        _accumulate(dgate_ref, scale * _colsum(dv * f_ref[...]), rows.seg_first(i))

    return rows.call(body, name, [rows.row(d), rows.row(d), rows.mod()], [rows.row(d), rows.segsum()],
                     [jax.ShapeDtypeStruct((rows.t_all, d), BF16), rows.segsum_shape()])(dxn, f, mv)


def swiglu(rows, gu, name):
    f = gu.shape[1] // 2

    def body(g_ref, u_ref, o_ref):
        for cs in _chunks(f):
            o_ref[:, cs] = (_silu_and_grad(g_ref[:, cs])[0] * u_ref[:, cs]).astype(BF16)

    return rows.call(body, name, [rows.row(f, 0), rows.row(f, 1)], rows.row(f), jax.ShapeDtypeStruct((rows.t_all, f), BF16))(gu, gu)


def swiglu_bwd(rows, gu, dact, name):
    f = gu.shape[1] // 2

    def body(g_ref, u_ref, da_ref, o_ref):
        for cs in _chunks(f):
            act, dact_dg = _silu_and_grad(g_ref[:, cs])
            dav = da_ref[:, cs]
            o_ref[:, cs] = (dav * u_ref[:, cs] * dact_dg).astype(BF16)
            o_ref[:, slice(f + cs.start, f + cs.stop)] = (dav * act).astype(BF16)

    return rows.call(body, name, [rows.row(f, 0), rows.row(f, 1), rows.row(f)], rows.row(2 * f),
                     jax.ShapeDtypeStruct((rows.t_all, 2 * f), BF16))(gu, gu, dact)


def glu(rows, pvg, name):
    w = pvg.shape[1] // 2

    def body(v_ref, t_ref, o_ref):
        for cs in _chunks(w):
            o_ref[:, cs] = v_ref[:, cs] * _sigmoid(t_ref[:, cs])

    return rows.call(body, name, [rows.row(w, 0), rows.row(w, 1)], rows.row(w), jax.ShapeDtypeStruct((rows.t_all, w), F32))(pvg, pvg)


def glu_bwd(rows, pvg, du, name):
    w = pvg.shape[1] // 2

    def body(v_ref, t_ref, du_ref, o_ref):
        for cs in _chunks(w):
            s = _sigmoid(t_ref[:, cs])
            duv = du_ref[:, cs]
            o_ref[:, cs] = (duv * s).astype(BF16)
            o_ref[:, slice(w + cs.start, w + cs.stop)] = (duv * v_ref[:, cs] * s * (1.0 - s)).astype(BF16)

    return rows.call(body, name, [rows.row(w, 0), rows.row(w, 1), rows.row(w)], rows.row(2 * w),
                     jax.ShapeDtypeStruct((rows.t_all, 2 * w), BF16))(pvg, pvg, du)


def ln_silu(rows, u, g, b, name):
    w = u.shape[1]

    def body(u_ref, g_ref, b_ref, o_ref):
        uv = u_ref[...]
        xc = uv - jnp.mean(uv, axis=-1, keepdims=True)
        n = xc * lax.rsqrt(jnp.mean(xc * xc, axis=-1, keepdims=True) + EPS)
        o_ref[...] = _silu_and_grad(n * g_ref[...] + b_ref[...])[0].astype(BF16)

    return rows.call(body, name, [rows.row(w), rows.vec(w), rows.vec(w)], rows.row(w),
                     jax.ShapeDtypeStruct((rows.t_all, w), BF16))(u, g, b)


def ln_silu_bwd(rows, u, ds, g, b, name):
    w = u.shape[1]

    def body(u_ref, ds_ref, g_ref, b_ref, du_ref, dg_ref, db_ref):
        i = pl.program_id(0)
        uv, gv = u_ref[...], g_ref[...]
        xc = uv - jnp.mean(uv, axis=-1, keepdims=True)
        r = lax.rsqrt(jnp.mean(xc * xc, axis=-1, keepdims=True) + EPS)
        n = xc * r
        dy = ds_ref[...] * _silu_and_grad(n * gv + b_ref[...])[1]
        dn = dy * gv
        du_ref[...] = r * (dn - jnp.mean(dn, axis=-1, keepdims=True) - n * jnp.mean(dn * n, axis=-1, keepdims=True))
        _accumulate(dg_ref, _colsum(dy * n), i == 0)
        _accumulate(db_ref, _colsum(dy), i == 0)

    return rows.call(body, name, [rows.row(w), rows.row(w), rows.vec(w), rows.vec(w)], [rows.row(w), rows.vec(w), rows.vec(w)],
                     [jax.ShapeDtypeStruct((rows.t_all, w), F32), jax.ShapeDtypeStruct((1, w), F32),
                      jax.ShapeDtypeStruct((1, w), F32)])(u, ds, g, b)


def lru_merge(rows, h, gl, name):
    w = gl.shape[1]

    def body(h_ref, gl_ref, o_ref):
        for cs in _chunks(w):
            o_ref[:, cs] = ((h_ref[0, :, cs] + h_ref[1, :, cs]) * _gelu_and_grad(gl_ref[:, cs])[0]).astype(BF16)

    return rows.call(body, name, [pl.BlockSpec((2, rows.tm, w), lambda i: (0, i, 0)), rows.row(w)], rows.row(w),
                     jax.ShapeDtypeStruct((rows.t_all, w), BF16))(h, gl)


def lru_merge_bwd(rows, h, gl, dy, name):
    w = gl.shape[1]

    def body(h_ref, gl_ref, dy_ref, dh_ref, dgl_ref):
        for cs in _chunks(w):
            act, dact = _gelu_and_grad(gl_ref[:, cs])
            dyv = dy_ref[:, cs]
            dh_ref[:, cs] = dyv * act
            dgl_ref[:, cs] = (dyv * (h_ref[0, :, cs] + h_ref[1, :, cs]) * dact).astype(BF16)

    return rows.call(body, name, [pl.BlockSpec((2, rows.tm, w), lambda i: (0, i, 0)), rows.row(w), rows.row(w)],
                     [rows.row(w), rows.row(w)],
                     [jax.ShapeDtypeStruct((rows.t_all, w), F32), jax.ShapeDtypeStruct((rows.t_all, w), BF16)])(h, gl, dy)


def z_merge(rows, pgc, yc, yl, name):
    d = rows.d

    def body(gc_ref, gr_ref, yc_ref, yl_ref, o_ref):
        for cs in _chunks(d):
            o_ref[:, cs] = (_sigmoid(gc_ref[:, cs]) * yc_ref[:, cs] + _sigmoid(gr_ref[:, cs]) * yl_ref[:, cs]).astype(BF16)

    return rows.call(body, name, [rows.row(d, 0), rows.row(d, 1), rows.row(d), rows.row(d)], rows.row(d),
                     jax.ShapeDtypeStruct((rows.t_all, d), BF16))(pgc, pgc, yc, yl)


def z_merge_bwd(rows, pgc, yc, yl, dz, name):
    d = rows.d

    def body(gc_ref, gr_ref, yc_ref, yl_ref, dz_ref, dyc_ref, dyl_ref, dp_ref):
        for cs in _chunks(d):
            sc, sr, dzv = _sigmoid(gc_ref[:, cs]), _sigmoid(gr_ref[:, cs]), dz_ref[:, cs]
            dyc_ref[:, cs] = (dzv * sc).astype(BF16)
            dyl_ref[:, cs] = (dzv * sr).astype(BF16)
            dp_ref[:, cs] = (dzv * yc_ref[:, cs] * sc * (1.0 - sc)).astype(BF16)
            dp_ref[:, slice(d + cs.start, d + cs.stop)] = (dzv * yl_ref[:, cs] * sr * (1.0 - sr)).astype(BF16)

    return rows.call(body, name, [rows.row(d, 0), rows.row(d, 1), rows.row(d), rows.row(d), rows.row(d)],
                     [rows.row(d), rows.row(d), rows.row(2 * d)],
                     [jax.ShapeDtypeStruct((rows.t_all, d), BF16), jax.ShapeDtypeStruct((rows.t_all, d), BF16),
                      jax.ShapeDtypeStruct((rows.t_all, 2 * d), BF16)])(pgc, pgc, yc, yl, dz)


def col_sum(rows, x, name):
    w = x.shape[1]

    def body(x_ref, o_ref):
        i = pl.program_id(0)
        for cs in _chunks(w):
            _accumulate(o_ref.at[:, cs], _colsum(x_ref[:, cs].astype(F32)), i == 0)

    return rows.call(body, name, [rows.row(w)], rows.vec(w), jax.ShapeDtypeStruct((1, w), F32))(x)


def loss_head(rows, x, f, mv, g, target, name):
    d = rows.d

    def body(x_ref, f_ref, mv_ref, g_ref, t_ref, loss_ref, dx_ref, dg_ref):
        i = pl.program_id(0)
        valid = jnp.where(i < rows.n_lat, 1.0, 0.0)
        xv = x_ref[...] + 0.5 * mv_ref[8:9, :] * f_ref[...]
        gv = g_ref[...]
        r = lax.rsqrt(jnp.mean(xv * xv, axis=-1, keepdims=True) + EPS)
        n = xv * r
        err = (n * gv - t_ref[...]) * valid
        part = 0.5 * jnp.sum(jnp.mean(err * err, axis=-1, keepdims=True), axis=0, keepdims=True)
        _accumulate(loss_ref, jnp.broadcast_to(part, (1, LANES)), i == 0)
        dy = err * (1.0 / d)
        dn = dy * gv
        dx_ref[...] = r * (dn - n * jnp.mean(dn * n, axis=-1, keepdims=True))
        _accumulate(dg_ref, _colsum(dy * n), i == 0)

    target_spec = pl.BlockSpec((rows.tm, d), lambda i: (jnp.minimum(i, rows.n_lat - 1), 0))
    return rows.call(body, name, [rows.row(d), rows.row(d), rows.mod(), rows.vec(d), target_spec],
                     [rows.vec(LANES), rows.row(d), rows.vec(d)],
                     [jax.ShapeDtypeStruct((1, LANES), F32), jax.ShapeDtypeStruct((rows.t_all, d), F32),
                      jax.ShapeDtypeStruct((1, d), F32)])(x, f, mv, g, target)


def _extended(rows, i, prev_ref, cur_ref, next_ref, cs):
    prev = jnp.where(rows.seg_first(i), 0.0, prev_ref[:, cs].astype(F32))
    nxt = jnp.where(rows.seg_last(i), 0.0, next_ref[:, cs].astype(F32))
    return jnp.concatenate([prev, cur_ref[:, cs].astype(F32), nxt], axis=0)


def _shifted(ext, offset, tm):
    n = ext.shape[0]
    return pltpu.roll(ext, (-offset) % n, 0)[HALO:HALO + tm] if offset else ext[HALO:HALO + tm]


def _pad_taps(w):
    k = w.shape[0]
    return jnp.pad(w, ((0, -k % SUBLANES), (0, 0)))


def dwconv(rows, u, w, b, pad_left, out_dtype, name):
    taps, width = w.shape
    wp = _pad_taps(w)

    def body(*refs):
        prev_ref, cur_ref, next_ref, w_ref = refs[:4]
        b_ref = refs[4] if b is not None else None
        o_ref = refs[-1]
        i = pl.program_id(0)
        for cs in _chunks(width, LANES):
            ext = _extended(rows, i, prev_ref, cur_ref, next_ref, cs)
            acc = jnp.zeros((rows.tm, LANES), F32) if b_ref is None else jnp.broadcast_to(b_ref[:, cs], (rows.tm, LANES))
            for k in range(taps):
                acc = acc + w_ref[k:k + 1, cs] * _shifted(ext, k - pad_left, rows.tm)
            o_ref[:, cs] = acc.astype(out_dtype)

    ins = [u, u, u, wp] + ([b] if b is not None else [])
    in_specs = [rows.halo(width, -1), rows.row(width), rows.halo(width, 1), rows.full(wp.shape)] + ([rows.vec(width)] if b is not None else [])
    return rows.call(body, name, in_specs, rows.row(width), jax.ShapeDtypeStruct((rows.t_all, width), out_dtype))(*ins)


def dwconv_bwd_w(rows, u, dy, taps, pad_left, name):
    width = u.shape[1]
    taps_p = taps + (-taps % SUBLANES)

    def body(prev_ref, cur_ref, next_ref, dy_ref, dw_ref, db_ref):
        i = pl.program_id(0)
        tap_row = lax.broadcasted_iota(jnp.int32, (taps_p, LANES), 0)
        for cs in _chunks(width, LANES):
            ext = _extended(rows, i, prev_ref, cur_ref, next_ref, cs)
            dyv = dy_ref[:, cs]
            total = jnp.zeros((taps_p, LANES), F32)
            for k in range(taps):
                total = total + jnp.where(tap_row == k, _colsum(dyv * _shifted(ext, k - pad_left, rows.tm)), 0.0)
            _accumulate(dw_ref.at[:, cs], total, i == 0)
            _accumulate(db_ref.at[:, cs], _colsum(dyv), i == 0)

    dw, db = rows.call(body, name, [rows.halo(width, -1), rows.row(width), rows.halo(width, 1), rows.row(width)],
                       [rows.full((taps_p, width)), rows.vec(width)],
                       [jax.ShapeDtypeStruct((taps_p, width), F32), jax.ShapeDtypeStruct((1, width), F32)])(u, u, u, dy)
    return dw[:taps], db


def _gate_values(xr, pre, lam, d):
    r = _sigmoid(pre[:, (2 * d) * LANES:(2 * d + 1) * LANES])
    ig = _sigmoid(pre[:, (2 * d + 1) * LANES:(2 * d + 2) * LANES])
    sp = _softplus(-lam[:, d * LANES:(d + 1) * LANES])
    log_a = -LRU_C * r * sp
    return r, ig, sp, jnp.exp(log_a), jnp.sqrt(_neg_expm1(2.0 * log_a))


def lru_gates(rows, xr, wcat, bcat, lam, name):
    n_blk = wcat.shape[0]
    width = xr.shape[1]

    def body(xr_ref, w_ref, b_ref, lam_ref, a_ref, bb_ref):
        for h in range(n_blk):
            cs = slice(h * LANES, (h + 1) * LANES)
            xv = xr_ref[:, cs]
            pre = jnp.dot(xv.astype(BF16), w_ref[h], preferred_element_type=F32) + b_ref[h]
            for d in range(2):
                _, ig, _, a, q = _gate_values(xv, pre, lam_ref[h], d)
                a_ref[d, :, cs] = a
                bb_ref[d, :, cs] = q * (ig * xv)

    both = pl.BlockSpec((2, rows.tm, width), lambda i: (0, i, 0))
    shape = jax.ShapeDtypeStruct((2, rows.t_all, width), F32)
    return rows.call(body, name, [rows.row(width), rows.full(wcat.shape), rows.full(bcat.shape), rows.full(lam.shape)],
                     [both, both], [shape, shape])(xr, wcat, bcat, lam)


def lru_gates_bwd(rows, xr, wcat, bcat, lam, da, dbb, name):
    n_blk = wcat.shape[0]
    width = xr.shape[1]

    def body(xr_ref, w_ref, b_ref, lam_ref, da_ref, dbb_ref, dxr_ref, dw_ref, db_ref, dlam_ref):
        i = pl.program_id(0)
        for h in range(n_blk):
            cs = slice(h * LANES, (h + 1) * LANES)
            xv = xr_ref[:, cs]
            xb = xv.astype(BF16)
            wv = w_ref[h]
            pre = jnp.dot(xb, wv, preferred_element_type=F32) + b_ref[h]
            dxr = jnp.zeros_like(xv)
            dpre, dlam = [], []
            for d in range(2):
                r, ig, sp, a, q = _gate_values(xv, pre, lam_ref[h], d)
                dav, dbv = da_ref[d, :, cs], dbb_ref[d, :, cs]
                dlog_a = dav * a - dbv * (ig * xv) * (a * a) / q
                dpre.append(dlog_a * (-LRU_C * sp) * r * (1.0 - r))
                dpre.append(dbv * q * xv * ig * (1.0 - ig))
                dxr = dxr + dbv * q * ig
                dlam.append(_colsum(dlog_a * (-LRU_C * r)) * (-_sigmoid(-lam_ref[h][:, d * LANES:(d + 1) * LANES])))
            dpre = jnp.concatenate(dpre, axis=1)
            dpb = dpre.astype(BF16)
            dxr_ref[:, cs] = dxr + lax.dot_general(dpb, wv, (((1,), (1,)), ((), ())), preferred_element_type=F32)
            _accumulate(dw_ref.at[h], lax.dot_general(xb, dpb, (((0,), (0,)), ((), ())), preferred_element_type=F32), i == 0)
            _accumulate(db_ref.at[h], _colsum(dpre), i == 0)
            _accumulate(dlam_ref.at[h], jnp.concatenate(dlam, axis=1), i == 0)

    both = pl.BlockSpec((2, rows.tm, width), lambda i: (0, i, 0))
    return rows.call(
        body, name, [rows.row(width), rows.full(wcat.shape), rows.full(bcat.shape), rows.full(lam.shape), both, both],
        [rows.row(width), rows.full(wcat.shape), rows.full(bcat.shape), rows.full(lam.shape)],
        [jax.ShapeDtypeStruct((rows.t_all, width), F32), jax.ShapeDtypeStruct(wcat.shape, F32),
         jax.ShapeDtypeStruct(bcat.shape, F32), jax.ShapeDtypeStruct(lam.shape, F32)])(xr, wcat, bcat, lam, da, dbb)


def _tile_scan(a, b, reverse):
    n = a.shape[0]
    row = lax.broadcasted_iota(jnp.int32, a.shape, 0)
    k = 1
    while k < n:
        ok = (row < n - k) if reverse else (row >= k)
        shift = n - k if reverse else k
        b = b + a * jnp.where(ok, pltpu.roll(b, shift, 0), 0.0)
        a = a * jnp.where(ok, pltpu.roll(a, shift, 0), 1.0)
        k *= 2
    return a, b


def _neighbour(v, edge, reverse):
    n = v.shape[0]
    row = lax.broadcasted_iota(jnp.int32, v.shape, 0)
    if reverse:
        return jnp.where(row < n - 1, pltpu.roll(v, n - 1, 0), edge)
    return jnp.where(row >= 1, pltpu.roll(v, 1, 0), edge)


def _scan_call(rows, body, name, ins, in_specs, n_out, width, adjoint):
    n_all, n_lat = rows.n_all, rows.n_lat

    def tile(d, s):
        s = n_all - 1 - s if adjoint else s
        return jnp.where(d == 0, (s + n_lat) % n_all, n_all - 1 - s)

    def per_dir(d, s):
        return (d, tile(d, s), 0)

    specs = [pl.BlockSpec((None, rows.tm, width), per_dir) if kind == "dir" else
             pl.BlockSpec((rows.tm, width), lambda d, s: (tile(d, s), 0)) for kind in in_specs]
    shape = jax.ShapeDtypeStruct((2, rows.t_all, width), F32)
    return pl.pallas_call(
        body, name=name, grid=(2, n_all), in_specs=specs, out_specs=[pl.BlockSpec((None, rows.tm, width), per_dir)] * n_out,
        out_shape=[shape] * n_out, scratch_shapes=[pltpu.VMEM((SUBLANES, width), F32)],
        compiler_params=_params("arbitrary", "arbitrary"))(*ins)


def lru_scan(rows, a, bb, name):
    width = a.shape[2]
    n = rows.tm

    def body(a_ref, bb_ref, h_ref, hp_ref, carry):
        d, s = pl.program_id(0), pl.program_id(1)

        @pl.when(s == 0)
        def _():
            carry[...] = jnp.zeros_like(carry)

        def run(reverse):
            for cs in _chunks(width, LANES):
                cum, h0 = _tile_scan(a_ref[:, cs], bb_ref[:, cs], reverse)
                h_in = carry[0:1, cs]
                h = h0 + cum * h_in
                h_ref[:, cs] = h
                hp_ref[:, cs] = _neighbour(h, h_in, reverse)
                carry[0:1, cs] = h[0:1] if reverse else h[n - 1:n]

        pl.when(d == 0)(lambda: run(False))
        pl.when(d == 1)(lambda: run(True))

    return _scan_call(rows, body, name, [a, bb], ["dir", "dir"], 2, width, adjoint=False)


def lru_scan_bwd(rows, dh, a, hp, name):
    width = a.shape[2]
    n = rows.tm

    def body(dh_ref, a_ref, hp_ref, da_ref, dbb_ref, carry):
        d, s = pl.program_id(0), pl.program_id(1)

        @pl.when(s == 0)
        def _():
            carry[...] = jnp.zeros_like(carry)

        def run(reverse):
            for cs in _chunks(width, LANES):
                av = a_ref[:, cs]
                cum, g0 = _tile_scan(_neighbour(av, 1.0, reverse), dh_ref[:, cs], reverse)
                g = g0 + cum * carry[0:1, cs]
                da_ref[:, cs] = g * hp_ref[:, cs]
                dbb_ref[:, cs] = g
                carry[0:1, cs] = (av * g)[0:1] if reverse else (av * g)[n - 1:n]

        pl.when(d == 0)(lambda: run(True))
        pl.when(d == 1)(lambda: run(False))

    return _scan_call(rows, body, name, [dh, a, hp], ["shared", "dir", "dir"], 2, width, adjoint=True)


def mod_matmul(c9, w_shard, b_shard, name):
    n = w_shard.shape[1]
    tn = _pick(n, 768)

    def body(c_ref, w_ref, b_ref, o_ref):
        act = _silu_and_grad(c_ref[...])[0]
        o_ref[...] = jnp.dot(act, w_ref[...], preferred_element_type=F32, precision=lax.Precision.HIGHEST) + b_ref[...]

    return pl.pallas_call(
        body, name=name, grid=(n // tn,),
        in_specs=[pl.BlockSpec(c9.shape, lambda j: (0, 0)), pl.BlockSpec((w_shard.shape[0], tn), lambda j: (0, j)),
                  pl.BlockSpec((1, tn), lambda j: (0, j))],
        out_specs=pl.BlockSpec((c9.shape[0], tn), lambda j: (0, j)), out_shape=jax.ShapeDtypeStruct((c9.shape[0], n), F32),
        compiler_params=_params("arbitrary"))(c9, w_shard, b_shard)


def mod_matmul_bwd(c9, d9, w_shard, name):
    n = w_shard.shape[1]
    tn = _pick(n, 768)
    steps = n // tn

    def body(c_ref, d_ref, w_ref, gw_ref, gc_ref):
        j = pl.program_id(0)
        act, dact = _silu_and_grad(c_ref[...])
        dv = d_ref[...]
        gw_ref[...] = lax.dot_general(act, dv, (((0,), (0,)), ((), ())), preferred_element_type=F32, precision=lax.Precision.HIGHEST)
        part = lax.dot_general(dv, w_ref[...], (((1,), (1,)), ((), ())), preferred_element_type=F32, precision=lax.Precision.HIGHEST)
        _accumulate(gc_ref, part * dact, j == 0)

    return pl.pallas_call(
        body, name=name, grid=(steps,),
        in_specs=[pl.BlockSpec(c9.shape, lambda j: (0, 0)), pl.BlockSpec((c9.shape[0], tn), lambda j: (0, j)),
                  pl.BlockSpec((w_shard.shape[0], tn), lambda j: (0, j))],
        out_specs=[pl.BlockSpec((w_shard.shape[0], tn), lambda j: (0, j)), pl.BlockSpec(c9.shape, lambda j: (0, 0))],
        out_shape=[jax.ShapeDtypeStruct(w_shard.shape, F32), jax.ShapeDtypeStruct(c9.shape, F32)],
        compiler_params=_params("arbitrary"))(c9, d9, w_shard)


def _row_tile(n_rows, n_cols):
    return _pick(n_rows, max(SUBLANES, (256 * 1024) // n_cols), SUBLANES)


def sum_parts(parts, name):
    n, n_rows, n_cols = parts.shape
    tr = _row_tile(n_rows, n_cols)

    def body(p_ref, o_ref):
        total = p_ref[0]
        for k in range(1, n):
            total = total + p_ref[k]
        o_ref[...] = total

    return pl.pallas_call(
        body, name=name, grid=(n_rows // tr,), in_specs=[pl.BlockSpec((n, tr, n_cols), lambda i: (0, i, 0))],
        out_specs=pl.BlockSpec((tr, n_cols), lambda i: (i, 0)), out_shape=jax.ShapeDtypeStruct((n_rows, n_cols), F32),
        compiler_params=_params("arbitrary"))(parts)


def adamw(parts, w, m, v, name):
    n, n_rows, n_cols = parts.shape
    tr = _row_tile(n_rows, n_cols)

    def body(p_ref, w_ref, m_ref, v_ref, g_ref, d_ref, m2_ref, v2_ref):
        g = p_ref[0].astype(F32)
        for k in range(1, n):
            g = g + p_ref[k].astype(F32)
        m2 = ADAM_B1 * m_ref[...] + (1.0 - ADAM_B1) * g
        v2 = ADAM_B2 * v_ref[...] + (1.0 - ADAM_B2) * (g * g)
        m_hat = m2 / (1.0 - ADAM_B1 ** ADAM_STEP)
        v_hat = v2 / (1.0 - ADAM_B2 ** ADAM_STEP)
        g_ref[...] = g
        d_ref[...] = -ADAM_LR * (m_hat / (jnp.sqrt(v_hat) + ADAM_EPS) + ADAM_WD * w_ref[...])
        m2_ref[...] = m2
        v2_ref[...] = v2

    blk = pl.BlockSpec((tr, n_cols), lambda i: (i, 0))
    shape = jax.ShapeDtypeStruct((n_rows, n_cols), F32)
    return pl.pallas_call(
        body, name=name, grid=(n_rows // tr,), in_specs=[pl.BlockSpec((n, tr, n_cols), lambda i: (0, i, 0)), blk, blk, blk],
        out_specs=[blk] * 4, out_shape=[shape] * 4, compiler_params=_params("arbitrary"))(parts, w, m, v)


def _pack(arrays):
    flat = [jnp.pad(a.reshape(-1), (0, -a.size % LANES)) for a in arrays]
    buf = jnp.concatenate(flat)
    buf = jnp.pad(buf, (0, -buf.size % (SUBLANES * LANES)))
    return buf.reshape(-1, LANES)


def _unpack(buf, shapes):
    lead = buf.shape[:-2]
    flat = buf.reshape(lead + (-1,))
    out, pos = [], 0
    for shape in shapes:
        size = 1
        for s in shape:
            size *= s
        out.append(flat[..., pos:pos + size].reshape(lead + tuple(shape)))
        pos += size + (-size % LANES)
    return out


def _pos_embedding(seq_len, dim):
    rows = seq_len // GRID_W
    t = jnp.arange(rows * GRID_W)
    row = (t // GRID_W).astype(F32)
    col = (t % GRID_W).astype(F32)
    q = dim // 4
    omega = 1.0 / (10000.0 ** (jnp.arange(q, dtype=F32) / q))
    er = row[:, None] * omega
    ec = col[:, None] * omega
    return jnp.concatenate([jnp.sin(er), jnp.cos(er), jnp.sin(ec), jnp.cos(ec)], axis=-1)


def _cols_from_shards(g):
    return jnp.transpose(g, (1, 0, 2)).reshape(g.shape[1], -1)


def _shards_from_cols(w, n_shards=4):
    k, n = w.shape
    return jnp.transpose(w.reshape(k, n_shards, n // n_shards), (1, 0, 2))


def kernel(x, c, ctx, c_ctx, w_mod, b_mod, g_n1, w_ffn1_up, w_ffn1_down, g_n2, w_in, b_in, w_dw, b_dw, g_ln, b_ln, w_conf_out, w_lru_conv, b_lru_conv, w_rec_gate, b_rec_gate, w_in_gate, b_in_gate, lru_lambda, w_lru_out, w_out, g_n3, w_ffn2_up, w_ffn2_down, g_final, loss_target, m_c_ctx, m_w_mod, m_b_mod, m_g_n1, m_w_ffn1_up, m_w_ffn1_down, m_g_n2, m_w_in, m_b_in, m_w_dw, m_b_dw, m_g_ln, m_b_ln, m_w_conf_out, m_w_lru_conv, m_b_lru_conv, m_w_rec_gate, m_b_rec_gate, m_w_in_gate, m_b_in_gate, m_lru_lambda, m_w_lru_out, m_w_out, m_g_n3, m_w_ffn2_up, m_w_ffn2_down, m_g_final, v_c_ctx, v_w_mod, v_b_mod, v_g_n1, v_w_ffn1_up, v_w_ffn1_down, v_g_n2, v_w_in, v_b_in, v_w_dw, v_b_dw, v_g_ln, v_b_ln, v_w_conf_out, v_w_lru_conv, v_b_lru_conv, v_w_rec_gate, v_b_rec_gate, v_w_in_gate, v_b_in_gate, v_lru_lambda, v_w_lru_out, v_w_out, v_g_n3, v_w_ffn2_up, v_w_ffn2_down, v_g_final):
    given = dict(locals())
    wt = {n: given[n] for n in W_NAMES}
    mom = {n: given["m_" + n] for n in W_NAMES}
    var = {n: given["v_" + n] for n in W_NAMES}

    def flat2(a):
        if a.ndim == 1:
            return a.reshape(1, -1)
        a = a[0]
        return a if a.ndim == 2 else a.reshape(-1, a.shape[-1])

    t_lat, d = x.shape[1], x.shape[2]
    t_ctx = ctx.shape[1]
    rows = Rows(t_lat, t_ctx, d)
    xi, yi, ci = lax.axis_index("x"), lax.axis_index("y"), lax.axis_index("c")
    me, chip = 4 * xi + 2 * yi + ci, 2 * xi + yi
    lru_w = b_lru_conv.shape[-1]
    n_blk = lru_w // LANES
    taps = w_dw.shape[1]
    lru_taps = w_lru_conv.shape[1]
    ffn = w_ffn1_down.shape[1] * 4

    small_shapes = [(1, d)] + [flat2(wt[n]).shape for n in SMALL_SHARDED]
    small_all = exchange("all", [_pack([c] + [flat2(wt[n]) for n in SMALL_SHARDED])], False, "gather_small")[0]
    small_all = _unpack(small_all, small_shapes)
    c_all = small_all[0][:, 0, :]
    full = {n: jnp.concatenate([a[0], a[2], a[4], a[6]], axis=-1) for n, a in zip(SMALL_SHARDED, small_all[1:])}
    big = COL_SHARDED + ROW_SHARDED
    gathered = exchange("chips", [wt[n][0].astype(BF16) for n in big], False, "gather_weights")
    wb = {}
    for n, g in zip(big, gathered):
        wb[n] = _cols_from_shards(g) if n in COL_SHARDED else g.reshape(-1, g.shape[-1])

    c9 = jnp.concatenate([c_all, c_ctx.reshape(1, d), jnp.zeros((7, d), F32)], axis=0)
    mod_cols = mod_matmul(c9, w_mod[0], lax.dynamic_slice_in_dim(b_mod, chip * w_mod.shape[2], w_mod.shape[2], axis=1), "mod_matmul")
    mod_all = exchange("all", [mod_cols], False, "gather_mod")[0]
    mod9 = jnp.concatenate([mod_all[0], mod_all[2], mod_all[4], mod_all[6]], axis=-1)
    mv = jnp.stack([lax.dynamic_index_in_dim(mod9, me, 0, keepdims=False).reshape(N_MOD, d), mod9[8].reshape(N_MOD, d)])

    x_in = jnp.concatenate([x[0], ctx[0]], axis=0)
    pos = jnp.concatenate([_pos_embedding(t_lat, d), jnp.zeros((t_ctx, d), F32)], axis=0)

    def ffn_fwd(tag, h, w_up, w_down):
        gu = matmul(h, w_up, "nn", F32, f"{tag}_up")
        act = swiglu(rows, gu, f"{tag}_act")
        return gu, act, matmul(act, w_down, "nn", F32, f"{tag}_down")

    x0, h1 = norm_mod(rows, x_in, g_n1, mv, 0, 1, "norm1", pos=pos)
    gu1, act1, f1 = ffn_fwd("ffn1", h1, wb["w_ffn1_up"], wb["w_ffn1_down"])
    x1, h2 = norm_mod(rows, x0, g_n2, mv, 3, 4, "norm2", resid=(f1, 2, 0.5))

    conf_w = w_conf_out.shape[1] * 4
    col_groups = [(0, 2 * conf_w), (2 * conf_w, lru_w), (2 * conf_w + lru_w, lru_w), (2 * conf_w + 2 * lru_w, 2 * d)]
    pvg, ux, gl, pgc = [matmul(h2, wb["w_in"][:, s:s + n], "nn", F32, f"in_proj{k}", bias=b_in[:, s:s + n])
                        for k, (s, n) in enumerate(col_groups)]
    u = glu(rows, pvg, "glu")
    u2 = dwconv(rows, u, full["w_dw"], b_dw, taps // 2, F32, "conf_conv")
    s_act = ln_silu(rows, u2, g_ln, b_ln, "conf_ln")
    yc = matmul(s_act, wb["w_conf_out"], "nn", F32, "conf_out")

    xr = dwconv(rows, ux, full["w_lru_conv"], b_lru_conv, LRU_PAD_LEFT, F32, "lru_conv")
    w_rec, w_ing = w_rec_gate[0].astype(BF16), w_in_gate[0].astype(BF16)
    wcat = jnp.concatenate([w_rec[0], w_ing[0], w_rec[1], w_ing[1]], axis=-1)

    def per_block(a):
        return jnp.transpose(a.reshape(2, n_blk, LANES), (1, 0, 2)).reshape(n_blk, 1, 2 * LANES)

    b_rec, b_ing = full["b_rec_gate"].reshape(2, n_blk, LANES), full["b_in_gate"].reshape(2, n_blk, LANES)
    bcat = jnp.concatenate([b_rec[0], b_ing[0], b_rec[1], b_ing[1]], axis=-1).reshape(n_blk, 1, 4 * LANES)
    lam = per_block(full["lru_lambda"])
    a_gate, b_gate = lru_gates(rows, xr, wcat, bcat, lam, "lru_gates")
    h_scan, h_prev = lru_scan(rows, a_gate, b_gate, "lru_scan")
    yl_in = lru_merge(rows, h_scan, gl, "lru_merge")
    yl = matmul(yl_in, wb["w_lru_out"], "nn", F32, "lru_out")
    z = z_merge(rows, pgc, yc, yl, "z_merge")
    y = matmul(z, wb["w_out"], "nn", F32, "mix_out")

    x2, h3 = norm_mod(rows, x1, g_n3, mv, 6, 7, "norm3", resid=(y, 5, 1.0))
    gu2, act2, f2 = ffn_fwd("ffn2", h3, wb["w_ffn2_up"], wb["w_ffn2_down"])
    loss_part, dx3, dg_final = loss_head(rows, x2, f2, mv, g_final.reshape(1, d), loss_target[0], "loss_head")
    loss = lax.psum(loss_part[0, 0], ("x", "y", "c"))

    grads = {"g_final": dg_final}
    dmv = [None] * N_MOD

    def ffn_bwd(tag, dxn, x_prev, h, gu, act, f, w_up, w_down, g, ks):
        k_shift, k_scale, k_gate = ks
        df, dmv[k_gate] = gate_bwd(rows, dxn, f, mv, k_gate, 0.5, f"{tag}_gate_bwd")
        grads[f"w_{tag}_down"] = matmul(act, df, "tn", F32, f"{tag}_down_dw")
        dact = matmul(df, w_down, "nt", F32, f"{tag}_down_dx")
        dgu = swiglu_bwd(rows, gu, dact, f"{tag}_act_bwd")
        grads[f"w_{tag}_up"] = matmul(h, dgu, "tn", F32, f"{tag}_up_dw")
        dh = matmul(dgu, w_up, "nt", F32, f"{tag}_up_dx")
        dx, dg, dmv[k_shift], dmv[k_scale] = norm_mod_bwd(rows, x_prev, dh, dxn, g, mv, k_shift, k_scale, f"{tag}_norm_bwd")
        return dx, dg

    dx2, grads["g_n3"] = ffn_bwd("ffn2", dx3, x2, h3, gu2, act2, f2, wb["w_ffn2_up"], wb["w_ffn2_down"], g_n3, (6, 7, 8))

    dy, dmv[5] = gate_bwd(rows, dx2, y, mv, 5, 1.0, "mix_gate_bwd")
    grads["w_out"] = matmul(z, dy, "tn", F32, "mix_out_dw")
    dz = matmul(dy, wb["w_out"], "nt", F32, "mix_out_dx")
    dyc, dyl, dpgc = z_merge_bwd(rows, pgc, yc, yl, dz, "z_merge_bwd")

    grads["w_conf_out"] = matmul(s_act, dyc, "tn", F32, "conf_out_dw")
    ds_act = matmul(dyc, wb["w_conf_out"], "nt", F32, "conf_out_dx")
    du2, grads["g_ln"], grads["b_ln"] = ln_silu_bwd(rows, u2, ds_act, g_ln, b_ln, "conf_ln_bwd")
    grads["w_dw"], grads["b_dw"] = dwconv_bwd_w(rows, u, du2, taps, taps // 2, "conf_conv_dw")
    du = dwconv(rows, du2, full["w_dw"][::-1], None, taps - 1 - taps // 2, F32, "conf_conv_dx")
    dpvg = glu_bwd(rows, pvg, du, "glu_bwd")

    grads["w_lru_out"] = matmul(yl_in, dyl, "tn", F32, "lru_out_dw")
    dyl_in = matmul(dyl, wb["w_lru_out"], "nt", F32, "lru_out_dx")
    dh_sum, dgl = lru_merge_bwd(rows, h_scan, gl, dyl_in, "lru_merge_bwd")
    da_gate, db_gate = lru_scan_bwd(rows, dh_sum, a_gate, h_prev, "lru_scan_bwd")
    dxr, dwcat, dbcat, dlam = lru_gates_bwd(rows, xr, wcat, bcat, lam, da_gate, db_gate, "lru_gates_bwd")
    grads["w_rec_gate"] = jnp.stack([dwcat[:, :, 0:LANES], dwcat[:, :, 2 * LANES:3 * LANES]])
    grads["w_in_gate"] = jnp.stack([dwcat[:, :, LANES:2 * LANES], dwcat[:, :, 3 * LANES:4 * LANES]])
    dbcat = dbcat.reshape(n_blk, 4, LANES)
    grads["b_rec_gate"] = jnp.stack([dbcat[:, 0], dbcat[:, 2]]).reshape(2, lru_w)
    grads["b_in_gate"] = jnp.stack([dbcat[:, 1], dbcat[:, 3]]).reshape(2, lru_w)
    grads["lru_lambda"] = jnp.transpose(dlam.reshape(n_blk, 2, LANES), (1, 0, 2)).reshape(2, lru_w)
    grads["w_lru_conv"], grads["b_lru_conv"] = dwconv_bwd_w(rows, ux, dxr, lru_taps, LRU_PAD_LEFT, "lru_conv_dw")
    dux = dwconv(rows, dxr, full["w_lru_conv"][::-1], None, lru_taps - 1 - LRU_PAD_LEFT, BF16, "lru_conv_dx")

    dproj = jnp.concatenate([dpvg, dux, dgl, dpgc], axis=1)
    grads["b_in"] = col_sum(rows, dproj, "in_proj_db")
    grads["w_in"] = matmul(h2, dproj, "tn", F32, "in_proj_dw")
    dh2 = matmul(dproj, wb["w_in"], "nt", F32, "in_proj_dx")
    dx1, grads["g_n2"], dmv[3], dmv[4] = norm_mod_bwd(rows, x1, dh2, dx2, g_n2, mv, 3, 4, "norm2_bwd")

    dx0, grads["g_n1"] = ffn_bwd("ffn1", dx1, x0, h1, gu1, act1, f1, wb["w_ffn1_up"], wb["w_ffn1_down"], g_n1, (0, 1, 2))
    grad_x = dx0[:t_lat][None]

    pieces = [_shards_from_cols(grads[n]) if n in COL_SHARDED else grads[n].reshape(4, -1, grads[n].shape[-1]) for n in big]
    from_chips = exchange("chips", pieces, True, "scatter_grads")
    chip_sums = [sum_parts(p.reshape(4, -1, p.shape[-1]), f"chip_sum_{n}") for n, p in zip(big, from_chips)]
    from_cores = exchange("sibling", chip_sums, False, "swap_grads")
    out = {}
    for n, p in zip(big, from_cores):
        out[n] = adamw(p, flat2(wt[n]), flat2(mom[n]), flat2(var[n]), f"adamw_{n}")

    dmod = jnp.concatenate(dmv, axis=1)
    small_names = ['g_n1', 'g_n2', 'b_in', 'b_dw', 'g_ln', 'b_ln', 'b_lru_conv', 'w_rec_gate', 'w_in_gate', 'g_n3', 'g_final'] + SMALL_SHARDED
    small_list = [grads[n] for n in small_names] + [dmod[0], dmod[1]]
    small_parts = exchange("all", [_pack(small_list)], False, "gather_small_grads")[0]
    small_sum = _unpack(sum_parts(small_parts, "sum_small_grads"), [a.shape for a in small_list])
    total = dict(zip(small_names, small_sum))
    dmod_all = _unpack(small_parts, [a.shape for a in small_list])[-2].reshape(8, N_MOD * d)
    dmc = small_sum[-1].reshape(1, N_MOD * d)
    total["b_mod"] = small_sum[-2].reshape(1, N_MOD * d) + dmc

    d9 = jnp.concatenate([dmod_all, dmc, jnp.zeros((7, N_MOD * d), F32)], axis=0)
    d9_cols = lax.dynamic_slice_in_dim(d9, chip * w_mod.shape[2], w_mod.shape[2], axis=1)
    g_wmod, dc9 = mod_matmul_bwd(c9, d9_cols, w_mod[0], "mod_matmul_bwd")
    dc_all = exchange("all", [dc9[8:16]], False, "gather_dc")[0]
    total["c_ctx"] = sum_parts(jnp.stack([dc_all[0], dc_all[2], dc_all[4], dc_all[6]]), "sum_dc")[0:1]
    out["w_mod"] = adamw(g_wmod[None], w_mod[0], m_w_mod[0], v_w_mod[0], "adamw_w_mod")

    small_all_names = [n for n in W_NAMES if n not in out]
    g_local, shapes = [], []
    for n in small_all_names:
        g = total[n].reshape(flat2(wt[n]).shape[:-1] + (-1,)) if n in SMALL_SHARDED else total[n].reshape(flat2(wt[n]).shape)
        if n in SMALL_SHARDED:
            width = wt[n].shape[-1]
            g = lax.dynamic_slice_in_dim(g, chip * width, width, axis=1)
        g_local.append(g)
        shapes.append(g.shape)
    packed = adamw(_pack(g_local)[None], _pack([flat2(wt[n]) for n in small_all_names]),
                   _pack([flat2(mom[n]) for n in small_all_names]), _pack([flat2(var[n]) for n in small_all_names]), "adamw_small")
    for k, n in enumerate(small_all_names):
        out[n] = [_unpack(p, shapes)[k] for p in packed]

    results = [loss, grad_x]
    for k in range(4):
        results += [out[n][k].reshape(wt[n].shape) for n in W_NAMES]
    return tuple(results)
```

```python
import functools

import jax
import jax.numpy as jnp
from jax import lax
from jax.experimental import pallas as pl
from jax.experimental.pallas import tpu as pltpu

F32 = jnp.float32
BF16 = jnp.bfloat16
MESH = pl.DeviceIdType.MESH

EPS = 1e-6
GRID_W = 64
N_MOD = 9
LRU_C = 8.0
LRU_PAD_LEFT = 2
ADAM_LR, ADAM_B1, ADAM_B2, ADAM_EPS, ADAM_WD, ADAM_STEP = 0.001, 0.9, 0.999, 1e-08, 0.01, 10

LANES = 128
SUBLANES = 8
HALO = 16
VMEM_LIMIT = 56 * 1024 * 1024

W_NAMES = ['c_ctx', 'w_mod', 'b_mod', 'g_n1', 'w_ffn1_up', 'w_ffn1_down', 'g_n2', 'w_in', 'b_in', 'w_dw', 'b_dw',
           'g_ln', 'b_ln', 'w_conf_out', 'w_lru_conv', 'b_lru_conv', 'w_rec_gate', 'b_rec_gate', 'w_in_gate',
           'b_in_gate', 'lru_lambda', 'w_lru_out', 'w_out', 'g_n3', 'w_ffn2_up', 'w_ffn2_down', 'g_final']
COL_SHARDED = ['w_ffn1_up', 'w_in', 'w_ffn2_up']
ROW_SHARDED = ['w_ffn1_down', 'w_conf_out', 'w_lru_out', 'w_out', 'w_ffn2_down']
SMALL_SHARDED = ['w_dw', 'w_lru_conv', 'b_rec_gate', 'b_in_gate', 'lru_lambda']


def _params(*semantics):
    return pltpu.CompilerParams(dimension_semantics=semantics, vmem_limit_bytes=VMEM_LIMIT)


def _pick(n, target, mult=LANES):
    best = None
    for t in range(mult, min(n, target) + 1, mult):
        if n % t == 0:
            best = t
    return best or n


def _chunks(width, target=512):
    w = _pick(width, target)
    return [slice(s, s + w) for s in range(0, width, w)]


def _sigmoid(x):
    return jax.nn.sigmoid(x)


def _silu_and_grad(x):
    s = _sigmoid(x)
    return x * s, s * (1.0 + x * (1.0 - s))


_GELU_K = 0.7978845608028654


def _gelu_and_grad(x):
    t = jnp.tanh(_GELU_K * (x + 0.044715 * (x * x * x)))
    return 0.5 * x * (1.0 + t), 0.5 * (1.0 + t) + 0.5 * x * (1.0 - t * t) * _GELU_K * (1.0 + 3 * 0.044715 * x * x)


def _neg_expm1(z):
    series = -(z * (1.0 + z * (0.5 + z * (1.0 / 6.0 + z * (1.0 / 24.0)))))
    return jnp.where(z > -0.01, series, 1.0 - jnp.exp(z))


def _softplus(x):
    return jnp.maximum(x, 0.0) + jnp.log1p(jnp.exp(-jnp.abs(x)))


def _accumulate(ref, value, first):
    @pl.when(first)
    def _():
        ref[...] = value

    @pl.when(jnp.logical_not(first))
    def _():
        ref[...] += value


def _colsum(x):
    return jnp.sum(x, axis=0, keepdims=True)


class Rows:
    def __init__(self, t_lat, t_ctx, d_model):
        self.tm = _pick(t_ctx, 256, HALO)
        assert t_lat % self.tm == 0 and t_ctx % self.tm == 0
        self.n_lat = t_lat // self.tm
        self.n_all = (t_lat + t_ctx) // self.tm
        self.t_all = t_lat + t_ctx
        self.d = d_model

    def seg(self, i):
        return jnp.where(i >= self.n_lat, 1, 0)

    def seg_first(self, i):
        return jnp.logical_or(i == 0, i == self.n_lat)

    def seg_last(self, i):
        return jnp.logical_or(i == self.n_lat - 1, i == self.n_all - 1)

    def row(self, width, col=0):
        return pl.BlockSpec((self.tm, width), lambda i: (i, col))

    def vec(self, width):
        return pl.BlockSpec((1, width), lambda i: (0, 0))

    def full(self, shape):
        return pl.BlockSpec(shape, lambda i: (0,) * len(shape))

    def mod(self):
        return pl.BlockSpec((None, N_MOD, self.d), lambda i: (self.seg(i), 0, 0))

    def segsum(self):
        return pl.BlockSpec((None, 1, self.d), lambda i: (self.seg(i), 0, 0))

    def segsum_shape(self):
        return jax.ShapeDtypeStruct((2, 1, self.d), F32)

    def halo(self, width, side):
        per = self.tm // HALO
        last = self.t_all // HALO - 1
        if side < 0:
            return pl.BlockSpec((HALO, width), lambda i: (jnp.maximum(i * per - 1, 0), 0))
        return pl.BlockSpec((HALO, width), lambda i: (jnp.minimum((i + 1) * per, last), 0))

    def call(self, body, name, in_specs, out_specs, out_shape, scratch=()):
        return pl.pallas_call(body, name=name, grid=(self.n_all,), in_specs=in_specs, out_specs=out_specs,
                              out_shape=out_shape, scratch_shapes=list(scratch), compiler_params=_params("arbitrary"))


_PEER_FLIPS = {
    "chips": [(1, 0, 0), (0, 1, 0), (1, 1, 0)],
    "sibling": [(0, 0, 1)],
    "all": [(0, 0, 1), (0, 1, 0), (0, 1, 1), (1, 0, 0), (1, 0, 1), (1, 1, 0), (1, 1, 1)],
}


def _slot(kind, x, y, c):
    return {"chips": 2 * x + y, "sibling": c, "all": 4 * x + 2 * y + c}[kind]


def exchange(kind, srcs, indexed, name):
    flips = _PEER_FLIPS[kind]
    n_slots = len(flips) + 1
    n_t = len(srcs)
    by_slot = list(indexed) if isinstance(indexed, (list, tuple)) else [indexed] * n_t
    out_shapes = [jax.ShapeDtypeStruct(s.shape if ix else (n_slots,) + s.shape, s.dtype) for s, ix in zip(srcs, by_slot)]

    def body(*refs):
        src_refs, dst_refs = refs[:n_t], refs[n_t:2 * n_t]
        send_sems, recv_sems, local_sems = refs[2 * n_t:]
        x, y, c = lax.axis_index("x"), lax.axis_index("y"), lax.axis_index("c")
        me = _slot(kind, x, y, c)

        def piece(t, k):
            return src_refs[t].at[k] if by_slot[t] else src_refs[t]

        peers = [(1 - x if fx else x, 1 - y if fy else y, 1 - c if fc else c) for fx, fy, fc in flips]
        local = [pltpu.make_async_copy(piece(t, me), dst_refs[t].at[me], local_sems.at[t]) for t in range(n_t)]
        for cp in local:
            cp.start()
        sends = []
        for t in range(n_t):
            for p, peer in enumerate(peers):
                sends.append(pltpu.make_async_remote_copy(
                    src_ref=piece(t, _slot(kind, *peer)), dst_ref=dst_refs[t].at[me],
                    send_sem=send_sems.at[t, p], recv_sem=recv_sems.at[t, p], device_id=peer, device_id_type=MESH))
        for cp in sends:
            cp.start()
        for t in range(n_t):
            for p, peer in enumerate(peers):
                theirs = _slot(kind, *peer)
                pltpu.make_async_remote_copy(
                    src_ref=piece(t, me), dst_ref=dst_refs[t].at[theirs],
                    send_sem=send_sems.at[t, p], recv_sem=recv_sems.at[t, p], device_id=peer, device_id_type=MESH).wait_recv()
        for cp in sends:
            cp.wait_send()
        for cp in local:
            cp.wait()

    any_spec = pl.BlockSpec(memory_space=pl.ANY)
    outs = pl.pallas_call(
        body, name=name, out_shape=out_shapes, in_specs=[any_spec] * n_t, out_specs=[any_spec] * n_t,
        scratch_shapes=[pltpu.SemaphoreType.DMA((n_t, len(flips))), pltpu.SemaphoreType.DMA((n_t, len(flips))),
                        pltpu.SemaphoreType.DMA((n_t,))],
        compiler_params=pltpu.CompilerParams(has_side_effects=True),
    )(*srcs)
    return list(outs)


def matmul(a, b, mode, out_dtype, name, bias=None):
    if mode == "nn":
        (n_i, n_r), (_, n_j) = a.shape, b.shape
    elif mode == "nt":
        (n_i, n_r), (n_j, _) = a.shape, b.shape
    else:
        (n_r, n_i), (_, n_j) = a.shape, b.shape
    ti = _pick(n_i, 1408 if mode == "tn" else 768)
    tj = _pick(n_j, 1664)
    tr = n_r if (mode == "nn" and n_r <= 2816) else _pick(n_r, 768 if mode == "tn" else 1664)
    steps = n_r // tr
    a_spec = {"nn": pl.BlockSpec((ti, tr), lambda i, j, r: (i, r)), "nt": pl.BlockSpec((ti, tr), lambda i, j, r: (i, r)),
              "tn": pl.BlockSpec((tr, ti), lambda i, j, r: (r, i))}[mode]
    b_spec = {"nn": pl.BlockSpec((tr, tj), lambda i, j, r: (r, j)), "nt": pl.BlockSpec((tj, tr), lambda i, j, r: (j, r)),
              "tn": pl.BlockSpec((tr, tj), lambda i, j, r: (r, j))}[mode]
    dims = {"nn": (((1,), (0,)), ((), ())), "nt": (((1,), (1,)), ((), ())), "tn": (((0,), (0,)), ((), ()))}[mode]

    def body(*refs):
        a_ref, b_ref = refs[:2]
        bias_ref = refs[2] if bias is not None else None
        o_ref = refs[3] if bias is not None else refs[2]
        prod = lax.dot_general(a_ref[...].astype(BF16), b_ref[...].astype(BF16), dims, preferred_element_type=F32)

        def finish(total):
            if bias_ref is not None:
                total = total + bias_ref[...]
            o_ref[...] = total.astype(out_dtype)

        if steps == 1:
            finish(prod)
        else:
            acc = refs[-1]
            r = pl.program_id(2)

            @pl.when(r == 0)
            def _():
                acc[...] = prod

            @pl.when(r > 0)
            def _():
                acc[...] += prod

            @pl.when(r == steps - 1)
            def _():
                finish(acc[...])

    in_specs = [a_spec, b_spec] + ([pl.BlockSpec((1, tj), lambda i, j, r: (0, j))] if bias is not None else [])
    args = (a, b) + ((bias,) if bias is not None else ())
    return pl.pallas_call(
        body, name=name, grid=(n_i // ti, n_j // tj, steps), in_specs=in_specs,
        out_specs=pl.BlockSpec((ti, tj), lambda i, j, r: (i, j)), out_shape=jax.ShapeDtypeStruct((n_i, n_j), out_dtype),
        scratch_shapes=[pltpu.VMEM((ti, tj), F32)] if steps > 1 else [],
        compiler_params=_params("arbitrary", "arbitrary", "arbitrary"),
    )(*args)


def norm_mod(rows, x, g, mv, shift_k, scale_k, name, resid=None, pos=None):
    d = rows.d
    changed = resid is not None or pos is not None

    def body(*refs):
        refs = list(refs)
        x_ref, g_ref, mv_ref = refs[:3]
        rest = refs[3:]
        xv = x_ref[...]
        if pos is not None:
            xv = xv + rest.pop(0)[...]
        if resid is not None:
            xv = xv + resid[2] * mv_ref[resid[1]:resid[1] + 1, :] * rest.pop(0)[...]
        if changed:
            rest.pop(0)[...] = xv
        r = lax.rsqrt(jnp.mean(xv * xv, axis=-1, keepdims=True) + EPS)
        h = (xv * r) * g_ref[...]
        if shift_k is not None:
            h = h * (1.0 + mv_ref[scale_k:scale_k + 1, :]) + mv_ref[shift_k:shift_k + 1, :]
        rest.pop(0)[...] = h.astype(BF16)

    ins = [x, g, mv] + ([pos] if pos is not None else []) + ([resid[0]] if resid is not None else [])
    in_specs = [rows.row(d), rows.vec(d), rows.mod()] + [rows.row(d)] * (len(ins) - 3)
    out_shape = ([jax.ShapeDtypeStruct((rows.t_all, d), F32)] if changed else []) + [jax.ShapeDtypeStruct((rows.t_all, d), BF16)]
    outs = rows.call(body, name, in_specs, [rows.row(d)] * len(out_shape), out_shape)(*ins)
    return (outs[0], outs[1]) if changed else (None, outs[0])


def norm_mod_bwd(rows, x, dh, dxn, g, mv, shift_k, scale_k, name):
    d = rows.d

    def body(x_ref, dh_ref, dxn_ref, g_ref, mv_ref, dx_ref, dg_ref, dsh_ref, dsc_ref):
        i = pl.program_id(0)
        xv, dhv, gv = x_ref[...], dh_ref[...], g_ref[...]
        r = lax.rsqrt(jnp.mean(xv * xv, axis=-1, keepdims=True) + EPS)
        n = xv * r
        dy = dhv * (1.0 + mv_ref[scale_k:scale_k + 1, :])
        dn = dy * gv
        dx_ref[...] = dxn_ref[...] + r * (dn - n * jnp.mean(dn * n, axis=-1, keepdims=True))
        _accumulate(dg_ref, _colsum(dy * n), i == 0)
        _accumulate(dsh_ref, _colsum(dhv), rows.seg_first(i))
        _accumulate(dsc_ref, _colsum(dhv * (n * gv)), rows.seg_first(i))

    return rows.call(
        body, name, [rows.row(d), rows.row(d), rows.row(d), rows.vec(d), rows.mod()],
        [rows.row(d), rows.vec(d), rows.segsum(), rows.segsum()],
        [jax.ShapeDtypeStruct((rows.t_all, d), F32), jax.ShapeDtypeStruct((1, d), F32), rows.segsum_shape(), rows.segsum_shape()],
    )(x, dh, dxn, g, mv)


def gate_bwd(rows, dxn, f, mv, gate_k, scale, name):
    d = rows.d

    def body(dxn_ref, f_ref, mv_ref, df_ref, dgate_ref):
        i = pl.program_id(0)
        dv = dxn_ref[...]
        df_ref[...] = (scale * mv_ref[gate_k:gate_k + 1, :] * dv).astype(BF16)
        _accumulate(dgate_ref, scale * _colsum(dv * f_ref[...]), rows.seg_first(i))

    return rows.call(body, name, [rows.row(d), rows.row(d), rows.mod()], [rows.row(d), rows.segsum()],
                     [jax.ShapeDtypeStruct((rows.t_all, d), BF16), rows.segsum_shape()])(dxn, f, mv)


def swiglu(rows, gu, name):
    f = gu.shape[1] // 2

    def body(g_ref, u_ref, o_ref):
        for cs in _chunks(f):
            o_ref[:, cs] = (_silu_and_grad(g_ref[:, cs])[0] * u_ref[:, cs]).astype(BF16)

    return rows.call(body, name, [rows.row(f, 0), rows.row(f, 1)], rows.row(f), jax.ShapeDtypeStruct((rows.t_all, f), BF16))(gu, gu)


def swiglu_bwd(rows, gu, dact, name):
    f = gu.shape[1] // 2

    def body(g_ref, u_ref, da_ref, o_ref):
        for cs in _chunks(f):
            act, dact_dg = _silu_and_grad(g_ref[:, cs])
            dav = da_ref[:, cs]
            o_ref[:, cs] = (dav * u_ref[:, cs] * dact_dg).astype(BF16)
            o_ref[:, slice(f + cs.start, f + cs.stop)] = (dav * act).astype(BF16)

    return rows.call(body, name, [rows.row(f, 0), rows.row(f, 1), rows.row(f)], rows.row(2 * f),
                     jax.ShapeDtypeStruct((rows.t_all, 2 * f), BF16))(gu, gu, dact)


def glu(rows, pvg, name):
    w = pvg.shape[1] // 2

    def body(v_ref, t_ref, o_ref):
        for cs in _chunks(w):
            o_ref[:, cs] = v_ref[:, cs] * _sigmoid(t_ref[:, cs])

    return rows.call(body, name, [rows.row(w, 0), rows.row(w, 1)], rows.row(w), jax.ShapeDtypeStruct((rows.t_all, w), F32))(pvg, pvg)


def glu_bwd(rows, pvg, du, name):
    w = pvg.shape[1] // 2

    def body(v_ref, t_ref, du_ref, o_ref):
        for cs in _chunks(w):
            s = _sigmoid(t_ref[:, cs])
            duv = du_ref[:, cs]
            o_ref[:, cs] = (duv * s).astype(BF16)
            o_ref[:, slice(w + cs.start, w + cs.stop)] = (duv * v_ref[:, cs] * s * (1.0 - s)).astype(BF16)

    return rows.call(body, name, [rows.row(w, 0), rows.row(w, 1), rows.row(w)], rows.row(2 * w),
                     jax.ShapeDtypeStruct((rows.t_all, 2 * w), BF16))(pvg, pvg, du)


def ln_silu(rows, u, g, b, name):
    w = u.shape[1]

    def body(u_ref, g_ref, b_ref, o_ref):
        uv = u_ref[...]
        xc = uv - jnp.mean(uv, axis=-1, keepdims=True)
        n = xc * lax.rsqrt(jnp.mean(xc * xc, axis=-1, keepdims=True) + EPS)
        o_ref[...] = _silu_and_grad(n * g_ref[...] + b_ref[...])[0].astype(BF16)

    return rows.call(body, name, [rows.row(w), rows.vec(w), rows.vec(w)], rows.row(w),
                     jax.ShapeDtypeStruct((rows.t_all, w), BF16))(u, g, b)


def ln_silu_bwd(rows, u, ds, g, b, name):
    w = u.shape[1]

    def body(u_ref, ds_ref, g_ref, b_ref, du_ref, dg_ref, db_ref):
        i = pl.program_id(0)
        uv, gv = u_ref[...], g_ref[...]
        xc = uv - jnp.mean(uv, axis=-1, keepdims=True)
        r = lax.rsqrt(jnp.mean(xc * xc, axis=-1, keepdims=True) + EPS)
        n = xc * r
        dy = ds_ref[...] * _silu_and_grad(n * gv + b_ref[...])[1]
        dn = dy * gv
        du_ref[...] = r * (dn - jnp.mean(dn, axis=-1, keepdims=True) - n * jnp.mean(dn * n, axis=-1, keepdims=True))
        _accumulate(dg_ref, _colsum(dy * n), i == 0)
        _accumulate(db_ref, _colsum(dy), i == 0)

    return rows.call(body, name, [rows.row(w), rows.row(w), rows.vec(w), rows.vec(w)], [rows.row(w), rows.vec(w), rows.vec(w)],
                     [jax.ShapeDtypeStruct((rows.t_all, w), F32), jax.ShapeDtypeStruct((1, w), F32),
                      jax.ShapeDtypeStruct((1, w), F32)])(u, ds, g, b)


def lru_merge(rows, h, gl, name):
    w = gl.shape[1]

    def body(h_ref, gl_ref, o_ref):
        for cs in _chunks(w):
            o_ref[:, cs] = ((h_ref[0, :, cs] + h_ref[1, :, cs]) * _gelu_and_grad(gl_ref[:, cs])[0]).astype(BF16)

    return rows.call(body, name, [pl.BlockSpec((2, rows.tm, w), lambda i: (0, i, 0)), rows.row(w)], rows.row(w),
                     jax.ShapeDtypeStruct((rows.t_all, w), BF16))(h, gl)


def lru_merge_bwd(rows, h, gl, dy, name):
    w = gl.shape[1]

    def body(h_ref, gl_ref, dy_ref, dh_ref, dgl_ref):
        for cs in _chunks(w):
            act, dact = _gelu_and_grad(gl_ref[:, cs])
            dyv = dy_ref[:, cs]
            dh_ref[:, cs] = dyv * act
            dgl_ref[:, cs] = (dyv * (h_ref[0, :, cs] + h_ref[1, :, cs]) * dact).astype(BF16)

    return rows.call(body, name, [pl.BlockSpec((2, rows.tm, w), lambda i: (0, i, 0)), rows.row(w), rows.row(w)],
                     [rows.row(w), rows.row(w)],
                     [jax.ShapeDtypeStruct((rows.t_all, w), F32), jax.ShapeDtypeStruct((rows.t_all, w), BF16)])(h, gl, dy)


def z_merge(rows, pgc, yc, yl, name):
    d = rows.d

    def body(gc_ref, gr_ref, yc_ref, yl_ref, o_ref):
        for cs in _chunks(d):
            o_ref[:, cs] = (_sigmoid(gc_ref[:, cs]) * yc_ref[:, cs] + _sigmoid(gr_ref[:, cs]) * yl_ref[:, cs]).astype(BF16)

    return rows.call(body, name, [rows.row(d, 0), rows.row(d, 1), rows.row(d), rows.row(d)], rows.row(d),
                     jax.ShapeDtypeStruct((rows.t_all, d), BF16))(pgc, pgc, yc, yl)


def z_merge_bwd(rows, pgc, yc, yl, dz, name):
    d = rows.d

    def body(gc_ref, gr_ref, yc_ref, yl_ref, dz_ref, dyc_ref, dyl_ref, dp_ref):
        for cs in _chunks(d):
            sc, sr, dzv = _sigmoid(gc_ref[:, cs]), _sigmoid(gr_ref[:, cs]), dz_ref[:, cs]
            dyc_ref[:, cs] = (dzv * sc).astype(BF16)
            dyl_ref[:, cs] = (dzv * sr).astype(BF16)
            dp_ref[:, cs] = (dzv * yc_ref[:, cs] * sc * (1.0 - sc)).astype(BF16)
            dp_ref[:, slice(d + cs.start, d + cs.stop)] = (dzv * yl_ref[:, cs] * sr * (1.0 - sr)).astype(BF16)

    return rows.call(body, name, [rows.row(d, 0), rows.row(d, 1), rows.row(d), rows.row(d), rows.row(d)],
                     [rows.row(d), rows.row(d), rows.row(2 * d)],
                     [jax.ShapeDtypeStruct((rows.t_all, d), BF16), jax.ShapeDtypeStruct((rows.t_all, d), BF16),
                      jax.ShapeDtypeStruct((rows.t_all, 2 * d), BF16)])(pgc, pgc, yc, yl, dz)


def col_sum(rows, x, name):
    w = x.shape[1]

    def body(x_ref, o_ref):
        i = pl.program_id(0)
        for cs in _chunks(w):
            _accumulate(o_ref.at[:, cs], _colsum(x_ref[:, cs].astype(F32)), i == 0)

    return rows.call(body, name, [rows.row(w)], rows.vec(w), jax.ShapeDtypeStruct((1, w), F32))(x)


def loss_head(rows, x, f, mv, g, target, name):
    d = rows.d

    def body(x_ref, f_ref, mv_ref, g_ref, t_ref, loss_ref, dx_ref, dg_ref):
        i = pl.program_id(0)
        valid = jnp.where(i < rows.n_lat, 1.0, 0.0)
        xv = x_ref[...] + 0.5 * mv_ref[8:9, :] * f_ref[...]
        gv = g_ref[...]
        r = lax.rsqrt(jnp.mean(xv * xv, axis=-1, keepdims=True) + EPS)
        n = xv * r
        err = (n * gv - t_ref[...]) * valid
        part = 0.5 * jnp.sum(jnp.mean(err * err, axis=-1, keepdims=True), axis=0, keepdims=True)
        _accumulate(loss_ref, jnp.broadcast_to(part, (1, LANES)), i == 0)
        dy = err * (1.0 / d)
        dn = dy * gv
        dx_ref[...] = r * (dn - n * jnp.mean(dn * n, axis=-1, keepdims=True))
        _accumulate(dg_ref, _colsum(dy * n), i == 0)

    target_spec = pl.BlockSpec((rows.tm, d), lambda i: (jnp.minimum(i, rows.n_lat - 1), 0))
    return rows.call(body, name, [rows.row(d), rows.row(d), rows.mod(), rows.vec(d), target_spec],
                     [rows.vec(LANES), rows.row(d), rows.vec(d)],
                     [jax.ShapeDtypeStruct((1, LANES), F32), jax.ShapeDtypeStruct((rows.t_all, d), F32),
                      jax.ShapeDtypeStruct((1, d), F32)])(x, f, mv, g, target)


def _extended(rows, i, prev_ref, cur_ref, next_ref, cs):
    prev = jnp.where(rows.seg_first(i), 0.0, prev_ref[:, cs].astype(F32))
    nxt = jnp.where(rows.seg_last(i), 0.0, next_ref[:, cs].astype(F32))
    return jnp.concatenate([prev, cur_ref[:, cs].astype(F32), nxt], axis=0)


def _shifted(ext, offset, tm):
    n = ext.shape[0]
    return pltpu.roll(ext, (-offset) % n, 0)[HALO:HALO + tm] if offset else ext[HALO:HALO + tm]


def _pad_taps(w):
    k = w.shape[0]
    return jnp.pad(w, ((0, -k % SUBLANES), (0, 0)))


def dwconv(rows, u, w, b, pad_left, out_dtype, name):
    taps, width = w.shape
    wp = _pad_taps(w)

    def body(*refs):
        prev_ref, cur_ref, next_ref, w_ref = refs[:4]
        b_ref = refs[4] if b is not None else None
        o_ref = refs[-1]
        i = pl.program_id(0)
        for cs in _chunks(width, LANES):
            ext = _extended(rows, i, prev_ref, cur_ref, next_ref, cs)
            acc = jnp.zeros((rows.tm, LANES), F32) if b_ref is None else jnp.broadcast_to(b_ref[:, cs], (rows.tm, LANES))
            for k in range(taps):
                acc = acc + w_ref[k:k + 1, cs] * _shifted(ext, k - pad_left, rows.tm)
            o_ref[:, cs] = acc.astype(out_dtype)

    ins = [u, u, u, wp] + ([b] if b is not None else [])
    in_specs = [rows.halo(width, -1), rows.row(width), rows.halo(width, 1), rows.full(wp.shape)] + ([rows.vec(width)] if b is not None else [])
    return rows.call(body, name, in_specs, rows.row(width), jax.ShapeDtypeStruct((rows.t_all, width), out_dtype))(*ins)


def dwconv_bwd_w(rows, u, dy, taps, pad_left, name):
    width = u.shape[1]
    taps_p = taps + (-taps % SUBLANES)

    def body(prev_ref, cur_ref, next_ref, dy_ref, dw_ref, db_ref):
        i = pl.program_id(0)
        tap_row = lax.broadcasted_iota(jnp.int32, (taps_p, LANES), 0)
        for cs in _chunks(width, LANES):
            ext = _extended(rows, i, prev_ref, cur_ref, next_ref, cs)
            dyv = dy_ref[:, cs]
            total = jnp.zeros((taps_p, LANES), F32)
            for k in range(taps):
                total = total + jnp.where(tap_row == k, _colsum(dyv * _shifted(ext, k - pad_left, rows.tm)), 0.0)
            _accumulate(dw_ref.at[:, cs], total, i == 0)
            _accumulate(db_ref.at[:, cs], _colsum(dyv), i == 0)

    dw, db = rows.call(body, name, [rows.halo(width, -1), rows.row(width), rows.halo(width, 1), rows.row(width)],
                       [rows.full((taps_p, width)), rows.vec(width)],
                       [jax.ShapeDtypeStruct((taps_p, width), F32), jax.ShapeDtypeStruct((1, width), F32)])(u, u, u, dy)
    return dw[:taps], db


def _gate_values(xr, pre, lam, d):
    r = _sigmoid(pre[:, (2 * d) * LANES:(2 * d + 1) * LANES])
    ig = _sigmoid(pre[:, (2 * d + 1) * LANES:(2 * d + 2) * LANES])
    sp = _softplus(-lam[:, d * LANES:(d + 1) * LANES])
    log_a = -LRU_C * r * sp
    return r, ig, sp, jnp.exp(log_a), jnp.sqrt(_neg_expm1(2.0 * log_a))


def lru_gates(rows, xr, wcat, bcat, lam, name):
    n_blk = wcat.shape[0]
    width = xr.shape[1]

    def body(xr_ref, w_ref, b_ref, lam_ref, a_ref, bb_ref):
        for h in range(n_blk):
            cs = slice(h * LANES, (h + 1) * LANES)
            xv = xr_ref[:, cs]
            pre = jnp.dot(xv.astype(BF16), w_ref[h], preferred_element_type=F32) + b_ref[h]
            for d in range(2):
                _, ig, _, a, q = _gate_values(xv, pre, lam_ref[h], d)
                a_ref[d, :, cs] = a
                bb_ref[d, :, cs] = q * (ig * xv)

    both = pl.BlockSpec((2, rows.tm, width), lambda i: (0, i, 0))
    shape = jax.ShapeDtypeStruct((2, rows.t_all, width), F32)
    return rows.call(body, name, [rows.row(width), rows.full(wcat.shape), rows.full(bcat.shape), rows.full(lam.shape)],
                     [both, both], [shape, shape])(xr, wcat, bcat, lam)


def lru_gates_bwd(rows, xr, wcat, bcat, lam, da, dbb, name):
    n_blk = wcat.shape[0]
    width = xr.shape[1]

    def body(xr_ref, w_ref, b_ref, lam_ref, da_ref, dbb_ref, dxr_ref, dw_ref, db_ref, dlam_ref):
        i = pl.program_id(0)
        for h in range(n_blk):
            cs = slice(h * LANES, (h + 1) * LANES)
            xv = xr_ref[:, cs]
            xb = xv.astype(BF16)
            wv = w_ref[h]
            pre = jnp.dot(xb, wv, preferred_element_type=F32) + b_ref[h]
            dxr = jnp.zeros_like(xv)
            dpre, dlam = [], []
            for d in range(2):
                r, ig, sp, a, q = _gate_values(xv, pre, lam_ref[h], d)
                dav, dbv = da_ref[d, :, cs], dbb_ref[d, :, cs]
                dlog_a = dav * a - dbv * (ig * xv) * (a * a) / q
                dpre.append(dlog_a * (-LRU_C * sp) * r * (1.0 - r))
                dpre.append(dbv * q * xv * ig * (1.0 - ig))
                dxr = dxr + dbv * q * ig
                dlam.append(_colsum(dlog_a * (-LRU_C * r)) * (-_sigmoid(-lam_ref[h][:, d * LANES:(d + 1) * LANES])))
            dpre = jnp.concatenate(dpre, axis=1)
            dpb = dpre.astype(BF16)
            dxr_ref[:, cs] = dxr + lax.dot_general(dpb, wv, (((1,), (1,)), ((), ())), preferred_element_type=F32)
            _accumulate(dw_ref.at[h], lax.dot_general(xb, dpb, (((0,), (0,)), ((), ())), preferred_element_type=F32), i == 0)
            _accumulate(db_ref.at[h], _colsum(dpre), i == 0)
            _accumulate(dlam_ref.at[h], jnp.concatenate(dlam, axis=1), i == 0)

    both = pl.BlockSpec((2, rows.tm, width), lambda i: (0, i, 0))
    return rows.call(
        body, name, [rows.row(width), rows.full(wcat.shape), rows.full(bcat.shape), rows.full(lam.shape), both, both],
        [rows.row(width), rows.full(wcat.shape), rows.full(bcat.shape), rows.full(lam.shape)],
        [jax.ShapeDtypeStruct((rows.t_all, width), F32), jax.ShapeDtypeStruct(wcat.shape, F32),
         jax.ShapeDtypeStruct(bcat.shape, F32), jax.ShapeDtypeStruct(lam.shape, F32)])(xr, wcat, bcat, lam, da, dbb)


def _tile_scan(a, b, reverse):
    n = a.shape[0]
    row = lax.broadcasted_iota(jnp.int32, a.shape, 0)
    k = 1
    while k < n:
        ok = (row < n - k) if reverse else (row >= k)
        shift = n - k if reverse else k
        b = b + a * jnp.where(ok, pltpu.roll(b, shift, 0), 0.0)
        a = a * jnp.where(ok, pltpu.roll(a, shift, 0), 1.0)
        k *= 2
    return a, b


def _neighbour(v, edge, reverse):
    n = v.shape[0]
    row = lax.broadcasted_iota(jnp.int32, v.shape, 0)
    if reverse:
        return jnp.where(row < n - 1, pltpu.roll(v, n - 1, 0), edge)
    return jnp.where(row >= 1, pltpu.roll(v, 1, 0), edge)


def _scan_call(rows, body, name, ins, in_specs, n_out, width, adjoint):
    n_all, n_lat = rows.n_all, rows.n_lat

    def tile(d, s):
        s = n_all - 1 - s if adjoint else s
        return jnp.where(d == 0, (s + n_lat) % n_all, n_all - 1 - s)

    def per_dir(d, s):
        return (d, tile(d, s), 0)

    specs = [pl.BlockSpec((None, rows.tm, width), per_dir) if kind == "dir" else
             pl.BlockSpec((rows.tm, width), lambda d, s: (tile(d, s), 0)) for kind in in_specs]
    shape = jax.ShapeDtypeStruct((2, rows.t_all, width), F32)
    return pl.pallas_call(
        body, name=name, grid=(2, n_all), in_specs=specs, out_specs=[pl.BlockSpec((None, rows.tm, width), per_dir)] * n_out,
        out_shape=[shape] * n_out, scratch_shapes=[pltpu.VMEM((SUBLANES, width), F32)],
        compiler_params=_params("arbitrary", "arbitrary"))(*ins)


def lru_scan(rows, a, bb, name):
    width = a.shape[2]
    n = rows.tm

    def body(a_ref, bb_ref, h_ref, hp_ref, carry):
        d, s = pl.program_id(0), pl.program_id(1)

        @pl.when(s == 0)
        def _():
            carry[...] = jnp.zeros_like(carry)

        def run(reverse):
            for cs in _chunks(width, LANES):
                cum, h0 = _tile_scan(a_ref[:, cs], bb_ref[:, cs], reverse)
                h_in = carry[0:1, cs]
                h = h0 + cum * h_in
                h_ref[:, cs] = h
                hp_ref[:, cs] = _neighbour(h, h_in, reverse)
                carry[0:1, cs] = h[0:1] if reverse else h[n - 1:n]

        pl.when(d == 0)(lambda: run(False))
        pl.when(d == 1)(lambda: run(True))

    return _scan_call(rows, body, name, [a, bb], ["dir", "dir"], 2, width, adjoint=False)


def lru_scan_bwd(rows, dh, a, hp, name):
    width = a.shape[2]
    n = rows.tm

    def body(dh_ref, a_ref, hp_ref, da_ref, dbb_ref, carry):
        d, s = pl.program_id(0), pl.program_id(1)

        @pl.when(s == 0)
        def _():
            carry[...] = jnp.zeros_like(carry)

        def run(reverse):
            for cs in _chunks(width, LANES):
                av = a_ref[:, cs]
                cum, g0 = _tile_scan(_neighbour(av, 1.0, reverse), dh_ref[:, cs], reverse)
                g = g0 + cum * carry[0:1, cs]
                da_ref[:, cs] = g * hp_ref[:, cs]
                dbb_ref[:, cs] = g
                carry[0:1, cs] = (av * g)[0:1] if reverse else (av * g)[n - 1:n]

        pl.when(d == 0)(lambda: run(True))
        pl.when(d == 1)(lambda: run(False))

    return _scan_call(rows, body, name, [dh, a, hp], ["shared", "dir", "dir"], 2, width, adjoint=True)


def mod_matmul(c9, w_shard, b_shard, name):
    n = w_shard.shape[1]
    tn = _pick(n, 768)

    def body(c_ref, w_ref, b_ref, o_ref):
        act = _silu_and_grad(c_ref[...])[0]
        o_ref[...] = jnp.dot(act, w_ref[...], preferred_element_type=F32, precision=lax.Precision.HIGHEST) + b_ref[...]

    return pl.pallas_call(
        body, name=name, grid=(n // tn,),
        in_specs=[pl.BlockSpec(c9.shape, lambda j: (0, 0)), pl.BlockSpec((w_shard.shape[0], tn), lambda j: (0, j)),
                  pl.BlockSpec((1, tn), lambda j: (0, j))],
        out_specs=pl.BlockSpec((c9.shape[0], tn), lambda j: (0, j)), out_shape=jax.ShapeDtypeStruct((c9.shape[0], n), F32),
        compiler_params=_params("arbitrary"))(c9, w_shard, b_shard)


def mod_matmul_bwd(c9, d9, w_shard, name):
    n = w_shard.shape[1]
    tn = _pick(n, 768)
    steps = n // tn

    def body(c_ref, d_ref, w_ref, gw_ref, gc_ref):
        j = pl.program_id(0)
        act, dact = _silu_and_grad(c_ref[...])
        dv = d_ref[...]
        gw_ref[...] = lax.dot_general(act, dv, (((0,), (0,)), ((), ())), preferred_element_type=F32, precision=lax.Precision.HIGHEST)
        part = lax.dot_general(dv, w_ref[...], (((1,), (1,)), ((), ())), preferred_element_type=F32, precision=lax.Precision.HIGHEST)
        _accumulate(gc_ref, part * dact, j == 0)

    return pl.pallas_call(
        body, name=name, grid=(steps,),
        in_specs=[pl.BlockSpec(c9.shape, lambda j: (0, 0)), pl.BlockSpec((c9.shape[0], tn), lambda j: (0, j)),
                  pl.BlockSpec((w_shard.shape[0], tn), lambda j: (0, j))],
        out_specs=[pl.BlockSpec((w_shard.shape[0], tn), lambda j: (0, j)), pl.BlockSpec(c9.shape, lambda j: (0, 0))],
        out_shape=[jax.ShapeDtypeStruct(w_shard.shape, F32), jax.ShapeDtypeStruct(c9.shape, F32)],
        compiler_params=_params("arbitrary"))(c9, d9, w_shard)


def _row_tile(n_rows, n_cols):
    return _pick(n_rows, max(2 * SUBLANES, (256 * 1024) // n_cols), 2 * SUBLANES)


def sum_parts(parts, name):
    n, n_rows, n_cols = parts.shape
    tr = _row_tile(n_rows, n_cols)

    def body(p_ref, o_ref):
        total = p_ref[0].astype(F32)
        for k in range(1, n):
            total = total + p_ref[k].astype(F32)
        o_ref[...] = total

    return pl.pallas_call(
        body, name=name, grid=(n_rows // tr,), in_specs=[pl.BlockSpec((n, tr, n_cols), lambda i: (0, i, 0))],
        out_specs=pl.BlockSpec((tr, n_cols), lambda i: (i, 0)), out_shape=jax.ShapeDtypeStruct((n_rows, n_cols), F32),
        compiler_params=_params("arbitrary"))(parts)


def _adamw_update(g, w_ref, m_ref, v_ref, g_ref, d_ref, m2_ref, v2_ref):
    m2 = ADAM_B1 * m_ref[...] + (1.0 - ADAM_B1) * g
    v2 = ADAM_B2 * v_ref[...] + (1.0 - ADAM_B2) * (g * g)
    m_hat = m2 / (1.0 - ADAM_B1 ** ADAM_STEP)
    v_hat = v2 / (1.0 - ADAM_B2 ** ADAM_STEP)
    g_ref[...] = g
    d_ref[...] = -ADAM_LR * (m_hat / (jnp.sqrt(v_hat) + ADAM_EPS) + ADAM_WD * w_ref[...])
    m2_ref[...] = m2
    v2_ref[...] = v2


def adamw(parts, w, m, v, name):
    n, n_rows, n_cols = parts.shape
    tr = _row_tile(n_rows, n_cols)

    def body(p_ref, *refs):
        g = p_ref[0].astype(F32)
        for k in range(1, n):
            g = g + p_ref[k].astype(F32)
        _adamw_update(g, *refs)

    blk = pl.BlockSpec((tr, n_cols), lambda i: (i, 0))
    shape = jax.ShapeDtypeStruct((n_rows, n_cols), F32)
    return pl.pallas_call(
        body, name=name, grid=(n_rows // tr,), in_specs=[pl.BlockSpec((n, tr, n_cols), lambda i: (0, i, 0)), blk, blk, blk],
        out_specs=[blk] * 4, out_shape=[shape] * 4, compiler_params=_params("arbitrary"))(parts, w, m, v)


def adamw_pair(part, w, m, v, name):
    n_rows, n_cols = part.shape
    tr = _row_tile(n_rows, n_cols)
    steps = n_rows // tr

    def body(p_ref, w_ref, m_ref, v_ref, g_ref, d_ref, m2_ref, v2_ref, inbox, send_sems, recv_sems, credits):
        i = pl.program_id(0)
        slot = i % 2
        sibling = (lax.axis_index("x"), lax.axis_index("y"), 1 - lax.axis_index("c"))

        @pl.when(i >= 2)
        def _():
            pl.semaphore_wait(credits.at[slot], 1)

        copy = pltpu.make_async_remote_copy(src_ref=p_ref, dst_ref=inbox.at[slot], send_sem=send_sems.at[slot],
                                            recv_sem=recv_sems.at[slot], device_id=sibling, device_id_type=MESH)
        copy.start()
        copy.wait_recv()
        copy.wait_send()
        _adamw_update(p_ref[...] + inbox[slot], w_ref, m_ref, v_ref, g_ref, d_ref, m2_ref, v2_ref)

        @pl.when(i + 2 < steps)
        def _():
            pl.semaphore_signal(credits.at[slot], inc=1, device_id=sibling, device_id_type=MESH)

    blk = pl.BlockSpec((tr, n_cols), lambda i: (i, 0))
    shape = jax.ShapeDtypeStruct((n_rows, n_cols), F32)
    return pl.pallas_call(
        body, name=name, grid=(steps,), in_specs=[blk] * 4, out_specs=[blk] * 4, out_shape=[shape] * 4,
        scratch_shapes=[pltpu.VMEM((2, tr, n_cols), F32), pltpu.SemaphoreType.DMA((2,)), pltpu.SemaphoreType.DMA((2,)),
                        pltpu.SemaphoreType.REGULAR((2,))],
        compiler_params=_params("arbitrary"))(part, w, m, v)


def _pack(arrays, row_multiple=SUBLANES):
    flat = [jnp.pad(a.reshape(-1), (0, -a.size % LANES)) for a in arrays]
    buf = jnp.concatenate(flat)
    buf = jnp.pad(buf, (0, -buf.size % (row_multiple * LANES)))
    return buf.reshape(-1, LANES)


def _unpack(buf, shapes):
    lead = buf.shape[:-2]
    flat = buf.reshape(lead + (-1,))
    out, pos = [], 0
    for shape in shapes:
        size = 1
        for s in shape:
            size *= s
        out.append(flat[..., pos:pos + size].reshape(lead + tuple(shape)))
        pos += size + (-size % LANES)
    return out


def _pos_embedding(seq_len, dim):
    rows = seq_len // GRID_W
    q = dim // 4
    omega = 1.0 / (10000.0 ** (jnp.arange(q, dtype=F32) / q))
    er = jnp.arange(rows).astype(F32)[:, None] * omega
    ec = jnp.arange(GRID_W).astype(F32)[:, None] * omega
    by_row = jnp.broadcast_to(jnp.concatenate([jnp.sin(er), jnp.cos(er)], axis=-1)[:, None, :], (rows, GRID_W, 2 * q))
    by_col = jnp.broadcast_to(jnp.concatenate([jnp.sin(ec), jnp.cos(ec)], axis=-1)[None, :, :], (rows, GRID_W, 2 * q))
    return jnp.concatenate([by_row, by_col], axis=-1).reshape(rows * GRID_W, dim)


def _cols_from_shards(g):
    return jnp.transpose(g, (1, 0, 2)).reshape(g.shape[1], -1)


def _shards_from_cols(w, n_shards=4):
    k, n = w.shape
    return jnp.transpose(w.reshape(k, n_shards, n // n_shards), (1, 0, 2))


def kernel(x, c, ctx, c_ctx, w_mod, b_mod, g_n1, w_ffn1_up, w_ffn1_down, g_n2, w_in, b_in, w_dw, b_dw, g_ln, b_ln, w_conf_out, w_lru_conv, b_lru_conv, w_rec_gate, b_rec_gate, w_in_gate, b_in_gate, lru_lambda, w_lru_out, w_out, g_n3, w_ffn2_up, w_ffn2_down, g_final, loss_target, m_c_ctx, m_w_mod, m_b_mod, m_g_n1, m_w_ffn1_up, m_w_ffn1_down, m_g_n2, m_w_in, m_b_in, m_w_dw, m_b_dw, m_g_ln, m_b_ln, m_w_conf_out, m_w_lru_conv, m_b_lru_conv, m_w_rec_gate, m_b_rec_gate, m_w_in_gate, m_b_in_gate, m_lru_lambda, m_w_lru_out, m_w_out, m_g_n3, m_w_ffn2_up, m_w_ffn2_down, m_g_final, v_c_ctx, v_w_mod, v_b_mod, v_g_n1, v_w_ffn1_up, v_w_ffn1_down, v_g_n2, v_w_in, v_b_in, v_w_dw, v_b_dw, v_g_ln, v_b_ln, v_w_conf_out, v_w_lru_conv, v_b_lru_conv, v_w_rec_gate, v_b_rec_gate, v_w_in_gate, v_b_in_gate, v_lru_lambda, v_w_lru_out, v_w_out, v_g_n3, v_w_ffn2_up, v_w_ffn2_down, v_g_final):
    given = dict(locals())
    wt = {n: given[n] for n in W_NAMES}
    mom = {n: given["m_" + n] for n in W_NAMES}
    var = {n: given["v_" + n] for n in W_NAMES}

    def flat2(a):
        if a.ndim == 1:
            return a.reshape(1, -1)
        a = a[0]
        return a if a.ndim == 2 else a.reshape(-1, a.shape[-1])

    t_lat, d = x.shape[1], x.shape[2]
    t_ctx = ctx.shape[1]
    rows = Rows(t_lat, t_ctx, d)
    xi, yi, ci = lax.axis_index("x"), lax.axis_index("y"), lax.axis_index("c")
    me, chip = 4 * xi + 2 * yi + ci, 2 * xi + yi
    lru_w = b_lru_conv.shape[-1]
    n_blk = lru_w // LANES
    taps = w_dw.shape[1]
    lru_taps = w_lru_conv.shape[1]
    ffn = w_ffn1_down.shape[1] * 4

    small_shapes = [(1, d)] + [flat2(wt[n]).shape for n in SMALL_SHARDED]
    small_all = exchange("all", [_pack([c] + [flat2(wt[n]) for n in SMALL_SHARDED])], False, "gather_small")[0]
    small_all = _unpack(small_all, small_shapes)
    c_all = small_all[0][:, 0, :]
    full = {n: jnp.concatenate([a[0], a[2], a[4], a[6]], axis=-1) for n, a in zip(SMALL_SHARDED, small_all[1:])}
    big = COL_SHARDED + ROW_SHARDED
    gathered = exchange("chips", [wt[n][0].astype(BF16) for n in big], False, "gather_weights")
    wb = {}
    for n, g in zip(big, gathered):
        wb[n] = _cols_from_shards(g) if n in COL_SHARDED else g.reshape(-1, g.shape[-1])

    c9 = jnp.concatenate([c_all, c_ctx.reshape(1, d), jnp.zeros((7, d), F32)], axis=0)
    mod_cols = mod_matmul(c9, w_mod[0], lax.dynamic_slice_in_dim(b_mod, chip * w_mod.shape[2], w_mod.shape[2], axis=1), "mod_matmul")
    mod_all = exchange("all", [mod_cols], False, "gather_mod")[0]
    mod9 = jnp.concatenate([mod_all[0], mod_all[2], mod_all[4], mod_all[6]], axis=-1)
    mv = jnp.stack([lax.dynamic_index_in_dim(mod9, me, 0, keepdims=False).reshape(N_MOD, d), mod9[8].reshape(N_MOD, d)])

    x_in = jnp.concatenate([x[0], ctx[0]], axis=0)
    pos = jnp.concatenate([_pos_embedding(t_lat, d), jnp.zeros((t_ctx, d), F32)], axis=0)

    def ffn_fwd(tag, h, w_up, w_down):
        gu = matmul(h, w_up, "nn", F32, f"{tag}_up")
        act = swiglu(rows, gu, f"{tag}_act")
        return gu, act, matmul(act, w_down, "nn", F32, f"{tag}_down")

    x0, h1 = norm_mod(rows, x_in, g_n1, mv, 0, 1, "norm1", pos=pos)
    gu1, act1, f1 = ffn_fwd("ffn1", h1, wb["w_ffn1_up"], wb["w_ffn1_down"])
    x1, h2 = norm_mod(rows, x0, g_n2, mv, 3, 4, "norm2", resid=(f1, 2, 0.5))

    conf_w = w_conf_out.shape[1] * 4
    col_groups = [(0, 2 * conf_w), (2 * conf_w, lru_w), (2 * conf_w + lru_w, lru_w), (2 * conf_w + 2 * lru_w, 2 * d)]
    pvg, ux, gl, pgc = [matmul(h2, wb["w_in"][:, s:s + n], "nn", F32, f"in_proj{k}", bias=b_in[:, s:s + n])
                        for k, (s, n) in enumerate(col_groups)]
    u = glu(rows, pvg, "glu")
    u2 = dwconv(rows, u, full["w_dw"], b_dw, taps // 2, F32, "conf_conv")
    s_act = ln_silu(rows, u2, g_ln, b_ln, "conf_ln")
    yc = matmul(s_act, wb["w_conf_out"], "nn", F32, "conf_out")

    xr = dwconv(rows, ux, full["w_lru_conv"], b_lru_conv, LRU_PAD_LEFT, F32, "lru_conv")
    w_rec, w_ing = w_rec_gate[0].astype(BF16), w_in_gate[0].astype(BF16)
    wcat = jnp.concatenate([w_rec[0], w_ing[0], w_rec[1], w_ing[1]], axis=-1)

    def per_block(a):
        return jnp.transpose(a.reshape(2, n_blk, LANES), (1, 0, 2)).reshape(n_blk, 1, 2 * LANES)

    b_rec, b_ing = full["b_rec_gate"].reshape(2, n_blk, LANES), full["b_in_gate"].reshape(2, n_blk, LANES)
    bcat = jnp.concatenate([b_rec[0], b_ing[0], b_rec[1], b_ing[1]], axis=-1).reshape(n_blk, 1, 4 * LANES)
    lam = per_block(full["lru_lambda"])
    a_gate, b_gate = lru_gates(rows, xr, wcat, bcat, lam, "lru_gates")
    h_scan, h_prev = lru_scan(rows, a_gate, b_gate, "lru_scan")
    yl_in = lru_merge(rows, h_scan, gl, "lru_merge")
    yl = matmul(yl_in, wb["w_lru_out"], "nn", F32, "lru_out")
    z = z_merge(rows, pgc, yc, yl, "z_merge")
    y = matmul(z, wb["w_out"], "nn", F32, "mix_out")

    x2, h3 = norm_mod(rows, x1, g_n3, mv, 6, 7, "norm3", resid=(y, 5, 1.0))
    gu2, act2, f2 = ffn_fwd("ffn2", h3, wb["w_ffn2_up"], wb["w_ffn2_down"])
    loss_part, dx3, dg_final = loss_head(rows, x2, f2, mv, g_final.reshape(1, d), loss_target[0], "loss_head")
    loss = lax.psum(loss_part[0, 0], ("x", "y", "c"))

    grads = {"g_final": dg_final}
    dmv = [None] * N_MOD

    def ffn_bwd(tag, dxn, x_prev, h, gu, act, f, w_up, w_down, g, ks):
        k_shift, k_scale, k_gate = ks
        df, dmv[k_gate] = gate_bwd(rows, dxn, f, mv, k_gate, 0.5, f"{tag}_gate_bwd")
        grads[f"w_{tag}_down"] = matmul(act, df, "tn", F32, f"{tag}_down_dw")
        dact = matmul(df, w_down, "nt", F32, f"{tag}_down_dx")
        dgu = swiglu_bwd(rows, gu, dact, f"{tag}_act_bwd")
        grads[f"w_{tag}_up"] = matmul(h, dgu, "tn", F32, f"{tag}_up_dw")
        dh = matmul(dgu, w_up, "nt", F32, f"{tag}_up_dx")
        dx, dg, dmv[k_shift], dmv[k_scale] = norm_mod_bwd(rows, x_prev, dh, dxn, g, mv, k_shift, k_scale, f"{tag}_norm_bwd")
        return dx, dg

    dx2, grads["g_n3"] = ffn_bwd("ffn2", dx3, x2, h3, gu2, act2, f2, wb["w_ffn2_up"], wb["w_ffn2_down"], g_n3, (6, 7, 8))

    dy, dmv[5] = gate_bwd(rows, dx2, y, mv, 5, 1.0, "mix_gate_bwd")
    grads["w_out"] = matmul(z, dy, "tn", F32, "mix_out_dw")
    dz = matmul(dy, wb["w_out"], "nt", F32, "mix_out_dx")
    dyc, dyl, dpgc = z_merge_bwd(rows, pgc, yc, yl, dz, "z_merge_bwd")

    grads["w_conf_out"] = matmul(s_act, dyc, "tn", F32, "conf_out_dw")
    ds_act = matmul(dyc, wb["w_conf_out"], "nt", F32, "conf_out_dx")
    du2, grads["g_ln"], grads["b_ln"] = ln_silu_bwd(rows, u2, ds_act, g_ln, b_ln, "conf_ln_bwd")
    grads["w_dw"], grads["b_dw"] = dwconv_bwd_w(rows, u, du2, taps, taps // 2, "conf_conv_dw")
    du = dwconv(rows, du2, full["w_dw"][::-1], None, taps - 1 - taps // 2, F32, "conf_conv_dx")
    dpvg = glu_bwd(rows, pvg, du, "glu_bwd")

    grads["w_lru_out"] = matmul(yl_in, dyl, "tn", F32, "lru_out_dw")
    dyl_in = matmul(dyl, wb["w_lru_out"], "nt", F32, "lru_out_dx")
    dh_sum, dgl = lru_merge_bwd(rows, h_scan, gl, dyl_in, "lru_merge_bwd")
    da_gate, db_gate = lru_scan_bwd(rows, dh_sum, a_gate, h_prev, "lru_scan_bwd")
    dxr, dwcat, dbcat, dlam = lru_gates_bwd(rows, xr, wcat, bcat, lam, da_gate, db_gate, "lru_gates_bwd")
    grads["w_rec_gate"] = jnp.stack([dwcat[:, :, 0:LANES], dwcat[:, :, 2 * LANES:3 * LANES]])
    grads["w_in_gate"] = jnp.stack([dwcat[:, :, LANES:2 * LANES], dwcat[:, :, 3 * LANES:4 * LANES]])
    dbcat = dbcat.reshape(n_blk, 4, LANES)
    grads["b_rec_gate"] = jnp.stack([dbcat[:, 0], dbcat[:, 2]]).reshape(2, lru_w)
    grads["b_in_gate"] = jnp.stack([dbcat[:, 1], dbcat[:, 3]]).reshape(2, lru_w)
    grads["lru_lambda"] = jnp.transpose(dlam.reshape(n_blk, 2, LANES), (1, 0, 2)).reshape(2, lru_w)
    grads["w_lru_conv"], grads["b_lru_conv"] = dwconv_bwd_w(rows, ux, dxr, lru_taps, LRU_PAD_LEFT, "lru_conv_dw")
    dux = dwconv(rows, dxr, full["w_lru_conv"][::-1], None, lru_taps - 1 - LRU_PAD_LEFT, BF16, "lru_conv_dx")

    dproj = jnp.concatenate([dpvg, dux, dgl, dpgc], axis=1)
    grads["b_in"] = col_sum(rows, dproj, "in_proj_db")
    grads["w_in"] = matmul(h2, dproj, "tn", F32, "in_proj_dw")
    dh2 = matmul(dproj, wb["w_in"], "nt", F32, "in_proj_dx")
    dx1, grads["g_n2"], dmv[3], dmv[4] = norm_mod_bwd(rows, x1, dh2, dx2, g_n2, mv, 3, 4, "norm2_bwd")

    dx0, grads["g_n1"] = ffn_bwd("ffn1", dx1, x0, h1, gu1, act1, f1, wb["w_ffn1_up"], wb["w_ffn1_down"], g_n1, (0, 1, 2))
    grad_x = dx0[:t_lat][None]

    pieces = [_shards_from_cols(grads[n]) if n in COL_SHARDED else grads[n].reshape(4, -1, grads[n].shape[-1]) for n in big]
    from_chips = exchange("chips", [p.astype(BF16) for p in pieces], True, "scatter_grads")
    chip_sums = [sum_parts(p.reshape(4, -1, p.shape[-1]), f"chip_sum_{n}") for n, p in zip(big, from_chips)]
    out = {}
    for n, p in zip(big, chip_sums):
        out[n] = adamw_pair(p, flat2(wt[n]), flat2(mom[n]), flat2(var[n]), f"adamw_{n}")

    dmod = jnp.concatenate(dmv, axis=1)
    small_names = ['g_n1', 'g_n2', 'b_in', 'b_dw', 'g_ln', 'b_ln', 'b_lru_conv', 'w_rec_gate', 'w_in_gate', 'g_n3', 'g_final'] + SMALL_SHARDED
    small_list = [grads[n] for n in small_names] + [dmod[0], dmod[1]]
    small_buf = _pack(small_list, 8 * 2 * SUBLANES)
    small_parts, dmod_all = exchange("all", [small_buf.reshape(8, -1, LANES), _pack([dmod[0]])], [True, False], "scatter_small_grads")
    small_eighth = sum_parts(small_parts, "sum_small_grads")
    small_sum = exchange("all", [small_eighth], False, "gather_small_grads")[0].reshape(small_buf.shape)
    small_sum = _unpack(small_sum, [a.shape for a in small_list])
    total = dict(zip(small_names, small_sum))
    dmod_all = _unpack(dmod_all, [dmod[0].shape])[0].reshape(8, N_MOD * d)
    dmc = small_sum[-1].reshape(1, N_MOD * d)
    total["b_mod"] = small_sum[-2].reshape(1, N_MOD * d) + dmc

    d9 = jnp.concatenate([dmod_all, dmc, jnp.zeros((7, N_MOD * d), F32)], axis=0)
    d9_cols = lax.dynamic_slice_in_dim(d9, chip * w_mod.shape[2], w_mod.shape[2], axis=1)
    g_wmod, dc9 = mod_matmul_bwd(c9, d9_cols, w_mod[0], "mod_matmul_bwd")
    dc_all = exchange("all", [dc9[8:16]], False, "gather_dc")[0]
    total["c_ctx"] = sum_parts(jnp.stack([dc_all[0], dc_all[2], dc_all[4], dc_all[6]]), "sum_dc")[0:1]
    out["w_mod"] = adamw(g_wmod[None], w_mod[0], m_w_mod[0], v_w_mod[0], "adamw_w_mod")

    small_all_names = [n for n in W_NAMES if n not in out]
    g_local, shapes = [], []
    for n in small_all_names:
        g = total[n].reshape(flat2(wt[n]).shape[:-1] + (-1,)) if n in SMALL_SHARDED else total[n].reshape(flat2(wt[n]).shape)
        if n in SMALL_SHARDED:
            width = wt[n].shape[-1]
            g = lax.dynamic_slice_in_dim(g, chip * width, width, axis=1)
        g_local.append(g)
        shapes.append(g.shape)
    packed = adamw(_pack(g_local)[None], _pack([flat2(wt[n]) for n in small_all_names]),
                   _pack([flat2(mom[n]) for n in small_all_names]), _pack([flat2(var[n]) for n in small_all_names]), "adamw_small")
    for k, n in enumerate(small_all_names):
        out[n] = [_unpack(p, shapes)[k] for p in packed]

    results = [loss, grad_x]
    for k in range(4):
        results += [out[n][k].reshape(wt[n].shape) for n in W_NAMES]
    return tuple(results)
```

```python
import functools

import jax
import jax.numpy as jnp
from jax import lax
from jax.experimental import pallas as pl
from jax.experimental.pallas import tpu as pltpu

F32 = jnp.float32
BF16 = jnp.bfloat16
MESH = pl.DeviceIdType.MESH

EPS = 1e-6
GRID_W = 64
N_MOD = 9
LRU_C = 8.0
LRU_PAD_LEFT = 2
ADAM_LR, ADAM_B1, ADAM_B2, ADAM_EPS, ADAM_WD, ADAM_STEP = 0.001, 0.9, 0.999, 1e-08, 0.01, 10

LANES = 128
SUBLANES = 8
HALO = 16
VMEM_LIMIT = 56 * 1024 * 1024

W_NAMES = ['c_ctx', 'w_mod', 'b_mod', 'g_n1', 'w_ffn1_up', 'w_ffn1_down', 'g_n2', 'w_in', 'b_in', 'w_dw', 'b_dw',
           'g_ln', 'b_ln', 'w_conf_out', 'w_lru_conv', 'b_lru_conv', 'w_rec_gate', 'b_rec_gate', 'w_in_gate',
           'b_in_gate', 'lru_lambda', 'w_lru_out', 'w_out', 'g_n3', 'w_ffn2_up', 'w_ffn2_down', 'g_final']
COL_SHARDED = ['w_ffn1_up', 'w_in', 'w_ffn2_up']
ROW_SHARDED = ['w_ffn1_down', 'w_conf_out', 'w_lru_out', 'w_out', 'w_ffn2_down']
SMALL_SHARDED = ['w_dw', 'w_lru_conv', 'b_rec_gate', 'b_in_gate', 'lru_lambda']


def _params(*semantics):
    return pltpu.CompilerParams(dimension_semantics=semantics, vmem_limit_bytes=VMEM_LIMIT)


def _pick(n, target, mult=LANES):
    best = None
    for t in range(mult, min(n, target) + 1, mult):
        if n % t == 0:
            best = t
    return best or n


def _chunks(width, target=512):
    w = _pick(width, target)
    return [slice(s, s + w) for s in range(0, width, w)]


def _sigmoid(x):
    return jax.nn.sigmoid(x)


def _silu_and_grad(x):
    s = _sigmoid(x)
    return x * s, s * (1.0 + x * (1.0 - s))


_GELU_K = 0.7978845608028654


def _gelu_and_grad(x):
    t = jnp.tanh(_GELU_K * (x + 0.044715 * (x * x * x)))
    return 0.5 * x * (1.0 + t), 0.5 * (1.0 + t) + 0.5 * x * (1.0 - t * t) * _GELU_K * (1.0 + 3 * 0.044715 * x * x)


def _neg_expm1(z):
    series = -(z * (1.0 + z * (0.5 + z * (1.0 / 6.0 + z * (1.0 / 24.0)))))
    return jnp.where(z > -0.01, series, 1.0 - jnp.exp(z))


def _softplus(x):
    return jnp.maximum(x, 0.0) + jnp.log1p(jnp.exp(-jnp.abs(x)))


def _accumulate(ref, value, first):
    @pl.when(first)
    def _():
        ref[...] = value

    @pl.when(jnp.logical_not(first))
    def _():
        ref[...] += value


def _colsum(x):
    return jnp.sum(x, axis=0, keepdims=True)


class Rows:
    def __init__(self, t_lat, t_ctx, d_model):
        self.tm = _pick(t_ctx, 256, HALO)
        assert t_lat % self.tm == 0 and t_ctx % self.tm == 0
        self.n_lat = t_lat // self.tm
        self.n_all = (t_lat + t_ctx) // self.tm
        self.t_all = t_lat + t_ctx
        self.d = d_model

    def seg(self, i):
        return jnp.where(i >= self.n_lat, 1, 0)

    def seg_first(self, i):
        return jnp.logical_or(i == 0, i == self.n_lat)

    def seg_last(self, i):
        return jnp.logical_or(i == self.n_lat - 1, i == self.n_all - 1)

    def row(self, width, col=0):
        return pl.BlockSpec((self.tm, width), lambda i: (i, col))

    def vec(self, width):
        return pl.BlockSpec((1, width), lambda i: (0, 0))

    def full(self, shape):
        return pl.BlockSpec(shape, lambda i: (0,) * len(shape))

    def mod(self):
        return pl.BlockSpec((None, N_MOD, self.d), lambda i: (self.seg(i), 0, 0))

    def segsum(self):
        return pl.BlockSpec((None, 1, self.d), lambda i: (self.seg(i), 0, 0))

    def segsum_shape(self):
        return jax.ShapeDtypeStruct((2, 1, self.d), F32)

    def halo(self, width, side):
        per = self.tm // HALO
        last = self.t_all // HALO - 1
        if side < 0:
            return pl.BlockSpec((HALO, width), lambda i: (jnp.maximum(i * per - 1, 0), 0))
        return pl.BlockSpec((HALO, width), lambda i: (jnp.minimum((i + 1) * per, last), 0))

    def call(self, body, name, in_specs, out_specs, out_shape, scratch=()):
        return pl.pallas_call(body, name=name, grid=(self.n_all,), in_specs=in_specs, out_specs=out_specs,
                              out_shape=out_shape, scratch_shapes=list(scratch), compiler_params=_params("arbitrary"))


_PEER_FLIPS = {
    "chips": [(1, 0, 0), (0, 1, 0), (1, 1, 0)],
    "sibling": [(0, 0, 1)],
    "all": [(0, 0, 1), (0, 1, 0), (0, 1, 1), (1, 0, 0), (1, 0, 1), (1, 1, 0), (1, 1, 1)],
}


def _slot(kind, x, y, c):
    return {"chips": 2 * x + y, "sibling": c, "all": 4 * x + 2 * y + c}[kind]


def exchange(kind, srcs, indexed, name):
    flips = _PEER_FLIPS[kind]
    n_slots = len(flips) + 1
    n_t = len(srcs)
    by_slot = list(indexed) if isinstance(indexed, (list, tuple)) else [indexed] * n_t
    out_shapes = [jax.ShapeDtypeStruct(s.shape if ix else (n_slots,) + s.shape, s.dtype) for s, ix in zip(srcs, by_slot)]

    def body(*refs):
        src_refs, dst_refs = refs[:n_t], refs[n_t:2 * n_t]
        send_sems, recv_sems, local_sems = refs[2 * n_t:]
        x, y, c = lax.axis_index("x"), lax.axis_index("y"), lax.axis_index("c")
        me = _slot(kind, x, y, c)

        def piece(t, k):
            return src_refs[t].at[k] if by_slot[t] else src_refs[t]

        peers = [(1 - x if fx else x, 1 - y if fy else y, 1 - c if fc else c) for fx, fy, fc in flips]
        local = [pltpu.make_async_copy(piece(t, me), dst_refs[t].at[me], local_sems.at[t]) for t in range(n_t)]
        for cp in local:
            cp.start()
        sends = []
        for t in range(n_t):
            for p, peer in enumerate(peers):
                sends.append(pltpu.make_async_remote_copy(
                    src_ref=piece(t, _slot(kind, *peer)), dst_ref=dst_refs[t].at[me],
                    send_sem=send_sems.at[t, p], recv_sem=recv_sems.at[t, p], device_id=peer, device_id_type=MESH))
        for cp in sends:
            cp.start()
        for t in range(n_t):
            for p, peer in enumerate(peers):
                theirs = _slot(kind, *peer)
                pltpu.make_async_remote_copy(
                    src_ref=piece(t, me), dst_ref=dst_refs[t].at[theirs],
                    send_sem=send_sems.at[t, p], recv_sem=recv_sems.at[t, p], device_id=peer, device_id_type=MESH).wait_recv()
        for cp in sends:
            cp.wait_send()
        for cp in local:
            cp.wait()

    any_spec = pl.BlockSpec(memory_space=pl.ANY)
    outs = pl.pallas_call(
        body, name=name, out_shape=out_shapes, in_specs=[any_spec] * n_t, out_specs=[any_spec] * n_t,
        scratch_shapes=[pltpu.SemaphoreType.DMA((n_t, len(flips))), pltpu.SemaphoreType.DMA((n_t, len(flips))),
                        pltpu.SemaphoreType.DMA((n_t,))],
        compiler_params=pltpu.CompilerParams(has_side_effects=True),
    )(*srcs)
    return list(outs)


_HBM_SPEC = pl.BlockSpec(memory_space=pltpu.HBM)
_SEM_SPEC = pl.BlockSpec(memory_space=pltpu.SEMAPHORE)
_DATAFLOW = pltpu.SideEffectType.DATAFLOW_SIDE_EFFECTING


def _in_hbm(a):
    return pltpu.with_memory_space_constraint(a, pltpu.HBM)


def exchange_start(kind, srcs, indexed, name):
    flips = _PEER_FLIPS[kind]
    n_slots = len(flips) + 1
    n_t = len(srcs)
    land_shapes = [s.shape if indexed else (n_slots,) + s.shape for s in srcs]

    n_p = len(flips)
    n_sem = n_t * n_p

    def body(*refs):
        src_refs, land_refs = refs[:n_t], refs[n_t:2 * n_t]
        send_sems, recv_sems = refs[4 * n_t:4 * n_t + n_sem], refs[4 * n_t + n_sem:4 * n_t + 2 * n_sem]
        local_sems = refs[-1]
        x, y, c = lax.axis_index("x"), lax.axis_index("y"), lax.axis_index("c")
        me = _slot(kind, x, y, c)

        def piece(t, k):
            return src_refs[t].at[k] if indexed else src_refs[t]

        local = [pltpu.make_async_copy(piece(t, me), land_refs[t].at[me], local_sems.at[t]) for t in range(n_t)]
        for cp in local:
            cp.start()
        for t in range(n_t):
            for p, (fx, fy, fc) in enumerate(flips):
                peer = (1 - x if fx else x, 1 - y if fy else y, 1 - c if fc else c)
                pltpu.make_async_remote_copy(
                    src_ref=piece(t, _slot(kind, *peer)), dst_ref=land_refs[t].at[me],
                    send_sem=send_sems[t * n_p + p], recv_sem=recv_sems[t * n_p + p], device_id=peer, device_id_type=MESH).start()
        for cp in local:
            cp.wait()

    outs = pl.pallas_call(
        body, name=name,
        out_shape=tuple(pltpu.HBM(s.shape, s.dtype) for s in srcs) + tuple(pltpu.HBM(shape, s.dtype) for shape, s in zip(land_shapes, srcs))
        + (pltpu.SemaphoreType.DMA(()),) * (2 * n_sem),
        in_specs=[_HBM_SPEC] * (2 * n_t), out_specs=(_HBM_SPEC,) * (2 * n_t) + (_SEM_SPEC,) * (2 * n_sem),
        input_output_aliases={t: t for t in range(2 * n_t)}, scratch_shapes=[pltpu.SemaphoreType.DMA((n_t,))],
        compiler_params=pltpu.CompilerParams(has_side_effects=_DATAFLOW),
    )(*[_in_hbm(s) for s in srcs], *[_in_hbm(lax.empty(shape, s.dtype)) for shape, s in zip(land_shapes, srcs)])
    return kind, indexed, list(outs[2 * n_t:]), list(outs[:n_t]), list(outs[n_t:2 * n_t])


def exchange_wait(handle, after, name):
    kind, indexed, sems, srcs, lands = handle
    flips = _PEER_FLIPS[kind]
    n_t, n_p = len(srcs), len(flips)
    n_sem = n_t * n_p

    def body(*refs):
        src_refs, land_refs = refs[:n_t], refs[n_t:2 * n_t]
        send_sems, recv_sems = refs[2 * n_t:2 * n_t + n_sem], refs[2 * n_t + n_sem:2 * n_t + 2 * n_sem]
        x, y, c = lax.axis_index("x"), lax.axis_index("y"), lax.axis_index("c")
        for t in range(n_t):
            for p, (fx, fy, fc) in enumerate(flips):
                peer = (1 - x if fx else x, 1 - y if fy else y, 1 - c if fc else c)
                theirs = _slot(kind, *peer)
                copy = pltpu.make_async_remote_copy(
                    src_ref=src_refs[t].at[theirs] if indexed else src_refs[t], dst_ref=land_refs[t].at[theirs],
                    send_sem=send_sems[t * n_p + p], recv_sem=recv_sems[t * n_p + p], device_id=peer, device_id_type=MESH)
                copy.wait_send()
                copy.wait_recv()

    outs = pl.pallas_call(
        body, name=name, out_shape=tuple(pltpu.HBM(a.shape, a.dtype) for a in srcs + lands),
        in_specs=[_HBM_SPEC] * (2 * n_t) + [_SEM_SPEC] * (2 * n_sem) + [pl.BlockSpec(memory_space=pl.ANY)],
        out_specs=(_HBM_SPEC,) * (2 * n_t), input_output_aliases={t: t for t in range(2 * n_t)},
        compiler_params=pltpu.CompilerParams(has_side_effects=_DATAFLOW),
    )(*srcs, *lands, *sems, after)
    return list(outs[n_t:])


def matmul(a, b, mode, out_dtype, name, bias=None):
    if mode == "nn":
        (n_i, n_r), (_, n_j) = a.shape, b.shape
    elif mode == "nt":
        (n_i, n_r), (n_j, _) = a.shape, b.shape
    else:
        (n_r, n_i), (_, n_j) = a.shape, b.shape
    ti = _pick(n_i, 1408 if mode == "tn" else 768)
    tj = _pick(n_j, 1664)
    tr = n_r if (mode == "nn" and n_r <= 2816) else _pick(n_r, 768 if mode == "tn" else 1664)
    steps = n_r // tr
    a_spec = {"nn": pl.BlockSpec((ti, tr), lambda i, j, r: (i, r)), "nt": pl.BlockSpec((ti, tr), lambda i, j, r: (i, r)),
              "tn": pl.BlockSpec((tr, ti), lambda i, j, r: (r, i))}[mode]
    b_spec = {"nn": pl.BlockSpec((tr, tj), lambda i, j, r: (r, j)), "nt": pl.BlockSpec((tj, tr), lambda i, j, r: (j, r)),
              "tn": pl.BlockSpec((tr, tj), lambda i, j, r: (r, j))}[mode]
    dims = {"nn": (((1,), (0,)), ((), ())), "nt": (((1,), (1,)), ((), ())), "tn": (((0,), (0,)), ((), ()))}[mode]

    def body(*refs):
        a_ref, b_ref = refs[:2]
        bias_ref = refs[2] if bias is not None else None
        o_ref = refs[3] if bias is not None else refs[2]
        prod = lax.dot_general(a_ref[...].astype(BF16), b_ref[...].astype(BF16), dims, preferred_element_type=F32)

        def finish(total):
            if bias_ref is not None:
                total = total + bias_ref[...]
            o_ref[...] = total.astype(out_dtype)

        if steps == 1:
            finish(prod)
        else:
            acc = refs[-1]
            r = pl.program_id(2)

            @pl.when(r == 0)
            def _():
                acc[...] = prod

            @pl.when(r > 0)
            def _():
                acc[...] += prod

            @pl.when(r == steps - 1)
            def _():
                finish(acc[...])

    in_specs = [a_spec, b_spec] + ([pl.BlockSpec((1, tj), lambda i, j, r: (0, j))] if bias is not None else [])
    args = (a, b) + ((bias,) if bias is not None else ())
    return pl.pallas_call(
        body, name=name, grid=(n_i // ti, n_j // tj, steps), in_specs=in_specs,
        out_specs=pl.BlockSpec((ti, tj), lambda i, j, r: (i, j)), out_shape=jax.ShapeDtypeStruct((n_i, n_j), out_dtype),
        scratch_shapes=[pltpu.VMEM((ti, tj), F32)] if steps > 1 else [],
        compiler_params=_params("arbitrary", "arbitrary", "arbitrary"),
    )(*args)


def norm_mod(rows, x, g, mv, shift_k, scale_k, name, resid=None, pos=None):
    d = rows.d
    changed = resid is not None or pos is not None

    def body(*refs):
        refs = list(refs)
        x_ref, g_ref, mv_ref = refs[:3]
        rest = refs[3:]
        xv = x_ref[...]
        if pos is not None:
            xv = xv + rest.pop(0)[...]
        if resid is not None:
            xv = xv + resid[2] * mv_ref[resid[1]:resid[1] + 1, :] * rest.pop(0)[...]
        if changed:
            rest.pop(0)[...] = xv
        r = lax.rsqrt(jnp.mean(xv * xv, axis=-1, keepdims=True) + EPS)
        h = (xv * r) * g_ref[...]
        if shift_k is not None:
            h = h * (1.0 + mv_ref[scale_k:scale_k + 1, :]) + mv_ref[shift_k:shift_k + 1, :]
        rest.pop(0)[...] = h.astype(BF16)

    ins = [x, g, mv] + ([pos] if pos is not None else []) + ([resid[0]] if resid is not None else [])
    in_specs = [rows.row(d), rows.vec(d), rows.mod()] + [rows.row(d)] * (len(ins) - 3)
    out_shape = ([jax.ShapeDtypeStruct((rows.t_all, d), F32)] if changed else []) + [jax.ShapeDtypeStruct((rows.t_all, d), BF16)]
    outs = rows.call(body, name, in_specs, [rows.row(d)] * len(out_shape), out_shape)(*ins)
    return (outs[0], outs[1]) if changed else (None, outs[0])


def norm_mod_bwd(rows, x, dh, dxn, g, mv, shift_k, scale_k, name):
    d = rows.d

    def body(x_ref, dh_ref, dxn_ref, g_ref, mv_ref, dx_ref, dg_ref, dsh_ref, dsc_ref):
        i = pl.program_id(0)
        xv, dhv, gv = x_ref[...], dh_ref[...], g_ref[...]
        r = lax.rsqrt(jnp.mean(xv * xv, axis=-1, keepdims=True) + EPS)
        n = xv * r
        dy = dhv * (1.0 + mv_ref[scale_k:scale_k + 1, :])
        dn = dy * gv
        dx_ref[...] = dxn_ref[...] + r * (dn - n * jnp.mean(dn * n, axis=-1, keepdims=True))
        _accumulate(dg_ref, _colsum(dy * n), i == 0)
        _accumulate(dsh_ref, _colsum(dhv), rows.seg_first(i))
        _accumulate(dsc_ref, _colsum(dhv * (n * gv)), rows.seg_first(i))

    return rows.call(
        body, name, [rows.row(d), rows.row(d), rows.row(d), rows.vec(d), rows.mod()],
        [rows.row(d), rows.vec(d), rows.segsum(), rows.segsum()],
        [jax.ShapeDtypeStruct((rows.t_all, d), F32), jax.ShapeDtypeStruct((1, d), F32), rows.segsum_shape(), rows.segsum_shape()],
    )(x, dh, dxn, g, mv)


def gate_bwd(rows, dxn, f, mv, gate_k, scale, name):
    d = rows.d

    def body(dxn_ref, f_ref, mv_ref, df_ref, dgate_ref):
        i = pl.program_id(0)
        dv = dxn_ref[...]
        df_ref[...] = (scale * mv_ref[gate_k:gate_k + 1, :] * dv).astype(BF16)
        _accumulate(dgate_ref, scale * _colsum(dv * f_ref[...]), rows.seg_first(i))

    return rows.call(body, name, [rows.row(d), rows.row(d), rows.mod()], [rows.row(d), rows.segsum()],
                     [jax.ShapeDtypeStruct((rows.t_all, d), BF16), rows.segsum_shape()])(dxn, f, mv)


def swiglu(rows, gu, name):
    f = gu.shape[1] // 2

    def body(g_ref, u_ref, o_ref):
        for cs in _chunks(f):
            o_ref[:, cs] = (_silu_and_grad(g_ref[:, cs])[0] * u_ref[:, cs]).astype(BF16)

    return rows.call(body, name, [rows.row(f, 0), rows.row(f, 1)], rows.row(f), jax.ShapeDtypeStruct((rows.t_all, f), BF16))(gu, gu)


def swiglu_bwd(rows, gu, dact, name):
    f = gu.shape[1] // 2

    def body(g_ref, u_ref, da_ref, o_ref):
        for cs in _chunks(f):
            act, dact_dg = _silu_and_grad(g_ref[:, cs])
            dav = da_ref[:, cs]
            o_ref[:, cs] = (dav * u_ref[:, cs] * dact_dg).astype(BF16)
            o_ref[:, slice(f + cs.start, f + cs.stop)] = (dav * act).astype(BF16)

    return rows.call(body, name, [rows.row(f, 0), rows.row(f, 1), rows.row(f)], rows.row(2 * f),
                     jax.ShapeDtypeStruct((rows.t_all, 2 * f), BF16))(gu, gu, dact)


def glu(rows, pvg, name):
    w = pvg.shape[1] // 2

    def body(v_ref, t_ref, o_ref):
        for cs in _chunks(w):
            o_ref[:, cs] = v_ref[:, cs] * _sigmoid(t_ref[:, cs])

    return rows.call(body, name, [rows.row(w, 0), rows.row(w, 1)], rows.row(w), jax.ShapeDtypeStruct((rows.t_all, w), F32))(pvg, pvg)


def glu_bwd(rows, pvg, du, name):
    w = pvg.shape[1] // 2

    def body(v_ref, t_ref, du_ref, o_ref):
        for cs in _chunks(w):
            s = _sigmoid(t_ref[:, cs])
            duv = du_ref[:, cs]
            o_ref[:, cs] = (duv * s).astype(BF16)
            o_ref[:, slice(w + cs.start, w + cs.stop)] = (duv * v_ref[:, cs] * s * (1.0 - s)).astype(BF16)

    return rows.call(body, name, [rows.row(w, 0), rows.row(w, 1), rows.row(w)], rows.row(2 * w),
                     jax.ShapeDtypeStruct((rows.t_all, 2 * w), BF16))(pvg, pvg, du)


def ln_silu(rows, u, g, b, name):
    w = u.shape[1]

    def body(u_ref, g_ref, b_ref, o_ref):
        uv = u_ref[...]
        xc = uv - jnp.mean(uv, axis=-1, keepdims=True)
        n = xc * lax.rsqrt(jnp.mean(xc * xc, axis=-1, keepdims=True) + EPS)
        o_ref[...] = _silu_and_grad(n * g_ref[...] + b_ref[...])[0].astype(BF16)

    return rows.call(body, name, [rows.row(w), rows.vec(w), rows.vec(w)], rows.row(w),
                     jax.ShapeDtypeStruct((rows.t_all, w), BF16))(u, g, b)


def ln_silu_bwd(rows, u, ds, g, b, name):
    w = u.shape[1]

    def body(u_ref, ds_ref, g_ref, b_ref, du_ref, dg_ref, db_ref):
        i = pl.program_id(0)
        uv, gv = u_ref[...], g_ref[...]
        xc = uv - jnp.mean(uv, axis=-1, keepdims=True)
        r = lax.rsqrt(jnp.mean(xc * xc, axis=-1, keepdims=True) + EPS)
        n = xc * r
        dy = ds_ref[...] * _silu_and_grad(n * gv + b_ref[...])[1]
        dn = dy * gv
        du_ref[...] = r * (dn - jnp.mean(dn, axis=-1, keepdims=True) - n * jnp.mean(dn * n, axis=-1, keepdims=True))
        _accumulate(dg_ref, _colsum(dy * n), i == 0)
        _accumulate(db_ref, _colsum(dy), i == 0)

    return rows.call(body, name, [rows.row(w), rows.row(w), rows.vec(w), rows.vec(w)], [rows.row(w), rows.vec(w), rows.vec(w)],
                     [jax.ShapeDtypeStruct((rows.t_all, w), F32), jax.ShapeDtypeStruct((1, w), F32),
                      jax.ShapeDtypeStruct((1, w), F32)])(u, ds, g, b)


def lru_merge(rows, h, gl, name):
    w = gl.shape[1]

    def body(h_ref, gl_ref, o_ref):
        for cs in _chunks(w):
            o_ref[:, cs] = ((h_ref[0, :, cs] + h_ref[1, :, cs]) * _gelu_and_grad(gl_ref[:, cs])[0]).astype(BF16)

    return rows.call(body, name, [pl.BlockSpec((2, rows.tm, w), lambda i: (0, i, 0)), rows.row(w)], rows.row(w),
                     jax.ShapeDtypeStruct((rows.t_all, w), BF16))(h, gl)


def lru_merge_bwd(rows, h, gl, dy, name):
    w = gl.shape[1]

    def body(h_ref, gl_ref, dy_ref, dh_ref, dgl_ref):
        for cs in _chunks(w):
            act, dact = _gelu_and_grad(gl_ref[:, cs])
            dyv = dy_ref[:, cs]
            dh_ref[:, cs] = dyv * act
            dgl_ref[:, cs] = (dyv * (h_ref[0, :, cs] + h_ref[1, :, cs]) * dact).astype(BF16)

    return rows.call(body, name, [pl.BlockSpec((2, rows.tm, w), lambda i: (0, i, 0)), rows.row(w), rows.row(w)],
                     [rows.row(w), rows.row(w)],
                     [jax.ShapeDtypeStruct((rows.t_all, w), F32), jax.ShapeDtypeStruct((rows.t_all, w), BF16)])(h, gl, dy)


def z_merge(rows, pgc, yc, yl, name):
    d = rows.d

    def body(gc_ref, gr_ref, yc_ref, yl_ref, o_ref):
        for cs in _chunks(d):
            o_ref[:, cs] = (_sigmoid(gc_ref[:, cs]) * yc_ref[:, cs] + _sigmoid(gr_ref[:, cs]) * yl_ref[:, cs]).astype(BF16)

    return rows.call(body, name, [rows.row(d, 0), rows.row(d, 1), rows.row(d), rows.row(d)], rows.row(d),
                     jax.ShapeDtypeStruct((rows.t_all, d), BF16))(pgc, pgc, yc, yl)


def z_merge_bwd(rows, pgc, yc, yl, dz, name):
    d = rows.d

    def body(gc_ref, gr_ref, yc_ref, yl_ref, dz_ref, dyc_ref, dyl_ref, dp_ref):
        for cs in _chunks(d):
            sc, sr, dzv = _sigmoid(gc_ref[:, cs]), _sigmoid(gr_ref[:, cs]), dz_ref[:, cs]
            dyc_ref[:, cs] = (dzv * sc).astype(BF16)
            dyl_ref[:, cs] = (dzv * sr).astype(BF16)
            dp_ref[:, cs] = (dzv * yc_ref[:, cs] * sc * (1.0 - sc)).astype(BF16)
            dp_ref[:, slice(d + cs.start, d + cs.stop)] = (dzv * yl_ref[:, cs] * sr * (1.0 - sr)).astype(BF16)

    return rows.call(body, name, [rows.row(d, 0), rows.row(d, 1), rows.row(d), rows.row(d), rows.row(d)],
                     [rows.row(d), rows.row(d), rows.row(2 * d)],
                     [jax.ShapeDtypeStruct((rows.t_all, d), BF16), jax.ShapeDtypeStruct((rows.t_all, d), BF16),
                      jax.ShapeDtypeStruct((rows.t_all, 2 * d), BF16)])(pgc, pgc, yc, yl, dz)


def col_sum(rows, x, name):
    w = x.shape[1]

    def body(x_ref, o_ref):
        i = pl.program_id(0)
        for cs in _chunks(w):
            _accumulate(o_ref.at[:, cs], _colsum(x_ref[:, cs].astype(F32)), i == 0)

    return rows.call(body, name, [rows.row(w)], rows.vec(w), jax.ShapeDtypeStruct((1, w), F32))(x)


def loss_head(rows, x, f, mv, g, target, name):
    d = rows.d

    def body(x_ref, f_ref, mv_ref, g_ref, t_ref, loss_ref, dx_ref, dg_ref):
        i = pl.program_id(0)
        valid = jnp.where(i < rows.n_lat, 1.0, 0.0)
        xv = x_ref[...] + 0.5 * mv_ref[8:9, :] * f_ref[...]
        gv = g_ref[...]
        r = lax.rsqrt(jnp.mean(xv * xv, axis=-1, keepdims=True) + EPS)
        n = xv * r
        err = (n * gv - t_ref[...]) * valid
        part = 0.5 * jnp.sum(jnp.mean(err * err, axis=-1, keepdims=True), axis=0, keepdims=True)
        _accumulate(loss_ref, jnp.broadcast_to(part, (1, LANES)), i == 0)
        dy = err * (1.0 / d)
        dn = dy * gv
        dx_ref[...] = r * (dn - n * jnp.mean(dn * n, axis=-1, keepdims=True))
        _accumulate(dg_ref, _colsum(dy * n), i == 0)

    target_spec = pl.BlockSpec((rows.tm, d), lambda i: (jnp.minimum(i, rows.n_lat - 1), 0))
    return rows.call(body, name, [rows.row(d), rows.row(d), rows.mod(), rows.vec(d), target_spec],
                     [rows.vec(LANES), rows.row(d), rows.vec(d)],
                     [jax.ShapeDtypeStruct((1, LANES), F32), jax.ShapeDtypeStruct((rows.t_all, d), F32),
                      jax.ShapeDtypeStruct((1, d), F32)])(x, f, mv, g, target)


def _extended(rows, i, prev_ref, cur_ref, next_ref, cs):
    prev = jnp.where(rows.seg_first(i), 0.0, prev_ref[:, cs].astype(F32))
    nxt = jnp.where(rows.seg_last(i), 0.0, next_ref[:, cs].astype(F32))
    return jnp.concatenate([prev, cur_ref[:, cs].astype(F32), nxt], axis=0)


def _shifted(ext, offset, tm):
    n = ext.shape[0]
    return pltpu.roll(ext, (-offset) % n, 0)[HALO:HALO + tm] if offset else ext[HALO:HALO + tm]


def _pad_taps(w):
    k = w.shape[0]
    return jnp.pad(w, ((0, -k % SUBLANES), (0, 0)))


def dwconv(rows, u, w, b, pad_left, out_dtype, name):
    taps, width = w.shape
    wp = _pad_taps(w)

    def body(*refs):
        prev_ref, cur_ref, next_ref, w_ref = refs[:4]
        b_ref = refs[4] if b is not None else None
        o_ref = refs[-1]
        i = pl.program_id(0)
        for cs in _chunks(width, LANES):
            ext = _extended(rows, i, prev_ref, cur_ref, next_ref, cs)
            acc = jnp.zeros((rows.tm, LANES), F32) if b_ref is None else jnp.broadcast_to(b_ref[:, cs], (rows.tm, LANES))
            for k in range(taps):
                acc = acc + w_ref[k:k + 1, cs] * _shifted(ext, k - pad_left, rows.tm)
            o_ref[:, cs] = acc.astype(out_dtype)

    ins = [u, u, u, wp] + ([b] if b is not None else [])
    in_specs = [rows.halo(width, -1), rows.row(width), rows.halo(width, 1), rows.full(wp.shape)] + ([rows.vec(width)] if b is not None else [])
    return rows.call(body, name, in_specs, rows.row(width), jax.ShapeDtypeStruct((rows.t_all, width), out_dtype))(*ins)


def dwconv_bwd_w(rows, u, dy, taps, pad_left, name):
    width = u.shape[1]
    taps_p = taps + (-taps % SUBLANES)

    def body(prev_ref, cur_ref, next_ref, dy_ref, dw_ref, db_ref):
        i = pl.program_id(0)
        tap_row = lax.broadcasted_iota(jnp.int32, (taps_p, LANES), 0)
        for cs in _chunks(width, LANES):
            ext = _extended(rows, i, prev_ref, cur_ref, next_ref, cs)
            dyv = dy_ref[:, cs]
            total = jnp.zeros((taps_p, LANES), F32)
            for k in range(taps):
                total = total + jnp.where(tap_row == k, _colsum(dyv * _shifted(ext, k - pad_left, rows.tm)), 0.0)
            _accumulate(dw_ref.at[:, cs], total, i == 0)
            _accumulate(db_ref.at[:, cs], _colsum(dyv), i == 0)

    dw, db = rows.call(body, name, [rows.halo(width, -1), rows.row(width), rows.halo(width, 1), rows.row(width)],
                       [rows.full((taps_p, width)), rows.vec(width)],
                       [jax.ShapeDtypeStruct((taps_p, width), F32), jax.ShapeDtypeStruct((1, width), F32)])(u, u, u, dy)
    return dw[:taps], db


def _gate_values(xr, pre, lam, d):
    r = _sigmoid(pre[:, (2 * d) * LANES:(2 * d + 1) * LANES])
    ig = _sigmoid(pre[:, (2 * d + 1) * LANES:(2 * d + 2) * LANES])
    sp = _softplus(-lam[:, d * LANES:(d + 1) * LANES])
    log_a = -LRU_C * r * sp
    return r, ig, sp, jnp.exp(log_a), jnp.sqrt(_neg_expm1(2.0 * log_a))


def lru_gates(rows, xr, wcat, bcat, lam, name):
    n_blk = wcat.shape[0]
    width = xr.shape[1]

    def body(xr_ref, w_ref, b_ref, lam_ref, a_ref, bb_ref):
        for h in range(n_blk):
            cs = slice(h * LANES, (h + 1) * LANES)
            xv = xr_ref[:, cs]
            pre = jnp.dot(xv.astype(BF16), w_ref[h], preferred_element_type=F32) + b_ref[h]
            for d in range(2):
                _, ig, _, a, q = _gate_values(xv, pre, lam_ref[h], d)
                a_ref[d, :, cs] = a
                bb_ref[d, :, cs] = q * (ig * xv)

    both = pl.BlockSpec((2, rows.tm, width), lambda i: (0, i, 0))
    shape = jax.ShapeDtypeStruct((2, rows.t_all, width), F32)
    return rows.call(body, name, [rows.row(width), rows.full(wcat.shape), rows.full(bcat.shape), rows.full(lam.shape)],
                     [both, both], [shape, shape])(xr, wcat, bcat, lam)


def lru_gates_bwd(rows, xr, wcat, bcat, lam, da, dbb, name):
    n_blk = wcat.shape[0]
    width = xr.shape[1]

    def body(xr_ref, w_ref, b_ref, lam_ref, da_ref, dbb_ref, dxr_ref, dw_ref, db_ref, dlam_ref):
        i = pl.program_id(0)
        for h in range(n_blk):
            cs = slice(h * LANES, (h + 1) * LANES)
            xv = xr_ref[:, cs]
            xb = xv.astype(BF16)
            wv = w_ref[h]
            pre = jnp.dot(xb, wv, preferred_element_type=F32) + b_ref[h]
            dxr = jnp.zeros_like(xv)
            dpre, dlam = [], []
            for d in range(2):
                r, ig, sp, a, q = _gate_values(xv, pre, lam_ref[h], d)
                dav, dbv = da_ref[d, :, cs], dbb_ref[d, :, cs]
                dlog_a = dav * a - dbv * (ig * xv) * (a * a) / q
                dpre.append(dlog_a * (-LRU_C * sp) * r * (1.0 - r))
                dpre.append(dbv * q * xv * ig * (1.0 - ig))
                dxr = dxr + dbv * q * ig
                dlam.append(_colsum(dlog_a * (-LRU_C * r)) * (-_sigmoid(-lam_ref[h][:, d * LANES:(d + 1) * LANES])))
            dpre = jnp.concatenate(dpre, axis=1)
            dpb = dpre.astype(BF16)
            dxr_ref[:, cs] = dxr + lax.dot_general(dpb, wv, (((1,), (1,)), ((), ())), preferred_element_type=F32)
            _accumulate(dw_ref.at[h], lax.dot_general(xb, dpb, (((0,), (0,)), ((), ())), preferred_element_type=F32), i == 0)
            _accumulate(db_ref.at[h], _colsum(dpre), i == 0)
            _accumulate(dlam_ref.at[h], jnp.concatenate(dlam, axis=1), i == 0)

    both = pl.BlockSpec((2, rows.tm, width), lambda i: (0, i, 0))
    return rows.call(
        body, name, [rows.row(width), rows.full(wcat.shape), rows.full(bcat.shape), rows.full(lam.shape), both, both],
        [rows.row(width), rows.full(wcat.shape), rows.full(bcat.shape), rows.full(lam.shape)],
        [jax.ShapeDtypeStruct((rows.t_all, width), F32), jax.ShapeDtypeStruct(wcat.shape, F32),
         jax.ShapeDtypeStruct(bcat.shape, F32), jax.ShapeDtypeStruct(lam.shape, F32)])(xr, wcat, bcat, lam, da, dbb)


def _tile_scan(a, b, reverse):
    n = a.shape[0]
    row = lax.broadcasted_iota(jnp.int32, a.shape, 0)
    k = 1
    while k < n:
        ok = (row < n - k) if reverse else (row >= k)
        shift = n - k if reverse else k
        b = b + a * jnp.where(ok, pltpu.roll(b, shift, 0), 0.0)
        a = a * jnp.where(ok, pltpu.roll(a, shift, 0), 1.0)
        k *= 2
    return a, b


def _neighbour(v, edge, reverse):
    n = v.shape[0]
    row = lax.broadcasted_iota(jnp.int32, v.shape, 0)
    if reverse:
        return jnp.where(row < n - 1, pltpu.roll(v, n - 1, 0), edge)
    return jnp.where(row >= 1, pltpu.roll(v, 1, 0), edge)


def _scan_call(rows, body, name, ins, in_specs, n_out, width, adjoint):
    n_all, n_lat = rows.n_all, rows.n_lat

    def tile(d, s):
        s = n_all - 1 - s if adjoint else s
        return jnp.where(d == 0, (s + n_lat) % n_all, n_all - 1 - s)

    def per_dir(d, s):
        return (d, tile(d, s), 0)

    specs = [pl.BlockSpec((None, rows.tm, width), per_dir) if kind == "dir" else
             pl.BlockSpec((rows.tm, width), lambda d, s: (tile(d, s), 0)) for kind in in_specs]
    shape = jax.ShapeDtypeStruct((2, rows.t_all, width), F32)
    return pl.pallas_call(
        body, name=name, grid=(2, n_all), in_specs=specs, out_specs=[pl.BlockSpec((None, rows.tm, width), per_dir)] * n_out,
        out_shape=[shape] * n_out, scratch_shapes=[pltpu.VMEM((SUBLANES, width), F32)],
        compiler_params=_params("arbitrary", "arbitrary"))(*ins)


def lru_scan(rows, a, bb, name):
    width = a.shape[2]
    n = rows.tm

    def body(a_ref, bb_ref, h_ref, hp_ref, carry):
        d, s = pl.program_id(0), pl.program_id(1)

        @pl.when(s == 0)
        def _():
            carry[...] = jnp.zeros_like(carry)

        def run(reverse):
            for cs in _chunks(width, LANES):
                cum, h0 = _tile_scan(a_ref[:, cs], bb_ref[:, cs], reverse)
                h_in = carry[0:1, cs]
                h = h0 + cum * h_in
                h_ref[:, cs] = h
                hp_ref[:, cs] = _neighbour(h, h_in, reverse)
                carry[0:1, cs] = h[0:1] if reverse else h[n - 1:n]

        pl.when(d == 0)(lambda: run(False))
        pl.when(d == 1)(lambda: run(True))

    return _scan_call(rows, body, name, [a, bb], ["dir", "dir"], 2, width, adjoint=False)


def lru_scan_bwd(rows, dh, a, hp, name):
    width = a.shape[2]
    n = rows.tm

    def body(dh_ref, a_ref, hp_ref, da_ref, dbb_ref, carry):
        d, s = pl.program_id(0), pl.program_id(1)

        @pl.when(s == 0)
        def _():
            carry[...] = jnp.zeros_like(carry)

        def run(reverse):
            for cs in _chunks(width, LANES):
                av = a_ref[:, cs]
                cum, g0 = _tile_scan(_neighbour(av, 1.0, reverse), dh_ref[:, cs], reverse)
                g = g0 + cum * carry[0:1, cs]
                da_ref[:, cs] = g * hp_ref[:, cs]
                dbb_ref[:, cs] = g
                carry[0:1, cs] = (av * g)[0:1] if reverse else (av * g)[n - 1:n]

        pl.when(d == 0)(lambda: run(True))
        pl.when(d == 1)(lambda: run(False))

    return _scan_call(rows, body, name, [dh, a, hp], ["shared", "dir", "dir"], 2, width, adjoint=True)


def mod_matmul(c9, w_shard, b_shard, name):
    n = w_shard.shape[1]
    tn = _pick(n, 768)

    def body(c_ref, w_ref, b_ref, o_ref):
        act = _silu_and_grad(c_ref[...])[0]
        o_ref[...] = jnp.dot(act, w_ref[...], preferred_element_type=F32, precision=lax.Precision.HIGHEST) + b_ref[...]

    return pl.pallas_call(
        body, name=name, grid=(n // tn,),
        in_specs=[pl.BlockSpec(c9.shape, lambda j: (0, 0)), pl.BlockSpec((w_shard.shape[0], tn), lambda j: (0, j)),
                  pl.BlockSpec((1, tn), lambda j: (0, j))],
        out_specs=pl.BlockSpec((c9.shape[0], tn), lambda j: (0, j)), out_shape=jax.ShapeDtypeStruct((c9.shape[0], n), F32),
        compiler_params=_params("arbitrary"))(c9, w_shard, b_shard)


def mod_matmul_bwd(c9, d9, w_shard, name):
    n = w_shard.shape[1]
    tn = _pick(n, 768)
    steps = n // tn

    def body(c_ref, d_ref, w_ref, gw_ref, gc_ref):
        j = pl.program_id(0)
        act, dact = _silu_and_grad(c_ref[...])
        dv = d_ref[...]
        gw_ref[...] = lax.dot_general(act, dv, (((0,), (0,)), ((), ())), preferred_element_type=F32, precision=lax.Precision.HIGHEST)
        part = lax.dot_general(dv, w_ref[...], (((1,), (1,)), ((), ())), preferred_element_type=F32, precision=lax.Precision.HIGHEST)
        _accumulate(gc_ref, part * dact, j == 0)

    return pl.pallas_call(
        body, name=name, grid=(steps,),
        in_specs=[pl.BlockSpec(c9.shape, lambda j: (0, 0)), pl.BlockSpec((c9.shape[0], tn), lambda j: (0, j)),
                  pl.BlockSpec((w_shard.shape[0], tn), lambda j: (0, j))],
        out_specs=[pl.BlockSpec((w_shard.shape[0], tn), lambda j: (0, j)), pl.BlockSpec(c9.shape, lambda j: (0, 0))],
        out_shape=[jax.ShapeDtypeStruct(w_shard.shape, F32), jax.ShapeDtypeStruct(c9.shape, F32)],
        compiler_params=_params("arbitrary"))(c9, d9, w_shard)


def _row_tile(n_rows, n_cols):
    return _pick(n_rows, max(2 * SUBLANES, (256 * 1024) // n_cols), 2 * SUBLANES)


def sum_parts(parts, name):
    n, n_rows, n_cols = parts.shape
    tr = _row_tile(n_rows, n_cols)

    def body(p_ref, o_ref):
        total = p_ref[0].astype(F32)
        for k in range(1, n):
            total = total + p_ref[k].astype(F32)
        o_ref[...] = total

    return pl.pallas_call(
        body, name=name, grid=(n_rows // tr,), in_specs=[pl.BlockSpec((n, tr, n_cols), lambda i: (0, i, 0))],
        out_specs=pl.BlockSpec((tr, n_cols), lambda i: (i, 0)), out_shape=jax.ShapeDtypeStruct((n_rows, n_cols), F32),
        compiler_params=_params("arbitrary"))(parts)


def _adamw_update(g, w_ref, m_ref, v_ref, g_ref, d_ref, m2_ref, v2_ref):
    m2 = ADAM_B1 * m_ref[...] + (1.0 - ADAM_B1) * g
    v2 = ADAM_B2 * v_ref[...] + (1.0 - ADAM_B2) * (g * g)
    m_hat = m2 / (1.0 - ADAM_B1 ** ADAM_STEP)
    v_hat = v2 / (1.0 - ADAM_B2 ** ADAM_STEP)
    g_ref[...] = g
    d_ref[...] = -ADAM_LR * (m_hat / (jnp.sqrt(v_hat) + ADAM_EPS) + ADAM_WD * w_ref[...])
    m2_ref[...] = m2
    v2_ref[...] = v2


def adamw(parts, w, m, v, name):
    n, n_rows, n_cols = parts.shape
    tr = _row_tile(n_rows, n_cols)

    def body(p_ref, *refs):
        g = p_ref[0].astype(F32)
        for k in range(1, n):
            g = g + p_ref[k].astype(F32)
        _adamw_update(g, *refs)

    blk = pl.BlockSpec((tr, n_cols), lambda i: (i, 0))
    shape = jax.ShapeDtypeStruct((n_rows, n_cols), F32)
    return pl.pallas_call(
        body, name=name, grid=(n_rows // tr,), in_specs=[pl.BlockSpec((n, tr, n_cols), lambda i: (0, i, 0)), blk, blk, blk],
        out_specs=[blk] * 4, out_shape=[shape] * 4, compiler_params=_params("arbitrary"))(parts, w, m, v)


def adamw_pair(part, w, m, v, name):
    n_rows, n_cols = part.shape
    tr = _row_tile(n_rows, n_cols)
    steps = n_rows // tr

    def body(p_ref, w_ref, m_ref, v_ref, g_ref, d_ref, m2_ref, v2_ref, inbox, send_sems, recv_sems, credits):
        i = pl.program_id(0)
        slot = i % 2
        sibling = (lax.axis_index("x"), lax.axis_index("y"), 1 - lax.axis_index("c"))

        @pl.when(i >= 2)
        def _():
            pl.semaphore_wait(credits.at[slot], 1)

        copy = pltpu.make_async_remote_copy(src_ref=p_ref, dst_ref=inbox.at[slot], send_sem=send_sems.at[slot],
                                            recv_sem=recv_sems.at[slot], device_id=sibling, device_id_type=MESH)
        copy.start()
        copy.wait_recv()
        copy.wait_send()
        _adamw_update(p_ref[...] + inbox[slot], w_ref, m_ref, v_ref, g_ref, d_ref, m2_ref, v2_ref)

        @pl.when(i + 2 < steps)
        def _():
            pl.semaphore_signal(credits.at[slot], inc=1, device_id=sibling, device_id_type=MESH)

    blk = pl.BlockSpec((tr, n_cols), lambda i: (i, 0))
    shape = jax.ShapeDtypeStruct((n_rows, n_cols), F32)
    return pl.pallas_call(
        body, name=name, grid=(steps,), in_specs=[blk] * 4, out_specs=[blk] * 4, out_shape=[shape] * 4,
        scratch_shapes=[pltpu.VMEM((2, tr, n_cols), F32), pltpu.SemaphoreType.DMA((2,)), pltpu.SemaphoreType.DMA((2,)),
                        pltpu.SemaphoreType.REGULAR((2,))],
        compiler_params=_params("arbitrary"))(part, w, m, v)


def _pack(arrays, row_multiple=SUBLANES):
    flat = [jnp.pad(a.reshape(-1), (0, -a.size % LANES)) for a in arrays]
    buf = jnp.concatenate(flat)
    buf = jnp.pad(buf, (0, -buf.size % (row_multiple * LANES)))
    return buf.reshape(-1, LANES)


def _unpack(buf, shapes):
    lead = buf.shape[:-2]
    flat = buf.reshape(lead + (-1,))
    out, pos = [], 0
    for shape in shapes:
        size = 1
        for s in shape:
            size *= s
        out.append(flat[..., pos:pos + size].reshape(lead + tuple(shape)))
        pos += size + (-size % LANES)
    return out


def _pos_embedding(seq_len, dim):
    rows = seq_len // GRID_W
    q = dim // 4
    omega = 1.0 / (10000.0 ** (jnp.arange(q, dtype=F32) / q))
    er = jnp.arange(rows).astype(F32)[:, None] * omega
    ec = jnp.arange(GRID_W).astype(F32)[:, None] * omega
    by_row = jnp.broadcast_to(jnp.concatenate([jnp.sin(er), jnp.cos(er)], axis=-1)[:, None, :], (rows, GRID_W, 2 * q))
    by_col = jnp.broadcast_to(jnp.concatenate([jnp.sin(ec), jnp.cos(ec)], axis=-1)[None, :, :], (rows, GRID_W, 2 * q))
    return jnp.concatenate([by_row, by_col], axis=-1).reshape(rows * GRID_W, dim)


def _cols_from_shards(g):
    return jnp.transpose(g, (1, 0, 2)).reshape(g.shape[1], -1)


def _shards_from_cols(w, n_shards=4):
    k, n = w.shape
    return jnp.transpose(w.reshape(k, n_shards, n // n_shards), (1, 0, 2))


def kernel(x, c, ctx, c_ctx, w_mod, b_mod, g_n1, w_ffn1_up, w_ffn1_down, g_n2, w_in, b_in, w_dw, b_dw, g_ln, b_ln, w_conf_out, w_lru_conv, b_lru_conv, w_rec_gate, b_rec_gate, w_in_gate, b_in_gate, lru_lambda, w_lru_out, w_out, g_n3, w_ffn2_up, w_ffn2_down, g_final, loss_target, m_c_ctx, m_w_mod, m_b_mod, m_g_n1, m_w_ffn1_up, m_w_ffn1_down, m_g_n2, m_w_in, m_b_in, m_w_dw, m_b_dw, m_g_ln, m_b_ln, m_w_conf_out, m_w_lru_conv, m_b_lru_conv, m_w_rec_gate, m_b_rec_gate, m_w_in_gate, m_b_in_gate, m_lru_lambda, m_w_lru_out, m_w_out, m_g_n3, m_w_ffn2_up, m_w_ffn2_down, m_g_final, v_c_ctx, v_w_mod, v_b_mod, v_g_n1, v_w_ffn1_up, v_w_ffn1_down, v_g_n2, v_w_in, v_b_in, v_w_dw, v_b_dw, v_g_ln, v_b_ln, v_w_conf_out, v_w_lru_conv, v_b_lru_conv, v_w_rec_gate, v_b_rec_gate, v_w_in_gate, v_b_in_gate, v_lru_lambda, v_w_lru_out, v_w_out, v_g_n3, v_w_ffn2_up, v_w_ffn2_down, v_g_final):
    given = dict(locals())
    wt = {n: given[n] for n in W_NAMES}
    mom = {n: given["m_" + n] for n in W_NAMES}
    var = {n: given["v_" + n] for n in W_NAMES}

    def flat2(a):
        if a.ndim == 1:
            return a.reshape(1, -1)
        a = a[0]
        return a if a.ndim == 2 else a.reshape(-1, a.shape[-1])

    t_lat, d = x.shape[1], x.shape[2]
    t_ctx = ctx.shape[1]
    rows = Rows(t_lat, t_ctx, d)
    xi, yi, ci = lax.axis_index("x"), lax.axis_index("y"), lax.axis_index("c")
    me, chip = 4 * xi + 2 * yi + ci, 2 * xi + yi
    lru_w = b_lru_conv.shape[-1]
    n_blk = lru_w // LANES
    taps = w_dw.shape[1]
    lru_taps = w_lru_conv.shape[1]
    ffn = w_ffn1_down.shape[1] * 4

    small_shapes = [(1, d)] + [flat2(wt[n]).shape for n in SMALL_SHARDED]
    small_all = exchange("all", [_pack([c] + [flat2(wt[n]) for n in SMALL_SHARDED])], False, "gather_small")[0]
    small_all = _unpack(small_all, small_shapes)
    c_all = small_all[0][:, 0, :]
    full = {n: jnp.concatenate([a[0], a[2], a[4], a[6]], axis=-1) for n, a in zip(SMALL_SHARDED, small_all[1:])}
    big = COL_SHARDED + ROW_SHARDED
    first_used = ['w_ffn1_up', 'w_ffn1_down']
    used_later = [n for n in big if n not in first_used]
    gather_first = exchange_start("chips", [wt[n][0].astype(BF16) for n in first_used], False, "gather_weights_first_start")
    gather_later = exchange_start("chips", [wt[n][0].astype(BF16) for n in used_later], False, "gather_weights_later_start")
    wb = {}

    def take_weights(names, gathered):
        for n, g in zip(names, gathered):
            wb[n] = _cols_from_shards(g) if n in COL_SHARDED else g.reshape(-1, g.shape[-1])

    c9 = jnp.concatenate([c_all, c_ctx.reshape(1, d), jnp.zeros((7, d), F32)], axis=0)
    mod_cols = mod_matmul(c9, w_mod[0], lax.dynamic_slice_in_dim(b_mod, chip * w_mod.shape[2], w_mod.shape[2], axis=1), "mod_matmul")
    mod_all = exchange("all", [mod_cols], False, "gather_mod")[0]
    mod9 = jnp.concatenate([mod_all[0], mod_all[2], mod_all[4], mod_all[6]], axis=-1)
    mv = jnp.stack([lax.dynamic_index_in_dim(mod9, me, 0, keepdims=False).reshape(N_MOD, d), mod9[8].reshape(N_MOD, d)])

    x_in = jnp.concatenate([x[0], ctx[0]], axis=0)
    pos = jnp.concatenate([_pos_embedding(t_lat, d), jnp.zeros((t_ctx, d), F32)], axis=0)

    def ffn_fwd(tag, h, w_up, w_down):
        gu = matmul(h, w_up, "nn", F32, f"{tag}_up")
        act = swiglu(rows, gu, f"{tag}_act")
        return gu, act, matmul(act, w_down, "nn", F32, f"{tag}_down")

    x0, h1 = norm_mod(rows, x_in, g_n1, mv, 0, 1, "norm1", pos=pos)
    take_weights(first_used, exchange_wait(gather_first, h1, "gather_weights_first_wait"))
    gu1, act1, f1 = ffn_fwd("ffn1", h1, wb["w_ffn1_up"], wb["w_ffn1_down"])
    take_weights(used_later, exchange_wait(gather_later, f1, "gather_weights_later_wait"))
    x1, h2 = norm_mod(rows, x0, g_n2, mv, 3, 4, "norm2", resid=(f1, 2, 0.5))

    conf_w = w_conf_out.shape[1] * 4
    col_groups = [(0, 2 * conf_w), (2 * conf_w, lru_w), (2 * conf_w + lru_w, lru_w), (2 * conf_w + 2 * lru_w, 2 * d)]
    pvg, ux, gl, pgc = [matmul(h2, wb["w_in"][:, s:s + n], "nn", F32, f"in_proj{k}", bias=b_in[:, s:s + n])
                        for k, (s, n) in enumerate(col_groups)]
    u = glu(rows, pvg, "glu")
    u2 = dwconv(rows, u, full["w_dw"], b_dw, taps // 2, F32, "conf_conv")
    s_act = ln_silu(rows, u2, g_ln, b_ln, "conf_ln")
    yc = matmul(s_act, wb["w_conf_out"], "nn", F32, "conf_out")

    xr = dwconv(rows, ux, full["w_lru_conv"], b_lru_conv, LRU_PAD_LEFT, F32, "lru_conv")
    w_rec, w_ing = w_rec_gate[0].astype(BF16), w_in_gate[0].astype(BF16)
    wcat = jnp.concatenate([w_rec[0], w_ing[0], w_rec[1], w_ing[1]], axis=-1)

    def per_block(a):
        return jnp.transpose(a.reshape(2, n_blk, LANES), (1, 0, 2)).reshape(n_blk, 1, 2 * LANES)

    b_rec, b_ing = full["b_rec_gate"].reshape(2, n_blk, LANES), full["b_in_gate"].reshape(2, n_blk, LANES)
    bcat = jnp.concatenate([b_rec[0], b_ing[0], b_rec[1], b_ing[1]], axis=-1).reshape(n_blk, 1, 4 * LANES)
    lam = per_block(full["lru_lambda"])
    a_gate, b_gate = lru_gates(rows, xr, wcat, bcat, lam, "lru_gates")
    h_scan, h_prev = lru_scan(rows, a_gate, b_gate, "lru_scan")
    yl_in = lru_merge(rows, h_scan, gl, "lru_merge")
    yl = matmul(yl_in, wb["w_lru_out"], "nn", F32, "lru_out")
    z = z_merge(rows, pgc, yc, yl, "z_merge")
    y = matmul(z, wb["w_out"], "nn", F32, "mix_out")

    x2, h3 = norm_mod(rows, x1, g_n3, mv, 6, 7, "norm3", resid=(y, 5, 1.0))
    gu2, act2, f2 = ffn_fwd("ffn2", h3, wb["w_ffn2_up"], wb["w_ffn2_down"])
    loss_part, dx3, dg_final = loss_head(rows, x2, f2, mv, g_final.reshape(1, d), loss_target[0], "loss_head")
    loss = lax.psum(loss_part[0, 0], ("x", "y", "c"))

    grads = {"g_final": dg_final}
    dmv = [None] * N_MOD

    def ffn_bwd(tag, dxn, x_prev, h, gu, act, f, w_up, w_down, g, ks):
        k_shift, k_scale, k_gate = ks
        df, dmv[k_gate] = gate_bwd(rows, dxn, f, mv, k_gate, 0.5, f"{tag}_gate_bwd")
        grads[f"w_{tag}_down"] = matmul(act, df, "tn", F32, f"{tag}_down_dw")
        dact = matmul(df, w_down, "nt", F32, f"{tag}_down_dx")
        dgu = swiglu_bwd(rows, gu, dact, f"{tag}_act_bwd")
        grads[f"w_{tag}_up"] = matmul(h, dgu, "tn", F32, f"{tag}_up_dw")
        dh = matmul(dgu, w_up, "nt", F32, f"{tag}_up_dx")
        dx, dg, dmv[k_shift], dmv[k_scale] = norm_mod_bwd(rows, x_prev, dh, dxn, g, mv, k_shift, k_scale, f"{tag}_norm_bwd")
        return dx, dg

    def grad_pieces(names):
        return [(_shards_from_cols(grads[n]) if n in COL_SHARDED else grads[n].reshape(4, -1, grads[n].shape[-1])).astype(BF16)
                for n in names]

    scatters = []

    def start_scatter(names, tag):
        scatters.append((names, tag, exchange_start("chips", grad_pieces(names), True, f"scatter_grads_{tag}_start")))

    dx2, grads["g_n3"] = ffn_bwd("ffn2", dx3, x2, h3, gu2, act2, f2, wb["w_ffn2_up"], wb["w_ffn2_down"], g_n3, (6, 7, 8))
    start_scatter(['w_ffn2_up', 'w_ffn2_down'], "ffn2")

    dy, dmv[5] = gate_bwd(rows, dx2, y, mv, 5, 1.0, "mix_gate_bwd")
    grads["w_out"] = matmul(z, dy, "tn", F32, "mix_out_dw")
    dz = matmul(dy, wb["w_out"], "nt", F32, "mix_out_dx")
    dyc, dyl, dpgc = z_merge_bwd(rows, pgc, yc, yl, dz, "z_merge_bwd")

    grads["w_conf_out"] = matmul(s_act, dyc, "tn", F32, "conf_out_dw")
    ds_act = matmul(dyc, wb["w_conf_out"], "nt", F32, "conf_out_dx")
    du2, grads["g_ln"], grads["b_ln"] = ln_silu_bwd(rows, u2, ds_act, g_ln, b_ln, "conf_ln_bwd")
    grads["w_dw"], grads["b_dw"] = dwconv_bwd_w(rows, u, du2, taps, taps // 2, "conf_conv_dw")
    du = dwconv(rows, du2, full["w_dw"][::-1], None, taps - 1 - taps // 2, F32, "conf_conv_dx")
    dpvg = glu_bwd(rows, pvg, du, "glu_bwd")

    grads["w_lru_out"] = matmul(yl_in, dyl, "tn", F32, "lru_out_dw")
    start_scatter(['w_out', 'w_conf_out', 'w_lru_out'], "mixer")
    dyl_in = matmul(dyl, wb["w_lru_out"], "nt", F32, "lru_out_dx")
    dh_sum, dgl = lru_merge_bwd(rows, h_scan, gl, dyl_in, "lru_merge_bwd")
    da_gate, db_gate = lru_scan_bwd(rows, dh_sum, a_gate, h_prev, "lru_scan_bwd")
    dxr, dwcat, dbcat, dlam = lru_gates_bwd(rows, xr, wcat, bcat, lam, da_gate, db_gate, "lru_gates_bwd")
    grads["w_rec_gate"] = jnp.stack([dwcat[:, :, 0:LANES], dwcat[:, :, 2 * LANES:3 * LANES]])
    grads["w_in_gate"] = jnp.stack([dwcat[:, :, LANES:2 * LANES], dwcat[:, :, 3 * LANES:4 * LANES]])
    dbcat = dbcat.reshape(n_blk, 4, LANES)
    grads["b_rec_gate"] = jnp.stack([dbcat[:, 0], dbcat[:, 2]]).reshape(2, lru_w)
    grads["b_in_gate"] = jnp.stack([dbcat[:, 1], dbcat[:, 3]]).reshape(2, lru_w)
    grads["lru_lambda"] = jnp.transpose(dlam.reshape(n_blk, 2, LANES), (1, 0, 2)).reshape(2, lru_w)
    grads["w_lru_conv"], grads["b_lru_conv"] = dwconv_bwd_w(rows, ux, dxr, lru_taps, LRU_PAD_LEFT, "lru_conv_dw")
    dux = dwconv(rows, dxr, full["w_lru_conv"][::-1], None, lru_taps - 1 - LRU_PAD_LEFT, BF16, "lru_conv_dx")

    dproj = jnp.concatenate([dpvg, dux, dgl, dpgc], axis=1)
    grads["b_in"] = col_sum(rows, dproj, "in_proj_db")
    grads["w_in"] = matmul(h2, dproj, "tn", F32, "in_proj_dw")
    start_scatter(['w_in'], "in_proj")
    dh2 = matmul(dproj, wb["w_in"], "nt", F32, "in_proj_dx")
    dx1, grads["g_n2"], dmv[3], dmv[4] = norm_mod_bwd(rows, x1, dh2, dx2, g_n2, mv, 3, 4, "norm2_bwd")

    dx0, grads["g_n1"] = ffn_bwd("ffn1", dx1, x0, h1, gu1, act1, f1, wb["w_ffn1_up"], wb["w_ffn1_down"], g_n1, (0, 1, 2))
    grad_x = dx0[:t_lat][None]

    last = ['w_ffn1_up', 'w_ffn1_down']
    from_chips = dict(zip(last, exchange("chips", grad_pieces(last), True, "scatter_grads_ffn1")))
    for names, tag, handle in scatters:
        from_chips.update(zip(names, exchange_wait(handle, dx0, f"scatter_grads_{tag}_wait")))
    chip_sums = [sum_parts(from_chips[n].reshape(4, -1, from_chips[n].shape[-1]), f"chip_sum_{n}") for n in big]
    out = {}
    for n, p in zip(big, chip_sums):
        out[n] = adamw_pair(p, flat2(wt[n]), flat2(mom[n]), flat2(var[n]), f"adamw_{n}")

    dmod = jnp.concatenate(dmv, axis=1)
    small_names = ['g_n1', 'g_n2', 'b_in', 'b_dw', 'g_ln', 'b_ln', 'b_lru_conv', 'w_rec_gate', 'w_in_gate', 'g_n3', 'g_final'] + SMALL_SHARDED
    small_list = [grads[n] for n in small_names] + [dmod[0], dmod[1]]
    small_buf = _pack(small_list, 8 * 2 * SUBLANES)
    small_parts, dmod_all = exchange("all", [small_buf.reshape(8, -1, LANES), _pack([dmod[0]])], [True, False], "scatter_small_grads")
    small_eighth = sum_parts(small_parts, "sum_small_grads")
    small_sum = exchange("all", [small_eighth], False, "gather_small_grads")[0].reshape(small_buf.shape)
    small_sum = _unpack(small_sum, [a.shape for a in small_list])
    total = dict(zip(small_names, small_sum))
    dmod_all = _unpack(dmod_all, [dmod[0].shape])[0].reshape(8, N_MOD * d)
    dmc = small_sum[-1].reshape(1, N_MOD * d)
    total["b_mod"] = small_sum[-2].reshape(1, N_MOD * d) + dmc

    d9 = jnp.concatenate([dmod_all, dmc, jnp.zeros((7, N_MOD * d), F32)], axis=0)
    d9_cols = lax.dynamic_slice_in_dim(d9, chip * w_mod.shape[2], w_mod.shape[2], axis=1)
    g_wmod, dc9 = mod_matmul_bwd(c9, d9_cols, w_mod[0], "mod_matmul_bwd")
    dc_all = exchange("all", [dc9[8:16]], False, "gather_dc")[0]
    total["c_ctx"] = sum_parts(jnp.stack([dc_all[0], dc_all[2], dc_all[4], dc_all[6]]), "sum_dc")[0:1]
    out["w_mod"] = adamw(g_wmod[None], w_mod[0], m_w_mod[0], v_w_mod[0], "adamw_w_mod")

    small_all_names = [n for n in W_NAMES if n not in out]
    g_local, shapes = [], []
    for n in small_all_names:
        g = total[n].reshape(flat2(wt[n]).shape[:-1] + (-1,)) if n in SMALL_SHARDED else total[n].reshape(flat2(wt[n]).shape)
        if n in SMALL_SHARDED:
            width = wt[n].shape[-1]
            g = lax.dynamic_slice_in_dim(g, chip * width, width, axis=1)
        g_local.append(g)
        shapes.append(g.shape)
    packed = adamw(_pack(g_local)[None], _pack([flat2(wt[n]) for n in small_all_names]),
                   _pack([flat2(mom[n]) for n in small_all_names]), _pack([flat2(var[n]) for n in small_all_names]), "adamw_small")
    for k, n in enumerate(small_all_names):
        out[n] = [_unpack(p, shapes)[k] for p in packed]

    results = [loss, grad_x]
    for k in range(4):
        results += [out[n][k].reshape(wt[n].shape) for n in W_NAMES]
    return tuple(results)
```

```python
import functools

import jax
import jax.numpy as jnp
from jax import lax
from jax.experimental import pallas as pl
from jax.experimental.pallas import tpu as pltpu

F32 = jnp.float32
BF16 = jnp.bfloat16
MESH = pl.DeviceIdType.MESH

EPS = 1e-6
GRID_W = 64
N_MOD = 9
LRU_C = 8.0
LRU_PAD_LEFT = 2
ADAM_LR, ADAM_B1, ADAM_B2, ADAM_EPS, ADAM_WD, ADAM_STEP = 0.001, 0.9, 0.999, 1e-08, 0.01, 10

LANES = 128
SUBLANES = 8
HALO = 16
VMEM_LIMIT = 56 * 1024 * 1024

W_NAMES = ['c_ctx', 'w_mod', 'b_mod', 'g_n1', 'w_ffn1_up', 'w_ffn1_down', 'g_n2', 'w_in', 'b_in', 'w_dw', 'b_dw',
           'g_ln', 'b_ln', 'w_conf_out', 'w_lru_conv', 'b_lru_conv', 'w_rec_gate', 'b_rec_gate', 'w_in_gate',
           'b_in_gate', 'lru_lambda', 'w_lru_out', 'w_out', 'g_n3', 'w_ffn2_up', 'w_ffn2_down', 'g_final']
COL_SHARDED = ['w_ffn1_up', 'w_in', 'w_ffn2_up']
ROW_SHARDED = ['w_ffn1_down', 'w_conf_out', 'w_lru_out', 'w_out', 'w_ffn2_down']
SMALL_SHARDED = ['w_dw', 'w_lru_conv', 'b_rec_gate', 'b_in_gate', 'lru_lambda']


def _params(*semantics):
    return pltpu.CompilerParams(dimension_semantics=semantics, vmem_limit_bytes=VMEM_LIMIT)


def _pick(n, target, mult=LANES):
    best = None
    for t in range(mult, min(n, target) + 1, mult):
        if n % t == 0:
            best = t
    return best or n


def _chunks(width, target=512):
    w = _pick(width, target)
    return [slice(s, s + w) for s in range(0, width, w)]


def _sigmoid(x):
    return jax.nn.sigmoid(x)


def _silu_and_grad(x):
    s = _sigmoid(x)
    return x * s, s * (1.0 + x * (1.0 - s))


_GELU_K = 0.7978845608028654


def _gelu_and_grad(x):
    t = jnp.tanh(_GELU_K * (x + 0.044715 * (x * x * x)))
    return 0.5 * x * (1.0 + t), 0.5 * (1.0 + t) + 0.5 * x * (1.0 - t * t) * _GELU_K * (1.0 + 3 * 0.044715 * x * x)


def _neg_expm1(z):
    series = -(z * (1.0 + z * (0.5 + z * (1.0 / 6.0 + z * (1.0 / 24.0)))))
    return jnp.where(z > -0.01, series, 1.0 - jnp.exp(z))


def _softplus(x):
    return jnp.maximum(x, 0.0) + jnp.log1p(jnp.exp(-jnp.abs(x)))


def _accumulate(ref, value, first):
    @pl.when(first)
    def _():
        ref[...] = value

    @pl.when(jnp.logical_not(first))
    def _():
        ref[...] += value


def _colsum(x):
    return jnp.sum(x, axis=0, keepdims=True)


class Rows:
    def __init__(self, t_lat, t_ctx, d_model):
        self.tm = _pick(t_ctx, 256, HALO)
        assert t_lat % self.tm == 0 and t_ctx % self.tm == 0
        self.n_lat = t_lat // self.tm
        self.n_all = (t_lat + t_ctx) // self.tm
        self.t_all = t_lat + t_ctx
        self.d = d_model

    def seg(self, i):
        return jnp.where(i >= self.n_lat, 1, 0)

    def seg_first(self, i):
        return jnp.logical_or(i == 0, i == self.n_lat)

    def seg_last(self, i):
        return jnp.logical_or(i == self.n_lat - 1, i == self.n_all - 1)

    def row(self, width, col=0):
        return pl.BlockSpec((self.tm, width), lambda i: (i, col))

    def vec(self, width):
        return pl.BlockSpec((1, width), lambda i: (0, 0))

    def full(self, shape):
        return pl.BlockSpec(shape, lambda i: (0,) * len(shape))

    def mod(self):
        return pl.BlockSpec((None, N_MOD, self.d), lambda i: (self.seg(i), 0, 0))

    def segsum(self):
        return pl.BlockSpec((None, 1, self.d), lambda i: (self.seg(i), 0, 0))

    def segsum_shape(self):
        return jax.ShapeDtypeStruct((2, 1, self.d), F32)

    def halo(self, width, side):
        per = self.tm // HALO
        last = self.t_all // HALO - 1
        if side < 0:
            return pl.BlockSpec((HALO, width), lambda i: (jnp.maximum(i * per - 1, 0), 0))
        return pl.BlockSpec((HALO, width), lambda i: (jnp.minimum((i + 1) * per, last), 0))

    def call(self, body, name, in_specs, out_specs, out_shape, scratch=()):
        return pl.pallas_call(body, name=name, grid=(self.n_all,), in_specs=in_specs, out_specs=out_specs,
                              out_shape=out_shape, scratch_shapes=list(scratch), compiler_params=_params("arbitrary"))


_PEER_FLIPS = {
    "chips": [(1, 0, 0), (0, 1, 0), (1, 1, 0)],
    "sibling": [(0, 0, 1)],
    "all": [(0, 0, 1), (0, 1, 0), (0, 1, 1), (1, 0, 0), (1, 0, 1), (1, 1, 0), (1, 1, 1)],
}


def _slot(kind, x, y, c):
    return {"chips": 2 * x + y, "sibling": c, "all": 4 * x + 2 * y + c}[kind]


def exchange(kind, srcs, indexed, name):
    flips = _PEER_FLIPS[kind]
    n_slots = len(flips) + 1
    n_t = len(srcs)
    by_slot = list(indexed) if isinstance(indexed, (list, tuple)) else [indexed] * n_t
    out_shapes = [jax.ShapeDtypeStruct(s.shape if ix else (n_slots,) + s.shape, s.dtype) for s, ix in zip(srcs, by_slot)]

    def body(*refs):
        src_refs, dst_refs = refs[:n_t], refs[n_t:2 * n_t]
        send_sems, recv_sems, local_sems = refs[2 * n_t:]
        x, y, c = lax.axis_index("x"), lax.axis_index("y"), lax.axis_index("c")
        me = _slot(kind, x, y, c)

        def piece(t, k):
            return src_refs[t].at[k] if by_slot[t] else src_refs[t]

        peers = [(1 - x if fx else x, 1 - y if fy else y, 1 - c if fc else c) for fx, fy, fc in flips]
        local = [pltpu.make_async_copy(piece(t, me), dst_refs[t].at[me], local_sems.at[t]) for t in range(n_t)]
        for cp in local:
            cp.start()
        sends = []
        for t in range(n_t):
            for p, peer in enumerate(peers):
                sends.append(pltpu.make_async_remote_copy(
                    src_ref=piece(t, _slot(kind, *peer)), dst_ref=dst_refs[t].at[me],
                    send_sem=send_sems.at[t, p], recv_sem=recv_sems.at[t, p], device_id=peer, device_id_type=MESH))
        for cp in sends:
            cp.start()
        for t in range(n_t):
            for p, peer in enumerate(peers):
                theirs = _slot(kind, *peer)
                pltpu.make_async_remote_copy(
                    src_ref=piece(t, me), dst_ref=dst_refs[t].at[theirs],
                    send_sem=send_sems.at[t, p], recv_sem=recv_sems.at[t, p], device_id=peer, device_id_type=MESH).wait_recv()
        for cp in sends:
            cp.wait_send()
        for cp in local:
            cp.wait()

    any_spec = pl.BlockSpec(memory_space=pl.ANY)
    outs = pl.pallas_call(
        body, name=name, out_shape=out_shapes, in_specs=[any_spec] * n_t, out_specs=[any_spec] * n_t,
        scratch_shapes=[pltpu.SemaphoreType.DMA((n_t, len(flips))), pltpu.SemaphoreType.DMA((n_t, len(flips))),
                        pltpu.SemaphoreType.DMA((n_t,))],
        compiler_params=pltpu.CompilerParams(has_side_effects=True),
    )(*srcs)
    return list(outs)


def matmul(a, b, mode, out_dtype, name, bias=None):
    if mode == "nn":
        (n_i, n_r), (_, n_j) = a.shape, b.shape
    elif mode == "nt":
        (n_i, n_r), (n_j, _) = a.shape, b.shape
    else:
        (n_r, n_i), (_, n_j) = a.shape, b.shape
    ti = _pick(n_i, 1408)
    tj = _pick(n_j, 1664)
    tr = n_r if (mode == "nn" and n_r <= 2816) else _pick(n_r, 768 if mode == "tn" else 1664)
    steps = n_r // tr
    bytes_a, bytes_b = a.size * a.dtype.itemsize, b.size * b.dtype.itemsize
    j_outer = steps == 1 and bytes_b + (n_j // tj) * bytes_a < bytes_a + (n_i // ti) * bytes_b

    def at(fn):
        return (lambda j, i, r: fn(i, j, r)) if j_outer else fn

    a_spec = {"nn": pl.BlockSpec((ti, tr), at(lambda i, j, r: (i, r))), "nt": pl.BlockSpec((ti, tr), at(lambda i, j, r: (i, r))),
              "tn": pl.BlockSpec((tr, ti), at(lambda i, j, r: (r, i)))}[mode]
    b_spec = {"nn": pl.BlockSpec((tr, tj), at(lambda i, j, r: (r, j))), "nt": pl.BlockSpec((tj, tr), at(lambda i, j, r: (j, r))),
              "tn": pl.BlockSpec((tr, tj), at(lambda i, j, r: (r, j)))}[mode]
    dims = {"nn": (((1,), (0,)), ((), ())), "nt": (((1,), (1,)), ((), ())), "tn": (((0,), (0,)), ((), ()))}[mode]

    def body(*refs):
        a_ref, b_ref = refs[:2]
        bias_ref = refs[2] if bias is not None else None
        o_ref = refs[3] if bias is not None else refs[2]
        prod = lax.dot_general(a_ref[...].astype(BF16), b_ref[...].astype(BF16), dims, preferred_element_type=F32)

        def finish(total):
            if bias_ref is not None:
                total = total + bias_ref[...]
            o_ref[...] = total.astype(out_dtype)

        if steps == 1:
            finish(prod)
        else:
            acc = refs[-1]
            r = pl.program_id(2)

            @pl.when(r == 0)
            def _():
                acc[...] = prod

            @pl.when(r > 0)
            def _():
                acc[...] += prod

            @pl.when(r == steps - 1)
            def _():
                finish(acc[...])

    in_specs = [a_spec, b_spec] + ([pl.BlockSpec((1, tj), at(lambda i, j, r: (0, j)))] if bias is not None else [])
    args = (a, b) + ((bias,) if bias is not None else ())
    grid = (n_j // tj, n_i // ti, steps) if j_outer else (n_i // ti, n_j // tj, steps)
    return pl.pallas_call(
        body, name=name, grid=grid, in_specs=in_specs,
        out_specs=pl.BlockSpec((ti, tj), at(lambda i, j, r: (i, j))), out_shape=jax.ShapeDtypeStruct((n_i, n_j), out_dtype),
        scratch_shapes=[pltpu.VMEM((ti, tj), F32)] if steps > 1 else [],
        compiler_params=_params("arbitrary", "arbitrary", "arbitrary"),
    )(*args)


def matmul_swiglu(h, w_up, name):
    n_t, d = h.shape
    f = w_up.shape[1] // 2
    ti, tj = _pick(n_t, 384), _pick(f, 1408)
    n_j = f // tj

    def body(h_ref, wg_ref, wu_ref, g_ref, u_ref, act_ref):
        hv = h_ref[...]
        g = jnp.dot(hv, wg_ref[...], preferred_element_type=F32)
        u = jnp.dot(hv, wu_ref[...], preferred_element_type=F32)
        g_ref[...] = g.astype(BF16)
        u_ref[...] = u.astype(BF16)
        act_ref[...] = (_silu_and_grad(g)[0] * u).astype(BF16)

    out = pl.BlockSpec((ti, tj), lambda j, i: (i, j))
    shape = jax.ShapeDtypeStruct((n_t, f), BF16)
    return pl.pallas_call(
        body, name=name, grid=(n_j, n_t // ti),
        in_specs=[pl.BlockSpec((ti, d), lambda j, i: (i, 0)), pl.BlockSpec((d, tj), lambda j, i: (0, j)),
                  pl.BlockSpec((d, tj), lambda j, i: (0, j + n_j))],
        out_specs=[out, out, out], out_shape=[shape, shape, shape], compiler_params=_params("arbitrary", "arbitrary"),
    )(h, w_up, w_up)


def norm_mod(rows, x, g, mv, shift_k, scale_k, name, resid=None, pos=None):
    d = rows.d
    changed = resid is not None or pos is not None

    def body(*refs):
        refs = list(refs)
        x_ref, g_ref, mv_ref = refs[:3]
        rest = refs[3:]
        xv = x_ref[...]
        if pos is not None:
            xv = xv + rest.pop(0)[...]
        if resid is not None:
            xv = xv + resid[2] * mv_ref[resid[1]:resid[1] + 1, :] * rest.pop(0)[...]
        if changed:
            rest.pop(0)[...] = xv
        r = lax.rsqrt(jnp.mean(xv * xv, axis=-1, keepdims=True) + EPS)
        h = (xv * r) * g_ref[...]
        if shift_k is not None:
            h = h * (1.0 + mv_ref[scale_k:scale_k + 1, :]) + mv_ref[shift_k:shift_k + 1, :]
        rest.pop(0)[...] = h.astype(BF16)

    ins = [x, g, mv] + ([pos] if pos is not None else []) + ([resid[0]] if resid is not None else [])
    in_specs = [rows.row(d), rows.vec(d), rows.mod()] + [rows.row(d)] * (len(ins) - 3)
    out_shape = ([jax.ShapeDtypeStruct((rows.t_all, d), F32)] if changed else []) + [jax.ShapeDtypeStruct((rows.t_all, d), BF16)]
    outs = rows.call(body, name, in_specs, [rows.row(d)] * len(out_shape), out_shape)(*ins)
    return (outs[0], outs[1]) if changed else (None, outs[0])


def norm_mod_bwd(rows, x, dh, dxn, g, mv, shift_k, scale_k, name):
    d = rows.d

    def body(x_ref, dh_ref, dxn_ref, g_ref, mv_ref, dx_ref, dg_ref, dsh_ref, dsc_ref):
        i = pl.program_id(0)
        xv, dhv, gv = x_ref[...], dh_ref[...], g_ref[...]
        r = lax.rsqrt(jnp.mean(xv * xv, axis=-1, keepdims=True) + EPS)
        n = xv * r
        dy = dhv * (1.0 + mv_ref[scale_k:scale_k + 1, :])
        dn = dy * gv
        dx_ref[...] = dxn_ref[...] + r * (dn - n * jnp.mean(dn * n, axis=-1, keepdims=True))
        _accumulate(dg_ref, _colsum(dy * n), i == 0)
        _accumulate(dsh_ref, _colsum(dhv), rows.seg_first(i))
        _accumulate(dsc_ref, _colsum(dhv * (n * gv)), rows.seg_first(i))

    return rows.call(
        body, name, [rows.row(d), rows.row(d), rows.row(d), rows.vec(d), rows.mod()],
        [rows.row(d), rows.vec(d), rows.segsum(), rows.segsum()],
        [jax.ShapeDtypeStruct((rows.t_all, d), F32), jax.ShapeDtypeStruct((1, d), F32), rows.segsum_shape(), rows.segsum_shape()],
    )(x, dh, dxn, g, mv)


def gate_bwd(rows, dxn, f, mv, gate_k, scale, name):
    d = rows.d

    def body(dxn_ref, f_ref, mv_ref, df_ref, dgate_ref):
        i = pl.program_id(0)
        dv = dxn_ref[...]
        df_ref[...] = (scale * mv_ref[gate_k:gate_k + 1, :] * dv).astype(BF16)
        _accumulate(dgate_ref, scale * _colsum(dv * f_ref[...]), rows.seg_first(i))

    return rows.call(body, name, [rows.row(d), rows.row(d), rows.mod()], [rows.row(d), rows.segsum()],
                     [jax.ShapeDtypeStruct((rows.t_all, d), BF16), rows.segsum_shape()])(dxn, f, mv)


def swiglu_bwd(rows, g, u, dact, name):
    f = g.shape[1]

    def body(g_ref, u_ref, da_ref, o_ref):
        for cs in _chunks(f):
            act, dact_dg = _silu_and_grad(g_ref[:, cs].astype(F32))
            dav = da_ref[:, cs].astype(F32)
            o_ref[:, cs] = (dav * u_ref[:, cs].astype(F32) * dact_dg).astype(BF16)
            o_ref[:, slice(f + cs.start, f + cs.stop)] = (dav * act).astype(BF16)

    return rows.call(body, name, [rows.row(f), rows.row(f), rows.row(f)], rows.row(2 * f),
                     jax.ShapeDtypeStruct((rows.t_all, 2 * f), BF16))(g, u, dact)


def glu(rows, pvg, name):
    w = pvg.shape[1] // 2

    def body(v_ref, t_ref, o_ref):
        for cs in _chunks(w):
            o_ref[:, cs] = v_ref[:, cs].astype(F32) * _sigmoid(t_ref[:, cs].astype(F32))

    return rows.call(body, name, [rows.row(w, 0), rows.row(w, 1)], rows.row(w), jax.ShapeDtypeStruct((rows.t_all, w), F32))(pvg, pvg)


def glu_bwd(rows, pvg, du, name):
    w = pvg.shape[1] // 2

    def body(v_ref, t_ref, du_ref, o_ref):
        for cs in _chunks(w):
            s = _sigmoid(t_ref[:, cs].astype(F32))
            duv = du_ref[:, cs]
            o_ref[:, cs] = (duv * s).astype(BF16)
            o_ref[:, slice(w + cs.start, w + cs.stop)] = (duv * v_ref[:, cs].astype(F32) * s * (1.0 - s)).astype(BF16)

    return rows.call(body, name, [rows.row(w, 0), rows.row(w, 1), rows.row(w)], rows.row(2 * w),
                     jax.ShapeDtypeStruct((rows.t_all, 2 * w), BF16))(pvg, pvg, du)


def ln_silu(rows, u, g, b, name):
    w = u.shape[1]

    def body(u_ref, g_ref, b_ref, o_ref):
        uv = u_ref[...]
        xc = uv - jnp.mean(uv, axis=-1, keepdims=True)
        n = xc * lax.rsqrt(jnp.mean(xc * xc, axis=-1, keepdims=True) + EPS)
        o_ref[...] = _silu_and_grad(n * g_ref[...] + b_ref[...])[0].astype(BF16)

    return rows.call(body, name, [rows.row(w), rows.vec(w), rows.vec(w)], rows.row(w),
                     jax.ShapeDtypeStruct((rows.t_all, w), BF16))(u, g, b)


def ln_silu_bwd(rows, u, ds, g, b, name):
    w = u.shape[1]

    def body(u_ref, ds_ref, g_ref, b_ref, du_ref, dg_ref, db_ref):
        i = pl.program_id(0)
        uv, gv = u_ref[...], g_ref[...]
        xc = uv - jnp.mean(uv, axis=-1, keepdims=True)
        r = lax.rsqrt(jnp.mean(xc * xc, axis=-1, keepdims=True) + EPS)
        n = xc * r
        dy = ds_ref[...].astype(F32) * _silu_and_grad(n * gv + b_ref[...])[1]
        dn = dy * gv
        du_ref[...] = r * (dn - jnp.mean(dn, axis=-1, keepdims=True) - n * jnp.mean(dn * n, axis=-1, keepdims=True))
        _accumulate(dg_ref, _colsum(dy * n), i == 0)
        _accumulate(db_ref, _colsum(dy), i == 0)

    return rows.call(body, name, [rows.row(w), rows.row(w), rows.vec(w), rows.vec(w)], [rows.row(w), rows.vec(w), rows.vec(w)],
                     [jax.ShapeDtypeStruct((rows.t_all, w), F32), jax.ShapeDtypeStruct((1, w), F32),
                      jax.ShapeDtypeStruct((1, w), F32)])(u, ds, g, b)


def lru_merge(rows, h, gl, name):
    w = gl.shape[1]

    def body(h_ref, gl_ref, o_ref):
        for cs in _chunks(w):
            o_ref[:, cs] = ((h_ref[0, :, cs] + h_ref[1, :, cs]) * _gelu_and_grad(gl_ref[:, cs].astype(F32))[0]).astype(BF16)

    return rows.call(body, name, [pl.BlockSpec((2, rows.tm, w), lambda i: (0, i, 0)), rows.row(w)], rows.row(w),
                     jax.ShapeDtypeStruct((rows.t_all, w), BF16))(h, gl)


def lru_merge_bwd(rows, h, gl, dy, name):
    w = gl.shape[1]

    def body(h_ref, gl_ref, dy_ref, dh_ref, dgl_ref):
        for cs in _chunks(w):
            act, dact = _gelu_and_grad(gl_ref[:, cs].astype(F32))
            dyv = dy_ref[:, cs].astype(F32)
            dh_ref[:, cs] = dyv * act
            dgl_ref[:, cs] = (dyv * (h_ref[0, :, cs] + h_ref[1, :, cs]) * dact).astype(BF16)

    return rows.call(body, name, [pl.BlockSpec((2, rows.tm, w), lambda i: (0, i, 0)), rows.row(w), rows.row(w)],
                     [rows.row(w), rows.row(w)],
                     [jax.ShapeDtypeStruct((rows.t_all, w), F32), jax.ShapeDtypeStruct((rows.t_all, w), BF16)])(h, gl, dy)


def z_merge(rows, pgc, yc, yl, name):
    d = rows.d

    def body(gc_ref, gr_ref, yc_ref, yl_ref, o_ref):
        for cs in _chunks(d):
            o_ref[:, cs] = (_sigmoid(gc_ref[:, cs].astype(F32)) * yc_ref[:, cs].astype(F32)
                            + _sigmoid(gr_ref[:, cs].astype(F32)) * yl_ref[:, cs].astype(F32)).astype(BF16)

    return rows.call(body, name, [rows.row(d, 0), rows.row(d, 1), rows.row(d), rows.row(d)], rows.row(d),
                     jax.ShapeDtypeStruct((rows.t_all, d), BF16))(pgc, pgc, yc, yl)


def z_merge_bwd(rows, pgc, yc, yl, dz, name):
    d = rows.d

    def body(gc_ref, gr_ref, yc_ref, yl_ref, dz_ref, dyc_ref, dyl_ref, dp_ref):
        for cs in _chunks(d):
            sc, sr = _sigmoid(gc_ref[:, cs].astype(F32)), _sigmoid(gr_ref[:, cs].astype(F32))
            dzv = dz_ref[:, cs].astype(F32)
            dyc_ref[:, cs] = (dzv * sc).astype(BF16)
            dyl_ref[:, cs] = (dzv * sr).astype(BF16)
            dp_ref[:, cs] = (dzv * yc_ref[:, cs].astype(F32) * sc * (1.0 - sc)).astype(BF16)
            dp_ref[:, slice(d + cs.start, d + cs.stop)] = (dzv * yl_ref[:, cs].astype(F32) * sr * (1.0 - sr)).astype(BF16)

    return rows.call(body, name, [rows.row(d, 0), rows.row(d, 1), rows.row(d), rows.row(d), rows.row(d)],
                     [rows.row(d), rows.row(d), rows.row(2 * d)],
                     [jax.ShapeDtypeStruct((rows.t_all, d), BF16), jax.ShapeDtypeStruct((rows.t_all, d), BF16),
                      jax.ShapeDtypeStruct((rows.t_all, 2 * d), BF16)])(pgc, pgc, yc, yl, dz)


def col_sum(rows, x, name):
    w = x.shape[1]

    def body(x_ref, o_ref):
        i = pl.program_id(0)
        for cs in _chunks(w):
            _accumulate(o_ref.at[:, cs], _colsum(x_ref[:, cs].astype(F32)), i == 0)

    return rows.call(body, name, [rows.row(w)], rows.vec(w), jax.ShapeDtypeStruct((1, w), F32))(x)


def loss_head(rows, x, f, mv, g, target, name):
    d = rows.d

    def body(x_ref, f_ref, mv_ref, g_ref, t_ref, loss_ref, dx_ref, dg_ref):
        i = pl.program_id(0)
        valid = jnp.where(i < rows.n_lat, 1.0, 0.0)
        xv = x_ref[...] + 0.5 * mv_ref[8:9, :] * f_ref[...]
        gv = g_ref[...]
        r = lax.rsqrt(jnp.mean(xv * xv, axis=-1, keepdims=True) + EPS)
        n = xv * r
        err = (n * gv - t_ref[...]) * valid
        part = 0.5 * jnp.sum(jnp.mean(err * err, axis=-1, keepdims=True), axis=0, keepdims=True)
        _accumulate(loss_ref, jnp.broadcast_to(part, (1, LANES)), i == 0)
        dy = err * (1.0 / d)
        dn = dy * gv
        dx_ref[...] = r * (dn - n * jnp.mean(dn * n, axis=-1, keepdims=True))
        _accumulate(dg_ref, _colsum(dy * n), i == 0)

    target_spec = pl.BlockSpec((rows.tm, d), lambda i: (jnp.minimum(i, rows.n_lat - 1), 0))
    return rows.call(body, name, [rows.row(d), rows.row(d), rows.mod(), rows.vec(d), target_spec],
                     [rows.vec(LANES), rows.row(d), rows.vec(d)],
                     [jax.ShapeDtypeStruct((1, LANES), F32), jax.ShapeDtypeStruct((rows.t_all, d), F32),
                      jax.ShapeDtypeStruct((1, d), F32)])(x, f, mv, g, target)


def _extended(rows, i, prev_ref, cur_ref, next_ref, cs):
    prev = jnp.where(rows.seg_first(i), 0.0, prev_ref[:, cs].astype(F32))
    nxt = jnp.where(rows.seg_last(i), 0.0, next_ref[:, cs].astype(F32))
    return jnp.concatenate([prev, cur_ref[:, cs].astype(F32), nxt], axis=0)


def _shifted(ext, offset, tm):
    n = ext.shape[0]
    return pltpu.roll(ext, (-offset) % n, 0)[HALO:HALO + tm] if offset else ext[HALO:HALO + tm]


def _pad_taps(w):
    k = w.shape[0]
    return jnp.pad(w, ((0, -k % SUBLANES), (0, 0)))


def dwconv(rows, u, w, b, pad_left, out_dtype, name):
    taps, width = w.shape
    wp = _pad_taps(w)

    def body(*refs):
        prev_ref, cur_ref, next_ref, w_ref = refs[:4]
        b_ref = refs[4] if b is not None else None
        o_ref = refs[-1]
        i = pl.program_id(0)
        for cs in _chunks(width, LANES):
            ext = _extended(rows, i, prev_ref, cur_ref, next_ref, cs)
            acc = jnp.zeros((rows.tm, LANES), F32) if b_ref is None else jnp.broadcast_to(b_ref[:, cs], (rows.tm, LANES))
            for k in range(taps):
                acc = acc + w_ref[k:k + 1, cs] * _shifted(ext, k - pad_left, rows.tm)
            o_ref[:, cs] = acc.astype(out_dtype)

    ins = [u, u, u, wp] + ([b] if b is not None else [])
    in_specs = [rows.halo(width, -1), rows.row(width), rows.halo(width, 1), rows.full(wp.shape)] + ([rows.vec(width)] if b is not None else [])
    return rows.call(body, name, in_specs, rows.row(width), jax.ShapeDtypeStruct((rows.t_all, width), out_dtype))(*ins)


def dwconv_bwd_w(rows, u, dy, taps, pad_left, name):
    width = u.shape[1]
    taps_p = taps + (-taps % SUBLANES)

    def body(prev_ref, cur_ref, next_ref, dy_ref, dw_ref, db_ref):
        i = pl.program_id(0)
        tap_row = lax.broadcasted_iota(jnp.int32, (taps_p, LANES), 0)
        for cs in _chunks(width, LANES):
            ext = _extended(rows, i, prev_ref, cur_ref, next_ref, cs)
            dyv = dy_ref[:, cs]
            total = jnp.zeros((taps_p, LANES), F32)
            for k in range(taps):
                total = total + jnp.where(tap_row == k, _colsum(dyv * _shifted(ext, k - pad_left, rows.tm)), 0.0)
            _accumulate(dw_ref.at[:, cs], total, i == 0)
            _accumulate(db_ref.at[:, cs], _colsum(dyv), i == 0)

    dw, db = rows.call(body, name, [rows.halo(width, -1), rows.row(width), rows.halo(width, 1), rows.row(width)],
                       [rows.full((taps_p, width)), rows.vec(width)],
                       [jax.ShapeDtypeStruct((taps_p, width), F32), jax.ShapeDtypeStruct((1, width), F32)])(u, u, u, dy)
    return dw[:taps], db


def _gate_values(xr, pre, lam, d):
    r = _sigmoid(pre[:, (2 * d) * LANES:(2 * d + 1) * LANES])
    ig = _sigmoid(pre[:, (2 * d + 1) * LANES:(2 * d + 2) * LANES])
    sp = _softplus(-lam[:, d * LANES:(d + 1) * LANES])
    log_a = -LRU_C * r * sp
    return r, ig, sp, jnp.exp(log_a), jnp.sqrt(_neg_expm1(2.0 * log_a))


def lru_gates(rows, xr, wcat, bcat, lam, name):
    n_blk = wcat.shape[0]
    width = xr.shape[1]

    def body(xr_ref, w_ref, b_ref, lam_ref, a_ref, bb_ref):
        for h in range(n_blk):
            cs = slice(h * LANES, (h + 1) * LANES)
            xv = xr_ref[:, cs]
            pre = jnp.dot(xv.astype(BF16), w_ref[h], preferred_element_type=F32) + b_ref[h]
            for d in range(2):
                _, ig, _, a, q = _gate_values(xv, pre, lam_ref[h], d)
                a_ref[d, :, cs] = a
                bb_ref[d, :, cs] = q * (ig * xv)

    both = pl.BlockSpec((2, rows.tm, width), lambda i: (0, i, 0))
    shape = jax.ShapeDtypeStruct((2, rows.t_all, width), F32)
    return rows.call(body, name, [rows.row(width), rows.full(wcat.shape), rows.full(bcat.shape), rows.full(lam.shape)],
                     [both, both], [shape, shape])(xr, wcat, bcat, lam)


def lru_gates_bwd(rows, xr, wcat, bcat, lam, da, dbb, name):
    n_blk = wcat.shape[0]
    width = xr.shape[1]

    def body(xr_ref, w_ref, b_ref, lam_ref, da_ref, dbb_ref, dxr_ref, dw_ref, db_ref, dlam_ref):
        i = pl.program_id(0)
        for h in range(n_blk):
            cs = slice(h * LANES, (h + 1) * LANES)
            xv = xr_ref[:, cs]
            xb = xv.astype(BF16)
            wv = w_ref[h]
            pre = jnp.dot(xb, wv, preferred_element_type=F32) + b_ref[h]
            dxr = jnp.zeros_like(xv)
            dpre, dlam = [], []
            for d in range(2):
                r, ig, sp, a, q = _gate_values(xv, pre, lam_ref[h], d)
                dav, dbv = da_ref[d, :, cs], dbb_ref[d, :, cs]
                dlog_a = dav * a - dbv * (ig * xv) * (a * a) / q
                dpre.append(dlog_a * (-LRU_C * sp) * r * (1.0 - r))
                dpre.append(dbv * q * xv * ig * (1.0 - ig))
                dxr = dxr + dbv * q * ig
                dlam.append(_colsum(dlog_a * (-LRU_C * r)) * (-_sigmoid(-lam_ref[h][:, d * LANES:(d + 1) * LANES])))
            dpre = jnp.concatenate(dpre, axis=1)
            dpb = dpre.astype(BF16)
            dxr_ref[:, cs] = dxr + lax.dot_general(dpb, wv, (((1,), (1,)), ((), ())), preferred_element_type=F32)
            _accumulate(dw_ref.at[h], lax.dot_general(xb, dpb, (((0,), (0,)), ((), ())), preferred_element_type=F32), i == 0)
            _accumulate(db_ref.at[h], _colsum(dpre), i == 0)
            _accumulate(dlam_ref.at[h], jnp.concatenate(dlam, axis=1), i == 0)

    both = pl.BlockSpec((2, rows.tm, width), lambda i: (0, i, 0))
    return rows.call(
        body, name, [rows.row(width), rows.full(wcat.shape), rows.full(bcat.shape), rows.full(lam.shape), both, both],
        [rows.row(width), rows.full(wcat.shape), rows.full(bcat.shape), rows.full(lam.shape)],
        [jax.ShapeDtypeStruct((rows.t_all, width), F32), jax.ShapeDtypeStruct(wcat.shape, F32),
         jax.ShapeDtypeStruct(bcat.shape, F32), jax.ShapeDtypeStruct(lam.shape, F32)])(xr, wcat, bcat, lam, da, dbb)


def _tile_scan(a, b, reverse):
    n = a.shape[0]
    row = lax.broadcasted_iota(jnp.int32, a.shape, 0)
    k = 1
    while k < n:
        ok = (row < n - k) if reverse else (row >= k)
        shift = n - k if reverse else k
        b = b + a * jnp.where(ok, pltpu.roll(b, shift, 0), 0.0)
        a = a * jnp.where(ok, pltpu.roll(a, shift, 0), 1.0)
        k *= 2
    return a, b


def _neighbour(v, edge, reverse):
    n = v.shape[0]
    row = lax.broadcasted_iota(jnp.int32, v.shape, 0)
    if reverse:
        return jnp.where(row < n - 1, pltpu.roll(v, n - 1, 0), edge)
    return jnp.where(row >= 1, pltpu.roll(v, 1, 0), edge)


def _scan_call(rows, body, name, ins, in_specs, n_out, width, adjoint):
    n_all, n_lat = rows.n_all, rows.n_lat

    def tile(d, s):
        s = n_all - 1 - s if adjoint else s
        return jnp.where(d == 0, (s + n_lat) % n_all, n_all - 1 - s)

    def per_dir(d, s):
        return (d, tile(d, s), 0)

    specs = [pl.BlockSpec((None, rows.tm, width), per_dir) if kind == "dir" else
             pl.BlockSpec((rows.tm, width), lambda d, s: (tile(d, s), 0)) for kind in in_specs]
    shape = jax.ShapeDtypeStruct((2, rows.t_all, width), F32)
    return pl.pallas_call(
        body, name=name, grid=(2, n_all), in_specs=specs, out_specs=[pl.BlockSpec((None, rows.tm, width), per_dir)] * n_out,
        out_shape=[shape] * n_out, scratch_shapes=[pltpu.VMEM((SUBLANES, width), F32)],
        compiler_params=_params("arbitrary", "arbitrary"))(*ins)


def lru_scan(rows, a, bb, name):
    width = a.shape[2]
    n = rows.tm

    def body(a_ref, bb_ref, h_ref, hp_ref, carry):
        d, s = pl.program_id(0), pl.program_id(1)

        @pl.when(s == 0)
        def _():
            carry[...] = jnp.zeros_like(carry)

        def run(reverse):
            for cs in _chunks(width, LANES):
                cum, h0 = _tile_scan(a_ref[:, cs], bb_ref[:, cs], reverse)
                h_in = carry[0:1, cs]
                h = h0 + cum * h_in
                h_ref[:, cs] = h
                hp_ref[:, cs] = _neighbour(h, h_in, reverse)
                carry[0:1, cs] = h[0:1] if reverse else h[n - 1:n]

        pl.when(d == 0)(lambda: run(False))
        pl.when(d == 1)(lambda: run(True))

    return _scan_call(rows, body, name, [a, bb], ["dir", "dir"], 2, width, adjoint=False)


def lru_scan_bwd(rows, dh, a, hp, name):
    width = a.shape[2]
    n = rows.tm

    def body(dh_ref, a_ref, hp_ref, da_ref, dbb_ref, carry):
        d, s = pl.program_id(0), pl.program_id(1)

        @pl.when(s == 0)
        def _():
            carry[...] = jnp.zeros_like(carry)

        def run(reverse):
            for cs in _chunks(width, LANES):
                av = a_ref[:, cs]
                cum, g0 = _tile_scan(_neighbour(av, 1.0, reverse), dh_ref[:, cs], reverse)
                g = g0 + cum * carry[0:1, cs]
                da_ref[:, cs] = g * hp_ref[:, cs]
                dbb_ref[:, cs] = g
                carry[0:1, cs] = (av * g)[0:1] if reverse else (av * g)[n - 1:n]

        pl.when(d == 0)(lambda: run(True))
        pl.when(d == 1)(lambda: run(False))

    return _scan_call(rows, body, name, [dh, a, hp], ["shared", "dir", "dir"], 2, width, adjoint=True)


def mod_matmul(c9, w_shard, b_shard, name):
    n = w_shard.shape[1]
    tn = _pick(n, 768)

    def body(c_ref, w_ref, b_ref, o_ref):
        act = _silu_and_grad(c_ref[...])[0]
        o_ref[...] = jnp.dot(act, w_ref[...], preferred_element_type=F32, precision=lax.Precision.HIGHEST) + b_ref[...]

    return pl.pallas_call(
        body, name=name, grid=(n // tn,),
        in_specs=[pl.BlockSpec(c9.shape, lambda j: (0, 0)), pl.BlockSpec((w_shard.shape[0], tn), lambda j: (0, j)),
                  pl.BlockSpec((1, tn), lambda j: (0, j))],
        out_specs=pl.BlockSpec((c9.shape[0], tn), lambda j: (0, j)), out_shape=jax.ShapeDtypeStruct((c9.shape[0], n), F32),
        compiler_params=_params("arbitrary"))(c9, w_shard, b_shard)


def mod_matmul_bwd(c9, d9, w_shard, name):
    n = w_shard.shape[1]
    tn = _pick(n, 768)
    steps = n // tn

    def body(c_ref, d_ref, w_ref, gw_ref, gc_ref):
        j = pl.program_id(0)
        act, dact = _silu_and_grad(c_ref[...])
        dv = d_ref[...]
        gw_ref[...] = lax.dot_general(act, dv, (((0,), (0,)), ((), ())), preferred_element_type=F32, precision=lax.Precision.HIGHEST)
        part = lax.dot_general(dv, w_ref[...], (((1,), (1,)), ((), ())), preferred_element_type=F32, precision=lax.Precision.HIGHEST)
        _accumulate(gc_ref, part * dact, j == 0)

    return pl.pallas_call(
        body, name=name, grid=(steps,),
        in_specs=[pl.BlockSpec(c9.shape, lambda j: (0, 0)), pl.BlockSpec((c9.shape[0], tn), lambda j: (0, j)),
                  pl.BlockSpec((w_shard.shape[0], tn), lambda j: (0, j))],
        out_specs=[pl.BlockSpec((w_shard.shape[0], tn), lambda j: (0, j)), pl.BlockSpec(c9.shape, lambda j: (0, 0))],
        out_shape=[jax.ShapeDtypeStruct(w_shard.shape, F32), jax.ShapeDtypeStruct(c9.shape, F32)],
        compiler_params=_params("arbitrary"))(c9, d9, w_shard)


def _row_tile(n_rows, n_cols):
    return _pick(n_rows, max(2 * SUBLANES, (256 * 1024) // n_cols), 2 * SUBLANES)


def sum_parts(parts, name):
    n, n_rows, n_cols = parts.shape
    tr = _row_tile(n_rows, n_cols)

    def body(p_ref, o_ref):
        total = p_ref[0].astype(F32)
        for k in range(1, n):
            total = total + p_ref[k].astype(F32)
        o_ref[...] = total

    return pl.pallas_call(
        body, name=name, grid=(n_rows // tr,), in_specs=[pl.BlockSpec((n, tr, n_cols), lambda i: (0, i, 0))],
        out_specs=pl.BlockSpec((tr, n_cols), lambda i: (i, 0)), out_shape=jax.ShapeDtypeStruct((n_rows, n_cols), F32),
        compiler_params=_params("arbitrary"))(parts)


def _adamw_update(g, w_ref, m_ref, v_ref, g_ref, d_ref, m2_ref, v2_ref):
    m2 = ADAM_B1 * m_ref[...] + (1.0 - ADAM_B1) * g
    v2 = ADAM_B2 * v_ref[...] + (1.0 - ADAM_B2) * (g * g)
    m_hat = m2 / (1.0 - ADAM_B1 ** ADAM_STEP)
    v_hat = v2 / (1.0 - ADAM_B2 ** ADAM_STEP)
    g_ref[...] = g
    d_ref[...] = -ADAM_LR * (m_hat / (jnp.sqrt(v_hat) + ADAM_EPS) + ADAM_WD * w_ref[...])
    m2_ref[...] = m2
    v2_ref[...] = v2


def adamw(parts, w, m, v, name):
    n, n_rows, n_cols = parts.shape
    tr = _row_tile(n_rows, n_cols)

    def body(p_ref, *refs):
        g = p_ref[0].astype(F32)
        for k in range(1, n):
            g = g + p_ref[k].astype(F32)
        _adamw_update(g, *refs)

    blk = pl.BlockSpec((tr, n_cols), lambda i: (i, 0))
    shape = jax.ShapeDtypeStruct((n_rows, n_cols), F32)
    return pl.pallas_call(
        body, name=name, grid=(n_rows // tr,), in_specs=[pl.BlockSpec((n, tr, n_cols), lambda i: (0, i, 0)), blk, blk, blk],
        out_specs=[blk] * 4, out_shape=[shape] * 4, compiler_params=_params("arbitrary"))(parts, w, m, v)


def adamw_pair(part, w, m, v, name):
    n_rows, n_cols = part.shape
    tr = _row_tile(n_rows, n_cols)
    steps = n_rows // tr

    def body(p_ref, w_ref, m_ref, v_ref, g_ref, d_ref, m2_ref, v2_ref, inbox, send_sems, recv_sems, credits):
        i = pl.program_id(0)
        slot = i % 2
        sibling = (lax.axis_index("x"), lax.axis_index("y"), 1 - lax.axis_index("c"))

        @pl.when(i >= 2)
        def _():
            pl.semaphore_wait(credits.at[slot], 1)

        copy = pltpu.make_async_remote_copy(src_ref=p_ref, dst_ref=inbox.at[slot], send_sem=send_sems.at[slot],
                                            recv_sem=recv_sems.at[slot], device_id=sibling, device_id_type=MESH)
        copy.start()
        copy.wait_recv()
        copy.wait_send()
        _adamw_update(p_ref[...] + inbox[slot], w_ref, m_ref, v_ref, g_ref, d_ref, m2_ref, v2_ref)

        @pl.when(i + 2 < steps)
        def _():
            pl.semaphore_signal(credits.at[slot], inc=1, device_id=sibling, device_id_type=MESH)

    blk = pl.BlockSpec((tr, n_cols), lambda i: (i, 0))
    shape = jax.ShapeDtypeStruct((n_rows, n_cols), F32)
    return pl.pallas_call(
        body, name=name, grid=(steps,), in_specs=[blk] * 4, out_specs=[blk] * 4, out_shape=[shape] * 4,
        scratch_shapes=[pltpu.VMEM((2, tr, n_cols), F32), pltpu.SemaphoreType.DMA((2,)), pltpu.SemaphoreType.DMA((2,)),
                        pltpu.SemaphoreType.REGULAR((2,))],
        compiler_params=_params("arbitrary"))(part, w, m, v)


def _pack(arrays, row_multiple=SUBLANES):
    flat = [jnp.pad(a.reshape(-1), (0, -a.size % LANES)) for a in arrays]
    buf = jnp.concatenate(flat)
    buf = jnp.pad(buf, (0, -buf.size % (row_multiple * LANES)))
    return buf.reshape(-1, LANES)


def _unpack(buf, shapes):
    lead = buf.shape[:-2]
    flat = buf.reshape(lead + (-1,))
    out, pos = [], 0
    for shape in shapes:
        size = 1
        for s in shape:
            size *= s
        out.append(flat[..., pos:pos + size].reshape(lead + tuple(shape)))
        pos += size + (-size % LANES)
    return out


def _pos_embedding(seq_len, dim):
    rows = seq_len // GRID_W
    q = dim // 4
    omega = 1.0 / (10000.0 ** (jnp.arange(q, dtype=F32) / q))
    er = jnp.arange(rows).astype(F32)[:, None] * omega
    ec = jnp.arange(GRID_W).astype(F32)[:, None] * omega
    by_row = jnp.broadcast_to(jnp.concatenate([jnp.sin(er), jnp.cos(er)], axis=-1)[:, None, :], (rows, GRID_W, 2 * q))
    by_col = jnp.broadcast_to(jnp.concatenate([jnp.sin(ec), jnp.cos(ec)], axis=-1)[None, :, :], (rows, GRID_W, 2 * q))
    return jnp.concatenate([by_row, by_col], axis=-1).reshape(rows * GRID_W, dim)


def _cols_from_shards(g):
    return jnp.transpose(g, (1, 0, 2)).reshape(g.shape[1], -1)


def _shards_from_cols(w, n_shards=4):
    k, n = w.shape
    return jnp.transpose(w.reshape(k, n_shards, n // n_shards), (1, 0, 2))


def kernel(x, c, ctx, c_ctx, w_mod, b_mod, g_n1, w_ffn1_up, w_ffn1_down, g_n2, w_in, b_in, w_dw, b_dw, g_ln, b_ln, w_conf_out, w_lru_conv, b_lru_conv, w_rec_gate, b_rec_gate, w_in_gate, b_in_gate, lru_lambda, w_lru_out, w_out, g_n3, w_ffn2_up, w_ffn2_down, g_final, loss_target, m_c_ctx, m_w_mod, m_b_mod, m_g_n1, m_w_ffn1_up, m_w_ffn1_down, m_g_n2, m_w_in, m_b_in, m_w_dw, m_b_dw, m_g_ln, m_b_ln, m_w_conf_out, m_w_lru_conv, m_b_lru_conv, m_w_rec_gate, m_b_rec_gate, m_w_in_gate, m_b_in_gate, m_lru_lambda, m_w_lru_out, m_w_out, m_g_n3, m_w_ffn2_up, m_w_ffn2_down, m_g_final, v_c_ctx, v_w_mod, v_b_mod, v_g_n1, v_w_ffn1_up, v_w_ffn1_down, v_g_n2, v_w_in, v_b_in, v_w_dw, v_b_dw, v_g_ln, v_b_ln, v_w_conf_out, v_w_lru_conv, v_b_lru_conv, v_w_rec_gate, v_b_rec_gate, v_w_in_gate, v_b_in_gate, v_lru_lambda, v_w_lru_out, v_w_out, v_g_n3, v_w_ffn2_up, v_w_ffn2_down, v_g_final):
    given = dict(locals())
    wt = {n: given[n] for n in W_NAMES}
    mom = {n: given["m_" + n] for n in W_NAMES}
    var = {n: given["v_" + n] for n in W_NAMES}

    def flat2(a):
        if a.ndim == 1:
            return a.reshape(1, -1)
        a = a[0]
        return a if a.ndim == 2 else a.reshape(-1, a.shape[-1])

    t_lat, d = x.shape[1], x.shape[2]
    t_ctx = ctx.shape[1]
    rows = Rows(t_lat, t_ctx, d)
    xi, yi, ci = lax.axis_index("x"), lax.axis_index("y"), lax.axis_index("c")
    me, chip = 4 * xi + 2 * yi + ci, 2 * xi + yi
    lru_w = b_lru_conv.shape[-1]
    n_blk = lru_w // LANES
    taps = w_dw.shape[1]
    lru_taps = w_lru_conv.shape[1]
    ffn = w_ffn1_down.shape[1] * 4

    small_shapes = [(1, d)] + [flat2(wt[n]).shape for n in SMALL_SHARDED]
    small_all = exchange("all", [_pack([c] + [flat2(wt[n]) for n in SMALL_SHARDED])], False, "gather_small")[0]
    small_all = _unpack(small_all, small_shapes)
    c_all = small_all[0][:, 0, :]
    full = {n: jnp.concatenate([a[0], a[2], a[4], a[6]], axis=-1) for n, a in zip(SMALL_SHARDED, small_all[1:])}
    big = COL_SHARDED + ROW_SHARDED
    gathered = exchange("chips", [wt[n][0].astype(BF16) for n in big], False, "gather_weights")
    wb = {}
    for n, g in zip(big, gathered):
        wb[n] = _cols_from_shards(g) if n in COL_SHARDED else g.reshape(-1, g.shape[-1])

    c9 = jnp.concatenate([c_all, c_ctx.reshape(1, d), jnp.zeros((7, d), F32)], axis=0)
    mod_cols = mod_matmul(c9, w_mod[0], lax.dynamic_slice_in_dim(b_mod, chip * w_mod.shape[2], w_mod.shape[2], axis=1), "mod_matmul")
    mod_all = exchange("all", [mod_cols], False, "gather_mod")[0]
    mod9 = jnp.concatenate([mod_all[0], mod_all[2], mod_all[4], mod_all[6]], axis=-1)
    mv = jnp.stack([lax.dynamic_index_in_dim(mod9, me, 0, keepdims=False).reshape(N_MOD, d), mod9[8].reshape(N_MOD, d)])

    x_in = jnp.concatenate([x[0], ctx[0]], axis=0)
    pos = jnp.concatenate([_pos_embedding(t_lat, d), jnp.zeros((t_ctx, d), F32)], axis=0)

    def ffn_fwd(tag, h, w_up, w_down):
        g, u, act = matmul_swiglu(h, w_up, f"{tag}_up")
        return (g, u), act, matmul(act, w_down, "nn", F32, f"{tag}_down")

    x0, h1 = norm_mod(rows, x_in, g_n1, mv, 0, 1, "norm1", pos=pos)
    gu1, act1, f1 = ffn_fwd("ffn1", h1, wb["w_ffn1_up"], wb["w_ffn1_down"])
    x1, h2 = norm_mod(rows, x0, g_n2, mv, 3, 4, "norm2", resid=(f1, 2, 0.5))

    conf_w = w_conf_out.shape[1] * 4
    col_groups = [(0, 2 * conf_w), (2 * conf_w, lru_w), (2 * conf_w + lru_w, lru_w), (2 * conf_w + 2 * lru_w, 2 * d)]
    pvg, ux, gl, pgc = [matmul(h2, wb["w_in"][:, s:s + n], "nn", BF16, f"in_proj{k}", bias=b_in[:, s:s + n])
                        for k, (s, n) in enumerate(col_groups)]
    u = glu(rows, pvg, "glu")
    u2 = dwconv(rows, u, full["w_dw"], b_dw, taps // 2, F32, "conf_conv")
    s_act = ln_silu(rows, u2, g_ln, b_ln, "conf_ln")
    yc = matmul(s_act, wb["w_conf_out"], "nn", BF16, "conf_out")

    xr = dwconv(rows, ux, full["w_lru_conv"], b_lru_conv, LRU_PAD_LEFT, F32, "lru_conv")
    w_rec, w_ing = w_rec_gate[0].astype(BF16), w_in_gate[0].astype(BF16)
    wcat = jnp.concatenate([w_rec[0], w_ing[0], w_rec[1], w_ing[1]], axis=-1)

    def per_block(a):
        return jnp.transpose(a.reshape(2, n_blk, LANES), (1, 0, 2)).reshape(n_blk, 1, 2 * LANES)

    b_rec, b_ing = full["b_rec_gate"].reshape(2, n_blk, LANES), full["b_in_gate"].reshape(2, n_blk, LANES)
    bcat = jnp.concatenate([b_rec[0], b_ing[0], b_rec[1], b_ing[1]], axis=-1).reshape(n_blk, 1, 4 * LANES)
    lam = per_block(full["lru_lambda"])
    a_gate, b_gate = lru_gates(rows, xr, wcat, bcat, lam, "lru_gates")
    h_scan, h_prev = lru_scan(rows, a_gate, b_gate, "lru_scan")
    yl_in = lru_merge(rows, h_scan, gl, "lru_merge")
    yl = matmul(yl_in, wb["w_lru_out"], "nn", BF16, "lru_out")
    z = z_merge(rows, pgc, yc, yl, "z_merge")
    y = matmul(z, wb["w_out"], "nn", F32, "mix_out")

    x2, h3 = norm_mod(rows, x1, g_n3, mv, 6, 7, "norm3", resid=(y, 5, 1.0))
    gu2, act2, f2 = ffn_fwd("ffn2", h3, wb["w_ffn2_up"], wb["w_ffn2_down"])
    loss_part, dx3, dg_final = loss_head(rows, x2, f2, mv, g_final.reshape(1, d), loss_target[0], "loss_head")
    loss = lax.psum(loss_part[0, 0], ("x", "y", "c"))

    grads = {"g_final": dg_final}
    dmv = [None] * N_MOD

    def ffn_bwd(tag, dxn, x_prev, h, gu, act, f, w_up, w_down, g, ks):
        k_shift, k_scale, k_gate = ks
        df, dmv[k_gate] = gate_bwd(rows, dxn, f, mv, k_gate, 0.5, f"{tag}_gate_bwd")
        grads[f"w_{tag}_down"] = matmul(act, df, "tn", F32, f"{tag}_down_dw")
        dact = matmul(df, w_down, "nt", BF16, f"{tag}_down_dx")
        dgu = swiglu_bwd(rows, gu[0], gu[1], dact, f"{tag}_act_bwd")
        grads[f"w_{tag}_up"] = matmul(h, dgu, "tn", F32, f"{tag}_up_dw")
        dh = matmul(dgu, w_up, "nt", F32, f"{tag}_up_dx")
        dx, dg, dmv[k_shift], dmv[k_scale] = norm_mod_bwd(rows, x_prev, dh, dxn, g, mv, k_shift, k_scale, f"{tag}_norm_bwd")
        return dx, dg

    def grad_pieces(names):
        return [(_shards_from_cols(grads[n]) if n in COL_SHARDED else grads[n].reshape(4, -1, grads[n].shape[-1])).astype(BF16)
                for n in names]

    dx2, grads["g_n3"] = ffn_bwd("ffn2", dx3, x2, h3, gu2, act2, f2, wb["w_ffn2_up"], wb["w_ffn2_down"], g_n3, (6, 7, 8))

    dy, dmv[5] = gate_bwd(rows, dx2, y, mv, 5, 1.0, "mix_gate_bwd")
    grads["w_out"] = matmul(z, dy, "tn", F32, "mix_out_dw")
    dz = matmul(dy, wb["w_out"], "nt", BF16, "mix_out_dx")
    dyc, dyl, dpgc = z_merge_bwd(rows, pgc, yc, yl, dz, "z_merge_bwd")

    grads["w_conf_out"] = matmul(s_act, dyc, "tn", F32, "conf_out_dw")
    ds_act = matmul(dyc, wb["w_conf_out"], "nt", BF16, "conf_out_dx")
    du2, grads["g_ln"], grads["b_ln"] = ln_silu_bwd(rows, u2, ds_act, g_ln, b_ln, "conf_ln_bwd")
    grads["w_dw"], grads["b_dw"] = dwconv_bwd_w(rows, u, du2, taps, taps // 2, "conf_conv_dw")
    du = dwconv(rows, du2, full["w_dw"][::-1], None, taps - 1 - taps // 2, F32, "conf_conv_dx")
    dpvg = glu_bwd(rows, pvg, du, "glu_bwd")

    grads["w_lru_out"] = matmul(yl_in, dyl, "tn", F32, "lru_out_dw")
    dyl_in = matmul(dyl, wb["w_lru_out"], "nt", BF16, "lru_out_dx")
    dh_sum, dgl = lru_merge_bwd(rows, h_scan, gl, dyl_in, "lru_merge_bwd")
    da_gate, db_gate = lru_scan_bwd(rows, dh_sum, a_gate, h_prev, "lru_scan_bwd")
    dxr, dwcat, dbcat, dlam = lru_gates_bwd(rows, xr, wcat, bcat, lam, da_gate, db_gate, "lru_gates_bwd")
    grads["w_rec_gate"] = jnp.stack([dwcat[:, :, 0:LANES], dwcat[:, :, 2 * LANES:3 * LANES]])
    grads["w_in_gate"] = jnp.stack([dwcat[:, :, LANES:2 * LANES], dwcat[:, :, 3 * LANES:4 * LANES]])
    dbcat = dbcat.reshape(n_blk, 4, LANES)
    grads["b_rec_gate"] = jnp.stack([dbcat[:, 0], dbcat[:, 2]]).reshape(2, lru_w)
    grads["b_in_gate"] = jnp.stack([dbcat[:, 1], dbcat[:, 3]]).reshape(2, lru_w)
    grads["lru_lambda"] = jnp.transpose(dlam.reshape(n_blk, 2, LANES), (1, 0, 2)).reshape(2, lru_w)
    grads["w_lru_conv"], grads["b_lru_conv"] = dwconv_bwd_w(rows, ux, dxr, lru_taps, LRU_PAD_LEFT, "lru_conv_dw")
    dux = dwconv(rows, dxr, full["w_lru_conv"][::-1], None, lru_taps - 1 - LRU_PAD_LEFT, BF16, "lru_conv_dx")

    dproj = jnp.concatenate([dpvg, dux, dgl, dpgc], axis=1)
    grads["b_in"] = col_sum(rows, dproj, "in_proj_db")
    grads["w_in"] = matmul(h2, dproj, "tn", F32, "in_proj_dw")
    dh2 = matmul(dproj, wb["w_in"], "nt", F32, "in_proj_dx")
    dx1, grads["g_n2"], dmv[3], dmv[4] = norm_mod_bwd(rows, x1, dh2, dx2, g_n2, mv, 3, 4, "norm2_bwd")

    dx0, grads["g_n1"] = ffn_bwd("ffn1", dx1, x0, h1, gu1, act1, f1, wb["w_ffn1_up"], wb["w_ffn1_down"], g_n1, (0, 1, 2))
    grad_x = dx0[:t_lat][None]

    from_chips = dict(zip(big, exchange("chips", grad_pieces(big), True, "scatter_grads")))
    chip_sums = [sum_parts(from_chips[n].reshape(4, -1, from_chips[n].shape[-1]), f"chip_sum_{n}") for n in big]
    out = {}
    for n, p in zip(big, chip_sums):
        out[n] = adamw_pair(p, flat2(wt[n]), flat2(mom[n]), flat2(var[n]), f"adamw_{n}")

    dmod = jnp.concatenate(dmv, axis=1)
    small_names = ['g_n1', 'g_n2', 'b_in', 'b_dw', 'g_ln', 'b_ln', 'b_lru_conv', 'w_rec_gate', 'w_in_gate', 'g_n3', 'g_final'] + SMALL_SHARDED
    small_list = [grads[n] for n in small_names] + [dmod[0], dmod[1]]
    small_buf = _pack(small_list, 8 * 2 * SUBLANES)
    small_parts, dmod_all = exchange("all", [small_buf.reshape(8, -1, LANES), _pack([dmod[0]])], [True, False], "scatter_small_grads")
    small_eighth = sum_parts(small_parts, "sum_small_grads")
    small_sum = exchange("all", [small_eighth], False, "gather_small_grads")[0].reshape(small_buf.shape)
    small_sum = _unpack(small_sum, [a.shape for a in small_list])
    total = dict(zip(small_names, small_sum))
    dmod_all = _unpack(dmod_all, [dmod[0].shape])[0].reshape(8, N_MOD * d)
    dmc = small_sum[-1].reshape(1, N_MOD * d)
    total["b_mod"] = small_sum[-2].reshape(1, N_MOD * d) + dmc

    d9 = jnp.concatenate([dmod_all, dmc, jnp.zeros((7, N_MOD * d), F32)], axis=0)
    d9_cols = lax.dynamic_slice_in_dim(d9, chip * w_mod.shape[2], w_mod.shape[2], axis=1)
    g_wmod, dc9 = mod_matmul_bwd(c9, d9_cols, w_mod[0], "mod_matmul_bwd")
    dc_all = exchange("all", [dc9[8:16]], False, "gather_dc")[0]
    total["c_ctx"] = sum_parts(jnp.stack([dc_all[0], dc_all[2], dc_all[4], dc_all[6]]), "sum_dc")[0:1]
    out["w_mod"] = adamw(g_wmod[None], w_mod[0], m_w_mod[0], v_w_mod[0], "adamw_w_mod")

    small_all_names = [n for n in W_NAMES if n not in out]
    g_local, shapes = [], []
    for n in small_all_names:
        g = total[n].reshape(flat2(wt[n]).shape[:-1] + (-1,)) if n in SMALL_SHARDED else total[n].reshape(flat2(wt[n]).shape)
        if n in SMALL_SHARDED:
            width = wt[n].shape[-1]
            g = lax.dynamic_slice_in_dim(g, chip * width, width, axis=1)
        g_local.append(g)
        shapes.append(g.shape)
    packed = adamw(_pack(g_local)[None], _pack([flat2(wt[n]) for n in small_all_names]),
                   _pack([flat2(mom[n]) for n in small_all_names]), _pack([flat2(var[n]) for n in small_all_names]), "adamw_small")
    for k, n in enumerate(small_all_names):
        out[n] = [_unpack(p, shapes)[k] for p in packed]

    results = [loss, grad_x]
    for k in range(4):
        results += [out[n][k].reshape(wt[n].shape) for n in W_NAMES]
    return tuple(results)
```

```python
import functools

import jax
import jax.numpy as jnp
from jax import lax
from jax.experimental import pallas as pl
from jax.experimental.pallas import tpu as pltpu
from jax.experimental.pallas import tpu_sc as plsc

F32 = jnp.float32
BF16 = jnp.bfloat16
MESH = pl.DeviceIdType.MESH

EPS = 1e-6
GRID_W = 64
N_MOD = 9
LRU_C = 8.0
LRU_PAD_LEFT = 2
ADAM_LR, ADAM_B1, ADAM_B2, ADAM_EPS, ADAM_WD, ADAM_STEP = 0.001, 0.9, 0.999, 1e-08, 0.01, 10

LANES = 128
SUBLANES = 8
HALO = 16
VMEM_LIMIT = 56 * 1024 * 1024

W_NAMES = ['c_ctx', 'w_mod', 'b_mod', 'g_n1', 'w_ffn1_up', 'w_ffn1_down', 'g_n2', 'w_in', 'b_in', 'w_dw', 'b_dw',
           'g_ln', 'b_ln', 'w_conf_out', 'w_lru_conv', 'b_lru_conv', 'w_rec_gate', 'b_rec_gate', 'w_in_gate',
           'b_in_gate', 'lru_lambda', 'w_lru_out', 'w_out', 'g_n3', 'w_ffn2_up', 'w_ffn2_down', 'g_final']
COL_SHARDED = ['w_ffn1_up', 'w_in', 'w_ffn2_up']
ROW_SHARDED = ['w_ffn1_down', 'w_conf_out', 'w_lru_out', 'w_out', 'w_ffn2_down']
SMALL_SHARDED = ['w_dw', 'w_lru_conv', 'b_rec_gate', 'b_in_gate', 'lru_lambda']


def _params(*semantics):
    return pltpu.CompilerParams(dimension_semantics=semantics, vmem_limit_bytes=VMEM_LIMIT)


def _pick(n, target, mult=LANES):
    best = None
    for t in range(mult, min(n, target) + 1, mult):
        if n % t == 0:
            best = t
    return best or n


def _chunks(width, target=512):
    w = _pick(width, target)
    return [slice(s, s + w) for s in range(0, width, w)]


def _sigmoid(x):
    return jax.nn.sigmoid(x)


def _silu_and_grad(x):
    s = _sigmoid(x)
    return x * s, s * (1.0 + x * (1.0 - s))


_GELU_K = 0.7978845608028654


def _gelu_and_grad(x):
    t = jnp.tanh(_GELU_K * (x + 0.044715 * (x * x * x)))
    return 0.5 * x * (1.0 + t), 0.5 * (1.0 + t) + 0.5 * x * (1.0 - t * t) * _GELU_K * (1.0 + 3 * 0.044715 * x * x)


def _neg_expm1(z):
    series = -(z * (1.0 + z * (0.5 + z * (1.0 / 6.0 + z * (1.0 / 24.0)))))
    return jnp.where(z > -0.01, series, 1.0 - jnp.exp(z))


def _softplus(x):
    return jnp.maximum(x, 0.0) + jnp.log1p(jnp.exp(-jnp.abs(x)))


def _accumulate(ref, value, first):
    @pl.when(first)
    def _():
        ref[...] = value

    @pl.when(jnp.logical_not(first))
    def _():
        ref[...] += value


def _colsum(x):
    return jnp.sum(x, axis=0, keepdims=True)


class Rows:
    def __init__(self, t_lat, t_ctx, d_model):
        self.tm = _pick(t_ctx, 256, HALO)
        assert t_lat % self.tm == 0 and t_ctx % self.tm == 0
        self.n_lat = t_lat // self.tm
        self.n_all = (t_lat + t_ctx) // self.tm
        self.t_all = t_lat + t_ctx
        self.d = d_model

    def seg(self, i):
        return jnp.where(i >= self.n_lat, 1, 0)

    def seg_first(self, i):
        return jnp.logical_or(i == 0, i == self.n_lat)

    def seg_last(self, i):
        return jnp.logical_or(i == self.n_lat - 1, i == self.n_all - 1)

    def row(self, width, col=0):
        return pl.BlockSpec((self.tm, width), lambda i: (i, col))

    def vec(self, width):
        return pl.BlockSpec((1, width), lambda i: (0, 0))

    def full(self, shape):
        return pl.BlockSpec(shape, lambda i: (0,) * len(shape))

    def mod(self):
        return pl.BlockSpec((None, N_MOD, self.d), lambda i: (self.seg(i), 0, 0))

    def segsum(self):
        return pl.BlockSpec((None, 1, self.d), lambda i: (self.seg(i), 0, 0))

    def segsum_shape(self):
        return jax.ShapeDtypeStruct((2, 1, self.d), F32)

    def halo(self, width, side):
        per = self.tm // HALO
        last = self.t_all // HALO - 1
        if side < 0:
            return pl.BlockSpec((HALO, width), lambda i: (jnp.maximum(i * per - 1, 0), 0))
        return pl.BlockSpec((HALO, width), lambda i: (jnp.minimum((i + 1) * per, last), 0))

    def call(self, body, name, in_specs, out_specs, out_shape, scratch=()):
        return pl.pallas_call(body, name=name, grid=(self.n_all,), in_specs=in_specs, out_specs=out_specs,
                              out_shape=out_shape, scratch_shapes=list(scratch), compiler_params=_params("arbitrary"))


_PEER_FLIPS = {
    "chips": [(1, 0, 0), (0, 1, 0), (1, 1, 0)],
    "sibling": [(0, 0, 1)],
    "all": [(0, 0, 1), (0, 1, 0), (0, 1, 1), (1, 0, 0), (1, 0, 1), (1, 1, 0), (1, 1, 1)],
}


def _slot(kind, x, y, c):
    return {"chips": 2 * x + y, "sibling": c, "all": 4 * x + 2 * y + c}[kind]


def exchange(kind, srcs, indexed, name):
    flips = _PEER_FLIPS[kind]
    n_slots = len(flips) + 1
    n_t = len(srcs)
    by_slot = list(indexed) if isinstance(indexed, (list, tuple)) else [indexed] * n_t
    out_shapes = [jax.ShapeDtypeStruct(s.shape if ix else (n_slots,) + s.shape, s.dtype) for s, ix in zip(srcs, by_slot)]

    def body(*refs):
        src_refs, dst_refs = refs[:n_t], refs[n_t:2 * n_t]
        send_sems, recv_sems, local_sems = refs[2 * n_t:]
        x, y, c = lax.axis_index("x"), lax.axis_index("y"), lax.axis_index("c")
        me = _slot(kind, x, y, c)

        def piece(t, k):
            return src_refs[t].at[k] if by_slot[t] else src_refs[t]

        peers = [(1 - x if fx else x, 1 - y if fy else y, 1 - c if fc else c) for fx, fy, fc in flips]
        local = [pltpu.make_async_copy(piece(t, me), dst_refs[t].at[me], local_sems.at[t]) for t in range(n_t)]
        for cp in local:
            cp.start()
        sends = []
        for t in range(n_t):
            for p, peer in enumerate(peers):
                sends.append(pltpu.make_async_remote_copy(
                    src_ref=piece(t, _slot(kind, *peer)), dst_ref=dst_refs[t].at[me],
                    send_sem=send_sems.at[t, p], recv_sem=recv_sems.at[t, p], device_id=peer, device_id_type=MESH))
        for cp in sends:
            cp.start()
        for t in range(n_t):
            for p, peer in enumerate(peers):
                theirs = _slot(kind, *peer)
                pltpu.make_async_remote_copy(
                    src_ref=piece(t, me), dst_ref=dst_refs[t].at[theirs],
                    send_sem=send_sems.at[t, p], recv_sem=recv_sems.at[t, p], device_id=peer, device_id_type=MESH).wait_recv()
        for cp in sends:
            cp.wait_send()
        for cp in local:
            cp.wait()

    any_spec = pl.BlockSpec(memory_space=pl.ANY)
    outs = pl.pallas_call(
        body, name=name, out_shape=out_shapes, in_specs=[any_spec] * n_t, out_specs=[any_spec] * n_t,
        scratch_shapes=[pltpu.SemaphoreType.DMA((n_t, len(flips))), pltpu.SemaphoreType.DMA((n_t, len(flips))),
                        pltpu.SemaphoreType.DMA((n_t,))],
        compiler_params=pltpu.CompilerParams(has_side_effects=True),
    )(*srcs)
    return list(outs)


def exchange_by_sequencer(srcs, indexed, name, collective_id):
    flips = _PEER_FLIPS["chips"]
    n_t, n_p = len(srcs), len(flips)
    src_refs = [jax.new_ref(s, memory_space=pltpu.MemorySpace.HBM) for s in srcs]
    land_refs = [jax.empty_ref(jax.ShapeDtypeStruct(s.shape if indexed else (n_p + 1,) + s.shape, s.dtype),
                               memory_space=pltpu.MemorySpace.HBM) for s in srcs]

    @pl.kernel(mesh=plsc.ScalarSubcoreMesh(axis_name="sequencer", num_cores=1), name=name,
               scratch_types=(pltpu.SemaphoreType.DMA((n_t, n_p)), pltpu.SemaphoreType.DMA((n_t, n_p)), pltpu.SemaphoreType.DMA((n_t,))),
               compiler_params=pltpu.CompilerParams(collective_id=collective_id))
    def launch(send_sems, recv_sems, local_sems):
        x, y, c = lax.axis_index("x"), lax.axis_index("y"), lax.axis_index("c")
        peers = [(1 - x if fx else x, 1 - y if fy else y, c) for fx, fy, _ in flips]
        barrier = pltpu.get_barrier_semaphore()
        for peer in peers:
            pl.semaphore_signal(barrier, inc=1, device_id=peer, device_id_type=MESH)
        pl.semaphore_wait(barrier, n_p)
        me = _slot("chips", x, y, c)

        def piece(t, k):
            return src_refs[t].at[k] if indexed else src_refs[t]

        local = [pltpu.make_async_copy(piece(t, me), land_refs[t].at[me], local_sems.at[t]) for t in range(n_t)]
        for cp in local:
            cp.start()
        sends = []
        for t in range(n_t):
            for p, peer in enumerate(peers):
                sends.append(pltpu.make_async_remote_copy(
                    src_ref=piece(t, _slot("chips", *peer)), dst_ref=land_refs[t].at[me],
                    send_sem=send_sems.at[t, p], recv_sem=recv_sems.at[t, p], device_id=peer, device_id_type=MESH))
        for cp in sends:
            cp.start()
        for t in range(n_t):
            for p, peer in enumerate(peers):
                pltpu.make_async_remote_copy(
                    src_ref=piece(t, me), dst_ref=land_refs[t].at[_slot("chips", *peer)],
                    send_sem=send_sems.at[t, p], recv_sem=recv_sems.at[t, p], device_id=peer, device_id_type=MESH).wait_recv()
        for cp in sends:
            cp.wait_send()
        for cp in local:
            cp.wait()

    launch()
    return [r[...] for r in land_refs]


def matmul(a, b, mode, out_dtype, name, bias=None):
    if mode == "nn":
        (n_i, n_r), (_, n_j) = a.shape, b.shape
    elif mode == "nt":
        (n_i, n_r), (n_j, _) = a.shape, b.shape
    else:
        (n_r, n_i), (_, n_j) = a.shape, b.shape
    ti = _pick(n_i, 1408)
    tj = _pick(n_j, 1664)
    tr = n_r if (mode == "nn" and n_r <= 2816) else _pick(n_r, 768 if mode == "tn" else 1664)
    steps = n_r // tr
    bytes_a, bytes_b = a.size * a.dtype.itemsize, b.size * b.dtype.itemsize
    j_outer = steps == 1 and bytes_b + (n_j // tj) * bytes_a < bytes_a + (n_i // ti) * bytes_b

    def at(fn):
        return (lambda j, i, r: fn(i, j, r)) if j_outer else fn

    a_spec = {"nn": pl.BlockSpec((ti, tr), at(lambda i, j, r: (i, r))), "nt": pl.BlockSpec((ti, tr), at(lambda i, j, r: (i, r))),
              "tn": pl.BlockSpec((tr, ti), at(lambda i, j, r: (r, i)))}[mode]
    b_spec = {"nn": pl.BlockSpec((tr, tj), at(lambda i, j, r: (r, j))), "nt": pl.BlockSpec((tj, tr), at(lambda i, j, r: (j, r))),
              "tn": pl.BlockSpec((tr, tj), at(lambda i, j, r: (r, j)))}[mode]
    dims = {"nn": (((1,), (0,)), ((), ())), "nt": (((1,), (1,)), ((), ())), "tn": (((0,), (0,)), ((), ()))}[mode]

    def body(*refs):
        a_ref, b_ref = refs[:2]
        bias_ref = refs[2] if bias is not None else None
        o_ref = refs[3] if bias is not None else refs[2]
        prod = lax.dot_general(a_ref[...].astype(BF16), b_ref[...].astype(BF16), dims, preferred_element_type=F32)

        def finish(total):
            if bias_ref is not None:
                total = total + bias_ref[...]
            o_ref[...] = total.astype(out_dtype)

        if steps == 1:
            finish(prod)
        else:
            acc = refs[-1]
            r = pl.program_id(2)

            @pl.when(r == 0)
            def _():
                acc[...] = prod

            @pl.when(r > 0)
            def _():
                acc[...] += prod

            @pl.when(r == steps - 1)
            def _():
                finish(acc[...])

    in_specs = [a_spec, b_spec] + ([pl.BlockSpec((1, tj), at(lambda i, j, r: (0, j)))] if bias is not None else [])
    args = (a, b) + ((bias,) if bias is not None else ())
    grid = (n_j // tj, n_i // ti, steps) if j_outer else (n_i // ti, n_j // tj, steps)
    return pl.pallas_call(
        body, name=name, grid=grid, in_specs=in_specs,
        out_specs=pl.BlockSpec((ti, tj), at(lambda i, j, r: (i, j))), out_shape=jax.ShapeDtypeStruct((n_i, n_j), out_dtype),
        scratch_shapes=[pltpu.VMEM((ti, tj), F32)] if steps > 1 else [],
        compiler_params=_params("arbitrary", "arbitrary", "arbitrary"),
    )(*args)


def matmul_swiglu(h, w_up, name):
    n_t, d = h.shape
    f = w_up.shape[1] // 2
    ti, tj = _pick(n_t, 384), _pick(f, 1408)
    n_j = f // tj

    def body(h_ref, wg_ref, wu_ref, g_ref, u_ref, act_ref):
        hv = h_ref[...]
        g = jnp.dot(hv, wg_ref[...], preferred_element_type=F32)
        u = jnp.dot(hv, wu_ref[...], preferred_element_type=F32)
        g_ref[...] = g.astype(BF16)
        u_ref[...] = u.astype(BF16)
        act_ref[...] = (_silu_and_grad(g)[0] * u).astype(BF16)

    out = pl.BlockSpec((ti, tj), lambda j, i: (i, j))
    shape = jax.ShapeDtypeStruct((n_t, f), BF16)
    return pl.pallas_call(
        body, name=name, grid=(n_j, n_t // ti),
        in_specs=[pl.BlockSpec((ti, d), lambda j, i: (i, 0)), pl.BlockSpec((d, tj), lambda j, i: (0, j)),
                  pl.BlockSpec((d, tj), lambda j, i: (0, j + n_j))],
        out_specs=[out, out, out], out_shape=[shape, shape, shape], compiler_params=_params("arbitrary", "arbitrary"),
    )(h, w_up, w_up)


def norm_mod(rows, x, g, mv, shift_k, scale_k, name, resid=None, pos=None):
    d = rows.d
    changed = resid is not None or pos is not None

    def body(*refs):
        refs = list(refs)
        x_ref, g_ref, mv_ref = refs[:3]
        rest = refs[3:]
        xv = x_ref[...]
        if pos is not None:
            xv = xv + rest.pop(0)[...]
        if resid is not None:
            xv = xv + resid[2] * mv_ref[resid[1]:resid[1] + 1, :] * rest.pop(0)[...]
        if changed:
            rest.pop(0)[...] = xv
        r = lax.rsqrt(jnp.mean(xv * xv, axis=-1, keepdims=True) + EPS)
        h = (xv * r) * g_ref[...]
        if shift_k is not None:
            h = h * (1.0 + mv_ref[scale_k:scale_k + 1, :]) + mv_ref[shift_k:shift_k + 1, :]
        rest.pop(0)[...] = h.astype(BF16)

    ins = [x, g, mv] + ([pos] if pos is not None else []) + ([resid[0]] if resid is not None else [])
    in_specs = [rows.row(d), rows.vec(d), rows.mod()] + [rows.row(d)] * (len(ins) - 3)
    out_shape = ([jax.ShapeDtypeStruct((rows.t_all, d), F32)] if changed else []) + [jax.ShapeDtypeStruct((rows.t_all, d), BF16)]
    outs = rows.call(body, name, in_specs, [rows.row(d)] * len(out_shape), out_shape)(*ins)
    return (outs[0], outs[1]) if changed else (None, outs[0])


def norm_mod_bwd(rows, x, dh, dxn, g, mv, shift_k, scale_k, name):
    d = rows.d

    def body(x_ref, dh_ref, dxn_ref, g_ref, mv_ref, dx_ref, dg_ref, dsh_ref, dsc_ref):
        i = pl.program_id(0)
        xv, dhv, gv = x_ref[...], dh_ref[...], g_ref[...]
        r = lax.rsqrt(jnp.mean(xv * xv, axis=-1, keepdims=True) + EPS)
        n = xv * r
        dy = dhv * (1.0 + mv_ref[scale_k:scale_k + 1, :])
        dn = dy * gv
        dx_ref[...] = dxn_ref[...] + r * (dn - n * jnp.mean(dn * n, axis=-1, keepdims=True))
        _accumulate(dg_ref, _colsum(dy * n), i == 0)
        _accumulate(dsh_ref, _colsum(dhv), rows.seg_first(i))
        _accumulate(dsc_ref, _colsum(dhv * (n * gv)), rows.seg_first(i))

    return rows.call(
        body, name, [rows.row(d), rows.row(d), rows.row(d), rows.vec(d), rows.mod()],
        [rows.row(d), rows.vec(d), rows.segsum(), rows.segsum()],
        [jax.ShapeDtypeStruct((rows.t_all, d), F32), jax.ShapeDtypeStruct((1, d), F32), rows.segsum_shape(), rows.segsum_shape()],
    )(x, dh, dxn, g, mv)


def gate_bwd(rows, dxn, f, mv, gate_k, scale, name):
    d = rows.d

    def body(dxn_ref, f_ref, mv_ref, df_ref, dgate_ref):
        i = pl.program_id(0)
        dv = dxn_ref[...]
        df_ref[...] = (scale * mv_ref[gate_k:gate_k + 1, :] * dv).astype(BF16)
        _accumulate(dgate_ref, scale * _colsum(dv * f_ref[...]), rows.seg_first(i))

    return rows.call(body, name, [rows.row(d), rows.row(d), rows.mod()], [rows.row(d), rows.segsum()],
                     [jax.ShapeDtypeStruct((rows.t_all, d), BF16), rows.segsum_shape()])(dxn, f, mv)


def swiglu_bwd(rows, g, u, dact, name):
    f = g.shape[1]

    def body(g_ref, u_ref, da_ref, o_ref):
        for cs in _chunks(f):
            act, dact_dg = _silu_and_grad(g_ref[:, cs].astype(F32))
            dav = da_ref[:, cs].astype(F32)
            o_ref[:, cs] = (dav * u_ref[:, cs].astype(F32) * dact_dg).astype(BF16)
            o_ref[:, slice(f + cs.start, f + cs.stop)] = (dav * act).astype(BF16)

    return rows.call(body, name, [rows.row(f), rows.row(f), rows.row(f)], rows.row(2 * f),
                     jax.ShapeDtypeStruct((rows.t_all, 2 * f), BF16))(g, u, dact)


def glu(rows, pvg, name):
    w = pvg.shape[1] // 2

    def body(v_ref, t_ref, o_ref):
        for cs in _chunks(w):
            o_ref[:, cs] = v_ref[:, cs].astype(F32) * _sigmoid(t_ref[:, cs].astype(F32))

    return rows.call(body, name, [rows.row(w, 0), rows.row(w, 1)], rows.row(w), jax.ShapeDtypeStruct((rows.t_all, w), F32))(pvg, pvg)


def glu_bwd(rows, pvg, du, name):
    w = pvg.shape[1] // 2

    def body(v_ref, t_ref, du_ref, o_ref):
        for cs in _chunks(w):
            s = _sigmoid(t_ref[:, cs].astype(F32))
            duv = du_ref[:, cs]
            o_ref[:, cs] = (duv * s).astype(BF16)
            o_ref[:, slice(w + cs.start, w + cs.stop)] = (duv * v_ref[:, cs].astype(F32) * s * (1.0 - s)).astype(BF16)

    return rows.call(body, name, [rows.row(w, 0), rows.row(w, 1), rows.row(w)], rows.row(2 * w),
                     jax.ShapeDtypeStruct((rows.t_all, 2 * w), BF16))(pvg, pvg, du)


def ln_silu(rows, u, g, b, name):
    w = u.shape[1]

    def body(u_ref, g_ref, b_ref, o_ref):
        uv = u_ref[...]
        xc = uv - jnp.mean(uv, axis=-1, keepdims=True)
        n = xc * lax.rsqrt(jnp.mean(xc * xc, axis=-1, keepdims=True) + EPS)
        o_ref[...] = _silu_and_grad(n * g_ref[...] + b_ref[...])[0].astype(BF16)

    return rows.call(body, name, [rows.row(w), rows.vec(w), rows.vec(w)], rows.row(w),
                     jax.ShapeDtypeStruct((rows.t_all, w), BF16))(u, g, b)


def ln_silu_bwd(rows, u, ds, g, b, name):
    w = u.shape[1]

    def body(u_ref, ds_ref, g_ref, b_ref, du_ref, dg_ref, db_ref):
        i = pl.program_id(0)
        uv, gv = u_ref[...], g_ref[...]
        xc = uv - jnp.mean(uv, axis=-1, keepdims=True)
        r = lax.rsqrt(jnp.mean(xc * xc, axis=-1, keepdims=True) + EPS)
        n = xc * r
        dy = ds_ref[...].astype(F32) * _silu_and_grad(n * gv + b_ref[...])[1]
        dn = dy * gv
        du_ref[...] = r * (dn - jnp.mean(dn, axis=-1, keepdims=True) - n * jnp.mean(dn * n, axis=-1, keepdims=True))
        _accumulate(dg_ref, _colsum(dy * n), i == 0)
        _accumulate(db_ref, _colsum(dy), i == 0)

    return rows.call(body, name, [rows.row(w), rows.row(w), rows.vec(w), rows.vec(w)], [rows.row(w), rows.vec(w), rows.vec(w)],
                     [jax.ShapeDtypeStruct((rows.t_all, w), F32), jax.ShapeDtypeStruct((1, w), F32),
                      jax.ShapeDtypeStruct((1, w), F32)])(u, ds, g, b)


def lru_merge(rows, h, gl, name):
    w = gl.shape[1]

    def body(h_ref, gl_ref, o_ref):
        for cs in _chunks(w):
            o_ref[:, cs] = ((h_ref[0, :, cs] + h_ref[1, :, cs]) * _gelu_and_grad(gl_ref[:, cs].astype(F32))[0]).astype(BF16)

    return rows.call(body, name, [pl.BlockSpec((2, rows.tm, w), lambda i: (0, i, 0)), rows.row(w)], rows.row(w),
                     jax.ShapeDtypeStruct((rows.t_all, w), BF16))(h, gl)


def lru_merge_bwd(rows, h, gl, dy, name):
    w = gl.shape[1]

    def body(h_ref, gl_ref, dy_ref, dh_ref, dgl_ref):
        for cs in _chunks(w):
            act, dact = _gelu_and_grad(gl_ref[:, cs].astype(F32))
            dyv = dy_ref[:, cs].astype(F32)
            dh_ref[:, cs] = dyv * act
            dgl_ref[:, cs] = (dyv * (h_ref[0, :, cs] + h_ref[1, :, cs]) * dact).astype(BF16)

    return rows.call(body, name, [pl.BlockSpec((2, rows.tm, w), lambda i: (0, i, 0)), rows.row(w), rows.row(w)],
                     [rows.row(w), rows.row(w)],
                     [jax.ShapeDtypeStruct((rows.t_all, w), F32), jax.ShapeDtypeStruct((rows.t_all, w), BF16)])(h, gl, dy)


def z_merge(rows, pgc, yc, yl, name):
    d = rows.d

    def body(gc_ref, gr_ref, yc_ref, yl_ref, o_ref):
        for cs in _chunks(d):
            o_ref[:, cs] = (_sigmoid(gc_ref[:, cs].astype(F32)) * yc_ref[:, cs].astype(F32)
                            + _sigmoid(gr_ref[:, cs].astype(F32)) * yl_ref[:, cs].astype(F32)).astype(BF16)

    return rows.call(body, name, [rows.row(d, 0), rows.row(d, 1), rows.row(d), rows.row(d)], rows.row(d),
                     jax.ShapeDtypeStruct((rows.t_all, d), BF16))(pgc, pgc, yc, yl)


def z_merge_bwd(rows, pgc, yc, yl, dz, name):
    d = rows.d

    def body(gc_ref, gr_ref, yc_ref, yl_ref, dz_ref, dyc_ref, dyl_ref, dp_ref):
        for cs in _chunks(d):
            sc, sr = _sigmoid(gc_ref[:, cs].astype(F32)), _sigmoid(gr_ref[:, cs].astype(F32))
            dzv = dz_ref[:, cs].astype(F32)
            dyc_ref[:, cs] = (dzv * sc).astype(BF16)
            dyl_ref[:, cs] = (dzv * sr).astype(BF16)
            dp_ref[:, cs] = (dzv * yc_ref[:, cs].astype(F32) * sc * (1.0 - sc)).astype(BF16)
            dp_ref[:, slice(d + cs.start, d + cs.stop)] = (dzv * yl_ref[:, cs].astype(F32) * sr * (1.0 - sr)).astype(BF16)

    return rows.call(body, name, [rows.row(d, 0), rows.row(d, 1), rows.row(d), rows.row(d), rows.row(d)],
                     [rows.row(d), rows.row(d), rows.row(2 * d)],
                     [jax.ShapeDtypeStruct((rows.t_all, d), BF16), jax.ShapeDtypeStruct((rows.t_all, d), BF16),
                      jax.ShapeDtypeStruct((rows.t_all, 2 * d), BF16)])(pgc, pgc, yc, yl, dz)


def col_sum(rows, x, name):
    w = x.shape[1]

    def body(x_ref, o_ref):
        i = pl.program_id(0)
        for cs in _chunks(w):
            _accumulate(o_ref.at[:, cs], _colsum(x_ref[:, cs].astype(F32)), i == 0)

    return rows.call(body, name, [rows.row(w)], rows.vec(w), jax.ShapeDtypeStruct((1, w), F32))(x)


def loss_head(rows, x, f, mv, g, target, name):
    d = rows.d

    def body(x_ref, f_ref, mv_ref, g_ref, t_ref, loss_ref, dx_ref, dg_ref):
        i = pl.program_id(0)
        valid = jnp.where(i < rows.n_lat, 1.0, 0.0)
        xv = x_ref[...] + 0.5 * mv_ref[8:9, :] * f_ref[...]
        gv = g_ref[...]
        r = lax.rsqrt(jnp.mean(xv * xv, axis=-1, keepdims=True) + EPS)
        n = xv * r
        err = (n * gv - t_ref[...]) * valid
        part = 0.5 * jnp.sum(jnp.mean(err * err, axis=-1, keepdims=True), axis=0, keepdims=True)
        _accumulate(loss_ref, jnp.broadcast_to(part, (1, LANES)), i == 0)
        dy = err * (1.0 / d)
        dn = dy * gv
        dx_ref[...] = r * (dn - n * jnp.mean(dn * n, axis=-1, keepdims=True))
        _accumulate(dg_ref, _colsum(dy * n), i == 0)

    target_spec = pl.BlockSpec((rows.tm, d), lambda i: (jnp.minimum(i, rows.n_lat - 1), 0))
    return rows.call(body, name, [rows.row(d), rows.row(d), rows.mod(), rows.vec(d), target_spec],
                     [rows.vec(LANES), rows.row(d), rows.vec(d)],
                     [jax.ShapeDtypeStruct((1, LANES), F32), jax.ShapeDtypeStruct((rows.t_all, d), F32),
                      jax.ShapeDtypeStruct((1, d), F32)])(x, f, mv, g, target)


def _extended(rows, i, prev_ref, cur_ref, next_ref, cs):
    prev = jnp.where(rows.seg_first(i), 0.0, prev_ref[:, cs].astype(F32))
    nxt = jnp.where(rows.seg_last(i), 0.0, next_ref[:, cs].astype(F32))
    return jnp.concatenate([prev, cur_ref[:, cs].astype(F32), nxt], axis=0)


def _shifted(ext, offset, tm):
    n = ext.shape[0]
    return pltpu.roll(ext, (-offset) % n, 0)[HALO:HALO + tm] if offset else ext[HALO:HALO + tm]


def _pad_taps(w):
    k = w.shape[0]
    return jnp.pad(w, ((0, -k % SUBLANES), (0, 0)))


def dwconv(rows, u, w, b, pad_left, out_dtype, name):
    taps, width = w.shape
    wp = _pad_taps(w)

    def body(*refs):
        prev_ref, cur_ref, next_ref, w_ref = refs[:4]
        b_ref = refs[4] if b is not None else None
        o_ref = refs[-1]
        i = pl.program_id(0)
        for cs in _chunks(width, LANES):
            ext = _extended(rows, i, prev_ref, cur_ref, next_ref, cs)
            acc = jnp.zeros((rows.tm, LANES), F32) if b_ref is None else jnp.broadcast_to(b_ref[:, cs], (rows.tm, LANES))
            for k in range(taps):
                acc = acc + w_ref[k:k + 1, cs] * _shifted(ext, k - pad_left, rows.tm)
            o_ref[:, cs] = acc.astype(out_dtype)

    ins = [u, u, u, wp] + ([b] if b is not None else [])
    in_specs = [rows.halo(width, -1), rows.row(width), rows.halo(width, 1), rows.full(wp.shape)] + ([rows.vec(width)] if b is not None else [])
    return rows.call(body, name, in_specs, rows.row(width), jax.ShapeDtypeStruct((rows.t_all, width), out_dtype))(*ins)


def dwconv_bwd_w(rows, u, dy, taps, pad_left, name):
    width = u.shape[1]
    taps_p = taps + (-taps % SUBLANES)

    def body(prev_ref, cur_ref, next_ref, dy_ref, dw_ref, db_ref):
        i = pl.program_id(0)
        tap_row = lax.broadcasted_iota(jnp.int32, (taps_p, LANES), 0)
        for cs in _chunks(width, LANES):
            ext = _extended(rows, i, prev_ref, cur_ref, next_ref, cs)
            dyv = dy_ref[:, cs]
            total = jnp.zeros((taps_p, LANES), F32)
            for k in range(taps):
                total = total + jnp.where(tap_row == k, _colsum(dyv * _shifted(ext, k - pad_left, rows.tm)), 0.0)
            _accumulate(dw_ref.at[:, cs], total, i == 0)
            _accumulate(db_ref.at[:, cs], _colsum(dyv), i == 0)

    dw, db = rows.call(body, name, [rows.halo(width, -1), rows.row(width), rows.halo(width, 1), rows.row(width)],
                       [rows.full((taps_p, width)), rows.vec(width)],
                       [jax.ShapeDtypeStruct((taps_p, width), F32), jax.ShapeDtypeStruct((1, width), F32)])(u, u, u, dy)
    return dw[:taps], db


def _gate_values(xr, pre, lam, d):
    r = _sigmoid(pre[:, (2 * d) * LANES:(2 * d + 1) * LANES])
    ig = _sigmoid(pre[:, (2 * d + 1) * LANES:(2 * d + 2) * LANES])
    sp = _softplus(-lam[:, d * LANES:(d + 1) * LANES])
    log_a = -LRU_C * r * sp
    return r, ig, sp, jnp.exp(log_a), jnp.sqrt(_neg_expm1(2.0 * log_a))


def lru_gates(rows, xr, wcat, bcat, lam, name):
    n_blk = wcat.shape[0]
    width = xr.shape[1]

    def body(xr_ref, w_ref, b_ref, lam_ref, a_ref, bb_ref):
        for h in range(n_blk):
            cs = slice(h * LANES, (h + 1) * LANES)
            xv = xr_ref[:, cs]
            pre = jnp.dot(xv.astype(BF16), w_ref[h], preferred_element_type=F32) + b_ref[h]
            for d in range(2):
                _, ig, _, a, q = _gate_values(xv, pre, lam_ref[h], d)
                a_ref[d, :, cs] = a
                bb_ref[d, :, cs] = q * (ig * xv)

    both = pl.BlockSpec((2, rows.tm, width), lambda i: (0, i, 0))
    shape = jax.ShapeDtypeStruct((2, rows.t_all, width), F32)
    return rows.call(body, name, [rows.row(width), rows.full(wcat.shape), rows.full(bcat.shape), rows.full(lam.shape)],
                     [both, both], [shape, shape])(xr, wcat, bcat, lam)


def lru_gates_bwd(rows, xr, wcat, bcat, lam, da, dbb, name):
    n_blk = wcat.shape[0]
    width = xr.shape[1]

    def body(xr_ref, w_ref, b_ref, lam_ref, da_ref, dbb_ref, dxr_ref, dw_ref, db_ref, dlam_ref):
        i = pl.program_id(0)
        for h in range(n_blk):
            cs = slice(h * LANES, (h + 1) * LANES)
            xv = xr_ref[:, cs]
            xb = xv.astype(BF16)
            wv = w_ref[h]
            pre = jnp.dot(xb, wv, preferred_element_type=F32) + b_ref[h]
            dxr = jnp.zeros_like(xv)
            dpre, dlam = [], []
            for d in range(2):
                r, ig, sp, a, q = _gate_values(xv, pre, lam_ref[h], d)
                dav, dbv = da_ref[d, :, cs], dbb_ref[d, :, cs]
                dlog_a = dav * a - dbv * (ig * xv) * (a * a) / q
                dpre.append(dlog_a * (-LRU_C * sp) * r * (1.0 - r))
                dpre.append(dbv * q * xv * ig * (1.0 - ig))
                dxr = dxr + dbv * q * ig
                dlam.append(_colsum(dlog_a * (-LRU_C * r)) * (-_sigmoid(-lam_ref[h][:, d * LANES:(d + 1) * LANES])))
            dpre = jnp.concatenate(dpre, axis=1)
            dpb = dpre.astype(BF16)
            dxr_ref[:, cs] = dxr + lax.dot_general(dpb, wv, (((1,), (1,)), ((), ())), preferred_element_type=F32)
            _accumulate(dw_ref.at[h], lax.dot_general(xb, dpb, (((0,), (0,)), ((), ())), preferred_element_type=F32), i == 0)
            _accumulate(db_ref.at[h], _colsum(dpre), i == 0)
            _accumulate(dlam_ref.at[h], jnp.concatenate(dlam, axis=1), i == 0)

    both = pl.BlockSpec((2, rows.tm, width), lambda i: (0, i, 0))
    return rows.call(
        body, name, [rows.row(width), rows.full(wcat.shape), rows.full(bcat.shape), rows.full(lam.shape), both, both],
        [rows.row(width), rows.full(wcat.shape), rows.full(bcat.shape), rows.full(lam.shape)],
        [jax.ShapeDtypeStruct((rows.t_all, width), F32), jax.ShapeDtypeStruct(wcat.shape, F32),
         jax.ShapeDtypeStruct(bcat.shape, F32), jax.ShapeDtypeStruct(lam.shape, F32)])(xr, wcat, bcat, lam, da, dbb)


def _tile_scan(a, b, reverse):
    n = a.shape[0]
    row = lax.broadcasted_iota(jnp.int32, a.shape, 0)
    k = 1
    while k < n:
        ok = (row < n - k) if reverse else (row >= k)
        shift = n - k if reverse else k
        b = b + a * jnp.where(ok, pltpu.roll(b, shift, 0), 0.0)
        a = a * jnp.where(ok, pltpu.roll(a, shift, 0), 1.0)
        k *= 2
    return a, b


def _neighbour(v, edge, reverse):
    n = v.shape[0]
    row = lax.broadcasted_iota(jnp.int32, v.shape, 0)
    if reverse:
        return jnp.where(row < n - 1, pltpu.roll(v, n - 1, 0), edge)
    return jnp.where(row >= 1, pltpu.roll(v, 1, 0), edge)


def _scan_call(rows, body, name, ins, in_specs, n_out, width, adjoint):
    n_all, n_lat = rows.n_all, rows.n_lat

    def tile(d, s):
        s = n_all - 1 - s if adjoint else s
        return jnp.where(d == 0, (s + n_lat) % n_all, n_all - 1 - s)

    def per_dir(d, s):
        return (d, tile(d, s), 0)

    specs = [pl.BlockSpec((None, rows.tm, width), per_dir) if kind == "dir" else
             pl.BlockSpec((rows.tm, width), lambda d, s: (tile(d, s), 0)) for kind in in_specs]
    shape = jax.ShapeDtypeStruct((2, rows.t_all, width), F32)
    return pl.pallas_call(
        body, name=name, grid=(2, n_all), in_specs=specs, out_specs=[pl.BlockSpec((None, rows.tm, width), per_dir)] * n_out,
        out_shape=[shape] * n_out, scratch_shapes=[pltpu.VMEM((SUBLANES, width), F32)],
        compiler_params=_params("arbitrary", "arbitrary"))(*ins)


def lru_scan(rows, a, bb, name):
    width = a.shape[2]
    n = rows.tm

    def body(a_ref, bb_ref, h_ref, hp_ref, carry):
        d, s = pl.program_id(0), pl.program_id(1)

        @pl.when(s == 0)
        def _():
            carry[...] = jnp.zeros_like(carry)

        def run(reverse):
            for cs in _chunks(width, LANES):
                cum, h0 = _tile_scan(a_ref[:, cs], bb_ref[:, cs], reverse)
                h_in = carry[0:1, cs]
                h = h0 + cum * h_in
                h_ref[:, cs] = h
                hp_ref[:, cs] = _neighbour(h, h_in, reverse)
                carry[0:1, cs] = h[0:1] if reverse else h[n - 1:n]

        pl.when(d == 0)(lambda: run(False))
        pl.when(d == 1)(lambda: run(True))

    return _scan_call(rows, body, name, [a, bb], ["dir", "dir"], 2, width, adjoint=False)


def lru_scan_bwd(rows, dh, a, hp, name):
    width = a.shape[2]
    n = rows.tm

    def body(dh_ref, a_ref, hp_ref, da_ref, dbb_ref, carry):
        d, s = pl.program_id(0), pl.program_id(1)

        @pl.when(s == 0)
        def _():
            carry[...] = jnp.zeros_like(carry)

        def run(reverse):
            for cs in _chunks(width, LANES):
                av = a_ref[:, cs]
                cum, g0 = _tile_scan(_neighbour(av, 1.0, reverse), dh_ref[:, cs], reverse)
                g = g0 + cum * carry[0:1, cs]
                da_ref[:, cs] = g * hp_ref[:, cs]
                dbb_ref[:, cs] = g
                carry[0:1, cs] = (av * g)[0:1] if reverse else (av * g)[n - 1:n]

        pl.when(d == 0)(lambda: run(True))
        pl.when(d == 1)(lambda: run(False))

    return _scan_call(rows, body, name, [dh, a, hp], ["shared", "dir", "dir"], 2, width, adjoint=True)


def mod_matmul(c9, w_shard, b_shard, name):
    n = w_shard.shape[1]
    tn = _pick(n, 768)

    def body(c_ref, w_ref, b_ref, o_ref):
        act = _silu_and_grad(c_ref[...])[0]
        o_ref[...] = jnp.dot(act, w_ref[...], preferred_element_type=F32, precision=lax.Precision.HIGHEST) + b_ref[...]

    return pl.pallas_call(
        body, name=name, grid=(n // tn,),
        in_specs=[pl.BlockSpec(c9.shape, lambda j: (0, 0)), pl.BlockSpec((w_shard.shape[0], tn), lambda j: (0, j)),
                  pl.BlockSpec((1, tn), lambda j: (0, j))],
        out_specs=pl.BlockSpec((c9.shape[0], tn), lambda j: (0, j)), out_shape=jax.ShapeDtypeStruct((c9.shape[0], n), F32),
        compiler_params=_params("arbitrary"))(c9, w_shard, b_shard)


def mod_matmul_bwd(c9, d9, w_shard, name):
    n = w_shard.shape[1]
    tn = _pick(n, 768)
    steps = n // tn

    def body(c_ref, d_ref, w_ref, gw_ref, gc_ref):
        j = pl.program_id(0)
        act, dact = _silu_and_grad(c_ref[...])
        dv = d_ref[...]
        gw_ref[...] = lax.dot_general(act, dv, (((0,), (0,)), ((), ())), preferred_element_type=F32, precision=lax.Precision.HIGHEST)
        part = lax.dot_general(dv, w_ref[...], (((1,), (1,)), ((), ())), preferred_element_type=F32, precision=lax.Precision.HIGHEST)
        _accumulate(gc_ref, part * dact, j == 0)

    return pl.pallas_call(
        body, name=name, grid=(steps,),
        in_specs=[pl.BlockSpec(c9.shape, lambda j: (0, 0)), pl.BlockSpec((c9.shape[0], tn), lambda j: (0, j)),
                  pl.BlockSpec((w_shard.shape[0], tn), lambda j: (0, j))],
        out_specs=[pl.BlockSpec((w_shard.shape[0], tn), lambda j: (0, j)), pl.BlockSpec(c9.shape, lambda j: (0, 0))],
        out_shape=[jax.ShapeDtypeStruct(w_shard.shape, F32), jax.ShapeDtypeStruct(c9.shape, F32)],
        compiler_params=_params("arbitrary"))(c9, d9, w_shard)


def _row_tile(n_rows, n_cols):
    return _pick(n_rows, max(2 * SUBLANES, (256 * 1024) // n_cols), 2 * SUBLANES)


def sum_parts(parts, name):
    n, n_rows, n_cols = parts.shape
    tr = _row_tile(n_rows, n_cols)

    def body(p_ref, o_ref):
        total = p_ref[0].astype(F32)
        for k in range(1, n):
            total = total + p_ref[k].astype(F32)
        o_ref[...] = total

    return pl.pallas_call(
        body, name=name, grid=(n_rows // tr,), in_specs=[pl.BlockSpec((n, tr, n_cols), lambda i: (0, i, 0))],
        out_specs=pl.BlockSpec((tr, n_cols), lambda i: (i, 0)), out_shape=jax.ShapeDtypeStruct((n_rows, n_cols), F32),
        compiler_params=_params("arbitrary"))(parts)


def _adamw_update(g, w_ref, m_ref, v_ref, g_ref, d_ref, m2_ref, v2_ref):
    m2 = ADAM_B1 * m_ref[...] + (1.0 - ADAM_B1) * g
    v2 = ADAM_B2 * v_ref[...] + (1.0 - ADAM_B2) * (g * g)
    m_hat = m2 / (1.0 - ADAM_B1 ** ADAM_STEP)
    v_hat = v2 / (1.0 - ADAM_B2 ** ADAM_STEP)
    g_ref[...] = g
    d_ref[...] = -ADAM_LR * (m_hat / (jnp.sqrt(v_hat) + ADAM_EPS) + ADAM_WD * w_ref[...])
    m2_ref[...] = m2
    v2_ref[...] = v2


def adamw(parts, w, m, v, name):
    n, n_rows, n_cols = parts.shape
    tr = _row_tile(n_rows, n_cols)

    def body(p_ref, *refs):
        g = p_ref[0].astype(F32)
        for k in range(1, n):
            g = g + p_ref[k].astype(F32)
        _adamw_update(g, *refs)

    blk = pl.BlockSpec((tr, n_cols), lambda i: (i, 0))
    shape = jax.ShapeDtypeStruct((n_rows, n_cols), F32)
    return pl.pallas_call(
        body, name=name, grid=(n_rows // tr,), in_specs=[pl.BlockSpec((n, tr, n_cols), lambda i: (0, i, 0)), blk, blk, blk],
        out_specs=[blk] * 4, out_shape=[shape] * 4, compiler_params=_params("arbitrary"))(parts, w, m, v)


def adamw_pair(part, w, m, v, name):
    n_rows, n_cols = part.shape
    tr = _row_tile(n_rows, n_cols)
    steps = n_rows // tr

    def body(p_ref, w_ref, m_ref, v_ref, g_ref, d_ref, m2_ref, v2_ref, inbox, send_sems, recv_sems, credits):
        i = pl.program_id(0)
        slot = i % 2
        sibling = (lax.axis_index("x"), lax.axis_index("y"), 1 - lax.axis_index("c"))

        @pl.when(i >= 2)
        def _():
            pl.semaphore_wait(credits.at[slot], 1)

        copy = pltpu.make_async_remote_copy(src_ref=p_ref, dst_ref=inbox.at[slot], send_sem=send_sems.at[slot],
                                            recv_sem=recv_sems.at[slot], device_id=sibling, device_id_type=MESH)
        copy.start()
        copy.wait_recv()
        copy.wait_send()
        _adamw_update(p_ref[...] + inbox[slot], w_ref, m_ref, v_ref, g_ref, d_ref, m2_ref, v2_ref)

        @pl.when(i + 2 < steps)
        def _():
            pl.semaphore_signal(credits.at[slot], inc=1, device_id=sibling, device_id_type=MESH)

    blk = pl.BlockSpec((tr, n_cols), lambda i: (i, 0))
    shape = jax.ShapeDtypeStruct((n_rows, n_cols), F32)
    return pl.pallas_call(
        body, name=name, grid=(steps,), in_specs=[blk] * 4, out_specs=[blk] * 4, out_shape=[shape] * 4,
        scratch_shapes=[pltpu.VMEM((2, tr, n_cols), F32), pltpu.SemaphoreType.DMA((2,)), pltpu.SemaphoreType.DMA((2,)),
                        pltpu.SemaphoreType.REGULAR((2,))],
        compiler_params=_params("arbitrary"))(part, w, m, v)


def _pack(arrays, row_multiple=SUBLANES):
    flat = [jnp.pad(a.reshape(-1), (0, -a.size % LANES)) for a in arrays]
    buf = jnp.concatenate(flat)
    buf = jnp.pad(buf, (0, -buf.size % (row_multiple * LANES)))
    return buf.reshape(-1, LANES)


def _unpack(buf, shapes):
    lead = buf.shape[:-2]
    flat = buf.reshape(lead + (-1,))
    out, pos = [], 0
    for shape in shapes:
        size = 1
        for s in shape:
            size *= s
        out.append(flat[..., pos:pos + size].reshape(lead + tuple(shape)))
        pos += size + (-size % LANES)
    return out


def _pos_embedding(seq_len, dim):
    rows = seq_len // GRID_W
    q = dim // 4
    omega = 1.0 / (10000.0 ** (jnp.arange(q, dtype=F32) / q))
    er = jnp.arange(rows).astype(F32)[:, None] * omega
    ec = jnp.arange(GRID_W).astype(F32)[:, None] * omega
    by_row = jnp.broadcast_to(jnp.concatenate([jnp.sin(er), jnp.cos(er)], axis=-1)[:, None, :], (rows, GRID_W, 2 * q))
    by_col = jnp.broadcast_to(jnp.concatenate([jnp.sin(ec), jnp.cos(ec)], axis=-1)[None, :, :], (rows, GRID_W, 2 * q))
    return jnp.concatenate([by_row, by_col], axis=-1).reshape(rows * GRID_W, dim)


def _cols_from_shards(g):
    return jnp.transpose(g, (1, 0, 2)).reshape(g.shape[1], -1)


def _shards_from_cols(w, n_shards=4):
    k, n = w.shape
    return jnp.transpose(w.reshape(k, n_shards, n // n_shards), (1, 0, 2))


def kernel(x, c, ctx, c_ctx, w_mod, b_mod, g_n1, w_ffn1_up, w_ffn1_down, g_n2, w_in, b_in, w_dw, b_dw, g_ln, b_ln, w_conf_out, w_lru_conv, b_lru_conv, w_rec_gate, b_rec_gate, w_in_gate, b_in_gate, lru_lambda, w_lru_out, w_out, g_n3, w_ffn2_up, w_ffn2_down, g_final, loss_target, m_c_ctx, m_w_mod, m_b_mod, m_g_n1, m_w_ffn1_up, m_w_ffn1_down, m_g_n2, m_w_in, m_b_in, m_w_dw, m_b_dw, m_g_ln, m_b_ln, m_w_conf_out, m_w_lru_conv, m_b_lru_conv, m_w_rec_gate, m_b_rec_gate, m_w_in_gate, m_b_in_gate, m_lru_lambda, m_w_lru_out, m_w_out, m_g_n3, m_w_ffn2_up, m_w_ffn2_down, m_g_final, v_c_ctx, v_w_mod, v_b_mod, v_g_n1, v_w_ffn1_up, v_w_ffn1_down, v_g_n2, v_w_in, v_b_in, v_w_dw, v_b_dw, v_g_ln, v_b_ln, v_w_conf_out, v_w_lru_conv, v_b_lru_conv, v_w_rec_gate, v_b_rec_gate, v_w_in_gate, v_b_in_gate, v_lru_lambda, v_w_lru_out, v_w_out, v_g_n3, v_w_ffn2_up, v_w_ffn2_down, v_g_final):
    given = dict(locals())
    wt = {n: given[n] for n in W_NAMES}
    mom = {n: given["m_" + n] for n in W_NAMES}
    var = {n: given["v_" + n] for n in W_NAMES}

    def flat2(a):
        if a.ndim == 1:
            return a.reshape(1, -1)
        a = a[0]
        return a if a.ndim == 2 else a.reshape(-1, a.shape[-1])

    t_lat, d = x.shape[1], x.shape[2]
    t_ctx = ctx.shape[1]
    rows = Rows(t_lat, t_ctx, d)
    xi, yi, ci = lax.axis_index("x"), lax.axis_index("y"), lax.axis_index("c")
    me, chip = 4 * xi + 2 * yi + ci, 2 * xi + yi
    lru_w = b_lru_conv.shape[-1]
    n_blk = lru_w // LANES
    taps = w_dw.shape[1]
    lru_taps = w_lru_conv.shape[1]
    ffn = w_ffn1_down.shape[1] * 4

    small_shapes = [(1, d)] + [flat2(wt[n]).shape for n in SMALL_SHARDED]
    small_all = exchange("all", [_pack([c] + [flat2(wt[n]) for n in SMALL_SHARDED])], False, "gather_small")[0]
    small_all = _unpack(small_all, small_shapes)
    c_all = small_all[0][:, 0, :]
    full = {n: jnp.concatenate([a[0], a[2], a[4], a[6]], axis=-1) for n, a in zip(SMALL_SHARDED, small_all[1:])}
    big = COL_SHARDED + ROW_SHARDED
    first_used = ['w_ffn1_up', 'w_ffn1_down']
    used_later = [n for n in big if n not in first_used]
    gathered_later = exchange_by_sequencer([wt[n][0].astype(BF16) for n in used_later], False, "gather_weights_later", 1)
    wb = {}

    def take_weights(names, gathered):
        for n, g in zip(names, gathered):
            wb[n] = _cols_from_shards(g) if n in COL_SHARDED else g.reshape(-1, g.shape[-1])

    take_weights(first_used, exchange("chips", [wt[n][0].astype(BF16) for n in first_used], False, "gather_weights_first"))

    c9 = jnp.concatenate([c_all, c_ctx.reshape(1, d), jnp.zeros((7, d), F32)], axis=0)
    mod_cols = mod_matmul(c9, w_mod[0], lax.dynamic_slice_in_dim(b_mod, chip * w_mod.shape[2], w_mod.shape[2], axis=1), "mod_matmul")
    mod_all = exchange("all", [mod_cols], False, "gather_mod")[0]
    mod9 = jnp.concatenate([mod_all[0], mod_all[2], mod_all[4], mod_all[6]], axis=-1)
    mv = jnp.stack([lax.dynamic_index_in_dim(mod9, me, 0, keepdims=False).reshape(N_MOD, d), mod9[8].reshape(N_MOD, d)])

    x_in = jnp.concatenate([x[0], ctx[0]], axis=0)
    pos = jnp.concatenate([_pos_embedding(t_lat, d), jnp.zeros((t_ctx, d), F32)], axis=0)

    def ffn_fwd(tag, h, w_up, w_down):
        g, u, act = matmul_swiglu(h, w_up, f"{tag}_up")
        return (g, u), act, matmul(act, w_down, "nn", F32, f"{tag}_down")

    x0, h1 = norm_mod(rows, x_in, g_n1, mv, 0, 1, "norm1", pos=pos)
    gu1, act1, f1 = ffn_fwd("ffn1", h1, wb["w_ffn1_up"], wb["w_ffn1_down"])
    gathered_later, f1 = lax.optimization_barrier((gathered_later, f1))
    take_weights(used_later, gathered_later)
    x1, h2 = norm_mod(rows, x0, g_n2, mv, 3, 4, "norm2", resid=(f1, 2, 0.5))

    conf_w = w_conf_out.shape[1] * 4
    col_groups = [(0, 2 * conf_w), (2 * conf_w, lru_w), (2 * conf_w + lru_w, lru_w), (2 * conf_w + 2 * lru_w, 2 * d)]
    pvg, ux, gl, pgc = [matmul(h2, wb["w_in"][:, s:s + n], "nn", BF16, f"in_proj{k}", bias=b_in[:, s:s + n])
                        for k, (s, n) in enumerate(col_groups)]
    u = glu(rows, pvg, "glu")
    u2 = dwconv(rows, u, full["w_dw"], b_dw, taps // 2, F32, "conf_conv")
    s_act = ln_silu(rows, u2, g_ln, b_ln, "conf_ln")
    yc = matmul(s_act, wb["w_conf_out"], "nn", BF16, "conf_out")

    xr = dwconv(rows, ux, full["w_lru_conv"], b_lru_conv, LRU_PAD_LEFT, F32, "lru_conv")
    w_rec, w_ing = w_rec_gate[0].astype(BF16), w_in_gate[0].astype(BF16)
    wcat = jnp.concatenate([w_rec[0], w_ing[0], w_rec[1], w_ing[1]], axis=-1)

    def per_block(a):
        return jnp.transpose(a.reshape(2, n_blk, LANES), (1, 0, 2)).reshape(n_blk, 1, 2 * LANES)

    b_rec, b_ing = full["b_rec_gate"].reshape(2, n_blk, LANES), full["b_in_gate"].reshape(2, n_blk, LANES)
    bcat = jnp.concatenate([b_rec[0], b_ing[0], b_rec[1], b_ing[1]], axis=-1).reshape(n_blk, 1, 4 * LANES)
    lam = per_block(full["lru_lambda"])
    a_gate, b_gate = lru_gates(rows, xr, wcat, bcat, lam, "lru_gates")
    h_scan, h_prev = lru_scan(rows, a_gate, b_gate, "lru_scan")
    yl_in = lru_merge(rows, h_scan, gl, "lru_merge")
    yl = matmul(yl_in, wb["w_lru_out"], "nn", BF16, "lru_out")
    z = z_merge(rows, pgc, yc, yl, "z_merge")
    y = matmul(z, wb["w_out"], "nn", F32, "mix_out")

    x2, h3 = norm_mod(rows, x1, g_n3, mv, 6, 7, "norm3", resid=(y, 5, 1.0))
    gu2, act2, f2 = ffn_fwd("ffn2", h3, wb["w_ffn2_up"], wb["w_ffn2_down"])
    loss_part, dx3, dg_final = loss_head(rows, x2, f2, mv, g_final.reshape(1, d), loss_target[0], "loss_head")
    loss = lax.psum(loss_part[0, 0], ("x", "y", "c"))

    grads = {"g_final": dg_final}
    dmv = [None] * N_MOD

    def ffn_bwd(tag, dxn, x_prev, h, gu, act, f, w_up, w_down, g, ks):
        k_shift, k_scale, k_gate = ks
        df, dmv[k_gate] = gate_bwd(rows, dxn, f, mv, k_gate, 0.5, f"{tag}_gate_bwd")
        grads[f"w_{tag}_down"] = matmul(act, df, "tn", F32, f"{tag}_down_dw")
        dact = matmul(df, w_down, "nt", BF16, f"{tag}_down_dx")
        dgu = swiglu_bwd(rows, gu[0], gu[1], dact, f"{tag}_act_bwd")
        grads[f"w_{tag}_up"] = matmul(h, dgu, "tn", F32, f"{tag}_up_dw")
        dh = matmul(dgu, w_up, "nt", F32, f"{tag}_up_dx")
        dx, dg, dmv[k_shift], dmv[k_scale] = norm_mod_bwd(rows, x_prev, dh, dxn, g, mv, k_shift, k_scale, f"{tag}_norm_bwd")
        return dx, dg

    def grad_pieces(names):
        return [(_shards_from_cols(grads[n]) if n in COL_SHARDED else grads[n].reshape(4, -1, grads[n].shape[-1])).astype(BF16)
                for n in names]

    from_chips = {}

    def scatter_behind(names, tag, collective_id, carry):
        ready, carry = lax.optimization_barrier(([grads[n] for n in names], carry))
        grads.update(zip(names, ready))
        from_chips.update(zip(names, exchange_by_sequencer(grad_pieces(names), True, f"scatter_grads_{tag}", collective_id)))
        return carry

    def settle(names, carry):
        landed, carry = lax.optimization_barrier(([from_chips[n] for n in names], carry))
        from_chips.update(zip(names, landed))
        return carry

    dx2, grads["g_n3"] = ffn_bwd("ffn2", dx3, x2, h3, gu2, act2, f2, wb["w_ffn2_up"], wb["w_ffn2_down"], g_n3, (6, 7, 8))
    dx2 = scatter_behind(['w_ffn2_up', 'w_ffn2_down'], "ffn2", 2, dx2)

    dy, dmv[5] = gate_bwd(rows, dx2, y, mv, 5, 1.0, "mix_gate_bwd")
    grads["w_out"] = matmul(z, dy, "tn", F32, "mix_out_dw")
    dz = matmul(dy, wb["w_out"], "nt", BF16, "mix_out_dx")
    dyc, dyl, dpgc = z_merge_bwd(rows, pgc, yc, yl, dz, "z_merge_bwd")

    grads["w_conf_out"] = matmul(s_act, dyc, "tn", F32, "conf_out_dw")
    ds_act = matmul(dyc, wb["w_conf_out"], "nt", BF16, "conf_out_dx")
    du2, grads["g_ln"], grads["b_ln"] = ln_silu_bwd(rows, u2, ds_act, g_ln, b_ln, "conf_ln_bwd")
    grads["w_dw"], grads["b_dw"] = dwconv_bwd_w(rows, u, du2, taps, taps // 2, "conf_conv_dw")
    du = dwconv(rows, du2, full["w_dw"][::-1], None, taps - 1 - taps // 2, F32, "conf_conv_dx")
    dpvg = glu_bwd(rows, pvg, du, "glu_bwd")

    grads["w_lru_out"] = matmul(yl_in, dyl, "tn", F32, "lru_out_dw")
    dyl = scatter_behind(['w_out', 'w_conf_out', 'w_lru_out'], "mixer", 3, dyl)
    dyl_in = matmul(dyl, wb["w_lru_out"], "nt", BF16, "lru_out_dx")
    dh_sum, dgl = lru_merge_bwd(rows, h_scan, gl, dyl_in, "lru_merge_bwd")
    da_gate, db_gate = lru_scan_bwd(rows, dh_sum, a_gate, h_prev, "lru_scan_bwd")
    dxr, dwcat, dbcat, dlam = lru_gates_bwd(rows, xr, wcat, bcat, lam, da_gate, db_gate, "lru_gates_bwd")
    grads["w_rec_gate"] = jnp.stack([dwcat[:, :, 0:LANES], dwcat[:, :, 2 * LANES:3 * LANES]])
    grads["w_in_gate"] = jnp.stack([dwcat[:, :, LANES:2 * LANES], dwcat[:, :, 3 * LANES:4 * LANES]])
    dbcat = dbcat.reshape(n_blk, 4, LANES)
    grads["b_rec_gate"] = jnp.stack([dbcat[:, 0], dbcat[:, 2]]).reshape(2, lru_w)
    grads["b_in_gate"] = jnp.stack([dbcat[:, 1], dbcat[:, 3]]).reshape(2, lru_w)
    grads["lru_lambda"] = jnp.transpose(dlam.reshape(n_blk, 2, LANES), (1, 0, 2)).reshape(2, lru_w)
    grads["w_lru_conv"], grads["b_lru_conv"] = dwconv_bwd_w(rows, ux, dxr, lru_taps, LRU_PAD_LEFT, "lru_conv_dw")
    dux = dwconv(rows, dxr, full["w_lru_conv"][::-1], None, lru_taps - 1 - LRU_PAD_LEFT, BF16, "lru_conv_dx")

    dproj = jnp.concatenate([dpvg, dux, dgl, dpgc], axis=1)
    grads["b_in"] = col_sum(rows, dproj, "in_proj_db")
    grads["w_in"] = matmul(h2, dproj, "tn", F32, "in_proj_dw")
    dproj = scatter_behind(['w_in'], "in_proj", 4, settle(['w_ffn2_up', 'w_ffn2_down'], dproj))
    dh2 = matmul(dproj, wb["w_in"], "nt", F32, "in_proj_dx")
    dx1, grads["g_n2"], dmv[3], dmv[4] = norm_mod_bwd(rows, x1, dh2, dx2, g_n2, mv, 3, 4, "norm2_bwd")
    dx1 = settle(['w_out', 'w_conf_out', 'w_lru_out'], dx1)

    dx0, grads["g_n1"] = ffn_bwd("ffn1", dx1, x0, h1, gu1, act1, f1, wb["w_ffn1_up"], wb["w_ffn1_down"], g_n1, (0, 1, 2))
    grad_x = dx0[:t_lat][None]

    last = ['w_ffn1_up', 'w_ffn1_down']
    from_chips.update(zip(last, exchange("chips", grad_pieces(last), True, "scatter_grads_ffn1")))
    chip_sums = [sum_parts(from_chips[n].reshape(4, -1, from_chips[n].shape[-1]), f"chip_sum_{n}") for n in big]
    out = {}
    for n, p in zip(big, chip_sums):
        out[n] = adamw_pair(p, flat2(wt[n]), flat2(mom[n]), flat2(var[n]), f"adamw_{n}")

    dmod = jnp.concatenate(dmv, axis=1)
    small_names = ['g_n1', 'g_n2', 'b_in', 'b_dw', 'g_ln', 'b_ln', 'b_lru_conv', 'w_rec_gate', 'w_in_gate', 'g_n3', 'g_final'] + SMALL_SHARDED
    small_list = [grads[n] for n in small_names] + [dmod[0], dmod[1]]
    small_buf = _pack(small_list, 8 * 2 * SUBLANES)
    small_parts, dmod_all = exchange("all", [small_buf.reshape(8, -1, LANES), _pack([dmod[0]])], [True, False], "scatter_small_grads")
    small_eighth = sum_parts(small_parts, "sum_small_grads")
    small_sum = exchange("all", [small_eighth], False, "gather_small_grads")[0].reshape(small_buf.shape)
    small_sum = _unpack(small_sum, [a.shape for a in small_list])
    total = dict(zip(small_names, small_sum))
    dmod_all = _unpack(dmod_all, [dmod[0].shape])[0].reshape(8, N_MOD * d)
    dmc = small_sum[-1].reshape(1, N_MOD * d)
    total["b_mod"] = small_sum[-2].reshape(1, N_MOD * d) + dmc

    d9 = jnp.concatenate([dmod_all, dmc, jnp.zeros((7, N_MOD * d), F32)], axis=0)
    d9_cols = lax.dynamic_slice_in_dim(d9, chip * w_mod.shape[2], w_mod.shape[2], axis=1)
    g_wmod, dc9 = mod_matmul_bwd(c9, d9_cols, w_mod[0], "mod_matmul_bwd")
    dc_all = exchange("all", [dc9[8:16]], False, "gather_dc")[0]
    total["c_ctx"] = sum_parts(jnp.stack([dc_all[0], dc_all[2], dc_all[4], dc_all[6]]), "sum_dc")[0:1]
    out["w_mod"] = adamw(g_wmod[None], w_mod[0], m_w_mod[0], v_w_mod[0], "adamw_w_mod")

    small_all_names = [n for n in W_NAMES if n not in out]
    g_local, shapes = [], []
    for n in small_all_names:
        g = total[n].reshape(flat2(wt[n]).shape[:-1] + (-1,)) if n in SMALL_SHARDED else total[n].reshape(flat2(wt[n]).shape)
        if n in SMALL_SHARDED:
            width = wt[n].shape[-1]
            g = lax.dynamic_slice_in_dim(g, chip * width, width, axis=1)
        g_local.append(g)
        shapes.append(g.shape)
    packed = adamw(_pack(g_local)[None], _pack([flat2(wt[n]) for n in small_all_names]),
                   _pack([flat2(mom[n]) for n in small_all_names]), _pack([flat2(var[n]) for n in small_all_names]), "adamw_small")
    for k, n in enumerate(small_all_names):
        out[n] = [_unpack(p, shapes)[k] for p in packed]

    results = [loss, grad_x]
    for k in range(4):
        results += [out[n][k].reshape(wt[n].shape) for n in W_NAMES]
    return tuple(results)
```

```python
import functools

import jax
import jax.numpy as jnp
from jax import lax
from jax.experimental import pallas as pl
from jax.experimental.pallas import tpu as pltpu
from jax.experimental.pallas import tpu_sc as plsc

F32 = jnp.float32
BF16 = jnp.bfloat16
MESH = pl.DeviceIdType.MESH

EPS = 1e-6
GRID_W = 64
N_MOD = 9
LRU_C = 8.0
LRU_PAD_LEFT = 2
ADAM_LR, ADAM_B1, ADAM_B2, ADAM_EPS, ADAM_WD, ADAM_STEP = 0.001, 0.9, 0.999, 1e-08, 0.01, 10

LANES = 128
SUBLANES = 8
HALO = 16
VMEM_LIMIT = 56 * 1024 * 1024

W_NAMES = ['c_ctx', 'w_mod', 'b_mod', 'g_n1', 'w_ffn1_up', 'w_ffn1_down', 'g_n2', 'w_in', 'b_in', 'w_dw', 'b_dw',
           'g_ln', 'b_ln', 'w_conf_out', 'w_lru_conv', 'b_lru_conv', 'w_rec_gate', 'b_rec_gate', 'w_in_gate',
           'b_in_gate', 'lru_lambda', 'w_lru_out', 'w_out', 'g_n3', 'w_ffn2_up', 'w_ffn2_down', 'g_final']
COL_SHARDED = ['w_ffn1_up', 'w_in', 'w_ffn2_up']
ROW_SHARDED = ['w_ffn1_down', 'w_conf_out', 'w_lru_out', 'w_out', 'w_ffn2_down']
SMALL_SHARDED = ['w_dw', 'w_lru_conv', 'b_rec_gate', 'b_in_gate', 'lru_lambda']


def _params(*semantics):
    return pltpu.CompilerParams(dimension_semantics=semantics, vmem_limit_bytes=VMEM_LIMIT)


def _pick(n, target, mult=LANES):
    best = None
    for t in range(mult, min(n, target) + 1, mult):
        if n % t == 0:
            best = t
    return best or n


def _chunks(width, target=512):
    w = _pick(width, target)
    return [slice(s, s + w) for s in range(0, width, w)]


def _sigmoid(x):
    return jax.nn.sigmoid(x)


def _silu_and_grad(x):
    s = _sigmoid(x)
    return x * s, s * (1.0 + x * (1.0 - s))


_GELU_K = 0.7978845608028654


def _gelu_and_grad(x):
    t = jnp.tanh(_GELU_K * (x + 0.044715 * (x * x * x)))
    return 0.5 * x * (1.0 + t), 0.5 * (1.0 + t) + 0.5 * x * (1.0 - t * t) * _GELU_K * (1.0 + 3 * 0.044715 * x * x)


def _neg_expm1(z):
    series = -(z * (1.0 + z * (0.5 + z * (1.0 / 6.0 + z * (1.0 / 24.0)))))
    return jnp.where(z > -0.01, series, 1.0 - jnp.exp(z))


def _softplus(x):
    return jnp.maximum(x, 0.0) + jnp.log1p(jnp.exp(-jnp.abs(x)))


def _accumulate(ref, value, first):
    @pl.when(first)
    def _():
        ref[...] = value

    @pl.when(jnp.logical_not(first))
    def _():
        ref[...] += value


def _colsum(x):
    return jnp.sum(x, axis=0, keepdims=True)


class Rows:
    def __init__(self, t_lat, t_ctx, d_model):
        self.tm = _pick(t_ctx, 256, HALO)
        assert t_lat % self.tm == 0 and t_ctx % self.tm == 0
        self.n_lat = t_lat // self.tm
        self.n_all = (t_lat + t_ctx) // self.tm
        self.t_all = t_lat + t_ctx
        self.d = d_model

    def seg(self, i):
        return jnp.where(i >= self.n_lat, 1, 0)

    def seg_first(self, i):
        return jnp.logical_or(i == 0, i == self.n_lat)

    def seg_last(self, i):
        return jnp.logical_or(i == self.n_lat - 1, i == self.n_all - 1)

    def row(self, width, col=0):
        return pl.BlockSpec((self.tm, width), lambda i: (i, col))

    def vec(self, width):
        return pl.BlockSpec((1, width), lambda i: (0, 0))

    def full(self, shape):
        return pl.BlockSpec(shape, lambda i: (0,) * len(shape))

    def mod(self):
        return pl.BlockSpec((None, N_MOD, self.d), lambda i: (self.seg(i), 0, 0))

    def segsum(self):
        return pl.BlockSpec((None, 1, self.d), lambda i: (self.seg(i), 0, 0))

    def segsum_shape(self):
        return jax.ShapeDtypeStruct((2, 1, self.d), F32)

    def halo(self, width, side):
        per = self.tm // HALO
        last = self.t_all // HALO - 1
        if side < 0:
            return pl.BlockSpec((HALO, width), lambda i: (jnp.maximum(i * per - 1, 0), 0))
        return pl.BlockSpec((HALO, width), lambda i: (jnp.minimum((i + 1) * per, last), 0))

    def call(self, body, name, in_specs, out_specs, out_shape, scratch=()):
        return pl.pallas_call(body, name=name, grid=(self.n_all,), in_specs=in_specs, out_specs=out_specs,
                              out_shape=out_shape, scratch_shapes=list(scratch), compiler_params=_params("arbitrary"))


_PEER_FLIPS = {
    "chips": [(1, 0, 0), (0, 1, 0), (1, 1, 0)],
    "sibling": [(0, 0, 1)],
    "all": [(0, 0, 1), (0, 1, 0), (0, 1, 1), (1, 0, 0), (1, 0, 1), (1, 1, 0), (1, 1, 1)],
}


def _slot(kind, x, y, c):
    return {"chips": 2 * x + y, "sibling": c, "all": 4 * x + 2 * y + c}[kind]


def exchange(kind, srcs, indexed, name):
    flips = _PEER_FLIPS[kind]
    n_slots = len(flips) + 1
    n_t = len(srcs)
    by_slot = list(indexed) if isinstance(indexed, (list, tuple)) else [indexed] * n_t
    out_shapes = [jax.ShapeDtypeStruct(s.shape if ix else (n_slots,) + s.shape, s.dtype) for s, ix in zip(srcs, by_slot)]

    def body(*refs):
        src_refs, dst_refs = refs[:n_t], refs[n_t:2 * n_t]
        send_sems, recv_sems, local_sems = refs[2 * n_t:]
        x, y, c = lax.axis_index("x"), lax.axis_index("y"), lax.axis_index("c")
        me = _slot(kind, x, y, c)

        def piece(t, k):
            return src_refs[t].at[k] if by_slot[t] else src_refs[t]

        peers = [(1 - x if fx else x, 1 - y if fy else y, 1 - c if fc else c) for fx, fy, fc in flips]
        local = [pltpu.make_async_copy(piece(t, me), dst_refs[t].at[me], local_sems.at[t]) for t in range(n_t)]
        for cp in local:
            cp.start()
        sends = []
        for t in range(n_t):
            for p, peer in enumerate(peers):
                sends.append(pltpu.make_async_remote_copy(
                    src_ref=piece(t, _slot(kind, *peer)), dst_ref=dst_refs[t].at[me],
                    send_sem=send_sems.at[t, p], recv_sem=recv_sems.at[t, p], device_id=peer, device_id_type=MESH))
        for cp in sends:
            cp.start()
        for t in range(n_t):
            for p, peer in enumerate(peers):
                theirs = _slot(kind, *peer)
                pltpu.make_async_remote_copy(
                    src_ref=piece(t, me), dst_ref=dst_refs[t].at[theirs],
                    send_sem=send_sems.at[t, p], recv_sem=recv_sems.at[t, p], device_id=peer, device_id_type=MESH).wait_recv()
        for cp in sends:
            cp.wait_send()
        for cp in local:
            cp.wait()

    any_spec = pl.BlockSpec(memory_space=pl.ANY)
    outs = pl.pallas_call(
        body, name=name, out_shape=out_shapes, in_specs=[any_spec] * n_t, out_specs=[any_spec] * n_t,
        scratch_shapes=[pltpu.SemaphoreType.DMA((n_t, len(flips))), pltpu.SemaphoreType.DMA((n_t, len(flips))),
                        pltpu.SemaphoreType.DMA((n_t,))],
        compiler_params=pltpu.CompilerParams(has_side_effects=True),
    )(*srcs)
    return list(outs)


def exchange_by_sequencer(srcs, indexed, name, collective_id):
    flips = _PEER_FLIPS["chips"]
    n_t, n_p = len(srcs), len(flips)
    src_refs = [jax.new_ref(s, memory_space=pltpu.MemorySpace.HBM) for s in srcs]
    land_refs = [jax.empty_ref(jax.ShapeDtypeStruct(s.shape if indexed else (n_p + 1,) + s.shape, s.dtype),
                               memory_space=pltpu.MemorySpace.HBM) for s in srcs]

    @pl.kernel(mesh=plsc.ScalarSubcoreMesh(axis_name="sequencer", num_cores=1), name=name,
               scratch_types=(pltpu.SemaphoreType.DMA((n_t, n_p)), pltpu.SemaphoreType.DMA((n_t, n_p)), pltpu.SemaphoreType.DMA((n_t,))),
               compiler_params=pltpu.CompilerParams(collective_id=collective_id))
    def launch(send_sems, recv_sems, local_sems):
        x, y, c = lax.axis_index("x"), lax.axis_index("y"), lax.axis_index("c")
        peers = [(1 - x if fx else x, 1 - y if fy else y, c) for fx, fy, _ in flips]
        barrier = pltpu.get_barrier_semaphore()
        for peer in peers:
            pl.semaphore_signal(barrier, inc=1, device_id=peer, device_id_type=MESH)
        pl.semaphore_wait(barrier, n_p)
        me = _slot("chips", x, y, c)

        def piece(t, k):
            return src_refs[t].at[k] if indexed else src_refs[t]

        local = [pltpu.make_async_copy(piece(t, me), land_refs[t].at[me], local_sems.at[t]) for t in range(n_t)]
        for cp in local:
            cp.start()
        sends = []
        for t in range(n_t):
            for p, peer in enumerate(peers):
                sends.append(pltpu.make_async_remote_copy(
                    src_ref=piece(t, _slot("chips", *peer)), dst_ref=land_refs[t].at[me],
                    send_sem=send_sems.at[t, p], recv_sem=recv_sems.at[t, p], device_id=peer, device_id_type=MESH))
        for cp in sends:
            cp.start()
        for t in range(n_t):
            for p, peer in enumerate(peers):
                pltpu.make_async_remote_copy(
                    src_ref=piece(t, me), dst_ref=land_refs[t].at[_slot("chips", *peer)],
                    send_sem=send_sems.at[t, p], recv_sem=recv_sems.at[t, p], device_id=peer, device_id_type=MESH).wait_recv()
        for cp in sends:
            cp.wait_send()
        for cp in local:
            cp.wait()

    launch()
    return [r[...] for r in land_refs]


def matmul(a, b, mode, out_dtype, name, bias=None):
    if mode == "nn":
        (n_i, n_r), (_, n_j) = a.shape, b.shape
    elif mode == "nt":
        (n_i, n_r), (n_j, _) = a.shape, b.shape
    else:
        (n_r, n_i), (_, n_j) = a.shape, b.shape
    ti = _pick(n_i, 1408)
    tj = _pick(n_j, 1664)
    tr = n_r if (mode == "nn" and n_r <= 2816) else _pick(n_r, 768 if mode == "tn" else 1664)
    steps = n_r // tr
    bytes_a, bytes_b = a.size * a.dtype.itemsize, b.size * b.dtype.itemsize
    j_outer = steps == 1 and bytes_b + (n_j // tj) * bytes_a < bytes_a + (n_i // ti) * bytes_b

    def at(fn):
        return (lambda j, i, r: fn(i, j, r)) if j_outer else fn

    a_spec = {"nn": pl.BlockSpec((ti, tr), at(lambda i, j, r: (i, r))), "nt": pl.BlockSpec((ti, tr), at(lambda i, j, r: (i, r))),
              "tn": pl.BlockSpec((tr, ti), at(lambda i, j, r: (r, i)))}[mode]
    b_spec = {"nn": pl.BlockSpec((tr, tj), at(lambda i, j, r: (r, j))), "nt": pl.BlockSpec((tj, tr), at(lambda i, j, r: (j, r))),
              "tn": pl.BlockSpec((tr, tj), at(lambda i, j, r: (r, j)))}[mode]
    dims = {"nn": (((1,), (0,)), ((), ())), "nt": (((1,), (1,)), ((), ())), "tn": (((0,), (0,)), ((), ()))}[mode]

    def body(*refs):
        a_ref, b_ref = refs[:2]
        bias_ref = refs[2] if bias is not None else None
        o_ref = refs[3] if bias is not None else refs[2]
        prod = lax.dot_general(a_ref[...].astype(BF16), b_ref[...].astype(BF16), dims, preferred_element_type=F32)

        def finish(total):
            if bias_ref is not None:
                total = total + bias_ref[...]
            o_ref[...] = total.astype(out_dtype)

        if steps == 1:
            finish(prod)
        else:
            acc = refs[-1]
            r = pl.program_id(2)

            @pl.when(r == 0)
            def _():
                acc[...] = prod

            @pl.when(r > 0)
            def _():
                acc[...] += prod

            @pl.when(r == steps - 1)
            def _():
                finish(acc[...])

    in_specs = [a_spec, b_spec] + ([pl.BlockSpec((1, tj), at(lambda i, j, r: (0, j)))] if bias is not None else [])
    args = (a, b) + ((bias,) if bias is not None else ())
    grid = (n_j // tj, n_i // ti, steps) if j_outer else (n_i // ti, n_j // tj, steps)
    return pl.pallas_call(
        body, name=name, grid=grid, in_specs=in_specs,
        out_specs=pl.BlockSpec((ti, tj), at(lambda i, j, r: (i, j))), out_shape=jax.ShapeDtypeStruct((n_i, n_j), out_dtype),
        scratch_shapes=[pltpu.VMEM((ti, tj), F32)] if steps > 1 else [],
        compiler_params=_params("arbitrary", "arbitrary", "arbitrary"),
    )(*args)


def matmul_swiglu(h, w_up, name):
    n_t, d = h.shape
    f = w_up.shape[1] // 2
    ti, tj = _pick(n_t, 384), _pick(f, 1408)
    n_j = f // tj

    def body(h_ref, wg_ref, wu_ref, g_ref, u_ref, act_ref):
        hv = h_ref[...]
        g = jnp.dot(hv, wg_ref[...], preferred_element_type=F32)
        u = jnp.dot(hv, wu_ref[...], preferred_element_type=F32)
        g_ref[...] = g.astype(BF16)
        u_ref[...] = u.astype(BF16)
        act_ref[...] = (_silu_and_grad(g)[0] * u).astype(BF16)

    out = pl.BlockSpec((ti, tj), lambda j, i: (i, j))
    shape = jax.ShapeDtypeStruct((n_t, f), BF16)
    return pl.pallas_call(
        body, name=name, grid=(n_j, n_t // ti),
        in_specs=[pl.BlockSpec((ti, d), lambda j, i: (i, 0)), pl.BlockSpec((d, tj), lambda j, i: (0, j)),
                  pl.BlockSpec((d, tj), lambda j, i: (0, j + n_j))],
        out_specs=[out, out, out], out_shape=[shape, shape, shape], compiler_params=_params("arbitrary", "arbitrary"),
    )(h, w_up, w_up)


def norm_mod(rows, x, g, mv, shift_k, scale_k, name, resid=None, pos=None):
    d = rows.d
    changed = resid is not None or pos is not None

    def body(*refs):
        refs = list(refs)
        x_ref, g_ref, mv_ref = refs[:3]
        rest = refs[3:]
        xv = x_ref[...]
        if pos is not None:
            xv = xv + rest.pop(0)[...]
        if resid is not None:
            xv = xv + resid[2] * mv_ref[resid[1]:resid[1] + 1, :] * rest.pop(0)[...]
        if changed:
            rest.pop(0)[...] = xv
        r = lax.rsqrt(jnp.mean(xv * xv, axis=-1, keepdims=True) + EPS)
        h = (xv * r) * g_ref[...]
        if shift_k is not None:
            h = h * (1.0 + mv_ref[scale_k:scale_k + 1, :]) + mv_ref[shift_k:shift_k + 1, :]
        rest.pop(0)[...] = h.astype(BF16)

    ins = [x, g, mv] + ([pos] if pos is not None else []) + ([resid[0]] if resid is not None else [])
    in_specs = [rows.row(d), rows.vec(d), rows.mod()] + [rows.row(d)] * (len(ins) - 3)
    out_shape = ([jax.ShapeDtypeStruct((rows.t_all, d), F32)] if changed else []) + [jax.ShapeDtypeStruct((rows.t_all, d), BF16)]
    outs = rows.call(body, name, in_specs, [rows.row(d)] * len(out_shape), out_shape)(*ins)
    return (outs[0], outs[1]) if changed else (None, outs[0])


def norm_mod_bwd(rows, x, dh, dxn, g, mv, shift_k, scale_k, name):
    d = rows.d

    def body(x_ref, dh_ref, dxn_ref, g_ref, mv_ref, dx_ref, dg_ref, dsh_ref, dsc_ref):
        i = pl.program_id(0)
        xv, dhv, gv = x_ref[...], dh_ref[...], g_ref[...]
        r = lax.rsqrt(jnp.mean(xv * xv, axis=-1, keepdims=True) + EPS)
        n = xv * r
        dy = dhv * (1.0 + mv_ref[scale_k:scale_k + 1, :])
        dn = dy * gv
        dx_ref[...] = dxn_ref[...] + r * (dn - n * jnp.mean(dn * n, axis=-1, keepdims=True))
        _accumulate(dg_ref, _colsum(dy * n), i == 0)
        _accumulate(dsh_ref, _colsum(dhv), rows.seg_first(i))
        _accumulate(dsc_ref, _colsum(dhv * (n * gv)), rows.seg_first(i))

    return rows.call(
        body, name, [rows.row(d), rows.row(d), rows.row(d), rows.vec(d), rows.mod()],
        [rows.row(d), rows.vec(d), rows.segsum(), rows.segsum()],
        [jax.ShapeDtypeStruct((rows.t_all, d), F32), jax.ShapeDtypeStruct((1, d), F32), rows.segsum_shape(), rows.segsum_shape()],
    )(x, dh, dxn, g, mv)


def gate_bwd(rows, dxn, f, mv, gate_k, scale, name):
    d = rows.d

    def body(dxn_ref, f_ref, mv_ref, df_ref, dgate_ref):
        i = pl.program_id(0)
        dv = dxn_ref[...]
        df_ref[...] = (scale * mv_ref[gate_k:gate_k + 1, :] * dv).astype(BF16)
        _accumulate(dgate_ref, scale * _colsum(dv * f_ref[...]), rows.seg_first(i))

    return rows.call(body, name, [rows.row(d), rows.row(d), rows.mod()], [rows.row(d), rows.segsum()],
                     [jax.ShapeDtypeStruct((rows.t_all, d), BF16), rows.segsum_shape()])(dxn, f, mv)


def swiglu_bwd(rows, g, u, dact, name):
    f = g.shape[1]

    def body(g_ref, u_ref, da_ref, o_ref):
        for cs in _chunks(f):
            act, dact_dg = _silu_and_grad(g_ref[:, cs].astype(F32))
            dav = da_ref[:, cs].astype(F32)
            o_ref[:, cs] = (dav * u_ref[:, cs].astype(F32) * dact_dg).astype(BF16)
            o_ref[:, slice(f + cs.start, f + cs.stop)] = (dav * act).astype(BF16)

    return rows.call(body, name, [rows.row(f), rows.row(f), rows.row(f)], rows.row(2 * f),
                     jax.ShapeDtypeStruct((rows.t_all, 2 * f), BF16))(g, u, dact)


def glu(rows, pvg, name):
    w = pvg.shape[1] // 2

    def body(v_ref, t_ref, o_ref):
        for cs in _chunks(w):
            o_ref[:, cs] = v_ref[:, cs].astype(F32) * _sigmoid(t_ref[:, cs].astype(F32))

    return rows.call(body, name, [rows.row(w, 0), rows.row(w, 1)], rows.row(w), jax.ShapeDtypeStruct((rows.t_all, w), F32))(pvg, pvg)


def glu_bwd(rows, pvg, du, name):
    w = pvg.shape[1] // 2

    def body(v_ref, t_ref, du_ref, o_ref):
        for cs in _chunks(w):
            s = _sigmoid(t_ref[:, cs].astype(F32))
            duv = du_ref[:, cs]
            o_ref[:, cs] = (duv * s).astype(BF16)
            o_ref[:, slice(w + cs.start, w + cs.stop)] = (duv * v_ref[:, cs].astype(F32) * s * (1.0 - s)).astype(BF16)

    return rows.call(body, name, [rows.row(w, 0), rows.row(w, 1), rows.row(w)], rows.row(2 * w),
                     jax.ShapeDtypeStruct((rows.t_all, 2 * w), BF16))(pvg, pvg, du)


def ln_silu(rows, u, g, b, name):
    w = u.shape[1]

    def body(u_ref, g_ref, b_ref, o_ref):
        uv = u_ref[...]
        xc = uv - jnp.mean(uv, axis=-1, keepdims=True)
        n = xc * lax.rsqrt(jnp.mean(xc * xc, axis=-1, keepdims=True) + EPS)
        o_ref[...] = _silu_and_grad(n * g_ref[...] + b_ref[...])[0].astype(BF16)

    return rows.call(body, name, [rows.row(w), rows.vec(w), rows.vec(w)], rows.row(w),
                     jax.ShapeDtypeStruct((rows.t_all, w), BF16))(u, g, b)


def ln_silu_bwd(rows, u, ds, g, b, name):
    w = u.shape[1]

    def body(u_ref, ds_ref, g_ref, b_ref, du_ref, dg_ref, db_ref):
        i = pl.program_id(0)
        uv, gv = u_ref[...], g_ref[...]
        xc = uv - jnp.mean(uv, axis=-1, keepdims=True)
        r = lax.rsqrt(jnp.mean(xc * xc, axis=-1, keepdims=True) + EPS)
        n = xc * r
        dy = ds_ref[...].astype(F32) * _silu_and_grad(n * gv + b_ref[...])[1]
        dn = dy * gv
        du_ref[...] = r * (dn - jnp.mean(dn, axis=-1, keepdims=True) - n * jnp.mean(dn * n, axis=-1, keepdims=True))
        _accumulate(dg_ref, _colsum(dy * n), i == 0)
        _accumulate(db_ref, _colsum(dy), i == 0)

    return rows.call(body, name, [rows.row(w), rows.row(w), rows.vec(w), rows.vec(w)], [rows.row(w), rows.vec(w), rows.vec(w)],
                     [jax.ShapeDtypeStruct((rows.t_all, w), F32), jax.ShapeDtypeStruct((1, w), F32),
                      jax.ShapeDtypeStruct((1, w), F32)])(u, ds, g, b)


def lru_merge(rows, h, gl, name):
    w = gl.shape[1]

    def body(h_ref, gl_ref, o_ref):
        for cs in _chunks(w):
            o_ref[:, cs] = ((h_ref[0, :, cs] + h_ref[1, :, cs]) * _gelu_and_grad(gl_ref[:, cs].astype(F32))[0]).astype(BF16)

    return rows.call(body, name, [pl.BlockSpec((2, rows.tm, w), lambda i: (0, i, 0)), rows.row(w)], rows.row(w),
                     jax.ShapeDtypeStruct((rows.t_all, w), BF16))(h, gl)


def lru_merge_bwd(rows, h, gl, dy, name):
    w = gl.shape[1]

    def body(h_ref, gl_ref, dy_ref, dh_ref, dgl_ref):
        for cs in _chunks(w):
            act, dact = _gelu_and_grad(gl_ref[:, cs].astype(F32))
            dyv = dy_ref[:, cs].astype(F32)
            dh_ref[:, cs] = dyv * act
            dgl_ref[:, cs] = (dyv * (h_ref[0, :, cs] + h_ref[1, :, cs]) * dact).astype(BF16)

    return rows.call(body, name, [pl.BlockSpec((2, rows.tm, w), lambda i: (0, i, 0)), rows.row(w), rows.row(w)],
                     [rows.row(w), rows.row(w)],
                     [jax.ShapeDtypeStruct((rows.t_all, w), F32), jax.ShapeDtypeStruct((rows.t_all, w), BF16)])(h, gl, dy)


def z_merge(rows, pgc, yc, yl, name):
    d = rows.d

    def body(gc_ref, gr_ref, yc_ref, yl_ref, o_ref):
        for cs in _chunks(d):
            o_ref[:, cs] = (_sigmoid(gc_ref[:, cs].astype(F32)) * yc_ref[:, cs].astype(F32)
                            + _sigmoid(gr_ref[:, cs].astype(F32)) * yl_ref[:, cs].astype(F32)).astype(BF16)

    return rows.call(body, name, [rows.row(d, 0), rows.row(d, 1), rows.row(d), rows.row(d)], rows.row(d),
                     jax.ShapeDtypeStruct((rows.t_all, d), BF16))(pgc, pgc, yc, yl)


def z_merge_bwd(rows, pgc, yc, yl, dz, name):
    d = rows.d

    def body(gc_ref, gr_ref, yc_ref, yl_ref, dz_ref, dyc_ref, dyl_ref, dp_ref):
        for cs in _chunks(d):
            sc, sr = _sigmoid(gc_ref[:, cs].astype(F32)), _sigmoid(gr_ref[:, cs].astype(F32))
            dzv = dz_ref[:, cs].astype(F32)
            dyc_ref[:, cs] = (dzv * sc).astype(BF16)
            dyl_ref[:, cs] = (dzv * sr).astype(BF16)
            dp_ref[:, cs] = (dzv * yc_ref[:, cs].astype(F32) * sc * (1.0 - sc)).astype(BF16)
            dp_ref[:, slice(d + cs.start, d + cs.stop)] = (dzv * yl_ref[:, cs].astype(F32) * sr * (1.0 - sr)).astype(BF16)

    return rows.call(body, name, [rows.row(d, 0), rows.row(d, 1), rows.row(d), rows.row(d), rows.row(d)],
                     [rows.row(d), rows.row(d), rows.row(2 * d)],
                     [jax.ShapeDtypeStruct((rows.t_all, d), BF16), jax.ShapeDtypeStruct((rows.t_all, d), BF16),
                      jax.ShapeDtypeStruct((rows.t_all, 2 * d), BF16)])(pgc, pgc, yc, yl, dz)


def col_sum(rows, x, name):
    w = x.shape[1]

    def body(x_ref, o_ref):
        i = pl.program_id(0)
        for cs in _chunks(w):
            _accumulate(o_ref.at[:, cs], _colsum(x_ref[:, cs].astype(F32)), i == 0)

    return rows.call(body, name, [rows.row(w)], rows.vec(w), jax.ShapeDtypeStruct((1, w), F32))(x)


def loss_head(rows, x, f, mv, g, target, name):
    d = rows.d

    def body(x_ref, f_ref, mv_ref, g_ref, t_ref, loss_ref, dx_ref, dg_ref):
        i = pl.program_id(0)
        valid = jnp.where(i < rows.n_lat, 1.0, 0.0)
        xv = x_ref[...] + 0.5 * mv_ref[8:9, :] * f_ref[...]
        gv = g_ref[...]
        r = lax.rsqrt(jnp.mean(xv * xv, axis=-1, keepdims=True) + EPS)
        n = xv * r
        err = (n * gv - t_ref[...]) * valid
        part = 0.5 * jnp.sum(jnp.mean(err * err, axis=-1, keepdims=True), axis=0, keepdims=True)
        _accumulate(loss_ref, jnp.broadcast_to(part, (1, LANES)), i == 0)
        dy = err * (1.0 / d)
        dn = dy * gv
        dx_ref[...] = r * (dn - n * jnp.mean(dn * n, axis=-1, keepdims=True))
        _accumulate(dg_ref, _colsum(dy * n), i == 0)

    target_spec = pl.BlockSpec((rows.tm, d), lambda i: (jnp.minimum(i, rows.n_lat - 1), 0))
    return rows.call(body, name, [rows.row(d), rows.row(d), rows.mod(), rows.vec(d), target_spec],
                     [rows.vec(LANES), rows.row(d), rows.vec(d)],
                     [jax.ShapeDtypeStruct((1, LANES), F32), jax.ShapeDtypeStruct((rows.t_all, d), F32),
                      jax.ShapeDtypeStruct((1, d), F32)])(x, f, mv, g, target)


def _extended(rows, i, prev_ref, cur_ref, next_ref, cs):
    prev = jnp.where(rows.seg_first(i), 0.0, prev_ref[:, cs].astype(F32))
    nxt = jnp.where(rows.seg_last(i), 0.0, next_ref[:, cs].astype(F32))
    return jnp.concatenate([prev, cur_ref[:, cs].astype(F32), nxt], axis=0)


def _shifted(ext, offset, tm):
    n = ext.shape[0]
    return pltpu.roll(ext, (-offset) % n, 0)[HALO:HALO + tm] if offset else ext[HALO:HALO + tm]


def _pad_taps(w):
    k = w.shape[0]
    return jnp.pad(w, ((0, -k % SUBLANES), (0, 0)))


def dwconv(rows, u, w, b, pad_left, out_dtype, name):
    taps, width = w.shape
    wp = _pad_taps(w)

    def body(*refs):
        prev_ref, cur_ref, next_ref, w_ref = refs[:4]
        b_ref = refs[4] if b is not None else None
        o_ref = refs[-1]
        i = pl.program_id(0)
        for cs in _chunks(width, LANES):
            ext = _extended(rows, i, prev_ref, cur_ref, next_ref, cs)
            acc = jnp.zeros((rows.tm, LANES), F32) if b_ref is None else jnp.broadcast_to(b_ref[:, cs], (rows.tm, LANES))
            for k in range(taps):
                acc = acc + w_ref[k:k + 1, cs] * _shifted(ext, k - pad_left, rows.tm)
            o_ref[:, cs] = acc.astype(out_dtype)

    ins = [u, u, u, wp] + ([b] if b is not None else [])
    in_specs = [rows.halo(width, -1), rows.row(width), rows.halo(width, 1), rows.full(wp.shape)] + ([rows.vec(width)] if b is not None else [])
    return rows.call(body, name, in_specs, rows.row(width), jax.ShapeDtypeStruct((rows.t_all, width), out_dtype))(*ins)


def dwconv_bwd_w(rows, u, dy, taps, pad_left, name):
    width = u.shape[1]
    taps_p = taps + (-taps % SUBLANES)

    def body(prev_ref, cur_ref, next_ref, dy_ref, dw_ref, db_ref):
        i = pl.program_id(0)
        tap_row = lax.broadcasted_iota(jnp.int32, (taps_p, LANES), 0)
        for cs in _chunks(width, LANES):
            ext = _extended(rows, i, prev_ref, cur_ref, next_ref, cs)
            dyv = dy_ref[:, cs]
            total = jnp.zeros((taps_p, LANES), F32)
            for k in range(taps):
                total = total + jnp.where(tap_row == k, _colsum(dyv * _shifted(ext, k - pad_left, rows.tm)), 0.0)
            _accumulate(dw_ref.at[:, cs], total, i == 0)
            _accumulate(db_ref.at[:, cs], _colsum(dyv), i == 0)

    dw, db = rows.call(body, name, [rows.halo(width, -1), rows.row(width), rows.halo(width, 1), rows.row(width)],
                       [rows.full((taps_p, width)), rows.vec(width)],
                       [jax.ShapeDtypeStruct((taps_p, width), F32), jax.ShapeDtypeStruct((1, width), F32)])(u, u, u, dy)
    return dw[:taps], db


def _gate_values(xr, pre, lam, d):
    r = _sigmoid(pre[:, (2 * d) * LANES:(2 * d + 1) * LANES])
    ig = _sigmoid(pre[:, (2 * d + 1) * LANES:(2 * d + 2) * LANES])
    sp = _softplus(-lam[:, d * LANES:(d + 1) * LANES])
    log_a = -LRU_C * r * sp
    return r, ig, sp, jnp.exp(log_a), jnp.sqrt(_neg_expm1(2.0 * log_a))


def lru_gates(rows, xr, wcat, bcat, lam, name):
    n_blk = wcat.shape[0]
    width = xr.shape[1]

    def body(xr_ref, w_ref, b_ref, lam_ref, a_ref, bb_ref):
        for h in range(n_blk):
            cs = slice(h * LANES, (h + 1) * LANES)
            xv = xr_ref[:, cs]
            pre = jnp.dot(xv.astype(BF16), w_ref[h], preferred_element_type=F32) + b_ref[h]
            for d in range(2):
                _, ig, _, a, q = _gate_values(xv, pre, lam_ref[h], d)
                a_ref[d, :, cs] = a
                bb_ref[d, :, cs] = q * (ig * xv)

    both = pl.BlockSpec((2, rows.tm, width), lambda i: (0, i, 0))
    shape = jax.ShapeDtypeStruct((2, rows.t_all, width), F32)
    return rows.call(body, name, [rows.row(width), rows.full(wcat.shape), rows.full(bcat.shape), rows.full(lam.shape)],
                     [both, both], [shape, shape])(xr, wcat, bcat, lam)


def lru_gates_bwd(rows, xr, wcat, bcat, lam, da, dbb, name):
    n_blk = wcat.shape[0]
    width = xr.shape[1]

    def body(xr_ref, w_ref, b_ref, lam_ref, da_ref, dbb_ref, dxr_ref, dw_ref, db_ref, dlam_ref):
        i = pl.program_id(0)
        for h in range(n_blk):
            cs = slice(h * LANES, (h + 1) * LANES)
            xv = xr_ref[:, cs]
            xb = xv.astype(BF16)
            wv = w_ref[h]
            pre = jnp.dot(xb, wv, preferred_element_type=F32) + b_ref[h]
            dxr = jnp.zeros_like(xv)
            dpre, dlam = [], []
            for d in range(2):
                r, ig, sp, a, q = _gate_values(xv, pre, lam_ref[h], d)
                dav, dbv = da_ref[d, :, cs], dbb_ref[d, :, cs]
                dlog_a = dav * a - dbv * (ig * xv) * (a * a) / q
                dpre.append(dlog_a * (-LRU_C * sp) * r * (1.0 - r))
                dpre.append(dbv * q * xv * ig * (1.0 - ig))
                dxr = dxr + dbv * q * ig
                dlam.append(_colsum(dlog_a * (-LRU_C * r)) * (-_sigmoid(-lam_ref[h][:, d * LANES:(d + 1) * LANES])))
            dpre = jnp.concatenate(dpre, axis=1)
            dpb = dpre.astype(BF16)
            dxr_ref[:, cs] = dxr + lax.dot_general(dpb, wv, (((1,), (1,)), ((), ())), preferred_element_type=F32)
            _accumulate(dw_ref.at[h], lax.dot_general(xb, dpb, (((0,), (0,)), ((), ())), preferred_element_type=F32), i == 0)
            _accumulate(db_ref.at[h], _colsum(dpre), i == 0)
            _accumulate(dlam_ref.at[h], jnp.concatenate(dlam, axis=1), i == 0)

    both = pl.BlockSpec((2, rows.tm, width), lambda i: (0, i, 0))
    return rows.call(
        body, name, [rows.row(width), rows.full(wcat.shape), rows.full(bcat.shape), rows.full(lam.shape), both, both],
        [rows.row(width), rows.full(wcat.shape), rows.full(bcat.shape), rows.full(lam.shape)],
        [jax.ShapeDtypeStruct((rows.t_all, width), F32), jax.ShapeDtypeStruct(wcat.shape, F32),
         jax.ShapeDtypeStruct(bcat.shape, F32), jax.ShapeDtypeStruct(lam.shape, F32)])(xr, wcat, bcat, lam, da, dbb)


def _tile_scan(a, b, reverse):
    n = a.shape[0]
    row = lax.broadcasted_iota(jnp.int32, a.shape, 0)
    k = 1
    while k < n:
        ok = (row < n - k) if reverse else (row >= k)
        shift = n - k if reverse else k
        b = b + a * jnp.where(ok, pltpu.roll(b, shift, 0), 0.0)
        a = a * jnp.where(ok, pltpu.roll(a, shift, 0), 1.0)
        k *= 2
    return a, b


def _neighbour(v, edge, reverse):
    n = v.shape[0]
    row = lax.broadcasted_iota(jnp.int32, v.shape, 0)
    if reverse:
        return jnp.where(row < n - 1, pltpu.roll(v, n - 1, 0), edge)
    return jnp.where(row >= 1, pltpu.roll(v, 1, 0), edge)


def _scan_call(rows, body, name, ins, in_specs, n_out, width, adjoint):
    n_all, n_lat = rows.n_all, rows.n_lat

    def tile(d, s):
        s = n_all - 1 - s if adjoint else s
        return jnp.where(d == 0, (s + n_lat) % n_all, n_all - 1 - s)

    def per_dir(d, s):
        return (d, tile(d, s), 0)

    specs = [pl.BlockSpec((None, rows.tm, width), per_dir) if kind == "dir" else
             pl.BlockSpec((rows.tm, width), lambda d, s: (tile(d, s), 0)) for kind in in_specs]
    shape = jax.ShapeDtypeStruct((2, rows.t_all, width), F32)
    return pl.pallas_call(
        body, name=name, grid=(2, n_all), in_specs=specs, out_specs=[pl.BlockSpec((None, rows.tm, width), per_dir)] * n_out,
        out_shape=[shape] * n_out, scratch_shapes=[pltpu.VMEM((SUBLANES, width), F32)],
        compiler_params=_params("arbitrary", "arbitrary"))(*ins)


def lru_scan(rows, a, bb, name):
    width = a.shape[2]
    n = rows.tm

    def body(a_ref, bb_ref, h_ref, hp_ref, carry):
        d, s = pl.program_id(0), pl.program_id(1)

        @pl.when(s == 0)
        def _():
            carry[...] = jnp.zeros_like(carry)

        def run(reverse):
            for cs in _chunks(width, LANES):
                cum, h0 = _tile_scan(a_ref[:, cs], bb_ref[:, cs], reverse)
                h_in = carry[0:1, cs]
                h = h0 + cum * h_in
                h_ref[:, cs] = h
                hp_ref[:, cs] = _neighbour(h, h_in, reverse)
                carry[0:1, cs] = h[0:1] if reverse else h[n - 1:n]

        pl.when(d == 0)(lambda: run(False))
        pl.when(d == 1)(lambda: run(True))

    return _scan_call(rows, body, name, [a, bb], ["dir", "dir"], 2, width, adjoint=False)


def lru_scan_bwd(rows, dh, a, hp, name):
    width = a.shape[2]
    n = rows.tm

    def body(dh_ref, a_ref, hp_ref, da_ref, dbb_ref, carry):
        d, s = pl.program_id(0), pl.program_id(1)

        @pl.when(s == 0)
        def _():
            carry[...] = jnp.zeros_like(carry)

        def run(reverse):
            for cs in _chunks(width, LANES):
                av = a_ref[:, cs]
                cum, g0 = _tile_scan(_neighbour(av, 1.0, reverse), dh_ref[:, cs], reverse)
                g = g0 + cum * carry[0:1, cs]
                da_ref[:, cs] = g * hp_ref[:, cs]
                dbb_ref[:, cs] = g
                carry[0:1, cs] = (av * g)[0:1] if reverse else (av * g)[n - 1:n]

        pl.when(d == 0)(lambda: run(True))
        pl.when(d == 1)(lambda: run(False))

    return _scan_call(rows, body, name, [dh, a, hp], ["shared", "dir", "dir"], 2, width, adjoint=True)


def mod_matmul(c9, w_shard, b_shard, name):
    n = w_shard.shape[1]
    tn = _pick(n, 768)

    def body(c_ref, w_ref, b_ref, o_ref):
        act = _silu_and_grad(c_ref[...])[0]
        o_ref[...] = jnp.dot(act, w_ref[...], preferred_element_type=F32, precision=lax.Precision.HIGHEST) + b_ref[...]

    return pl.pallas_call(
        body, name=name, grid=(n // tn,),
        in_specs=[pl.BlockSpec(c9.shape, lambda j: (0, 0)), pl.BlockSpec((w_shard.shape[0], tn), lambda j: (0, j)),
                  pl.BlockSpec((1, tn), lambda j: (0, j))],
        out_specs=pl.BlockSpec((c9.shape[0], tn), lambda j: (0, j)), out_shape=jax.ShapeDtypeStruct((c9.shape[0], n), F32),
        compiler_params=_params("arbitrary"))(c9, w_shard, b_shard)


def mod_matmul_bwd(c9, d9, w_shard, name):
    n = w_shard.shape[1]
    tn = _pick(n, 768)
    steps = n // tn

    def body(c_ref, d_ref, w_ref, gw_ref, gc_ref):
        j = pl.program_id(0)
        act, dact = _silu_and_grad(c_ref[...])
        dv = d_ref[...]
        gw_ref[...] = lax.dot_general(act, dv, (((0,), (0,)), ((), ())), preferred_element_type=F32, precision=lax.Precision.HIGHEST)
        part = lax.dot_general(dv, w_ref[...], (((1,), (1,)), ((), ())), preferred_element_type=F32, precision=lax.Precision.HIGHEST)
        _accumulate(gc_ref, part * dact, j == 0)

    return pl.pallas_call(
        body, name=name, grid=(steps,),
        in_specs=[pl.BlockSpec(c9.shape, lambda j: (0, 0)), pl.BlockSpec((c9.shape[0], tn), lambda j: (0, j)),
                  pl.BlockSpec((w_shard.shape[0], tn), lambda j: (0, j))],
        out_specs=[pl.BlockSpec((w_shard.shape[0], tn), lambda j: (0, j)), pl.BlockSpec(c9.shape, lambda j: (0, 0))],
        out_shape=[jax.ShapeDtypeStruct(w_shard.shape, F32), jax.ShapeDtypeStruct(c9.shape, F32)],
        compiler_params=_params("arbitrary"))(c9, d9, w_shard)


def _row_tile(n_rows, n_cols):
    return _pick(n_rows, max(2 * SUBLANES, (256 * 1024) // n_cols), 2 * SUBLANES)


def sum_parts(parts, name):
    n, n_rows, n_cols = parts.shape
    tr = _row_tile(n_rows, n_cols)

    def body(p_ref, o_ref):
        total = p_ref[0].astype(F32)
        for k in range(1, n):
            total = total + p_ref[k].astype(F32)
        o_ref[...] = total

    return pl.pallas_call(
        body, name=name, grid=(n_rows // tr,), in_specs=[pl.BlockSpec((n, tr, n_cols), lambda i: (0, i, 0))],
        out_specs=pl.BlockSpec((tr, n_cols), lambda i: (i, 0)), out_shape=jax.ShapeDtypeStruct((n_rows, n_cols), F32),
        compiler_params=_params("arbitrary"))(parts)


def _adamw_update(g, w_ref, m_ref, v_ref, g_ref, d_ref, m2_ref, v2_ref):
    m2 = ADAM_B1 * m_ref[...] + (1.0 - ADAM_B1) * g
    v2 = ADAM_B2 * v_ref[...] + (1.0 - ADAM_B2) * (g * g)
    m_hat = m2 / (1.0 - ADAM_B1 ** ADAM_STEP)
    v_hat = v2 / (1.0 - ADAM_B2 ** ADAM_STEP)
    g_ref[...] = g
    d_ref[...] = -ADAM_LR * (m_hat / (jnp.sqrt(v_hat) + ADAM_EPS) + ADAM_WD * w_ref[...])
    m2_ref[...] = m2
    v2_ref[...] = v2


def adamw(parts, w, m, v, name):
    n, n_rows, n_cols = parts.shape
    tr = _row_tile(n_rows, n_cols)

    def body(p_ref, *refs):
        g = p_ref[0].astype(F32)
        for k in range(1, n):
            g = g + p_ref[k].astype(F32)
        _adamw_update(g, *refs)

    blk = pl.BlockSpec((tr, n_cols), lambda i: (i, 0))
    shape = jax.ShapeDtypeStruct((n_rows, n_cols), F32)
    return pl.pallas_call(
        body, name=name, grid=(n_rows // tr,), in_specs=[pl.BlockSpec((n, tr, n_cols), lambda i: (0, i, 0)), blk, blk, blk],
        out_specs=[blk] * 4, out_shape=[shape] * 4, compiler_params=_params("arbitrary"))(parts, w, m, v)


def adamw_pair(part, w, m, v, name):
    n_rows, n_cols = part.shape
    tr = _row_tile(n_rows, n_cols)
    steps = n_rows // tr

    def body(p_ref, w_ref, m_ref, v_ref, g_ref, d_ref, m2_ref, v2_ref, inbox, send_sems, recv_sems, credits):
        i = pl.program_id(0)
        slot = i % 2
        sibling = (lax.axis_index("x"), lax.axis_index("y"), 1 - lax.axis_index("c"))

        @pl.when(i >= 2)
        def _():
            pl.semaphore_wait(credits.at[slot], 1)

        copy = pltpu.make_async_remote_copy(src_ref=p_ref, dst_ref=inbox.at[slot], send_sem=send_sems.at[slot],
                                            recv_sem=recv_sems.at[slot], device_id=sibling, device_id_type=MESH)
        copy.start()
        copy.wait_recv()
        copy.wait_send()
        _adamw_update(p_ref[...] + inbox[slot], w_ref, m_ref, v_ref, g_ref, d_ref, m2_ref, v2_ref)

        @pl.when(i + 2 < steps)
        def _():
            pl.semaphore_signal(credits.at[slot], inc=1, device_id=sibling, device_id_type=MESH)

    blk = pl.BlockSpec((tr, n_cols), lambda i: (i, 0))
    shape = jax.ShapeDtypeStruct((n_rows, n_cols), F32)
    return pl.pallas_call(
        body, name=name, grid=(steps,), in_specs=[blk] * 4, out_specs=[blk] * 4, out_shape=[shape] * 4,
        scratch_shapes=[pltpu.VMEM((2, tr, n_cols), F32), pltpu.SemaphoreType.DMA((2,)), pltpu.SemaphoreType.DMA((2,)),
                        pltpu.SemaphoreType.REGULAR((2,))],
        compiler_params=_params("arbitrary"))(part, w, m, v)


def _pack(arrays, row_multiple=SUBLANES):
    flat = [jnp.pad(a.reshape(-1), (0, -a.size % LANES)) for a in arrays]
    buf = jnp.concatenate(flat)
    buf = jnp.pad(buf, (0, -buf.size % (row_multiple * LANES)))
    return buf.reshape(-1, LANES)


def _unpack(buf, shapes):
    lead = buf.shape[:-2]
    flat = buf.reshape(lead + (-1,))
    out, pos = [], 0
    for shape in shapes:
        size = 1
        for s in shape:
            size *= s
        out.append(flat[..., pos:pos + size].reshape(lead + tuple(shape)))
        pos += size + (-size % LANES)
    return out


def _pos_embedding(seq_len, dim):
    rows = seq_len // GRID_W
    q = dim // 4
    omega = 1.0 / (10000.0 ** (jnp.arange(q, dtype=F32) / q))
    er = jnp.arange(rows).astype(F32)[:, None] * omega
    ec = jnp.arange(GRID_W).astype(F32)[:, None] * omega
    by_row = jnp.broadcast_to(jnp.concatenate([jnp.sin(er), jnp.cos(er)], axis=-1)[:, None, :], (rows, GRID_W, 2 * q))
    by_col = jnp.broadcast_to(jnp.concatenate([jnp.sin(ec), jnp.cos(ec)], axis=-1)[None, :, :], (rows, GRID_W, 2 * q))
    return jnp.concatenate([by_row, by_col], axis=-1).reshape(rows * GRID_W, dim)


def _cols_from_shards(g):
    return jnp.transpose(g, (1, 0, 2)).reshape(g.shape[1], -1)


def _shards_from_cols(w, n_shards=4):
    k, n = w.shape
    return jnp.transpose(w.reshape(k, n_shards, n // n_shards), (1, 0, 2))


def kernel(x, c, ctx, c_ctx, w_mod, b_mod, g_n1, w_ffn1_up, w_ffn1_down, g_n2, w_in, b_in, w_dw, b_dw, g_ln, b_ln, w_conf_out, w_lru_conv, b_lru_conv, w_rec_gate, b_rec_gate, w_in_gate, b_in_gate, lru_lambda, w_lru_out, w_out, g_n3, w_ffn2_up, w_ffn2_down, g_final, loss_target, m_c_ctx, m_w_mod, m_b_mod, m_g_n1, m_w_ffn1_up, m_w_ffn1_down, m_g_n2, m_w_in, m_b_in, m_w_dw, m_b_dw, m_g_ln, m_b_ln, m_w_conf_out, m_w_lru_conv, m_b_lru_conv, m_w_rec_gate, m_b_rec_gate, m_w_in_gate, m_b_in_gate, m_lru_lambda, m_w_lru_out, m_w_out, m_g_n3, m_w_ffn2_up, m_w_ffn2_down, m_g_final, v_c_ctx, v_w_mod, v_b_mod, v_g_n1, v_w_ffn1_up, v_w_ffn1_down, v_g_n2, v_w_in, v_b_in, v_w_dw, v_b_dw, v_g_ln, v_b_ln, v_w_conf_out, v_w_lru_conv, v_b_lru_conv, v_w_rec_gate, v_b_rec_gate, v_w_in_gate, v_b_in_gate, v_lru_lambda, v_w_lru_out, v_w_out, v_g_n3, v_w_ffn2_up, v_w_ffn2_down, v_g_final):
    given = dict(locals())
    wt = {n: given[n] for n in W_NAMES}
    mom = {n: given["m_" + n] for n in W_NAMES}
    var = {n: given["v_" + n] for n in W_NAMES}

    def flat2(a):
        if a.ndim == 1:
            return a.reshape(1, -1)
        a = a[0]
        return a if a.ndim == 2 else a.reshape(-1, a.shape[-1])

    t_lat, d = x.shape[1], x.shape[2]
    t_ctx = ctx.shape[1]
    rows = Rows(t_lat, t_ctx, d)
    xi, yi, ci = lax.axis_index("x"), lax.axis_index("y"), lax.axis_index("c")
    me, chip = 4 * xi + 2 * yi + ci, 2 * xi + yi
    lru_w = b_lru_conv.shape[-1]
    n_blk = lru_w // LANES
    taps = w_dw.shape[1]
    lru_taps = w_lru_conv.shape[1]
    ffn = w_ffn1_down.shape[1] * 4

    small_shapes = [(1, d)] + [flat2(wt[n]).shape for n in SMALL_SHARDED]
    small_all = exchange("all", [_pack([c] + [flat2(wt[n]) for n in SMALL_SHARDED])], False, "gather_small")[0]
    small_all = _unpack(small_all, small_shapes)
    c_all = small_all[0][:, 0, :]
    full = {n: jnp.concatenate([a[0], a[2], a[4], a[6]], axis=-1) for n, a in zip(SMALL_SHARDED, small_all[1:])}
    big = COL_SHARDED + ROW_SHARDED
    ffn1_names = ['w_ffn1_up', 'w_ffn1_down']
    ffn2_names = ['w_ffn2_up', 'w_ffn2_down']
    mixer_names = [n for n in big if n not in ffn1_names + ffn2_names]
    wb = {}

    def take_weights(names, gathered):
        for n, g in zip(names, gathered):
            wb[n] = _cols_from_shards(g) if n in COL_SHARDED else g.reshape(-1, g.shape[-1])

    def shards_after(names, earlier):
        return lax.optimization_barrier(([wt[n][0].astype(BF16) for n in names], earlier))[0]

    c9 = jnp.concatenate([c_all, c_ctx.reshape(1, d), jnp.zeros((7, d), F32)], axis=0)
    mod_cols = mod_matmul(c9, w_mod[0], lax.dynamic_slice_in_dim(b_mod, chip * w_mod.shape[2], w_mod.shape[2], axis=1), "mod_matmul")
    mod_all = exchange("all", [mod_cols], False, "gather_mod")[0]
    mod9 = jnp.concatenate([mod_all[0], mod_all[2], mod_all[4], mod_all[6]], axis=-1)
    mv = jnp.stack([lax.dynamic_index_in_dim(mod9, me, 0, keepdims=False).reshape(N_MOD, d), mod9[8].reshape(N_MOD, d)])

    gathered_ffn1 = exchange("chips", shards_after(ffn1_names, mod_all), False, "gather_weights_ffn1")
    take_weights(ffn1_names, gathered_ffn1)
    gathered_mixer = exchange_by_sequencer(shards_after(mixer_names, gathered_ffn1), False, "gather_weights_mixer", 1)

    x_in = jnp.concatenate([x[0], ctx[0]], axis=0)
    pos = jnp.concatenate([_pos_embedding(t_lat, d), jnp.zeros((t_ctx, d), F32)], axis=0)

    def ffn_fwd(tag, h, w_up, w_down):
        g, u, act = matmul_swiglu(h, w_up, f"{tag}_up")
        return (g, u), act, matmul(act, w_down, "nn", F32, f"{tag}_down")

    x0, h1 = norm_mod(rows, x_in, g_n1, mv, 0, 1, "norm1", pos=pos)
    gu1, act1, f1 = ffn_fwd("ffn1", h1, wb["w_ffn1_up"], wb["w_ffn1_down"])
    gathered_mixer, f1 = lax.optimization_barrier((gathered_mixer, f1))
    take_weights(mixer_names, gathered_mixer)
    gathered_ffn2 = exchange_by_sequencer(shards_after(ffn2_names, gathered_mixer), False, "gather_weights_ffn2", 5)
    x1, h2 = norm_mod(rows, x0, g_n2, mv, 3, 4, "norm2", resid=(f1, 2, 0.5))

    conf_w = w_conf_out.shape[1] * 4
    col_groups = [(0, 2 * conf_w), (2 * conf_w, lru_w), (2 * conf_w + lru_w, lru_w), (2 * conf_w + 2 * lru_w, 2 * d)]
    pvg, ux, gl, pgc = [matmul(h2, wb["w_in"][:, s:s + n], "nn", BF16, f"in_proj{k}", bias=b_in[:, s:s + n])
                        for k, (s, n) in enumerate(col_groups)]
    u = glu(rows, pvg, "glu")
    u2 = dwconv(rows, u, full["w_dw"], b_dw, taps // 2, F32, "conf_conv")
    s_act = ln_silu(rows, u2, g_ln, b_ln, "conf_ln")
    yc = matmul(s_act, wb["w_conf_out"], "nn", BF16, "conf_out")

    xr = dwconv(rows, ux, full["w_lru_conv"], b_lru_conv, LRU_PAD_LEFT, F32, "lru_conv")
    w_rec, w_ing = w_rec_gate[0].astype(BF16), w_in_gate[0].astype(BF16)
    wcat = jnp.concatenate([w_rec[0], w_ing[0], w_rec[1], w_ing[1]], axis=-1)

    def per_block(a):
        return jnp.transpose(a.reshape(2, n_blk, LANES), (1, 0, 2)).reshape(n_blk, 1, 2 * LANES)

    b_rec, b_ing = full["b_rec_gate"].reshape(2, n_blk, LANES), full["b_in_gate"].reshape(2, n_blk, LANES)
    bcat = jnp.concatenate([b_rec[0], b_ing[0], b_rec[1], b_ing[1]], axis=-1).reshape(n_blk, 1, 4 * LANES)
    lam = per_block(full["lru_lambda"])
    a_gate, b_gate = lru_gates(rows, xr, wcat, bcat, lam, "lru_gates")
    h_scan, h_prev = lru_scan(rows, a_gate, b_gate, "lru_scan")
    yl_in = lru_merge(rows, h_scan, gl, "lru_merge")
    yl = matmul(yl_in, wb["w_lru_out"], "nn", BF16, "lru_out")
    z = z_merge(rows, pgc, yc, yl, "z_merge")
    y = matmul(z, wb["w_out"], "nn", F32, "mix_out")

    gathered_ffn2, y = lax.optimization_barrier((gathered_ffn2, y))
    take_weights(ffn2_names, gathered_ffn2)
    x2, h3 = norm_mod(rows, x1, g_n3, mv, 6, 7, "norm3", resid=(y, 5, 1.0))
    gu2, act2, f2 = ffn_fwd("ffn2", h3, wb["w_ffn2_up"], wb["w_ffn2_down"])
    loss_part, dx3, dg_final = loss_head(rows, x2, f2, mv, g_final.reshape(1, d), loss_target[0], "loss_head")
    loss = lax.psum(loss_part[0, 0], ("x", "y", "c"))

    grads = {"g_final": dg_final}
    dmv = [None] * N_MOD

    def ffn_bwd(tag, dxn, x_prev, h, gu, act, f, w_up, w_down, g, ks, collective_id, settle_first=()):
        k_shift, k_scale, k_gate = ks
        df, dmv[k_gate] = gate_bwd(rows, dxn, f, mv, k_gate, 0.5, f"{tag}_gate_bwd")
        grads[f"w_{tag}_down"] = matmul(act, df, "tn", F32, f"{tag}_down_dw")
        dact = matmul(df, w_down, "nt", BF16, f"{tag}_down_dx")
        if settle_first:
            dact = settle(settle_first, dact)
        dgu = swiglu_bwd(rows, gu[0], gu[1], dact, f"{tag}_act_bwd")
        grads[f"w_{tag}_up"] = matmul(h, dgu, "tn", F32, f"{tag}_up_dw")
        dgu = scatter_behind([f"w_{tag}_up", f"w_{tag}_down"], tag, collective_id, dgu)
        dh = matmul(dgu, w_up, "nt", F32, f"{tag}_up_dx")
        dx, dg, dmv[k_shift], dmv[k_scale] = norm_mod_bwd(rows, x_prev, dh, dxn, g, mv, k_shift, k_scale, f"{tag}_norm_bwd")
        return dx, dg

    def grad_pieces(names):
        return [(_shards_from_cols(grads[n]) if n in COL_SHARDED else grads[n].reshape(4, -1, grads[n].shape[-1])).astype(BF16)
                for n in names]

    from_chips = {}

    def scatter_behind(names, tag, collective_id, carry):
        ready, carry = lax.optimization_barrier(([grads[n] for n in names], carry))
        grads.update(zip(names, ready))
        from_chips.update(zip(names, exchange_by_sequencer(grad_pieces(names), True, f"scatter_grads_{tag}", collective_id)))
        return carry

    def settle(names, carry):
        landed, carry = lax.optimization_barrier(([from_chips[n] for n in names], carry))
        from_chips.update(zip(names, landed))
        return carry

    dx2, grads["g_n3"] = ffn_bwd("ffn2", dx3, x2, h3, gu2, act2, f2, wb["w_ffn2_up"], wb["w_ffn2_down"], g_n3, (6, 7, 8), 2)

    dy, dmv[5] = gate_bwd(rows, dx2, y, mv, 5, 1.0, "mix_gate_bwd")
    grads["w_out"] = matmul(z, dy, "tn", F32, "mix_out_dw")
    dz = matmul(dy, wb["w_out"], "nt", BF16, "mix_out_dx")
    dyc, dyl, dpgc = z_merge_bwd(rows, pgc, yc, yl, dz, "z_merge_bwd")

    grads["w_conf_out"] = matmul(s_act, dyc, "tn", F32, "conf_out_dw")
    ds_act = matmul(dyc, wb["w_conf_out"], "nt", BF16, "conf_out_dx")
    du2, grads["g_ln"], grads["b_ln"] = ln_silu_bwd(rows, u2, ds_act, g_ln, b_ln, "conf_ln_bwd")
    grads["w_dw"], grads["b_dw"] = dwconv_bwd_w(rows, u, du2, taps, taps // 2, "conf_conv_dw")
    du = dwconv(rows, du2, full["w_dw"][::-1], None, taps - 1 - taps // 2, F32, "conf_conv_dx")
    dpvg = glu_bwd(rows, pvg, du, "glu_bwd")

    grads["w_lru_out"] = matmul(yl_in, dyl, "tn", F32, "lru_out_dw")
    dyl = scatter_behind(['w_out', 'w_conf_out', 'w_lru_out'], "mixer", 3, settle(ffn2_names, dyl))
    dyl_in = matmul(dyl, wb["w_lru_out"], "nt", BF16, "lru_out_dx")
    dh_sum, dgl = lru_merge_bwd(rows, h_scan, gl, dyl_in, "lru_merge_bwd")
    da_gate, db_gate = lru_scan_bwd(rows, dh_sum, a_gate, h_prev, "lru_scan_bwd")
    dxr, dwcat, dbcat, dlam = lru_gates_bwd(rows, xr, wcat, bcat, lam, da_gate, db_gate, "lru_gates_bwd")
    grads["w_rec_gate"] = jnp.stack([dwcat[:, :, 0:LANES], dwcat[:, :, 2 * LANES:3 * LANES]])
    grads["w_in_gate"] = jnp.stack([dwcat[:, :, LANES:2 * LANES], dwcat[:, :, 3 * LANES:4 * LANES]])
    dbcat = dbcat.reshape(n_blk, 4, LANES)
    grads["b_rec_gate"] = jnp.stack([dbcat[:, 0], dbcat[:, 2]]).reshape(2, lru_w)
    grads["b_in_gate"] = jnp.stack([dbcat[:, 1], dbcat[:, 3]]).reshape(2, lru_w)
    grads["lru_lambda"] = jnp.transpose(dlam.reshape(n_blk, 2, LANES), (1, 0, 2)).reshape(2, lru_w)
    grads["w_lru_conv"], grads["b_lru_conv"] = dwconv_bwd_w(rows, ux, dxr, lru_taps, LRU_PAD_LEFT, "lru_conv_dw")
    dux = dwconv(rows, dxr, full["w_lru_conv"][::-1], None, lru_taps - 1 - LRU_PAD_LEFT, BF16, "lru_conv_dx")

    dproj = jnp.concatenate([dpvg, dux, dgl, dpgc], axis=1)
    grads["b_in"] = col_sum(rows, dproj, "in_proj_db")
    grads["w_in"] = matmul(h2, dproj, "tn", F32, "in_proj_dw")
    dproj = scatter_behind(['w_in'], "in_proj", 4, settle(['w_out', 'w_conf_out', 'w_lru_out'], dproj))
    dh2 = matmul(dproj, wb["w_in"], "nt", F32, "in_proj_dx")
    dx1, grads["g_n2"], dmv[3], dmv[4] = norm_mod_bwd(rows, x1, dh2, dx2, g_n2, mv, 3, 4, "norm2_bwd")

    dx0, grads["g_n1"] = ffn_bwd("ffn1", dx1, x0, h1, gu1, act1, f1, wb["w_ffn1_up"], wb["w_ffn1_down"], g_n1, (0, 1, 2), 6, settle_first=['w_in'])
    grad_x = dx0[:t_lat][None]

    chip_sums = [sum_parts(from_chips[n].reshape(4, -1, from_chips[n].shape[-1]), f"chip_sum_{n}") for n in big]
    out = {}
    for n, p in zip(big, chip_sums):
        out[n] = adamw_pair(p, flat2(wt[n]), flat2(mom[n]), flat2(var[n]), f"adamw_{n}")

    dmod = jnp.concatenate(dmv, axis=1)
    small_names = ['g_n1', 'g_n2', 'b_in', 'b_dw', 'g_ln', 'b_ln', 'b_lru_conv', 'w_rec_gate', 'w_in_gate', 'g_n3', 'g_final'] + SMALL_SHARDED
    small_list = [grads[n] for n in small_names] + [dmod[0], dmod[1]]
    small_buf = _pack(small_list, 8 * 2 * SUBLANES)
    small_parts, dmod_all = exchange("all", [small_buf.reshape(8, -1, LANES), _pack([dmod[0]])], [True, False], "scatter_small_grads")
    small_eighth = sum_parts(small_parts, "sum_small_grads")
    small_sum = exchange("all", [small_eighth], False, "gather_small_grads")[0].reshape(small_buf.shape)
    small_sum = _unpack(small_sum, [a.shape for a in small_list])
    total = dict(zip(small_names, small_sum))
    dmod_all = _unpack(dmod_all, [dmod[0].shape])[0].reshape(8, N_MOD * d)
    dmc = small_sum[-1].reshape(1, N_MOD * d)
    total["b_mod"] = small_sum[-2].reshape(1, N_MOD * d) + dmc

    d9 = jnp.concatenate([dmod_all, dmc, jnp.zeros((7, N_MOD * d), F32)], axis=0)
    d9_cols = lax.dynamic_slice_in_dim(d9, chip * w_mod.shape[2], w_mod.shape[2], axis=1)
    g_wmod, dc9 = mod_matmul_bwd(c9, d9_cols, w_mod[0], "mod_matmul_bwd")
    dc_all = exchange("all", [dc9[8:16]], False, "gather_dc")[0]
    total["c_ctx"] = sum_parts(jnp.stack([dc_all[0], dc_all[2], dc_all[4], dc_all[6]]), "sum_dc")[0:1]
    out["w_mod"] = adamw(g_wmod[None], w_mod[0], m_w_mod[0], v_w_mod[0], "adamw_w_mod")

    small_all_names = [n for n in W_NAMES if n not in out]
    g_local, shapes = [], []
    for n in small_all_names:
        g = total[n].reshape(flat2(wt[n]).shape[:-1] + (-1,)) if n in SMALL_SHARDED else total[n].reshape(flat2(wt[n]).shape)
        if n in SMALL_SHARDED:
            width = wt[n].shape[-1]
            g = lax.dynamic_slice_in_dim(g, chip * width, width, axis=1)
        g_local.append(g)
        shapes.append(g.shape)
    packed = adamw(_pack(g_local)[None], _pack([flat2(wt[n]) for n in small_all_names]),
                   _pack([flat2(mom[n]) for n in small_all_names]), _pack([flat2(var[n]) for n in small_all_names]), "adamw_small")
    for k, n in enumerate(small_all_names):
        out[n] = [_unpack(p, shapes)[k] for p in packed]

    results = [loss, grad_x]
    for k in range(4):
        results += [out[n][k].reshape(wt[n].shape) for n in W_NAMES]
    return tuple(results)
```

```python
import functools

import jax
import jax.numpy as jnp
from jax import lax
from jax.experimental import pallas as pl
from jax.experimental.pallas import tpu as pltpu
from jax.experimental.pallas import tpu_sc as plsc

F32 = jnp.float32
BF16 = jnp.bfloat16
MESH = pl.DeviceIdType.MESH

EPS = 1e-6
GRID_W = 64
N_MOD = 9
LRU_C = 8.0
LRU_PAD_LEFT = 2
ADAM_LR, ADAM_B1, ADAM_B2, ADAM_EPS, ADAM_WD, ADAM_STEP = 0.001, 0.9, 0.999, 1e-08, 0.01, 10

LANES = 128
SUBLANES = 8
HALO = 16
VMEM_LIMIT = 56 * 1024 * 1024

W_NAMES = ['c_ctx', 'w_mod', 'b_mod', 'g_n1', 'w_ffn1_up', 'w_ffn1_down', 'g_n2', 'w_in', 'b_in', 'w_dw', 'b_dw',
           'g_ln', 'b_ln', 'w_conf_out', 'w_lru_conv', 'b_lru_conv', 'w_rec_gate', 'b_rec_gate', 'w_in_gate',
           'b_in_gate', 'lru_lambda', 'w_lru_out', 'w_out', 'g_n3', 'w_ffn2_up', 'w_ffn2_down', 'g_final']
COL_SHARDED = ['w_ffn1_up', 'w_in', 'w_ffn2_up']
ROW_SHARDED = ['w_ffn1_down', 'w_conf_out', 'w_lru_out', 'w_out', 'w_ffn2_down']
SMALL_SHARDED = ['w_dw', 'w_lru_conv', 'b_rec_gate', 'b_in_gate', 'lru_lambda']


def _params(*semantics):
    return pltpu.CompilerParams(dimension_semantics=semantics, vmem_limit_bytes=VMEM_LIMIT)


def _pick(n, target, mult=LANES):
    best = None
    for t in range(mult, min(n, target) + 1, mult):
        if n % t == 0:
            best = t
    return best or n


def _chunks(width, target=512):
    w = _pick(width, target)
    return [slice(s, s + w) for s in range(0, width, w)]


def _sigmoid(x):
    return jax.nn.sigmoid(x)


def _silu_and_grad(x):
    s = _sigmoid(x)
    return x * s, s * (1.0 + x * (1.0 - s))


_GELU_K = 0.7978845608028654


def _gelu_and_grad(x):
    t = jnp.tanh(_GELU_K * (x + 0.044715 * (x * x * x)))
    return 0.5 * x * (1.0 + t), 0.5 * (1.0 + t) + 0.5 * x * (1.0 - t * t) * _GELU_K * (1.0 + 3 * 0.044715 * x * x)


def _neg_expm1(z):
    series = -(z * (1.0 + z * (0.5 + z * (1.0 / 6.0))))
    return jnp.where(z > -0.01, series, 1.0 - jnp.exp(z))


def _softplus(x):
    return jnp.maximum(x, 0.0) + jnp.log1p(jnp.exp(-jnp.abs(x)))


def _accumulate(ref, value, first):
    @pl.when(first)
    def _():
        ref[...] = value

    @pl.when(jnp.logical_not(first))
    def _():
        ref[...] += value


def _colsum(x):
    return jnp.sum(x, axis=0, keepdims=True)


class Rows:
    def __init__(self, t_lat, t_ctx, d_model):
        self.tm = _pick(t_ctx, 256, HALO)
        assert t_lat % self.tm == 0 and t_ctx % self.tm == 0
        self.n_lat = t_lat // self.tm
        self.n_all = (t_lat + t_ctx) // self.tm
        self.t_all = t_lat + t_ctx
        self.d = d_model

    def seg(self, i):
        return jnp.where(i >= self.n_lat, 1, 0)

    def seg_first(self, i):
        return jnp.logical_or(i == 0, i == self.n_lat)

    def seg_last(self, i):
        return jnp.logical_or(i == self.n_lat - 1, i == self.n_all - 1)

    def row(self, width, col=0):
        return pl.BlockSpec((self.tm, width), lambda i: (i, col))

    def vec(self, width):
        return pl.BlockSpec((1, width), lambda i: (0, 0))

    def full(self, shape):
        return pl.BlockSpec(shape, lambda i: (0,) * len(shape))

    def mod(self):
        return pl.BlockSpec((None, N_MOD, self.d), lambda i: (self.seg(i), 0, 0))

    def segsum(self):
        return pl.BlockSpec((None, 1, self.d), lambda i: (self.seg(i), 0, 0))

    def segsum_shape(self):
        return jax.ShapeDtypeStruct((2, 1, self.d), F32)

    def halo(self, width, side):
        per = self.tm // HALO
        last = self.t_all // HALO - 1
        if side < 0:
            return pl.BlockSpec((HALO, width), lambda i: (jnp.maximum(i * per - 1, 0), 0))
        return pl.BlockSpec((HALO, width), lambda i: (jnp.minimum((i + 1) * per, last), 0))

    def call(self, body, name, in_specs, out_specs, out_shape, scratch=()):
        return pl.pallas_call(body, name=name, grid=(self.n_all,), in_specs=in_specs, out_specs=out_specs,
                              out_shape=out_shape, scratch_shapes=list(scratch), compiler_params=_params("arbitrary"))


_PEER_FLIPS = {
    "chips": [(1, 0, 0), (0, 1, 0), (1, 1, 0)],
    "sibling": [(0, 0, 1)],
    "all": [(0, 0, 1), (0, 1, 0), (0, 1, 1), (1, 0, 0), (1, 0, 1), (1, 1, 0), (1, 1, 1)],
}


def _slot(kind, x, y, c):
    return {"chips": 2 * x + y, "sibling": c, "all": 4 * x + 2 * y + c}[kind]


def exchange(kind, srcs, indexed, name):
    flips = _PEER_FLIPS[kind]
    n_slots = len(flips) + 1
    n_t = len(srcs)
    by_slot = list(indexed) if isinstance(indexed, (list, tuple)) else [indexed] * n_t
    out_shapes = [jax.ShapeDtypeStruct(s.shape if ix else (n_slots,) + s.shape, s.dtype) for s, ix in zip(srcs, by_slot)]

    def body(*refs):
        src_refs, dst_refs = refs[:n_t], refs[n_t:2 * n_t]
        send_sems, recv_sems, local_sems = refs[2 * n_t:]
        x, y, c = lax.axis_index("x"), lax.axis_index("y"), lax.axis_index("c")
        me = _slot(kind, x, y, c)

        def piece(t, k):
            return src_refs[t].at[k] if by_slot[t] else src_refs[t]

        peers = [(1 - x if fx else x, 1 - y if fy else y, 1 - c if fc else c) for fx, fy, fc in flips]
        local = [pltpu.make_async_copy(piece(t, me), dst_refs[t].at[me], local_sems.at[t]) for t in range(n_t)]
        for cp in local:
            cp.start()
        sends = []
        for t in range(n_t):
            for p, peer in enumerate(peers):
                sends.append(pltpu.make_async_remote_copy(
                    src_ref=piece(t, _slot(kind, *peer)), dst_ref=dst_refs[t].at[me],
                    send_sem=send_sems.at[t, p], recv_sem=recv_sems.at[t, p], device_id=peer, device_id_type=MESH))
        for cp in sends:
            cp.start()
        for t in range(n_t):
            for p, peer in enumerate(peers):
                theirs = _slot(kind, *peer)
                pltpu.make_async_remote_copy(
                    src_ref=piece(t, me), dst_ref=dst_refs[t].at[theirs],
                    send_sem=send_sems.at[t, p], recv_sem=recv_sems.at[t, p], device_id=peer, device_id_type=MESH).wait_recv()
        for cp in sends:
            cp.wait_send()
        for cp in local:
            cp.wait()

    any_spec = pl.BlockSpec(memory_space=pl.ANY)
    outs = pl.pallas_call(
        body, name=name, out_shape=out_shapes, in_specs=[any_spec] * n_t, out_specs=[any_spec] * n_t,
        scratch_shapes=[pltpu.SemaphoreType.DMA((n_t, len(flips))), pltpu.SemaphoreType.DMA((n_t, len(flips))),
                        pltpu.SemaphoreType.DMA((n_t,))],
        compiler_params=pltpu.CompilerParams(has_side_effects=True),
    )(*srcs)
    return list(outs)


def exchange_by_sequencer(srcs, indexed, name, collective_id):
    flips = _PEER_FLIPS["chips"]
    n_t, n_p = len(srcs), len(flips)
    src_refs = [jax.new_ref(s, memory_space=pltpu.MemorySpace.HBM) for s in srcs]
    land_refs = [jax.empty_ref(jax.ShapeDtypeStruct(s.shape if indexed else (n_p + 1,) + s.shape, s.dtype),
                               memory_space=pltpu.MemorySpace.HBM) for s in srcs]

    @pl.kernel(mesh=plsc.ScalarSubcoreMesh(axis_name="sequencer", num_cores=1), name=name,
               scratch_types=(pltpu.SemaphoreType.DMA((n_t, n_p)), pltpu.SemaphoreType.DMA((n_t, n_p)), pltpu.SemaphoreType.DMA((n_t,))),
               compiler_params=pltpu.CompilerParams(collective_id=collective_id))
    def launch(send_sems, recv_sems, local_sems):
        x, y, c = lax.axis_index("x"), lax.axis_index("y"), lax.axis_index("c")
        peers = [(1 - x if fx else x, 1 - y if fy else y, c) for fx, fy, _ in flips]
        barrier = pltpu.get_barrier_semaphore()
        for peer in peers:
            pl.semaphore_signal(barrier, inc=1, device_id=peer, device_id_type=MESH)
        pl.semaphore_wait(barrier, n_p)
        me = _slot("chips", x, y, c)

        def piece(t, k):
            return src_refs[t].at[k] if indexed else src_refs[t]

        local = [pltpu.make_async_copy(piece(t, me), land_refs[t].at[me], local_sems.at[t]) for t in range(n_t)]
        for cp in local:
            cp.start()
        sends = []
        for t in range(n_t):
            for p, peer in enumerate(peers):
                sends.append(pltpu.make_async_remote_copy(
                    src_ref=piece(t, _slot("chips", *peer)), dst_ref=land_refs[t].at[me],
                    send_sem=send_sems.at[t, p], recv_sem=recv_sems.at[t, p], device_id=peer, device_id_type=MESH))
        for cp in sends:
            cp.start()
        for t in range(n_t):
            for p, peer in enumerate(peers):
                pltpu.make_async_remote_copy(
                    src_ref=piece(t, me), dst_ref=land_refs[t].at[_slot("chips", *peer)],
                    send_sem=send_sems.at[t, p], recv_sem=recv_sems.at[t, p], device_id=peer, device_id_type=MESH).wait_recv()
        for cp in sends:
            cp.wait_send()
        for cp in local:
            cp.wait()

    launch()
    return [r[...] for r in land_refs]


def matmul(a, b, mode, out_dtype, name, bias=None):
    if mode == "nn":
        (n_i, n_r), (_, n_j) = a.shape, b.shape
    elif mode == "nt":
        (n_i, n_r), (n_j, _) = a.shape, b.shape
    else:
        (n_r, n_i), (_, n_j) = a.shape, b.shape
    ti = _pick(n_i, 1408)
    tj = _pick(n_j, 1664)
    tr = n_r if (mode == "nn" and n_r <= 2816) else _pick(n_r, 768 if mode == "tn" else 1664)
    steps = n_r // tr
    bytes_a, bytes_b = a.size * a.dtype.itemsize, b.size * b.dtype.itemsize
    j_outer = steps == 1 and bytes_b + (n_j // tj) * bytes_a < bytes_a + (n_i // ti) * bytes_b

    def at(fn):
        return (lambda j, i, r: fn(i, j, r)) if j_outer else fn

    a_spec = {"nn": pl.BlockSpec((ti, tr), at(lambda i, j, r: (i, r))), "nt": pl.BlockSpec((ti, tr), at(lambda i, j, r: (i, r))),
              "tn": pl.BlockSpec((tr, ti), at(lambda i, j, r: (r, i)))}[mode]
    b_spec = {"nn": pl.BlockSpec((tr, tj), at(lambda i, j, r: (r, j))), "nt": pl.BlockSpec((tj, tr), at(lambda i, j, r: (j, r))),
              "tn": pl.BlockSpec((tr, tj), at(lambda i, j, r: (r, j)))}[mode]
    dims = {"nn": (((1,), (0,)), ((), ())), "nt": (((1,), (1,)), ((), ())), "tn": (((0,), (0,)), ((), ()))}[mode]

    def body(*refs):
        a_ref, b_ref = refs[:2]
        bias_ref = refs[2] if bias is not None else None
        o_ref = refs[3] if bias is not None else refs[2]
        prod = lax.dot_general(a_ref[...].astype(BF16), b_ref[...].astype(BF16), dims, preferred_element_type=F32)

        def finish(total):
            if bias_ref is not None:
                total = total + bias_ref[...]
            o_ref[...] = total.astype(out_dtype)

        if steps == 1:
            finish(prod)
        else:
            acc = refs[-1]
            r = pl.program_id(2)

            @pl.when(r == 0)
            def _():
                acc[...] = prod

            @pl.when(r > 0)
            def _():
                acc[...] += prod

            @pl.when(r == steps - 1)
            def _():
                finish(acc[...])

    in_specs = [a_spec, b_spec] + ([pl.BlockSpec((1, tj), at(lambda i, j, r: (0, j)))] if bias is not None else [])
    args = (a, b) + ((bias,) if bias is not None else ())
    grid = (n_j // tj, n_i // ti, steps) if j_outer else (n_i // ti, n_j // tj, steps)
    return pl.pallas_call(
        body, name=name, grid=grid, in_specs=in_specs,
        out_specs=pl.BlockSpec((ti, tj), at(lambda i, j, r: (i, j))), out_shape=jax.ShapeDtypeStruct((n_i, n_j), out_dtype),
        scratch_shapes=[pltpu.VMEM((ti, tj), F32)] if steps > 1 else [],
        compiler_params=_params("arbitrary", "arbitrary", "arbitrary"),
    )(*args)


def matmul_swiglu(h, w_up, name):
    n_t, d = h.shape
    f = w_up.shape[1] // 2
    ti, tj = _pick(n_t, 384), _pick(f, 1408)
    n_j = f // tj

    def body(h_ref, wg_ref, wu_ref, g_ref, u_ref, act_ref):
        hv = h_ref[...]
        g = jnp.dot(hv, wg_ref[...], preferred_element_type=F32)
        u = jnp.dot(hv, wu_ref[...], preferred_element_type=F32)
        g_ref[...] = g.astype(BF16)
        u_ref[...] = u.astype(BF16)
        act_ref[...] = (_silu_and_grad(g)[0] * u).astype(BF16)

    out = pl.BlockSpec((ti, tj), lambda j, i: (i, j))
    shape = jax.ShapeDtypeStruct((n_t, f), BF16)
    return pl.pallas_call(
        body, name=name, grid=(n_j, n_t // ti),
        in_specs=[pl.BlockSpec((ti, d), lambda j, i: (i, 0)), pl.BlockSpec((d, tj), lambda j, i: (0, j)),
                  pl.BlockSpec((d, tj), lambda j, i: (0, j + n_j))],
        out_specs=[out, out, out], out_shape=[shape, shape, shape], compiler_params=_params("arbitrary", "arbitrary"),
    )(h, w_up, w_up)


def norm_mod(rows, x, g, mv, shift_k, scale_k, name, resid=None, pos=None):
    d = rows.d
    changed = resid is not None or pos is not None

    def body(*refs):
        refs = list(refs)
        x_ref, g_ref, mv_ref = refs[:3]
        rest = refs[3:]
        xv = x_ref[...]
        if pos is not None:
            xv = xv + rest.pop(0)[...]
        if resid is not None:
            xv = xv + resid[2] * mv_ref[resid[1]:resid[1] + 1, :] * rest.pop(0)[...]
        if changed:
            rest.pop(0)[...] = xv
        r = lax.rsqrt(jnp.mean(xv * xv, axis=-1, keepdims=True) + EPS)
        h = (xv * r) * g_ref[...]
        if shift_k is not None:
            h = h * (1.0 + mv_ref[scale_k:scale_k + 1, :]) + mv_ref[shift_k:shift_k + 1, :]
        rest.pop(0)[...] = h.astype(BF16)

    ins = [x, g, mv] + ([pos] if pos is not None else []) + ([resid[0]] if resid is not None else [])
    in_specs = [rows.row(d), rows.vec(d), rows.mod()] + [rows.row(d)] * (len(ins) - 3)
    out_shape = ([jax.ShapeDtypeStruct((rows.t_all, d), F32)] if changed else []) + [jax.ShapeDtypeStruct((rows.t_all, d), BF16)]
    outs = rows.call(body, name, in_specs, [rows.row(d)] * len(out_shape), out_shape)(*ins)
    return (outs[0], outs[1]) if changed else (None, outs[0])


def norm_mod_bwd(rows, x, dh, dxn, g, mv, shift_k, scale_k, name):
    d = rows.d

    def body(x_ref, dh_ref, dxn_ref, g_ref, mv_ref, dx_ref, dg_ref, dsh_ref, dsc_ref):
        i = pl.program_id(0)
        xv, dhv, gv = x_ref[...], dh_ref[...], g_ref[...]
        r = lax.rsqrt(jnp.mean(xv * xv, axis=-1, keepdims=True) + EPS)
        n = xv * r
        dy = dhv * (1.0 + mv_ref[scale_k:scale_k + 1, :])
        dn = dy * gv
        dx_ref[...] = dxn_ref[...] + r * (dn - n * jnp.mean(dn * n, axis=-1, keepdims=True))
        _accumulate(dg_ref, _colsum(dy * n), i == 0)
        _accumulate(dsh_ref, _colsum(dhv), rows.seg_first(i))
        _accumulate(dsc_ref, _colsum(dhv * (n * gv)), rows.seg_first(i))

    return rows.call(
        body, name, [rows.row(d), rows.row(d), rows.row(d), rows.vec(d), rows.mod()],
        [rows.row(d), rows.vec(d), rows.segsum(), rows.segsum()],
        [jax.ShapeDtypeStruct((rows.t_all, d), F32), jax.ShapeDtypeStruct((1, d), F32), rows.segsum_shape(), rows.segsum_shape()],
    )(x, dh, dxn, g, mv)


def gate_bwd(rows, dxn, f, mv, gate_k, scale, name):
    d = rows.d

    def body(dxn_ref, f_ref, mv_ref, df_ref, dgate_ref):
        i = pl.program_id(0)
        dv = dxn_ref[...]
        df_ref[...] = (scale * mv_ref[gate_k:gate_k + 1, :] * dv).astype(BF16)
        _accumulate(dgate_ref, scale * _colsum(dv * f_ref[...]), rows.seg_first(i))

    return rows.call(body, name, [rows.row(d), rows.row(d), rows.mod()], [rows.row(d), rows.segsum()],
                     [jax.ShapeDtypeStruct((rows.t_all, d), BF16), rows.segsum_shape()])(dxn, f, mv)


def swiglu_bwd(rows, g, u, dact, name):
    f = g.shape[1]

    def body(g_ref, u_ref, da_ref, o_ref):
        for cs in _chunks(f):
            act, dact_dg = _silu_and_grad(g_ref[:, cs].astype(F32))
            dav = da_ref[:, cs].astype(F32)
            o_ref[:, cs] = (dav * u_ref[:, cs].astype(F32) * dact_dg).astype(BF16)
            o_ref[:, slice(f + cs.start, f + cs.stop)] = (dav * act).astype(BF16)

    return rows.call(body, name, [rows.row(f), rows.row(f), rows.row(f)], rows.row(2 * f),
                     jax.ShapeDtypeStruct((rows.t_all, 2 * f), BF16))(g, u, dact)


def glu(rows, pvg, name):
    w = pvg.shape[1] // 2

    def body(v_ref, t_ref, o_ref):
        for cs in _chunks(w):
            o_ref[:, cs] = v_ref[:, cs].astype(F32) * _sigmoid(t_ref[:, cs].astype(F32))

    return rows.call(body, name, [rows.row(w, 0), rows.row(w, 1)], rows.row(w), jax.ShapeDtypeStruct((rows.t_all, w), F32))(pvg, pvg)


def glu_bwd(rows, pvg, du, name):
    w = pvg.shape[1] // 2

    def body(v_ref, t_ref, du_ref, o_ref):
        for cs in _chunks(w):
            s = _sigmoid(t_ref[:, cs].astype(F32))
            duv = du_ref[:, cs]
            o_ref[:, cs] = (duv * s).astype(BF16)
            o_ref[:, slice(w + cs.start, w + cs.stop)] = (duv * v_ref[:, cs].astype(F32) * s * (1.0 - s)).astype(BF16)

    return rows.call(body, name, [rows.row(w, 0), rows.row(w, 1), rows.row(w)], rows.row(2 * w),
                     jax.ShapeDtypeStruct((rows.t_all, 2 * w), BF16))(pvg, pvg, du)


def ln_silu(rows, u, g, b, name):
    w = u.shape[1]

    def body(u_ref, g_ref, b_ref, o_ref):
        uv = u_ref[...]
        xc = uv - jnp.mean(uv, axis=-1, keepdims=True)
        n = xc * lax.rsqrt(jnp.mean(xc * xc, axis=-1, keepdims=True) + EPS)
        o_ref[...] = _silu_and_grad(n * g_ref[...] + b_ref[...])[0].astype(BF16)

    return rows.call(body, name, [rows.row(w), rows.vec(w), rows.vec(w)], rows.row(w),
                     jax.ShapeDtypeStruct((rows.t_all, w), BF16))(u, g, b)


def ln_silu_bwd(rows, u, ds, g, b, name):
    w = u.shape[1]

    def body(u_ref, ds_ref, g_ref, b_ref, du_ref, dg_ref, db_ref):
        i = pl.program_id(0)
        uv, gv = u_ref[...], g_ref[...]
        xc = uv - jnp.mean(uv, axis=-1, keepdims=True)
        r = lax.rsqrt(jnp.mean(xc * xc, axis=-1, keepdims=True) + EPS)
        n = xc * r
        dy = ds_ref[...].astype(F32) * _silu_and_grad(n * gv + b_ref[...])[1]
        dn = dy * gv
        du_ref[...] = r * (dn - jnp.mean(dn, axis=-1, keepdims=True) - n * jnp.mean(dn * n, axis=-1, keepdims=True))
        _accumulate(dg_ref, _colsum(dy * n), i == 0)
        _accumulate(db_ref, _colsum(dy), i == 0)

    return rows.call(body, name, [rows.row(w), rows.row(w), rows.vec(w), rows.vec(w)], [rows.row(w), rows.vec(w), rows.vec(w)],
                     [jax.ShapeDtypeStruct((rows.t_all, w), F32), jax.ShapeDtypeStruct((1, w), F32),
                      jax.ShapeDtypeStruct((1, w), F32)])(u, ds, g, b)


def lru_merge(rows, h, gl, name):
    w = gl.shape[1]

    def body(h_ref, gl_ref, o_ref):
        for cs in _chunks(w):
            o_ref[:, cs] = ((h_ref[0, :, cs] + h_ref[1, :, cs]) * _gelu_and_grad(gl_ref[:, cs].astype(F32))[0]).astype(BF16)

    return rows.call(body, name, [pl.BlockSpec((2, rows.tm, w), lambda i: (0, i, 0)), rows.row(w)], rows.row(w),
                     jax.ShapeDtypeStruct((rows.t_all, w), BF16))(h, gl)


def lru_merge_bwd(rows, h, gl, dy, name):
    w = gl.shape[1]

    def body(h_ref, gl_ref, dy_ref, dh_ref, dgl_ref):
        for cs in _chunks(w):
            act, dact = _gelu_and_grad(gl_ref[:, cs].astype(F32))
            dyv = dy_ref[:, cs].astype(F32)
            dh_ref[:, cs] = dyv * act
            dgl_ref[:, cs] = (dyv * (h_ref[0, :, cs] + h_ref[1, :, cs]) * dact).astype(BF16)

    return rows.call(body, name, [pl.BlockSpec((2, rows.tm, w), lambda i: (0, i, 0)), rows.row(w), rows.row(w)],
                     [rows.row(w), rows.row(w)],
                     [jax.ShapeDtypeStruct((rows.t_all, w), F32), jax.ShapeDtypeStruct((rows.t_all, w), BF16)])(h, gl, dy)


def z_merge(rows, pgc, yc, yl, name):
    d = rows.d

    def body(gc_ref, gr_ref, yc_ref, yl_ref, o_ref):
        for cs in _chunks(d):
            o_ref[:, cs] = (_sigmoid(gc_ref[:, cs].astype(F32)) * yc_ref[:, cs].astype(F32)
                            + _sigmoid(gr_ref[:, cs].astype(F32)) * yl_ref[:, cs].astype(F32)).astype(BF16)

    return rows.call(body, name, [rows.row(d, 0), rows.row(d, 1), rows.row(d), rows.row(d)], rows.row(d),
                     jax.ShapeDtypeStruct((rows.t_all, d), BF16))(pgc, pgc, yc, yl)


def z_merge_bwd(rows, pgc, yc, yl, dz, name):
    d = rows.d

    def body(gc_ref, gr_ref, yc_ref, yl_ref, dz_ref, dyc_ref, dyl_ref, dp_ref):
        for cs in _chunks(d):
            sc, sr = _sigmoid(gc_ref[:, cs].astype(F32)), _sigmoid(gr_ref[:, cs].astype(F32))
            dzv = dz_ref[:, cs].astype(F32)
            dyc_ref[:, cs] = (dzv * sc).astype(BF16)
            dyl_ref[:, cs] = (dzv * sr).astype(BF16)
            dp_ref[:, cs] = (dzv * yc_ref[:, cs].astype(F32) * sc * (1.0 - sc)).astype(BF16)
            dp_ref[:, slice(d + cs.start, d + cs.stop)] = (dzv * yl_ref[:, cs].astype(F32) * sr * (1.0 - sr)).astype(BF16)

    return rows.call(body, name, [rows.row(d, 0), rows.row(d, 1), rows.row(d), rows.row(d), rows.row(d)],
                     [rows.row(d), rows.row(d), rows.row(2 * d)],
                     [jax.ShapeDtypeStruct((rows.t_all, d), BF16), jax.ShapeDtypeStruct((rows.t_all, d), BF16),
                      jax.ShapeDtypeStruct((rows.t_all, 2 * d), BF16)])(pgc, pgc, yc, yl, dz)


def col_sum(rows, x, name):
    w = x.shape[1]

    def body(x_ref, o_ref):
        i = pl.program_id(0)
        for cs in _chunks(w):
            _accumulate(o_ref.at[:, cs], _colsum(x_ref[:, cs].astype(F32)), i == 0)

    return rows.call(body, name, [rows.row(w)], rows.vec(w), jax.ShapeDtypeStruct((1, w), F32))(x)


def loss_head(rows, x, f, mv, g, target, name):
    d = rows.d

    def body(x_ref, f_ref, mv_ref, g_ref, t_ref, loss_ref, dx_ref, dg_ref):
        i = pl.program_id(0)
        valid = jnp.where(i < rows.n_lat, 1.0, 0.0)
        xv = x_ref[...] + 0.5 * mv_ref[8:9, :] * f_ref[...]
        gv = g_ref[...]
        r = lax.rsqrt(jnp.mean(xv * xv, axis=-1, keepdims=True) + EPS)
        n = xv * r
        err = (n * gv - t_ref[...]) * valid
        part = 0.5 * jnp.sum(jnp.mean(err * err, axis=-1, keepdims=True), axis=0, keepdims=True)
        _accumulate(loss_ref, jnp.broadcast_to(part, (1, LANES)), i == 0)
        dy = err * (1.0 / d)
        dn = dy * gv
        dx_ref[...] = r * (dn - n * jnp.mean(dn * n, axis=-1, keepdims=True))
        _accumulate(dg_ref, _colsum(dy * n), i == 0)

    target_spec = pl.BlockSpec((rows.tm, d), lambda i: (jnp.minimum(i, rows.n_lat - 1), 0))
    return rows.call(body, name, [rows.row(d), rows.row(d), rows.mod(), rows.vec(d), target_spec],
                     [rows.vec(LANES), rows.row(d), rows.vec(d)],
                     [jax.ShapeDtypeStruct((1, LANES), F32), jax.ShapeDtypeStruct((rows.t_all, d), F32),
                      jax.ShapeDtypeStruct((1, d), F32)])(x, f, mv, g, target)


def _extended(rows, i, prev_ref, cur_ref, next_ref, cs):
    prev = jnp.where(rows.seg_first(i), 0.0, prev_ref[:, cs].astype(F32))
    nxt = jnp.where(rows.seg_last(i), 0.0, next_ref[:, cs].astype(F32))
    return jnp.concatenate([prev, cur_ref[:, cs].astype(F32), nxt], axis=0)


class _Shifts:
    def __init__(self, ext, tm):
        self.ext, self.tm, self.rolled = ext, tm, {0: ext}

    def at(self, offset):
        residue = offset % SUBLANES
        if residue not in self.rolled:
            self.rolled[residue] = pltpu.roll(self.ext, self.ext.shape[0] - residue, 0)
        start = HALO + offset - residue
        return self.rolled[residue][start:start + self.tm]


def _pad_taps(w):
    k = w.shape[0]
    return jnp.pad(w, ((0, -k % SUBLANES), (0, 0)))


def dwconv(rows, u, w, b, pad_left, out_dtype, name):
    taps, width = w.shape
    wp = _pad_taps(w)

    def body(*refs):
        prev_ref, cur_ref, next_ref, w_ref = refs[:4]
        b_ref = refs[4] if b is not None else None
        o_ref = refs[-1]
        i = pl.program_id(0)
        for cs in _chunks(width, LANES):
            shifts = _Shifts(_extended(rows, i, prev_ref, cur_ref, next_ref, cs), rows.tm)
            acc = jnp.zeros((rows.tm, LANES), F32) if b_ref is None else jnp.broadcast_to(b_ref[:, cs], (rows.tm, LANES))
            for k in range(taps):
                acc = acc + w_ref[k:k + 1, cs] * shifts.at(k - pad_left)
            o_ref[:, cs] = acc.astype(out_dtype)

    ins = [u, u, u, wp] + ([b] if b is not None else [])
    in_specs = [rows.halo(width, -1), rows.row(width), rows.halo(width, 1), rows.full(wp.shape)] + ([rows.vec(width)] if b is not None else [])
    return rows.call(body, name, in_specs, rows.row(width), jax.ShapeDtypeStruct((rows.t_all, width), out_dtype))(*ins)


def dwconv_bwd_w(rows, u, dy, taps, pad_left, name):
    width = u.shape[1]
    taps_p = taps + (-taps % SUBLANES)

    def body(prev_ref, cur_ref, next_ref, dy_ref, dw_ref, db_ref):
        i = pl.program_id(0)
        tap_row = lax.broadcasted_iota(jnp.int32, (taps_p, LANES), 0)
        for cs in _chunks(width, LANES):
            shifts = _Shifts(_extended(rows, i, prev_ref, cur_ref, next_ref, cs), rows.tm)
            dyv = dy_ref[:, cs]
            total = jnp.zeros((taps_p, LANES), F32)
            for k in range(taps):
                total = total + jnp.where(tap_row == k, _colsum(dyv * shifts.at(k - pad_left)), 0.0)
            _accumulate(dw_ref.at[:, cs], total, i == 0)
            _accumulate(db_ref.at[:, cs], _colsum(dyv), i == 0)

    dw, db = rows.call(body, name, [rows.halo(width, -1), rows.row(width), rows.halo(width, 1), rows.row(width)],
                       [rows.full((taps_p, width)), rows.vec(width)],
                       [jax.ShapeDtypeStruct((taps_p, width), F32), jax.ShapeDtypeStruct((1, width), F32)])(u, u, u, dy)
    return dw[:taps], db


def _gate_values(xr, pre, lam, d):
    r = _sigmoid(pre[:, (2 * d) * LANES:(2 * d + 1) * LANES])
    ig = _sigmoid(pre[:, (2 * d + 1) * LANES:(2 * d + 2) * LANES])
    sp = _softplus(-lam[:, d * LANES:(d + 1) * LANES])
    log_a = -LRU_C * r * sp
    return r, ig, sp, jnp.exp(log_a), _neg_expm1(2.0 * log_a)


def lru_gates(rows, xr, wcat, bcat, lam, name):
    n_blk = wcat.shape[0]
    width = xr.shape[1]

    def body(xr_ref, w_ref, b_ref, lam_ref, a_ref, bb_ref):
        for h in range(n_blk):
            cs = slice(h * LANES, (h + 1) * LANES)
            xv = xr_ref[:, cs]
            pre = jnp.dot(xv.astype(BF16), w_ref[h], preferred_element_type=F32) + b_ref[h]
            for d in range(2):
                _, ig, _, a, one_minus_a2 = _gate_values(xv, pre, lam_ref[h], d)
                a_ref[d, :, cs] = a
                bb_ref[d, :, cs] = jnp.sqrt(one_minus_a2) * (ig * xv)

    both = pl.BlockSpec((2, rows.tm, width), lambda i: (0, i, 0))
    shape = jax.ShapeDtypeStruct((2, rows.t_all, width), F32)
    return rows.call(body, name, [rows.row(width), rows.full(wcat.shape), rows.full(bcat.shape), rows.full(lam.shape)],
                     [both, both], [shape, shape])(xr, wcat, bcat, lam)


def lru_gates_bwd(rows, xr, wcat, bcat, lam, da, dbb, name):
    n_blk = wcat.shape[0]
    width = xr.shape[1]

    def body(xr_ref, w_ref, b_ref, lam_ref, da_ref, dbb_ref, dxr_ref, dw_ref, db_ref, dlam_ref):
        i = pl.program_id(0)
        for h in range(n_blk):
            cs = slice(h * LANES, (h + 1) * LANES)
            xv = xr_ref[:, cs]
            xb = xv.astype(BF16)
            wv = w_ref[h]
            pre = jnp.dot(xb, wv, preferred_element_type=F32) + b_ref[h]
            dxr = jnp.zeros_like(xv)
            dpre, dlam = [], []
            for d in range(2):
                r, ig, sp, a, one_minus_a2 = _gate_values(xv, pre, lam_ref[h], d)
                inv_q = lax.rsqrt(one_minus_a2)
                dav, dbv = da_ref[d, :, cs], dbb_ref[d, :, cs]
                dbq = dbv * (one_minus_a2 * inv_q) * ig
                dlog_a = dav * a - dbv * (ig * xv) * ((a * a) * inv_q)
                dpre.append(dlog_a * (-LRU_C * sp) * (r - r * r))
                dpre.append(dbq * xv * (1.0 - ig))
                dxr = dxr + dbq
                dlam.append(_colsum(dlog_a * (-LRU_C * r)) * (-_sigmoid(-lam_ref[h][:, d * LANES:(d + 1) * LANES])))
            dpre = jnp.concatenate(dpre, axis=1)
            dpb = dpre.astype(BF16)
            dxr_ref[:, cs] = dxr + lax.dot_general(dpb, wv, (((1,), (1,)), ((), ())), preferred_element_type=F32)
            _accumulate(dw_ref.at[h], lax.dot_general(xb, dpb, (((0,), (0,)), ((), ())), preferred_element_type=F32), i == 0)
            _accumulate(db_ref.at[h], _colsum(dpre), i == 0)
            _accumulate(dlam_ref.at[h], jnp.concatenate(dlam, axis=1), i == 0)

    both = pl.BlockSpec((2, rows.tm, width), lambda i: (0, i, 0))
    return rows.call(
        body, name, [rows.row(width), rows.full(wcat.shape), rows.full(bcat.shape), rows.full(lam.shape), both, both],
        [rows.row(width), rows.full(wcat.shape), rows.full(bcat.shape), rows.full(lam.shape)],
        [jax.ShapeDtypeStruct((rows.t_all, width), F32), jax.ShapeDtypeStruct(wcat.shape, F32),
         jax.ShapeDtypeStruct(bcat.shape, F32), jax.ShapeDtypeStruct(lam.shape, F32)])(xr, wcat, bcat, lam, da, dbb)


def _tile_scan(a, b, reverse):
    n = a.shape[0]
    row = lax.broadcasted_iota(jnp.int32, a.shape, 0)
    k = 1
    while k < n:
        ok = (row < n - k) if reverse else (row >= k)
        shift = n - k if reverse else k
        b = b + a * jnp.where(ok, pltpu.roll(b, shift, 0), 0.0)
        a = a * jnp.where(ok, pltpu.roll(a, shift, 0), 1.0)
        k *= 2
    return a, b


def _chain_scan(a, b, h_in, reverse):
    n_blocks = a.shape[0] // SUBLANES
    out = [None] * n_blocks
    state = h_in
    for j in (range(n_blocks - 1, -1, -1) if reverse else range(n_blocks)):
        rows_j = slice(j * SUBLANES, (j + 1) * SUBLANES)
        cum, h0 = _tile_scan(a[rows_j], b[rows_j], reverse)
        out[j] = h0 + cum * state
        state = out[j][0:1] if reverse else out[j][SUBLANES - 1:SUBLANES]
    return jnp.concatenate(out, axis=0), state


def _neighbour(v, edge, reverse):
    n = v.shape[0]
    row = lax.broadcasted_iota(jnp.int32, v.shape, 0)
    if reverse:
        return jnp.where(row < n - 1, pltpu.roll(v, n - 1, 0), edge)
    return jnp.where(row >= 1, pltpu.roll(v, 1, 0), edge)


def _scan_call(rows, body, name, ins, in_specs, n_out, width, adjoint):
    n_all, n_lat = rows.n_all, rows.n_lat

    def tile(d, s):
        s = n_all - 1 - s if adjoint else s
        return jnp.where(d == 0, (s + n_lat) % n_all, n_all - 1 - s)

    def per_dir(d, s):
        return (d, tile(d, s), 0)

    specs = [pl.BlockSpec((None, rows.tm, width), per_dir) if kind == "dir" else
             pl.BlockSpec((rows.tm, width), lambda d, s: (tile(d, s), 0)) for kind in in_specs]
    shape = jax.ShapeDtypeStruct((2, rows.t_all, width), F32)
    return pl.pallas_call(
        body, name=name, grid=(2, n_all), in_specs=specs, out_specs=[pl.BlockSpec((None, rows.tm, width), per_dir)] * n_out,
        out_shape=[shape] * n_out, scratch_shapes=[pltpu.VMEM((SUBLANES, width), F32)],
        compiler_params=_params("arbitrary", "arbitrary"))(*ins)


def lru_scan(rows, a, bb, name):
    width = a.shape[2]
    n = rows.tm

    def body(a_ref, bb_ref, h_ref, hp_ref, carry):
        d, s = pl.program_id(0), pl.program_id(1)

        @pl.when(s == 0)
        def _():
            carry[...] = jnp.zeros_like(carry)

        def run(reverse):
            for cs in _chunks(width, LANES):
                h_in = carry[0:1, cs]
                h, carry[0:1, cs] = _chain_scan(a_ref[:, cs], bb_ref[:, cs], h_in, reverse)
                h_ref[:, cs] = h
                hp_ref[:, cs] = _neighbour(h, h_in, reverse)

        pl.when(d == 0)(lambda: run(False))
        pl.when(d == 1)(lambda: run(True))

    return _scan_call(rows, body, name, [a, bb], ["dir", "dir"], 2, width, adjoint=False)


def lru_scan_bwd(rows, dh, a, hp, name):
    width = a.shape[2]
    n = rows.tm

    def body(dh_ref, a_ref, hp_ref, da_ref, dbb_ref, carry):
        d, s = pl.program_id(0), pl.program_id(1)

        @pl.when(s == 0)
        def _():
            carry[...] = jnp.zeros_like(carry)

        def run(reverse):
            for cs in _chunks(width, LANES):
                av = a_ref[:, cs]
                g, _ = _chain_scan(_neighbour(av, 1.0, reverse), dh_ref[:, cs], carry[0:1, cs], reverse)
                da_ref[:, cs] = g * hp_ref[:, cs]
                dbb_ref[:, cs] = g
                carry[0:1, cs] = (av * g)[0:1] if reverse else (av * g)[n - 1:n]

        pl.when(d == 0)(lambda: run(True))
        pl.when(d == 1)(lambda: run(False))

    return _scan_call(rows, body, name, [dh, a, hp], ["shared", "dir", "dir"], 2, width, adjoint=True)


def mod_matmul(c9, w_shard, b_shard, name):
    n = w_shard.shape[1]
    tn = _pick(n, 768)

    def body(c_ref, w_ref, b_ref, o_ref):
        act = _silu_and_grad(c_ref[...])[0]
        o_ref[...] = jnp.dot(act, w_ref[...], preferred_element_type=F32, precision=lax.Precision.HIGHEST) + b_ref[...]

    return pl.pallas_call(
        body, name=name, grid=(n // tn,),
        in_specs=[pl.BlockSpec(c9.shape, lambda j: (0, 0)), pl.BlockSpec((w_shard.shape[0], tn), lambda j: (0, j)),
                  pl.BlockSpec((1, tn), lambda j: (0, j))],
        out_specs=pl.BlockSpec((c9.shape[0], tn), lambda j: (0, j)), out_shape=jax.ShapeDtypeStruct((c9.shape[0], n), F32),
        compiler_params=_params("arbitrary"))(c9, w_shard, b_shard)


def mod_matmul_bwd(c9, d9, w_shard, name):
    n = w_shard.shape[1]
    tn = _pick(n, 768)
    steps = n // tn

    def body(c_ref, d_ref, w_ref, gw_ref, gc_ref):
        j = pl.program_id(0)
        act, dact = _silu_and_grad(c_ref[...])
        dv = d_ref[...]
        gw_ref[...] = lax.dot_general(act, dv, (((0,), (0,)), ((), ())), preferred_element_type=F32, precision=lax.Precision.HIGHEST)
        part = lax.dot_general(dv, w_ref[...], (((1,), (1,)), ((), ())), preferred_element_type=F32, precision=lax.Precision.HIGHEST)
        _accumulate(gc_ref, part * dact, j == 0)

    return pl.pallas_call(
        body, name=name, grid=(steps,),
        in_specs=[pl.BlockSpec(c9.shape, lambda j: (0, 0)), pl.BlockSpec((c9.shape[0], tn), lambda j: (0, j)),
                  pl.BlockSpec((w_shard.shape[0], tn), lambda j: (0, j))],
        out_specs=[pl.BlockSpec((w_shard.shape[0], tn), lambda j: (0, j)), pl.BlockSpec(c9.shape, lambda j: (0, 0))],
        out_shape=[jax.ShapeDtypeStruct(w_shard.shape, F32), jax.ShapeDtypeStruct(c9.shape, F32)],
        compiler_params=_params("arbitrary"))(c9, d9, w_shard)


def _row_tile(n_rows, n_cols):
    return _pick(n_rows, max(2 * SUBLANES, (256 * 1024) // n_cols), 2 * SUBLANES)


def sum_parts(parts, name):
    n, n_rows, n_cols = parts.shape
    tr = _row_tile(n_rows, n_cols)

    def body(p_ref, o_ref):
        total = p_ref[0].astype(F32)
        for k in range(1, n):
            total = total + p_ref[k].astype(F32)
        o_ref[...] = total

    return pl.pallas_call(
        body, name=name, grid=(n_rows // tr,), in_specs=[pl.BlockSpec((n, tr, n_cols), lambda i: (0, i, 0))],
        out_specs=pl.BlockSpec((tr, n_cols), lambda i: (i, 0)), out_shape=jax.ShapeDtypeStruct((n_rows, n_cols), F32),
        compiler_params=_params("arbitrary"))(parts)


def _adamw_update(g, w_ref, m_ref, v_ref, g_ref, d_ref, m2_ref, v2_ref):
    m2 = ADAM_B1 * m_ref[...] + (1.0 - ADAM_B1) * g
    v2 = ADAM_B2 * v_ref[...] + (1.0 - ADAM_B2) * (g * g)
    m_hat = m2 / (1.0 - ADAM_B1 ** ADAM_STEP)
    v_hat = v2 / (1.0 - ADAM_B2 ** ADAM_STEP)
    g_ref[...] = g
    d_ref[...] = -ADAM_LR * (m_hat / (jnp.sqrt(v_hat) + ADAM_EPS) + ADAM_WD * w_ref[...])
    m2_ref[...] = m2
    v2_ref[...] = v2


def adamw(parts, w, m, v, name):
    n, n_rows, n_cols = parts.shape
    tr = _row_tile(n_rows, n_cols)

    def body(p_ref, *refs):
        g = p_ref[0].astype(F32)
        for k in range(1, n):
            g = g + p_ref[k].astype(F32)
        _adamw_update(g, *refs)

    blk = pl.BlockSpec((tr, n_cols), lambda i: (i, 0))
    shape = jax.ShapeDtypeStruct((n_rows, n_cols), F32)
    return pl.pallas_call(
        body, name=name, grid=(n_rows // tr,), in_specs=[pl.BlockSpec((n, tr, n_cols), lambda i: (0, i, 0)), blk, blk, blk],
        out_specs=[blk] * 4, out_shape=[shape] * 4, compiler_params=_params("arbitrary"))(parts, w, m, v)


def adamw_pair(part, w, m, v, name):
    n_rows, n_cols = part.shape
    tr = _row_tile(n_rows, n_cols)
    steps = n_rows // tr

    def body(p_ref, w_ref, m_ref, v_ref, g_ref, d_ref, m2_ref, v2_ref, inbox, send_sems, recv_sems, credits):
        i = pl.program_id(0)
        slot = i % 2
        sibling = (lax.axis_index("x"), lax.axis_index("y"), 1 - lax.axis_index("c"))

        @pl.when(i >= 2)
        def _():
            pl.semaphore_wait(credits.at[slot], 1)

        copy = pltpu.make_async_remote_copy(src_ref=p_ref, dst_ref=inbox.at[slot], send_sem=send_sems.at[slot],
                                            recv_sem=recv_sems.at[slot], device_id=sibling, device_id_type=MESH)
        copy.start()
        copy.wait_recv()
        copy.wait_send()
        _adamw_update(p_ref[...] + inbox[slot], w_ref, m_ref, v_ref, g_ref, d_ref, m2_ref, v2_ref)

        @pl.when(i + 2 < steps)
        def _():
            pl.semaphore_signal(credits.at[slot], inc=1, device_id=sibling, device_id_type=MESH)

    blk = pl.BlockSpec((tr, n_cols), lambda i: (i, 0))
    shape = jax.ShapeDtypeStruct((n_rows, n_cols), F32)
    return pl.pallas_call(
        body, name=name, grid=(steps,), in_specs=[blk] * 4, out_specs=[blk] * 4, out_shape=[shape] * 4,
        scratch_shapes=[pltpu.VMEM((2, tr, n_cols), F32), pltpu.SemaphoreType.DMA((2,)), pltpu.SemaphoreType.DMA((2,)),
                        pltpu.SemaphoreType.REGULAR((2,))],
        compiler_params=_params("arbitrary"))(part, w, m, v)


def _pack(arrays, row_multiple=SUBLANES):
    flat = [jnp.pad(a.reshape(-1), (0, -a.size % LANES)) for a in arrays]
    buf = jnp.concatenate(flat)
    buf = jnp.pad(buf, (0, -buf.size % (row_multiple * LANES)))
    return buf.reshape(-1, LANES)


def _unpack(buf, shapes):
    lead = buf.shape[:-2]
    flat = buf.reshape(lead + (-1,))
    out, pos = [], 0
    for shape in shapes:
        size = 1
        for s in shape:
            size *= s
        out.append(flat[..., pos:pos + size].reshape(lead + tuple(shape)))
        pos += size + (-size % LANES)
    return out


def _pos_embedding(seq_len, dim):
    rows = seq_len // GRID_W
    q = dim // 4
    omega = 1.0 / (10000.0 ** (jnp.arange(q, dtype=F32) / q))
    er = jnp.arange(rows).astype(F32)[:, None] * omega
    ec = jnp.arange(GRID_W).astype(F32)[:, None] * omega
    by_row = jnp.broadcast_to(jnp.concatenate([jnp.sin(er), jnp.cos(er)], axis=-1)[:, None, :], (rows, GRID_W, 2 * q))
    by_col = jnp.broadcast_to(jnp.concatenate([jnp.sin(ec), jnp.cos(ec)], axis=-1)[None, :, :], (rows, GRID_W, 2 * q))
    return jnp.concatenate([by_row, by_col], axis=-1).reshape(rows * GRID_W, dim)


def _cols_from_shards(g):
    return jnp.transpose(g, (1, 0, 2)).reshape(g.shape[1], -1)


def _shards_from_cols(w, n_shards=4):
    k, n = w.shape
    return jnp.transpose(w.reshape(k, n_shards, n // n_shards), (1, 0, 2))


def kernel(x, c, ctx, c_ctx, w_mod, b_mod, g_n1, w_ffn1_up, w_ffn1_down, g_n2, w_in, b_in, w_dw, b_dw, g_ln, b_ln, w_conf_out, w_lru_conv, b_lru_conv, w_rec_gate, b_rec_gate, w_in_gate, b_in_gate, lru_lambda, w_lru_out, w_out, g_n3, w_ffn2_up, w_ffn2_down, g_final, loss_target, m_c_ctx, m_w_mod, m_b_mod, m_g_n1, m_w_ffn1_up, m_w_ffn1_down, m_g_n2, m_w_in, m_b_in, m_w_dw, m_b_dw, m_g_ln, m_b_ln, m_w_conf_out, m_w_lru_conv, m_b_lru_conv, m_w_rec_gate, m_b_rec_gate, m_w_in_gate, m_b_in_gate, m_lru_lambda, m_w_lru_out, m_w_out, m_g_n3, m_w_ffn2_up, m_w_ffn2_down, m_g_final, v_c_ctx, v_w_mod, v_b_mod, v_g_n1, v_w_ffn1_up, v_w_ffn1_down, v_g_n2, v_w_in, v_b_in, v_w_dw, v_b_dw, v_g_ln, v_b_ln, v_w_conf_out, v_w_lru_conv, v_b_lru_conv, v_w_rec_gate, v_b_rec_gate, v_w_in_gate, v_b_in_gate, v_lru_lambda, v_w_lru_out, v_w_out, v_g_n3, v_w_ffn2_up, v_w_ffn2_down, v_g_final):
    given = dict(locals())
    wt = {n: given[n] for n in W_NAMES}
    mom = {n: given["m_" + n] for n in W_NAMES}
    var = {n: given["v_" + n] for n in W_NAMES}

    def flat2(a):
        if a.ndim == 1:
            return a.reshape(1, -1)
        a = a[0]
        return a if a.ndim == 2 else a.reshape(-1, a.shape[-1])

    t_lat, d = x.shape[1], x.shape[2]
    t_ctx = ctx.shape[1]
    rows = Rows(t_lat, t_ctx, d)
    xi, yi, ci = lax.axis_index("x"), lax.axis_index("y"), lax.axis_index("c")
    me, chip = 4 * xi + 2 * yi + ci, 2 * xi + yi
    lru_w = b_lru_conv.shape[-1]
    n_blk = lru_w // LANES
    taps = w_dw.shape[1]
    lru_taps = w_lru_conv.shape[1]
    ffn = w_ffn1_down.shape[1] * 4

    small_shapes = [(1, d)] + [flat2(wt[n]).shape for n in SMALL_SHARDED]
    small_all = exchange("all", [_pack([c] + [flat2(wt[n]) for n in SMALL_SHARDED])], False, "gather_small")[0]
    small_all = _unpack(small_all, small_shapes)
    c_all = small_all[0][:, 0, :]
    full = {n: jnp.concatenate([a[0], a[2], a[4], a[6]], axis=-1) for n, a in zip(SMALL_SHARDED, small_all[1:])}
    big = COL_SHARDED + ROW_SHARDED
    ffn1_names = ['w_ffn1_up', 'w_ffn1_down']
    ffn2_names = ['w_ffn2_up', 'w_ffn2_down']
    mixer_names = [n for n in big if n not in ffn1_names + ffn2_names]
    wb = {}

    def take_weights(names, gathered):
        for n, g in zip(names, gathered):
            wb[n] = _cols_from_shards(g) if n in COL_SHARDED else g.reshape(-1, g.shape[-1])

    def shards_after(names, earlier):
        return lax.optimization_barrier(([wt[n][0].astype(BF16) for n in names], earlier))[0]

    c9 = jnp.concatenate([c_all, c_ctx.reshape(1, d), jnp.zeros((7, d), F32)], axis=0)
    mod_cols = mod_matmul(c9, w_mod[0], lax.dynamic_slice_in_dim(b_mod, chip * w_mod.shape[2], w_mod.shape[2], axis=1), "mod_matmul")
    mod_all = exchange("all", [mod_cols], False, "gather_mod")[0]
    mod9 = jnp.concatenate([mod_all[0], mod_all[2], mod_all[4], mod_all[6]], axis=-1)
    mv = jnp.stack([lax.dynamic_index_in_dim(mod9, me, 0, keepdims=False).reshape(N_MOD, d), mod9[8].reshape(N_MOD, d)])

    gathered_ffn1 = exchange("chips", shards_after(ffn1_names, mod_all), False, "gather_weights_ffn1")
    take_weights(ffn1_names, gathered_ffn1)
    gathered_mixer = exchange_by_sequencer(shards_after(mixer_names, gathered_ffn1), False, "gather_weights_mixer", 1)

    x_in = jnp.concatenate([x[0], ctx[0]], axis=0)
    pos = jnp.concatenate([_pos_embedding(t_lat, d), jnp.zeros((t_ctx, d), F32)], axis=0)

    def ffn_fwd(tag, h, w_up, w_down):
        g, u, act = matmul_swiglu(h, w_up, f"{tag}_up")
        return (g, u), act, matmul(act, w_down, "nn", F32, f"{tag}_down")

    x0, h1 = norm_mod(rows, x_in, g_n1, mv, 0, 1, "norm1", pos=pos)
    gu1, act1, f1 = ffn_fwd("ffn1", h1, wb["w_ffn1_up"], wb["w_ffn1_down"])
    gathered_mixer, f1 = lax.optimization_barrier((gathered_mixer, f1))
    take_weights(mixer_names, gathered_mixer)
    gathered_ffn2 = exchange_by_sequencer(shards_after(ffn2_names, gathered_mixer), False, "gather_weights_ffn2", 5)
    x1, h2 = norm_mod(rows, x0, g_n2, mv, 3, 4, "norm2", resid=(f1, 2, 0.5))

    conf_w = w_conf_out.shape[1] * 4
    col_groups = [(0, 2 * conf_w), (2 * conf_w, lru_w), (2 * conf_w + lru_w, lru_w), (2 * conf_w + 2 * lru_w, 2 * d)]
    pvg, ux, gl, pgc = [matmul(h2, wb["w_in"][:, s:s + n], "nn", BF16, f"in_proj{k}", bias=b_in[:, s:s + n])
                        for k, (s, n) in enumerate(col_groups)]
    u = glu(rows, pvg, "glu")
    u2 = dwconv(rows, u, full["w_dw"], b_dw, taps // 2, F32, "conf_conv")
    s_act = ln_silu(rows, u2, g_ln, b_ln, "conf_ln")
    yc = matmul(s_act, wb["w_conf_out"], "nn", BF16, "conf_out")

    xr = dwconv(rows, ux, full["w_lru_conv"], b_lru_conv, LRU_PAD_LEFT, F32, "lru_conv")
    w_rec, w_ing = w_rec_gate[0].astype(BF16), w_in_gate[0].astype(BF16)
    wcat = jnp.concatenate([w_rec[0], w_ing[0], w_rec[1], w_ing[1]], axis=-1)

    def per_block(a):
        return jnp.transpose(a.reshape(2, n_blk, LANES), (1, 0, 2)).reshape(n_blk, 1, 2 * LANES)

    b_rec, b_ing = full["b_rec_gate"].reshape(2, n_blk, LANES), full["b_in_gate"].reshape(2, n_blk, LANES)
    bcat = jnp.concatenate([b_rec[0], b_ing[0], b_rec[1], b_ing[1]], axis=-1).reshape(n_blk, 1, 4 * LANES)
    lam = per_block(full["lru_lambda"])
    a_gate, b_gate = lru_gates(rows, xr, wcat, bcat, lam, "lru_gates")
    h_scan, h_prev = lru_scan(rows, a_gate, b_gate, "lru_scan")
    yl_in = lru_merge(rows, h_scan, gl, "lru_merge")
    yl = matmul(yl_in, wb["w_lru_out"], "nn", BF16, "lru_out")
    z = z_merge(rows, pgc, yc, yl, "z_merge")
    y = matmul(z, wb["w_out"], "nn", F32, "mix_out")

    gathered_ffn2, y = lax.optimization_barrier((gathered_ffn2, y))
    take_weights(ffn2_names, gathered_ffn2)
    x2, h3 = norm_mod(rows, x1, g_n3, mv, 6, 7, "norm3", resid=(y, 5, 1.0))
    gu2, act2, f2 = ffn_fwd("ffn2", h3, wb["w_ffn2_up"], wb["w_ffn2_down"])
    loss_part, dx3, dg_final = loss_head(rows, x2, f2, mv, g_final.reshape(1, d), loss_target[0], "loss_head")

    grads = {"g_final": dg_final}
    dmv = [None] * N_MOD

    def ffn_bwd(tag, dxn, x_prev, h, gu, act, f, w_up, w_down, g, ks, collective_id, settle_first=()):
        k_shift, k_scale, k_gate = ks
        df, dmv[k_gate] = gate_bwd(rows, dxn, f, mv, k_gate, 0.5, f"{tag}_gate_bwd")
        grads[f"w_{tag}_down"] = matmul(act, df, "tn", F32, f"{tag}_down_dw")
        dact = matmul(df, w_down, "nt", BF16, f"{tag}_down_dx")
        if settle_first:
            dact = settle(settle_first, dact)
        dgu = swiglu_bwd(rows, gu[0], gu[1], dact, f"{tag}_act_bwd")
        grads[f"w_{tag}_up"] = matmul(h, dgu, "tn", F32, f"{tag}_up_dw")
        dgu = scatter_behind([f"w_{tag}_up", f"w_{tag}_down"], tag, collective_id, dgu)
        dh = matmul(dgu, w_up, "nt", F32, f"{tag}_up_dx")
        dx, dg, dmv[k_shift], dmv[k_scale] = norm_mod_bwd(rows, x_prev, dh, dxn, g, mv, k_shift, k_scale, f"{tag}_norm_bwd")
        return dx, dg

    def grad_pieces(names):
        return [(_shards_from_cols(grads[n]) if n in COL_SHARDED else grads[n].reshape(4, -1, grads[n].shape[-1])).astype(BF16)
                for n in names]

    from_chips = {}

    def scatter_behind(names, tag, collective_id, carry):
        ready, carry = lax.optimization_barrier(([grads[n] for n in names], carry))
        grads.update(zip(names, ready))
        from_chips.update(zip(names, exchange_by_sequencer(grad_pieces(names), True, f"scatter_grads_{tag}", collective_id)))
        return carry

    def settle(names, carry):
        landed, carry = lax.optimization_barrier(([from_chips[n] for n in names], carry))
        from_chips.update(zip(names, landed))
        return carry

    dx2, grads["g_n3"] = ffn_bwd("ffn2", dx3, x2, h3, gu2, act2, f2, wb["w_ffn2_up"], wb["w_ffn2_down"], g_n3, (6, 7, 8), 2)

    dy, dmv[5] = gate_bwd(rows, dx2, y, mv, 5, 1.0, "mix_gate_bwd")
    grads["w_out"] = matmul(z, dy, "tn", F32, "mix_out_dw")
    dz = matmul(dy, wb["w_out"], "nt", BF16, "mix_out_dx")
    dyc, dyl, dpgc = z_merge_bwd(rows, pgc, yc, yl, dz, "z_merge_bwd")

    grads["w_conf_out"] = matmul(s_act, dyc, "tn", F32, "conf_out_dw")
    ds_act = matmul(dyc, wb["w_conf_out"], "nt", BF16, "conf_out_dx")
    du2, grads["g_ln"], grads["b_ln"] = ln_silu_bwd(rows, u2, ds_act, g_ln, b_ln, "conf_ln_bwd")
    grads["w_dw"], grads["b_dw"] = dwconv_bwd_w(rows, u, du2, taps, taps // 2, "conf_conv_dw")
    du = dwconv(rows, du2, full["w_dw"][::-1], None, taps - 1 - taps // 2, F32, "conf_conv_dx")
    dpvg = glu_bwd(rows, pvg, du, "glu_bwd")

    grads["w_lru_out"] = matmul(yl_in, dyl, "tn", F32, "lru_out_dw")
    dyl = scatter_behind(['w_out', 'w_conf_out', 'w_lru_out'], "mixer", 3, settle(ffn2_names, dyl))
    dyl_in = matmul(dyl, wb["w_lru_out"], "nt", BF16, "lru_out_dx")
    dh_sum, dgl = lru_merge_bwd(rows, h_scan, gl, dyl_in, "lru_merge_bwd")
    da_gate, db_gate = lru_scan_bwd(rows, dh_sum, a_gate, h_prev, "lru_scan_bwd")
    dxr, dwcat, dbcat, dlam = lru_gates_bwd(rows, xr, wcat, bcat, lam, da_gate, db_gate, "lru_gates_bwd")
    grads["w_rec_gate"] = jnp.stack([dwcat[:, :, 0:LANES], dwcat[:, :, 2 * LANES:3 * LANES]])
    grads["w_in_gate"] = jnp.stack([dwcat[:, :, LANES:2 * LANES], dwcat[:, :, 3 * LANES:4 * LANES]])
    dbcat = dbcat.reshape(n_blk, 4, LANES)
    grads["b_rec_gate"] = jnp.stack([dbcat[:, 0], dbcat[:, 2]]).reshape(2, lru_w)
    grads["b_in_gate"] = jnp.stack([dbcat[:, 1], dbcat[:, 3]]).reshape(2, lru_w)
    grads["lru_lambda"] = jnp.transpose(dlam.reshape(n_blk, 2, LANES), (1, 0, 2)).reshape(2, lru_w)
    grads["w_lru_conv"], grads["b_lru_conv"] = dwconv_bwd_w(rows, ux, dxr, lru_taps, LRU_PAD_LEFT, "lru_conv_dw")
    dux = dwconv(rows, dxr, full["w_lru_conv"][::-1], None, lru_taps - 1 - LRU_PAD_LEFT, BF16, "lru_conv_dx")

    dproj = jnp.concatenate([dpvg, dux, dgl, dpgc], axis=1)
    grads["b_in"] = col_sum(rows, dproj, "in_proj_db")
    grads["w_in"] = matmul(h2, dproj, "tn", F32, "in_proj_dw")
    dproj = scatter_behind(['w_in'], "in_proj", 4, settle(['w_out', 'w_conf_out', 'w_lru_out'], dproj))
    dh2 = matmul(dproj, wb["w_in"], "nt", F32, "in_proj_dx")
    dx1, grads["g_n2"], dmv[3], dmv[4] = norm_mod_bwd(rows, x1, dh2, dx2, g_n2, mv, 3, 4, "norm2_bwd")

    dx0, grads["g_n1"] = ffn_bwd("ffn1", dx1, x0, h1, gu1, act1, f1, wb["w_ffn1_up"], wb["w_ffn1_down"], g_n1, (0, 1, 2), 6, settle_first=['w_in'])
    grad_x = dx0[:t_lat][None]

    chip_sums = [sum_parts(from_chips[n].reshape(4, -1, from_chips[n].shape[-1]), f"chip_sum_{n}") for n in big]
    out = {}
    for n, p in zip(big, chip_sums):
        out[n] = adamw_pair(p, flat2(wt[n]), flat2(mom[n]), flat2(var[n]), f"adamw_{n}")

    dmod = jnp.concatenate(dmv, axis=1)
    small_names = ['g_n1', 'g_n2', 'b_in', 'b_dw', 'g_ln', 'b_ln', 'b_lru_conv', 'w_rec_gate', 'w_in_gate', 'g_n3', 'g_final'] + SMALL_SHARDED
    small_list = [loss_part] + [grads[n] for n in small_names] + [dmod[0], dmod[1]]
    small_buf = _pack(small_list, 8 * 2 * SUBLANES)
    small_parts, dmod_all = exchange("all", [small_buf.reshape(8, -1, LANES), _pack([dmod[0]])], [True, False], "scatter_small_grads")
    small_eighth = sum_parts(small_parts, "sum_small_grads")
    small_sum = exchange("all", [small_eighth], False, "gather_small_grads")[0].reshape(small_buf.shape)
    small_sum = _unpack(small_sum, [a.shape for a in small_list])
    loss = small_sum[0][0, 0]
    total = dict(zip(small_names, small_sum[1:]))
    dmod_all = _unpack(dmod_all, [dmod[0].shape])[0].reshape(8, N_MOD * d)
    dmc = small_sum[-1].reshape(1, N_MOD * d)
    total["b_mod"] = small_sum[-2].reshape(1, N_MOD * d) + dmc

    d9 = jnp.concatenate([dmod_all, dmc, jnp.zeros((7, N_MOD * d), F32)], axis=0)
    d9_cols = lax.dynamic_slice_in_dim(d9, chip * w_mod.shape[2], w_mod.shape[2], axis=1)
    g_wmod, dc9 = mod_matmul_bwd(c9, d9_cols, w_mod[0], "mod_matmul_bwd")
    dc_all = exchange("all", [dc9[8:16]], False, "gather_dc")[0]
    total["c_ctx"] = sum_parts(jnp.stack([dc_all[0], dc_all[2], dc_all[4], dc_all[6]]), "sum_dc")[0:1]
    out["w_mod"] = adamw(g_wmod[None], w_mod[0], m_w_mod[0], v_w_mod[0], "adamw_w_mod")

    small_all_names = [n for n in W_NAMES if n not in out]
    g_local, shapes = [], []
    for n in small_all_names:
        g = total[n].reshape(flat2(wt[n]).shape[:-1] + (-1,)) if n in SMALL_SHARDED else total[n].reshape(flat2(wt[n]).shape)
        if n in SMALL_SHARDED:
            width = wt[n].shape[-1]
            g = lax.dynamic_slice_in_dim(g, chip * width, width, axis=1)
        g_local.append(g)
        shapes.append(g.shape)
    packed = adamw(_pack(g_local)[None], _pack([flat2(wt[n]) for n in small_all_names]),
                   _pack([flat2(mom[n]) for n in small_all_names]), _pack([flat2(var[n]) for n in small_all_names]), "adamw_small")
    for k, n in enumerate(small_all_names):
        out[n] = [_unpack(p, shapes)[k] for p in packed]

    results = [loss, grad_x]
    for k in range(4):
        results += [out[n][k].reshape(wt[n].shape) for n in W_NAMES]
    return tuple(results)
```

```python
import functools

import jax
import jax.numpy as jnp
from jax import lax
from jax.experimental import pallas as pl
from jax.experimental.pallas import tpu as pltpu
from jax.experimental.pallas import tpu_sc as plsc

F32 = jnp.float32
BF16 = jnp.bfloat16
MESH = pl.DeviceIdType.MESH

EPS = 1e-6
GRID_W = 64
N_MOD = 9
LRU_C = 8.0
LRU_PAD_LEFT = 2
ADAM_LR, ADAM_B1, ADAM_B2, ADAM_EPS, ADAM_WD, ADAM_STEP = 0.001, 0.9, 0.999, 1e-08, 0.01, 10

LANES = 128
SUBLANES = 8
HALO = 16
VMEM_LIMIT = 56 * 1024 * 1024

W_NAMES = ['c_ctx', 'w_mod', 'b_mod', 'g_n1', 'w_ffn1_up', 'w_ffn1_down', 'g_n2', 'w_in', 'b_in', 'w_dw', 'b_dw',
           'g_ln', 'b_ln', 'w_conf_out', 'w_lru_conv', 'b_lru_conv', 'w_rec_gate', 'b_rec_gate', 'w_in_gate',
           'b_in_gate', 'lru_lambda', 'w_lru_out', 'w_out', 'g_n3', 'w_ffn2_up', 'w_ffn2_down', 'g_final']
COL_SHARDED = ['w_ffn1_up', 'w_in', 'w_ffn2_up']
ROW_SHARDED = ['w_ffn1_down', 'w_conf_out', 'w_lru_out', 'w_out', 'w_ffn2_down']
SMALL_SHARDED = ['w_dw', 'w_lru_conv', 'b_rec_gate', 'b_in_gate', 'lru_lambda']


def _params(*semantics):
    return pltpu.CompilerParams(dimension_semantics=semantics, vmem_limit_bytes=VMEM_LIMIT)


def _pick(n, target, mult=LANES):
    best = None
    for t in range(mult, min(n, target) + 1, mult):
        if n % t == 0:
            best = t
    return best or n


def _chunks(width, target=512):
    w = _pick(width, target)
    return [slice(s, s + w) for s in range(0, width, w)]


def _sigmoid(x):
    return jax.nn.sigmoid(x)


def _silu_and_grad(x):
    s = _sigmoid(x)
    return x * s, s * (1.0 + x * (1.0 - s))


_GELU_K = 0.7978845608028654


def _gelu_and_grad(x):
    t = jnp.tanh(_GELU_K * (x + 0.044715 * (x * x * x)))
    return 0.5 * x * (1.0 + t), 0.5 * (1.0 + t) + 0.5 * x * (1.0 - t * t) * _GELU_K * (1.0 + 3 * 0.044715 * x * x)


def _neg_expm1(z):
    series = -(z * (1.0 + z * (0.5 + z * (1.0 / 6.0))))
    return jnp.where(z > -0.01, series, 1.0 - jnp.exp(z))


def _softplus(x):
    return jnp.maximum(x, 0.0) + jnp.log1p(jnp.exp(-jnp.abs(x)))


def _accumulate(ref, value, first):
    @pl.when(first)
    def _():
        ref[...] = value

    @pl.when(jnp.logical_not(first))
    def _():
        ref[...] += value


def _colsum(x):
    return jnp.sum(x, axis=0, keepdims=True)


class Rows:
    def __init__(self, t_lat, t_ctx, d_model):
        self.tm = _pick(t_ctx, 256, HALO)
        assert t_lat % self.tm == 0 and t_ctx % self.tm == 0
        self.n_lat = t_lat // self.tm
        self.n_all = (t_lat + t_ctx) // self.tm
        self.t_all = t_lat + t_ctx
        self.d = d_model

    def seg(self, i):
        return jnp.where(i >= self.n_lat, 1, 0)

    def seg_first(self, i):
        return jnp.logical_or(i == 0, i == self.n_lat)

    def seg_last(self, i):
        return jnp.logical_or(i == self.n_lat - 1, i == self.n_all - 1)

    def row(self, width, col=0):
        return pl.BlockSpec((self.tm, width), lambda i: (i, col))

    def vec(self, width):
        return pl.BlockSpec((1, width), lambda i: (0, 0))

    def full(self, shape):
        return pl.BlockSpec(shape, lambda i: (0,) * len(shape))

    def mod(self):
        return pl.BlockSpec((None, N_MOD, self.d), lambda i: (self.seg(i), 0, 0))

    def segsum(self):
        return pl.BlockSpec((None, 1, self.d), lambda i: (self.seg(i), 0, 0))

    def segsum_shape(self):
        return jax.ShapeDtypeStruct((2, 1, self.d), F32)

    def halo(self, width, side):
        per = self.tm // HALO
        last = self.t_all // HALO - 1
        if side < 0:
            return pl.BlockSpec((HALO, width), lambda i: (jnp.maximum(i * per - 1, 0), 0))
        return pl.BlockSpec((HALO, width), lambda i: (jnp.minimum((i + 1) * per, last), 0))

    def call(self, body, name, in_specs, out_specs, out_shape, scratch=()):
        return pl.pallas_call(body, name=name, grid=(self.n_all,), in_specs=in_specs, out_specs=out_specs,
                              out_shape=out_shape, scratch_shapes=list(scratch), compiler_params=_params("arbitrary"))


_PEER_FLIPS = {
    "chips": [(1, 0, 0), (0, 1, 0), (1, 1, 0)],
    "sibling": [(0, 0, 1)],
    "all": [(0, 0, 1), (0, 1, 0), (0, 1, 1), (1, 0, 0), (1, 0, 1), (1, 1, 0), (1, 1, 1)],
}


def _slot(kind, x, y, c):
    return {"chips": 2 * x + y, "sibling": c, "all": 4 * x + 2 * y + c}[kind]


def exchange(kind, srcs, indexed, name):
    flips = _PEER_FLIPS[kind]
    n_slots = len(flips) + 1
    n_t = len(srcs)
    by_slot = list(indexed) if isinstance(indexed, (list, tuple)) else [indexed] * n_t
    out_shapes = [jax.ShapeDtypeStruct(s.shape if ix else (n_slots,) + s.shape, s.dtype) for s, ix in zip(srcs, by_slot)]

    def body(*refs):
        src_refs, dst_refs = refs[:n_t], refs[n_t:2 * n_t]
        send_sems, recv_sems, local_sems = refs[2 * n_t:]
        x, y, c = lax.axis_index("x"), lax.axis_index("y"), lax.axis_index("c")
        me = _slot(kind, x, y, c)

        def piece(t, k):
            return src_refs[t].at[k] if by_slot[t] else src_refs[t]

        peers = [(1 - x if fx else x, 1 - y if fy else y, 1 - c if fc else c) for fx, fy, fc in flips]
        local = [pltpu.make_async_copy(piece(t, me), dst_refs[t].at[me], local_sems.at[t]) for t in range(n_t)]
        for cp in local:
            cp.start()
        sends = []
        for t in range(n_t):
            for p, peer in enumerate(peers):
                sends.append(pltpu.make_async_remote_copy(
                    src_ref=piece(t, _slot(kind, *peer)), dst_ref=dst_refs[t].at[me],
                    send_sem=send_sems.at[t, p], recv_sem=recv_sems.at[t, p], device_id=peer, device_id_type=MESH))
        for cp in sends:
            cp.start()
        for t in range(n_t):
            for p, peer in enumerate(peers):
                theirs = _slot(kind, *peer)
                pltpu.make_async_remote_copy(
                    src_ref=piece(t, me), dst_ref=dst_refs[t].at[theirs],
                    send_sem=send_sems.at[t, p], recv_sem=recv_sems.at[t, p], device_id=peer, device_id_type=MESH).wait_recv()
        for cp in sends:
            cp.wait_send()
        for cp in local:
            cp.wait()

    any_spec = pl.BlockSpec(memory_space=pl.ANY)
    outs = pl.pallas_call(
        body, name=name, out_shape=out_shapes, in_specs=[any_spec] * n_t, out_specs=[any_spec] * n_t,
        scratch_shapes=[pltpu.SemaphoreType.DMA((n_t, len(flips))), pltpu.SemaphoreType.DMA((n_t, len(flips))),
                        pltpu.SemaphoreType.DMA((n_t,))],
        compiler_params=pltpu.CompilerParams(has_side_effects=True),
    )(*srcs)
    return list(outs)


def exchange_by_sequencer(srcs, indexed, name, collective_id):
    flips = _PEER_FLIPS["chips"]
    n_t, n_p = len(srcs), len(flips)
    src_refs = [jax.new_ref(s, memory_space=pltpu.MemorySpace.HBM) for s in srcs]
    land_refs = [jax.empty_ref(jax.ShapeDtypeStruct(s.shape if indexed else (n_p + 1,) + s.shape, s.dtype),
                               memory_space=pltpu.MemorySpace.HBM) for s in srcs]

    @pl.kernel(mesh=plsc.ScalarSubcoreMesh(axis_name="sequencer", num_cores=1), name=name,
               scratch_types=(pltpu.SemaphoreType.DMA((n_t, n_p)), pltpu.SemaphoreType.DMA((n_t, n_p)), pltpu.SemaphoreType.DMA((n_t,))),
               compiler_params=pltpu.CompilerParams(collective_id=collective_id))
    def launch(send_sems, recv_sems, local_sems):
        x, y, c = lax.axis_index("x"), lax.axis_index("y"), lax.axis_index("c")
        peers = [(1 - x if fx else x, 1 - y if fy else y, c) for fx, fy, _ in flips]
        barrier = pltpu.get_barrier_semaphore()
        for peer in peers:
            pl.semaphore_signal(barrier, inc=1, device_id=peer, device_id_type=MESH)
        pl.semaphore_wait(barrier, n_p)
        me = _slot("chips", x, y, c)

        def piece(t, k):
            return src_refs[t].at[k] if indexed else src_refs[t]

        local = [pltpu.make_async_copy(piece(t, me), land_refs[t].at[me], local_sems.at[t]) for t in range(n_t)]
        for cp in local:
            cp.start()
        sends = []
        for t in range(n_t):
            for p, peer in enumerate(peers):
                sends.append(pltpu.make_async_remote_copy(
                    src_ref=piece(t, _slot("chips", *peer)), dst_ref=land_refs[t].at[me],
                    send_sem=send_sems.at[t, p], recv_sem=recv_sems.at[t, p], device_id=peer, device_id_type=MESH))
        for cp in sends:
            cp.start()
        for t in range(n_t):
            for p, peer in enumerate(peers):
                pltpu.make_async_remote_copy(
                    src_ref=piece(t, me), dst_ref=land_refs[t].at[_slot("chips", *peer)],
                    send_sem=send_sems.at[t, p], recv_sem=recv_sems.at[t, p], device_id=peer, device_id_type=MESH).wait_recv()
        for cp in sends:
            cp.wait_send()
        for cp in local:
            cp.wait()

    launch()
    return [r[...] for r in land_refs]


MATMUL_VMEM_BUDGET = 44 * 1024 * 1024
MATMUL_STEP_BYTES = 1024 * 1024


def _divisors(n, most):
    return [t for t in range(LANES, min(n, most) + 1, LANES) if n % t == 0] or [n]


def _matmul_tiles(n_i, n_j, n_r, a_bytes, b_bytes, out_bytes):
    best = None
    for tr in _divisors(n_r, n_r):
        steps = n_r // tr
        for ti in _divisors(n_i, 1408):
            for tj in _divisors(n_j, 1664):
                vmem = 2 * (ti * tr * a_bytes + tr * tj * b_bytes + ti * tj * out_bytes) + (ti * tj * 4 if steps > 1 else 0)
                if vmem > MATMUL_VMEM_BUDGET or ti < min(n_i, 2 * LANES) or tj < min(n_j, 4 * LANES):
                    continue
                size_a, size_b = n_i * n_r * a_bytes, n_r * n_j * b_bytes
                if steps == 1:
                    moved = min(size_a + (n_i // ti) * size_b, size_b + (n_j // tj) * size_a)
                else:
                    moved = (n_j // tj) * size_a + (n_i // ti) * size_b + (steps - 1) * n_i * n_j * 8
                moved += n_i * n_j * out_bytes + (n_i // ti) * (n_j // tj) * steps * MATMUL_STEP_BYTES
                if best is None or moved < best[0]:
                    best = (moved, ti, tj, tr)
    return best[1:]


def matmul(a, b, mode, out_dtype, name, bias=None):
    if mode == "nn":
        (n_i, n_r), (_, n_j) = a.shape, b.shape
    elif mode == "nt":
        (n_i, n_r), (n_j, _) = a.shape, b.shape
    else:
        (n_r, n_i), (_, n_j) = a.shape, b.shape
    ti, tj, tr = _matmul_tiles(n_i, n_j, n_r, a.dtype.itemsize, b.dtype.itemsize, jnp.dtype(out_dtype).itemsize)
    steps = n_r // tr
    bytes_a, bytes_b = a.size * a.dtype.itemsize, b.size * b.dtype.itemsize
    j_outer = steps == 1 and bytes_b + (n_j // tj) * bytes_a < bytes_a + (n_i // ti) * bytes_b

    def at(fn):
        return (lambda j, i, r: fn(i, j, r)) if j_outer else fn

    a_spec = {"nn": pl.BlockSpec((ti, tr), at(lambda i, j, r: (i, r))), "nt": pl.BlockSpec((ti, tr), at(lambda i, j, r: (i, r))),
              "tn": pl.BlockSpec((tr, ti), at(lambda i, j, r: (r, i)))}[mode]
    b_spec = {"nn": pl.BlockSpec((tr, tj), at(lambda i, j, r: (r, j))), "nt": pl.BlockSpec((tj, tr), at(lambda i, j, r: (j, r))),
              "tn": pl.BlockSpec((tr, tj), at(lambda i, j, r: (r, j)))}[mode]
    dims = {"nn": (((1,), (0,)), ((), ())), "nt": (((1,), (1,)), ((), ())), "tn": (((0,), (0,)), ((), ()))}[mode]

    def body(*refs):
        a_ref, b_ref = refs[:2]
        bias_ref = refs[2] if bias is not None else None
        o_ref = refs[3] if bias is not None else refs[2]
        prod = lax.dot_general(a_ref[...].astype(BF16), b_ref[...].astype(BF16), dims, preferred_element_type=F32)

        def finish(total):
            if bias_ref is not None:
                total = total + bias_ref[...]
            o_ref[...] = total.astype(out_dtype)

        if steps == 1:
            finish(prod)
        else:
            acc = refs[-1]
            r = pl.program_id(2)

            @pl.when(r == 0)
            def _():
                acc[...] = prod

            @pl.when(r > 0)
            def _():
                acc[...] += prod

            @pl.when(r == steps - 1)
            def _():
                finish(acc[...])

    in_specs = [a_spec, b_spec] + ([pl.BlockSpec((1, tj), at(lambda i, j, r: (0, j)))] if bias is not None else [])
    args = (a, b) + ((bias,) if bias is not None else ())
    grid = (n_j // tj, n_i // ti, steps) if j_outer else (n_i // ti, n_j // tj, steps)
    return pl.pallas_call(
        body, name=name, grid=grid, in_specs=in_specs,
        out_specs=pl.BlockSpec((ti, tj), at(lambda i, j, r: (i, j))), out_shape=jax.ShapeDtypeStruct((n_i, n_j), out_dtype),
        scratch_shapes=[pltpu.VMEM((ti, tj), F32)] if steps > 1 else [],
        compiler_params=_params("arbitrary", "arbitrary", "arbitrary"),
    )(*args)


def matmul_swiglu(h, w_up, name):
    n_t, d = h.shape
    f = w_up.shape[1] // 2
    ti, tj = _pick(n_t, 384), _pick(f, 1408)
    n_j = f // tj

    def body(h_ref, wg_ref, wu_ref, g_ref, u_ref, act_ref):
        hv = h_ref[...]
        g = jnp.dot(hv, wg_ref[...], preferred_element_type=F32)
        u = jnp.dot(hv, wu_ref[...], preferred_element_type=F32)
        g_ref[...] = g.astype(BF16)
        u_ref[...] = u.astype(BF16)
        act_ref[...] = (_silu_and_grad(g)[0] * u).astype(BF16)

    out = pl.BlockSpec((ti, tj), lambda j, i: (i, j))
    shape = jax.ShapeDtypeStruct((n_t, f), BF16)
    return pl.pallas_call(
        body, name=name, grid=(n_j, n_t // ti),
        in_specs=[pl.BlockSpec((ti, d), lambda j, i: (i, 0)), pl.BlockSpec((d, tj), lambda j, i: (0, j)),
                  pl.BlockSpec((d, tj), lambda j, i: (0, j + n_j))],
        out_specs=[out, out, out], out_shape=[shape, shape, shape], compiler_params=_params("arbitrary", "arbitrary"),
    )(h, w_up, w_up)


def norm_mod(rows, x, g, mv, shift_k, scale_k, name, resid=None, pos=None):
    d = rows.d
    changed = resid is not None or pos is not None

    def body(*refs):
        refs = list(refs)
        x_ref, g_ref, mv_ref = refs[:3]
        rest = refs[3:]
        xv = x_ref[...]
        if pos is not None:
            xv = xv + rest.pop(0)[...]
        if resid is not None:
            xv = xv + resid[2] * mv_ref[resid[1]:resid[1] + 1, :] * rest.pop(0)[...]
        if changed:
            rest.pop(0)[...] = xv
        r = lax.rsqrt(jnp.mean(xv * xv, axis=-1, keepdims=True) + EPS)
        h = (xv * r) * g_ref[...]
        if shift_k is not None:
            h = h * (1.0 + mv_ref[scale_k:scale_k + 1, :]) + mv_ref[shift_k:shift_k + 1, :]
        rest.pop(0)[...] = h.astype(BF16)

    ins = [x, g, mv] + ([pos] if pos is not None else []) + ([resid[0]] if resid is not None else [])
    in_specs = [rows.row(d), rows.vec(d), rows.mod()] + [rows.row(d)] * (len(ins) - 3)
    out_shape = ([jax.ShapeDtypeStruct((rows.t_all, d), F32)] if changed else []) + [jax.ShapeDtypeStruct((rows.t_all, d), BF16)]
    outs = rows.call(body, name, in_specs, [rows.row(d)] * len(out_shape), out_shape)(*ins)
    return (outs[0], outs[1]) if changed else (None, outs[0])


def norm_mod_bwd(rows, x, dh, dxn, g, mv, shift_k, scale_k, name):
    d = rows.d

    def body(x_ref, dh_ref, dxn_ref, g_ref, mv_ref, dx_ref, dg_ref, dsh_ref, dsc_ref):
        i = pl.program_id(0)
        xv, dhv, gv = x_ref[...], dh_ref[...], g_ref[...]
        r = lax.rsqrt(jnp.mean(xv * xv, axis=-1, keepdims=True) + EPS)
        n = xv * r
        dy = dhv * (1.0 + mv_ref[scale_k:scale_k + 1, :])
        dn = dy * gv
        dx_ref[...] = dxn_ref[...] + r * (dn - n * jnp.mean(dn * n, axis=-1, keepdims=True))
        _accumulate(dg_ref, _colsum(dy * n), i == 0)
        _accumulate(dsh_ref, _colsum(dhv), rows.seg_first(i))
        _accumulate(dsc_ref, _colsum(dhv * (n * gv)), rows.seg_first(i))

    return rows.call(
        body, name, [rows.row(d), rows.row(d), rows.row(d), rows.vec(d), rows.mod()],
        [rows.row(d), rows.vec(d), rows.segsum(), rows.segsum()],
        [jax.ShapeDtypeStruct((rows.t_all, d), F32), jax.ShapeDtypeStruct((1, d), F32), rows.segsum_shape(), rows.segsum_shape()],
    )(x, dh, dxn, g, mv)


def gate_bwd(rows, dxn, f, mv, gate_k, scale, name):
    d = rows.d

    def body(dxn_ref, f_ref, mv_ref, df_ref, dgate_ref):
        i = pl.program_id(0)
        dv = dxn_ref[...]
        df_ref[...] = (scale * mv_ref[gate_k:gate_k + 1, :] * dv).astype(BF16)
        _accumulate(dgate_ref, scale * _colsum(dv * f_ref[...]), rows.seg_first(i))

    return rows.call(body, name, [rows.row(d), rows.row(d), rows.mod()], [rows.row(d), rows.segsum()],
                     [jax.ShapeDtypeStruct((rows.t_all, d), BF16), rows.segsum_shape()])(dxn, f, mv)


def swiglu_bwd(rows, g, u, dact, name):
    f = g.shape[1]

    def body(g_ref, u_ref, da_ref, o_ref):
        for cs in _chunks(f):
            act, dact_dg = _silu_and_grad(g_ref[:, cs].astype(F32))
            dav = da_ref[:, cs].astype(F32)
            o_ref[:, cs] = (dav * u_ref[:, cs].astype(F32) * dact_dg).astype(BF16)
            o_ref[:, slice(f + cs.start, f + cs.stop)] = (dav * act).astype(BF16)

    return rows.call(body, name, [rows.row(f), rows.row(f), rows.row(f)], rows.row(2 * f),
                     jax.ShapeDtypeStruct((rows.t_all, 2 * f), BF16))(g, u, dact)


def glu(rows, pvg, name):
    w = pvg.shape[1] // 2

    def body(v_ref, t_ref, o_ref):
        for cs in _chunks(w):
            o_ref[:, cs] = v_ref[:, cs].astype(F32) * _sigmoid(t_ref[:, cs].astype(F32))

    return rows.call(body, name, [rows.row(w, 0), rows.row(w, 1)], rows.row(w), jax.ShapeDtypeStruct((rows.t_all, w), F32))(pvg, pvg)


def glu_bwd(rows, pvg, du, name):
    w = pvg.shape[1] // 2

    def body(v_ref, t_ref, du_ref, o_ref):
        for cs in _chunks(w):
            s = _sigmoid(t_ref[:, cs].astype(F32))
            duv = du_ref[:, cs]
            o_ref[:, cs] = (duv * s).astype(BF16)
            o_ref[:, slice(w + cs.start, w + cs.stop)] = (duv * v_ref[:, cs].astype(F32) * s * (1.0 - s)).astype(BF16)

    return rows.call(body, name, [rows.row(w, 0), rows.row(w, 1), rows.row(w)], rows.row(2 * w),
                     jax.ShapeDtypeStruct((rows.t_all, 2 * w), BF16))(pvg, pvg, du)


def ln_silu(rows, u, g, b, name):
    w = u.shape[1]

    def body(u_ref, g_ref, b_ref, o_ref):
        uv = u_ref[...]
        xc = uv - jnp.mean(uv, axis=-1, keepdims=True)
        n = xc * lax.rsqrt(jnp.mean(xc * xc, axis=-1, keepdims=True) + EPS)
        o_ref[...] = _silu_and_grad(n * g_ref[...] + b_ref[...])[0].astype(BF16)

    return rows.call(body, name, [rows.row(w), rows.vec(w), rows.vec(w)], rows.row(w),
                     jax.ShapeDtypeStruct((rows.t_all, w), BF16))(u, g, b)


def ln_silu_bwd(rows, u, ds, g, b, name):
    w = u.shape[1]

    def body(u_ref, ds_ref, g_ref, b_ref, du_ref, dg_ref, db_ref):
        i = pl.program_id(0)
        uv, gv = u_ref[...], g_ref[...]
        xc = uv - jnp.mean(uv, axis=-1, keepdims=True)
        r = lax.rsqrt(jnp.mean(xc * xc, axis=-1, keepdims=True) + EPS)
        n = xc * r
        dy = ds_ref[...].astype(F32) * _silu_and_grad(n * gv + b_ref[...])[1]
        dn = dy * gv
        du_ref[...] = r * (dn - jnp.mean(dn, axis=-1, keepdims=True) - n * jnp.mean(dn * n, axis=-1, keepdims=True))
        _accumulate(dg_ref, _colsum(dy * n), i == 0)
        _accumulate(db_ref, _colsum(dy), i == 0)

    return rows.call(body, name, [rows.row(w), rows.row(w), rows.vec(w), rows.vec(w)], [rows.row(w), rows.vec(w), rows.vec(w)],
                     [jax.ShapeDtypeStruct((rows.t_all, w), F32), jax.ShapeDtypeStruct((1, w), F32),
                      jax.ShapeDtypeStruct((1, w), F32)])(u, ds, g, b)


def lru_merge(rows, h, gl, name):
    w = gl.shape[1]

    def body(h_ref, gl_ref, o_ref):
        for cs in _chunks(w):
            o_ref[:, cs] = ((h_ref[0, :, cs] + h_ref[1, :, cs]) * _gelu_and_grad(gl_ref[:, cs].astype(F32))[0]).astype(BF16)

    return rows.call(body, name, [pl.BlockSpec((2, rows.tm, w), lambda i: (0, i, 0)), rows.row(w)], rows.row(w),
                     jax.ShapeDtypeStruct((rows.t_all, w), BF16))(h, gl)


def lru_merge_bwd(rows, h, gl, dy, name):
    w = gl.shape[1]

    def body(h_ref, gl_ref, dy_ref, dh_ref, dgl_ref):
        for cs in _chunks(w):
            act, dact = _gelu_and_grad(gl_ref[:, cs].astype(F32))
            dyv = dy_ref[:, cs].astype(F32)
            dh_ref[:, cs] = dyv * act
            dgl_ref[:, cs] = (dyv * (h_ref[0, :, cs] + h_ref[1, :, cs]) * dact).astype(BF16)

    return rows.call(body, name, [pl.BlockSpec((2, rows.tm, w), lambda i: (0, i, 0)), rows.row(w), rows.row(w)],
                     [rows.row(w), rows.row(w)],
                     [jax.ShapeDtypeStruct((rows.t_all, w), F32), jax.ShapeDtypeStruct((rows.t_all, w), BF16)])(h, gl, dy)


def z_merge(rows, pgc, yc, yl, name):
    d = rows.d

    def body(gc_ref, gr_ref, yc_ref, yl_ref, o_ref):
        for cs in _chunks(d):
            o_ref[:, cs] = (_sigmoid(gc_ref[:, cs].astype(F32)) * yc_ref[:, cs].astype(F32)
                            + _sigmoid(gr_ref[:, cs].astype(F32)) * yl_ref[:, cs].astype(F32)).astype(BF16)

    return rows.call(body, name, [rows.row(d, 0), rows.row(d, 1), rows.row(d), rows.row(d)], rows.row(d),
                     jax.ShapeDtypeStruct((rows.t_all, d), BF16))(pgc, pgc, yc, yl)


def z_merge_bwd(rows, pgc, yc, yl, dz, name):
    d = rows.d

    def body(gc_ref, gr_ref, yc_ref, yl_ref, dz_ref, dyc_ref, dyl_ref, dp_ref):
        for cs in _chunks(d):
            sc, sr = _sigmoid(gc_ref[:, cs].astype(F32)), _sigmoid(gr_ref[:, cs].astype(F32))
            dzv = dz_ref[:, cs].astype(F32)
            dyc_ref[:, cs] = (dzv * sc).astype(BF16)
            dyl_ref[:, cs] = (dzv * sr).astype(BF16)
            dp_ref[:, cs] = (dzv * yc_ref[:, cs].astype(F32) * sc * (1.0 - sc)).astype(BF16)
            dp_ref[:, slice(d + cs.start, d + cs.stop)] = (dzv * yl_ref[:, cs].astype(F32) * sr * (1.0 - sr)).astype(BF16)

    return rows.call(body, name, [rows.row(d, 0), rows.row(d, 1), rows.row(d), rows.row(d), rows.row(d)],
                     [rows.row(d), rows.row(d), rows.row(2 * d)],
                     [jax.ShapeDtypeStruct((rows.t_all, d), BF16), jax.ShapeDtypeStruct((rows.t_all, d), BF16),
                      jax.ShapeDtypeStruct((rows.t_all, 2 * d), BF16)])(pgc, pgc, yc, yl, dz)


def col_sum(rows, x, name):
    w = x.shape[1]

    def body(x_ref, o_ref):
        i = pl.program_id(0)
        for cs in _chunks(w):
            _accumulate(o_ref.at[:, cs], _colsum(x_ref[:, cs].astype(F32)), i == 0)

    return rows.call(body, name, [rows.row(w)], rows.vec(w), jax.ShapeDtypeStruct((1, w), F32))(x)


def loss_head(rows, x, f, mv, g, target, name):
    d = rows.d

    def body(x_ref, f_ref, mv_ref, g_ref, t_ref, loss_ref, dx_ref, dg_ref):
        i = pl.program_id(0)
        valid = jnp.where(i < rows.n_lat, 1.0, 0.0)
        xv = x_ref[...] + 0.5 * mv_ref[8:9, :] * f_ref[...]
        gv = g_ref[...]
        r = lax.rsqrt(jnp.mean(xv * xv, axis=-1, keepdims=True) + EPS)
        n = xv * r
        err = (n * gv - t_ref[...]) * valid
        part = 0.5 * jnp.sum(jnp.mean(err * err, axis=-1, keepdims=True), axis=0, keepdims=True)
        _accumulate(loss_ref, jnp.broadcast_to(part, (1, LANES)), i == 0)
        dy = err * (1.0 / d)
        dn = dy * gv
        dx_ref[...] = r * (dn - n * jnp.mean(dn * n, axis=-1, keepdims=True))
        _accumulate(dg_ref, _colsum(dy * n), i == 0)

    target_spec = pl.BlockSpec((rows.tm, d), lambda i: (jnp.minimum(i, rows.n_lat - 1), 0))
    return rows.call(body, name, [rows.row(d), rows.row(d), rows.mod(), rows.vec(d), target_spec],
                     [rows.vec(LANES), rows.row(d), rows.vec(d)],
                     [jax.ShapeDtypeStruct((1, LANES), F32), jax.ShapeDtypeStruct((rows.t_all, d), F32),
                      jax.ShapeDtypeStruct((1, d), F32)])(x, f, mv, g, target)


def _extended(rows, i, prev_ref, cur_ref, next_ref, cs):
    prev = jnp.where(rows.seg_first(i), 0.0, prev_ref[:, cs].astype(F32))
    nxt = jnp.where(rows.seg_last(i), 0.0, next_ref[:, cs].astype(F32))
    return jnp.concatenate([prev, cur_ref[:, cs].astype(F32), nxt], axis=0)


class _Shifts:
    def __init__(self, ext, tm):
        self.ext, self.tm, self.rolled = ext, tm, {0: ext}

    def at(self, offset):
        residue = offset % SUBLANES
        if residue not in self.rolled:
            self.rolled[residue] = pltpu.roll(self.ext, self.ext.shape[0] - residue, 0)
        start = HALO + offset - residue
        return self.rolled[residue][start:start + self.tm]


def _pad_taps(w):
    k = w.shape[0]
    return jnp.pad(w, ((0, -k % SUBLANES), (0, 0)))


def dwconv(rows, u, w, b, pad_left, out_dtype, name):
    taps, width = w.shape
    wp = _pad_taps(w)

    def body(*refs):
        prev_ref, cur_ref, next_ref, w_ref = refs[:4]
        b_ref = refs[4] if b is not None else None
        o_ref = refs[-1]
        i = pl.program_id(0)
        for cs in _chunks(width, LANES):
            shifts = _Shifts(_extended(rows, i, prev_ref, cur_ref, next_ref, cs), rows.tm)
            acc = jnp.zeros((rows.tm, LANES), F32) if b_ref is None else jnp.broadcast_to(b_ref[:, cs], (rows.tm, LANES))
            for k in range(taps):
                acc = acc + w_ref[k:k + 1, cs] * shifts.at(k - pad_left)
            o_ref[:, cs] = acc.astype(out_dtype)

    ins = [u, u, u, wp] + ([b] if b is not None else [])
    in_specs = [rows.halo(width, -1), rows.row(width), rows.halo(width, 1), rows.full(wp.shape)] + ([rows.vec(width)] if b is not None else [])
    return rows.call(body, name, in_specs, rows.row(width), jax.ShapeDtypeStruct((rows.t_all, width), out_dtype))(*ins)


def dwconv_bwd_w(rows, u, dy, taps, pad_left, name):
    width = u.shape[1]
    taps_p = taps + (-taps % SUBLANES)

    def body(prev_ref, cur_ref, next_ref, dy_ref, dw_ref, db_ref):
        i = pl.program_id(0)
        tap_row = lax.broadcasted_iota(jnp.int32, (taps_p, LANES), 0)
        for cs in _chunks(width, LANES):
            shifts = _Shifts(_extended(rows, i, prev_ref, cur_ref, next_ref, cs), rows.tm)
            dyv = dy_ref[:, cs]
            total = jnp.zeros((taps_p, LANES), F32)
            for k in range(taps):
                total = total + jnp.where(tap_row == k, _colsum(dyv * shifts.at(k - pad_left)), 0.0)
            _accumulate(dw_ref.at[:, cs], total, i == 0)
            _accumulate(db_ref.at[:, cs], _colsum(dyv), i == 0)

    dw, db = rows.call(body, name, [rows.halo(width, -1), rows.row(width), rows.halo(width, 1), rows.row(width)],
                       [rows.full((taps_p, width)), rows.vec(width)],
                       [jax.ShapeDtypeStruct((taps_p, width), F32), jax.ShapeDtypeStruct((1, width), F32)])(u, u, u, dy)
    return dw[:taps], db


def _gate_values(xr, pre, lam, d):
    r = _sigmoid(pre[:, (2 * d) * LANES:(2 * d + 1) * LANES])
    ig = _sigmoid(pre[:, (2 * d + 1) * LANES:(2 * d + 2) * LANES])
    sp = _softplus(-lam[:, d * LANES:(d + 1) * LANES])
    log_a = -LRU_C * r * sp
    return r, ig, sp, jnp.exp(log_a), _neg_expm1(2.0 * log_a)


def lru_gates(rows, xr, wcat, bcat, lam, name):
    n_blk = wcat.shape[0]
    width = xr.shape[1]

    def body(xr_ref, w_ref, b_ref, lam_ref, a_ref, bb_ref):
        for h in range(n_blk):
            cs = slice(h * LANES, (h + 1) * LANES)
            xv = xr_ref[:, cs]
            pre = jnp.dot(xv.astype(BF16), w_ref[h], preferred_element_type=F32) + b_ref[h]
            for d in range(2):
                _, ig, _, a, one_minus_a2 = _gate_values(xv, pre, lam_ref[h], d)
                a_ref[d, :, cs] = a
                bb_ref[d, :, cs] = jnp.sqrt(one_minus_a2) * (ig * xv)

    both = pl.BlockSpec((2, rows.tm, width), lambda i: (0, i, 0))
    shape = jax.ShapeDtypeStruct((2, rows.t_all, width), F32)
    return rows.call(body, name, [rows.row(width), rows.full(wcat.shape), rows.full(bcat.shape), rows.full(lam.shape)],
                     [both, both], [shape, shape])(xr, wcat, bcat, lam)


def lru_gates_bwd(rows, xr, wcat, bcat, lam, da, dbb, name):
    n_blk = wcat.shape[0]
    width = xr.shape[1]

    def body(xr_ref, w_ref, b_ref, lam_ref, da_ref, dbb_ref, dxr_ref, dw_ref, db_ref, dlam_ref):
        i = pl.program_id(0)
        for h in range(n_blk):
            cs = slice(h * LANES, (h + 1) * LANES)
            xv = xr_ref[:, cs]
            xb = xv.astype(BF16)
            wv = w_ref[h]
            pre = jnp.dot(xb, wv, preferred_element_type=F32) + b_ref[h]
            dxr = jnp.zeros_like(xv)
            dpre, dlam = [], []
            for d in range(2):
                r, ig, sp, a, one_minus_a2 = _gate_values(xv, pre, lam_ref[h], d)
                inv_q = lax.rsqrt(one_minus_a2)
                dav, dbv = da_ref[d, :, cs], dbb_ref[d, :, cs]
                dbq = dbv * (one_minus_a2 * inv_q) * ig
                dlog_a = dav * a - dbv * (ig * xv) * ((a * a) * inv_q)
                dpre.append(dlog_a * (-LRU_C * sp) * (r - r * r))
                dpre.append(dbq * xv * (1.0 - ig))
                dxr = dxr + dbq
                dlam.append(_colsum(dlog_a * (-LRU_C * r)) * (-_sigmoid(-lam_ref[h][:, d * LANES:(d + 1) * LANES])))
            dpre = jnp.concatenate(dpre, axis=1)
            dpb = dpre.astype(BF16)
            dxr_ref[:, cs] = dxr + lax.dot_general(dpb, wv, (((1,), (1,)), ((), ())), preferred_element_type=F32)
            _accumulate(dw_ref.at[h], lax.dot_general(xb, dpb, (((0,), (0,)), ((), ())), preferred_element_type=F32), i == 0)
            _accumulate(db_ref.at[h], _colsum(dpre), i == 0)
            _accumulate(dlam_ref.at[h], jnp.concatenate(dlam, axis=1), i == 0)

    both = pl.BlockSpec((2, rows.tm, width), lambda i: (0, i, 0))
    return rows.call(
        body, name, [rows.row(width), rows.full(wcat.shape), rows.full(bcat.shape), rows.full(lam.shape), both, both],
        [rows.row(width), rows.full(wcat.shape), rows.full(bcat.shape), rows.full(lam.shape)],
        [jax.ShapeDtypeStruct((rows.t_all, width), F32), jax.ShapeDtypeStruct(wcat.shape, F32),
         jax.ShapeDtypeStruct(bcat.shape, F32), jax.ShapeDtypeStruct(lam.shape, F32)])(xr, wcat, bcat, lam, da, dbb)


def _tile_scan(a, b, reverse):
    n = a.shape[0]
    row = lax.broadcasted_iota(jnp.int32, a.shape, 0)
    k = 1
    while k < n:
        ok = (row < n - k) if reverse else (row >= k)
        shift = n - k if reverse else k
        b = b + a * jnp.where(ok, pltpu.roll(b, shift, 0), 0.0)
        a = a * jnp.where(ok, pltpu.roll(a, shift, 0), 1.0)
        k *= 2
    return a, b


def _chain_scan(a, b, h_in, reverse):
    n_blocks = a.shape[0] // SUBLANES
    out = [None] * n_blocks
    state = h_in
    for j in (range(n_blocks - 1, -1, -1) if reverse else range(n_blocks)):
        rows_j = slice(j * SUBLANES, (j + 1) * SUBLANES)
        cum, h0 = _tile_scan(a[rows_j], b[rows_j], reverse)
        out[j] = h0 + cum * state
        state = out[j][0:1] if reverse else out[j][SUBLANES - 1:SUBLANES]
    return jnp.concatenate(out, axis=0), state


def _neighbour(v, edge, reverse):
    n = v.shape[0]
    row = lax.broadcasted_iota(jnp.int32, v.shape, 0)
    if reverse:
        return jnp.where(row < n - 1, pltpu.roll(v, n - 1, 0), edge)
    return jnp.where(row >= 1, pltpu.roll(v, 1, 0), edge)


def _scan_call(rows, body, name, ins, in_specs, n_out, width, adjoint):
    n_all, n_lat = rows.n_all, rows.n_lat

    def tile(d, s):
        s = n_all - 1 - s if adjoint else s
        return jnp.where(d == 0, (s + n_lat) % n_all, n_all - 1 - s)

    def per_dir(d, s):
        return (d, tile(d, s), 0)

    specs = [pl.BlockSpec((None, rows.tm, width), per_dir) if kind == "dir" else
             pl.BlockSpec((rows.tm, width), lambda d, s: (tile(d, s), 0)) for kind in in_specs]
    shape = jax.ShapeDtypeStruct((2, rows.t_all, width), F32)
    return pl.pallas_call(
        body, name=name, grid=(2, n_all), in_specs=specs, out_specs=[pl.BlockSpec((None, rows.tm, width), per_dir)] * n_out,
        out_shape=[shape] * n_out, scratch_shapes=[pltpu.VMEM((SUBLANES, width), F32)],
        compiler_params=_params("arbitrary", "arbitrary"))(*ins)


def lru_scan(rows, a, bb, name):
    width = a.shape[2]
    n = rows.tm

    def body(a_ref, bb_ref, h_ref, hp_ref, carry):
        d, s = pl.program_id(0), pl.program_id(1)

        @pl.when(s == 0)
        def _():
            carry[...] = jnp.zeros_like(carry)

        def run(reverse):
            for cs in _chunks(width, LANES):
                h_in = carry[0:1, cs]
                h, carry[0:1, cs] = _chain_scan(a_ref[:, cs], bb_ref[:, cs], h_in, reverse)
                h_ref[:, cs] = h
                hp_ref[:, cs] = _neighbour(h, h_in, reverse)

        pl.when(d == 0)(lambda: run(False))
        pl.when(d == 1)(lambda: run(True))

    return _scan_call(rows, body, name, [a, bb], ["dir", "dir"], 2, width, adjoint=False)


def lru_scan_bwd(rows, dh, a, hp, name):
    width = a.shape[2]
    n = rows.tm

    def body(dh_ref, a_ref, hp_ref, da_ref, dbb_ref, carry):
        d, s = pl.program_id(0), pl.program_id(1)

        @pl.when(s == 0)
        def _():
            carry[...] = jnp.zeros_like(carry)

        def run(reverse):
            for cs in _chunks(width, LANES):
                av = a_ref[:, cs]
                g, _ = _chain_scan(_neighbour(av, 1.0, reverse), dh_ref[:, cs], carry[0:1, cs], reverse)
                da_ref[:, cs] = g * hp_ref[:, cs]
                dbb_ref[:, cs] = g
                carry[0:1, cs] = (av * g)[0:1] if reverse else (av * g)[n - 1:n]

        pl.when(d == 0)(lambda: run(True))
        pl.when(d == 1)(lambda: run(False))

    return _scan_call(rows, body, name, [dh, a, hp], ["shared", "dir", "dir"], 2, width, adjoint=True)


def mod_matmul(c9, w_shard, b_shard, name):
    n = w_shard.shape[1]
    tn = _pick(n, 768)

    def body(c_ref, w_ref, b_ref, o_ref):
        act = _silu_and_grad(c_ref[...])[0]
        o_ref[...] = jnp.dot(act, w_ref[...], preferred_element_type=F32, precision=lax.Precision.HIGHEST) + b_ref[...]

    return pl.pallas_call(
        body, name=name, grid=(n // tn,),
        in_specs=[pl.BlockSpec(c9.shape, lambda j: (0, 0)), pl.BlockSpec((w_shard.shape[0], tn), lambda j: (0, j)),
                  pl.BlockSpec((1, tn), lambda j: (0, j))],
        out_specs=pl.BlockSpec((c9.shape[0], tn), lambda j: (0, j)), out_shape=jax.ShapeDtypeStruct((c9.shape[0], n), F32),
        compiler_params=_params("arbitrary"))(c9, w_shard, b_shard)


def mod_matmul_bwd(c9, d9, w_shard, name):
    n = w_shard.shape[1]
    tn = _pick(n, 768)
    steps = n // tn

    def body(c_ref, d_ref, w_ref, gw_ref, gc_ref):
        j = pl.program_id(0)
        act, dact = _silu_and_grad(c_ref[...])
        dv = d_ref[...]
        gw_ref[...] = lax.dot_general(act, dv, (((0,), (0,)), ((), ())), preferred_element_type=F32, precision=lax.Precision.HIGHEST)
        part = lax.dot_general(dv, w_ref[...], (((1,), (1,)), ((), ())), preferred_element_type=F32, precision=lax.Precision.HIGHEST)
        _accumulate(gc_ref, part * dact, j == 0)

    return pl.pallas_call(
        body, name=name, grid=(steps,),
        in_specs=[pl.BlockSpec(c9.shape, lambda j: (0, 0)), pl.BlockSpec((c9.shape[0], tn), lambda j: (0, j)),
                  pl.BlockSpec((w_shard.shape[0], tn), lambda j: (0, j))],
        out_specs=[pl.BlockSpec((w_shard.shape[0], tn), lambda j: (0, j)), pl.BlockSpec(c9.shape, lambda j: (0, 0))],
        out_shape=[jax.ShapeDtypeStruct(w_shard.shape, F32), jax.ShapeDtypeStruct(c9.shape, F32)],
        compiler_params=_params("arbitrary"))(c9, d9, w_shard)


def _row_tile(n_rows, n_cols):
    return _pick(n_rows, max(2 * SUBLANES, (256 * 1024) // n_cols), 2 * SUBLANES)


def sum_parts(parts, name):
    n, n_rows, n_cols = parts.shape
    tr = _row_tile(n_rows, n_cols)

    def body(p_ref, o_ref):
        total = p_ref[0].astype(F32)
        for k in range(1, n):
            total = total + p_ref[k].astype(F32)
        o_ref[...] = total

    return pl.pallas_call(
        body, name=name, grid=(n_rows // tr,), in_specs=[pl.BlockSpec((n, tr, n_cols), lambda i: (0, i, 0))],
        out_specs=pl.BlockSpec((tr, n_cols), lambda i: (i, 0)), out_shape=jax.ShapeDtypeStruct((n_rows, n_cols), F32),
        compiler_params=_params("arbitrary"))(parts)


def _adamw_update(g, w_ref, m_ref, v_ref, g_ref, d_ref, m2_ref, v2_ref):
    m2 = ADAM_B1 * m_ref[...] + (1.0 - ADAM_B1) * g
    v2 = ADAM_B2 * v_ref[...] + (1.0 - ADAM_B2) * (g * g)
    m_hat = m2 / (1.0 - ADAM_B1 ** ADAM_STEP)
    v_hat = v2 / (1.0 - ADAM_B2 ** ADAM_STEP)
    g_ref[...] = g
    d_ref[...] = -ADAM_LR * (m_hat / (jnp.sqrt(v_hat) + ADAM_EPS) + ADAM_WD * w_ref[...])
    m2_ref[...] = m2
    v2_ref[...] = v2


def adamw(parts, w, m, v, name):
    n, n_rows, n_cols = parts.shape
    tr = _row_tile(n_rows, n_cols)

    def body(p_ref, *refs):
        g = p_ref[0].astype(F32)
        for k in range(1, n):
            g = g + p_ref[k].astype(F32)
        _adamw_update(g, *refs)

    blk = pl.BlockSpec((tr, n_cols), lambda i: (i, 0))
    shape = jax.ShapeDtypeStruct((n_rows, n_cols), F32)
    return pl.pallas_call(
        body, name=name, grid=(n_rows // tr,), in_specs=[pl.BlockSpec((n, tr, n_cols), lambda i: (0, i, 0)), blk, blk, blk],
        out_specs=[blk] * 4, out_shape=[shape] * 4, compiler_params=_params("arbitrary"))(parts, w, m, v)


def adamw_pair(part, w, m, v, name):
    n_rows, n_cols = part.shape
    tr = _row_tile(n_rows, n_cols)
    steps = n_rows // tr

    def body(p_ref, w_ref, m_ref, v_ref, g_ref, d_ref, m2_ref, v2_ref, inbox, send_sems, recv_sems, credits):
        i = pl.program_id(0)
        slot = i % 2
        sibling = (lax.axis_index("x"), lax.axis_index("y"), 1 - lax.axis_index("c"))

        @pl.when(i >= 2)
        def _():
            pl.semaphore_wait(credits.at[slot], 1)

        copy = pltpu.make_async_remote_copy(src_ref=p_ref, dst_ref=inbox.at[slot], send_sem=send_sems.at[slot],
                                            recv_sem=recv_sems.at[slot], device_id=sibling, device_id_type=MESH)
        copy.start()
        copy.wait_recv()
        copy.wait_send()
        _adamw_update(p_ref[...] + inbox[slot], w_ref, m_ref, v_ref, g_ref, d_ref, m2_ref, v2_ref)

        @pl.when(i + 2 < steps)
        def _():
            pl.semaphore_signal(credits.at[slot], inc=1, device_id=sibling, device_id_type=MESH)

    blk = pl.BlockSpec((tr, n_cols), lambda i: (i, 0))
    shape = jax.ShapeDtypeStruct((n_rows, n_cols), F32)
    return pl.pallas_call(
        body, name=name, grid=(steps,), in_specs=[blk] * 4, out_specs=[blk] * 4, out_shape=[shape] * 4,
        scratch_shapes=[pltpu.VMEM((2, tr, n_cols), F32), pltpu.SemaphoreType.DMA((2,)), pltpu.SemaphoreType.DMA((2,)),
                        pltpu.SemaphoreType.REGULAR((2,))],
        compiler_params=_params("arbitrary"))(part, w, m, v)


def _pack(arrays, row_multiple=SUBLANES):
    flat = [jnp.pad(a.reshape(-1), (0, -a.size % LANES)) for a in arrays]
    buf = jnp.concatenate(flat)
    buf = jnp.pad(buf, (0, -buf.size % (row_multiple * LANES)))
    return buf.reshape(-1, LANES)


def _unpack(buf, shapes):
    lead = buf.shape[:-2]
    flat = buf.reshape(lead + (-1,))
    out, pos = [], 0
    for shape in shapes:
        size = 1
        for s in shape:
            size *= s
        out.append(flat[..., pos:pos + size].reshape(lead + tuple(shape)))
        pos += size + (-size % LANES)
    return out


def _pos_embedding(seq_len, dim):
    rows = seq_len // GRID_W
    q = dim // 4
    omega = 1.0 / (10000.0 ** (jnp.arange(q, dtype=F32) / q))
    er = jnp.arange(rows).astype(F32)[:, None] * omega
    ec = jnp.arange(GRID_W).astype(F32)[:, None] * omega
    by_row = jnp.broadcast_to(jnp.concatenate([jnp.sin(er), jnp.cos(er)], axis=-1)[:, None, :], (rows, GRID_W, 2 * q))
    by_col = jnp.broadcast_to(jnp.concatenate([jnp.sin(ec), jnp.cos(ec)], axis=-1)[None, :, :], (rows, GRID_W, 2 * q))
    return jnp.concatenate([by_row, by_col], axis=-1).reshape(rows * GRID_W, dim)


def _cols_from_shards(g):
    return jnp.transpose(g, (1, 0, 2)).reshape(g.shape[1], -1)


def _shards_from_cols(w, n_shards=4):
    k, n = w.shape
    return jnp.transpose(w.reshape(k, n_shards, n // n_shards), (1, 0, 2))


def kernel(x, c, ctx, c_ctx, w_mod, b_mod, g_n1, w_ffn1_up, w_ffn1_down, g_n2, w_in, b_in, w_dw, b_dw, g_ln, b_ln, w_conf_out, w_lru_conv, b_lru_conv, w_rec_gate, b_rec_gate, w_in_gate, b_in_gate, lru_lambda, w_lru_out, w_out, g_n3, w_ffn2_up, w_ffn2_down, g_final, loss_target, m_c_ctx, m_w_mod, m_b_mod, m_g_n1, m_w_ffn1_up, m_w_ffn1_down, m_g_n2, m_w_in, m_b_in, m_w_dw, m_b_dw, m_g_ln, m_b_ln, m_w_conf_out, m_w_lru_conv, m_b_lru_conv, m_w_rec_gate, m_b_rec_gate, m_w_in_gate, m_b_in_gate, m_lru_lambda, m_w_lru_out, m_w_out, m_g_n3, m_w_ffn2_up, m_w_ffn2_down, m_g_final, v_c_ctx, v_w_mod, v_b_mod, v_g_n1, v_w_ffn1_up, v_w_ffn1_down, v_g_n2, v_w_in, v_b_in, v_w_dw, v_b_dw, v_g_ln, v_b_ln, v_w_conf_out, v_w_lru_conv, v_b_lru_conv, v_w_rec_gate, v_b_rec_gate, v_w_in_gate, v_b_in_gate, v_lru_lambda, v_w_lru_out, v_w_out, v_g_n3, v_w_ffn2_up, v_w_ffn2_down, v_g_final):
    given = dict(locals())
    wt = {n: given[n] for n in W_NAMES}
    mom = {n: given["m_" + n] for n in W_NAMES}
    var = {n: given["v_" + n] for n in W_NAMES}

    def flat2(a):
        if a.ndim == 1:
            return a.reshape(1, -1)
        a = a[0]
        return a if a.ndim == 2 else a.reshape(-1, a.shape[-1])

    t_lat, d = x.shape[1], x.shape[2]
    t_ctx = ctx.shape[1]
    rows = Rows(t_lat, t_ctx, d)
    xi, yi, ci = lax.axis_index("x"), lax.axis_index("y"), lax.axis_index("c")
    me, chip = 4 * xi + 2 * yi + ci, 2 * xi + yi
    lru_w = b_lru_conv.shape[-1]
    n_blk = lru_w // LANES
    taps = w_dw.shape[1]
    lru_taps = w_lru_conv.shape[1]
    ffn = w_ffn1_down.shape[1] * 4

    small_shapes = [(1, d)] + [flat2(wt[n]).shape for n in SMALL_SHARDED]
    small_all = exchange("all", [_pack([c] + [flat2(wt[n]) for n in SMALL_SHARDED])], False, "gather_small")[0]
    small_all = _unpack(small_all, small_shapes)
    c_all = small_all[0][:, 0, :]
    full = {n: jnp.concatenate([a[0], a[2], a[4], a[6]], axis=-1) for n, a in zip(SMALL_SHARDED, small_all[1:])}
    big = COL_SHARDED + ROW_SHARDED
    ffn1_names = ['w_ffn1_up', 'w_ffn1_down']
    ffn2_names = ['w_ffn2_up', 'w_ffn2_down']
    mixer_names = [n for n in big if n not in ffn1_names + ffn2_names]
    wb = {}

    def take_weights(names, gathered):
        for n, g in zip(names, gathered):
            wb[n] = _cols_from_shards(g) if n in COL_SHARDED else g.reshape(-1, g.shape[-1])

    def shards_after(names, earlier):
        return lax.optimization_barrier(([wt[n][0].astype(BF16) for n in names], earlier))[0]

    c9 = jnp.concatenate([c_all, c_ctx.reshape(1, d), jnp.zeros((7, d), F32)], axis=0)
    mod_cols = mod_matmul(c9, w_mod[0], lax.dynamic_slice_in_dim(b_mod, chip * w_mod.shape[2], w_mod.shape[2], axis=1), "mod_matmul")
    mod_all = exchange("all", [mod_cols], False, "gather_mod")[0]
    mod9 = jnp.concatenate([mod_all[0], mod_all[2], mod_all[4], mod_all[6]], axis=-1)
    mv = jnp.stack([lax.dynamic_index_in_dim(mod9, me, 0, keepdims=False).reshape(N_MOD, d), mod9[8].reshape(N_MOD, d)])

    gathered_ffn1 = exchange("chips", shards_after(ffn1_names, mod_all), False, "gather_weights_ffn1")
    take_weights(ffn1_names, gathered_ffn1)
    gathered_mixer = exchange_by_sequencer(shards_after(mixer_names, gathered_ffn1), False, "gather_weights_mixer", 1)

    x_in = jnp.concatenate([x[0], ctx[0]], axis=0)
    pos = jnp.concatenate([_pos_embedding(t_lat, d), jnp.zeros((t_ctx, d), F32)], axis=0)

    def ffn_fwd(tag, h, w_up, w_down):
        g, u, act = matmul_swiglu(h, w_up, f"{tag}_up")
        return (g, u), act, matmul(act, w_down, "nn", F32, f"{tag}_down")

    x0, h1 = norm_mod(rows, x_in, g_n1, mv, 0, 1, "norm1", pos=pos)
    gu1, act1, f1 = ffn_fwd("ffn1", h1, wb["w_ffn1_up"], wb["w_ffn1_down"])
    gathered_mixer, f1 = lax.optimization_barrier((gathered_mixer, f1))
    take_weights(mixer_names, gathered_mixer)
    gathered_ffn2 = exchange_by_sequencer(shards_after(ffn2_names, gathered_mixer), False, "gather_weights_ffn2", 5)
    x1, h2 = norm_mod(rows, x0, g_n2, mv, 3, 4, "norm2", resid=(f1, 2, 0.5))

    conf_w = w_conf_out.shape[1] * 4
    col_groups = [(0, 2 * conf_w), (2 * conf_w, lru_w), (2 * conf_w + lru_w, lru_w), (2 * conf_w + 2 * lru_w, 2 * d)]
    pvg, ux, gl, pgc = [matmul(h2, wb["w_in"][:, s:s + n], "nn", BF16, f"in_proj{k}", bias=b_in[:, s:s + n])
                        for k, (s, n) in enumerate(col_groups)]
    u = glu(rows, pvg, "glu")
    u2 = dwconv(rows, u, full["w_dw"], b_dw, taps // 2, F32, "conf_conv")
    s_act = ln_silu(rows, u2, g_ln, b_ln, "conf_ln")
    yc = matmul(s_act, wb["w_conf_out"], "nn", BF16, "conf_out")

    xr = dwconv(rows, ux, full["w_lru_conv"], b_lru_conv, LRU_PAD_LEFT, F32, "lru_conv")
    w_rec, w_ing = w_rec_gate[0].astype(BF16), w_in_gate[0].astype(BF16)
    wcat = jnp.concatenate([w_rec[0], w_ing[0], w_rec[1], w_ing[1]], axis=-1)

    def per_block(a):
        return jnp.transpose(a.reshape(2, n_blk, LANES), (1, 0, 2)).reshape(n_blk, 1, 2 * LANES)

    b_rec, b_ing = full["b_rec_gate"].reshape(2, n_blk, LANES), full["b_in_gate"].reshape(2, n_blk, LANES)
    bcat = jnp.concatenate([b_rec[0], b_ing[0], b_rec[1], b_ing[1]], axis=-1).reshape(n_blk, 1, 4 * LANES)
    lam = per_block(full["lru_lambda"])
    a_gate, b_gate = lru_gates(rows, xr, wcat, bcat, lam, "lru_gates")
    h_scan, h_prev = lru_scan(rows, a_gate, b_gate, "lru_scan")
    yl_in = lru_merge(rows, h_scan, gl, "lru_merge")
    yl = matmul(yl_in, wb["w_lru_out"], "nn", BF16, "lru_out")
    z = z_merge(rows, pgc, yc, yl, "z_merge")
    y = matmul(z, wb["w_out"], "nn", F32, "mix_out")

    gathered_ffn2, y = lax.optimization_barrier((gathered_ffn2, y))
    take_weights(ffn2_names, gathered_ffn2)
    x2, h3 = norm_mod(rows, x1, g_n3, mv, 6, 7, "norm3", resid=(y, 5, 1.0))
    gu2, act2, f2 = ffn_fwd("ffn2", h3, wb["w_ffn2_up"], wb["w_ffn2_down"])
    loss_part, dx3, dg_final = loss_head(rows, x2, f2, mv, g_final.reshape(1, d), loss_target[0], "loss_head")

    grads = {"g_final": dg_final}
    dmv = [None] * N_MOD

    def ffn_bwd(tag, dxn, x_prev, h, gu, act, f, w_up, w_down, g, ks, collective_id, settle_first=()):
        k_shift, k_scale, k_gate = ks
        df, dmv[k_gate] = gate_bwd(rows, dxn, f, mv, k_gate, 0.5, f"{tag}_gate_bwd")
        grads[f"w_{tag}_down"] = matmul(act, df, "tn", F32, f"{tag}_down_dw")
        dact = matmul(df, w_down, "nt", BF16, f"{tag}_down_dx")
        if settle_first:
            dact = settle(settle_first, dact)
        dgu = swiglu_bwd(rows, gu[0], gu[1], dact, f"{tag}_act_bwd")
        grads[f"w_{tag}_up"] = matmul(h, dgu, "tn", F32, f"{tag}_up_dw")
        dgu = scatter_behind([f"w_{tag}_up", f"w_{tag}_down"], tag, collective_id, dgu)
        dh = matmul(dgu, w_up, "nt", F32, f"{tag}_up_dx")
        dx, dg, dmv[k_shift], dmv[k_scale] = norm_mod_bwd(rows, x_prev, dh, dxn, g, mv, k_shift, k_scale, f"{tag}_norm_bwd")
        return dx, dg

    def grad_pieces(names):
        return [(_shards_from_cols(grads[n]) if n in COL_SHARDED else grads[n].reshape(4, -1, grads[n].shape[-1])).astype(BF16)
                for n in names]

    from_chips = {}

    def scatter_behind(names, tag, collective_id, carry):
        ready, carry = lax.optimization_barrier(([grads[n] for n in names], carry))
        grads.update(zip(names, ready))
        from_chips.update(zip(names, exchange_by_sequencer(grad_pieces(names), True, f"scatter_grads_{tag}", collective_id)))
        return carry

    def settle(names, carry):
        landed, carry = lax.optimization_barrier(([from_chips[n] for n in names], carry))
        from_chips.update(zip(names, landed))
        return carry

    dx2, grads["g_n3"] = ffn_bwd("ffn2", dx3, x2, h3, gu2, act2, f2, wb["w_ffn2_up"], wb["w_ffn2_down"], g_n3, (6, 7, 8), 2)

    dy, dmv[5] = gate_bwd(rows, dx2, y, mv, 5, 1.0, "mix_gate_bwd")
    grads["w_out"] = matmul(z, dy, "tn", F32, "mix_out_dw")
    dz = matmul(dy, wb["w_out"], "nt", BF16, "mix_out_dx")
    dyc, dyl, dpgc = z_merge_bwd(rows, pgc, yc, yl, dz, "z_merge_bwd")

    grads["w_conf_out"] = matmul(s_act, dyc, "tn", F32, "conf_out_dw")
    ds_act = matmul(dyc, wb["w_conf_out"], "nt", BF16, "conf_out_dx")
    du2, grads["g_ln"], grads["b_ln"] = ln_silu_bwd(rows, u2, ds_act, g_ln, b_ln, "conf_ln_bwd")
    grads["w_dw"], grads["b_dw"] = dwconv_bwd_w(rows, u, du2, taps, taps // 2, "conf_conv_dw")
    du = dwconv(rows, du2, full["w_dw"][::-1], None, taps - 1 - taps // 2, F32, "conf_conv_dx")
    dpvg = glu_bwd(rows, pvg, du, "glu_bwd")

    grads["w_lru_out"] = matmul(yl_in, dyl, "tn", F32, "lru_out_dw")
    dyl = scatter_behind(['w_out', 'w_conf_out', 'w_lru_out'], "mixer", 3, settle(ffn2_names, dyl))
    dyl_in = matmul(dyl, wb["w_lru_out"], "nt", BF16, "lru_out_dx")
    dh_sum, dgl = lru_merge_bwd(rows, h_scan, gl, dyl_in, "lru_merge_bwd")
    da_gate, db_gate = lru_scan_bwd(rows, dh_sum, a_gate, h_prev, "lru_scan_bwd")
    dxr, dwcat, dbcat, dlam = lru_gates_bwd(rows, xr, wcat, bcat, lam, da_gate, db_gate, "lru_gates_bwd")
    grads["w_rec_gate"] = jnp.stack([dwcat[:, :, 0:LANES], dwcat[:, :, 2 * LANES:3 * LANES]])
    grads["w_in_gate"] = jnp.stack([dwcat[:, :, LANES:2 * LANES], dwcat[:, :, 3 * LANES:4 * LANES]])
    dbcat = dbcat.reshape(n_blk, 4, LANES)
    grads["b_rec_gate"] = jnp.stack([dbcat[:, 0], dbcat[:, 2]]).reshape(2, lru_w)
    grads["b_in_gate"] = jnp.stack([dbcat[:, 1], dbcat[:, 3]]).reshape(2, lru_w)
    grads["lru_lambda"] = jnp.transpose(dlam.reshape(n_blk, 2, LANES), (1, 0, 2)).reshape(2, lru_w)
    grads["w_lru_conv"], grads["b_lru_conv"] = dwconv_bwd_w(rows, ux, dxr, lru_taps, LRU_PAD_LEFT, "lru_conv_dw")
    dux = dwconv(rows, dxr, full["w_lru_conv"][::-1], None, lru_taps - 1 - LRU_PAD_LEFT, BF16, "lru_conv_dx")

    dproj = jnp.concatenate([dpvg, dux, dgl, dpgc], axis=1)
    grads["b_in"] = col_sum(rows, dproj, "in_proj_db")
    grads["w_in"] = matmul(h2, dproj, "tn", F32, "in_proj_dw")
    dproj = scatter_behind(['w_in'], "in_proj", 4, settle(['w_out', 'w_conf_out', 'w_lru_out'], dproj))
    dh2 = matmul(dproj, wb["w_in"], "nt", F32, "in_proj_dx")
    dx1, grads["g_n2"], dmv[3], dmv[4] = norm_mod_bwd(rows, x1, dh2, dx2, g_n2, mv, 3, 4, "norm2_bwd")

    dx0, grads["g_n1"] = ffn_bwd("ffn1", dx1, x0, h1, gu1, act1, f1, wb["w_ffn1_up"], wb["w_ffn1_down"], g_n1, (0, 1, 2), 6, settle_first=['w_in'])
    grad_x = dx0[:t_lat][None]

    chip_sums = [sum_parts(from_chips[n].reshape(4, -1, from_chips[n].shape[-1]), f"chip_sum_{n}") for n in big]
    out = {}
    for n, p in zip(big, chip_sums):
        out[n] = adamw_pair(p, flat2(wt[n]), flat2(mom[n]), flat2(var[n]), f"adamw_{n}")

    dmod = jnp.concatenate(dmv, axis=1)
    small_names = ['g_n1', 'g_n2', 'b_in', 'b_dw', 'g_ln', 'b_ln', 'b_lru_conv', 'w_rec_gate', 'w_in_gate', 'g_n3', 'g_final'] + SMALL_SHARDED
    small_list = [loss_part] + [grads[n] for n in small_names] + [dmod[0], dmod[1]]
    small_buf = _pack(small_list, 8 * 2 * SUBLANES)
    small_parts, dmod_all = exchange("all", [small_buf.reshape(8, -1, LANES), _pack([dmod[0]])], [True, False], "scatter_small_grads")
    small_eighth = sum_parts(small_parts, "sum_small_grads")
    small_sum = exchange("all", [small_eighth], False, "gather_small_grads")[0].reshape(small_buf.shape)
    small_sum = _unpack(small_sum, [a.shape for a in small_list])
    loss = small_sum[0][0, 0]
    total = dict(zip(small_names, small_sum[1:]))
    dmod_all = _unpack(dmod_all, [dmod[0].shape])[0].reshape(8, N_MOD * d)
    dmc = small_sum[-1].reshape(1, N_MOD * d)
    total["b_mod"] = small_sum[-2].reshape(1, N_MOD * d) + dmc

    d9 = jnp.concatenate([dmod_all, dmc, jnp.zeros((7, N_MOD * d), F32)], axis=0)
    d9_cols = lax.dynamic_slice_in_dim(d9, chip * w_mod.shape[2], w_mod.shape[2], axis=1)
    g_wmod, dc9 = mod_matmul_bwd(c9, d9_cols, w_mod[0], "mod_matmul_bwd")
    dc_all = exchange("all", [dc9[8:16]], False, "gather_dc")[0]
    total["c_ctx"] = sum_parts(jnp.stack([dc_all[0], dc_all[2], dc_all[4], dc_all[6]]), "sum_dc")[0:1]
    out["w_mod"] = adamw(g_wmod[None], w_mod[0], m_w_mod[0], v_w_mod[0], "adamw_w_mod")

    small_all_names = [n for n in W_NAMES if n not in out]
    g_local, shapes = [], []
    for n in small_all_names:
        g = total[n].reshape(flat2(wt[n]).shape[:-1] + (-1,)) if n in SMALL_SHARDED else total[n].reshape(flat2(wt[n]).shape)
        if n in SMALL_SHARDED:
            width = wt[n].shape[-1]
            g = lax.dynamic_slice_in_dim(g, chip * width, width, axis=1)
        g_local.append(g)
        shapes.append(g.shape)
    packed = adamw(_pack(g_local)[None], _pack([flat2(wt[n]) for n in small_all_names]),
                   _pack([flat2(mom[n]) for n in small_all_names]), _pack([flat2(var[n]) for n in small_all_names]), "adamw_small")
    for k, n in enumerate(small_all_names):
        out[n] = [_unpack(p, shapes)[k] for p in packed]

    results = [loss, grad_x]
    for k in range(4):
        results += [out[n][k].reshape(wt[n].shape) for n in W_NAMES]
    return tuple(results)
```

```python
import functools

import jax
import jax.numpy as jnp
from jax import lax
from jax.experimental import pallas as pl
from jax.experimental.pallas import tpu as pltpu
from jax.experimental.pallas import tpu_sc as plsc

F32 = jnp.float32
BF16 = jnp.bfloat16
MESH = pl.DeviceIdType.MESH

EPS = 1e-6
GRID_W = 64
N_MOD = 9
LRU_C = 8.0
LRU_PAD_LEFT = 2
ADAM_LR, ADAM_B1, ADAM_B2, ADAM_EPS, ADAM_WD, ADAM_STEP = 0.001, 0.9, 0.999, 1e-08, 0.01, 10

LANES = 128
SUBLANES = 8
HALO = 16
VMEM_LIMIT = 56 * 1024 * 1024

W_NAMES = ['c_ctx', 'w_mod', 'b_mod', 'g_n1', 'w_ffn1_up', 'w_ffn1_down', 'g_n2', 'w_in', 'b_in', 'w_dw', 'b_dw',
           'g_ln', 'b_ln', 'w_conf_out', 'w_lru_conv', 'b_lru_conv', 'w_rec_gate', 'b_rec_gate', 'w_in_gate',
           'b_in_gate', 'lru_lambda', 'w_lru_out', 'w_out', 'g_n3', 'w_ffn2_up', 'w_ffn2_down', 'g_final']
COL_SHARDED = ['w_ffn1_up', 'w_in', 'w_ffn2_up']
ROW_SHARDED = ['w_ffn1_down', 'w_conf_out', 'w_lru_out', 'w_out', 'w_ffn2_down']
SMALL_SHARDED = ['w_dw', 'w_lru_conv', 'b_rec_gate', 'b_in_gate', 'lru_lambda']


def _params(*semantics):
    return pltpu.CompilerParams(dimension_semantics=semantics, vmem_limit_bytes=VMEM_LIMIT)


def _pick(n, target, mult=LANES):
    best = None
    for t in range(mult, min(n, target) + 1, mult):
        if n % t == 0:
            best = t
    return best or n


def _chunks(width, target=512):
    w = _pick(width, target)
    return [slice(s, s + w) for s in range(0, width, w)]


def _sigmoid(x):
    return jax.nn.sigmoid(x)


def _silu_and_grad(x):
    s = _sigmoid(x)
    return x * s, s * (1.0 + x * (1.0 - s))


_GELU_K = 0.7978845608028654


def _gelu_and_grad(x):
    t = jnp.tanh(_GELU_K * (x + 0.044715 * (x * x * x)))
    return 0.5 * x * (1.0 + t), 0.5 * (1.0 + t) + 0.5 * x * (1.0 - t * t) * _GELU_K * (1.0 + 3 * 0.044715 * x * x)


def _neg_expm1(z):
    series = -(z * (1.0 + z * (0.5 + z * (1.0 / 6.0))))
    return jnp.where(z > -0.01, series, 1.0 - jnp.exp(z))


def _softplus(x):
    return jnp.maximum(x, 0.0) + jnp.log1p(jnp.exp(-jnp.abs(x)))


def _accumulate(ref, value, first):
    @pl.when(first)
    def _():
        ref[...] = value

    @pl.when(jnp.logical_not(first))
    def _():
        ref[...] += value


def _colsum(x):
    return jnp.sum(x, axis=0, keepdims=True)


class Rows:
    def __init__(self, t_lat, t_ctx, d_model):
        self.tm = _pick(t_ctx, 256, HALO)
        assert t_lat % self.tm == 0 and t_ctx % self.tm == 0
        self.n_lat = t_lat // self.tm
        self.n_all = (t_lat + t_ctx) // self.tm
        self.t_all = t_lat + t_ctx
        self.d = d_model

    def seg(self, i):
        return jnp.where(i >= self.n_lat, 1, 0)

    def seg_first(self, i):
        return jnp.logical_or(i == 0, i == self.n_lat)

    def seg_last(self, i):
        return jnp.logical_or(i == self.n_lat - 1, i == self.n_all - 1)

    def row(self, width, col=0):
        return pl.BlockSpec((self.tm, width), lambda i: (i, col))

    def vec(self, width):
        return pl.BlockSpec((1, width), lambda i: (0, 0))

    def full(self, shape):
        return pl.BlockSpec(shape, lambda i: (0,) * len(shape))

    def mod(self):
        return pl.BlockSpec((None, N_MOD, self.d), lambda i: (self.seg(i), 0, 0))

    def segsum(self):
        return pl.BlockSpec((None, 1, self.d), lambda i: (self.seg(i), 0, 0))

    def segsum_shape(self):
        return jax.ShapeDtypeStruct((2, 1, self.d), F32)

    def halo(self, width, side):
        per = self.tm // HALO
        last = self.t_all // HALO - 1
        if side < 0:
            return pl.BlockSpec((HALO, width), lambda i: (jnp.maximum(i * per - 1, 0), 0))
        return pl.BlockSpec((HALO, width), lambda i: (jnp.minimum((i + 1) * per, last), 0))

    def call(self, body, name, in_specs, out_specs, out_shape, scratch=()):
        return pl.pallas_call(body, name=name, grid=(self.n_all,), in_specs=in_specs, out_specs=out_specs,
                              out_shape=out_shape, scratch_shapes=list(scratch), compiler_params=_params("arbitrary"))


_PEER_FLIPS = {
    "chips": [(1, 0, 0), (0, 1, 0), (1, 1, 0)],
    "sibling": [(0, 0, 1)],
    "all": [(0, 0, 1), (0, 1, 0), (0, 1, 1), (1, 0, 0), (1, 0, 1), (1, 1, 0), (1, 1, 1)],
}


def _slot(kind, x, y, c):
    return {"chips": 2 * x + y, "sibling": c, "all": 4 * x + 2 * y + c}[kind]


def exchange(kind, srcs, indexed, name):
    flips = _PEER_FLIPS[kind]
    n_slots = len(flips) + 1
    n_t = len(srcs)
    by_slot = list(indexed) if isinstance(indexed, (list, tuple)) else [indexed] * n_t
    out_shapes = [jax.ShapeDtypeStruct(s.shape if ix else (n_slots,) + s.shape, s.dtype) for s, ix in zip(srcs, by_slot)]

    def body(*refs):
        src_refs, dst_refs = refs[:n_t], refs[n_t:2 * n_t]
        send_sems, recv_sems, local_sems = refs[2 * n_t:]
        x, y, c = lax.axis_index("x"), lax.axis_index("y"), lax.axis_index("c")
        me = _slot(kind, x, y, c)

        def piece(t, k):
            return src_refs[t].at[k] if by_slot[t] else src_refs[t]

        peers = [(1 - x if fx else x, 1 - y if fy else y, 1 - c if fc else c) for fx, fy, fc in flips]
        local = [pltpu.make_async_copy(piece(t, me), dst_refs[t].at[me], local_sems.at[t]) for t in range(n_t)]
        for cp in local:
            cp.start()
        sends = []
        for t in range(n_t):
            for p, peer in enumerate(peers):
                sends.append(pltpu.make_async_remote_copy(
                    src_ref=piece(t, _slot(kind, *peer)), dst_ref=dst_refs[t].at[me],
                    send_sem=send_sems.at[t, p], recv_sem=recv_sems.at[t, p], device_id=peer, device_id_type=MESH))
        for cp in sends:
            cp.start()
        for t in range(n_t):
            for p, peer in enumerate(peers):
                theirs = _slot(kind, *peer)
                pltpu.make_async_remote_copy(
                    src_ref=piece(t, me), dst_ref=dst_refs[t].at[theirs],
                    send_sem=send_sems.at[t, p], recv_sem=recv_sems.at[t, p], device_id=peer, device_id_type=MESH).wait_recv()
        for cp in sends:
            cp.wait_send()
        for cp in local:
            cp.wait()

    any_spec = pl.BlockSpec(memory_space=pl.ANY)
    outs = pl.pallas_call(
        body, name=name, out_shape=out_shapes, in_specs=[any_spec] * n_t, out_specs=[any_spec] * n_t,
        scratch_shapes=[pltpu.SemaphoreType.DMA((n_t, len(flips))), pltpu.SemaphoreType.DMA((n_t, len(flips))),
                        pltpu.SemaphoreType.DMA((n_t,))],
        compiler_params=pltpu.CompilerParams(has_side_effects=True),
    )(*srcs)
    return list(outs)


def exchange_by_sequencer(srcs, indexed, name, collective_id):
    flips = _PEER_FLIPS["chips"]
    n_t, n_p = len(srcs), len(flips)
    src_refs = [jax.new_ref(s, memory_space=pltpu.MemorySpace.HBM) for s in srcs]
    land_refs = [jax.empty_ref(jax.ShapeDtypeStruct(s.shape if indexed else (n_p + 1,) + s.shape, s.dtype),
                               memory_space=pltpu.MemorySpace.HBM) for s in srcs]

    @pl.kernel(mesh=plsc.ScalarSubcoreMesh(axis_name="sequencer", num_cores=1), name=name,
               scratch_types=(pltpu.SemaphoreType.DMA((n_t, n_p)), pltpu.SemaphoreType.DMA((n_t, n_p)), pltpu.SemaphoreType.DMA((n_t,))),
               compiler_params=pltpu.CompilerParams(collective_id=collective_id))
    def launch(send_sems, recv_sems, local_sems):
        x, y, c = lax.axis_index("x"), lax.axis_index("y"), lax.axis_index("c")
        peers = [(1 - x if fx else x, 1 - y if fy else y, c) for fx, fy, _ in flips]
        barrier = pltpu.get_barrier_semaphore()
        for peer in peers:
            pl.semaphore_signal(barrier, inc=1, device_id=peer, device_id_type=MESH)
        pl.semaphore_wait(barrier, n_p)
        me = _slot("chips", x, y, c)

        def piece(t, k):
            return src_refs[t].at[k] if indexed else src_refs[t]

        local = [pltpu.make_async_copy(piece(t, me), land_refs[t].at[me], local_sems.at[t]) for t in range(n_t)]
        for cp in local:
            cp.start()
        sends = []
        for t in range(n_t):
            for p, peer in enumerate(peers):
                sends.append(pltpu.make_async_remote_copy(
                    src_ref=piece(t, _slot("chips", *peer)), dst_ref=land_refs[t].at[me],
                    send_sem=send_sems.at[t, p], recv_sem=recv_sems.at[t, p], device_id=peer, device_id_type=MESH))
        for cp in sends:
            cp.start()
        for t in range(n_t):
            for p, peer in enumerate(peers):
                pltpu.make_async_remote_copy(
                    src_ref=piece(t, me), dst_ref=land_refs[t].at[_slot("chips", *peer)],
                    send_sem=send_sems.at[t, p], recv_sem=recv_sems.at[t, p], device_id=peer, device_id_type=MESH).wait_recv()
        for cp in sends:
            cp.wait_send()
        for cp in local:
            cp.wait()

    launch()
    return [r[...] for r in land_refs]


MATMUL_VMEM_BUDGET = 44 * 1024 * 1024
MATMUL_STEP_BYTES = 1024 * 1024


def _divisors(n, most):
    return [t for t in range(LANES, min(n, most) + 1, LANES) if n % t == 0] or [n]


def _matmul_tiles(n_i, n_j, n_r, a_bytes, b_bytes, out_bytes):
    best = None
    for tr in _divisors(n_r, n_r):
        steps = n_r // tr
        for ti in _divisors(n_i, 1408):
            for tj in _divisors(n_j, 1664):
                vmem = 2 * (ti * tr * a_bytes + tr * tj * b_bytes + ti * tj * out_bytes) + (ti * tj * 4 if steps > 1 else 0)
                if vmem > MATMUL_VMEM_BUDGET or ti < min(n_i, 2 * LANES) or tj < min(n_j, 4 * LANES):
                    continue
                size_a, size_b = n_i * n_r * a_bytes, n_r * n_j * b_bytes
                if steps == 1:
                    moved = min(size_a + (n_i // ti) * size_b, size_b + (n_j // tj) * size_a)
                else:
                    moved = (n_j // tj) * size_a + (n_i // ti) * size_b + (steps - 1) * n_i * n_j * 8
                moved += n_i * n_j * out_bytes + (n_i // ti) * (n_j // tj) * steps * MATMUL_STEP_BYTES
                if best is None or moved < best[0]:
                    best = (moved, ti, tj, tr)
    return best[1:]


def matmul(a, b, mode, out_dtype, name, bias=None):
    if mode == "nn":
        (n_i, n_r), (_, n_j) = a.shape, b.shape
    elif mode == "nt":
        (n_i, n_r), (n_j, _) = a.shape, b.shape
    else:
        (n_r, n_i), (_, n_j) = a.shape, b.shape
    ti, tj, tr = _matmul_tiles(n_i, n_j, n_r, a.dtype.itemsize, b.dtype.itemsize, jnp.dtype(out_dtype).itemsize)
    steps = n_r // tr
    bytes_a, bytes_b = a.size * a.dtype.itemsize, b.size * b.dtype.itemsize
    j_outer = steps == 1 and bytes_b + (n_j // tj) * bytes_a < bytes_a + (n_i // ti) * bytes_b

    def at(fn):
        return (lambda j, i, r: fn(i, j, r)) if j_outer else fn

    a_spec = {"nn": pl.BlockSpec((ti, tr), at(lambda i, j, r: (i, r))), "nt": pl.BlockSpec((ti, tr), at(lambda i, j, r: (i, r))),
              "tn": pl.BlockSpec((tr, ti), at(lambda i, j, r: (r, i)))}[mode]
    b_spec = {"nn": pl.BlockSpec((tr, tj), at(lambda i, j, r: (r, j))), "nt": pl.BlockSpec((tj, tr), at(lambda i, j, r: (j, r))),
              "tn": pl.BlockSpec((tr, tj), at(lambda i, j, r: (r, j)))}[mode]
    dims = {"nn": (((1,), (0,)), ((), ())), "nt": (((1,), (1,)), ((), ())), "tn": (((0,), (0,)), ((), ()))}[mode]

    def body(*refs):
        a_ref, b_ref = refs[:2]
        bias_ref = refs[2] if bias is not None else None
        o_ref = refs[3] if bias is not None else refs[2]
        prod = lax.dot_general(a_ref[...].astype(BF16), b_ref[...].astype(BF16), dims, preferred_element_type=F32)

        def finish(total):
            if bias_ref is not None:
                total = total + bias_ref[...]
            o_ref[...] = total.astype(out_dtype)

        if steps == 1:
            finish(prod)
        else:
            acc = refs[-1]
            r = pl.program_id(2)

            @pl.when(r == 0)
            def _():
                acc[...] = prod

            @pl.when(r > 0)
            def _():
                acc[...] += prod

            @pl.when(r == steps - 1)
            def _():
                finish(acc[...])

    in_specs = [a_spec, b_spec] + ([pl.BlockSpec((1, tj), at(lambda i, j, r: (0, j)))] if bias is not None else [])
    args = (a, b) + ((bias,) if bias is not None else ())
    grid = (n_j // tj, n_i // ti, steps) if j_outer else (n_i // ti, n_j // tj, steps)
    return pl.pallas_call(
        body, name=name, grid=grid, in_specs=in_specs,
        out_specs=pl.BlockSpec((ti, tj), at(lambda i, j, r: (i, j))), out_shape=jax.ShapeDtypeStruct((n_i, n_j), out_dtype),
        scratch_shapes=[pltpu.VMEM((ti, tj), F32)] if steps > 1 else [],
        compiler_params=_params("arbitrary", "arbitrary", "arbitrary"),
    )(*args)


def matmul_swiglu(h, w_up, name):
    n_t, d = h.shape
    f = w_up.shape[1] // 2
    ti, tj = _pick(n_t, 384), _pick(f, 1408)
    n_j = f // tj

    def body(h_ref, wg_ref, wu_ref, g_ref, u_ref, act_ref):
        hv = h_ref[...]
        g = jnp.dot(hv, wg_ref[...], preferred_element_type=F32)
        u = jnp.dot(hv, wu_ref[...], preferred_element_type=F32)
        g_ref[...] = g.astype(BF16)
        u_ref[...] = u.astype(BF16)
        act_ref[...] = (_silu_and_grad(g)[0] * u).astype(BF16)

    out = pl.BlockSpec((ti, tj), lambda j, i: (i, j))
    shape = jax.ShapeDtypeStruct((n_t, f), BF16)
    return pl.pallas_call(
        body, name=name, grid=(n_j, n_t // ti),
        in_specs=[pl.BlockSpec((ti, d), lambda j, i: (i, 0)), pl.BlockSpec((d, tj), lambda j, i: (0, j)),
                  pl.BlockSpec((d, tj), lambda j, i: (0, j + n_j))],
        out_specs=[out, out, out], out_shape=[shape, shape, shape], compiler_params=_params("arbitrary", "arbitrary"),
    )(h, w_up, w_up)


def norm_mod(rows, x, g, mv, shift_k, scale_k, name, resid=None):
    d = rows.d
    changed = resid is not None

    def body(*refs):
        refs = list(refs)
        x_ref, g_ref, mv_ref = refs[:3]
        rest = refs[3:]
        xv = x_ref[...]
        if resid is not None:
            xv = xv + resid[2] * mv_ref[resid[1]:resid[1] + 1, :] * rest.pop(0)[...]
        if changed:
            rest.pop(0)[...] = xv
        r = lax.rsqrt(jnp.mean(xv * xv, axis=-1, keepdims=True) + EPS)
        h = (xv * r) * g_ref[...]
        if shift_k is not None:
            h = h * (1.0 + mv_ref[scale_k:scale_k + 1, :]) + mv_ref[shift_k:shift_k + 1, :]
        rest.pop(0)[...] = h.astype(BF16)

    ins = [x, g, mv] + ([resid[0]] if resid is not None else [])
    in_specs = [rows.row(d), rows.vec(d), rows.mod()] + [rows.row(d)] * (len(ins) - 3)
    out_shape = ([jax.ShapeDtypeStruct((rows.t_all, d), F32)] if changed else []) + [jax.ShapeDtypeStruct((rows.t_all, d), BF16)]
    outs = rows.call(body, name, in_specs, [rows.row(d)] * len(out_shape), out_shape)(*ins)
    return (outs[0], outs[1]) if changed else (None, outs[0])


def embed_norm_mod(rows, x_lat, x_ctx, row_tab, col_tab, g, mv, shift_k, scale_k, name):
    d = rows.d
    half = d // 2
    per_tile = rows.tm // GRID_W
    assert rows.tm % GRID_W == 0 and per_tile <= SUBLANES

    def body(xl_ref, xc_ref, rt_ref, ct_ref, g_ref, mv_ref, x0_ref, h_ref):
        i = pl.program_id(0)
        by_row = jnp.concatenate([jnp.broadcast_to(rt_ref[r:r + 1, :], (GRID_W, half)) for r in range(per_tile)], axis=0)
        by_col = jnp.concatenate([ct_ref[...]] * per_tile, axis=0)
        xv = jnp.where(i < rows.n_lat, xl_ref[...] + jnp.concatenate([by_row, by_col], axis=1), xc_ref[...])
        x0_ref[...] = xv
        r = lax.rsqrt(jnp.mean(xv * xv, axis=-1, keepdims=True) + EPS)
        h = (xv * r) * g_ref[...]
        h_ref[...] = (h * (1.0 + mv_ref[scale_k:scale_k + 1, :]) + mv_ref[shift_k:shift_k + 1, :]).astype(BF16)

    last = rows.n_lat - 1
    in_specs = [pl.BlockSpec((rows.tm, d), lambda i: (jnp.minimum(i, last), 0)),
                pl.BlockSpec((rows.tm, d), lambda i: (jnp.maximum(i - rows.n_lat, 0), 0)),
                pl.BlockSpec((SUBLANES, half), lambda i: (jnp.minimum(i, last), 0)), rows.full(col_tab.shape), rows.vec(d), rows.mod()]
    return rows.call(body, name, in_specs, [rows.row(d), rows.row(d)],
                     [jax.ShapeDtypeStruct((rows.t_all, d), F32), jax.ShapeDtypeStruct((rows.t_all, d), BF16)])(
                         x_lat, x_ctx, row_tab, col_tab, g, mv)


def norm_mod_bwd(rows, x, dh, dxn, g, mv, shift_k, scale_k, name, latent_only=False):
    d = rows.d
    n_dx = rows.n_lat if latent_only else rows.n_all

    def body(x_ref, dh_ref, dxn_ref, g_ref, mv_ref, dx_ref, dg_ref, dsh_ref, dsc_ref):
        i = pl.program_id(0)
        xv, dhv, gv = x_ref[...], dh_ref[...], g_ref[...]
        r = lax.rsqrt(jnp.mean(xv * xv, axis=-1, keepdims=True) + EPS)
        n = xv * r
        dy = dhv * (1.0 + mv_ref[scale_k:scale_k + 1, :])
        dn = dy * gv

        @pl.when(i < n_dx)
        def _():
            dx_ref[...] = dxn_ref[...] + r * (dn - n * jnp.mean(dn * n, axis=-1, keepdims=True))

        _accumulate(dg_ref, _colsum(dy * n), i == 0)
        _accumulate(dsh_ref, _colsum(dhv), rows.seg_first(i))
        _accumulate(dsc_ref, _colsum(dhv * (n * gv)), rows.seg_first(i))

    dx_spec = pl.BlockSpec((rows.tm, d), lambda i: (jnp.minimum(i, n_dx - 1), 0))
    return rows.call(
        body, name, [rows.row(d), rows.row(d), rows.row(d), rows.vec(d), rows.mod()],
        [dx_spec, rows.vec(d), rows.segsum(), rows.segsum()],
        [jax.ShapeDtypeStruct((n_dx * rows.tm, d), F32), jax.ShapeDtypeStruct((1, d), F32), rows.segsum_shape(), rows.segsum_shape()],
    )(x, dh, dxn, g, mv)


def gate_bwd(rows, dxn, f, mv, gate_k, scale, name):
    d = rows.d

    def body(dxn_ref, f_ref, mv_ref, df_ref, dgate_ref):
        i = pl.program_id(0)
        dv = dxn_ref[...]
        df_ref[...] = (scale * mv_ref[gate_k:gate_k + 1, :] * dv).astype(BF16)
        _accumulate(dgate_ref, scale * _colsum(dv * f_ref[...]), rows.seg_first(i))

    return rows.call(body, name, [rows.row(d), rows.row(d), rows.mod()], [rows.row(d), rows.segsum()],
                     [jax.ShapeDtypeStruct((rows.t_all, d), BF16), rows.segsum_shape()])(dxn, f, mv)


def swiglu_bwd(rows, g, u, dact, name):
    f = g.shape[1]

    def body(g_ref, u_ref, da_ref, o_ref):
        for cs in _chunks(f):
            act, dact_dg = _silu_and_grad(g_ref[:, cs].astype(F32))
            dav = da_ref[:, cs].astype(F32)
            o_ref[:, cs] = (dav * u_ref[:, cs].astype(F32) * dact_dg).astype(BF16)
            o_ref[:, slice(f + cs.start, f + cs.stop)] = (dav * act).astype(BF16)

    return rows.call(body, name, [rows.row(f), rows.row(f), rows.row(f)], rows.row(2 * f),
                     jax.ShapeDtypeStruct((rows.t_all, 2 * f), BF16))(g, u, dact)


def glu(rows, pvg, name):
    w = pvg.shape[1] // 2

    def body(v_ref, t_ref, o_ref):
        for cs in _chunks(w):
            o_ref[:, cs] = v_ref[:, cs].astype(F32) * _sigmoid(t_ref[:, cs].astype(F32))

    return rows.call(body, name, [rows.row(w, 0), rows.row(w, 1)], rows.row(w), jax.ShapeDtypeStruct((rows.t_all, w), F32))(pvg, pvg)


def glu_bwd(rows, pvg, du, name):
    w = pvg.shape[1] // 2

    def body(v_ref, t_ref, du_ref, o_ref):
        for cs in _chunks(w):
            s = _sigmoid(t_ref[:, cs].astype(F32))
            duv = du_ref[:, cs]
            o_ref[:, cs] = (duv * s).astype(BF16)
            o_ref[:, slice(w + cs.start, w + cs.stop)] = (duv * v_ref[:, cs].astype(F32) * s * (1.0 - s)).astype(BF16)

    return rows.call(body, name, [rows.row(w, 0), rows.row(w, 1), rows.row(w)], rows.row(2 * w),
                     jax.ShapeDtypeStruct((rows.t_all, 2 * w), BF16))(pvg, pvg, du)


def ln_silu(rows, u, g, b, name):
    w = u.shape[1]

    def body(u_ref, g_ref, b_ref, o_ref):
        uv = u_ref[...]
        xc = uv - jnp.mean(uv, axis=-1, keepdims=True)
        n = xc * lax.rsqrt(jnp.mean(xc * xc, axis=-1, keepdims=True) + EPS)
        o_ref[...] = _silu_and_grad(n * g_ref[...] + b_ref[...])[0].astype(BF16)

    return rows.call(body, name, [rows.row(w), rows.vec(w), rows.vec(w)], rows.row(w),
                     jax.ShapeDtypeStruct((rows.t_all, w), BF16))(u, g, b)


def ln_silu_bwd(rows, u, ds, g, b, name):
    w = u.shape[1]

    def body(u_ref, ds_ref, g_ref, b_ref, du_ref, dg_ref, db_ref):
        i = pl.program_id(0)
        uv, gv = u_ref[...], g_ref[...]
        xc = uv - jnp.mean(uv, axis=-1, keepdims=True)
        r = lax.rsqrt(jnp.mean(xc * xc, axis=-1, keepdims=True) + EPS)
        n = xc * r
        dy = ds_ref[...].astype(F32) * _silu_and_grad(n * gv + b_ref[...])[1]
        dn = dy * gv
        du_ref[...] = r * (dn - jnp.mean(dn, axis=-1, keepdims=True) - n * jnp.mean(dn * n, axis=-1, keepdims=True))
        _accumulate(dg_ref, _colsum(dy * n), i == 0)
        _accumulate(db_ref, _colsum(dy), i == 0)

    return rows.call(body, name, [rows.row(w), rows.row(w), rows.vec(w), rows.vec(w)], [rows.row(w), rows.vec(w), rows.vec(w)],
                     [jax.ShapeDtypeStruct((rows.t_all, w), F32), jax.ShapeDtypeStruct((1, w), F32),
                      jax.ShapeDtypeStruct((1, w), F32)])(u, ds, g, b)


def lru_merge(rows, h, gl, name):
    w = gl.shape[1]

    def body(h_ref, gl_ref, o_ref):
        for cs in _chunks(w):
            o_ref[:, cs] = ((h_ref[0, :, cs] + h_ref[1, :, cs]) * _gelu_and_grad(gl_ref[:, cs].astype(F32))[0]).astype(BF16)

    return rows.call(body, name, [pl.BlockSpec((2, rows.tm, w), lambda i: (0, i, 0)), rows.row(w)], rows.row(w),
                     jax.ShapeDtypeStruct((rows.t_all, w), BF16))(h, gl)


def lru_merge_bwd(rows, h, gl, dy, name):
    w = gl.shape[1]

    def body(h_ref, gl_ref, dy_ref, dh_ref, dgl_ref):
        for cs in _chunks(w):
            act, dact = _gelu_and_grad(gl_ref[:, cs].astype(F32))
            dyv = dy_ref[:, cs].astype(F32)
            dh_ref[:, cs] = dyv * act
            dgl_ref[:, cs] = (dyv * (h_ref[0, :, cs] + h_ref[1, :, cs]) * dact).astype(BF16)

    return rows.call(body, name, [pl.BlockSpec((2, rows.tm, w), lambda i: (0, i, 0)), rows.row(w), rows.row(w)],
                     [rows.row(w), rows.row(w)],
                     [jax.ShapeDtypeStruct((rows.t_all, w), F32), jax.ShapeDtypeStruct((rows.t_all, w), BF16)])(h, gl, dy)


def z_merge(rows, pgc, yc, yl, name):
    d = rows.d

    def body(gc_ref, gr_ref, yc_ref, yl_ref, o_ref):
        for cs in _chunks(d):
            o_ref[:, cs] = (_sigmoid(gc_ref[:, cs].astype(F32)) * yc_ref[:, cs].astype(F32)
                            + _sigmoid(gr_ref[:, cs].astype(F32)) * yl_ref[:, cs].astype(F32)).astype(BF16)

    return rows.call(body, name, [rows.row(d, 0), rows.row(d, 1), rows.row(d), rows.row(d)], rows.row(d),
                     jax.ShapeDtypeStruct((rows.t_all, d), BF16))(pgc, pgc, yc, yl)


def z_merge_bwd(rows, pgc, yc, yl, dz, name):
    d = rows.d

    def body(gc_ref, gr_ref, yc_ref, yl_ref, dz_ref, dyc_ref, dyl_ref, dp_ref):
        for cs in _chunks(d):
            sc, sr = _sigmoid(gc_ref[:, cs].astype(F32)), _sigmoid(gr_ref[:, cs].astype(F32))
            dzv = dz_ref[:, cs].astype(F32)
            dyc_ref[:, cs] = (dzv * sc).astype(BF16)
            dyl_ref[:, cs] = (dzv * sr).astype(BF16)
            dp_ref[:, cs] = (dzv * yc_ref[:, cs].astype(F32) * sc * (1.0 - sc)).astype(BF16)
            dp_ref[:, slice(d + cs.start, d + cs.stop)] = (dzv * yl_ref[:, cs].astype(F32) * sr * (1.0 - sr)).astype(BF16)

    return rows.call(body, name, [rows.row(d, 0), rows.row(d, 1), rows.row(d), rows.row(d), rows.row(d)],
                     [rows.row(d), rows.row(d), rows.row(2 * d)],
                     [jax.ShapeDtypeStruct((rows.t_all, d), BF16), jax.ShapeDtypeStruct((rows.t_all, d), BF16),
                      jax.ShapeDtypeStruct((rows.t_all, 2 * d), BF16)])(pgc, pgc, yc, yl, dz)


def col_sum(rows, x, name):
    w = x.shape[1]

    def body(x_ref, o_ref):
        i = pl.program_id(0)
        for cs in _chunks(w):
            _accumulate(o_ref.at[:, cs], _colsum(x_ref[:, cs].astype(F32)), i == 0)

    return rows.call(body, name, [rows.row(w)], rows.vec(w), jax.ShapeDtypeStruct((1, w), F32))(x)


def loss_head(rows, x, f, mv, g, target, name):
    d = rows.d

    def body(x_ref, f_ref, mv_ref, g_ref, t_ref, loss_ref, dx_ref, dg_ref):
        i = pl.program_id(0)
        valid = jnp.where(i < rows.n_lat, 1.0, 0.0)
        xv = x_ref[...] + 0.5 * mv_ref[8:9, :] * f_ref[...]
        gv = g_ref[...]
        r = lax.rsqrt(jnp.mean(xv * xv, axis=-1, keepdims=True) + EPS)
        n = xv * r
        err = (n * gv - t_ref[...]) * valid
        part = 0.5 * jnp.sum(jnp.mean(err * err, axis=-1, keepdims=True), axis=0, keepdims=True)
        _accumulate(loss_ref, jnp.broadcast_to(part, (1, LANES)), i == 0)
        dy = err * (1.0 / d)
        dn = dy * gv
        dx_ref[...] = r * (dn - n * jnp.mean(dn * n, axis=-1, keepdims=True))
        _accumulate(dg_ref, _colsum(dy * n), i == 0)

    target_spec = pl.BlockSpec((rows.tm, d), lambda i: (jnp.minimum(i, rows.n_lat - 1), 0))
    return rows.call(body, name, [rows.row(d), rows.row(d), rows.mod(), rows.vec(d), target_spec],
                     [rows.vec(LANES), rows.row(d), rows.vec(d)],
                     [jax.ShapeDtypeStruct((1, LANES), F32), jax.ShapeDtypeStruct((rows.t_all, d), F32),
                      jax.ShapeDtypeStruct((1, d), F32)])(x, f, mv, g, target)


def _extended(rows, i, prev_ref, cur_ref, next_ref, cs):
    prev = jnp.where(rows.seg_first(i), 0.0, prev_ref[:, cs].astype(F32))
    nxt = jnp.where(rows.seg_last(i), 0.0, next_ref[:, cs].astype(F32))
    return jnp.concatenate([prev, cur_ref[:, cs].astype(F32), nxt], axis=0)


class _Shifts:
    def __init__(self, ext, tm):
        self.ext, self.tm, self.rolled = ext, tm, {0: ext}

    def at(self, offset):
        residue = offset % SUBLANES
        if residue not in self.rolled:
            self.rolled[residue] = pltpu.roll(self.ext, self.ext.shape[0] - residue, 0)
        start = HALO + offset - residue
        return self.rolled[residue][start:start + self.tm]


def _pad_taps(w):
    k = w.shape[0]
    return jnp.pad(w, ((0, -k % SUBLANES), (0, 0)))


def dwconv(rows, u, w, b, pad_left, out_dtype, name):
    taps, width = w.shape
    wp = _pad_taps(w)

    def body(*refs):
        prev_ref, cur_ref, next_ref, w_ref = refs[:4]
        b_ref = refs[4] if b is not None else None
        o_ref = refs[-1]
        i = pl.program_id(0)
        for cs in _chunks(width, LANES):
            shifts = _Shifts(_extended(rows, i, prev_ref, cur_ref, next_ref, cs), rows.tm)
            acc = jnp.zeros((rows.tm, LANES), F32) if b_ref is None else jnp.broadcast_to(b_ref[:, cs], (rows.tm, LANES))
            for k in range(taps):
                acc = acc + w_ref[k:k + 1, cs] * shifts.at(k - pad_left)
            o_ref[:, cs] = acc.astype(out_dtype)

    ins = [u, u, u, wp] + ([b] if b is not None else [])
    in_specs = [rows.halo(width, -1), rows.row(width), rows.halo(width, 1), rows.full(wp.shape)] + ([rows.vec(width)] if b is not None else [])
    return rows.call(body, name, in_specs, rows.row(width), jax.ShapeDtypeStruct((rows.t_all, width), out_dtype))(*ins)


def dwconv_bwd_w(rows, u, dy, taps, pad_left, name):
    width = u.shape[1]
    taps_p = taps + (-taps % SUBLANES)

    def body(prev_ref, cur_ref, next_ref, dy_ref, dw_ref, db_ref):
        i = pl.program_id(0)
        tap_row = lax.broadcasted_iota(jnp.int32, (taps_p, LANES), 0)
        for cs in _chunks(width, LANES):
            shifts = _Shifts(_extended(rows, i, prev_ref, cur_ref, next_ref, cs), rows.tm)
            dyv = dy_ref[:, cs]
            total = jnp.zeros((taps_p, LANES), F32)
            for k in range(taps):
                total = total + jnp.where(tap_row == k, _colsum(dyv * shifts.at(k - pad_left)), 0.0)
            _accumulate(dw_ref.at[:, cs], total, i == 0)
            _accumulate(db_ref.at[:, cs], _colsum(dyv), i == 0)

    dw, db = rows.call(body, name, [rows.halo(width, -1), rows.row(width), rows.halo(width, 1), rows.row(width)],
                       [rows.full((taps_p, width)), rows.vec(width)],
                       [jax.ShapeDtypeStruct((taps_p, width), F32), jax.ShapeDtypeStruct((1, width), F32)])(u, u, u, dy)
    return dw[:taps], db


def _gate_values(xr, pre, lam, d):
    r = _sigmoid(pre[:, (2 * d) * LANES:(2 * d + 1) * LANES])
    ig = _sigmoid(pre[:, (2 * d + 1) * LANES:(2 * d + 2) * LANES])
    sp = _softplus(-lam[:, d * LANES:(d + 1) * LANES])
    log_a = -LRU_C * r * sp
    return r, ig, sp, jnp.exp(log_a), _neg_expm1(2.0 * log_a)


def lru_gates(rows, xr, wcat, bcat, lam, name):
    n_blk = wcat.shape[0]
    width = xr.shape[1]

    def body(xr_ref, w_ref, b_ref, lam_ref, a_ref, bb_ref):
        for h in range(n_blk):
            cs = slice(h * LANES, (h + 1) * LANES)
            xv = xr_ref[:, cs]
            pre = jnp.dot(xv.astype(BF16), w_ref[h], preferred_element_type=F32) + b_ref[h]
            for d in range(2):
                _, ig, _, a, one_minus_a2 = _gate_values(xv, pre, lam_ref[h], d)
                a_ref[d, :, cs] = a
                bb_ref[d, :, cs] = jnp.sqrt(one_minus_a2) * (ig * xv)

    both = pl.BlockSpec((2, rows.tm, width), lambda i: (0, i, 0))
    shape = jax.ShapeDtypeStruct((2, rows.t_all, width), F32)
    return rows.call(body, name, [rows.row(width), rows.full(wcat.shape), rows.full(bcat.shape), rows.full(lam.shape)],
                     [both, both], [shape, shape])(xr, wcat, bcat, lam)


def lru_gates_bwd(rows, xr, wcat, bcat, lam, da, dbb, name):
    n_blk = wcat.shape[0]
    width = xr.shape[1]

    def body(xr_ref, w_ref, b_ref, lam_ref, da_ref, dbb_ref, dxr_ref, dw_ref, db_ref, dlam_ref):
        i = pl.program_id(0)
        for h in range(n_blk):
            cs = slice(h * LANES, (h + 1) * LANES)
            xv = xr_ref[:, cs]
            xb = xv.astype(BF16)
            wv = w_ref[h]
            pre = jnp.dot(xb, wv, preferred_element_type=F32) + b_ref[h]
            dxr = jnp.zeros_like(xv)
            dpre, dlam = [], []
            for d in range(2):
                r, ig, sp, a, one_minus_a2 = _gate_values(xv, pre, lam_ref[h], d)
                inv_q = lax.rsqrt(one_minus_a2)
                dav, dbv = da_ref[d, :, cs], dbb_ref[d, :, cs]
                dbq = dbv * (one_minus_a2 * inv_q) * ig
                dlog_a = dav * a - dbv * (ig * xv) * ((a * a) * inv_q)
                dpre.append(dlog_a * (-LRU_C * sp) * (r - r * r))
                dpre.append(dbq * xv * (1.0 - ig))
                dxr = dxr + dbq
                dlam.append(_colsum(dlog_a * (-LRU_C * r)) * (-_sigmoid(-lam_ref[h][:, d * LANES:(d + 1) * LANES])))
            dpre = jnp.concatenate(dpre, axis=1)
            dpb = dpre.astype(BF16)
            dxr_ref[:, cs] = dxr + lax.dot_general(dpb, wv, (((1,), (1,)), ((), ())), preferred_element_type=F32)
            _accumulate(dw_ref.at[h], lax.dot_general(xb, dpb, (((0,), (0,)), ((), ())), preferred_element_type=F32), i == 0)
            _accumulate(db_ref.at[h], _colsum(dpre), i == 0)
            _accumulate(dlam_ref.at[h], jnp.concatenate(dlam, axis=1), i == 0)

    both = pl.BlockSpec((2, rows.tm, width), lambda i: (0, i, 0))
    return rows.call(
        body, name, [rows.row(width), rows.full(wcat.shape), rows.full(bcat.shape), rows.full(lam.shape), both, both],
        [rows.row(width), rows.full(wcat.shape), rows.full(bcat.shape), rows.full(lam.shape)],
        [jax.ShapeDtypeStruct((rows.t_all, width), F32), jax.ShapeDtypeStruct(wcat.shape, F32),
         jax.ShapeDtypeStruct(bcat.shape, F32), jax.ShapeDtypeStruct(lam.shape, F32)])(xr, wcat, bcat, lam, da, dbb)


def _tile_scan(a, b, reverse):
    n = a.shape[0]
    row = lax.broadcasted_iota(jnp.int32, a.shape, 0)
    k = 1
    while k < n:
        ok = (row < n - k) if reverse else (row >= k)
        shift = n - k if reverse else k
        b = b + a * jnp.where(ok, pltpu.roll(b, shift, 0), 0.0)
        a = a * jnp.where(ok, pltpu.roll(a, shift, 0), 1.0)
        k *= 2
    return a, b


def _chain_scan(a, b, h_in, reverse):
    n_blocks = a.shape[0] // SUBLANES
    out = [None] * n_blocks
    state = h_in
    for j in (range(n_blocks - 1, -1, -1) if reverse else range(n_blocks)):
        rows_j = slice(j * SUBLANES, (j + 1) * SUBLANES)
        cum, h0 = _tile_scan(a[rows_j], b[rows_j], reverse)
        out[j] = h0 + cum * state
        state = out[j][0:1] if reverse else out[j][SUBLANES - 1:SUBLANES]
    return jnp.concatenate(out, axis=0), state


def _neighbour(v, edge, reverse):
    n = v.shape[0]
    row = lax.broadcasted_iota(jnp.int32, v.shape, 0)
    if reverse:
        return jnp.where(row < n - 1, pltpu.roll(v, n - 1, 0), edge)
    return jnp.where(row >= 1, pltpu.roll(v, 1, 0), edge)


def _scan_call(rows, body, name, ins, in_specs, n_out, width, adjoint):
    n_all, n_lat = rows.n_all, rows.n_lat

    def tile(d, s):
        s = n_all - 1 - s if adjoint else s
        return jnp.where(d == 0, (s + n_lat) % n_all, n_all - 1 - s)

    def per_dir(d, s):
        return (d, tile(d, s), 0)

    specs = [pl.BlockSpec((None, rows.tm, width), per_dir) if kind == "dir" else
             pl.BlockSpec((rows.tm, width), lambda d, s: (tile(d, s), 0)) for kind in in_specs]
    shape = jax.ShapeDtypeStruct((2, rows.t_all, width), F32)
    return pl.pallas_call(
        body, name=name, grid=(2, n_all), in_specs=specs, out_specs=[pl.BlockSpec((None, rows.tm, width), per_dir)] * n_out,
        out_shape=[shape] * n_out, scratch_shapes=[pltpu.VMEM((SUBLANES, width), F32)],
        compiler_params=_params("arbitrary", "arbitrary"))(*ins)


def lru_scan(rows, a, bb, name):
    width = a.shape[2]
    n = rows.tm

    def body(a_ref, bb_ref, h_ref, hp_ref, carry):
        d, s = pl.program_id(0), pl.program_id(1)

        @pl.when(s == 0)
        def _():
            carry[...] = jnp.zeros_like(carry)

        def run(reverse):
            for cs in _chunks(width, LANES):
                h_in = carry[0:1, cs]
                h, carry[0:1, cs] = _chain_scan(a_ref[:, cs], bb_ref[:, cs], h_in, reverse)
                h_ref[:, cs] = h
                hp_ref[:, cs] = _neighbour(h, h_in, reverse)

        pl.when(d == 0)(lambda: run(False))
        pl.when(d == 1)(lambda: run(True))

    return _scan_call(rows, body, name, [a, bb], ["dir", "dir"], 2, width, adjoint=False)


def lru_scan_bwd(rows, dh, a, hp, name):
    width = a.shape[2]
    n = rows.tm

    def body(dh_ref, a_ref, hp_ref, da_ref, dbb_ref, carry):
        d, s = pl.program_id(0), pl.program_id(1)

        @pl.when(s == 0)
        def _():
            carry[...] = jnp.zeros_like(carry)

        def run(reverse):
            for cs in _chunks(width, LANES):
                av = a_ref[:, cs]
                g, _ = _chain_scan(_neighbour(av, 1.0, reverse), dh_ref[:, cs], carry[0:1, cs], reverse)
                da_ref[:, cs] = g * hp_ref[:, cs]
                dbb_ref[:, cs] = g
                carry[0:1, cs] = (av * g)[0:1] if reverse else (av * g)[n - 1:n]

        pl.when(d == 0)(lambda: run(True))
        pl.when(d == 1)(lambda: run(False))

    return _scan_call(rows, body, name, [dh, a, hp], ["shared", "dir", "dir"], 2, width, adjoint=True)


def mod_matmul(c9, w_shard, b_shard, name):
    n = w_shard.shape[1]
    tn = _pick(n, 768)

    def body(c_ref, w_ref, b_ref, o_ref):
        act = _silu_and_grad(c_ref[...])[0]
        o_ref[...] = jnp.dot(act, w_ref[...], preferred_element_type=F32, precision=lax.Precision.HIGHEST) + b_ref[...]

    return pl.pallas_call(
        body, name=name, grid=(n // tn,),
        in_specs=[pl.BlockSpec(c9.shape, lambda j: (0, 0)), pl.BlockSpec((w_shard.shape[0], tn), lambda j: (0, j)),
                  pl.BlockSpec((1, tn), lambda j: (0, j))],
        out_specs=pl.BlockSpec((c9.shape[0], tn), lambda j: (0, j)), out_shape=jax.ShapeDtypeStruct((c9.shape[0], n), F32),
        compiler_params=_params("arbitrary"))(c9, w_shard, b_shard)


def mod_matmul_bwd(c9, d9, w_shard, name):
    n = w_shard.shape[1]
    tn = _pick(n, 768)
    steps = n // tn

    def body(c_ref, d_ref, w_ref, gw_ref, gc_ref):
        j = pl.program_id(0)
        act, dact = _silu_and_grad(c_ref[...])
        dv = d_ref[...]
        gw_ref[...] = lax.dot_general(act, dv, (((0,), (0,)), ((), ())), preferred_element_type=F32, precision=lax.Precision.HIGHEST)
        part = lax.dot_general(dv, w_ref[...], (((1,), (1,)), ((), ())), preferred_element_type=F32, precision=lax.Precision.HIGHEST)
        _accumulate(gc_ref, part * dact, j == 0)

    return pl.pallas_call(
        body, name=name, grid=(steps,),
        in_specs=[pl.BlockSpec(c9.shape, lambda j: (0, 0)), pl.BlockSpec((c9.shape[0], tn), lambda j: (0, j)),
                  pl.BlockSpec((w_shard.shape[0], tn), lambda j: (0, j))],
        out_specs=[pl.BlockSpec((w_shard.shape[0], tn), lambda j: (0, j)), pl.BlockSpec(c9.shape, lambda j: (0, 0))],
        out_shape=[jax.ShapeDtypeStruct(w_shard.shape, F32), jax.ShapeDtypeStruct(c9.shape, F32)],
        compiler_params=_params("arbitrary"))(c9, d9, w_shard)


def _row_tile(n_rows, n_cols):
    return _pick(n_rows, max(2 * SUBLANES, (256 * 1024) // n_cols), 2 * SUBLANES)


def sum_parts(parts, name):
    n, n_rows, n_cols = parts.shape
    tr = _row_tile(n_rows, n_cols)

    def body(p_ref, o_ref):
        total = p_ref[0].astype(F32)
        for k in range(1, n):
            total = total + p_ref[k].astype(F32)
        o_ref[...] = total

    return pl.pallas_call(
        body, name=name, grid=(n_rows // tr,), in_specs=[pl.BlockSpec((n, tr, n_cols), lambda i: (0, i, 0))],
        out_specs=pl.BlockSpec((tr, n_cols), lambda i: (i, 0)), out_shape=jax.ShapeDtypeStruct((n_rows, n_cols), F32),
        compiler_params=_params("arbitrary"))(parts)


def _adamw_update(g, w_ref, m_ref, v_ref, g_ref, d_ref, m2_ref, v2_ref):
    m2 = ADAM_B1 * m_ref[...] + (1.0 - ADAM_B1) * g
    v2 = ADAM_B2 * v_ref[...] + (1.0 - ADAM_B2) * (g * g)
    m_hat = m2 / (1.0 - ADAM_B1 ** ADAM_STEP)
    v_hat = v2 / (1.0 - ADAM_B2 ** ADAM_STEP)
    g_ref[...] = g
    d_ref[...] = -ADAM_LR * (m_hat / (jnp.sqrt(v_hat) + ADAM_EPS) + ADAM_WD * w_ref[...])
    m2_ref[...] = m2
    v2_ref[...] = v2


def adamw(parts, w, m, v, name):
    n, n_rows, n_cols = parts.shape
    tr = _row_tile(n_rows, n_cols)

    def body(p_ref, *refs):
        g = p_ref[0].astype(F32)
        for k in range(1, n):
            g = g + p_ref[k].astype(F32)
        _adamw_update(g, *refs)

    blk = pl.BlockSpec((tr, n_cols), lambda i: (i, 0))
    shape = jax.ShapeDtypeStruct((n_rows, n_cols), F32)
    return pl.pallas_call(
        body, name=name, grid=(n_rows // tr,), in_specs=[pl.BlockSpec((n, tr, n_cols), lambda i: (0, i, 0)), blk, blk, blk],
        out_specs=[blk] * 4, out_shape=[shape] * 4, compiler_params=_params("arbitrary"))(parts, w, m, v)


def adamw_pair(part, w, m, v, name):
    n_rows, n_cols = part.shape
    tr = _row_tile(n_rows, n_cols)
    steps = n_rows // tr

    def body(p_ref, w_ref, m_ref, v_ref, g_ref, d_ref, m2_ref, v2_ref, inbox, send_sems, recv_sems, credits):
        i = pl.program_id(0)
        slot = i % 2
        sibling = (lax.axis_index("x"), lax.axis_index("y"), 1 - lax.axis_index("c"))

        @pl.when(i >= 2)
        def _():
            pl.semaphore_wait(credits.at[slot], 1)

        copy = pltpu.make_async_remote_copy(src_ref=p_ref, dst_ref=inbox.at[slot], send_sem=send_sems.at[slot],
                                            recv_sem=recv_sems.at[slot], device_id=sibling, device_id_type=MESH)
        copy.start()
        copy.wait_recv()
        copy.wait_send()
        _adamw_update(p_ref[...] + inbox[slot], w_ref, m_ref, v_ref, g_ref, d_ref, m2_ref, v2_ref)

        @pl.when(i + 2 < steps)
        def _():
            pl.semaphore_signal(credits.at[slot], inc=1, device_id=sibling, device_id_type=MESH)

    blk = pl.BlockSpec((tr, n_cols), lambda i: (i, 0))
    shape = jax.ShapeDtypeStruct((n_rows, n_cols), F32)
    return pl.pallas_call(
        body, name=name, grid=(steps,), in_specs=[blk] * 4, out_specs=[blk] * 4, out_shape=[shape] * 4,
        scratch_shapes=[pltpu.VMEM((2, tr, n_cols), F32), pltpu.SemaphoreType.DMA((2,)), pltpu.SemaphoreType.DMA((2,)),
                        pltpu.SemaphoreType.REGULAR((2,))],
        compiler_params=_params("arbitrary"))(part, w, m, v)


def _pack(arrays, row_multiple=SUBLANES):
    flat = [jnp.pad(a.reshape(-1), (0, -a.size % LANES)) for a in arrays]
    buf = jnp.concatenate(flat)
    buf = jnp.pad(buf, (0, -buf.size % (row_multiple * LANES)))
    return buf.reshape(-1, LANES)


def _unpack(buf, shapes):
    lead = buf.shape[:-2]
    flat = buf.reshape(lead + (-1,))
    out, pos = [], 0
    for shape in shapes:
        size = 1
        for s in shape:
            size *= s
        out.append(flat[..., pos:pos + size].reshape(lead + tuple(shape)))
        pos += size + (-size % LANES)
    return out


def _pos_tables(seq_len, dim, tile_rows):
    grid_rows = seq_len // GRID_W
    per_tile = tile_rows // GRID_W
    q = dim // 4
    omega = 1.0 / (10000.0 ** (jnp.arange(q, dtype=F32) / q))
    er = jnp.arange(grid_rows).astype(F32)[:, None] * omega
    ec = jnp.arange(GRID_W).astype(F32)[:, None] * omega
    by_row = jnp.concatenate([jnp.sin(er), jnp.cos(er)], axis=-1).reshape(grid_rows // per_tile, per_tile, 2 * q)
    by_row = jnp.pad(by_row, ((0, 0), (0, SUBLANES - per_tile), (0, 0))).reshape(-1, 2 * q)
    return by_row, jnp.concatenate([jnp.sin(ec), jnp.cos(ec)], axis=-1)


def _cols_from_shards(g):
    return jnp.transpose(g, (1, 0, 2)).reshape(g.shape[1], -1)


def _shards_from_cols(w, n_shards=4):
    k, n = w.shape
    return jnp.transpose(w.reshape(k, n_shards, n // n_shards), (1, 0, 2))


def kernel(x, c, ctx, c_ctx, w_mod, b_mod, g_n1, w_ffn1_up, w_ffn1_down, g_n2, w_in, b_in, w_dw, b_dw, g_ln, b_ln, w_conf_out, w_lru_conv, b_lru_conv, w_rec_gate, b_rec_gate, w_in_gate, b_in_gate, lru_lambda, w_lru_out, w_out, g_n3, w_ffn2_up, w_ffn2_down, g_final, loss_target, m_c_ctx, m_w_mod, m_b_mod, m_g_n1, m_w_ffn1_up, m_w_ffn1_down, m_g_n2, m_w_in, m_b_in, m_w_dw, m_b_dw, m_g_ln, m_b_ln, m_w_conf_out, m_w_lru_conv, m_b_lru_conv, m_w_rec_gate, m_b_rec_gate, m_w_in_gate, m_b_in_gate, m_lru_lambda, m_w_lru_out, m_w_out, m_g_n3, m_w_ffn2_up, m_w_ffn2_down, m_g_final, v_c_ctx, v_w_mod, v_b_mod, v_g_n1, v_w_ffn1_up, v_w_ffn1_down, v_g_n2, v_w_in, v_b_in, v_w_dw, v_b_dw, v_g_ln, v_b_ln, v_w_conf_out, v_w_lru_conv, v_b_lru_conv, v_w_rec_gate, v_b_rec_gate, v_w_in_gate, v_b_in_gate, v_lru_lambda, v_w_lru_out, v_w_out, v_g_n3, v_w_ffn2_up, v_w_ffn2_down, v_g_final):
    given = dict(locals())
    wt = {n: given[n] for n in W_NAMES}
    mom = {n: given["m_" + n] for n in W_NAMES}
    var = {n: given["v_" + n] for n in W_NAMES}

    def flat2(a):
        if a.ndim == 1:
            return a.reshape(1, -1)
        a = a[0]
        return a if a.ndim == 2 else a.reshape(-1, a.shape[-1])

    t_lat, d = x.shape[1], x.shape[2]
    t_ctx = ctx.shape[1]
    rows = Rows(t_lat, t_ctx, d)
    xi, yi, ci = lax.axis_index("x"), lax.axis_index("y"), lax.axis_index("c")
    me, chip = 4 * xi + 2 * yi + ci, 2 * xi + yi
    lru_w = b_lru_conv.shape[-1]
    n_blk = lru_w // LANES
    taps = w_dw.shape[1]
    lru_taps = w_lru_conv.shape[1]
    ffn = w_ffn1_down.shape[1] * 4

    small_shapes = [(1, d)] + [flat2(wt[n]).shape for n in SMALL_SHARDED]
    small_all = exchange("all", [_pack([c] + [flat2(wt[n]) for n in SMALL_SHARDED])], False, "gather_small")[0]
    small_all = _unpack(small_all, small_shapes)
    c_all = small_all[0][:, 0, :]
    full = {n: jnp.concatenate([a[0], a[2], a[4], a[6]], axis=-1) for n, a in zip(SMALL_SHARDED, small_all[1:])}
    big = COL_SHARDED + ROW_SHARDED
    ffn1_names = ['w_ffn1_up', 'w_ffn1_down']
    ffn2_names = ['w_ffn2_up', 'w_ffn2_down']
    mixer_names = [n for n in big if n not in ffn1_names + ffn2_names]
    wb = {}

    def take_weights(names, gathered):
        for n, g in zip(names, gathered):
            wb[n] = _cols_from_shards(g) if n in COL_SHARDED else g.reshape(-1, g.shape[-1])

    def shards_after(names, earlier):
        return lax.optimization_barrier(([wt[n][0].astype(BF16) for n in names], earlier))[0]

    c9 = jnp.concatenate([c_all, c_ctx.reshape(1, d), jnp.zeros((7, d), F32)], axis=0)
    mod_cols = mod_matmul(c9, w_mod[0], lax.dynamic_slice_in_dim(b_mod, chip * w_mod.shape[2], w_mod.shape[2], axis=1), "mod_matmul")
    mod_all = exchange("all", [mod_cols], False, "gather_mod")[0]
    mod9 = jnp.concatenate([mod_all[0], mod_all[2], mod_all[4], mod_all[6]], axis=-1)
    mv = jnp.stack([lax.dynamic_index_in_dim(mod9, me, 0, keepdims=False).reshape(N_MOD, d), mod9[8].reshape(N_MOD, d)])

    gathered_ffn1 = exchange("chips", shards_after(ffn1_names, mod_all), False, "gather_weights_ffn1")
    take_weights(ffn1_names, gathered_ffn1)
    gathered_mixer = exchange_by_sequencer(shards_after(mixer_names, gathered_ffn1), False, "gather_weights_mixer", 1)

    row_tab, col_tab = _pos_tables(t_lat, d, rows.tm)

    def ffn_fwd(tag, h, w_up, w_down):
        g, u, act = matmul_swiglu(h, w_up, f"{tag}_up")
        return (g, u), act, matmul(act, w_down, "nn", F32, f"{tag}_down")

    x0, h1 = embed_norm_mod(rows, x[0], ctx[0], row_tab, col_tab, g_n1, mv, 0, 1, "norm1")
    gu1, act1, f1 = ffn_fwd("ffn1", h1, wb["w_ffn1_up"], wb["w_ffn1_down"])
    gathered_mixer, f1 = lax.optimization_barrier((gathered_mixer, f1))
    take_weights(mixer_names, gathered_mixer)
    gathered_ffn2 = exchange_by_sequencer(shards_after(ffn2_names, gathered_mixer), False, "gather_weights_ffn2", 5)
    x1, h2 = norm_mod(rows, x0, g_n2, mv, 3, 4, "norm2", resid=(f1, 2, 0.5))

    conf_w = w_conf_out.shape[1] * 4
    col_groups = [(0, 2 * conf_w), (2 * conf_w, lru_w), (2 * conf_w + lru_w, lru_w), (2 * conf_w + 2 * lru_w, 2 * d)]
    pvg, ux, gl, pgc = [matmul(h2, wb["w_in"][:, s:s + n], "nn", BF16, f"in_proj{k}", bias=b_in[:, s:s + n])
                        for k, (s, n) in enumerate(col_groups)]
    u = glu(rows, pvg, "glu")
    u2 = dwconv(rows, u, full["w_dw"], b_dw, taps // 2, F32, "conf_conv")
    s_act = ln_silu(rows, u2, g_ln, b_ln, "conf_ln")
    yc = matmul(s_act, wb["w_conf_out"], "nn", BF16, "conf_out")

    xr = dwconv(rows, ux, full["w_lru_conv"], b_lru_conv, LRU_PAD_LEFT, F32, "lru_conv")
    w_rec, w_ing = w_rec_gate[0].astype(BF16), w_in_gate[0].astype(BF16)
    wcat = jnp.concatenate([w_rec[0], w_ing[0], w_rec[1], w_ing[1]], axis=-1)

    def per_block(a):
        return jnp.transpose(a.reshape(2, n_blk, LANES), (1, 0, 2)).reshape(n_blk, 1, 2 * LANES)

    b_rec, b_ing = full["b_rec_gate"].reshape(2, n_blk, LANES), full["b_in_gate"].reshape(2, n_blk, LANES)
    bcat = jnp.concatenate([b_rec[0], b_ing[0], b_rec[1], b_ing[1]], axis=-1).reshape(n_blk, 1, 4 * LANES)
    lam = per_block(full["lru_lambda"])
    a_gate, b_gate = lru_gates(rows, xr, wcat, bcat, lam, "lru_gates")
    h_scan, h_prev = lru_scan(rows, a_gate, b_gate, "lru_scan")
    yl_in = lru_merge(rows, h_scan, gl, "lru_merge")
    yl = matmul(yl_in, wb["w_lru_out"], "nn", BF16, "lru_out")
    z = z_merge(rows, pgc, yc, yl, "z_merge")
    y = matmul(z, wb["w_out"], "nn", F32, "mix_out")

    gathered_ffn2, y = lax.optimization_barrier((gathered_ffn2, y))
    take_weights(ffn2_names, gathered_ffn2)
    x2, h3 = norm_mod(rows, x1, g_n3, mv, 6, 7, "norm3", resid=(y, 5, 1.0))
    gu2, act2, f2 = ffn_fwd("ffn2", h3, wb["w_ffn2_up"], wb["w_ffn2_down"])
    loss_part, dx3, dg_final = loss_head(rows, x2, f2, mv, g_final.reshape(1, d), loss_target[0], "loss_head")

    grads = {"g_final": dg_final}
    dmv = [None] * N_MOD

    def ffn_bwd(tag, dxn, x_prev, h, gu, act, f, w_up, w_down, g, ks, collective_id, settle_first=(), latent_only=False):
        k_shift, k_scale, k_gate = ks
        df, dmv[k_gate] = gate_bwd(rows, dxn, f, mv, k_gate, 0.5, f"{tag}_gate_bwd")
        grads[f"w_{tag}_down"] = matmul(act, df, "tn", BF16, f"{tag}_down_dw")
        dact = matmul(df, w_down, "nt", BF16, f"{tag}_down_dx")
        if settle_first:
            dact = settle(settle_first, dact)
        dgu = swiglu_bwd(rows, gu[0], gu[1], dact, f"{tag}_act_bwd")
        grads[f"w_{tag}_up"] = matmul(h, dgu, "tn", BF16, f"{tag}_up_dw")
        dgu = scatter_behind([f"w_{tag}_up", f"w_{tag}_down"], tag, collective_id, dgu)
        dh = matmul(dgu, w_up, "nt", F32, f"{tag}_up_dx")
        dx, dg, dmv[k_shift], dmv[k_scale] = norm_mod_bwd(rows, x_prev, dh, dxn, g, mv, k_shift, k_scale, f"{tag}_norm_bwd",
                                                          latent_only=latent_only)
        return dx, dg

    def grad_pieces(names):
        return [(_shards_from_cols(grads[n]) if n in COL_SHARDED else grads[n].reshape(4, -1, grads[n].shape[-1])).astype(BF16)
                for n in names]

    from_chips = {}

    def scatter_behind(names, tag, collective_id, carry):
        ready, carry = lax.optimization_barrier(([grads[n] for n in names], carry))
        grads.update(zip(names, ready))
        from_chips.update(zip(names, exchange_by_sequencer(grad_pieces(names), True, f"scatter_grads_{tag}", collective_id)))
        return carry

    def settle(names, carry):
        landed, carry = lax.optimization_barrier(([from_chips[n] for n in names], carry))
        from_chips.update(zip(names, landed))
        return carry

    dx2, grads["g_n3"] = ffn_bwd("ffn2", dx3, x2, h3, gu2, act2, f2, wb["w_ffn2_up"], wb["w_ffn2_down"], g_n3, (6, 7, 8), 2)

    dy, dmv[5] = gate_bwd(rows, dx2, y, mv, 5, 1.0, "mix_gate_bwd")
    grads["w_out"] = matmul(z, dy, "tn", BF16, "mix_out_dw")
    dz = matmul(dy, wb["w_out"], "nt", BF16, "mix_out_dx")
    dyc, dyl, dpgc = z_merge_bwd(rows, pgc, yc, yl, dz, "z_merge_bwd")

    grads["w_conf_out"] = matmul(s_act, dyc, "tn", BF16, "conf_out_dw")
    ds_act = matmul(dyc, wb["w_conf_out"], "nt", BF16, "conf_out_dx")
    du2, grads["g_ln"], grads["b_ln"] = ln_silu_bwd(rows, u2, ds_act, g_ln, b_ln, "conf_ln_bwd")
    grads["w_dw"], grads["b_dw"] = dwconv_bwd_w(rows, u, du2, taps, taps // 2, "conf_conv_dw")
    du = dwconv(rows, du2, full["w_dw"][::-1], None, taps - 1 - taps // 2, F32, "conf_conv_dx")
    dpvg = glu_bwd(rows, pvg, du, "glu_bwd")

    grads["w_lru_out"] = matmul(yl_in, dyl, "tn", BF16, "lru_out_dw")
    dyl = scatter_behind(['w_out', 'w_conf_out', 'w_lru_out'], "mixer", 3, settle(ffn2_names, dyl))
    dyl_in = matmul(dyl, wb["w_lru_out"], "nt", BF16, "lru_out_dx")
    dh_sum, dgl = lru_merge_bwd(rows, h_scan, gl, dyl_in, "lru_merge_bwd")
    da_gate, db_gate = lru_scan_bwd(rows, dh_sum, a_gate, h_prev, "lru_scan_bwd")
    dxr, dwcat, dbcat, dlam = lru_gates_bwd(rows, xr, wcat, bcat, lam, da_gate, db_gate, "lru_gates_bwd")
    grads["w_rec_gate"] = jnp.stack([dwcat[:, :, 0:LANES], dwcat[:, :, 2 * LANES:3 * LANES]])
    grads["w_in_gate"] = jnp.stack([dwcat[:, :, LANES:2 * LANES], dwcat[:, :, 3 * LANES:4 * LANES]])
    dbcat = dbcat.reshape(n_blk, 4, LANES)
    grads["b_rec_gate"] = jnp.stack([dbcat[:, 0], dbcat[:, 2]]).reshape(2, lru_w)
    grads["b_in_gate"] = jnp.stack([dbcat[:, 1], dbcat[:, 3]]).reshape(2, lru_w)
    grads["lru_lambda"] = jnp.transpose(dlam.reshape(n_blk, 2, LANES), (1, 0, 2)).reshape(2, lru_w)
    grads["w_lru_conv"], grads["b_lru_conv"] = dwconv_bwd_w(rows, ux, dxr, lru_taps, LRU_PAD_LEFT, "lru_conv_dw")
    dux = dwconv(rows, dxr, full["w_lru_conv"][::-1], None, lru_taps - 1 - LRU_PAD_LEFT, BF16, "lru_conv_dx")

    dproj = jnp.concatenate([dpvg, dux, dgl, dpgc], axis=1)
    grads["b_in"] = col_sum(rows, dproj, "in_proj_db")
    grads["w_in"] = matmul(h2, dproj, "tn", BF16, "in_proj_dw")
    dproj = scatter_behind(['w_in'], "in_proj", 4, settle(['w_out', 'w_conf_out', 'w_lru_out'], dproj))
    dh2 = matmul(dproj, wb["w_in"], "nt", F32, "in_proj_dx")
    dx1, grads["g_n2"], dmv[3], dmv[4] = norm_mod_bwd(rows, x1, dh2, dx2, g_n2, mv, 3, 4, "norm2_bwd")

    dx0, grads["g_n1"] = ffn_bwd("ffn1", dx1, x0, h1, gu1, act1, f1, wb["w_ffn1_up"], wb["w_ffn1_down"], g_n1, (0, 1, 2), 6, settle_first=['w_in'], latent_only=True)
    grad_x = dx0[None]

    chip_sums = [sum_parts(from_chips[n].reshape(4, -1, from_chips[n].shape[-1]), f"chip_sum_{n}") for n in big]
    out = {}
    for n, p in zip(big, chip_sums):
        out[n] = adamw_pair(p, flat2(wt[n]), flat2(mom[n]), flat2(var[n]), f"adamw_{n}")

    dmod = jnp.concatenate(dmv, axis=1)
    small_names = ['g_n1', 'g_n2', 'b_in', 'b_dw', 'g_ln', 'b_ln', 'b_lru_conv', 'g_n3', 'g_final'] + SMALL_SHARDED
    gate_names = ['w_rec_gate', 'w_in_gate']
    small_list = [loss_part] + [grads[n] for n in small_names] + [dmod[0], dmod[1]]
    small_buf = _pack(small_list, 8 * 2 * SUBLANES)
    eighths = [small_buf.reshape(8, -1, LANES)] + [grads[n].reshape(8, -1, LANES) for n in gate_names]
    *parts, dmod_all = exchange("all", eighths + [_pack([dmod[0]])], [True] * len(eighths) + [False], "scatter_small_grads")
    sums = exchange("all", [sum_parts(p, f"sum_small_grads{k}") for k, p in enumerate(parts)], False, "gather_small_grads")
    small_sum = _unpack(sums[0].reshape(small_buf.shape), [a.shape for a in small_list])
    loss = small_sum[0][0, 0]
    total = dict(zip(small_names, small_sum[1:]))
    total.update({n: s.reshape(grads[n].shape) for n, s in zip(gate_names, sums[1:])})
    dmod_all = _unpack(dmod_all, [dmod[0].shape])[0].reshape(8, N_MOD * d)
    dmc = small_sum[-1].reshape(1, N_MOD * d)
    total["b_mod"] = small_sum[-2].reshape(1, N_MOD * d) + dmc

    d9 = jnp.concatenate([dmod_all, dmc, jnp.zeros((7, N_MOD * d), F32)], axis=0)
    d9_cols = lax.dynamic_slice_in_dim(d9, chip * w_mod.shape[2], w_mod.shape[2], axis=1)
    g_wmod, dc9 = mod_matmul_bwd(c9, d9_cols, w_mod[0], "mod_matmul_bwd")
    dc_all = exchange("all", [dc9[8:16]], False, "gather_dc")[0]
    total["c_ctx"] = sum_parts(jnp.stack([dc_all[0], dc_all[2], dc_all[4], dc_all[6]]), "sum_dc")[0:1]
    out["w_mod"] = adamw(g_wmod[None], w_mod[0], m_w_mod[0], v_w_mod[0], "adamw_w_mod")

    for n in gate_names:
        out[n] = adamw(total[n].reshape(1, -1, LANES), flat2(wt[n]), flat2(mom[n]), flat2(var[n]), f"adamw_{n}")

    small_all_names = [n for n in W_NAMES if n not in out]
    g_local, shapes = [], []
    for n in small_all_names:
        g = total[n].reshape(flat2(wt[n]).shape[:-1] + (-1,)) if n in SMALL_SHARDED else total[n].reshape(flat2(wt[n]).shape)
        if n in SMALL_SHARDED:
            width = wt[n].shape[-1]
            g = lax.dynamic_slice_in_dim(g, chip * width, width, axis=1)
        g_local.append(g)
        shapes.append(g.shape)
    packed = adamw(_pack(g_local)[None], _pack([flat2(wt[n]) for n in small_all_names]),
                   _pack([flat2(mom[n]) for n in small_all_names]), _pack([flat2(var[n]) for n in small_all_names]), "adamw_small")
    for k, n in enumerate(small_all_names):
        out[n] = [_unpack(p, shapes)[k] for p in packed]

    results = [loss, grad_x]
    for k in range(4):
        results += [out[n][k].reshape(wt[n].shape) for n in W_NAMES]
    return tuple(results)
```

```python
import functools

import jax
import jax.numpy as jnp
from jax import lax
from jax.experimental import pallas as pl
from jax.experimental.pallas import tpu as pltpu
from jax.experimental.pallas import tpu_sc as plsc

F32 = jnp.float32
BF16 = jnp.bfloat16
MESH = pl.DeviceIdType.MESH

EPS = 1e-6
GRID_W = 64
N_MOD = 9
LRU_C = 8.0
LRU_PAD_LEFT = 2
ADAM_LR, ADAM_B1, ADAM_B2, ADAM_EPS, ADAM_WD, ADAM_STEP = 0.001, 0.9, 0.999, 1e-08, 0.01, 10

LANES = 128
SUBLANES = 8
HALO = 16
VMEM_LIMIT = 56 * 1024 * 1024

W_NAMES = ['c_ctx', 'w_mod', 'b_mod', 'g_n1', 'w_ffn1_up', 'w_ffn1_down', 'g_n2', 'w_in', 'b_in', 'w_dw', 'b_dw',
           'g_ln', 'b_ln', 'w_conf_out', 'w_lru_conv', 'b_lru_conv', 'w_rec_gate', 'b_rec_gate', 'w_in_gate',
           'b_in_gate', 'lru_lambda', 'w_lru_out', 'w_out', 'g_n3', 'w_ffn2_up', 'w_ffn2_down', 'g_final']
COL_SHARDED = ['w_ffn1_up', 'w_in', 'w_ffn2_up']
ROW_SHARDED = ['w_ffn1_down', 'w_conf_out', 'w_lru_out', 'w_out', 'w_ffn2_down']
SMALL_SHARDED = ['w_dw', 'w_lru_conv', 'b_rec_gate', 'b_in_gate', 'lru_lambda']


def _params(*semantics):
    return pltpu.CompilerParams(dimension_semantics=semantics, vmem_limit_bytes=VMEM_LIMIT)


def _pick(n, target, mult=LANES):
    best = None
    for t in range(mult, min(n, target) + 1, mult):
        if n % t == 0:
            best = t
    return best or n


def _chunks(width, target=512):
    w = _pick(width, target)
    return [slice(s, s + w) for s in range(0, width, w)]


def _sigmoid(x):
    return jax.nn.sigmoid(x)


def _silu_and_grad(x):
    s = _sigmoid(x)
    return x * s, s * (1.0 + x * (1.0 - s))


_GELU_K = 0.7978845608028654


def _gelu_and_grad(x):
    t = jnp.tanh(_GELU_K * (x + 0.044715 * (x * x * x)))
    return 0.5 * x * (1.0 + t), 0.5 * (1.0 + t) + 0.5 * x * (1.0 - t * t) * _GELU_K * (1.0 + 3 * 0.044715 * x * x)


def _neg_expm1(z):
    series = -(z * (1.0 + z * (0.5 + z * (1.0 / 6.0))))
    return jnp.where(z > -0.01, series, 1.0 - jnp.exp(z))


def _softplus(x):
    return jnp.maximum(x, 0.0) + jnp.log1p(jnp.exp(-jnp.abs(x)))


def _accumulate(ref, value, first):
    @pl.when(first)
    def _():
        ref[...] = value

    @pl.when(jnp.logical_not(first))
    def _():
        ref[...] += value


def _colsum(x):
    return jnp.sum(x, axis=0, keepdims=True)


class Rows:
    def __init__(self, t_lat, t_ctx, d_model):
        self.tm = _pick(t_ctx, 256, HALO)
        assert t_lat % self.tm == 0 and t_ctx % self.tm == 0
        self.n_lat = t_lat // self.tm
        self.n_all = (t_lat + t_ctx) // self.tm
        self.t_all = t_lat + t_ctx
        self.d = d_model

    def seg(self, i):
        return jnp.where(i >= self.n_lat, 1, 0)

    def seg_first(self, i):
        return jnp.logical_or(i == 0, i == self.n_lat)

    def seg_last(self, i):
        return jnp.logical_or(i == self.n_lat - 1, i == self.n_all - 1)

    def row(self, width, col=0):
        return pl.BlockSpec((self.tm, width), lambda i: (i, col))

    def vec(self, width):
        return pl.BlockSpec((1, width), lambda i: (0, 0))

    def full(self, shape):
        return pl.BlockSpec(shape, lambda i: (0,) * len(shape))

    def mod(self):
        return pl.BlockSpec((None, N_MOD, self.d), lambda i: (self.seg(i), 0, 0))

    def segsum(self):
        return pl.BlockSpec((None, 1, self.d), lambda i: (self.seg(i), 0, 0))

    def segsum_shape(self):
        return jax.ShapeDtypeStruct((2, 1, self.d), F32)

    def halo(self, width, side):
        per = self.tm // HALO
        last = self.t_all // HALO - 1
        if side < 0:
            return pl.BlockSpec((HALO, width), lambda i: (jnp.maximum(i * per - 1, 0), 0))
        return pl.BlockSpec((HALO, width), lambda i: (jnp.minimum((i + 1) * per, last), 0))

    def call(self, body, name, in_specs, out_specs, out_shape, scratch=()):
        return pl.pallas_call(body, name=name, grid=(self.n_all,), in_specs=in_specs, out_specs=out_specs,
                              out_shape=out_shape, scratch_shapes=list(scratch), compiler_params=_params("arbitrary"))


_PEER_FLIPS = {
    "chips": [(1, 0, 0), (0, 1, 0), (1, 1, 0)],
    "sibling": [(0, 0, 1)],
    "all": [(0, 0, 1), (0, 1, 0), (0, 1, 1), (1, 0, 0), (1, 0, 1), (1, 1, 0), (1, 1, 1)],
}


def _slot(kind, x, y, c):
    return {"chips": 2 * x + y, "sibling": c, "all": 4 * x + 2 * y + c}[kind]


def exchange(kind, srcs, indexed, name):
    flips = _PEER_FLIPS[kind]
    n_slots = len(flips) + 1
    n_t = len(srcs)
    by_slot = list(indexed) if isinstance(indexed, (list, tuple)) else [indexed] * n_t
    out_shapes = [jax.ShapeDtypeStruct(s.shape if ix else (n_slots,) + s.shape, s.dtype) for s, ix in zip(srcs, by_slot)]

    def body(*refs):
        src_refs, dst_refs = refs[:n_t], refs[n_t:2 * n_t]
        send_sems, recv_sems, local_sems = refs[2 * n_t:]
        x, y, c = lax.axis_index("x"), lax.axis_index("y"), lax.axis_index("c")
        me = _slot(kind, x, y, c)

        def piece(t, k):
            return src_refs[t].at[k] if by_slot[t] else src_refs[t]

        peers = [(1 - x if fx else x, 1 - y if fy else y, 1 - c if fc else c) for fx, fy, fc in flips]
        local = [pltpu.make_async_copy(piece(t, me), dst_refs[t].at[me], local_sems.at[t]) for t in range(n_t)]
        for cp in local:
            cp.start()
        sends = []
        for t in range(n_t):
            for p, peer in enumerate(peers):
                sends.append(pltpu.make_async_remote_copy(
                    src_ref=piece(t, _slot(kind, *peer)), dst_ref=dst_refs[t].at[me],
                    send_sem=send_sems.at[t, p], recv_sem=recv_sems.at[t, p], device_id=peer, device_id_type=MESH))
        for cp in sends:
            cp.start()
        for t in range(n_t):
            for p, peer in enumerate(peers):
                theirs = _slot(kind, *peer)
                pltpu.make_async_remote_copy(
                    src_ref=piece(t, me), dst_ref=dst_refs[t].at[theirs],
                    send_sem=send_sems.at[t, p], recv_sem=recv_sems.at[t, p], device_id=peer, device_id_type=MESH).wait_recv()
        for cp in sends:
            cp.wait_send()
        for cp in local:
            cp.wait()

    any_spec = pl.BlockSpec(memory_space=pl.ANY)
    outs = pl.pallas_call(
        body, name=name, out_shape=out_shapes, in_specs=[any_spec] * n_t, out_specs=[any_spec] * n_t,
        scratch_shapes=[pltpu.SemaphoreType.DMA((n_t, len(flips))), pltpu.SemaphoreType.DMA((n_t, len(flips))),
                        pltpu.SemaphoreType.DMA((n_t,))],
        compiler_params=pltpu.CompilerParams(has_side_effects=True),
    )(*srcs)
    return list(outs)


def exchange_by_sequencer(srcs, indexed, name, collective_id):
    flips = _PEER_FLIPS["chips"]
    n_t, n_p = len(srcs), len(flips)
    src_refs = [jax.new_ref(s, memory_space=pltpu.MemorySpace.HBM) for s in srcs]
    land_refs = [jax.empty_ref(jax.ShapeDtypeStruct(s.shape if indexed else (n_p + 1,) + s.shape, s.dtype),
                               memory_space=pltpu.MemorySpace.HBM) for s in srcs]

    @pl.kernel(mesh=plsc.ScalarSubcoreMesh(axis_name="sequencer", num_cores=1), name=name,
               scratch_types=(pltpu.SemaphoreType.DMA((n_t, n_p)), pltpu.SemaphoreType.DMA((n_t, n_p)), pltpu.SemaphoreType.DMA((n_t,))),
               compiler_params=pltpu.CompilerParams(collective_id=collective_id))
    def launch(send_sems, recv_sems, local_sems):
        x, y, c = lax.axis_index("x"), lax.axis_index("y"), lax.axis_index("c")
        peers = [(1 - x if fx else x, 1 - y if fy else y, c) for fx, fy, _ in flips]
        barrier = pltpu.get_barrier_semaphore()
        for peer in peers:
            pl.semaphore_signal(barrier, inc=1, device_id=peer, device_id_type=MESH)
        pl.semaphore_wait(barrier, n_p)
        me = _slot("chips", x, y, c)

        def piece(t, k):
            return src_refs[t].at[k] if indexed else src_refs[t]

        local = [pltpu.make_async_copy(piece(t, me), land_refs[t].at[me], local_sems.at[t]) for t in range(n_t)]
        for cp in local:
            cp.start()
        sends = []
        for t in range(n_t):
            for p, peer in enumerate(peers):
                sends.append(pltpu.make_async_remote_copy(
                    src_ref=piece(t, _slot("chips", *peer)), dst_ref=land_refs[t].at[me],
                    send_sem=send_sems.at[t, p], recv_sem=recv_sems.at[t, p], device_id=peer, device_id_type=MESH))
        for cp in sends:
            cp.start()
        for t in range(n_t):
            for p, peer in enumerate(peers):
                pltpu.make_async_remote_copy(
                    src_ref=piece(t, me), dst_ref=land_refs[t].at[_slot("chips", *peer)],
                    send_sem=send_sems.at[t, p], recv_sem=recv_sems.at[t, p], device_id=peer, device_id_type=MESH).wait_recv()
        for cp in sends:
            cp.wait_send()
        for cp in local:
            cp.wait()

    launch()
    return [r[...] for r in land_refs]


MATMUL_VMEM_BUDGET = 44 * 1024 * 1024
MATMUL_STEP_BYTES = 1024 * 1024


def _divisors(n, most):
    return [t for t in range(LANES, min(n, most) + 1, LANES) if n % t == 0] or [n]


def _matmul_tiles(n_i, n_j, n_r, a_bytes, b_bytes, out_bytes):
    best = None
    for tr in _divisors(n_r, n_r):
        steps = n_r // tr
        for ti in _divisors(n_i, 1408):
            for tj in _divisors(n_j, 1664):
                vmem = 2 * (ti * tr * a_bytes + tr * tj * b_bytes + ti * tj * out_bytes) + (ti * tj * 4 if steps > 1 else 0)
                if vmem > MATMUL_VMEM_BUDGET or ti < min(n_i, 2 * LANES) or tj < min(n_j, 4 * LANES):
                    continue
                size_a, size_b = n_i * n_r * a_bytes, n_r * n_j * b_bytes
                if steps == 1:
                    moved = min(size_a + (n_i // ti) * size_b, size_b + (n_j // tj) * size_a)
                else:
                    moved = (n_j // tj) * size_a + (n_i // ti) * size_b + (steps - 1) * n_i * n_j * 8
                moved += n_i * n_j * out_bytes + (n_i // ti) * (n_j // tj) * steps * MATMUL_STEP_BYTES
                if best is None or moved < best[0]:
                    best = (moved, ti, tj, tr)
    return best[1:]


def matmul(a, b, mode, out_dtype, name, bias=None):
    if mode == "nn":
        (n_i, n_r), (_, n_j) = a.shape, b.shape
    elif mode == "nt":
        (n_i, n_r), (n_j, _) = a.shape, b.shape
    else:
        (n_r, n_i), (_, n_j) = a.shape, b.shape
    ti, tj, tr = _matmul_tiles(n_i, n_j, n_r, a.dtype.itemsize, b.dtype.itemsize, jnp.dtype(out_dtype).itemsize)
    steps = n_r // tr
    bytes_a, bytes_b = a.size * a.dtype.itemsize, b.size * b.dtype.itemsize
    j_outer = steps == 1 and bytes_b + (n_j // tj) * bytes_a < bytes_a + (n_i // ti) * bytes_b

    def at(fn):
        return (lambda j, i, r: fn(i, j, r)) if j_outer else fn

    a_spec = {"nn": pl.BlockSpec((ti, tr), at(lambda i, j, r: (i, r))), "nt": pl.BlockSpec((ti, tr), at(lambda i, j, r: (i, r))),
              "tn": pl.BlockSpec((tr, ti), at(lambda i, j, r: (r, i)))}[mode]
    b_spec = {"nn": pl.BlockSpec((tr, tj), at(lambda i, j, r: (r, j))), "nt": pl.BlockSpec((tj, tr), at(lambda i, j, r: (j, r))),
              "tn": pl.BlockSpec((tr, tj), at(lambda i, j, r: (r, j)))}[mode]
    dims = {"nn": (((1,), (0,)), ((), ())), "nt": (((1,), (1,)), ((), ())), "tn": (((0,), (0,)), ((), ()))}[mode]

    def body(*refs):
        a_ref, b_ref = refs[:2]
        bias_ref = refs[2] if bias is not None else None
        o_ref = refs[3] if bias is not None else refs[2]
        prod = lax.dot_general(a_ref[...].astype(BF16), b_ref[...].astype(BF16), dims, preferred_element_type=F32)

        def finish(total):
            if bias_ref is not None:
                total = total + bias_ref[...]
            o_ref[...] = total.astype(out_dtype)

        if steps == 1:
            finish(prod)
        else:
            acc = refs[-1]
            r = pl.program_id(2)

            @pl.when(r == 0)
            def _():
                acc[...] = prod

            @pl.when(r > 0)
            def _():
                acc[...] += prod

            @pl.when(r == steps - 1)
            def _():
                finish(acc[...])

    in_specs = [a_spec, b_spec] + ([pl.BlockSpec((1, tj), at(lambda i, j, r: (0, j)))] if bias is not None else [])
    args = (a, b) + ((bias,) if bias is not None else ())
    grid = (n_j // tj, n_i // ti, steps) if j_outer else (n_i // ti, n_j // tj, steps)
    return pl.pallas_call(
        body, name=name, grid=grid, in_specs=in_specs,
        out_specs=pl.BlockSpec((ti, tj), at(lambda i, j, r: (i, j))), out_shape=jax.ShapeDtypeStruct((n_i, n_j), out_dtype),
        scratch_shapes=[pltpu.VMEM((ti, tj), F32)] if steps > 1 else [],
        compiler_params=_params("arbitrary", "arbitrary", "arbitrary"),
    )(*args)


def matmul_swiglu(h, w_up, name):
    n_t, d = h.shape
    f = w_up.shape[1] // 2
    ti, tj = _pick(n_t, 384), _pick(f, 1408)
    n_j = f // tj

    def body(h_ref, wg_ref, wu_ref, g_ref, u_ref, act_ref):
        hv = h_ref[...]
        g = jnp.dot(hv, wg_ref[...], preferred_element_type=F32)
        u = jnp.dot(hv, wu_ref[...], preferred_element_type=F32)
        g_ref[...] = g.astype(BF16)
        u_ref[...] = u.astype(BF16)
        act_ref[...] = (_silu_and_grad(g)[0] * u).astype(BF16)

    out = pl.BlockSpec((ti, tj), lambda j, i: (i, j))
    shape = jax.ShapeDtypeStruct((n_t, f), BF16)
    return pl.pallas_call(
        body, name=name, grid=(n_j, n_t // ti),
        in_specs=[pl.BlockSpec((ti, d), lambda j, i: (i, 0)), pl.BlockSpec((d, tj), lambda j, i: (0, j)),
                  pl.BlockSpec((d, tj), lambda j, i: (0, j + n_j))],
        out_specs=[out, out, out], out_shape=[shape, shape, shape], compiler_params=_params("arbitrary", "arbitrary"),
    )(h, w_up, w_up)


def matmul_swiglu_bwd(df, w_down, g, u, name):
    n_t, d = df.shape
    f = w_down.shape[0]
    ti, tj = _pick(n_t, 768), _pick(f, 1408)

    def body(df_ref, w_ref, g_ref, u_ref, dg_ref, du_ref):
        dact = lax.dot_general(df_ref[...], w_ref[...], (((1,), (1,)), ((), ())), preferred_element_type=F32)
        act, dact_dg = _silu_and_grad(g_ref[...].astype(F32))
        dg_ref[...] = (dact * u_ref[...].astype(F32) * dact_dg).astype(BF16)
        du_ref[...] = (dact * act).astype(BF16)

    tile = pl.BlockSpec((ti, tj), lambda j, i: (i, j))
    shape = jax.ShapeDtypeStruct((n_t, f), BF16)
    return pl.pallas_call(
        body, name=name, grid=(f // tj, n_t // ti),
        in_specs=[pl.BlockSpec((ti, d), lambda j, i: (i, 0)), pl.BlockSpec((tj, d), lambda j, i: (j, 0)), tile, tile],
        out_specs=[tile, tile], out_shape=[shape, shape], compiler_params=_params("arbitrary", "arbitrary"))(df, w_down, g, u)


def matmul_pair_nt(a1, a2, b, name):
    n_t, f = a1.shape
    d = b.shape[0]
    ti = _pick(n_t, 768)
    dims = (((1,), (1,)), ((), ()))

    def body(a1_ref, a2_ref, b1_ref, b2_ref, o_ref):
        o_ref[...] = (lax.dot_general(a1_ref[...], b1_ref[...], dims, preferred_element_type=F32)
                      + lax.dot_general(a2_ref[...], b2_ref[...], dims, preferred_element_type=F32))

    rows_in = pl.BlockSpec((ti, f), lambda i: (i, 0))
    return pl.pallas_call(
        body, name=name, grid=(n_t // ti,),
        in_specs=[rows_in, rows_in, pl.BlockSpec((d, f), lambda i: (0, 0)), pl.BlockSpec((d, f), lambda i: (0, 1))],
        out_specs=pl.BlockSpec((ti, d), lambda i: (i, 0)), out_shape=jax.ShapeDtypeStruct((n_t, d), F32),
        compiler_params=_params("arbitrary"))(a1, a2, b, b)


def norm_mod(rows, x, g, mv, shift_k, scale_k, name, resid=None):
    d = rows.d
    changed = resid is not None

    def body(*refs):
        refs = list(refs)
        x_ref, g_ref, mv_ref = refs[:3]
        rest = refs[3:]
        xv = x_ref[...]
        if resid is not None:
            xv = xv + resid[2] * mv_ref[resid[1]:resid[1] + 1, :] * rest.pop(0)[...]
        if changed:
            rest.pop(0)[...] = xv
        r = lax.rsqrt(jnp.mean(xv * xv, axis=-1, keepdims=True) + EPS)
        h = (xv * r) * g_ref[...]
        if shift_k is not None:
            h = h * (1.0 + mv_ref[scale_k:scale_k + 1, :]) + mv_ref[shift_k:shift_k + 1, :]
        rest.pop(0)[...] = h.astype(BF16)

    ins = [x, g, mv] + ([resid[0]] if resid is not None else [])
    in_specs = [rows.row(d), rows.vec(d), rows.mod()] + [rows.row(d)] * (len(ins) - 3)
    out_shape = ([jax.ShapeDtypeStruct((rows.t_all, d), F32)] if changed else []) + [jax.ShapeDtypeStruct((rows.t_all, d), BF16)]
    outs = rows.call(body, name, in_specs, [rows.row(d)] * len(out_shape), out_shape)(*ins)
    return (outs[0], outs[1]) if changed else (None, outs[0])


def embed_norm_mod(rows, x_lat, x_ctx, row_tab, col_tab, g, mv, shift_k, scale_k, name):
    d = rows.d
    half = d // 2
    per_tile = rows.tm // GRID_W
    assert rows.tm % GRID_W == 0 and per_tile <= SUBLANES

    def body(xl_ref, xc_ref, rt_ref, ct_ref, g_ref, mv_ref, x0_ref, h_ref):
        i = pl.program_id(0)
        by_row = jnp.concatenate([jnp.broadcast_to(rt_ref[r:r + 1, :], (GRID_W, half)) for r in range(per_tile)], axis=0)
        by_col = jnp.concatenate([ct_ref[...]] * per_tile, axis=0)
        xv = jnp.where(i < rows.n_lat, xl_ref[...] + jnp.concatenate([by_row, by_col], axis=1), xc_ref[...])
        x0_ref[...] = xv
        r = lax.rsqrt(jnp.mean(xv * xv, axis=-1, keepdims=True) + EPS)
        h = (xv * r) * g_ref[...]
        h_ref[...] = (h * (1.0 + mv_ref[scale_k:scale_k + 1, :]) + mv_ref[shift_k:shift_k + 1, :]).astype(BF16)

    last = rows.n_lat - 1
    in_specs = [pl.BlockSpec((rows.tm, d), lambda i: (jnp.minimum(i, last), 0)),
                pl.BlockSpec((rows.tm, d), lambda i: (jnp.maximum(i - rows.n_lat, 0), 0)),
                pl.BlockSpec((SUBLANES, half), lambda i: (jnp.minimum(i, last), 0)), rows.full(col_tab.shape), rows.vec(d), rows.mod()]
    return rows.call(body, name, in_specs, [rows.row(d), rows.row(d)],
                     [jax.ShapeDtypeStruct((rows.t_all, d), F32), jax.ShapeDtypeStruct((rows.t_all, d), BF16)])(
                         x_lat, x_ctx, row_tab, col_tab, g, mv)


def norm_mod_bwd(rows, x, dh, dxn, g, mv, shift_k, scale_k, name, latent_only=False, then_gate=None):
    d = rows.d
    n_dx = rows.n_lat if latent_only else rows.n_all

    def body(*refs):
        x_ref, dh_ref, dxn_ref, g_ref, mv_ref = refs[:5]
        dx_ref, dg_ref, dsh_ref, dsc_ref = refs[-4:] if then_gate is None else refs[6:10]
        i = pl.program_id(0)
        xv, dhv, gv = x_ref[...], dh_ref[...], g_ref[...]
        r = lax.rsqrt(jnp.mean(xv * xv, axis=-1, keepdims=True) + EPS)
        n = xv * r
        dy = dhv * (1.0 + mv_ref[scale_k:scale_k + 1, :])
        dn = dy * gv
        dx = dxn_ref[...] + r * (dn - n * jnp.mean(dn * n, axis=-1, keepdims=True))

        @pl.when(i < n_dx)
        def _():
            dx_ref[...] = dx

        _accumulate(dg_ref, _colsum(dy * n), i == 0)
        _accumulate(dsh_ref, _colsum(dhv), rows.seg_first(i))
        _accumulate(dsc_ref, _colsum(dhv * (n * gv)), rows.seg_first(i))
        if then_gate is not None:
            _gate_bwd(rows, i, dx, refs[5], mv_ref, then_gate[1], then_gate[2], refs[10], refs[11])

    dx_spec = pl.BlockSpec((rows.tm, d), lambda i: (jnp.minimum(i, n_dx - 1), 0))
    ins = [x, dh, dxn, g, mv] + ([then_gate[0]] if then_gate is not None else [])
    in_specs = [rows.row(d), rows.row(d), rows.row(d), rows.vec(d), rows.mod()] + ([rows.row(d)] if then_gate is not None else [])
    out_specs = [dx_spec, rows.vec(d), rows.segsum(), rows.segsum()]
    out_shape = [jax.ShapeDtypeStruct((n_dx * rows.tm, d), F32), jax.ShapeDtypeStruct((1, d), F32), rows.segsum_shape(), rows.segsum_shape()]
    if then_gate is not None:
        out_specs += [rows.row(d), rows.segsum()]
        out_shape += [jax.ShapeDtypeStruct((rows.t_all, d), BF16), rows.segsum_shape()]
    return rows.call(body, name, in_specs, out_specs, out_shape)(*ins)


def _gate_bwd(rows, i, dx, f_ref, mv_ref, gate_k, scale, df_ref, dgate_ref):
    df_ref[...] = (scale * mv_ref[gate_k:gate_k + 1, :] * dx).astype(BF16)
    _accumulate(dgate_ref, scale * _colsum(dx * f_ref[...]), rows.seg_first(i))


def glu(rows, pvg, name):
    w = pvg.shape[1] // 2

    def body(v_ref, t_ref, o_ref):
        for cs in _chunks(w):
            o_ref[:, cs] = v_ref[:, cs].astype(F32) * _sigmoid(t_ref[:, cs].astype(F32))

    return rows.call(body, name, [rows.row(w, 0), rows.row(w, 1)], rows.row(w), jax.ShapeDtypeStruct((rows.t_all, w), F32))(pvg, pvg)


def glu_bwd(rows, pvg, du, name):
    w = pvg.shape[1] // 2

    def body(v_ref, t_ref, du_ref, o_ref):
        for cs in _chunks(w):
            s = _sigmoid(t_ref[:, cs].astype(F32))
            duv = du_ref[:, cs]
            o_ref[:, cs] = (duv * s).astype(BF16)
            o_ref[:, slice(w + cs.start, w + cs.stop)] = (duv * v_ref[:, cs].astype(F32) * s * (1.0 - s)).astype(BF16)

    return rows.call(body, name, [rows.row(w, 0), rows.row(w, 1), rows.row(w)], rows.row(2 * w),
                     jax.ShapeDtypeStruct((rows.t_all, 2 * w), BF16))(pvg, pvg, du)


def ln_silu(rows, u, g, b, name):
    w = u.shape[1]

    def body(u_ref, g_ref, b_ref, o_ref):
        uv = u_ref[...]
        xc = uv - jnp.mean(uv, axis=-1, keepdims=True)
        n = xc * lax.rsqrt(jnp.mean(xc * xc, axis=-1, keepdims=True) + EPS)
        o_ref[...] = _silu_and_grad(n * g_ref[...] + b_ref[...])[0].astype(BF16)

    return rows.call(body, name, [rows.row(w), rows.vec(w), rows.vec(w)], rows.row(w),
                     jax.ShapeDtypeStruct((rows.t_all, w), BF16))(u, g, b)


def ln_silu_bwd(rows, u, ds, g, b, name):
    w = u.shape[1]

    def body(u_ref, ds_ref, g_ref, b_ref, du_ref, dg_ref, db_ref):
        i = pl.program_id(0)
        uv, gv = u_ref[...], g_ref[...]
        xc = uv - jnp.mean(uv, axis=-1, keepdims=True)
        r = lax.rsqrt(jnp.mean(xc * xc, axis=-1, keepdims=True) + EPS)
        n = xc * r
        dy = ds_ref[...].astype(F32) * _silu_and_grad(n * gv + b_ref[...])[1]
        dn = dy * gv
        du_ref[...] = r * (dn - jnp.mean(dn, axis=-1, keepdims=True) - n * jnp.mean(dn * n, axis=-1, keepdims=True))
        _accumulate(dg_ref, _colsum(dy * n), i == 0)
        _accumulate(db_ref, _colsum(dy), i == 0)

    return rows.call(body, name, [rows.row(w), rows.row(w), rows.vec(w), rows.vec(w)], [rows.row(w), rows.vec(w), rows.vec(w)],
                     [jax.ShapeDtypeStruct((rows.t_all, w), F32), jax.ShapeDtypeStruct((1, w), F32),
                      jax.ShapeDtypeStruct((1, w), F32)])(u, ds, g, b)


def lru_merge(rows, h, gl, name):
    w = gl.shape[1]

    def body(h_ref, gl_ref, o_ref):
        for cs in _chunks(w):
            o_ref[:, cs] = ((h_ref[0, :, cs] + h_ref[1, :, cs]) * _gelu_and_grad(gl_ref[:, cs].astype(F32))[0]).astype(BF16)

    return rows.call(body, name, [pl.BlockSpec((2, rows.tm, w), lambda i: (0, i, 0)), rows.row(w)], rows.row(w),
                     jax.ShapeDtypeStruct((rows.t_all, w), BF16))(h, gl)


def lru_merge_bwd(rows, h, gl, dy, name):
    w = gl.shape[1]

    def body(h_ref, gl_ref, dy_ref, dh_ref, dgl_ref):
        for cs in _chunks(w):
            act, dact = _gelu_and_grad(gl_ref[:, cs].astype(F32))
            dyv = dy_ref[:, cs].astype(F32)
            dh_ref[:, cs] = dyv * act
            dgl_ref[:, cs] = (dyv * (h_ref[0, :, cs] + h_ref[1, :, cs]) * dact).astype(BF16)

    return rows.call(body, name, [pl.BlockSpec((2, rows.tm, w), lambda i: (0, i, 0)), rows.row(w), rows.row(w)],
                     [rows.row(w), rows.row(w)],
                     [jax.ShapeDtypeStruct((rows.t_all, w), F32), jax.ShapeDtypeStruct((rows.t_all, w), BF16)])(h, gl, dy)


def z_merge(rows, pgc, yc, yl, name):
    d = rows.d

    def body(gc_ref, gr_ref, yc_ref, yl_ref, o_ref):
        for cs in _chunks(d):
            o_ref[:, cs] = (_sigmoid(gc_ref[:, cs].astype(F32)) * yc_ref[:, cs].astype(F32)
                            + _sigmoid(gr_ref[:, cs].astype(F32)) * yl_ref[:, cs].astype(F32)).astype(BF16)

    return rows.call(body, name, [rows.row(d, 0), rows.row(d, 1), rows.row(d), rows.row(d)], rows.row(d),
                     jax.ShapeDtypeStruct((rows.t_all, d), BF16))(pgc, pgc, yc, yl)


def z_merge_bwd(rows, pgc, yc, yl, dz, name):
    d = rows.d

    def body(gc_ref, gr_ref, yc_ref, yl_ref, dz_ref, dyc_ref, dyl_ref, dp_ref):
        for cs in _chunks(d):
            sc, sr = _sigmoid(gc_ref[:, cs].astype(F32)), _sigmoid(gr_ref[:, cs].astype(F32))
            dzv = dz_ref[:, cs].astype(F32)
            dyc_ref[:, cs] = (dzv * sc).astype(BF16)
            dyl_ref[:, cs] = (dzv * sr).astype(BF16)
            dp_ref[:, cs] = (dzv * yc_ref[:, cs].astype(F32) * sc * (1.0 - sc)).astype(BF16)
            dp_ref[:, slice(d + cs.start, d + cs.stop)] = (dzv * yl_ref[:, cs].astype(F32) * sr * (1.0 - sr)).astype(BF16)

    return rows.call(body, name, [rows.row(d, 0), rows.row(d, 1), rows.row(d), rows.row(d), rows.row(d)],
                     [rows.row(d), rows.row(d), rows.row(2 * d)],
                     [jax.ShapeDtypeStruct((rows.t_all, d), BF16), jax.ShapeDtypeStruct((rows.t_all, d), BF16),
                      jax.ShapeDtypeStruct((rows.t_all, 2 * d), BF16)])(pgc, pgc, yc, yl, dz)


def col_sum(rows, x, name):
    w = x.shape[1]

    def body(x_ref, o_ref):
        i = pl.program_id(0)
        for cs in _chunks(w):
            _accumulate(o_ref.at[:, cs], _colsum(x_ref[:, cs].astype(F32)), i == 0)

    return rows.call(body, name, [rows.row(w)], rows.vec(w), jax.ShapeDtypeStruct((1, w), F32))(x)


def loss_head(rows, x, f, mv, g, target, name):
    d = rows.d

    def body(x_ref, f_ref, mv_ref, g_ref, t_ref, loss_ref, dx_ref, dg_ref, df_ref, dgate_ref):
        i = pl.program_id(0)
        valid = jnp.where(i < rows.n_lat, 1.0, 0.0)
        xv = x_ref[...] + 0.5 * mv_ref[8:9, :] * f_ref[...]
        gv = g_ref[...]
        r = lax.rsqrt(jnp.mean(xv * xv, axis=-1, keepdims=True) + EPS)
        n = xv * r
        err = (n * gv - t_ref[...]) * valid
        part = 0.5 * jnp.sum(jnp.mean(err * err, axis=-1, keepdims=True), axis=0, keepdims=True)
        _accumulate(loss_ref, jnp.broadcast_to(part, (1, LANES)), i == 0)
        dy = err * (1.0 / d)
        dn = dy * gv
        dx = r * (dn - n * jnp.mean(dn * n, axis=-1, keepdims=True))
        dx_ref[...] = dx
        _accumulate(dg_ref, _colsum(dy * n), i == 0)
        _gate_bwd(rows, i, dx, f_ref, mv_ref, 8, 0.5, df_ref, dgate_ref)

    target_spec = pl.BlockSpec((rows.tm, d), lambda i: (jnp.minimum(i, rows.n_lat - 1), 0))
    return rows.call(body, name, [rows.row(d), rows.row(d), rows.mod(), rows.vec(d), target_spec],
                     [rows.vec(LANES), rows.row(d), rows.vec(d), rows.row(d), rows.segsum()],
                     [jax.ShapeDtypeStruct((1, LANES), F32), jax.ShapeDtypeStruct((rows.t_all, d), F32),
                      jax.ShapeDtypeStruct((1, d), F32), jax.ShapeDtypeStruct((rows.t_all, d), BF16), rows.segsum_shape()])(
                          x, f, mv, g, target)


def _extended(rows, i, prev_ref, cur_ref, next_ref, cs):
    prev = jnp.where(rows.seg_first(i), 0.0, prev_ref[:, cs].astype(F32))
    nxt = jnp.where(rows.seg_last(i), 0.0, next_ref[:, cs].astype(F32))
    return jnp.concatenate([prev, cur_ref[:, cs].astype(F32), nxt], axis=0)


class _Shifts:
    def __init__(self, ext, tm):
        self.ext, self.tm, self.rolled = ext, tm, {0: ext}

    def at(self, offset):
        residue = offset % SUBLANES
        if residue not in self.rolled:
            self.rolled[residue] = pltpu.roll(self.ext, self.ext.shape[0] - residue, 0)
        start = HALO + offset - residue
        return self.rolled[residue][start:start + self.tm]


def _pad_taps(w):
    k = w.shape[0]
    return jnp.pad(w, ((0, -k % SUBLANES), (0, 0)))


def dwconv(rows, u, w, b, pad_left, out_dtype, name):
    taps, width = w.shape
    wp = _pad_taps(w)

    def body(*refs):
        prev_ref, cur_ref, next_ref, w_ref = refs[:4]
        b_ref = refs[4] if b is not None else None
        o_ref = refs[-1]
        i = pl.program_id(0)
        for cs in _chunks(width, LANES):
            shifts = _Shifts(_extended(rows, i, prev_ref, cur_ref, next_ref, cs), rows.tm)
            acc = jnp.zeros((rows.tm, LANES), F32) if b_ref is None else jnp.broadcast_to(b_ref[:, cs], (rows.tm, LANES))
            for k in range(taps):
                acc = acc + w_ref[k:k + 1, cs] * shifts.at(k - pad_left)
            o_ref[:, cs] = acc.astype(out_dtype)

    ins = [u, u, u, wp] + ([b] if b is not None else [])
    in_specs = [rows.halo(width, -1), rows.row(width), rows.halo(width, 1), rows.full(wp.shape)] + ([rows.vec(width)] if b is not None else [])
    return rows.call(body, name, in_specs, rows.row(width), jax.ShapeDtypeStruct((rows.t_all, width), out_dtype))(*ins)


def dwconv_bwd_w(rows, u, dy, taps, pad_left, name):
    width = u.shape[1]
    taps_p = taps + (-taps % SUBLANES)

    def body(prev_ref, cur_ref, next_ref, dy_ref, dw_ref, db_ref):
        i = pl.program_id(0)
        tap_row = lax.broadcasted_iota(jnp.int32, (taps_p, LANES), 0)
        for cs in _chunks(width, LANES):
            shifts = _Shifts(_extended(rows, i, prev_ref, cur_ref, next_ref, cs), rows.tm)
            dyv = dy_ref[:, cs]
            total = jnp.zeros((taps_p, LANES), F32)
            for k in range(taps):
                total = total + jnp.where(tap_row == k, _colsum(dyv * shifts.at(k - pad_left)), 0.0)
            _accumulate(dw_ref.at[:, cs], total, i == 0)
            _accumulate(db_ref.at[:, cs], _colsum(dyv), i == 0)

    dw, db = rows.call(body, name, [rows.halo(width, -1), rows.row(width), rows.halo(width, 1), rows.row(width)],
                       [rows.full((taps_p, width)), rows.vec(width)],
                       [jax.ShapeDtypeStruct((taps_p, width), F32), jax.ShapeDtypeStruct((1, width), F32)])(u, u, u, dy)
    return dw[:taps], db


def _gate_values(xr, pre, lam, d):
    r = _sigmoid(pre[:, (2 * d) * LANES:(2 * d + 1) * LANES])
    ig = _sigmoid(pre[:, (2 * d + 1) * LANES:(2 * d + 2) * LANES])
    sp = _softplus(-lam[:, d * LANES:(d + 1) * LANES])
    log_a = -LRU_C * r * sp
    return r, ig, sp, jnp.exp(log_a), _neg_expm1(2.0 * log_a)


def lru_gates(rows, xr, wcat, bcat, lam, name):
    n_blk = wcat.shape[0]
    width = xr.shape[1]

    def body(xr_ref, w_ref, b_ref, lam_ref, a_ref, bb_ref):
        for h in range(n_blk):
            cs = slice(h * LANES, (h + 1) * LANES)
            xv = xr_ref[:, cs]
            pre = jnp.dot(xv.astype(BF16), w_ref[h], preferred_element_type=F32) + b_ref[h]
            for d in range(2):
                _, ig, _, a, one_minus_a2 = _gate_values(xv, pre, lam_ref[h], d)
                a_ref[d, :, cs] = a
                bb_ref[d, :, cs] = jnp.sqrt(one_minus_a2) * (ig * xv)

    both = pl.BlockSpec((2, rows.tm, width), lambda i: (0, i, 0))
    shape = jax.ShapeDtypeStruct((2, rows.t_all, width), F32)
    return rows.call(body, name, [rows.row(width), rows.full(wcat.shape), rows.full(bcat.shape), rows.full(lam.shape)],
                     [both, both], [shape, shape])(xr, wcat, bcat, lam)


def lru_gates_bwd(rows, xr, wcat, bcat, lam, da, dbb, name):
    n_blk = wcat.shape[0]
    width = xr.shape[1]

    def body(xr_ref, w_ref, b_ref, lam_ref, da_ref, dbb_ref, dxr_ref, dw_ref, db_ref, dlam_ref):
        i = pl.program_id(0)
        for h in range(n_blk):
            cs = slice(h * LANES, (h + 1) * LANES)
            xv = xr_ref[:, cs]
            xb = xv.astype(BF16)
            wv = w_ref[h]
            pre = jnp.dot(xb, wv, preferred_element_type=F32) + b_ref[h]
            dxr = jnp.zeros_like(xv)
            dpre, dlam = [], []
            for d in range(2):
                r, ig, sp, a, one_minus_a2 = _gate_values(xv, pre, lam_ref[h], d)
                inv_q = lax.rsqrt(one_minus_a2)
                dav, dbv = da_ref[d, :, cs], dbb_ref[d, :, cs]
                dbq = dbv * (one_minus_a2 * inv_q) * ig
                dlog_a = dav * a - dbv * (ig * xv) * ((a * a) * inv_q)
                dpre.append(dlog_a * (-LRU_C * sp) * (r - r * r))
                dpre.append(dbq * xv * (1.0 - ig))
                dxr = dxr + dbq
                dlam.append(_colsum(dlog_a * (-LRU_C * r)) * (-_sigmoid(-lam_ref[h][:, d * LANES:(d + 1) * LANES])))
            dpre = jnp.concatenate(dpre, axis=1)
            dpb = dpre.astype(BF16)
            dxr_ref[:, cs] = dxr + lax.dot_general(dpb, wv, (((1,), (1,)), ((), ())), preferred_element_type=F32)
            _accumulate(dw_ref.at[h], lax.dot_general(xb, dpb, (((0,), (0,)), ((), ())), preferred_element_type=F32), i == 0)
            _accumulate(db_ref.at[h], _colsum(dpre), i == 0)
            _accumulate(dlam_ref.at[h], jnp.concatenate(dlam, axis=1), i == 0)

    both = pl.BlockSpec((2, rows.tm, width), lambda i: (0, i, 0))
    return rows.call(
        body, name, [rows.row(width), rows.full(wcat.shape), rows.full(bcat.shape), rows.full(lam.shape), both, both],
        [rows.row(width), rows.full(wcat.shape), rows.full(bcat.shape), rows.full(lam.shape)],
        [jax.ShapeDtypeStruct((rows.t_all, width), F32), jax.ShapeDtypeStruct(wcat.shape, F32),
         jax.ShapeDtypeStruct(bcat.shape, F32), jax.ShapeDtypeStruct(lam.shape, F32)])(xr, wcat, bcat, lam, da, dbb)


def _tile_scan(a, b, reverse):
    n = a.shape[0]
    row = lax.broadcasted_iota(jnp.int32, a.shape, 0)
    k = 1
    while k < n:
        ok = (row < n - k) if reverse else (row >= k)
        shift = n - k if reverse else k
        b = b + a * jnp.where(ok, pltpu.roll(b, shift, 0), 0.0)
        a = a * jnp.where(ok, pltpu.roll(a, shift, 0), 1.0)
        k *= 2
    return a, b


def _chain_scan(a, b, h_in, reverse):
    n_blocks = a.shape[0] // SUBLANES
    out = [None] * n_blocks
    state = h_in
    for j in (range(n_blocks - 1, -1, -1) if reverse else range(n_blocks)):
        rows_j = slice(j * SUBLANES, (j + 1) * SUBLANES)
        cum, h0 = _tile_scan(a[rows_j], b[rows_j], reverse)
        out[j] = h0 + cum * state
        state = out[j][0:1] if reverse else out[j][SUBLANES - 1:SUBLANES]
    return jnp.concatenate(out, axis=0), state


def _neighbour(v, edge, reverse):
    n = v.shape[0]
    row = lax.broadcasted_iota(jnp.int32, v.shape, 0)
    if reverse:
        return jnp.where(row < n - 1, pltpu.roll(v, n - 1, 0), edge)
    return jnp.where(row >= 1, pltpu.roll(v, 1, 0), edge)


def _scan_call(rows, body, name, ins, in_specs, n_out, width, adjoint):
    n_all, n_lat = rows.n_all, rows.n_lat

    def tile(d, s):
        s = n_all - 1 - s if adjoint else s
        return jnp.where(d == 0, (s + n_lat) % n_all, n_all - 1 - s)

    def per_dir(d, s):
        return (d, tile(d, s), 0)

    specs = [pl.BlockSpec((None, rows.tm, width), per_dir) if kind == "dir" else
             pl.BlockSpec((rows.tm, width), lambda d, s: (tile(d, s), 0)) for kind in in_specs]
    shape = jax.ShapeDtypeStruct((2, rows.t_all, width), F32)
    return pl.pallas_call(
        body, name=name, grid=(2, n_all), in_specs=specs, out_specs=[pl.BlockSpec((None, rows.tm, width), per_dir)] * n_out,
        out_shape=[shape] * n_out, scratch_shapes=[pltpu.VMEM((SUBLANES, width), F32)],
        compiler_params=_params("arbitrary", "arbitrary"))(*ins)


def lru_scan(rows, a, bb, name):
    width = a.shape[2]
    n = rows.tm

    def body(a_ref, bb_ref, h_ref, hp_ref, carry):
        d, s = pl.program_id(0), pl.program_id(1)

        @pl.when(s == 0)
        def _():
            carry[...] = jnp.zeros_like(carry)

        def run(reverse):
            for cs in _chunks(width, LANES):
                h_in = carry[0:1, cs]
                h, carry[0:1, cs] = _chain_scan(a_ref[:, cs], bb_ref[:, cs], h_in, reverse)
                h_ref[:, cs] = h
                hp_ref[:, cs] = _neighbour(h, h_in, reverse)

        pl.when(d == 0)(lambda: run(False))
        pl.when(d == 1)(lambda: run(True))

    return _scan_call(rows, body, name, [a, bb], ["dir", "dir"], 2, width, adjoint=False)


def lru_scan_bwd(rows, dh, a, hp, name):
    width = a.shape[2]
    n = rows.tm

    def body(dh_ref, a_ref, hp_ref, da_ref, dbb_ref, carry):
        d, s = pl.program_id(0), pl.program_id(1)

        @pl.when(s == 0)
        def _():
            carry[...] = jnp.zeros_like(carry)

        def run(reverse):
            for cs in _chunks(width, LANES):
                av = a_ref[:, cs]
                g, _ = _chain_scan(_neighbour(av, 1.0, reverse), dh_ref[:, cs], carry[0:1, cs], reverse)
                da_ref[:, cs] = g * hp_ref[:, cs]
                dbb_ref[:, cs] = g
                carry[0:1, cs] = (av * g)[0:1] if reverse else (av * g)[n - 1:n]

        pl.when(d == 0)(lambda: run(True))
        pl.when(d == 1)(lambda: run(False))

    return _scan_call(rows, body, name, [dh, a, hp], ["shared", "dir", "dir"], 2, width, adjoint=True)


def mod_matmul(c9, w_shard, b_shard, name):
    n = w_shard.shape[1]
    tn = _pick(n, 768)

    def body(c_ref, w_ref, b_ref, o_ref):
        act = _silu_and_grad(c_ref[...])[0]
        o_ref[...] = jnp.dot(act, w_ref[...], preferred_element_type=F32, precision=lax.Precision.HIGHEST) + b_ref[...]

    return pl.pallas_call(
        body, name=name, grid=(n // tn,),
        in_specs=[pl.BlockSpec(c9.shape, lambda j: (0, 0)), pl.BlockSpec((w_shard.shape[0], tn), lambda j: (0, j)),
                  pl.BlockSpec((1, tn), lambda j: (0, j))],
        out_specs=pl.BlockSpec((c9.shape[0], tn), lambda j: (0, j)), out_shape=jax.ShapeDtypeStruct((c9.shape[0], n), F32),
        compiler_params=_params("arbitrary"))(c9, w_shard, b_shard)


def mod_matmul_bwd(c9, d9, w_shard, name):
    n = w_shard.shape[1]
    tn = _pick(n, 768)
    steps = n // tn

    def body(c_ref, d_ref, w_ref, gw_ref, gc_ref):
        j = pl.program_id(0)
        act, dact = _silu_and_grad(c_ref[...])
        dv = d_ref[...]
        gw_ref[...] = lax.dot_general(act, dv, (((0,), (0,)), ((), ())), preferred_element_type=F32, precision=lax.Precision.HIGHEST)
        part = lax.dot_general(dv, w_ref[...], (((1,), (1,)), ((), ())), preferred_element_type=F32, precision=lax.Precision.HIGHEST)
        _accumulate(gc_ref, part * dact, j == 0)

    return pl.pallas_call(
        body, name=name, grid=(steps,),
        in_specs=[pl.BlockSpec(c9.shape, lambda j: (0, 0)), pl.BlockSpec((c9.shape[0], tn), lambda j: (0, j)),
                  pl.BlockSpec((w_shard.shape[0], tn), lambda j: (0, j))],
        out_specs=[pl.BlockSpec((w_shard.shape[0], tn), lambda j: (0, j)), pl.BlockSpec(c9.shape, lambda j: (0, 0))],
        out_shape=[jax.ShapeDtypeStruct(w_shard.shape, F32), jax.ShapeDtypeStruct(c9.shape, F32)],
        compiler_params=_params("arbitrary"))(c9, d9, w_shard)


def _row_tile(n_rows, n_cols):
    return _pick(n_rows, max(2 * SUBLANES, (256 * 1024) // n_cols), 2 * SUBLANES)


def sum_parts(parts, name):
    n, n_rows, n_cols = parts.shape
    tr = _row_tile(n_rows, n_cols)

    def body(p_ref, o_ref):
        total = p_ref[0].astype(F32)
        for k in range(1, n):
            total = total + p_ref[k].astype(F32)
        o_ref[...] = total

    return pl.pallas_call(
        body, name=name, grid=(n_rows // tr,), in_specs=[pl.BlockSpec((n, tr, n_cols), lambda i: (0, i, 0))],
        out_specs=pl.BlockSpec((tr, n_cols), lambda i: (i, 0)), out_shape=jax.ShapeDtypeStruct((n_rows, n_cols), F32),
        compiler_params=_params("arbitrary"))(parts)


def _adamw_update(g, w_ref, m_ref, v_ref, g_ref, d_ref, m2_ref, v2_ref):
    m2 = ADAM_B1 * m_ref[...] + (1.0 - ADAM_B1) * g
    v2 = ADAM_B2 * v_ref[...] + (1.0 - ADAM_B2) * (g * g)
    m_hat = m2 / (1.0 - ADAM_B1 ** ADAM_STEP)
    v_hat = v2 / (1.0 - ADAM_B2 ** ADAM_STEP)
    g_ref[...] = g
    d_ref[...] = -ADAM_LR * (m_hat / (jnp.sqrt(v_hat) + ADAM_EPS) + ADAM_WD * w_ref[...])
    m2_ref[...] = m2
    v2_ref[...] = v2


def adamw(parts, w, m, v, name):
    n, n_rows, n_cols = parts.shape
    tr = _row_tile(n_rows, n_cols)

    def body(p_ref, *refs):
        g = p_ref[0].astype(F32)
        for k in range(1, n):
            g = g + p_ref[k].astype(F32)
        _adamw_update(g, *refs)

    blk = pl.BlockSpec((tr, n_cols), lambda i: (i, 0))
    shape = jax.ShapeDtypeStruct((n_rows, n_cols), F32)
    return pl.pallas_call(
        body, name=name, grid=(n_rows // tr,), in_specs=[pl.BlockSpec((n, tr, n_cols), lambda i: (0, i, 0)), blk, blk, blk],
        out_specs=[blk] * 4, out_shape=[shape] * 4, compiler_params=_params("arbitrary"))(parts, w, m, v)


def adamw_pair(part, w, m, v, name):
    n_rows, n_cols = part.shape
    tr = _row_tile(n_rows, n_cols)
    steps = n_rows // tr

    def body(p_ref, w_ref, m_ref, v_ref, g_ref, d_ref, m2_ref, v2_ref, inbox, send_sems, recv_sems, credits):
        i = pl.program_id(0)
        slot = i % 2
        sibling = (lax.axis_index("x"), lax.axis_index("y"), 1 - lax.axis_index("c"))

        @pl.when(i >= 2)
        def _():
            pl.semaphore_wait(credits.at[slot], 1)

        copy = pltpu.make_async_remote_copy(src_ref=p_ref, dst_ref=inbox.at[slot], send_sem=send_sems.at[slot],
                                            recv_sem=recv_sems.at[slot], device_id=sibling, device_id_type=MESH)
        copy.start()
        copy.wait_recv()
        copy.wait_send()
        _adamw_update(p_ref[...] + inbox[slot], w_ref, m_ref, v_ref, g_ref, d_ref, m2_ref, v2_ref)

        @pl.when(i + 2 < steps)
        def _():
            pl.semaphore_signal(credits.at[slot], inc=1, device_id=sibling, device_id_type=MESH)

    blk = pl.BlockSpec((tr, n_cols), lambda i: (i, 0))
    shape = jax.ShapeDtypeStruct((n_rows, n_cols), F32)
    return pl.pallas_call(
        body, name=name, grid=(steps,), in_specs=[blk] * 4, out_specs=[blk] * 4, out_shape=[shape] * 4,
        scratch_shapes=[pltpu.VMEM((2, tr, n_cols), F32), pltpu.SemaphoreType.DMA((2,)), pltpu.SemaphoreType.DMA((2,)),
                        pltpu.SemaphoreType.REGULAR((2,))],
        compiler_params=_params("arbitrary"))(part, w, m, v)


def _pack(arrays, row_multiple=SUBLANES):
    flat = [jnp.pad(a.reshape(-1), (0, -a.size % LANES)) for a in arrays]
    buf = jnp.concatenate(flat)
    buf = jnp.pad(buf, (0, -buf.size % (row_multiple * LANES)))
    return buf.reshape(-1, LANES)


def _unpack(buf, shapes):
    lead = buf.shape[:-2]
    flat = buf.reshape(lead + (-1,))
    out, pos = [], 0
    for shape in shapes:
        size = 1
        for s in shape:
            size *= s
        out.append(flat[..., pos:pos + size].reshape(lead + tuple(shape)))
        pos += size + (-size % LANES)
    return out


def _pos_tables(seq_len, dim, tile_rows):
    grid_rows = seq_len // GRID_W
    per_tile = tile_rows // GRID_W
    q = dim // 4
    omega = 1.0 / (10000.0 ** (jnp.arange(q, dtype=F32) / q))
    er = jnp.arange(grid_rows).astype(F32)[:, None] * omega
    ec = jnp.arange(GRID_W).astype(F32)[:, None] * omega
    by_row = jnp.concatenate([jnp.sin(er), jnp.cos(er)], axis=-1).reshape(grid_rows // per_tile, per_tile, 2 * q)
    by_row = jnp.pad(by_row, ((0, 0), (0, SUBLANES - per_tile), (0, 0))).reshape(-1, 2 * q)
    return by_row, jnp.concatenate([jnp.sin(ec), jnp.cos(ec)], axis=-1)


def _cols_from_shards(g):
    return jnp.transpose(g, (1, 0, 2)).reshape(g.shape[1], -1)


def _shards_from_cols(w, n_shards=4):
    k, n = w.shape
    return jnp.transpose(w.reshape(k, n_shards, n // n_shards), (1, 0, 2))


def kernel(x, c, ctx, c_ctx, w_mod, b_mod, g_n1, w_ffn1_up, w_ffn1_down, g_n2, w_in, b_in, w_dw, b_dw, g_ln, b_ln, w_conf_out, w_lru_conv, b_lru_conv, w_rec_gate, b_rec_gate, w_in_gate, b_in_gate, lru_lambda, w_lru_out, w_out, g_n3, w_ffn2_up, w_ffn2_down, g_final, loss_target, m_c_ctx, m_w_mod, m_b_mod, m_g_n1, m_w_ffn1_up, m_w_ffn1_down, m_g_n2, m_w_in, m_b_in, m_w_dw, m_b_dw, m_g_ln, m_b_ln, m_w_conf_out, m_w_lru_conv, m_b_lru_conv, m_w_rec_gate, m_b_rec_gate, m_w_in_gate, m_b_in_gate, m_lru_lambda, m_w_lru_out, m_w_out, m_g_n3, m_w_ffn2_up, m_w_ffn2_down, m_g_final, v_c_ctx, v_w_mod, v_b_mod, v_g_n1, v_w_ffn1_up, v_w_ffn1_down, v_g_n2, v_w_in, v_b_in, v_w_dw, v_b_dw, v_g_ln, v_b_ln, v_w_conf_out, v_w_lru_conv, v_b_lru_conv, v_w_rec_gate, v_b_rec_gate, v_w_in_gate, v_b_in_gate, v_lru_lambda, v_w_lru_out, v_w_out, v_g_n3, v_w_ffn2_up, v_w_ffn2_down, v_g_final):
    given = dict(locals())
    wt = {n: given[n] for n in W_NAMES}
    mom = {n: given["m_" + n] for n in W_NAMES}
    var = {n: given["v_" + n] for n in W_NAMES}

    def flat2(a):
        if a.ndim == 1:
            return a.reshape(1, -1)
        a = a[0]
        return a if a.ndim == 2 else a.reshape(-1, a.shape[-1])

    t_lat, d = x.shape[1], x.shape[2]
    t_ctx = ctx.shape[1]
    rows = Rows(t_lat, t_ctx, d)
    xi, yi, ci = lax.axis_index("x"), lax.axis_index("y"), lax.axis_index("c")
    me, chip = 4 * xi + 2 * yi + ci, 2 * xi + yi
    lru_w = b_lru_conv.shape[-1]
    n_blk = lru_w // LANES
    taps = w_dw.shape[1]
    lru_taps = w_lru_conv.shape[1]
    ffn = w_ffn1_down.shape[1] * 4

    small_shapes = [(1, d)] + [flat2(wt[n]).shape for n in SMALL_SHARDED]
    small_all = exchange("all", [_pack([c] + [flat2(wt[n]) for n in SMALL_SHARDED])], False, "gather_small")[0]
    small_all = _unpack(small_all, small_shapes)
    c_all = small_all[0][:, 0, :]
    full = {n: jnp.concatenate([a[0], a[2], a[4], a[6]], axis=-1) for n, a in zip(SMALL_SHARDED, small_all[1:])}
    big = COL_SHARDED + ROW_SHARDED
    ffn1_names = ['w_ffn1_up', 'w_ffn1_down']
    ffn2_names = ['w_ffn2_up', 'w_ffn2_down']
    mixer_names = [n for n in big if n not in ffn1_names + ffn2_names]
    wb = {}

    def take_weights(names, gathered):
        for n, g in zip(names, gathered):
            wb[n] = _cols_from_shards(g) if n in COL_SHARDED else g.reshape(-1, g.shape[-1])

    def shards_after(names, earlier):
        return lax.optimization_barrier(([wt[n][0].astype(BF16) for n in names], earlier))[0]

    c9 = jnp.concatenate([c_all, c_ctx.reshape(1, d), jnp.zeros((7, d), F32)], axis=0)
    mod_cols = mod_matmul(c9, w_mod[0], lax.dynamic_slice_in_dim(b_mod, chip * w_mod.shape[2], w_mod.shape[2], axis=1), "mod_matmul")
    mod_all = exchange("all", [mod_cols], False, "gather_mod")[0]
    mod9 = jnp.concatenate([mod_all[0], mod_all[2], mod_all[4], mod_all[6]], axis=-1)
    mv = jnp.stack([lax.dynamic_index_in_dim(mod9, me, 0, keepdims=False).reshape(N_MOD, d), mod9[8].reshape(N_MOD, d)])

    gathered_ffn1 = exchange("chips", shards_after(ffn1_names, mod_all), False, "gather_weights_ffn1")
    take_weights(ffn1_names, gathered_ffn1)
    gathered_mixer = exchange_by_sequencer(shards_after(mixer_names, gathered_ffn1), False, "gather_weights_mixer", 1)

    row_tab, col_tab = _pos_tables(t_lat, d, rows.tm)

    def ffn_fwd(tag, h, w_up, w_down):
        g, u, act = matmul_swiglu(h, w_up, f"{tag}_up")
        return (g, u), act, matmul(act, w_down, "nn", F32, f"{tag}_down")

    x0, h1 = embed_norm_mod(rows, x[0], ctx[0], row_tab, col_tab, g_n1, mv, 0, 1, "norm1")
    gu1, act1, f1 = ffn_fwd("ffn1", h1, wb["w_ffn1_up"], wb["w_ffn1_down"])
    gathered_mixer, f1 = lax.optimization_barrier((gathered_mixer, f1))
    take_weights(mixer_names, gathered_mixer)
    gathered_ffn2 = exchange_by_sequencer(shards_after(ffn2_names, gathered_mixer), False, "gather_weights_ffn2", 5)
    x1, h2 = norm_mod(rows, x0, g_n2, mv, 3, 4, "norm2", resid=(f1, 2, 0.5))

    conf_w = w_conf_out.shape[1] * 4
    col_groups = [(0, 2 * conf_w), (2 * conf_w, lru_w), (2 * conf_w + lru_w, lru_w), (2 * conf_w + 2 * lru_w, 2 * d)]
    pvg, ux, gl, pgc = [matmul(h2, wb["w_in"][:, s:s + n], "nn", BF16, f"in_proj{k}", bias=b_in[:, s:s + n])
                        for k, (s, n) in enumerate(col_groups)]
    u = glu(rows, pvg, "glu")
    u2 = dwconv(rows, u, full["w_dw"], b_dw, taps // 2, F32, "conf_conv")
    s_act = ln_silu(rows, u2, g_ln, b_ln, "conf_ln")
    yc = matmul(s_act, wb["w_conf_out"], "nn", BF16, "conf_out")

    xr = dwconv(rows, ux, full["w_lru_conv"], b_lru_conv, LRU_PAD_LEFT, F32, "lru_conv")
    w_rec, w_ing = w_rec_gate[0].astype(BF16), w_in_gate[0].astype(BF16)
    wcat = jnp.concatenate([w_rec[0], w_ing[0], w_rec[1], w_ing[1]], axis=-1)

    def per_block(a):
        return jnp.transpose(a.reshape(2, n_blk, LANES), (1, 0, 2)).reshape(n_blk, 1, 2 * LANES)

    b_rec, b_ing = full["b_rec_gate"].reshape(2, n_blk, LANES), full["b_in_gate"].reshape(2, n_blk, LANES)
    bcat = jnp.concatenate([b_rec[0], b_ing[0], b_rec[1], b_ing[1]], axis=-1).reshape(n_blk, 1, 4 * LANES)
    lam = per_block(full["lru_lambda"])
    a_gate, b_gate = lru_gates(rows, xr, wcat, bcat, lam, "lru_gates")
    h_scan, h_prev = lru_scan(rows, a_gate, b_gate, "lru_scan")
    yl_in = lru_merge(rows, h_scan, gl, "lru_merge")
    yl = matmul(yl_in, wb["w_lru_out"], "nn", BF16, "lru_out")
    z = z_merge(rows, pgc, yc, yl, "z_merge")
    y = matmul(z, wb["w_out"], "nn", F32, "mix_out")

    gathered_ffn2, y = lax.optimization_barrier((gathered_ffn2, y))
    take_weights(ffn2_names, gathered_ffn2)
    x2, h3 = norm_mod(rows, x1, g_n3, mv, 6, 7, "norm3", resid=(y, 5, 1.0))
    gu2, act2, f2 = ffn_fwd("ffn2", h3, wb["w_ffn2_up"], wb["w_ffn2_down"])
    loss_part, dx3, dg_final, df2, dgate2 = loss_head(rows, x2, f2, mv, g_final.reshape(1, d), loss_target[0], "loss_head")

    grads = {"g_final": dg_final}
    dmv = [None] * N_MOD

    def ffn_bwd(tag, dxn, df, x_prev, h, gu, act, w_up, w_down, g, ks, collective_id, settle_first=(), **norm_bwd_options):
        k_shift, k_scale = ks
        grads[f"w_{tag}_down"] = matmul(act, df, "tn", BF16, f"{tag}_down_dw")
        if settle_first:
            df = settle(settle_first, df)
        dg_, du_ = matmul_swiglu_bwd(df, w_down, gu[0], gu[1], f"{tag}_down_dx")
        grads[f"w_{tag}_up"] = jnp.concatenate([matmul(h, dg_, "tn", BF16, f"{tag}_up_dw_g"), matmul(h, du_, "tn", BF16, f"{tag}_up_dw_u")], axis=1)
        dg_, du_ = scatter_behind([f"w_{tag}_up", f"w_{tag}_down"], tag, collective_id, (dg_, du_))
        dh = matmul_pair_nt(dg_, du_, w_up, f"{tag}_up_dx")
        dx, dg, dmv[k_shift], dmv[k_scale], *more = norm_mod_bwd(rows, x_prev, dh, dxn, g, mv, k_shift, k_scale, f"{tag}_norm_bwd",
                                                                 **norm_bwd_options)
        return (dx, dg, *more)

    def grad_pieces(names):
        return [(_shards_from_cols(grads[n]) if n in COL_SHARDED else grads[n].reshape(4, -1, grads[n].shape[-1])).astype(BF16)
                for n in names]

    from_chips = {}

    def scatter_behind(names, tag, collective_id, carry):
        ready, carry = lax.optimization_barrier(([grads[n] for n in names], carry))
        grads.update(zip(names, ready))
        from_chips.update(zip(names, exchange_by_sequencer(grad_pieces(names), True, f"scatter_grads_{tag}", collective_id)))
        return carry

    def settle(names, carry):
        landed, carry = lax.optimization_barrier(([from_chips[n] for n in names], carry))
        from_chips.update(zip(names, landed))
        return carry

    dmv[8] = dgate2
    dx2, grads["g_n3"], dy, dmv[5] = ffn_bwd("ffn2", dx3, df2, x2, h3, gu2, act2, wb["w_ffn2_up"], wb["w_ffn2_down"], g_n3, (6, 7), 2,
                                             then_gate=(y, 5, 1.0))

    grads["w_out"] = matmul(z, dy, "tn", BF16, "mix_out_dw")
    dz = matmul(dy, wb["w_out"], "nt", BF16, "mix_out_dx")
    dyc, dyl, dpgc = z_merge_bwd(rows, pgc, yc, yl, dz, "z_merge_bwd")

    grads["w_conf_out"] = matmul(s_act, dyc, "tn", BF16, "conf_out_dw")
    ds_act = matmul(dyc, wb["w_conf_out"], "nt", BF16, "conf_out_dx")
    du2, grads["g_ln"], grads["b_ln"] = ln_silu_bwd(rows, u2, ds_act, g_ln, b_ln, "conf_ln_bwd")
    grads["w_dw"], grads["b_dw"] = dwconv_bwd_w(rows, u, du2, taps, taps // 2, "conf_conv_dw")
    du = dwconv(rows, du2, full["w_dw"][::-1], None, taps - 1 - taps // 2, F32, "conf_conv_dx")
    dpvg = glu_bwd(rows, pvg, du, "glu_bwd")

    grads["w_lru_out"] = matmul(yl_in, dyl, "tn", BF16, "lru_out_dw")
    dyl = scatter_behind(['w_out', 'w_conf_out', 'w_lru_out'], "mixer", 3, settle(ffn2_names, dyl))
    dyl_in = matmul(dyl, wb["w_lru_out"], "nt", BF16, "lru_out_dx")
    dh_sum, dgl = lru_merge_bwd(rows, h_scan, gl, dyl_in, "lru_merge_bwd")
    da_gate, db_gate = lru_scan_bwd(rows, dh_sum, a_gate, h_prev, "lru_scan_bwd")
    dxr, dwcat, dbcat, dlam = lru_gates_bwd(rows, xr, wcat, bcat, lam, da_gate, db_gate, "lru_gates_bwd")
    grads["w_rec_gate"] = jnp.stack([dwcat[:, :, 0:LANES], dwcat[:, :, 2 * LANES:3 * LANES]])
    grads["w_in_gate"] = jnp.stack([dwcat[:, :, LANES:2 * LANES], dwcat[:, :, 3 * LANES:4 * LANES]])
    dbcat = dbcat.reshape(n_blk, 4, LANES)
    grads["b_rec_gate"] = jnp.stack([dbcat[:, 0], dbcat[:, 2]]).reshape(2, lru_w)
    grads["b_in_gate"] = jnp.stack([dbcat[:, 1], dbcat[:, 3]]).reshape(2, lru_w)
    grads["lru_lambda"] = jnp.transpose(dlam.reshape(n_blk, 2, LANES), (1, 0, 2)).reshape(2, lru_w)
    grads["w_lru_conv"], grads["b_lru_conv"] = dwconv_bwd_w(rows, ux, dxr, lru_taps, LRU_PAD_LEFT, "lru_conv_dw")
    dux = dwconv(rows, dxr, full["w_lru_conv"][::-1], None, lru_taps - 1 - LRU_PAD_LEFT, BF16, "lru_conv_dx")

    dproj = jnp.concatenate([dpvg, dux, dgl, dpgc], axis=1)
    grads["b_in"] = col_sum(rows, dproj, "in_proj_db")
    grads["w_in"] = matmul(h2, dproj, "tn", BF16, "in_proj_dw")
    dproj = scatter_behind(['w_in'], "in_proj", 4, settle(['w_out', 'w_conf_out', 'w_lru_out'], dproj))
    dh2 = matmul(dproj, wb["w_in"], "nt", F32, "in_proj_dx")
    dx1, grads["g_n2"], dmv[3], dmv[4], df1, dmv[2] = norm_mod_bwd(rows, x1, dh2, dx2, g_n2, mv, 3, 4, "norm2_bwd", then_gate=(f1, 2, 0.5))

    dx0, grads["g_n1"] = ffn_bwd("ffn1", dx1, df1, x0, h1, gu1, act1, wb["w_ffn1_up"], wb["w_ffn1_down"], g_n1, (0, 1), 6,
                                 settle_first=['w_in'], latent_only=True)
    grad_x = dx0[None]

    chip_sums = [sum_parts(from_chips[n].reshape(4, -1, from_chips[n].shape[-1]), f"chip_sum_{n}") for n in big]
    out = {}
    for n, p in zip(big, chip_sums):
        out[n] = adamw_pair(p, flat2(wt[n]), flat2(mom[n]), flat2(var[n]), f"adamw_{n}")

    dmod = jnp.concatenate(dmv, axis=1)
    small_names = ['g_n1', 'g_n2', 'b_in', 'b_dw', 'g_ln', 'b_ln', 'b_lru_conv', 'g_n3', 'g_final'] + SMALL_SHARDED
    gate_names = ['w_rec_gate', 'w_in_gate']
    small_list = [loss_part] + [grads[n] for n in small_names] + [dmod[0], dmod[1]]
    small_buf = _pack(small_list, 8 * 2 * SUBLANES)
    eighths = [small_buf.reshape(8, -1, LANES)] + [grads[n].reshape(8, -1, LANES) for n in gate_names]
    *parts, dmod_all = exchange("all", eighths + [_pack([dmod[0]])], [True] * len(eighths) + [False], "scatter_small_grads")
    sums = exchange("all", [sum_parts(p, f"sum_small_grads{k}") for k, p in enumerate(parts)], False, "gather_small_grads")
    small_sum = _unpack(sums[0].reshape(small_buf.shape), [a.shape for a in small_list])
    loss = small_sum[0][0, 0]
    total = dict(zip(small_names, small_sum[1:]))
    total.update({n: s.reshape(grads[n].shape) for n, s in zip(gate_names, sums[1:])})
    dmod_all = _unpack(dmod_all, [dmod[0].shape])[0].reshape(8, N_MOD * d)
    dmc = small_sum[-1].reshape(1, N_MOD * d)
    total["b_mod"] = small_sum[-2].reshape(1, N_MOD * d) + dmc

    d9 = jnp.concatenate([dmod_all, dmc, jnp.zeros((7, N_MOD * d), F32)], axis=0)
    d9_cols = lax.dynamic_slice_in_dim(d9, chip * w_mod.shape[2], w_mod.shape[2], axis=1)
    g_wmod, dc9 = mod_matmul_bwd(c9, d9_cols, w_mod[0], "mod_matmul_bwd")
    dc_all = exchange("all", [dc9[8:16]], False, "gather_dc")[0]
    total["c_ctx"] = sum_parts(jnp.stack([dc_all[0], dc_all[2], dc_all[4], dc_all[6]]), "sum_dc")[0:1]
    out["w_mod"] = adamw(g_wmod[None], w_mod[0], m_w_mod[0], v_w_mod[0], "adamw_w_mod")

    for n in gate_names:
        out[n] = adamw(total[n].reshape(1, -1, LANES), flat2(wt[n]), flat2(mom[n]), flat2(var[n]), f"adamw_{n}")

    small_all_names = [n for n in W_NAMES if n not in out]
    g_local, shapes = [], []
    for n in small_all_names:
        g = total[n].reshape(flat2(wt[n]).shape[:-1] + (-1,)) if n in SMALL_SHARDED else total[n].reshape(flat2(wt[n]).shape)
        if n in SMALL_SHARDED:
            width = wt[n].shape[-1]
            g = lax.dynamic_slice_in_dim(g, chip * width, width, axis=1)
        g_local.append(g)
        shapes.append(g.shape)
    packed = adamw(_pack(g_local)[None], _pack([flat2(wt[n]) for n in small_all_names]),
                   _pack([flat2(mom[n]) for n in small_all_names]), _pack([flat2(var[n]) for n in small_all_names]), "adamw_small")
    for k, n in enumerate(small_all_names):
        out[n] = [_unpack(p, shapes)[k] for p in packed]

    results = [loss, grad_x]
    for k in range(4):
        results += [out[n][k].reshape(wt[n].shape) for n in W_NAMES]
    return tuple(results)
```

```python
import functools

import jax
import jax.numpy as jnp
from jax import lax
from jax.experimental import pallas as pl
from jax.experimental.pallas import tpu as pltpu
from jax.experimental.pallas import tpu_sc as plsc

F32 = jnp.float32
BF16 = jnp.bfloat16
MESH = pl.DeviceIdType.MESH

EPS = 1e-6
GRID_W = 64
N_MOD = 9
LRU_C = 8.0
LRU_PAD_LEFT = 2
ADAM_LR, ADAM_B1, ADAM_B2, ADAM_EPS, ADAM_WD, ADAM_STEP = 0.001, 0.9, 0.999, 1e-08, 0.01, 10

LANES = 128
SUBLANES = 8
HALO = 16
VMEM_LIMIT = 56 * 1024 * 1024

W_NAMES = ['c_ctx', 'w_mod', 'b_mod', 'g_n1', 'w_ffn1_up', 'w_ffn1_down', 'g_n2', 'w_in', 'b_in', 'w_dw', 'b_dw',
           'g_ln', 'b_ln', 'w_conf_out', 'w_lru_conv', 'b_lru_conv', 'w_rec_gate', 'b_rec_gate', 'w_in_gate',
           'b_in_gate', 'lru_lambda', 'w_lru_out', 'w_out', 'g_n3', 'w_ffn2_up', 'w_ffn2_down', 'g_final']
COL_SHARDED = ['w_ffn1_up', 'w_in', 'w_ffn2_up']
ROW_SHARDED = ['w_ffn1_down', 'w_conf_out', 'w_lru_out', 'w_out', 'w_ffn2_down']
SMALL_SHARDED = ['w_dw', 'w_lru_conv', 'b_rec_gate', 'b_in_gate', 'lru_lambda']


def _params(*semantics):
    return pltpu.CompilerParams(dimension_semantics=semantics, vmem_limit_bytes=VMEM_LIMIT)


def _pick(n, target, mult=LANES):
    best = None
    for t in range(mult, min(n, target) + 1, mult):
        if n % t == 0:
            best = t
    return best or n


def _chunks(width, target=512):
    w = _pick(width, target)
    return [slice(s, s + w) for s in range(0, width, w)]


def _sigmoid(x):
    return jax.nn.sigmoid(x)


def _silu_and_grad(x):
    s = _sigmoid(x)
    return x * s, s * (1.0 + x * (1.0 - s))


_GELU_K = 0.7978845608028654


def _gelu_and_grad(x):
    t = jnp.tanh(_GELU_K * (x + 0.044715 * (x * x * x)))
    return 0.5 * x * (1.0 + t), 0.5 * (1.0 + t) + 0.5 * x * (1.0 - t * t) * _GELU_K * (1.0 + 3 * 0.044715 * x * x)


def _neg_expm1(z):
    series = -(z * (1.0 + z * (0.5 + z * (1.0 / 6.0))))
    return jnp.where(z > -0.01, series, 1.0 - jnp.exp(z))


def _softplus(x):
    return jnp.maximum(x, 0.0) + jnp.log1p(jnp.exp(-jnp.abs(x)))


def _accumulate(ref, value, first):
    @pl.when(first)
    def _():
        ref[...] = value

    @pl.when(jnp.logical_not(first))
    def _():
        ref[...] += value


def _colsum(x):
    return jnp.sum(x, axis=0, keepdims=True)


class Rows:
    def __init__(self, t_lat, t_ctx, d_model):
        self.tm = _pick(t_ctx, 256, HALO)
        assert t_lat % self.tm == 0 and t_ctx % self.tm == 0
        self.n_lat = t_lat // self.tm
        self.n_all = (t_lat + t_ctx) // self.tm
        self.t_all = t_lat + t_ctx
        self.d = d_model

    def seg(self, i):
        return jnp.where(i >= self.n_lat, 1, 0)

    def seg_first(self, i):
        return jnp.logical_or(i == 0, i == self.n_lat)

    def seg_last(self, i):
        return jnp.logical_or(i == self.n_lat - 1, i == self.n_all - 1)

    def row(self, width, col=0):
        return pl.BlockSpec((self.tm, width), lambda i: (i, col))

    def vec(self, width):
        return pl.BlockSpec((1, width), lambda i: (0, 0))

    def full(self, shape):
        return pl.BlockSpec(shape, lambda i: (0,) * len(shape))

    def mod(self):
        return pl.BlockSpec((None, N_MOD, self.d), lambda i: (self.seg(i), 0, 0))

    def segsum(self):
        return pl.BlockSpec((None, 1, self.d), lambda i: (self.seg(i), 0, 0))

    def segsum_shape(self):
        return jax.ShapeDtypeStruct((2, 1, self.d), F32)

    def halo(self, width, side):
        per = self.tm // HALO
        last = self.t_all // HALO - 1
        if side < 0:
            return pl.BlockSpec((HALO, width), lambda i: (jnp.maximum(i * per - 1, 0), 0))
        return pl.BlockSpec((HALO, width), lambda i: (jnp.minimum((i + 1) * per, last), 0))

    def call(self, body, name, in_specs, out_specs, out_shape, scratch=()):
        return pl.pallas_call(body, name=name, grid=(self.n_all,), in_specs=in_specs, out_specs=out_specs,
                              out_shape=out_shape, scratch_shapes=list(scratch), compiler_params=_params("arbitrary"))


_PEER_FLIPS = {
    "chips": [(1, 0, 0), (0, 1, 0), (1, 1, 0)],
    "sibling": [(0, 0, 1)],
    "all": [(0, 0, 1), (0, 1, 0), (0, 1, 1), (1, 0, 0), (1, 0, 1), (1, 1, 0), (1, 1, 1)],
}


def _slot(kind, x, y, c):
    return {"chips": 2 * x + y, "sibling": c, "all": 4 * x + 2 * y + c}[kind]


def exchange(kind, srcs, indexed, name):
    flips = _PEER_FLIPS[kind]
    n_slots = len(flips) + 1
    n_t = len(srcs)
    by_slot = list(indexed) if isinstance(indexed, (list, tuple)) else [indexed] * n_t
    out_shapes = [jax.ShapeDtypeStruct(s.shape if ix else (n_slots,) + s.shape, s.dtype) for s, ix in zip(srcs, by_slot)]

    def body(*refs):
        src_refs, dst_refs = refs[:n_t], refs[n_t:2 * n_t]
        send_sems, recv_sems, local_sems = refs[2 * n_t:]
        x, y, c = lax.axis_index("x"), lax.axis_index("y"), lax.axis_index("c")
        me = _slot(kind, x, y, c)

        def piece(t, k):
            return src_refs[t].at[k] if by_slot[t] else src_refs[t]

        peers = [(1 - x if fx else x, 1 - y if fy else y, 1 - c if fc else c) for fx, fy, fc in flips]
        local = [pltpu.make_async_copy(piece(t, me), dst_refs[t].at[me], local_sems.at[t]) for t in range(n_t)]
        for cp in local:
            cp.start()
        sends = []
        for t in range(n_t):
            for p, peer in enumerate(peers):
                sends.append(pltpu.make_async_remote_copy(
                    src_ref=piece(t, _slot(kind, *peer)), dst_ref=dst_refs[t].at[me],
                    send_sem=send_sems.at[t, p], recv_sem=recv_sems.at[t, p], device_id=peer, device_id_type=MESH))
        for cp in sends:
            cp.start()
        for t in range(n_t):
            for p, peer in enumerate(peers):
                theirs = _slot(kind, *peer)
                pltpu.make_async_remote_copy(
                    src_ref=piece(t, me), dst_ref=dst_refs[t].at[theirs],
                    send_sem=send_sems.at[t, p], recv_sem=recv_sems.at[t, p], device_id=peer, device_id_type=MESH).wait_recv()
        for cp in sends:
            cp.wait_send()
        for cp in local:
            cp.wait()

    any_spec = pl.BlockSpec(memory_space=pl.ANY)
    outs = pl.pallas_call(
        body, name=name, out_shape=out_shapes, in_specs=[any_spec] * n_t, out_specs=[any_spec] * n_t,
        scratch_shapes=[pltpu.SemaphoreType.DMA((n_t, len(flips))), pltpu.SemaphoreType.DMA((n_t, len(flips))),
                        pltpu.SemaphoreType.DMA((n_t,))],
        compiler_params=pltpu.CompilerParams(has_side_effects=True),
    )(*srcs)
    return list(outs)


def exchange_by_sequencer(srcs, indexed, name, collective_id):
    flips = _PEER_FLIPS["chips"]
    n_t, n_p = len(srcs), len(flips)
    src_refs = [jax.new_ref(s, memory_space=pltpu.MemorySpace.HBM) for s in srcs]
    land_refs = [jax.empty_ref(jax.ShapeDtypeStruct(s.shape if indexed else (n_p + 1,) + s.shape, s.dtype),
                               memory_space=pltpu.MemorySpace.HBM) for s in srcs]

    @pl.kernel(mesh=plsc.ScalarSubcoreMesh(axis_name="sequencer", num_cores=1), name=name,
               scratch_types=(pltpu.SemaphoreType.DMA((n_t, n_p)), pltpu.SemaphoreType.DMA((n_t, n_p)), pltpu.SemaphoreType.DMA((n_t,))),
               compiler_params=pltpu.CompilerParams(collective_id=collective_id))
    def launch(send_sems, recv_sems, local_sems):
        x, y, c = lax.axis_index("x"), lax.axis_index("y"), lax.axis_index("c")
        peers = [(1 - x if fx else x, 1 - y if fy else y, c) for fx, fy, _ in flips]
        barrier = pltpu.get_barrier_semaphore()
        for peer in peers:
            pl.semaphore_signal(barrier, inc=1, device_id=peer, device_id_type=MESH)
        pl.semaphore_wait(barrier, n_p)
        me = _slot("chips", x, y, c)

        def piece(t, k):
            return src_refs[t].at[k] if indexed else src_refs[t]

        local = [pltpu.make_async_copy(piece(t, me), land_refs[t].at[me], local_sems.at[t]) for t in range(n_t)]
        for cp in local:
            cp.start()
        sends = []
        for t in range(n_t):
            for p, peer in enumerate(peers):
                sends.append(pltpu.make_async_remote_copy(
                    src_ref=piece(t, _slot("chips", *peer)), dst_ref=land_refs[t].at[me],
                    send_sem=send_sems.at[t, p], recv_sem=recv_sems.at[t, p], device_id=peer, device_id_type=MESH))
        for cp in sends:
            cp.start()
        for t in range(n_t):
            for p, peer in enumerate(peers):
                pltpu.make_async_remote_copy(
                    src_ref=piece(t, me), dst_ref=land_refs[t].at[_slot("chips", *peer)],
                    send_sem=send_sems.at[t, p], recv_sem=recv_sems.at[t, p], device_id=peer, device_id_type=MESH).wait_recv()
        for cp in sends:
            cp.wait_send()
        for cp in local:
            cp.wait()

    launch()
    return [r[...] for r in land_refs]


MATMUL_VMEM_BUDGET = 44 * 1024 * 1024
MATMUL_STEP_BYTES = 1024 * 1024


def _divisors(n, most):
    return [t for t in range(LANES, min(n, most) + 1, LANES) if n % t == 0] or [n]


def _matmul_tiles(n_i, n_j, n_r, a_bytes, b_bytes, out_bytes):
    best = None
    for tr in _divisors(n_r, n_r):
        steps = n_r // tr
        for ti in _divisors(n_i, 1408):
            for tj in _divisors(n_j, 1664):
                vmem = 2 * (ti * tr * a_bytes + tr * tj * b_bytes + ti * tj * out_bytes) + (ti * tj * 4 if steps > 1 else 0)
                if vmem > MATMUL_VMEM_BUDGET or ti < min(n_i, 2 * LANES) or tj < min(n_j, 4 * LANES):
                    continue
                size_a, size_b = n_i * n_r * a_bytes, n_r * n_j * b_bytes
                if steps == 1:
                    moved = min(size_a + (n_i // ti) * size_b, size_b + (n_j // tj) * size_a)
                else:
                    moved = (n_j // tj) * size_a + (n_i // ti) * size_b + (steps - 1) * n_i * n_j * 8
                moved += n_i * n_j * out_bytes + (n_i // ti) * (n_j // tj) * steps * MATMUL_STEP_BYTES
                if best is None or moved < best[0]:
                    best = (moved, ti, tj, tr)
    return best[1:]


def matmul(a, b, mode, out_dtype, name, bias=None):
    if mode == "nn":
        (n_i, n_r), (_, n_j) = a.shape, b.shape
    elif mode == "nt":
        (n_i, n_r), (n_j, _) = a.shape, b.shape
    else:
        (n_r, n_i), (_, n_j) = a.shape, b.shape
    ti, tj, tr = _matmul_tiles(n_i, n_j, n_r, a.dtype.itemsize, b.dtype.itemsize, jnp.dtype(out_dtype).itemsize)
    steps = n_r // tr
    bytes_a, bytes_b = a.size * a.dtype.itemsize, b.size * b.dtype.itemsize
    j_outer = steps == 1 and bytes_b + (n_j // tj) * bytes_a < bytes_a + (n_i // ti) * bytes_b

    def at(fn):
        return (lambda j, i, r: fn(i, j, r)) if j_outer else fn

    a_spec = {"nn": pl.BlockSpec((ti, tr), at(lambda i, j, r: (i, r))), "nt": pl.BlockSpec((ti, tr), at(lambda i, j, r: (i, r))),
              "tn": pl.BlockSpec((tr, ti), at(lambda i, j, r: (r, i)))}[mode]
    b_spec = {"nn": pl.BlockSpec((tr, tj), at(lambda i, j, r: (r, j))), "nt": pl.BlockSpec((tj, tr), at(lambda i, j, r: (j, r))),
              "tn": pl.BlockSpec((tr, tj), at(lambda i, j, r: (r, j)))}[mode]
    dims = {"nn": (((1,), (0,)), ((), ())), "nt": (((1,), (1,)), ((), ())), "tn": (((0,), (0,)), ((), ()))}[mode]

    def body(*refs):
        a_ref, b_ref = refs[:2]
        bias_ref = refs[2] if bias is not None else None
        o_ref = refs[3] if bias is not None else refs[2]
        prod = lax.dot_general(a_ref[...].astype(BF16), b_ref[...].astype(BF16), dims, preferred_element_type=F32)

        def finish(total):
            if bias_ref is not None:
                total = total + bias_ref[...]
            o_ref[...] = total.astype(out_dtype)

        if steps == 1:
            finish(prod)
        else:
            acc = refs[-1]
            r = pl.program_id(2)

            @pl.when(r == 0)
            def _():
                acc[...] = prod

            @pl.when(r > 0)
            def _():
                acc[...] += prod

            @pl.when(r == steps - 1)
            def _():
                finish(acc[...])

    in_specs = [a_spec, b_spec] + ([pl.BlockSpec((1, tj), at(lambda i, j, r: (0, j)))] if bias is not None else [])
    args = (a, b) + ((bias,) if bias is not None else ())
    grid = (n_j // tj, n_i // ti, steps) if j_outer else (n_i // ti, n_j // tj, steps)
    return pl.pallas_call(
        body, name=name, grid=grid, in_specs=in_specs,
        out_specs=pl.BlockSpec((ti, tj), at(lambda i, j, r: (i, j))), out_shape=jax.ShapeDtypeStruct((n_i, n_j), out_dtype),
        scratch_shapes=[pltpu.VMEM((ti, tj), F32)] if steps > 1 else [],
        compiler_params=_params("arbitrary", "arbitrary", "arbitrary"),
    )(*args)


def matmul_swiglu(h, w_up, name):
    n_t, d = h.shape
    f = w_up.shape[1] // 2
    ti, tj = _pick(n_t, 384), _pick(f, 1408)
    n_j = f // tj

    def body(h_ref, wg_ref, wu_ref, g_ref, u_ref, act_ref):
        hv = h_ref[...]
        g = jnp.dot(hv, wg_ref[...], preferred_element_type=F32)
        u = jnp.dot(hv, wu_ref[...], preferred_element_type=F32)
        g_ref[...] = g.astype(BF16)
        u_ref[...] = u.astype(BF16)
        act_ref[...] = (_silu_and_grad(g)[0] * u).astype(BF16)

    out = pl.BlockSpec((ti, tj), lambda j, i: (i, j))
    shape = jax.ShapeDtypeStruct((n_t, f), BF16)
    return pl.pallas_call(
        body, name=name, grid=(n_j, n_t // ti),
        in_specs=[pl.BlockSpec((ti, d), lambda j, i: (i, 0)), pl.BlockSpec((d, tj), lambda j, i: (0, j)),
                  pl.BlockSpec((d, tj), lambda j, i: (0, j + n_j))],
        out_specs=[out, out, out], out_shape=[shape, shape, shape], compiler_params=_params("arbitrary", "arbitrary"),
    )(h, w_up, w_up)


def matmul_swiglu_bwd(df, w_down, g, u, name):
    n_t, d = df.shape
    f = w_down.shape[0]
    ti, tj = _pick(n_t, 768), _pick(f, 1408)

    def body(df_ref, w_ref, g_ref, u_ref, dg_ref, du_ref):
        dact = lax.dot_general(df_ref[...], w_ref[...], (((1,), (1,)), ((), ())), preferred_element_type=F32)
        act, dact_dg = _silu_and_grad(g_ref[...].astype(F32))
        dg_ref[...] = (dact * u_ref[...].astype(F32) * dact_dg).astype(BF16)
        du_ref[...] = (dact * act).astype(BF16)

    tile = pl.BlockSpec((ti, tj), lambda j, i: (i, j))
    shape = jax.ShapeDtypeStruct((n_t, f), BF16)
    return pl.pallas_call(
        body, name=name, grid=(f // tj, n_t // ti),
        in_specs=[pl.BlockSpec((ti, d), lambda j, i: (i, 0)), pl.BlockSpec((tj, d), lambda j, i: (j, 0)), tile, tile],
        out_specs=[tile, tile], out_shape=[shape, shape], compiler_params=_params("arbitrary", "arbitrary"))(df, w_down, g, u)


def matmul_groups_nt(parts, weights, name):
    n_g = len(parts)
    n_t, d = parts[0].shape[0], weights[0].shape[0]
    ti = _pick(n_t, 384)
    dims = (((1,), (1,)), ((), ()))

    def body(*refs):
        a_refs, b_refs, o_ref, sum_refs = refs[:n_g], refs[n_g:2 * n_g], refs[2 * n_g], refs[2 * n_g + 1:]
        i = pl.program_id(0)
        total = None
        for a_ref, b_ref, s_ref in zip(a_refs, b_refs, sum_refs):
            av = a_ref[...]
            prod = lax.dot_general(av, b_ref[...], dims, preferred_element_type=F32)
            total = prod if total is None else total + prod
            _accumulate(s_ref, _colsum(av.astype(F32)), i == 0)
        o_ref[...] = total

    widths = [p.shape[1] for p in parts]
    outs = pl.pallas_call(
        body, name=name, grid=(n_t // ti,),
        in_specs=[pl.BlockSpec((ti, w), lambda i: (i, 0)) for w in widths] + [pl.BlockSpec((d, w), lambda i: (0, 0)) for w in widths],
        out_specs=[pl.BlockSpec((ti, d), lambda i: (i, 0))] + [pl.BlockSpec((1, w), lambda i: (0, 0)) for w in widths],
        out_shape=[jax.ShapeDtypeStruct((n_t, d), F32)] + [jax.ShapeDtypeStruct((1, w), F32) for w in widths],
        compiler_params=_params("arbitrary"))(*parts, *weights)
    return outs[0], list(outs[1:])


def matmul_pair_nt(a1, a2, b, name):
    n_t, f = a1.shape
    d = b.shape[0]
    ti = _pick(n_t, 768)
    dims = (((1,), (1,)), ((), ()))

    def body(a1_ref, a2_ref, b1_ref, b2_ref, o_ref):
        o_ref[...] = (lax.dot_general(a1_ref[...], b1_ref[...], dims, preferred_element_type=F32)
                      + lax.dot_general(a2_ref[...], b2_ref[...], dims, preferred_element_type=F32))

    rows_in = pl.BlockSpec((ti, f), lambda i: (i, 0))
    return pl.pallas_call(
        body, name=name, grid=(n_t // ti,),
        in_specs=[rows_in, rows_in, pl.BlockSpec((d, f), lambda i: (0, 0)), pl.BlockSpec((d, f), lambda i: (0, 1))],
        out_specs=pl.BlockSpec((ti, d), lambda i: (i, 0)), out_shape=jax.ShapeDtypeStruct((n_t, d), F32),
        compiler_params=_params("arbitrary"))(a1, a2, b, b)


def norm_mod(rows, x, g, mv, shift_k, scale_k, name, resid=None):
    d = rows.d
    changed = resid is not None

    def body(*refs):
        refs = list(refs)
        x_ref, g_ref, mv_ref = refs[:3]
        rest = refs[3:]
        xv = x_ref[...]
        if resid is not None:
            xv = xv + resid[2] * mv_ref[resid[1]:resid[1] + 1, :] * rest.pop(0)[...]
        if changed:
            rest.pop(0)[...] = xv
        r = lax.rsqrt(jnp.mean(xv * xv, axis=-1, keepdims=True) + EPS)
        h = (xv * r) * g_ref[...]
        if shift_k is not None:
            h = h * (1.0 + mv_ref[scale_k:scale_k + 1, :]) + mv_ref[shift_k:shift_k + 1, :]
        rest.pop(0)[...] = h.astype(BF16)

    ins = [x, g, mv] + ([resid[0]] if resid is not None else [])
    in_specs = [rows.row(d), rows.vec(d), rows.mod()] + [rows.row(d)] * (len(ins) - 3)
    out_shape = ([jax.ShapeDtypeStruct((rows.t_all, d), F32)] if changed else []) + [jax.ShapeDtypeStruct((rows.t_all, d), BF16)]
    outs = rows.call(body, name, in_specs, [rows.row(d)] * len(out_shape), out_shape)(*ins)
    return (outs[0], outs[1]) if changed else (None, outs[0])


def embed_norm_mod(rows, x_lat, x_ctx, row_tab, col_tab, g, mv, shift_k, scale_k, name):
    d = rows.d
    half = d // 2
    per_tile = rows.tm // GRID_W
    assert rows.tm % GRID_W == 0 and per_tile <= SUBLANES

    def body(xl_ref, xc_ref, rt_ref, ct_ref, g_ref, mv_ref, x0_ref, h_ref):
        i = pl.program_id(0)
        by_row = jnp.concatenate([jnp.broadcast_to(rt_ref[r:r + 1, :], (GRID_W, half)) for r in range(per_tile)], axis=0)
        by_col = jnp.concatenate([ct_ref[...]] * per_tile, axis=0)
        xv = jnp.where(i < rows.n_lat, xl_ref[...] + jnp.concatenate([by_row, by_col], axis=1), xc_ref[...])
        x0_ref[...] = xv
        r = lax.rsqrt(jnp.mean(xv * xv, axis=-1, keepdims=True) + EPS)
        h = (xv * r) * g_ref[...]
        h_ref[...] = (h * (1.0 + mv_ref[scale_k:scale_k + 1, :]) + mv_ref[shift_k:shift_k + 1, :]).astype(BF16)

    last = rows.n_lat - 1
    in_specs = [pl.BlockSpec((rows.tm, d), lambda i: (jnp.minimum(i, last), 0)),
                pl.BlockSpec((rows.tm, d), lambda i: (jnp.maximum(i - rows.n_lat, 0), 0)),
                pl.BlockSpec((SUBLANES, half), lambda i: (jnp.minimum(i, last), 0)), rows.full(col_tab.shape), rows.vec(d), rows.mod()]
    return rows.call(body, name, in_specs, [rows.row(d), rows.row(d)],
                     [jax.ShapeDtypeStruct((rows.t_all, d), F32), jax.ShapeDtypeStruct((rows.t_all, d), BF16)])(
                         x_lat, x_ctx, row_tab, col_tab, g, mv)


def norm_mod_bwd(rows, x, dh, dxn, g, mv, shift_k, scale_k, name, latent_only=False, then_gate=None):
    d = rows.d
    n_dx = rows.n_lat if latent_only else rows.n_all

    def body(*refs):
        x_ref, dh_ref, dxn_ref, g_ref, mv_ref = refs[:5]
        dx_ref, dg_ref, dsh_ref, dsc_ref = refs[-4:] if then_gate is None else refs[6:10]
        i = pl.program_id(0)
        xv, dhv, gv = x_ref[...], dh_ref[...], g_ref[...]
        r = lax.rsqrt(jnp.mean(xv * xv, axis=-1, keepdims=True) + EPS)
        n = xv * r
        dy = dhv * (1.0 + mv_ref[scale_k:scale_k + 1, :])
        dn = dy * gv
        dx = dxn_ref[...] + r * (dn - n * jnp.mean(dn * n, axis=-1, keepdims=True))

        @pl.when(i < n_dx)
        def _():
            dx_ref[...] = dx

        _accumulate(dg_ref, _colsum(dy * n), i == 0)
        _accumulate(dsh_ref, _colsum(dhv), rows.seg_first(i))
        _accumulate(dsc_ref, _colsum(dhv * (n * gv)), rows.seg_first(i))
        if then_gate is not None:
            _gate_bwd(rows, i, dx, refs[5], mv_ref, then_gate[1], then_gate[2], refs[10], refs[11])

    dx_spec = pl.BlockSpec((rows.tm, d), lambda i: (jnp.minimum(i, n_dx - 1), 0))
    ins = [x, dh, dxn, g, mv] + ([then_gate[0]] if then_gate is not None else [])
    in_specs = [rows.row(d), rows.row(d), rows.row(d), rows.vec(d), rows.mod()] + ([rows.row(d)] if then_gate is not None else [])
    out_specs = [dx_spec, rows.vec(d), rows.segsum(), rows.segsum()]
    out_shape = [jax.ShapeDtypeStruct((n_dx * rows.tm, d), F32), jax.ShapeDtypeStruct((1, d), F32), rows.segsum_shape(), rows.segsum_shape()]
    if then_gate is not None:
        out_specs += [rows.row(d), rows.segsum()]
        out_shape += [jax.ShapeDtypeStruct((rows.t_all, d), BF16), rows.segsum_shape()]
    return rows.call(body, name, in_specs, out_specs, out_shape)(*ins)


def _gate_bwd(rows, i, dx, f_ref, mv_ref, gate_k, scale, df_ref, dgate_ref):
    df_ref[...] = (scale * mv_ref[gate_k:gate_k + 1, :] * dx).astype(BF16)
    _accumulate(dgate_ref, scale * _colsum(dx * f_ref[...]), rows.seg_first(i))


def glu(rows, pvg, name):
    w = pvg.shape[1] // 2

    def body(v_ref, t_ref, o_ref):
        for cs in _chunks(w):
            o_ref[:, cs] = v_ref[:, cs].astype(F32) * _sigmoid(t_ref[:, cs].astype(F32))

    return rows.call(body, name, [rows.row(w, 0), rows.row(w, 1)], rows.row(w), jax.ShapeDtypeStruct((rows.t_all, w), F32))(pvg, pvg)


def glu_bwd(rows, pvg, du, name):
    w = pvg.shape[1] // 2

    def body(v_ref, t_ref, du_ref, o_ref):
        for cs in _chunks(w):
            s = _sigmoid(t_ref[:, cs].astype(F32))
            duv = du_ref[:, cs]
            o_ref[:, cs] = (duv * s).astype(BF16)
            o_ref[:, slice(w + cs.start, w + cs.stop)] = (duv * v_ref[:, cs].astype(F32) * s * (1.0 - s)).astype(BF16)

    return rows.call(body, name, [rows.row(w, 0), rows.row(w, 1), rows.row(w)], rows.row(2 * w),
                     jax.ShapeDtypeStruct((rows.t_all, 2 * w), BF16))(pvg, pvg, du)


def ln_silu(rows, u, g, b, name):
    w = u.shape[1]

    def body(u_ref, g_ref, b_ref, o_ref):
        uv = u_ref[...]
        xc = uv - jnp.mean(uv, axis=-1, keepdims=True)
        n = xc * lax.rsqrt(jnp.mean(xc * xc, axis=-1, keepdims=True) + EPS)
        o_ref[...] = _silu_and_grad(n * g_ref[...] + b_ref[...])[0].astype(BF16)

    return rows.call(body, name, [rows.row(w), rows.vec(w), rows.vec(w)], rows.row(w),
                     jax.ShapeDtypeStruct((rows.t_all, w), BF16))(u, g, b)


def ln_silu_bwd(rows, u, ds, g, b, name):
    w = u.shape[1]

    def body(u_ref, ds_ref, g_ref, b_ref, du_ref, dg_ref, db_ref):
        i = pl.program_id(0)
        uv, gv = u_ref[...], g_ref[...]
        xc = uv - jnp.mean(uv, axis=-1, keepdims=True)
        r = lax.rsqrt(jnp.mean(xc * xc, axis=-1, keepdims=True) + EPS)
        n = xc * r
        dy = ds_ref[...].astype(F32) * _silu_and_grad(n * gv + b_ref[...])[1]
        dn = dy * gv
        du_ref[...] = r * (dn - jnp.mean(dn, axis=-1, keepdims=True) - n * jnp.mean(dn * n, axis=-1, keepdims=True))
        _accumulate(dg_ref, _colsum(dy * n), i == 0)
        _accumulate(db_ref, _colsum(dy), i == 0)

    return rows.call(body, name, [rows.row(w), rows.row(w), rows.vec(w), rows.vec(w)], [rows.row(w), rows.vec(w), rows.vec(w)],
                     [jax.ShapeDtypeStruct((rows.t_all, w), F32), jax.ShapeDtypeStruct((1, w), F32),
                      jax.ShapeDtypeStruct((1, w), F32)])(u, ds, g, b)


def lru_merge(rows, h, gl, name):
    w = gl.shape[1]

    def body(h_ref, gl_ref, o_ref):
        for cs in _chunks(w):
            o_ref[:, cs] = ((h_ref[0, :, cs] + h_ref[1, :, cs]) * _gelu_and_grad(gl_ref[:, cs].astype(F32))[0]).astype(BF16)

    return rows.call(body, name, [pl.BlockSpec((2, rows.tm, w), lambda i: (0, i, 0)), rows.row(w)], rows.row(w),
                     jax.ShapeDtypeStruct((rows.t_all, w), BF16))(h, gl)


def lru_merge_bwd(rows, h, gl, dy, name):
    w = gl.shape[1]

    def body(h_ref, gl_ref, dy_ref, dh_ref, dgl_ref):
        for cs in _chunks(w):
            act, dact = _gelu_and_grad(gl_ref[:, cs].astype(F32))
            dyv = dy_ref[:, cs].astype(F32)
            dh_ref[:, cs] = dyv * act
            dgl_ref[:, cs] = (dyv * (h_ref[0, :, cs] + h_ref[1, :, cs]) * dact).astype(BF16)

    return rows.call(body, name, [pl.BlockSpec((2, rows.tm, w), lambda i: (0, i, 0)), rows.row(w), rows.row(w)],
                     [rows.row(w), rows.row(w)],
                     [jax.ShapeDtypeStruct((rows.t_all, w), F32), jax.ShapeDtypeStruct((rows.t_all, w), BF16)])(h, gl, dy)


def z_merge(rows, pgc, yc, yl, name):
    d = rows.d

    def body(gc_ref, gr_ref, yc_ref, yl_ref, o_ref):
        for cs in _chunks(d):
            o_ref[:, cs] = (_sigmoid(gc_ref[:, cs].astype(F32)) * yc_ref[:, cs].astype(F32)
                            + _sigmoid(gr_ref[:, cs].astype(F32)) * yl_ref[:, cs].astype(F32)).astype(BF16)

    return rows.call(body, name, [rows.row(d, 0), rows.row(d, 1), rows.row(d), rows.row(d)], rows.row(d),
                     jax.ShapeDtypeStruct((rows.t_all, d), BF16))(pgc, pgc, yc, yl)


def z_merge_bwd(rows, pgc, yc, yl, dz, name):
    d = rows.d

    def body(gc_ref, gr_ref, yc_ref, yl_ref, dz_ref, dyc_ref, dyl_ref, dp_ref):
        for cs in _chunks(d):
            sc, sr = _sigmoid(gc_ref[:, cs].astype(F32)), _sigmoid(gr_ref[:, cs].astype(F32))
            dzv = dz_ref[:, cs].astype(F32)
            dyc_ref[:, cs] = (dzv * sc).astype(BF16)
            dyl_ref[:, cs] = (dzv * sr).astype(BF16)
            dp_ref[:, cs] = (dzv * yc_ref[:, cs].astype(F32) * sc * (1.0 - sc)).astype(BF16)
            dp_ref[:, slice(d + cs.start, d + cs.stop)] = (dzv * yl_ref[:, cs].astype(F32) * sr * (1.0 - sr)).astype(BF16)

    return rows.call(body, name, [rows.row(d, 0), rows.row(d, 1), rows.row(d), rows.row(d), rows.row(d)],
                     [rows.row(d), rows.row(d), rows.row(2 * d)],
                     [jax.ShapeDtypeStruct((rows.t_all, d), BF16), jax.ShapeDtypeStruct((rows.t_all, d), BF16),
                      jax.ShapeDtypeStruct((rows.t_all, 2 * d), BF16)])(pgc, pgc, yc, yl, dz)


def loss_head(rows, x, f, mv, g, target, name):
    d = rows.d

    def body(x_ref, f_ref, mv_ref, g_ref, t_ref, loss_ref, dx_ref, dg_ref, df_ref, dgate_ref):
        i = pl.program_id(0)
        valid = jnp.where(i < rows.n_lat, 1.0, 0.0)
        xv = x_ref[...] + 0.5 * mv_ref[8:9, :] * f_ref[...]
        gv = g_ref[...]
        r = lax.rsqrt(jnp.mean(xv * xv, axis=-1, keepdims=True) + EPS)
        n = xv * r
        err = (n * gv - t_ref[...]) * valid
        part = 0.5 * jnp.sum(jnp.mean(err * err, axis=-1, keepdims=True), axis=0, keepdims=True)
        _accumulate(loss_ref, jnp.broadcast_to(part, (1, LANES)), i == 0)
        dy = err * (1.0 / d)
        dn = dy * gv
        dx = r * (dn - n * jnp.mean(dn * n, axis=-1, keepdims=True))
        dx_ref[...] = dx
        _accumulate(dg_ref, _colsum(dy * n), i == 0)
        _gate_bwd(rows, i, dx, f_ref, mv_ref, 8, 0.5, df_ref, dgate_ref)

    target_spec = pl.BlockSpec((rows.tm, d), lambda i: (jnp.minimum(i, rows.n_lat - 1), 0))
    return rows.call(body, name, [rows.row(d), rows.row(d), rows.mod(), rows.vec(d), target_spec],
                     [rows.vec(LANES), rows.row(d), rows.vec(d), rows.row(d), rows.segsum()],
                     [jax.ShapeDtypeStruct((1, LANES), F32), jax.ShapeDtypeStruct((rows.t_all, d), F32),
                      jax.ShapeDtypeStruct((1, d), F32), jax.ShapeDtypeStruct((rows.t_all, d), BF16), rows.segsum_shape()])(
                          x, f, mv, g, target)


def _extended(rows, i, prev_ref, cur_ref, next_ref, cs):
    prev = jnp.where(rows.seg_first(i), 0.0, prev_ref[:, cs].astype(F32))
    nxt = jnp.where(rows.seg_last(i), 0.0, next_ref[:, cs].astype(F32))
    return jnp.concatenate([prev, cur_ref[:, cs].astype(F32), nxt], axis=0)


class _Shifts:
    def __init__(self, ext, tm):
        self.ext, self.tm, self.rolled = ext, tm, {0: ext}

    def at(self, offset):
        residue = offset % SUBLANES
        if residue not in self.rolled:
            self.rolled[residue] = pltpu.roll(self.ext, self.ext.shape[0] - residue, 0)
        start = HALO + offset - residue
        return self.rolled[residue][start:start + self.tm]


def _pad_taps(w):
    k = w.shape[0]
    return jnp.pad(w, ((0, -k % SUBLANES), (0, 0)))


def dwconv(rows, u, w, b, pad_left, out_dtype, name):
    taps, width = w.shape
    wp = _pad_taps(w)

    def body(*refs):
        prev_ref, cur_ref, next_ref, w_ref = refs[:4]
        b_ref = refs[4] if b is not None else None
        o_ref = refs[-1]
        i = pl.program_id(0)
        for cs in _chunks(width, LANES):
            shifts = _Shifts(_extended(rows, i, prev_ref, cur_ref, next_ref, cs), rows.tm)
            acc = jnp.zeros((rows.tm, LANES), F32) if b_ref is None else jnp.broadcast_to(b_ref[:, cs], (rows.tm, LANES))
            for k in range(taps):
                acc = acc + w_ref[k:k + 1, cs] * shifts.at(k - pad_left)
            o_ref[:, cs] = acc.astype(out_dtype)

    ins = [u, u, u, wp] + ([b] if b is not None else [])
    in_specs = [rows.halo(width, -1), rows.row(width), rows.halo(width, 1), rows.full(wp.shape)] + ([rows.vec(width)] if b is not None else [])
    return rows.call(body, name, in_specs, rows.row(width), jax.ShapeDtypeStruct((rows.t_all, width), out_dtype))(*ins)


def dwconv_bwd_w(rows, u, dy, taps, pad_left, name):
    width = u.shape[1]
    taps_p = taps + (-taps % SUBLANES)

    def body(prev_ref, cur_ref, next_ref, dy_ref, dw_ref, db_ref):
        i = pl.program_id(0)
        tap_row = lax.broadcasted_iota(jnp.int32, (taps_p, LANES), 0)
        for cs in _chunks(width, LANES):
            shifts = _Shifts(_extended(rows, i, prev_ref, cur_ref, next_ref, cs), rows.tm)
            dyv = dy_ref[:, cs]
            total = jnp.zeros((taps_p, LANES), F32)
            for k in range(taps):
                total = total + jnp.where(tap_row == k, _colsum(dyv * shifts.at(k - pad_left)), 0.0)
            _accumulate(dw_ref.at[:, cs], total, i == 0)
            _accumulate(db_ref.at[:, cs], _colsum(dyv), i == 0)

    dw, db = rows.call(body, name, [rows.halo(width, -1), rows.row(width), rows.halo(width, 1), rows.row(width)],
                       [rows.full((taps_p, width)), rows.vec(width)],
                       [jax.ShapeDtypeStruct((taps_p, width), F32), jax.ShapeDtypeStruct((1, width), F32)])(u, u, u, dy)
    return dw[:taps], db


def _gate_values(xr, pre, lam, d):
    r = _sigmoid(pre[:, (2 * d) * LANES:(2 * d + 1) * LANES])
    ig = _sigmoid(pre[:, (2 * d + 1) * LANES:(2 * d + 2) * LANES])
    sp = _softplus(-lam[:, d * LANES:(d + 1) * LANES])
    log_a = -LRU_C * r * sp
    return r, ig, sp, jnp.exp(log_a), _neg_expm1(2.0 * log_a)


def lru_gates(rows, xr, wcat, bcat, lam, name):
    n_blk = wcat.shape[0]
    width = xr.shape[1]

    def body(xr_ref, w_ref, b_ref, lam_ref, a_ref, bb_ref):
        for h in range(n_blk):
            cs = slice(h * LANES, (h + 1) * LANES)
            xv = xr_ref[:, cs]
            pre = jnp.dot(xv.astype(BF16), w_ref[h], preferred_element_type=F32) + b_ref[h]
            for d in range(2):
                _, ig, _, a, one_minus_a2 = _gate_values(xv, pre, lam_ref[h], d)
                a_ref[d, :, cs] = a
                bb_ref[d, :, cs] = jnp.sqrt(one_minus_a2) * (ig * xv)

    both = pl.BlockSpec((2, rows.tm, width), lambda i: (0, i, 0))
    shape = jax.ShapeDtypeStruct((2, rows.t_all, width), F32)
    return rows.call(body, name, [rows.row(width), rows.full(wcat.shape), rows.full(bcat.shape), rows.full(lam.shape)],
                     [both, both], [shape, shape])(xr, wcat, bcat, lam)


def lru_gates_bwd(rows, xr, wcat, bcat, lam, da, dbb, name):
    n_blk = wcat.shape[0]
    width = xr.shape[1]

    def body(xr_ref, w_ref, b_ref, lam_ref, da_ref, dbb_ref, dxr_ref, dw_ref, db_ref, dlam_ref):
        i = pl.program_id(0)
        for h in range(n_blk):
            cs = slice(h * LANES, (h + 1) * LANES)
            xv = xr_ref[:, cs]
            xb = xv.astype(BF16)
            wv = w_ref[h]
            pre = jnp.dot(xb, wv, preferred_element_type=F32) + b_ref[h]
            dxr = jnp.zeros_like(xv)
            dpre, dlam = [], []
            for d in range(2):
                r, ig, sp, a, one_minus_a2 = _gate_values(xv, pre, lam_ref[h], d)
                inv_q = lax.rsqrt(one_minus_a2)
                dav, dbv = da_ref[d, :, cs], dbb_ref[d, :, cs]
                dbq = dbv * (one_minus_a2 * inv_q) * ig
                dlog_a = dav * a - dbv * (ig * xv) * ((a * a) * inv_q)
                dpre.append(dlog_a * (-LRU_C * sp) * (r - r * r))
                dpre.append(dbq * xv * (1.0 - ig))
                dxr = dxr + dbq
                dlam.append(_colsum(dlog_a * (-LRU_C * r)) * (-_sigmoid(-lam_ref[h][:, d * LANES:(d + 1) * LANES])))
            dpre = jnp.concatenate(dpre, axis=1)
            dpb = dpre.astype(BF16)
            dxr_ref[:, cs] = dxr + lax.dot_general(dpb, wv, (((1,), (1,)), ((), ())), preferred_element_type=F32)
            _accumulate(dw_ref.at[h], lax.dot_general(xb, dpb, (((0,), (0,)), ((), ())), preferred_element_type=F32), i == 0)
            _accumulate(db_ref.at[h], _colsum(dpre), i == 0)
            _accumulate(dlam_ref.at[h], jnp.concatenate(dlam, axis=1), i == 0)

    both = pl.BlockSpec((2, rows.tm, width), lambda i: (0, i, 0))
    return rows.call(
        body, name, [rows.row(width), rows.full(wcat.shape), rows.full(bcat.shape), rows.full(lam.shape), both, both],
        [rows.row(width), rows.full(wcat.shape), rows.full(bcat.shape), rows.full(lam.shape)],
        [jax.ShapeDtypeStruct((rows.t_all, width), F32), jax.ShapeDtypeStruct(wcat.shape, F32),
         jax.ShapeDtypeStruct(bcat.shape, F32), jax.ShapeDtypeStruct(lam.shape, F32)])(xr, wcat, bcat, lam, da, dbb)


def _tile_scan(a, b, reverse):
    n = a.shape[0]
    row = lax.broadcasted_iota(jnp.int32, a.shape, 0)
    k = 1
    while k < n:
        ok = (row < n - k) if reverse else (row >= k)
        shift = n - k if reverse else k
        b = b + a * jnp.where(ok, pltpu.roll(b, shift, 0), 0.0)
        a = a * jnp.where(ok, pltpu.roll(a, shift, 0), 1.0)
        k *= 2
    return a, b


def _chain_scan(a, b, h_in, reverse):
    n_blocks = a.shape[0] // SUBLANES
    out = [None] * n_blocks
    state = h_in
    for j in (range(n_blocks - 1, -1, -1) if reverse else range(n_blocks)):
        rows_j = slice(j * SUBLANES, (j + 1) * SUBLANES)
        cum, h0 = _tile_scan(a[rows_j], b[rows_j], reverse)
        out[j] = h0 + cum * state
        state = out[j][0:1] if reverse else out[j][SUBLANES - 1:SUBLANES]
    return jnp.concatenate(out, axis=0), state


def _neighbour(v, edge, reverse):
    n = v.shape[0]
    row = lax.broadcasted_iota(jnp.int32, v.shape, 0)
    if reverse:
        return jnp.where(row < n - 1, pltpu.roll(v, n - 1, 0), edge)
    return jnp.where(row >= 1, pltpu.roll(v, 1, 0), edge)


def _scan_call(rows, body, name, ins, in_specs, n_out, width, adjoint):
    n_all, n_lat = rows.n_all, rows.n_lat

    def tile(d, s):
        s = n_all - 1 - s if adjoint else s
        return jnp.where(d == 0, (s + n_lat) % n_all, n_all - 1 - s)

    def per_dir(d, s):
        return (d, tile(d, s), 0)

    specs = [pl.BlockSpec((None, rows.tm, width), per_dir) if kind == "dir" else
             pl.BlockSpec((rows.tm, width), lambda d, s: (tile(d, s), 0)) for kind in in_specs]
    shape = jax.ShapeDtypeStruct((2, rows.t_all, width), F32)
    return pl.pallas_call(
        body, name=name, grid=(2, n_all), in_specs=specs, out_specs=[pl.BlockSpec((None, rows.tm, width), per_dir)] * n_out,
        out_shape=[shape] * n_out, scratch_shapes=[pltpu.VMEM((SUBLANES, width), F32)],
        compiler_params=_params("arbitrary", "arbitrary"))(*ins)


def lru_scan(rows, a, bb, name):
    width = a.shape[2]
    n = rows.tm

    def body(a_ref, bb_ref, h_ref, hp_ref, carry):
        d, s = pl.program_id(0), pl.program_id(1)

        @pl.when(s == 0)
        def _():
            carry[...] = jnp.zeros_like(carry)

        def run(reverse):
            for cs in _chunks(width, LANES):
                h_in = carry[0:1, cs]
                h, carry[0:1, cs] = _chain_scan(a_ref[:, cs], bb_ref[:, cs], h_in, reverse)
                h_ref[:, cs] = h
                hp_ref[:, cs] = _neighbour(h, h_in, reverse)

        pl.when(d == 0)(lambda: run(False))
        pl.when(d == 1)(lambda: run(True))

    return _scan_call(rows, body, name, [a, bb], ["dir", "dir"], 2, width, adjoint=False)


def lru_scan_bwd(rows, dh, a, hp, name):
    width = a.shape[2]
    n = rows.tm

    def body(dh_ref, a_ref, hp_ref, da_ref, dbb_ref, carry):
        d, s = pl.program_id(0), pl.program_id(1)

        @pl.when(s == 0)
        def _():
            carry[...] = jnp.zeros_like(carry)

        def run(reverse):
            for cs in _chunks(width, LANES):
                av = a_ref[:, cs]
                g, _ = _chain_scan(_neighbour(av, 1.0, reverse), dh_ref[:, cs], carry[0:1, cs], reverse)
                da_ref[:, cs] = g * hp_ref[:, cs]
                dbb_ref[:, cs] = g
                carry[0:1, cs] = (av * g)[0:1] if reverse else (av * g)[n - 1:n]

        pl.when(d == 0)(lambda: run(True))
        pl.when(d == 1)(lambda: run(False))

    return _scan_call(rows, body, name, [dh, a, hp], ["shared", "dir", "dir"], 2, width, adjoint=True)


def mod_matmul(c9, w_shard, b_shard, name):
    n = w_shard.shape[1]
    tn = _pick(n, 768)

    def body(c_ref, w_ref, b_ref, o_ref):
        act = _silu_and_grad(c_ref[...])[0]
        o_ref[...] = jnp.dot(act, w_ref[...], preferred_element_type=F32, precision=lax.Precision.HIGHEST) + b_ref[...]

    return pl.pallas_call(
        body, name=name, grid=(n // tn,),
        in_specs=[pl.BlockSpec(c9.shape, lambda j: (0, 0)), pl.BlockSpec((w_shard.shape[0], tn), lambda j: (0, j)),
                  pl.BlockSpec((1, tn), lambda j: (0, j))],
        out_specs=pl.BlockSpec((c9.shape[0], tn), lambda j: (0, j)), out_shape=jax.ShapeDtypeStruct((c9.shape[0], n), F32),
        compiler_params=_params("arbitrary"))(c9, w_shard, b_shard)


def mod_matmul_bwd(c9, d9, w_shard, name):
    n = w_shard.shape[1]
    tn = _pick(n, 768)
    steps = n // tn

    def body(c_ref, d_ref, w_ref, gw_ref, gc_ref):
        j = pl.program_id(0)
        act, dact = _silu_and_grad(c_ref[...])
        dv = d_ref[...]
        gw_ref[...] = lax.dot_general(act, dv, (((0,), (0,)), ((), ())), preferred_element_type=F32, precision=lax.Precision.HIGHEST)
        part = lax.dot_general(dv, w_ref[...], (((1,), (1,)), ((), ())), preferred_element_type=F32, precision=lax.Precision.HIGHEST)
        _accumulate(gc_ref, part * dact, j == 0)

    return pl.pallas_call(
        body, name=name, grid=(steps,),
        in_specs=[pl.BlockSpec(c9.shape, lambda j: (0, 0)), pl.BlockSpec((c9.shape[0], tn), lambda j: (0, j)),
                  pl.BlockSpec((w_shard.shape[0], tn), lambda j: (0, j))],
        out_specs=[pl.BlockSpec((w_shard.shape[0], tn), lambda j: (0, j)), pl.BlockSpec(c9.shape, lambda j: (0, 0))],
        out_shape=[jax.ShapeDtypeStruct(w_shard.shape, F32), jax.ShapeDtypeStruct(c9.shape, F32)],
        compiler_params=_params("arbitrary"))(c9, d9, w_shard)


def _row_tile(n_rows, n_cols):
    return _pick(n_rows, max(2 * SUBLANES, (256 * 1024) // n_cols), 2 * SUBLANES)


def sum_parts(parts, name):
    n, n_rows, n_cols = parts.shape
    tr = _row_tile(n_rows, n_cols)

    def body(p_ref, o_ref):
        total = p_ref[0].astype(F32)
        for k in range(1, n):
            total = total + p_ref[k].astype(F32)
        o_ref[...] = total

    return pl.pallas_call(
        body, name=name, grid=(n_rows // tr,), in_specs=[pl.BlockSpec((n, tr, n_cols), lambda i: (0, i, 0))],
        out_specs=pl.BlockSpec((tr, n_cols), lambda i: (i, 0)), out_shape=jax.ShapeDtypeStruct((n_rows, n_cols), F32),
        compiler_params=_params("arbitrary"))(parts)


def _adamw_update(g, w_ref, m_ref, v_ref, g_ref, d_ref, m2_ref, v2_ref):
    m2 = ADAM_B1 * m_ref[...] + (1.0 - ADAM_B1) * g
    v2 = ADAM_B2 * v_ref[...] + (1.0 - ADAM_B2) * (g * g)
    m_hat = m2 / (1.0 - ADAM_B1 ** ADAM_STEP)
    v_hat = v2 / (1.0 - ADAM_B2 ** ADAM_STEP)
    g_ref[...] = g
    d_ref[...] = -ADAM_LR * (m_hat / (jnp.sqrt(v_hat) + ADAM_EPS) + ADAM_WD * w_ref[...])
    m2_ref[...] = m2
    v2_ref[...] = v2


def adamw(parts, w, m, v, name):
    n, n_rows, n_cols = parts.shape
    tr = _row_tile(n_rows, n_cols)

    def body(p_ref, *refs):
        g = p_ref[0].astype(F32)
        for k in range(1, n):
            g = g + p_ref[k].astype(F32)
        _adamw_update(g, *refs)

    blk = pl.BlockSpec((tr, n_cols), lambda i: (i, 0))
    shape = jax.ShapeDtypeStruct((n_rows, n_cols), F32)
    return pl.pallas_call(
        body, name=name, grid=(n_rows // tr,), in_specs=[pl.BlockSpec((n, tr, n_cols), lambda i: (0, i, 0)), blk, blk, blk],
        out_specs=[blk] * 4, out_shape=[shape] * 4, compiler_params=_params("arbitrary"))(parts, w, m, v)


def adamw_pair(parts, w, m, v, name):
    n_parts, n_rows, n_cols = parts.shape
    tr = _row_tile(n_rows, n_cols)
    steps = n_rows // tr

    def body(p_ref, w_ref, m_ref, v_ref, g_ref, d_ref, m2_ref, v2_ref, outbox, inbox, send_sems, recv_sems, credits):
        i = pl.program_id(0)
        sibling = (lax.axis_index("x"), lax.axis_index("y"), 1 - lax.axis_index("c"))

        def copy(slot):
            return pltpu.make_async_remote_copy(src_ref=outbox.at[slot], dst_ref=inbox.at[slot], send_sem=send_sems.at[slot],
                                                recv_sem=recv_sems.at[slot], device_id=sibling, device_id_type=MESH)

        @pl.when(i < steps)
        def _():
            slot = i % 2
            total = p_ref[0].astype(F32)
            for k in range(1, n_parts):
                total = total + p_ref[k].astype(F32)
            outbox[slot] = total

            @pl.when(i >= 2)
            def _():
                pl.semaphore_wait(credits.at[slot], 1)

            copy(slot).start()

        @pl.when(i >= 1)
        def _():
            slot = (i - 1) % 2
            landed = copy(slot)
            landed.wait_recv()
            landed.wait_send()
            _adamw_update(outbox[slot] + inbox[slot], w_ref, m_ref, v_ref, g_ref, d_ref, m2_ref, v2_ref)

            @pl.when(i + 1 < steps)
            def _():
                pl.semaphore_signal(credits.at[slot], inc=1, device_id=sibling, device_id_type=MESH)

    blk = pl.BlockSpec((tr, n_cols), lambda i: (jnp.maximum(i - 1, 0), 0))
    shape = jax.ShapeDtypeStruct((n_rows, n_cols), F32)
    return pl.pallas_call(
        body, name=name, grid=(steps + 1,),
        in_specs=[pl.BlockSpec((n_parts, tr, n_cols), lambda i: (0, jnp.minimum(i, steps - 1), 0)), blk, blk, blk],
        out_specs=[blk] * 4, out_shape=[shape] * 4,
        scratch_shapes=[pltpu.VMEM((2, tr, n_cols), F32), pltpu.VMEM((2, tr, n_cols), F32), pltpu.SemaphoreType.DMA((2,)),
                        pltpu.SemaphoreType.DMA((2,)), pltpu.SemaphoreType.REGULAR((2,))],
        compiler_params=_params("arbitrary"))(parts, w, m, v)


def _pack(arrays, row_multiple=SUBLANES):
    flat = [jnp.pad(a.reshape(-1), (0, -a.size % LANES)) for a in arrays]
    buf = jnp.concatenate(flat)
    buf = jnp.pad(buf, (0, -buf.size % (row_multiple * LANES)))
    return buf.reshape(-1, LANES)


def _unpack(buf, shapes):
    lead = buf.shape[:-2]
    flat = buf.reshape(lead + (-1,))
    out, pos = [], 0
    for shape in shapes:
        size = 1
        for s in shape:
            size *= s
        out.append(flat[..., pos:pos + size].reshape(lead + tuple(shape)))
        pos += size + (-size % LANES)
    return out


def _pos_tables(seq_len, dim, tile_rows):
    grid_rows = seq_len // GRID_W
    per_tile = tile_rows // GRID_W
    q = dim // 4
    omega = 1.0 / (10000.0 ** (jnp.arange(q, dtype=F32) / q))
    er = jnp.arange(grid_rows).astype(F32)[:, None] * omega
    ec = jnp.arange(GRID_W).astype(F32)[:, None] * omega
    by_row = jnp.concatenate([jnp.sin(er), jnp.cos(er)], axis=-1).reshape(grid_rows // per_tile, per_tile, 2 * q)
    by_row = jnp.pad(by_row, ((0, 0), (0, SUBLANES - per_tile), (0, 0))).reshape(-1, 2 * q)
    return by_row, jnp.concatenate([jnp.sin(ec), jnp.cos(ec)], axis=-1)


def _cols_from_shards(g):
    return jnp.transpose(g, (1, 0, 2)).reshape(g.shape[1], -1)


def _shards_from_cols(w, n_shards=4):
    k, n = w.shape
    return jnp.transpose(w.reshape(k, n_shards, n // n_shards), (1, 0, 2))


def kernel(x, c, ctx, c_ctx, w_mod, b_mod, g_n1, w_ffn1_up, w_ffn1_down, g_n2, w_in, b_in, w_dw, b_dw, g_ln, b_ln, w_conf_out, w_lru_conv, b_lru_conv, w_rec_gate, b_rec_gate, w_in_gate, b_in_gate, lru_lambda, w_lru_out, w_out, g_n3, w_ffn2_up, w_ffn2_down, g_final, loss_target, m_c_ctx, m_w_mod, m_b_mod, m_g_n1, m_w_ffn1_up, m_w_ffn1_down, m_g_n2, m_w_in, m_b_in, m_w_dw, m_b_dw, m_g_ln, m_b_ln, m_w_conf_out, m_w_lru_conv, m_b_lru_conv, m_w_rec_gate, m_b_rec_gate, m_w_in_gate, m_b_in_gate, m_lru_lambda, m_w_lru_out, m_w_out, m_g_n3, m_w_ffn2_up, m_w_ffn2_down, m_g_final, v_c_ctx, v_w_mod, v_b_mod, v_g_n1, v_w_ffn1_up, v_w_ffn1_down, v_g_n2, v_w_in, v_b_in, v_w_dw, v_b_dw, v_g_ln, v_b_ln, v_w_conf_out, v_w_lru_conv, v_b_lru_conv, v_w_rec_gate, v_b_rec_gate, v_w_in_gate, v_b_in_gate, v_lru_lambda, v_w_lru_out, v_w_out, v_g_n3, v_w_ffn2_up, v_w_ffn2_down, v_g_final):
    given = dict(locals())
    wt = {n: given[n] for n in W_NAMES}
    mom = {n: given["m_" + n] for n in W_NAMES}
    var = {n: given["v_" + n] for n in W_NAMES}

    def flat2(a):
        if a.ndim == 1:
            return a.reshape(1, -1)
        a = a[0]
        return a if a.ndim == 2 else a.reshape(-1, a.shape[-1])

    t_lat, d = x.shape[1], x.shape[2]
    t_ctx = ctx.shape[1]
    rows = Rows(t_lat, t_ctx, d)
    xi, yi, ci = lax.axis_index("x"), lax.axis_index("y"), lax.axis_index("c")
    me, chip = 4 * xi + 2 * yi + ci, 2 * xi + yi
    lru_w = b_lru_conv.shape[-1]
    n_blk = lru_w // LANES
    taps = w_dw.shape[1]
    lru_taps = w_lru_conv.shape[1]
    ffn = w_ffn1_down.shape[1] * 4

    small_shapes = [(1, d)] + [flat2(wt[n]).shape for n in SMALL_SHARDED]
    small_all = exchange("all", [_pack([c] + [flat2(wt[n]) for n in SMALL_SHARDED])], False, "gather_small")[0]
    small_all = _unpack(small_all, small_shapes)
    c_all = small_all[0][:, 0, :]
    full = {n: jnp.concatenate([a[0], a[2], a[4], a[6]], axis=-1) for n, a in zip(SMALL_SHARDED, small_all[1:])}
    big = COL_SHARDED + ROW_SHARDED
    ffn1_names = ['w_ffn1_up', 'w_ffn1_down']
    ffn2_names = ['w_ffn2_up', 'w_ffn2_down']
    mixer_names = [n for n in big if n not in ffn1_names + ffn2_names]
    wb = {}

    def take_weights(names, gathered):
        for n, g in zip(names, gathered):
            wb[n] = _cols_from_shards(g) if n in COL_SHARDED else g.reshape(-1, g.shape[-1])

    def shards_after(names, earlier):
        return lax.optimization_barrier(([wt[n][0].astype(BF16) for n in names], earlier))[0]

    c9 = jnp.concatenate([c_all, c_ctx.reshape(1, d), jnp.zeros((7, d), F32)], axis=0)
    mod_cols = mod_matmul(c9, w_mod[0], lax.dynamic_slice_in_dim(b_mod, chip * w_mod.shape[2], w_mod.shape[2], axis=1), "mod_matmul")
    mod_all = exchange("all", [mod_cols], False, "gather_mod")[0]
    mod9 = jnp.concatenate([mod_all[0], mod_all[2], mod_all[4], mod_all[6]], axis=-1)
    mv = jnp.stack([lax.dynamic_index_in_dim(mod9, me, 0, keepdims=False).reshape(N_MOD, d), mod9[8].reshape(N_MOD, d)])

    gathered_ffn1 = exchange("chips", shards_after(ffn1_names, mod_all), False, "gather_weights_ffn1")
    take_weights(ffn1_names, gathered_ffn1)
    gathered_mixer = exchange_by_sequencer(shards_after(mixer_names, gathered_ffn1), False, "gather_weights_mixer", 1)

    row_tab, col_tab = _pos_tables(t_lat, d, rows.tm)

    def ffn_fwd(tag, h, w_up, w_down):
        g, u, act = matmul_swiglu(h, w_up, f"{tag}_up")
        return (g, u), act, matmul(act, w_down, "nn", F32, f"{tag}_down")

    x0, h1 = embed_norm_mod(rows, x[0], ctx[0], row_tab, col_tab, g_n1, mv, 0, 1, "norm1")
    gu1, act1, f1 = ffn_fwd("ffn1", h1, wb["w_ffn1_up"], wb["w_ffn1_down"])
    gathered_mixer, f1 = lax.optimization_barrier((gathered_mixer, f1))
    take_weights(mixer_names, gathered_mixer)
    gathered_ffn2 = exchange_by_sequencer(shards_after(ffn2_names, gathered_mixer), False, "gather_weights_ffn2", 5)
    x1, h2 = norm_mod(rows, x0, g_n2, mv, 3, 4, "norm2", resid=(f1, 2, 0.5))

    conf_w = w_conf_out.shape[1] * 4
    col_groups = [(0, 2 * conf_w), (2 * conf_w, lru_w), (2 * conf_w + lru_w, lru_w), (2 * conf_w + 2 * lru_w, 2 * d)]
    w_in_groups = [wb["w_in"][:, s:s + n] for s, n in col_groups]
    pvg, ux, gl, pgc = [matmul(h2, w_g, "nn", BF16, f"in_proj{k}", bias=b_in[:, s:s + n])
                        for k, (w_g, (s, n)) in enumerate(zip(w_in_groups, col_groups))]
    u = glu(rows, pvg, "glu")
    u2 = dwconv(rows, u, full["w_dw"], b_dw, taps // 2, F32, "conf_conv")
    s_act = ln_silu(rows, u2, g_ln, b_ln, "conf_ln")
    yc = matmul(s_act, wb["w_conf_out"], "nn", BF16, "conf_out")

    xr = dwconv(rows, ux, full["w_lru_conv"], b_lru_conv, LRU_PAD_LEFT, F32, "lru_conv")
    w_rec, w_ing = w_rec_gate[0].astype(BF16), w_in_gate[0].astype(BF16)
    wcat = jnp.concatenate([w_rec[0], w_ing[0], w_rec[1], w_ing[1]], axis=-1)

    def per_block(a):
        return jnp.transpose(a.reshape(2, n_blk, LANES), (1, 0, 2)).reshape(n_blk, 1, 2 * LANES)

    b_rec, b_ing = full["b_rec_gate"].reshape(2, n_blk, LANES), full["b_in_gate"].reshape(2, n_blk, LANES)
    bcat = jnp.concatenate([b_rec[0], b_ing[0], b_rec[1], b_ing[1]], axis=-1).reshape(n_blk, 1, 4 * LANES)
    lam = per_block(full["lru_lambda"])
    a_gate, b_gate = lru_gates(rows, xr, wcat, bcat, lam, "lru_gates")
    h_scan, h_prev = lru_scan(rows, a_gate, b_gate, "lru_scan")
    yl_in = lru_merge(rows, h_scan, gl, "lru_merge")
    yl = matmul(yl_in, wb["w_lru_out"], "nn", BF16, "lru_out")
    z = z_merge(rows, pgc, yc, yl, "z_merge")
    y = matmul(z, wb["w_out"], "nn", F32, "mix_out")

    gathered_ffn2, y = lax.optimization_barrier((gathered_ffn2, y))
    take_weights(ffn2_names, gathered_ffn2)
    x2, h3 = norm_mod(rows, x1, g_n3, mv, 6, 7, "norm3", resid=(y, 5, 1.0))
    gu2, act2, f2 = ffn_fwd("ffn2", h3, wb["w_ffn2_up"], wb["w_ffn2_down"])
    loss_part, dx3, dg_final, df2, dgate2 = loss_head(rows, x2, f2, mv, g_final.reshape(1, d), loss_target[0], "loss_head")

    grads = {"g_final": dg_final}
    dmv = [None] * N_MOD

    def ffn_bwd(tag, dxn, df, x_prev, h, gu, act, w_up, w_down, g, ks, collective_id, settle_first=(), **norm_bwd_options):
        k_shift, k_scale = ks
        grads[f"w_{tag}_down"] = matmul(act, df, "tn", BF16, f"{tag}_down_dw")
        if settle_first:
            df = settle(settle_first, df)
        dg_, du_ = matmul_swiglu_bwd(df, w_down, gu[0], gu[1], f"{tag}_down_dx")
        grads[f"w_{tag}_up"] = jnp.concatenate([matmul(h, dg_, "tn", BF16, f"{tag}_up_dw_g"), matmul(h, du_, "tn", BF16, f"{tag}_up_dw_u")], axis=1)
        dg_, du_ = scatter_behind([f"w_{tag}_up", f"w_{tag}_down"], tag, collective_id, (dg_, du_))
        dh = matmul_pair_nt(dg_, du_, w_up, f"{tag}_up_dx")
        dx, dg, dmv[k_shift], dmv[k_scale], *more = norm_mod_bwd(rows, x_prev, dh, dxn, g, mv, k_shift, k_scale, f"{tag}_norm_bwd",
                                                                 **norm_bwd_options)
        return (dx, dg, *more)

    def grad_pieces(names):
        return [(_shards_from_cols(grads[n]) if n in COL_SHARDED else grads[n].reshape(4, -1, grads[n].shape[-1])).astype(BF16)
                for n in names]

    from_chips = {}

    def scatter_behind(names, tag, collective_id, carry):
        ready, carry = lax.optimization_barrier(([grads[n] for n in names], carry))
        grads.update(zip(names, ready))
        from_chips.update(zip(names, exchange_by_sequencer(grad_pieces(names), True, f"scatter_grads_{tag}", collective_id)))
        return carry

    def settle(names, carry):
        landed, carry = lax.optimization_barrier(([from_chips[n] for n in names], carry))
        from_chips.update(zip(names, landed))
        return carry

    dmv[8] = dgate2
    dx2, grads["g_n3"], dy, dmv[5] = ffn_bwd("ffn2", dx3, df2, x2, h3, gu2, act2, wb["w_ffn2_up"], wb["w_ffn2_down"], g_n3, (6, 7), 2,
                                             then_gate=(y, 5, 1.0))

    grads["w_out"] = matmul(z, dy, "tn", BF16, "mix_out_dw")
    dz = matmul(dy, wb["w_out"], "nt", BF16, "mix_out_dx")
    dyc, dyl, dpgc = z_merge_bwd(rows, pgc, yc, yl, dz, "z_merge_bwd")

    grads["w_conf_out"] = matmul(s_act, dyc, "tn", BF16, "conf_out_dw")
    ds_act = matmul(dyc, wb["w_conf_out"], "nt", BF16, "conf_out_dx")
    du2, grads["g_ln"], grads["b_ln"] = ln_silu_bwd(rows, u2, ds_act, g_ln, b_ln, "conf_ln_bwd")
    grads["w_dw"], grads["b_dw"] = dwconv_bwd_w(rows, u, du2, taps, taps // 2, "conf_conv_dw")
    du = dwconv(rows, du2, full["w_dw"][::-1], None, taps - 1 - taps // 2, F32, "conf_conv_dx")
    dpvg = glu_bwd(rows, pvg, du, "glu_bwd")

    grads["w_lru_out"] = matmul(yl_in, dyl, "tn", BF16, "lru_out_dw")
    dyl = scatter_behind(['w_out', 'w_conf_out', 'w_lru_out'], "mixer", 3, settle(ffn2_names, dyl))
    dyl_in = matmul(dyl, wb["w_lru_out"], "nt", BF16, "lru_out_dx")
    dh_sum, dgl = lru_merge_bwd(rows, h_scan, gl, dyl_in, "lru_merge_bwd")
    da_gate, db_gate = lru_scan_bwd(rows, dh_sum, a_gate, h_prev, "lru_scan_bwd")
    dxr, dwcat, dbcat, dlam = lru_gates_bwd(rows, xr, wcat, bcat, lam, da_gate, db_gate, "lru_gates_bwd")
    grads["w_rec_gate"] = jnp.stack([dwcat[:, :, 0:LANES], dwcat[:, :, 2 * LANES:3 * LANES]])
    grads["w_in_gate"] = jnp.stack([dwcat[:, :, LANES:2 * LANES], dwcat[:, :, 3 * LANES:4 * LANES]])
    dbcat = dbcat.reshape(n_blk, 4, LANES)
    grads["b_rec_gate"] = jnp.stack([dbcat[:, 0], dbcat[:, 2]]).reshape(2, lru_w)
    grads["b_in_gate"] = jnp.stack([dbcat[:, 1], dbcat[:, 3]]).reshape(2, lru_w)
    grads["lru_lambda"] = jnp.transpose(dlam.reshape(n_blk, 2, LANES), (1, 0, 2)).reshape(2, lru_w)
    grads["w_lru_conv"], grads["b_lru_conv"] = dwconv_bwd_w(rows, ux, dxr, lru_taps, LRU_PAD_LEFT, "lru_conv_dw")
    dux = dwconv(rows, dxr, full["w_lru_conv"][::-1], None, lru_taps - 1 - LRU_PAD_LEFT, BF16, "lru_conv_dx")

    dproj = [dpvg, dux, dgl, dpgc]
    grads["w_in"] = jnp.concatenate([matmul(h2, dp, "tn", BF16, f"in_proj{k}_dw") for k, dp in enumerate(dproj)], axis=1)
    dproj = scatter_behind(['w_in'], "in_proj", 4, settle(['w_out', 'w_conf_out', 'w_lru_out'], dproj))
    dh2, db_in = matmul_groups_nt(dproj, w_in_groups, "in_proj_dx")
    grads["b_in"] = jnp.concatenate(db_in, axis=1)
    dx1, grads["g_n2"], dmv[3], dmv[4], df1, dmv[2] = norm_mod_bwd(rows, x1, dh2, dx2, g_n2, mv, 3, 4, "norm2_bwd", then_gate=(f1, 2, 0.5))

    dx0, grads["g_n1"] = ffn_bwd("ffn1", dx1, df1, x0, h1, gu1, act1, wb["w_ffn1_up"], wb["w_ffn1_down"], g_n1, (0, 1), 6,
                                 settle_first=['w_in'], latent_only=True)
    grad_x = dx0[None]

    out = {}
    for n in big:
        parts = from_chips[n].reshape(4, -1, from_chips[n].shape[-1])
        out[n] = adamw_pair(parts, flat2(wt[n]), flat2(mom[n]), flat2(var[n]), f"adamw_{n}")

    dmod = jnp.concatenate(dmv, axis=1)
    small_names = ['g_n1', 'g_n2', 'b_in', 'b_dw', 'g_ln', 'b_ln', 'b_lru_conv', 'g_n3', 'g_final'] + SMALL_SHARDED
    gate_names = ['w_rec_gate', 'w_in_gate']
    small_list = [loss_part] + [grads[n] for n in small_names] + [dmod[0], dmod[1]]
    small_buf = _pack(small_list, 8 * 2 * SUBLANES)
    eighths = [small_buf.reshape(8, -1, LANES)] + [grads[n].reshape(8, -1, LANES) for n in gate_names]
    *parts, dmod_all = exchange("all", eighths + [_pack([dmod[0]])], [True] * len(eighths) + [False], "scatter_small_grads")
    sums = exchange("all", [sum_parts(p, f"sum_small_grads{k}") for k, p in enumerate(parts)], False, "gather_small_grads")
    small_sum = _unpack(sums[0].reshape(small_buf.shape), [a.shape for a in small_list])
    loss = small_sum[0][0, 0]
    total = dict(zip(small_names, small_sum[1:]))
    total.update({n: s.reshape(grads[n].shape) for n, s in zip(gate_names, sums[1:])})
    dmod_all = _unpack(dmod_all, [dmod[0].shape])[0].reshape(8, N_MOD * d)
    dmc = small_sum[-1].reshape(1, N_MOD * d)
    total["b_mod"] = small_sum[-2].reshape(1, N_MOD * d) + dmc

    d9 = jnp.concatenate([dmod_all, dmc, jnp.zeros((7, N_MOD * d), F32)], axis=0)
    d9_cols = lax.dynamic_slice_in_dim(d9, chip * w_mod.shape[2], w_mod.shape[2], axis=1)
    g_wmod, dc9 = mod_matmul_bwd(c9, d9_cols, w_mod[0], "mod_matmul_bwd")
    dc_all = exchange("all", [dc9[8:16]], False, "gather_dc")[0]
    total["c_ctx"] = sum_parts(jnp.stack([dc_all[0], dc_all[2], dc_all[4], dc_all[6]]), "sum_dc")[0:1]
    out["w_mod"] = adamw(g_wmod[None], w_mod[0], m_w_mod[0], v_w_mod[0], "adamw_w_mod")

    for n in gate_names:
        out[n] = adamw(total[n].reshape(1, -1, LANES), flat2(wt[n]), flat2(mom[n]), flat2(var[n]), f"adamw_{n}")

    small_all_names = [n for n in W_NAMES if n not in out]
    g_local, shapes = [], []
    for n in small_all_names:
        g = total[n].reshape(flat2(wt[n]).shape[:-1] + (-1,)) if n in SMALL_SHARDED else total[n].reshape(flat2(wt[n]).shape)
        if n in SMALL_SHARDED:
            width = wt[n].shape[-1]
            g = lax.dynamic_slice_in_dim(g, chip * width, width, axis=1)
        g_local.append(g)
        shapes.append(g.shape)
    packed = adamw(_pack(g_local)[None], _pack([flat2(wt[n]) for n in small_all_names]),
                   _pack([flat2(mom[n]) for n in small_all_names]), _pack([flat2(var[n]) for n in small_all_names]), "adamw_small")
    for k, n in enumerate(small_all_names):
        out[n] = [_unpack(p, shapes)[k] for p in packed]

    results = [loss, grad_x]
    for k in range(4):
        results += [out[n][k].reshape(wt[n].shape) for n in W_NAMES]
    return tuple(results)
```

```python
import functools

import jax
import jax.numpy as jnp
from jax import lax
from jax.experimental import pallas as pl
from jax.experimental.pallas import tpu as pltpu
from jax.experimental.pallas import tpu_sc as plsc

F32 = jnp.float32
BF16 = jnp.bfloat16
MESH = pl.DeviceIdType.MESH

EPS = 1e-6
GRID_W = 64
N_MOD = 9
LRU_C = 8.0
LRU_PAD_LEFT = 2
ADAM_LR, ADAM_B1, ADAM_B2, ADAM_EPS, ADAM_WD, ADAM_STEP = 0.001, 0.9, 0.999, 1e-08, 0.01, 10

LANES = 128
SUBLANES = 8
HALO = 16
VMEM_LIMIT = 56 * 1024 * 1024

W_NAMES = ['c_ctx', 'w_mod', 'b_mod', 'g_n1', 'w_ffn1_up', 'w_ffn1_down', 'g_n2', 'w_in', 'b_in', 'w_dw', 'b_dw',
           'g_ln', 'b_ln', 'w_conf_out', 'w_lru_conv', 'b_lru_conv', 'w_rec_gate', 'b_rec_gate', 'w_in_gate',
           'b_in_gate', 'lru_lambda', 'w_lru_out', 'w_out', 'g_n3', 'w_ffn2_up', 'w_ffn2_down', 'g_final']
COL_SHARDED = ['w_ffn1_up', 'w_in', 'w_ffn2_up']
ROW_SHARDED = ['w_ffn1_down', 'w_conf_out', 'w_lru_out', 'w_out', 'w_ffn2_down']
SMALL_SHARDED = ['w_dw', 'w_lru_conv', 'b_rec_gate', 'b_in_gate', 'lru_lambda']


def _params(*semantics):
    return pltpu.CompilerParams(dimension_semantics=semantics, vmem_limit_bytes=VMEM_LIMIT)


def _pick(n, target, mult=LANES):
    best = None
    for t in range(mult, min(n, target) + 1, mult):
        if n % t == 0:
            best = t
    return best or n


def _chunks(width, target=512):
    w = _pick(width, target)
    return [slice(s, s + w) for s in range(0, width, w)]


def _sigmoid(x):
    return jax.nn.sigmoid(x)


def _silu_and_grad(x):
    s = _sigmoid(x)
    return x * s, s * (1.0 + x * (1.0 - s))


_GELU_K = 0.7978845608028654


def _gelu_and_grad(x):
    t = jnp.tanh(_GELU_K * (x + 0.044715 * (x * x * x)))
    return 0.5 * x * (1.0 + t), 0.5 * (1.0 + t) + 0.5 * x * (1.0 - t * t) * _GELU_K * (1.0 + 3 * 0.044715 * x * x)


def _neg_expm1(z):
    series = -(z * (1.0 + z * (0.5 + z * (1.0 / 6.0))))
    return jnp.where(z > -0.01, series, 1.0 - jnp.exp(z))


def _softplus(x):
    return jnp.maximum(x, 0.0) + jnp.log1p(jnp.exp(-jnp.abs(x)))


def _accumulate(ref, value, first):
    @pl.when(first)
    def _():
        ref[...] = value

    @pl.when(jnp.logical_not(first))
    def _():
        ref[...] += value


def _colsum(x):
    return jnp.sum(x, axis=0, keepdims=True)


class Rows:
    def __init__(self, t_lat, t_ctx, d_model):
        self.tm = _pick(t_ctx, 256, HALO)
        assert t_lat % self.tm == 0 and t_ctx % self.tm == 0
        self.n_lat = t_lat // self.tm
        self.n_all = (t_lat + t_ctx) // self.tm
        self.t_all = t_lat + t_ctx
        self.d = d_model

    def seg(self, i):
        return jnp.where(i >= self.n_lat, 1, 0)

    def seg_first(self, i):
        return jnp.logical_or(i == 0, i == self.n_lat)

    def seg_last(self, i):
        return jnp.logical_or(i == self.n_lat - 1, i == self.n_all - 1)

    def row(self, width, col=0):
        return pl.BlockSpec((self.tm, width), lambda i: (i, col))

    def vec(self, width):
        return pl.BlockSpec((1, width), lambda i: (0, 0))

    def full(self, shape):
        return pl.BlockSpec(shape, lambda i: (0,) * len(shape))

    def mod(self):
        return pl.BlockSpec((None, N_MOD, self.d), lambda i: (self.seg(i), 0, 0))

    def segsum(self):
        return pl.BlockSpec((None, 1, self.d), lambda i: (self.seg(i), 0, 0))

    def segsum_shape(self):
        return jax.ShapeDtypeStruct((2, 1, self.d), F32)

    def halo(self, width, side):
        per = self.tm // HALO
        last = self.t_all // HALO - 1
        if side < 0:
            return pl.BlockSpec((HALO, width), lambda i: (jnp.maximum(i * per - 1, 0), 0))
        return pl.BlockSpec((HALO, width), lambda i: (jnp.minimum((i + 1) * per, last), 0))

    def call(self, body, name, in_specs, out_specs, out_shape, scratch=()):
        return pl.pallas_call(body, name=name, grid=(self.n_all,), in_specs=in_specs, out_specs=out_specs,
                              out_shape=out_shape, scratch_shapes=list(scratch), compiler_params=_params("arbitrary"))


_PEER_FLIPS = {
    "chips": [(1, 0, 0), (0, 1, 0), (1, 1, 0)],
    "sibling": [(0, 0, 1)],
    "all": [(0, 0, 1), (0, 1, 0), (0, 1, 1), (1, 0, 0), (1, 0, 1), (1, 1, 0), (1, 1, 1)],
}


def _slot(kind, x, y, c):
    return {"chips": 2 * x + y, "sibling": c, "all": 4 * x + 2 * y + c}[kind]


def exchange(kind, srcs, indexed, name):
    flips = _PEER_FLIPS[kind]
    n_slots = len(flips) + 1
    n_t = len(srcs)
    by_slot = list(indexed) if isinstance(indexed, (list, tuple)) else [indexed] * n_t
    out_shapes = [jax.ShapeDtypeStruct(s.shape if ix else (n_slots,) + s.shape, s.dtype) for s, ix in zip(srcs, by_slot)]

    def body(*refs):
        src_refs, dst_refs = refs[:n_t], refs[n_t:2 * n_t]
        send_sems, recv_sems, local_sems = refs[2 * n_t:]
        x, y, c = lax.axis_index("x"), lax.axis_index("y"), lax.axis_index("c")
        me = _slot(kind, x, y, c)

        def piece(t, k):
            return src_refs[t].at[k] if by_slot[t] else src_refs[t]

        peers = [(1 - x if fx else x, 1 - y if fy else y, 1 - c if fc else c) for fx, fy, fc in flips]
        local = [pltpu.make_async_copy(piece(t, me), dst_refs[t].at[me], local_sems.at[t]) for t in range(n_t)]
        for cp in local:
            cp.start()
        sends = []
        for t in range(n_t):
            for p, peer in enumerate(peers):
                sends.append(pltpu.make_async_remote_copy(
                    src_ref=piece(t, _slot(kind, *peer)), dst_ref=dst_refs[t].at[me],
                    send_sem=send_sems.at[t, p], recv_sem=recv_sems.at[t, p], device_id=peer, device_id_type=MESH))
        for cp in sends:
            cp.start()
        for t in range(n_t):
            for p, peer in enumerate(peers):
                theirs = _slot(kind, *peer)
                pltpu.make_async_remote_copy(
                    src_ref=piece(t, me), dst_ref=dst_refs[t].at[theirs],
                    send_sem=send_sems.at[t, p], recv_sem=recv_sems.at[t, p], device_id=peer, device_id_type=MESH).wait_recv()
        for cp in sends:
            cp.wait_send()
        for cp in local:
            cp.wait()

    any_spec = pl.BlockSpec(memory_space=pl.ANY)
    outs = pl.pallas_call(
        body, name=name, out_shape=out_shapes, in_specs=[any_spec] * n_t, out_specs=[any_spec] * n_t,
        scratch_shapes=[pltpu.SemaphoreType.DMA((n_t, len(flips))), pltpu.SemaphoreType.DMA((n_t, len(flips))),
                        pltpu.SemaphoreType.DMA((n_t,))],
        compiler_params=pltpu.CompilerParams(has_side_effects=True),
    )(*srcs)
    return list(outs)


def exchange_by_sequencer(srcs, indexed, name, collective_id):
    flips = _PEER_FLIPS["chips"]
    n_t, n_p = len(srcs), len(flips)
    src_refs = [jax.new_ref(s, memory_space=pltpu.MemorySpace.HBM) for s in srcs]
    land_refs = [jax.empty_ref(jax.ShapeDtypeStruct(s.shape if indexed else (n_p + 1,) + s.shape, s.dtype),
                               memory_space=pltpu.MemorySpace.HBM) for s in srcs]

    @pl.kernel(mesh=plsc.ScalarSubcoreMesh(axis_name="sequencer", num_cores=1), name=name,
               scratch_types=(pltpu.SemaphoreType.DMA((n_t, n_p)), pltpu.SemaphoreType.DMA((n_t, n_p)), pltpu.SemaphoreType.DMA((n_t,))),
               compiler_params=pltpu.CompilerParams(collective_id=collective_id))
    def launch(send_sems, recv_sems, local_sems):
        x, y, c = lax.axis_index("x"), lax.axis_index("y"), lax.axis_index("c")
        peers = [(1 - x if fx else x, 1 - y if fy else y, c) for fx, fy, _ in flips]
        barrier = pltpu.get_barrier_semaphore()
        for peer in peers:
            pl.semaphore_signal(barrier, inc=1, device_id=peer, device_id_type=MESH)
        pl.semaphore_wait(barrier, n_p)
        me = _slot("chips", x, y, c)

        def piece(t, k):
            return src_refs[t].at[k] if indexed else src_refs[t]

        local = [pltpu.make_async_copy(piece(t, me), land_refs[t].at[me], local_sems.at[t]) for t in range(n_t)]
        for cp in local:
            cp.start()
        sends = []
        for t in range(n_t):
            for p, peer in enumerate(peers):
                sends.append(pltpu.make_async_remote_copy(
                    src_ref=piece(t, _slot("chips", *peer)), dst_ref=land_refs[t].at[me],
                    send_sem=send_sems.at[t, p], recv_sem=recv_sems.at[t, p], device_id=peer, device_id_type=MESH))
        for cp in sends:
            cp.start()
        for t in range(n_t):
            for p, peer in enumerate(peers):
                pltpu.make_async_remote_copy(
                    src_ref=piece(t, me), dst_ref=land_refs[t].at[_slot("chips", *peer)],
                    send_sem=send_sems.at[t, p], recv_sem=recv_sems.at[t, p], device_id=peer, device_id_type=MESH).wait_recv()
        for cp in sends:
            cp.wait_send()
        for cp in local:
            cp.wait()

    launch()
    return [r[...] for r in land_refs]


MATMUL_VMEM_BUDGET = 44 * 1024 * 1024
MATMUL_STEP_BYTES = 1024 * 1024


def _divisors(n, most):
    return [t for t in range(LANES, min(n, most) + 1, LANES) if n % t == 0] or [n]


def _matmul_tiles(n_i, n_j, n_r, a_bytes, b_bytes, out_bytes):
    best = None
    for tr in _divisors(n_r, n_r):
        steps = n_r // tr
        for ti in _divisors(n_i, 1408):
            for tj in _divisors(n_j, 1664):
                vmem = 2 * (ti * tr * a_bytes + tr * tj * b_bytes + ti * tj * out_bytes) + (ti * tj * 4 if steps > 1 else 0)
                if vmem > MATMUL_VMEM_BUDGET or ti < min(n_i, 2 * LANES) or tj < min(n_j, 4 * LANES):
                    continue
                size_a, size_b = n_i * n_r * a_bytes, n_r * n_j * b_bytes
                if steps == 1:
                    moved = min(size_a + (n_i // ti) * size_b, size_b + (n_j // tj) * size_a)
                else:
                    moved = (n_j // tj) * size_a + (n_i // ti) * size_b + (steps - 1) * n_i * n_j * 8
                moved += n_i * n_j * out_bytes + (n_i // ti) * (n_j // tj) * steps * MATMUL_STEP_BYTES
                if best is None or moved < best[0]:
                    best = (moved, ti, tj, tr)
    return best[1:]


def matmul(a, b, mode, out_dtype, name, bias=None):
    if mode == "nn":
        (n_i, n_r), (_, n_j) = a.shape, b.shape
    elif mode == "nt":
        (n_i, n_r), (n_j, _) = a.shape, b.shape
    else:
        (n_r, n_i), (_, n_j) = a.shape, b.shape
    ti, tj, tr = _matmul_tiles(n_i, n_j, n_r, a.dtype.itemsize, b.dtype.itemsize, jnp.dtype(out_dtype).itemsize)
    steps = n_r // tr
    bytes_a, bytes_b = a.size * a.dtype.itemsize, b.size * b.dtype.itemsize
    j_outer = steps == 1 and bytes_b + (n_j // tj) * bytes_a < bytes_a + (n_i // ti) * bytes_b

    def at(fn):
        return (lambda j, i, r: fn(i, j, r)) if j_outer else fn

    a_spec = {"nn": pl.BlockSpec((ti, tr), at(lambda i, j, r: (i, r))), "nt": pl.BlockSpec((ti, tr), at(lambda i, j, r: (i, r))),
              "tn": pl.BlockSpec((tr, ti), at(lambda i, j, r: (r, i)))}[mode]
    b_spec = {"nn": pl.BlockSpec((tr, tj), at(lambda i, j, r: (r, j))), "nt": pl.BlockSpec((tj, tr), at(lambda i, j, r: (j, r))),
              "tn": pl.BlockSpec((tr, tj), at(lambda i, j, r: (r, j)))}[mode]
    dims = {"nn": (((1,), (0,)), ((), ())), "nt": (((1,), (1,)), ((), ())), "tn": (((0,), (0,)), ((), ()))}[mode]

    def body(*refs):
        a_ref, b_ref = refs[:2]
        bias_ref = refs[2] if bias is not None else None
        o_ref = refs[3] if bias is not None else refs[2]
        prod = lax.dot_general(a_ref[...].astype(BF16), b_ref[...].astype(BF16), dims, preferred_element_type=F32)

        def finish(total):
            if bias_ref is not None:
                total = total + bias_ref[...]
            o_ref[...] = total.astype(out_dtype)

        if steps == 1:
            finish(prod)
        else:
            acc = refs[-1]
            r = pl.program_id(2)

            @pl.when(r == 0)
            def _():
                acc[...] = prod

            @pl.when(r > 0)
            def _():
                acc[...] += prod

            @pl.when(r == steps - 1)
            def _():
                finish(acc[...])

    in_specs = [a_spec, b_spec] + ([pl.BlockSpec((1, tj), at(lambda i, j, r: (0, j)))] if bias is not None else [])
    args = (a, b) + ((bias,) if bias is not None else ())
    grid = (n_j // tj, n_i // ti, steps) if j_outer else (n_i // ti, n_j // tj, steps)
    return pl.pallas_call(
        body, name=name, grid=grid, in_specs=in_specs,
        out_specs=pl.BlockSpec((ti, tj), at(lambda i, j, r: (i, j))), out_shape=jax.ShapeDtypeStruct((n_i, n_j), out_dtype),
        scratch_shapes=[pltpu.VMEM((ti, tj), F32)] if steps > 1 else [],
        compiler_params=_params("arbitrary", "arbitrary", "arbitrary"),
    )(*args)


def matmul_swiglu(h, w_up, name):
    n_t, d = h.shape
    f = w_up.shape[1] // 2
    ti, tj = _pick(n_t, 384), _pick(f, 1408)
    n_j = f // tj

    def body(h_ref, wg_ref, wu_ref, g_ref, u_ref, act_ref):
        hv = h_ref[...]
        g = jnp.dot(hv, wg_ref[...], preferred_element_type=F32)
        u = jnp.dot(hv, wu_ref[...], preferred_element_type=F32)
        g_ref[...] = g.astype(BF16)
        u_ref[...] = u.astype(BF16)
        act_ref[...] = (_silu_and_grad(g)[0] * u).astype(BF16)

    out = pl.BlockSpec((ti, tj), lambda j, i: (i, j))
    shape = jax.ShapeDtypeStruct((n_t, f), BF16)
    return pl.pallas_call(
        body, name=name, grid=(n_j, n_t // ti),
        in_specs=[pl.BlockSpec((ti, d), lambda j, i: (i, 0)), pl.BlockSpec((d, tj), lambda j, i: (0, j)),
                  pl.BlockSpec((d, tj), lambda j, i: (0, j + n_j))],
        out_specs=[out, out, out], out_shape=[shape, shape, shape], compiler_params=_params("arbitrary", "arbitrary"),
    )(h, w_up, w_up)


def matmul_swiglu_bwd(df, w_down, g, u, name):
    n_t, d = df.shape
    f = w_down.shape[0]
    ti, tj = _pick(n_t, 768), _pick(f, 1408)

    def body(df_ref, w_ref, g_ref, u_ref, dg_ref, du_ref):
        dact = lax.dot_general(df_ref[...], w_ref[...], (((1,), (1,)), ((), ())), preferred_element_type=F32)
        act, dact_dg = _silu_and_grad(g_ref[...].astype(F32))
        dg_ref[...] = (dact * u_ref[...].astype(F32) * dact_dg).astype(BF16)
        du_ref[...] = (dact * act).astype(BF16)

    tile = pl.BlockSpec((ti, tj), lambda j, i: (i, j))
    shape = jax.ShapeDtypeStruct((n_t, f), BF16)
    return pl.pallas_call(
        body, name=name, grid=(f // tj, n_t // ti),
        in_specs=[pl.BlockSpec((ti, d), lambda j, i: (i, 0)), pl.BlockSpec((tj, d), lambda j, i: (j, 0)), tile, tile],
        out_specs=[tile, tile], out_shape=[shape, shape], compiler_params=_params("arbitrary", "arbitrary"))(df, w_down, g, u)


def matmul_groups_nt(parts, weights, name):
    n_g = len(parts)
    n_t, d = parts[0].shape[0], weights[0].shape[0]
    ti = _pick(n_t, 384)
    dims = (((1,), (1,)), ((), ()))

    def body(*refs):
        a_refs, b_refs, o_ref, sum_refs = refs[:n_g], refs[n_g:2 * n_g], refs[2 * n_g], refs[2 * n_g + 1:]
        i = pl.program_id(0)
        total = None
        for a_ref, b_ref, s_ref in zip(a_refs, b_refs, sum_refs):
            av = a_ref[...]
            prod = lax.dot_general(av, b_ref[...], dims, preferred_element_type=F32)
            total = prod if total is None else total + prod
            _accumulate(s_ref, _colsum(av.astype(F32)), i == 0)
        o_ref[...] = total

    widths = [p.shape[1] for p in parts]
    outs = pl.pallas_call(
        body, name=name, grid=(n_t // ti,),
        in_specs=[pl.BlockSpec((ti, w), lambda i: (i, 0)) for w in widths] + [pl.BlockSpec((d, w), lambda i: (0, 0)) for w in widths],
        out_specs=[pl.BlockSpec((ti, d), lambda i: (i, 0))] + [pl.BlockSpec((1, w), lambda i: (0, 0)) for w in widths],
        out_shape=[jax.ShapeDtypeStruct((n_t, d), F32)] + [jax.ShapeDtypeStruct((1, w), F32) for w in widths],
        compiler_params=_params("arbitrary"))(*parts, *weights)
    return outs[0], list(outs[1:])


def matmul_pair_nt(a1, a2, b, name):
    n_t, f = a1.shape
    d = b.shape[0]
    ti = _pick(n_t, 768)
    dims = (((1,), (1,)), ((), ()))

    def body(a1_ref, a2_ref, b1_ref, b2_ref, o_ref):
        o_ref[...] = (lax.dot_general(a1_ref[...], b1_ref[...], dims, preferred_element_type=F32)
                      + lax.dot_general(a2_ref[...], b2_ref[...], dims, preferred_element_type=F32))

    rows_in = pl.BlockSpec((ti, f), lambda i: (i, 0))
    return pl.pallas_call(
        body, name=name, grid=(n_t // ti,),
        in_specs=[rows_in, rows_in, pl.BlockSpec((d, f), lambda i: (0, 0)), pl.BlockSpec((d, f), lambda i: (0, 1))],
        out_specs=pl.BlockSpec((ti, d), lambda i: (i, 0)), out_shape=jax.ShapeDtypeStruct((n_t, d), F32),
        compiler_params=_params("arbitrary"))(a1, a2, b, b)


def norm_mod(rows, x, g, mv, shift_k, scale_k, name, resid=None):
    d = rows.d
    changed = resid is not None

    def body(*refs):
        refs = list(refs)
        x_ref, g_ref, mv_ref = refs[:3]
        rest = refs[3:]
        xv = x_ref[...]
        if resid is not None:
            xv = xv + resid[2] * mv_ref[resid[1]:resid[1] + 1, :] * rest.pop(0)[...]
        if changed:
            rest.pop(0)[...] = xv
        r = lax.rsqrt(jnp.mean(xv * xv, axis=-1, keepdims=True) + EPS)
        h = (xv * r) * g_ref[...]
        if shift_k is not None:
            h = h * (1.0 + mv_ref[scale_k:scale_k + 1, :]) + mv_ref[shift_k:shift_k + 1, :]
        rest.pop(0)[...] = h.astype(BF16)

    ins = [x, g, mv] + ([resid[0]] if resid is not None else [])
    in_specs = [rows.row(d), rows.vec(d), rows.mod()] + [rows.row(d)] * (len(ins) - 3)
    out_shape = ([jax.ShapeDtypeStruct((rows.t_all, d), F32)] if changed else []) + [jax.ShapeDtypeStruct((rows.t_all, d), BF16)]
    outs = rows.call(body, name, in_specs, [rows.row(d)] * len(out_shape), out_shape)(*ins)
    return (outs[0], outs[1]) if changed else (None, outs[0])


def embed_norm_mod(rows, x_lat, x_ctx, row_tab, col_tab, g, mv, shift_k, scale_k, name):
    d = rows.d
    half = d // 2
    per_tile = rows.tm // GRID_W
    assert rows.tm % GRID_W == 0 and per_tile <= SUBLANES

    def body(xl_ref, xc_ref, rt_ref, ct_ref, g_ref, mv_ref, x0_ref, h_ref):
        i = pl.program_id(0)
        by_row = jnp.concatenate([jnp.broadcast_to(rt_ref[r:r + 1, :], (GRID_W, half)) for r in range(per_tile)], axis=0)
        by_col = jnp.concatenate([ct_ref[...]] * per_tile, axis=0)
        xv = jnp.where(i < rows.n_lat, xl_ref[...] + jnp.concatenate([by_row, by_col], axis=1), xc_ref[...])
        x0_ref[...] = xv
        r = lax.rsqrt(jnp.mean(xv * xv, axis=-1, keepdims=True) + EPS)
        h = (xv * r) * g_ref[...]
        h_ref[...] = (h * (1.0 + mv_ref[scale_k:scale_k + 1, :]) + mv_ref[shift_k:shift_k + 1, :]).astype(BF16)

    last = rows.n_lat - 1
    in_specs = [pl.BlockSpec((rows.tm, d), lambda i: (jnp.minimum(i, last), 0)),
                pl.BlockSpec((rows.tm, d), lambda i: (jnp.maximum(i - rows.n_lat, 0), 0)),
                pl.BlockSpec((SUBLANES, half), lambda i: (jnp.minimum(i, last), 0)), rows.full(col_tab.shape), rows.vec(d), rows.mod()]
    return rows.call(body, name, in_specs, [rows.row(d), rows.row(d)],
                     [jax.ShapeDtypeStruct((rows.t_all, d), F32), jax.ShapeDtypeStruct((rows.t_all, d), BF16)])(
                         x_lat, x_ctx, row_tab, col_tab, g, mv)


def norm_mod_bwd(rows, x, dh, dxn, g, mv, shift_k, scale_k, name, latent_only=False, then_gate=None):
    d = rows.d
    n_dx = rows.n_lat if latent_only else rows.n_all

    def body(*refs):
        x_ref, dh_ref, dxn_ref, g_ref, mv_ref = refs[:5]
        dx_ref, dg_ref, dsh_ref, dsc_ref = refs[-4:] if then_gate is None else refs[6:10]
        i = pl.program_id(0)
        xv, dhv, gv = x_ref[...], dh_ref[...], g_ref[...]
        r = lax.rsqrt(jnp.mean(xv * xv, axis=-1, keepdims=True) + EPS)
        n = xv * r
        dy = dhv * (1.0 + mv_ref[scale_k:scale_k + 1, :])
        dn = dy * gv
        dx = dxn_ref[...] + r * (dn - n * jnp.mean(dn * n, axis=-1, keepdims=True))

        @pl.when(i < n_dx)
        def _():
            dx_ref[...] = dx

        _accumulate(dg_ref, _colsum(dy * n), i == 0)
        _accumulate(dsh_ref, _colsum(dhv), rows.seg_first(i))
        _accumulate(dsc_ref, _colsum(dhv * (n * gv)), rows.seg_first(i))
        if then_gate is not None:
            _gate_bwd(rows, i, dx, refs[5], mv_ref, then_gate[1], then_gate[2], refs[10], refs[11])

    dx_spec = pl.BlockSpec((rows.tm, d), lambda i: (jnp.minimum(i, n_dx - 1), 0))
    ins = [x, dh, dxn, g, mv] + ([then_gate[0]] if then_gate is not None else [])
    in_specs = [rows.row(d), rows.row(d), rows.row(d), rows.vec(d), rows.mod()] + ([rows.row(d)] if then_gate is not None else [])
    out_specs = [dx_spec, rows.vec(d), rows.segsum(), rows.segsum()]
    out_shape = [jax.ShapeDtypeStruct((n_dx * rows.tm, d), F32), jax.ShapeDtypeStruct((1, d), F32), rows.segsum_shape(), rows.segsum_shape()]
    if then_gate is not None:
        out_specs += [rows.row(d), rows.segsum()]
        out_shape += [jax.ShapeDtypeStruct((rows.t_all, d), BF16), rows.segsum_shape()]
    return rows.call(body, name, in_specs, out_specs, out_shape)(*ins)


def _gate_bwd(rows, i, dx, f_ref, mv_ref, gate_k, scale, df_ref, dgate_ref):
    df_ref[...] = (scale * mv_ref[gate_k:gate_k + 1, :] * dx).astype(BF16)
    _accumulate(dgate_ref, scale * _colsum(dx * f_ref[...]), rows.seg_first(i))


def glu(rows, pvg, name):
    w = pvg.shape[1] // 2

    def body(v_ref, t_ref, o_ref):
        for cs in _chunks(w):
            o_ref[:, cs] = v_ref[:, cs].astype(F32) * _sigmoid(t_ref[:, cs].astype(F32))

    return rows.call(body, name, [rows.row(w, 0), rows.row(w, 1)], rows.row(w), jax.ShapeDtypeStruct((rows.t_all, w), F32))(pvg, pvg)


def glu_bwd(rows, pvg, du, name):
    w = pvg.shape[1] // 2

    def body(v_ref, t_ref, du_ref, o_ref):
        for cs in _chunks(w):
            s = _sigmoid(t_ref[:, cs].astype(F32))
            duv = du_ref[:, cs]
            o_ref[:, cs] = (duv * s).astype(BF16)
            o_ref[:, slice(w + cs.start, w + cs.stop)] = (duv * v_ref[:, cs].astype(F32) * s * (1.0 - s)).astype(BF16)

    return rows.call(body, name, [rows.row(w, 0), rows.row(w, 1), rows.row(w)], rows.row(2 * w),
                     jax.ShapeDtypeStruct((rows.t_all, 2 * w), BF16))(pvg, pvg, du)


def ln_silu(rows, u, g, b, name):
    w = u.shape[1]

    def body(u_ref, g_ref, b_ref, o_ref):
        uv = u_ref[...]
        xc = uv - jnp.mean(uv, axis=-1, keepdims=True)
        n = xc * lax.rsqrt(jnp.mean(xc * xc, axis=-1, keepdims=True) + EPS)
        o_ref[...] = _silu_and_grad(n * g_ref[...] + b_ref[...])[0].astype(BF16)

    return rows.call(body, name, [rows.row(w), rows.vec(w), rows.vec(w)], rows.row(w),
                     jax.ShapeDtypeStruct((rows.t_all, w), BF16))(u, g, b)


def ln_silu_bwd(rows, u, ds, g, b, name):
    w = u.shape[1]

    def body(u_ref, ds_ref, g_ref, b_ref, du_ref, dg_ref, db_ref):
        i = pl.program_id(0)
        uv, gv = u_ref[...], g_ref[...]
        xc = uv - jnp.mean(uv, axis=-1, keepdims=True)
        r = lax.rsqrt(jnp.mean(xc * xc, axis=-1, keepdims=True) + EPS)
        n = xc * r
        dy = ds_ref[...].astype(F32) * _silu_and_grad(n * gv + b_ref[...])[1]
        dn = dy * gv
        du_ref[...] = r * (dn - jnp.mean(dn, axis=-1, keepdims=True) - n * jnp.mean(dn * n, axis=-1, keepdims=True))
        _accumulate(dg_ref, _colsum(dy * n), i == 0)
        _accumulate(db_ref, _colsum(dy), i == 0)

    return rows.call(body, name, [rows.row(w), rows.row(w), rows.vec(w), rows.vec(w)], [rows.row(w), rows.vec(w), rows.vec(w)],
                     [jax.ShapeDtypeStruct((rows.t_all, w), F32), jax.ShapeDtypeStruct((1, w), F32),
                      jax.ShapeDtypeStruct((1, w), F32)])(u, ds, g, b)


def lru_merge(rows, h, gl, name):
    w = gl.shape[1]

    def body(h_ref, gl_ref, o_ref):
        for cs in _chunks(w):
            h_sum = h_ref[0, :, cs].astype(F32) + h_ref[1, :, cs].astype(F32)
            o_ref[:, cs] = (h_sum * _gelu_and_grad(gl_ref[:, cs].astype(F32))[0]).astype(BF16)

    return rows.call(body, name, [pl.BlockSpec((2, rows.tm, w), lambda i: (0, i, 0)), rows.row(w)], rows.row(w),
                     jax.ShapeDtypeStruct((rows.t_all, w), BF16))(h, gl)


def lru_merge_bwd(rows, h, gl, dy, name):
    w = gl.shape[1]

    def body(h_ref, gl_ref, dy_ref, dh_ref, dgl_ref):
        for cs in _chunks(w):
            act, dact = _gelu_and_grad(gl_ref[:, cs].astype(F32))
            dyv = dy_ref[:, cs].astype(F32)
            dh_ref[:, cs] = (dyv * act).astype(BF16)
            dgl_ref[:, cs] = (dyv * (h_ref[0, :, cs].astype(F32) + h_ref[1, :, cs].astype(F32)) * dact).astype(BF16)

    return rows.call(body, name, [pl.BlockSpec((2, rows.tm, w), lambda i: (0, i, 0)), rows.row(w), rows.row(w)],
                     [rows.row(w), rows.row(w)],
                     [jax.ShapeDtypeStruct((rows.t_all, w), BF16), jax.ShapeDtypeStruct((rows.t_all, w), BF16)])(h, gl, dy)


def z_merge(rows, pgc, yc, yl, name):
    d = rows.d

    def body(gc_ref, gr_ref, yc_ref, yl_ref, o_ref):
        for cs in _chunks(d):
            o_ref[:, cs] = (_sigmoid(gc_ref[:, cs].astype(F32)) * yc_ref[:, cs].astype(F32)
                            + _sigmoid(gr_ref[:, cs].astype(F32)) * yl_ref[:, cs].astype(F32)).astype(BF16)

    return rows.call(body, name, [rows.row(d, 0), rows.row(d, 1), rows.row(d), rows.row(d)], rows.row(d),
                     jax.ShapeDtypeStruct((rows.t_all, d), BF16))(pgc, pgc, yc, yl)


def z_merge_bwd(rows, pgc, yc, yl, dz, name):
    d = rows.d

    def body(gc_ref, gr_ref, yc_ref, yl_ref, dz_ref, dyc_ref, dyl_ref, dp_ref):
        for cs in _chunks(d):
            sc, sr = _sigmoid(gc_ref[:, cs].astype(F32)), _sigmoid(gr_ref[:, cs].astype(F32))
            dzv = dz_ref[:, cs].astype(F32)
            dyc_ref[:, cs] = (dzv * sc).astype(BF16)
            dyl_ref[:, cs] = (dzv * sr).astype(BF16)
            dp_ref[:, cs] = (dzv * yc_ref[:, cs].astype(F32) * sc * (1.0 - sc)).astype(BF16)
            dp_ref[:, slice(d + cs.start, d + cs.stop)] = (dzv * yl_ref[:, cs].astype(F32) * sr * (1.0 - sr)).astype(BF16)

    return rows.call(body, name, [rows.row(d, 0), rows.row(d, 1), rows.row(d), rows.row(d), rows.row(d)],
                     [rows.row(d), rows.row(d), rows.row(2 * d)],
                     [jax.ShapeDtypeStruct((rows.t_all, d), BF16), jax.ShapeDtypeStruct((rows.t_all, d), BF16),
                      jax.ShapeDtypeStruct((rows.t_all, 2 * d), BF16)])(pgc, pgc, yc, yl, dz)


def loss_head(rows, x, f, mv, g, target, name):
    d = rows.d

    def body(x_ref, f_ref, mv_ref, g_ref, t_ref, loss_ref, dx_ref, dg_ref, df_ref, dgate_ref):
        i = pl.program_id(0)
        valid = jnp.where(i < rows.n_lat, 1.0, 0.0)
        xv = x_ref[...] + 0.5 * mv_ref[8:9, :] * f_ref[...]
        gv = g_ref[...]
        r = lax.rsqrt(jnp.mean(xv * xv, axis=-1, keepdims=True) + EPS)
        n = xv * r
        err = (n * gv - t_ref[...]) * valid
        part = 0.5 * jnp.sum(jnp.mean(err * err, axis=-1, keepdims=True), axis=0, keepdims=True)
        _accumulate(loss_ref, jnp.broadcast_to(part, (1, LANES)), i == 0)
        dy = err * (1.0 / d)
        dn = dy * gv
        dx = r * (dn - n * jnp.mean(dn * n, axis=-1, keepdims=True))
        dx_ref[...] = dx
        _accumulate(dg_ref, _colsum(dy * n), i == 0)
        _gate_bwd(rows, i, dx, f_ref, mv_ref, 8, 0.5, df_ref, dgate_ref)

    target_spec = pl.BlockSpec((rows.tm, d), lambda i: (jnp.minimum(i, rows.n_lat - 1), 0))
    return rows.call(body, name, [rows.row(d), rows.row(d), rows.mod(), rows.vec(d), target_spec],
                     [rows.vec(LANES), rows.row(d), rows.vec(d), rows.row(d), rows.segsum()],
                     [jax.ShapeDtypeStruct((1, LANES), F32), jax.ShapeDtypeStruct((rows.t_all, d), F32),
                      jax.ShapeDtypeStruct((1, d), F32), jax.ShapeDtypeStruct((rows.t_all, d), BF16), rows.segsum_shape()])(
                          x, f, mv, g, target)


def _extended(rows, i, prev_ref, cur_ref, next_ref, cs):
    prev = jnp.where(rows.seg_first(i), 0.0, prev_ref[:, cs].astype(F32))
    nxt = jnp.where(rows.seg_last(i), 0.0, next_ref[:, cs].astype(F32))
    return jnp.concatenate([prev, cur_ref[:, cs].astype(F32), nxt], axis=0)


class _Shifts:
    def __init__(self, ext, tm):
        self.ext, self.tm, self.rolled = ext, tm, {0: ext}

    def at(self, offset):
        residue = offset % SUBLANES
        if residue not in self.rolled:
            self.rolled[residue] = pltpu.roll(self.ext, self.ext.shape[0] - residue, 0)
        start = HALO + offset - residue
        return self.rolled[residue][start:start + self.tm]


def _pad_taps(w):
    k = w.shape[0]
    return jnp.pad(w, ((0, -k % SUBLANES), (0, 0)))


def dwconv(rows, u, w, b, pad_left, out_dtype, name):
    taps, width = w.shape
    wp = _pad_taps(w)

    def body(*refs):
        prev_ref, cur_ref, next_ref, w_ref = refs[:4]
        b_ref = refs[4] if b is not None else None
        o_ref = refs[-1]
        i = pl.program_id(0)
        for cs in _chunks(width, LANES):
            shifts = _Shifts(_extended(rows, i, prev_ref, cur_ref, next_ref, cs), rows.tm)
            acc = jnp.zeros((rows.tm, LANES), F32) if b_ref is None else jnp.broadcast_to(b_ref[:, cs], (rows.tm, LANES))
            for k in range(taps):
                acc = acc + w_ref[k:k + 1, cs] * shifts.at(k - pad_left)
            o_ref[:, cs] = acc.astype(out_dtype)

    ins = [u, u, u, wp] + ([b] if b is not None else [])
    in_specs = [rows.halo(width, -1), rows.row(width), rows.halo(width, 1), rows.full(wp.shape)] + ([rows.vec(width)] if b is not None else [])
    return rows.call(body, name, in_specs, rows.row(width), jax.ShapeDtypeStruct((rows.t_all, width), out_dtype))(*ins)


def dwconv_bwd_w(rows, u, dy, taps, pad_left, name):
    width = u.shape[1]
    taps_p = taps + (-taps % SUBLANES)

    def body(prev_ref, cur_ref, next_ref, dy_ref, dw_ref, db_ref):
        i = pl.program_id(0)
        tap_row = lax.broadcasted_iota(jnp.int32, (taps_p, LANES), 0)
        for cs in _chunks(width, LANES):
            shifts = _Shifts(_extended(rows, i, prev_ref, cur_ref, next_ref, cs), rows.tm)
            dyv = dy_ref[:, cs]
            total = jnp.zeros((taps_p, LANES), F32)
            for k in range(taps):
                total = total + jnp.where(tap_row == k, _colsum(dyv * shifts.at(k - pad_left)), 0.0)
            _accumulate(dw_ref.at[:, cs], total, i == 0)
            _accumulate(db_ref.at[:, cs], _colsum(dyv), i == 0)

    dw, db = rows.call(body, name, [rows.halo(width, -1), rows.row(width), rows.halo(width, 1), rows.row(width)],
                       [rows.full((taps_p, width)), rows.vec(width)],
                       [jax.ShapeDtypeStruct((taps_p, width), F32), jax.ShapeDtypeStruct((1, width), F32)])(u, u, u, dy)
    return dw[:taps], db


def _gate_values(xr, pre, lam, d):
    r = _sigmoid(pre[:, (2 * d) * LANES:(2 * d + 1) * LANES])
    ig = _sigmoid(pre[:, (2 * d + 1) * LANES:(2 * d + 2) * LANES])
    sp = _softplus(-lam[:, d * LANES:(d + 1) * LANES])
    log_a = -LRU_C * r * sp
    return r, ig, sp, jnp.exp(log_a), _neg_expm1(2.0 * log_a)


def lru_gates(rows, xr, wcat, bcat, lam, name):
    n_blk = wcat.shape[0]
    width = xr.shape[1]

    def body(xr_ref, w_ref, b_ref, lam_ref, a_ref, bb_ref):
        for h in range(n_blk):
            cs = slice(h * LANES, (h + 1) * LANES)
            xv = xr_ref[:, cs]
            pre = jnp.dot(xv.astype(BF16), w_ref[h], preferred_element_type=F32) + b_ref[h]
            for d in range(2):
                _, ig, _, a, one_minus_a2 = _gate_values(xv, pre, lam_ref[h], d)
                a_ref[d, :, cs] = a
                bb_ref[d, :, cs] = (jnp.sqrt(one_minus_a2) * (ig * xv)).astype(BF16)

    both = pl.BlockSpec((2, rows.tm, width), lambda i: (0, i, 0))
    return rows.call(body, name, [rows.row(width), rows.full(wcat.shape), rows.full(bcat.shape), rows.full(lam.shape)], [both, both],
                     [jax.ShapeDtypeStruct((2, rows.t_all, width), F32), jax.ShapeDtypeStruct((2, rows.t_all, width), BF16)])(
                         xr, wcat, bcat, lam)


def lru_gates_bwd(rows, xr, wcat, bcat, lam, da, dbb, name):
    n_blk = wcat.shape[0]
    width = xr.shape[1]

    def body(xr_ref, w_ref, b_ref, lam_ref, da_ref, dbb_ref, dxr_ref, dw_ref, db_ref, dlam_ref):
        i = pl.program_id(0)
        for h in range(n_blk):
            cs = slice(h * LANES, (h + 1) * LANES)
            xv = xr_ref[:, cs]
            xb = xv.astype(BF16)
            wv = w_ref[h]
            pre = jnp.dot(xb, wv, preferred_element_type=F32) + b_ref[h]
            dxr = jnp.zeros_like(xv)
            dpre, dlam = [], []
            for d in range(2):
                r, ig, sp, a, one_minus_a2 = _gate_values(xv, pre, lam_ref[h], d)
                inv_q = lax.rsqrt(one_minus_a2)
                dav, dbv = da_ref[d, :, cs].astype(F32), dbb_ref[d, :, cs].astype(F32)
                dbq = dbv * (one_minus_a2 * inv_q) * ig
                dlog_a = dav * a - dbv * (ig * xv) * ((a * a) * inv_q)
                dpre.append(dlog_a * (-LRU_C * sp) * (r - r * r))
                dpre.append(dbq * xv * (1.0 - ig))
                dxr = dxr + dbq
                dlam.append(_colsum(dlog_a * (-LRU_C * r)) * (-_sigmoid(-lam_ref[h][:, d * LANES:(d + 1) * LANES])))
            dpre = jnp.concatenate(dpre, axis=1)
            dpb = dpre.astype(BF16)
            dxr_ref[:, cs] = dxr + lax.dot_general(dpb, wv, (((1,), (1,)), ((), ())), preferred_element_type=F32)
            _accumulate(dw_ref.at[h], lax.dot_general(xb, dpb, (((0,), (0,)), ((), ())), preferred_element_type=F32), i == 0)
            _accumulate(db_ref.at[h], _colsum(dpre), i == 0)
            _accumulate(dlam_ref.at[h], jnp.concatenate(dlam, axis=1), i == 0)

    both = pl.BlockSpec((2, rows.tm, width), lambda i: (0, i, 0))
    return rows.call(
        body, name, [rows.row(width), rows.full(wcat.shape), rows.full(bcat.shape), rows.full(lam.shape), both, both],
        [rows.row(width), rows.full(wcat.shape), rows.full(bcat.shape), rows.full(lam.shape)],
        [jax.ShapeDtypeStruct((rows.t_all, width), F32), jax.ShapeDtypeStruct(wcat.shape, F32),
         jax.ShapeDtypeStruct(bcat.shape, F32), jax.ShapeDtypeStruct(lam.shape, F32)])(xr, wcat, bcat, lam, da, dbb)


def _tile_scan(a, b, reverse):
    n = a.shape[0]
    row = lax.broadcasted_iota(jnp.int32, a.shape, 0)
    k = 1
    while k < n:
        ok = (row < n - k) if reverse else (row >= k)
        shift = n - k if reverse else k
        b = b + a * jnp.where(ok, pltpu.roll(b, shift, 0), 0.0)
        a = a * jnp.where(ok, pltpu.roll(a, shift, 0), 1.0)
        k *= 2
    return a, b


def _chain_scan(a, b, h_in, reverse):
    n_blocks = a.shape[0] // SUBLANES
    out = [None] * n_blocks
    state = h_in
    for j in (range(n_blocks - 1, -1, -1) if reverse else range(n_blocks)):
        rows_j = slice(j * SUBLANES, (j + 1) * SUBLANES)
        cum, h0 = _tile_scan(a[rows_j], b[rows_j], reverse)
        out[j] = h0 + cum * state
        state = out[j][0:1] if reverse else out[j][SUBLANES - 1:SUBLANES]
    return jnp.concatenate(out, axis=0), state


def _neighbour(v, edge, reverse):
    n = v.shape[0]
    row = lax.broadcasted_iota(jnp.int32, v.shape, 0)
    if reverse:
        return jnp.where(row < n - 1, pltpu.roll(v, n - 1, 0), edge)
    return jnp.where(row >= 1, pltpu.roll(v, 1, 0), edge)


def _scan_call(rows, body, name, ins, in_specs, n_out, width, adjoint):
    n_all, n_lat = rows.n_all, rows.n_lat

    def tile(d, s):
        s = n_all - 1 - s if adjoint else s
        return jnp.where(d == 0, (s + n_lat) % n_all, n_all - 1 - s)

    def per_dir(d, s):
        return (d, tile(d, s), 0)

    specs = [pl.BlockSpec((None, rows.tm, width), per_dir) if kind == "dir" else
             pl.BlockSpec((rows.tm, width), lambda d, s: (tile(d, s), 0)) for kind in in_specs]
    shape = jax.ShapeDtypeStruct((2, rows.t_all, width), BF16)
    return pl.pallas_call(
        body, name=name, grid=(2, n_all), in_specs=specs, out_specs=[pl.BlockSpec((None, rows.tm, width), per_dir)] * n_out,
        out_shape=[shape] * n_out, scratch_shapes=[pltpu.VMEM((SUBLANES, width), F32)],
        compiler_params=_params("arbitrary", "arbitrary"))(*ins)


def lru_scan(rows, a, bb, name):
    width = a.shape[2]
    n = rows.tm

    def body(a_ref, bb_ref, h_ref, hp_ref, carry):
        d, s = pl.program_id(0), pl.program_id(1)

        @pl.when(s == 0)
        def _():
            carry[...] = jnp.zeros_like(carry)

        def run(reverse):
            for cs in _chunks(width, LANES):
                h_in = carry[0:1, cs]
                h, carry[0:1, cs] = _chain_scan(a_ref[:, cs], bb_ref[:, cs].astype(F32), h_in, reverse)
                h_ref[:, cs] = h.astype(BF16)
                hp_ref[:, cs] = _neighbour(h, h_in, reverse).astype(BF16)

        pl.when(d == 0)(lambda: run(False))
        pl.when(d == 1)(lambda: run(True))

    return _scan_call(rows, body, name, [a, bb], ["dir", "dir"], 2, width, adjoint=False)


def lru_scan_bwd(rows, dh, a, hp, name):
    width = a.shape[2]
    n = rows.tm

    def body(dh_ref, a_ref, hp_ref, da_ref, dbb_ref, carry):
        d, s = pl.program_id(0), pl.program_id(1)

        @pl.when(s == 0)
        def _():
            carry[...] = jnp.zeros_like(carry)

        def run(reverse):
            for cs in _chunks(width, LANES):
                av = a_ref[:, cs]
                g, _ = _chain_scan(_neighbour(av, 1.0, reverse), dh_ref[:, cs].astype(F32), carry[0:1, cs], reverse)
                da_ref[:, cs] = (g * hp_ref[:, cs].astype(F32)).astype(BF16)
                dbb_ref[:, cs] = g.astype(BF16)
                carry[0:1, cs] = (av * g)[0:1] if reverse else (av * g)[n - 1:n]

        pl.when(d == 0)(lambda: run(True))
        pl.when(d == 1)(lambda: run(False))

    return _scan_call(rows, body, name, [dh, a, hp], ["shared", "dir", "dir"], 2, width, adjoint=True)


def mod_matmul(c9, w_shard, b_shard, name):
    n = w_shard.shape[1]
    tn = _pick(n, 768)

    def body(c_ref, w_ref, b_ref, o_ref):
        act = _silu_and_grad(c_ref[...])[0]
        o_ref[...] = jnp.dot(act, w_ref[...], preferred_element_type=F32, precision=lax.Precision.HIGHEST) + b_ref[...]

    return pl.pallas_call(
        body, name=name, grid=(n // tn,),
        in_specs=[pl.BlockSpec(c9.shape, lambda j: (0, 0)), pl.BlockSpec((w_shard.shape[0], tn), lambda j: (0, j)),
                  pl.BlockSpec((1, tn), lambda j: (0, j))],
        out_specs=pl.BlockSpec((c9.shape[0], tn), lambda j: (0, j)), out_shape=jax.ShapeDtypeStruct((c9.shape[0], n), F32),
        compiler_params=_params("arbitrary"))(c9, w_shard, b_shard)


def mod_matmul_bwd(c9, d9, w_shard, name):
    n = w_shard.shape[1]
    tn = _pick(n, 768)
    steps = n // tn

    def body(c_ref, d_ref, w_ref, gw_ref, gc_ref):
        j = pl.program_id(0)
        act, dact = _silu_and_grad(c_ref[...])
        dv = d_ref[...]
        gw_ref[...] = lax.dot_general(act, dv, (((0,), (0,)), ((), ())), preferred_element_type=F32, precision=lax.Precision.HIGHEST)
        part = lax.dot_general(dv, w_ref[...], (((1,), (1,)), ((), ())), preferred_element_type=F32, precision=lax.Precision.HIGHEST)
        _accumulate(gc_ref, part * dact, j == 0)

    return pl.pallas_call(
        body, name=name, grid=(steps,),
        in_specs=[pl.BlockSpec(c9.shape, lambda j: (0, 0)), pl.BlockSpec((c9.shape[0], tn), lambda j: (0, j)),
                  pl.BlockSpec((w_shard.shape[0], tn), lambda j: (0, j))],
        out_specs=[pl.BlockSpec((w_shard.shape[0], tn), lambda j: (0, j)), pl.BlockSpec(c9.shape, lambda j: (0, 0))],
        out_shape=[jax.ShapeDtypeStruct(w_shard.shape, F32), jax.ShapeDtypeStruct(c9.shape, F32)],
        compiler_params=_params("arbitrary"))(c9, d9, w_shard)


def _row_tile(n_rows, n_cols):
    return _pick(n_rows, max(2 * SUBLANES, (256 * 1024) // n_cols), 2 * SUBLANES)


def sum_parts(parts, name):
    n, n_rows, n_cols = parts.shape
    tr = _row_tile(n_rows, n_cols)

    def body(p_ref, o_ref):
        total = p_ref[0].astype(F32)
        for k in range(1, n):
            total = total + p_ref[k].astype(F32)
        o_ref[...] = total

    return pl.pallas_call(
        body, name=name, grid=(n_rows // tr,), in_specs=[pl.BlockSpec((n, tr, n_cols), lambda i: (0, i, 0))],
        out_specs=pl.BlockSpec((tr, n_cols), lambda i: (i, 0)), out_shape=jax.ShapeDtypeStruct((n_rows, n_cols), F32),
        compiler_params=_params("arbitrary"))(parts)


def _adamw_update(g, w_ref, m_ref, v_ref, g_ref, d_ref, m2_ref, v2_ref):
    m2 = ADAM_B1 * m_ref[...] + (1.0 - ADAM_B1) * g
    v2 = ADAM_B2 * v_ref[...] + (1.0 - ADAM_B2) * (g * g)
    m_hat = m2 / (1.0 - ADAM_B1 ** ADAM_STEP)
    v_hat = v2 / (1.0 - ADAM_B2 ** ADAM_STEP)
    g_ref[...] = g
    d_ref[...] = -ADAM_LR * (m_hat / (jnp.sqrt(v_hat) + ADAM_EPS) + ADAM_WD * w_ref[...])
    m2_ref[...] = m2
    v2_ref[...] = v2


def adamw(parts, w, m, v, name):
    n, n_rows, n_cols = parts.shape
    tr = _row_tile(n_rows, n_cols)

    def body(p_ref, *refs):
        g = p_ref[0].astype(F32)
        for k in range(1, n):
            g = g + p_ref[k].astype(F32)
        _adamw_update(g, *refs)

    blk = pl.BlockSpec((tr, n_cols), lambda i: (i, 0))
    shape = jax.ShapeDtypeStruct((n_rows, n_cols), F32)
    return pl.pallas_call(
        body, name=name, grid=(n_rows // tr,), in_specs=[pl.BlockSpec((n, tr, n_cols), lambda i: (0, i, 0)), blk, blk, blk],
        out_specs=[blk] * 4, out_shape=[shape] * 4, compiler_params=_params("arbitrary"))(parts, w, m, v)


def adamw_pair(parts, w, m, v, name):
    n_parts, n_rows, n_cols = parts.shape
    tr = _row_tile(n_rows, n_cols)
    steps = n_rows // tr

    def body(p_ref, w_ref, m_ref, v_ref, g_ref, d_ref, m2_ref, v2_ref, outbox, inbox, send_sems, recv_sems, credits):
        i = pl.program_id(0)
        sibling = (lax.axis_index("x"), lax.axis_index("y"), 1 - lax.axis_index("c"))

        def copy(slot):
            return pltpu.make_async_remote_copy(src_ref=outbox.at[slot], dst_ref=inbox.at[slot], send_sem=send_sems.at[slot],
                                                recv_sem=recv_sems.at[slot], device_id=sibling, device_id_type=MESH)

        @pl.when(i < steps)
        def _():
            slot = i % 2
            total = p_ref[0].astype(F32)
            for k in range(1, n_parts):
                total = total + p_ref[k].astype(F32)
            outbox[slot] = total

            @pl.when(i >= 2)
            def _():
                pl.semaphore_wait(credits.at[slot], 1)

            copy(slot).start()

        @pl.when(i >= 1)
        def _():
            slot = (i - 1) % 2
            landed = copy(slot)
            landed.wait_recv()
            landed.wait_send()
            _adamw_update(outbox[slot] + inbox[slot], w_ref, m_ref, v_ref, g_ref, d_ref, m2_ref, v2_ref)

            @pl.when(i + 1 < steps)
            def _():
                pl.semaphore_signal(credits.at[slot], inc=1, device_id=sibling, device_id_type=MESH)

    blk = pl.BlockSpec((tr, n_cols), lambda i: (jnp.maximum(i - 1, 0), 0))
    shape = jax.ShapeDtypeStruct((n_rows, n_cols), F32)
    return pl.pallas_call(
        body, name=name, grid=(steps + 1,),
        in_specs=[pl.BlockSpec((n_parts, tr, n_cols), lambda i: (0, jnp.minimum(i, steps - 1), 0)), blk, blk, blk],
        out_specs=[blk] * 4, out_shape=[shape] * 4,
        scratch_shapes=[pltpu.VMEM((2, tr, n_cols), F32), pltpu.VMEM((2, tr, n_cols), F32), pltpu.SemaphoreType.DMA((2,)),
                        pltpu.SemaphoreType.DMA((2,)), pltpu.SemaphoreType.REGULAR((2,))],
        compiler_params=_params("arbitrary"))(parts, w, m, v)


def _pack(arrays, row_multiple=SUBLANES):
    flat = [jnp.pad(a.reshape(-1), (0, -a.size % LANES)) for a in arrays]
    buf = jnp.concatenate(flat)
    buf = jnp.pad(buf, (0, -buf.size % (row_multiple * LANES)))
    return buf.reshape(-1, LANES)


def _unpack(buf, shapes):
    lead = buf.shape[:-2]
    flat = buf.reshape(lead + (-1,))
    out, pos = [], 0
    for shape in shapes:
        size = 1
        for s in shape:
            size *= s
        out.append(flat[..., pos:pos + size].reshape(lead + tuple(shape)))
        pos += size + (-size % LANES)
    return out


def _pos_tables(seq_len, dim, tile_rows):
    grid_rows = seq_len // GRID_W
    per_tile = tile_rows // GRID_W
    q = dim // 4
    omega = 1.0 / (10000.0 ** (jnp.arange(q, dtype=F32) / q))
    er = jnp.arange(grid_rows).astype(F32)[:, None] * omega
    ec = jnp.arange(GRID_W).astype(F32)[:, None] * omega
    by_row = jnp.concatenate([jnp.sin(er), jnp.cos(er)], axis=-1).reshape(grid_rows // per_tile, per_tile, 2 * q)
    by_row = jnp.pad(by_row, ((0, 0), (0, SUBLANES - per_tile), (0, 0))).reshape(-1, 2 * q)
    return by_row, jnp.concatenate([jnp.sin(ec), jnp.cos(ec)], axis=-1)


def _cols_from_shards(g):
    return jnp.transpose(g, (1, 0, 2)).reshape(g.shape[1], -1)


def _shards_from_cols(w, n_shards=4):
    k, n = w.shape
    return jnp.transpose(w.reshape(k, n_shards, n // n_shards), (1, 0, 2))


def kernel(x, c, ctx, c_ctx, w_mod, b_mod, g_n1, w_ffn1_up, w_ffn1_down, g_n2, w_in, b_in, w_dw, b_dw, g_ln, b_ln, w_conf_out, w_lru_conv, b_lru_conv, w_rec_gate, b_rec_gate, w_in_gate, b_in_gate, lru_lambda, w_lru_out, w_out, g_n3, w_ffn2_up, w_ffn2_down, g_final, loss_target, m_c_ctx, m_w_mod, m_b_mod, m_g_n1, m_w_ffn1_up, m_w_ffn1_down, m_g_n2, m_w_in, m_b_in, m_w_dw, m_b_dw, m_g_ln, m_b_ln, m_w_conf_out, m_w_lru_conv, m_b_lru_conv, m_w_rec_gate, m_b_rec_gate, m_w_in_gate, m_b_in_gate, m_lru_lambda, m_w_lru_out, m_w_out, m_g_n3, m_w_ffn2_up, m_w_ffn2_down, m_g_final, v_c_ctx, v_w_mod, v_b_mod, v_g_n1, v_w_ffn1_up, v_w_ffn1_down, v_g_n2, v_w_in, v_b_in, v_w_dw, v_b_dw, v_g_ln, v_b_ln, v_w_conf_out, v_w_lru_conv, v_b_lru_conv, v_w_rec_gate, v_b_rec_gate, v_w_in_gate, v_b_in_gate, v_lru_lambda, v_w_lru_out, v_w_out, v_g_n3, v_w_ffn2_up, v_w_ffn2_down, v_g_final):
    given = dict(locals())
    wt = {n: given[n] for n in W_NAMES}
    mom = {n: given["m_" + n] for n in W_NAMES}
    var = {n: given["v_" + n] for n in W_NAMES}

    def flat2(a):
        if a.ndim == 1:
            return a.reshape(1, -1)
        a = a[0]
        return a if a.ndim == 2 else a.reshape(-1, a.shape[-1])

    t_lat, d = x.shape[1], x.shape[2]
    t_ctx = ctx.shape[1]
    rows = Rows(t_lat, t_ctx, d)
    xi, yi, ci = lax.axis_index("x"), lax.axis_index("y"), lax.axis_index("c")
    me, chip = 4 * xi + 2 * yi + ci, 2 * xi + yi
    lru_w = b_lru_conv.shape[-1]
    n_blk = lru_w // LANES
    taps = w_dw.shape[1]
    lru_taps = w_lru_conv.shape[1]
    ffn = w_ffn1_down.shape[1] * 4

    small_shapes = [(1, d)] + [flat2(wt[n]).shape for n in SMALL_SHARDED]
    small_all = exchange("all", [_pack([c] + [flat2(wt[n]) for n in SMALL_SHARDED])], False, "gather_small")[0]
    small_all = _unpack(small_all, small_shapes)
    c_all = small_all[0][:, 0, :]
    full = {n: jnp.concatenate([a[0], a[2], a[4], a[6]], axis=-1) for n, a in zip(SMALL_SHARDED, small_all[1:])}
    big = COL_SHARDED + ROW_SHARDED
    ffn1_names = ['w_ffn1_up', 'w_ffn1_down']
    ffn2_names = ['w_ffn2_up', 'w_ffn2_down']
    mixer_names = [n for n in big if n not in ffn1_names + ffn2_names]
    wb = {}

    def take_weights(names, gathered):
        for n, g in zip(names, gathered):
            wb[n] = _cols_from_shards(g) if n in COL_SHARDED else g.reshape(-1, g.shape[-1])

    def shards_after(names, earlier):
        return lax.optimization_barrier(([wt[n][0].astype(BF16) for n in names], earlier))[0]

    c9 = jnp.concatenate([c_all, c_ctx.reshape(1, d), jnp.zeros((7, d), F32)], axis=0)
    mod_cols = mod_matmul(c9, w_mod[0], lax.dynamic_slice_in_dim(b_mod, chip * w_mod.shape[2], w_mod.shape[2], axis=1), "mod_matmul")
    mod_all = exchange("all", [mod_cols], False, "gather_mod")[0]
    mod9 = jnp.concatenate([mod_all[0], mod_all[2], mod_all[4], mod_all[6]], axis=-1)
    mv = jnp.stack([lax.dynamic_index_in_dim(mod9, me, 0, keepdims=False).reshape(N_MOD, d), mod9[8].reshape(N_MOD, d)])

    gathered_ffn1 = exchange("chips", shards_after(ffn1_names, mod_all), False, "gather_weights_ffn1")
    take_weights(ffn1_names, gathered_ffn1)
    gathered_mixer = exchange_by_sequencer(shards_after(mixer_names, gathered_ffn1), False, "gather_weights_mixer", 1)

    row_tab, col_tab = _pos_tables(t_lat, d, rows.tm)

    def ffn_fwd(tag, h, w_up, w_down):
        g, u, act = matmul_swiglu(h, w_up, f"{tag}_up")
        return (g, u), act, matmul(act, w_down, "nn", F32, f"{tag}_down")

    x0, h1 = embed_norm_mod(rows, x[0], ctx[0], row_tab, col_tab, g_n1, mv, 0, 1, "norm1")
    gu1, act1, f1 = ffn_fwd("ffn1", h1, wb["w_ffn1_up"], wb["w_ffn1_down"])
    gathered_mixer, f1 = lax.optimization_barrier((gathered_mixer, f1))
    take_weights(mixer_names, gathered_mixer)
    gathered_ffn2 = exchange_by_sequencer(shards_after(ffn2_names, gathered_mixer), False, "gather_weights_ffn2", 5)
    x1, h2 = norm_mod(rows, x0, g_n2, mv, 3, 4, "norm2", resid=(f1, 2, 0.5))

    conf_w = w_conf_out.shape[1] * 4
    col_groups = [(0, 2 * conf_w), (2 * conf_w, lru_w), (2 * conf_w + lru_w, lru_w), (2 * conf_w + 2 * lru_w, 2 * d)]
    w_in_groups = [wb["w_in"][:, s:s + n] for s, n in col_groups]
    pvg, ux, gl, pgc = [matmul(h2, w_g, "nn", BF16, f"in_proj{k}", bias=b_in[:, s:s + n])
                        for k, (w_g, (s, n)) in enumerate(zip(w_in_groups, col_groups))]
    u = glu(rows, pvg, "glu")
    u2 = dwconv(rows, u, full["w_dw"], b_dw, taps // 2, F32, "conf_conv")
    s_act = ln_silu(rows, u2, g_ln, b_ln, "conf_ln")
    yc = matmul(s_act, wb["w_conf_out"], "nn", BF16, "conf_out")

    xr = dwconv(rows, ux, full["w_lru_conv"], b_lru_conv, LRU_PAD_LEFT, F32, "lru_conv")
    w_rec, w_ing = w_rec_gate[0].astype(BF16), w_in_gate[0].astype(BF16)
    wcat = jnp.concatenate([w_rec[0], w_ing[0], w_rec[1], w_ing[1]], axis=-1)

    def per_block(a):
        return jnp.transpose(a.reshape(2, n_blk, LANES), (1, 0, 2)).reshape(n_blk, 1, 2 * LANES)

    b_rec, b_ing = full["b_rec_gate"].reshape(2, n_blk, LANES), full["b_in_gate"].reshape(2, n_blk, LANES)
    bcat = jnp.concatenate([b_rec[0], b_ing[0], b_rec[1], b_ing[1]], axis=-1).reshape(n_blk, 1, 4 * LANES)
    lam = per_block(full["lru_lambda"])
    a_gate, b_gate = lru_gates(rows, xr, wcat, bcat, lam, "lru_gates")
    h_scan, h_prev = lru_scan(rows, a_gate, b_gate, "lru_scan")
    yl_in = lru_merge(rows, h_scan, gl, "lru_merge")
    yl = matmul(yl_in, wb["w_lru_out"], "nn", BF16, "lru_out")
    z = z_merge(rows, pgc, yc, yl, "z_merge")
    y = matmul(z, wb["w_out"], "nn", F32, "mix_out")

    gathered_ffn2, y = lax.optimization_barrier((gathered_ffn2, y))
    take_weights(ffn2_names, gathered_ffn2)
    x2, h3 = norm_mod(rows, x1, g_n3, mv, 6, 7, "norm3", resid=(y, 5, 1.0))
    gu2, act2, f2 = ffn_fwd("ffn2", h3, wb["w_ffn2_up"], wb["w_ffn2_down"])
    loss_part, dx3, dg_final, df2, dgate2 = loss_head(rows, x2, f2, mv, g_final.reshape(1, d), loss_target[0], "loss_head")

    grads = {"g_final": dg_final}
    dmv = [None] * N_MOD

    def ffn_bwd(tag, dxn, df, x_prev, h, gu, act, w_up, w_down, g, ks, collective_id, settle_first=(), **norm_bwd_options):
        k_shift, k_scale = ks
        grads[f"w_{tag}_down"] = matmul(act, df, "tn", BF16, f"{tag}_down_dw")
        if settle_first:
            df = settle(settle_first, df)
        dg_, du_ = matmul_swiglu_bwd(df, w_down, gu[0], gu[1], f"{tag}_down_dx")
        grads[f"w_{tag}_up"] = jnp.concatenate([matmul(h, dg_, "tn", BF16, f"{tag}_up_dw_g"), matmul(h, du_, "tn", BF16, f"{tag}_up_dw_u")], axis=1)
        dg_, du_ = scatter_behind([f"w_{tag}_up", f"w_{tag}_down"], tag, collective_id, (dg_, du_))
        dh = matmul_pair_nt(dg_, du_, w_up, f"{tag}_up_dx")
        dx, dg, dmv[k_shift], dmv[k_scale], *more = norm_mod_bwd(rows, x_prev, dh, dxn, g, mv, k_shift, k_scale, f"{tag}_norm_bwd",
                                                                 **norm_bwd_options)
        return (dx, dg, *more)

    def grad_pieces(names):
        return [(_shards_from_cols(grads[n]) if n in COL_SHARDED else grads[n].reshape(4, -1, grads[n].shape[-1])).astype(BF16)
                for n in names]

    from_chips = {}

    def scatter_behind(names, tag, collective_id, carry):
        ready, carry = lax.optimization_barrier(([grads[n] for n in names], carry))
        grads.update(zip(names, ready))
        from_chips.update(zip(names, exchange_by_sequencer(grad_pieces(names), True, f"scatter_grads_{tag}", collective_id)))
        return carry

    def settle(names, carry):
        landed, carry = lax.optimization_barrier(([from_chips[n] for n in names], carry))
        from_chips.update(zip(names, landed))
        return carry

    dmv[8] = dgate2
    dx2, grads["g_n3"], dy, dmv[5] = ffn_bwd("ffn2", dx3, df2, x2, h3, gu2, act2, wb["w_ffn2_up"], wb["w_ffn2_down"], g_n3, (6, 7), 2,
                                             then_gate=(y, 5, 1.0))

    grads["w_out"] = matmul(z, dy, "tn", BF16, "mix_out_dw")
    dz = matmul(dy, wb["w_out"], "nt", BF16, "mix_out_dx")
    dyc, dyl, dpgc = z_merge_bwd(rows, pgc, yc, yl, dz, "z_merge_bwd")

    grads["w_conf_out"] = matmul(s_act, dyc, "tn", BF16, "conf_out_dw")
    ds_act = matmul(dyc, wb["w_conf_out"], "nt", BF16, "conf_out_dx")
    du2, grads["g_ln"], grads["b_ln"] = ln_silu_bwd(rows, u2, ds_act, g_ln, b_ln, "conf_ln_bwd")
    grads["w_dw"], grads["b_dw"] = dwconv_bwd_w(rows, u, du2, taps, taps // 2, "conf_conv_dw")
    du = dwconv(rows, du2, full["w_dw"][::-1], None, taps - 1 - taps // 2, F32, "conf_conv_dx")
    dpvg = glu_bwd(rows, pvg, du, "glu_bwd")

    grads["w_lru_out"] = matmul(yl_in, dyl, "tn", BF16, "lru_out_dw")
    dyl = scatter_behind(['w_out', 'w_conf_out', 'w_lru_out'], "mixer", 3, settle(ffn2_names, dyl))
    dyl_in = matmul(dyl, wb["w_lru_out"], "nt", BF16, "lru_out_dx")
    dh_sum, dgl = lru_merge_bwd(rows, h_scan, gl, dyl_in, "lru_merge_bwd")
    da_gate, db_gate = lru_scan_bwd(rows, dh_sum, a_gate, h_prev, "lru_scan_bwd")
    dxr, dwcat, dbcat, dlam = lru_gates_bwd(rows, xr, wcat, bcat, lam, da_gate, db_gate, "lru_gates_bwd")
    grads["w_rec_gate"] = jnp.stack([dwcat[:, :, 0:LANES], dwcat[:, :, 2 * LANES:3 * LANES]])
    grads["w_in_gate"] = jnp.stack([dwcat[:, :, LANES:2 * LANES], dwcat[:, :, 3 * LANES:4 * LANES]])
    dbcat = dbcat.reshape(n_blk, 4, LANES)
    grads["b_rec_gate"] = jnp.stack([dbcat[:, 0], dbcat[:, 2]]).reshape(2, lru_w)
    grads["b_in_gate"] = jnp.stack([dbcat[:, 1], dbcat[:, 3]]).reshape(2, lru_w)
    grads["lru_lambda"] = jnp.transpose(dlam.reshape(n_blk, 2, LANES), (1, 0, 2)).reshape(2, lru_w)
    grads["w_lru_conv"], grads["b_lru_conv"] = dwconv_bwd_w(rows, ux, dxr, lru_taps, LRU_PAD_LEFT, "lru_conv_dw")
    dux = dwconv(rows, dxr, full["w_lru_conv"][::-1], None, lru_taps - 1 - LRU_PAD_LEFT, BF16, "lru_conv_dx")

    dproj = [dpvg, dux, dgl, dpgc]
    grads["w_in"] = jnp.concatenate([matmul(h2, dp, "tn", BF16, f"in_proj{k}_dw") for k, dp in enumerate(dproj)], axis=1)
    dproj = scatter_behind(['w_in'], "in_proj", 4, settle(['w_out', 'w_conf_out', 'w_lru_out'], dproj))
    dh2, db_in = matmul_groups_nt(dproj, w_in_groups, "in_proj_dx")
    grads["b_in"] = jnp.concatenate(db_in, axis=1)
    dx1, grads["g_n2"], dmv[3], dmv[4], df1, dmv[2] = norm_mod_bwd(rows, x1, dh2, dx2, g_n2, mv, 3, 4, "norm2_bwd", then_gate=(f1, 2, 0.5))

    dx0, grads["g_n1"] = ffn_bwd("ffn1", dx1, df1, x0, h1, gu1, act1, wb["w_ffn1_up"], wb["w_ffn1_down"], g_n1, (0, 1), 6,
                                 settle_first=['w_in'], latent_only=True)
    grad_x = dx0[None]

    out = {}
    for n in big:
        parts = from_chips[n].reshape(4, -1, from_chips[n].shape[-1])
        out[n] = adamw_pair(parts, flat2(wt[n]), flat2(mom[n]), flat2(var[n]), f"adamw_{n}")

    dmod = jnp.concatenate(dmv, axis=1)
    small_names = ['g_n1', 'g_n2', 'b_in', 'b_dw', 'g_ln', 'b_ln', 'b_lru_conv', 'g_n3', 'g_final'] + SMALL_SHARDED
    gate_names = ['w_rec_gate', 'w_in_gate']
    small_list = [loss_part] + [grads[n] for n in small_names] + [dmod[0], dmod[1]]
    small_buf = _pack(small_list, 8 * 2 * SUBLANES)
    eighths = [small_buf.reshape(8, -1, LANES)] + [grads[n].reshape(8, -1, LANES) for n in gate_names]
    *parts, dmod_all = exchange("all", eighths + [_pack([dmod[0]])], [True] * len(eighths) + [False], "scatter_small_grads")
    sums = exchange("all", [sum_parts(p, f"sum_small_grads{k}") for k, p in enumerate(parts)], False, "gather_small_grads")
    small_sum = _unpack(sums[0].reshape(small_buf.shape), [a.shape for a in small_list])
    loss = small_sum[0][0, 0]
    total = dict(zip(small_names, small_sum[1:]))
    total.update({n: s.reshape(grads[n].shape) for n, s in zip(gate_names, sums[1:])})
    dmod_all = _unpack(dmod_all, [dmod[0].shape])[0].reshape(8, N_MOD * d)
    dmc = small_sum[-1].reshape(1, N_MOD * d)
    total["b_mod"] = small_sum[-2].reshape(1, N_MOD * d) + dmc

    d9 = jnp.concatenate([dmod_all, dmc, jnp.zeros((7, N_MOD * d), F32)], axis=0)
    d9_cols = lax.dynamic_slice_in_dim(d9, chip * w_mod.shape[2], w_mod.shape[2], axis=1)
    g_wmod, dc9 = mod_matmul_bwd(c9, d9_cols, w_mod[0], "mod_matmul_bwd")
    dc_all = exchange("all", [dc9[8:16]], False, "gather_dc")[0]
    total["c_ctx"] = sum_parts(jnp.stack([dc_all[0], dc_all[2], dc_all[4], dc_all[6]]), "sum_dc")[0:1]
    out["w_mod"] = adamw(g_wmod[None], w_mod[0], m_w_mod[0], v_w_mod[0], "adamw_w_mod")

    for n in gate_names:
        out[n] = adamw(total[n].reshape(1, -1, LANES), flat2(wt[n]), flat2(mom[n]), flat2(var[n]), f"adamw_{n}")

    small_all_names = [n for n in W_NAMES if n not in out]
    g_local, shapes = [], []
    for n in small_all_names:
        g = total[n].reshape(flat2(wt[n]).shape[:-1] + (-1,)) if n in SMALL_SHARDED else total[n].reshape(flat2(wt[n]).shape)
        if n in SMALL_SHARDED:
            width = wt[n].shape[-1]
            g = lax.dynamic_slice_in_dim(g, chip * width, width, axis=1)
        g_local.append(g)
        shapes.append(g.shape)
    packed = adamw(_pack(g_local)[None], _pack([flat2(wt[n]) for n in small_all_names]),
                   _pack([flat2(mom[n]) for n in small_all_names]), _pack([flat2(var[n]) for n in small_all_names]), "adamw_small")
    for k, n in enumerate(small_all_names):
        out[n] = [_unpack(p, shapes)[k] for p in packed]

    results = [loss, grad_x]
    for k in range(4):
        results += [out[n][k].reshape(wt[n].shape) for n in W_NAMES]
    return tuple(results)
```

```python
import functools

import jax
import jax.numpy as jnp
from jax import lax
from jax.experimental import pallas as pl
from jax.experimental.pallas import tpu as pltpu
from jax.experimental.pallas import tpu_sc as plsc

F32 = jnp.float32
BF16 = jnp.bfloat16
MESH = pl.DeviceIdType.MESH

EPS = 1e-6
GRID_W = 64
N_MOD = 9
LRU_C = 8.0
LRU_PAD_LEFT = 2
ADAM_LR, ADAM_B1, ADAM_B2, ADAM_EPS, ADAM_WD, ADAM_STEP = 0.001, 0.9, 0.999, 1e-08, 0.01, 10

LANES = 128
SUBLANES = 8
HALO = 16
VMEM_LIMIT = 56 * 1024 * 1024

W_NAMES = ['c_ctx', 'w_mod', 'b_mod', 'g_n1', 'w_ffn1_up', 'w_ffn1_down', 'g_n2', 'w_in', 'b_in', 'w_dw', 'b_dw',
           'g_ln', 'b_ln', 'w_conf_out', 'w_lru_conv', 'b_lru_conv', 'w_rec_gate', 'b_rec_gate', 'w_in_gate',
           'b_in_gate', 'lru_lambda', 'w_lru_out', 'w_out', 'g_n3', 'w_ffn2_up', 'w_ffn2_down', 'g_final']
COL_SHARDED = ['w_ffn1_up', 'w_in', 'w_ffn2_up']
ROW_SHARDED = ['w_ffn1_down', 'w_conf_out', 'w_lru_out', 'w_out', 'w_ffn2_down']
SMALL_SHARDED = ['w_dw', 'w_lru_conv', 'b_rec_gate', 'b_in_gate', 'lru_lambda']


def _params(*semantics):
    return pltpu.CompilerParams(dimension_semantics=semantics, vmem_limit_bytes=VMEM_LIMIT)


def _pick(n, target, mult=LANES):
    best = None
    for t in range(mult, min(n, target) + 1, mult):
        if n % t == 0:
            best = t
    return best or n


def _chunks(width, target=512):
    w = _pick(width, target)
    return [slice(s, s + w) for s in range(0, width, w)]


def _sigmoid(x):
    return jax.nn.sigmoid(x)


def _silu_and_grad(x):
    s = _sigmoid(x)
    return x * s, s * (1.0 + x * (1.0 - s))


_GELU_K = 0.7978845608028654


def _gelu_and_grad(x):
    t = jnp.tanh(_GELU_K * (x + 0.044715 * (x * x * x)))
    return 0.5 * x * (1.0 + t), 0.5 * (1.0 + t) + 0.5 * x * (1.0 - t * t) * _GELU_K * (1.0 + 3 * 0.044715 * x * x)


def _neg_expm1(z):
    series = -(z * (1.0 + z * (0.5 + z * (1.0 / 6.0))))
    return jnp.where(z > -0.01, series, 1.0 - jnp.exp(z))


def _softplus(x):
    return jnp.maximum(x, 0.0) + jnp.log1p(jnp.exp(-jnp.abs(x)))


def _accumulate(ref, value, first):
    @pl.when(first)
    def _():
        ref[...] = value

    @pl.when(jnp.logical_not(first))
    def _():
        ref[...] += value


def _colsum(x):
    return jnp.sum(x, axis=0, keepdims=True)


class Rows:
    def __init__(self, t_lat, t_ctx, d_model):
        self.tm = _pick(t_ctx, 256, HALO)
        assert t_lat % self.tm == 0 and t_ctx % self.tm == 0
        self.n_lat = t_lat // self.tm
        self.n_all = (t_lat + t_ctx) // self.tm
        self.t_all = t_lat + t_ctx
        self.d = d_model

    def seg(self, i):
        return jnp.where(i >= self.n_lat, 1, 0)

    def seg_first(self, i):
        return jnp.logical_or(i == 0, i == self.n_lat)

    def seg_last(self, i):
        return jnp.logical_or(i == self.n_lat - 1, i == self.n_all - 1)

    def row(self, width, col=0):
        return pl.BlockSpec((self.tm, width), lambda i: (i, col))

    def vec(self, width):
        return pl.BlockSpec((1, width), lambda i: (0, 0))

    def full(self, shape):
        return pl.BlockSpec(shape, lambda i: (0,) * len(shape))

    def mod(self):
        return pl.BlockSpec((None, N_MOD, self.d), lambda i: (self.seg(i), 0, 0))

    def segsum(self):
        return pl.BlockSpec((None, 1, self.d), lambda i: (self.seg(i), 0, 0))

    def segsum_shape(self):
        return jax.ShapeDtypeStruct((2, 1, self.d), F32)

    def halo(self, width, side):
        per = self.tm // HALO
        last = self.t_all // HALO - 1
        if side < 0:
            return pl.BlockSpec((HALO, width), lambda i: (jnp.maximum(i * per - 1, 0), 0))
        return pl.BlockSpec((HALO, width), lambda i: (jnp.minimum((i + 1) * per, last), 0))

    def call(self, body, name, in_specs, out_specs, out_shape, scratch=()):
        return pl.pallas_call(body, name=name, grid=(self.n_all,), in_specs=in_specs, out_specs=out_specs,
                              out_shape=out_shape, scratch_shapes=list(scratch), compiler_params=_params("arbitrary"))


_PEER_FLIPS = {
    "chips": [(1, 0, 0), (0, 1, 0), (1, 1, 0)],
    "sibling": [(0, 0, 1)],
    "all": [(0, 0, 1), (0, 1, 0), (0, 1, 1), (1, 0, 0), (1, 0, 1), (1, 1, 0), (1, 1, 1)],
}


def _slot(kind, x, y, c):
    return {"chips": 2 * x + y, "sibling": c, "all": 4 * x + 2 * y + c}[kind]


def exchange(kind, srcs, indexed, name):
    flips = _PEER_FLIPS[kind]
    n_slots = len(flips) + 1
    n_t = len(srcs)
    by_slot = list(indexed) if isinstance(indexed, (list, tuple)) else [indexed] * n_t
    out_shapes = [jax.ShapeDtypeStruct(s.shape if ix else (n_slots,) + s.shape, s.dtype) for s, ix in zip(srcs, by_slot)]

    def body(*refs):
        src_refs, dst_refs = refs[:n_t], refs[n_t:2 * n_t]
        send_sems, recv_sems, local_sems = refs[2 * n_t:]
        x, y, c = lax.axis_index("x"), lax.axis_index("y"), lax.axis_index("c")
        me = _slot(kind, x, y, c)

        def piece(t, k):
            return src_refs[t].at[k] if by_slot[t] else src_refs[t]

        peers = [(1 - x if fx else x, 1 - y if fy else y, 1 - c if fc else c) for fx, fy, fc in flips]
        local = [pltpu.make_async_copy(piece(t, me), dst_refs[t].at[me], local_sems.at[t]) for t in range(n_t)]
        for cp in local:
            cp.start()
        sends = []
        for t in range(n_t):
            for p, peer in enumerate(peers):
                sends.append(pltpu.make_async_remote_copy(
                    src_ref=piece(t, _slot(kind, *peer)), dst_ref=dst_refs[t].at[me],
                    send_sem=send_sems.at[t, p], recv_sem=recv_sems.at[t, p], device_id=peer, device_id_type=MESH))
        for cp in sends:
            cp.start()
        for t in range(n_t):
            for p, peer in enumerate(peers):
                theirs = _slot(kind, *peer)
                pltpu.make_async_remote_copy(
                    src_ref=piece(t, me), dst_ref=dst_refs[t].at[theirs],
                    send_sem=send_sems.at[t, p], recv_sem=recv_sems.at[t, p], device_id=peer, device_id_type=MESH).wait_recv()
        for cp in sends:
            cp.wait_send()
        for cp in local:
            cp.wait()

    any_spec = pl.BlockSpec(memory_space=pl.ANY)
    outs = pl.pallas_call(
        body, name=name, out_shape=out_shapes, in_specs=[any_spec] * n_t, out_specs=[any_spec] * n_t,
        scratch_shapes=[pltpu.SemaphoreType.DMA((n_t, len(flips))), pltpu.SemaphoreType.DMA((n_t, len(flips))),
                        pltpu.SemaphoreType.DMA((n_t,))],
        compiler_params=pltpu.CompilerParams(has_side_effects=True),
    )(*srcs)
    return list(outs)


def exchange_by_sequencer(srcs, indexed, name, collective_id):
    flips = _PEER_FLIPS["chips"]
    n_t, n_p = len(srcs), len(flips)
    src_refs = [jax.new_ref(s, memory_space=pltpu.MemorySpace.HBM) for s in srcs]
    land_refs = [jax.empty_ref(jax.ShapeDtypeStruct(s.shape if indexed else (n_p + 1,) + s.shape, s.dtype),
                               memory_space=pltpu.MemorySpace.HBM) for s in srcs]

    @pl.kernel(mesh=plsc.ScalarSubcoreMesh(axis_name="sequencer", num_cores=1), name=name,
               scratch_types=(pltpu.SemaphoreType.DMA((n_t, n_p)), pltpu.SemaphoreType.DMA((n_t, n_p)), pltpu.SemaphoreType.DMA((n_t,))),
               compiler_params=pltpu.CompilerParams(collective_id=collective_id))
    def launch(send_sems, recv_sems, local_sems):
        x, y, c = lax.axis_index("x"), lax.axis_index("y"), lax.axis_index("c")
        peers = [(1 - x if fx else x, 1 - y if fy else y, c) for fx, fy, _ in flips]
        barrier = pltpu.get_barrier_semaphore()
        for peer in peers:
            pl.semaphore_signal(barrier, inc=1, device_id=peer, device_id_type=MESH)
        pl.semaphore_wait(barrier, n_p)
        me = _slot("chips", x, y, c)

        def piece(t, k):
            return src_refs[t].at[k] if indexed else src_refs[t]

        local = [pltpu.make_async_copy(piece(t, me), land_refs[t].at[me], local_sems.at[t]) for t in range(n_t)]
        for cp in local:
            cp.start()
        sends = []
        for t in range(n_t):
            for p, peer in enumerate(peers):
                sends.append(pltpu.make_async_remote_copy(
                    src_ref=piece(t, _slot("chips", *peer)), dst_ref=land_refs[t].at[me],
                    send_sem=send_sems.at[t, p], recv_sem=recv_sems.at[t, p], device_id=peer, device_id_type=MESH))
        for cp in sends:
            cp.start()
        for t in range(n_t):
            for p, peer in enumerate(peers):
                pltpu.make_async_remote_copy(
                    src_ref=piece(t, me), dst_ref=land_refs[t].at[_slot("chips", *peer)],
                    send_sem=send_sems.at[t, p], recv_sem=recv_sems.at[t, p], device_id=peer, device_id_type=MESH).wait_recv()
        for cp in sends:
            cp.wait_send()
        for cp in local:
            cp.wait()

    launch()
    return [r[...] for r in land_refs]


MATMUL_VMEM_BUDGET = 44 * 1024 * 1024
MATMUL_STEP_BYTES = 1024 * 1024


def _divisors(n, most):
    return [t for t in range(LANES, min(n, most) + 1, LANES) if n % t == 0] or [n]


def _matmul_tiles(n_i, n_j, n_r, a_bytes, b_bytes, out_bytes):
    best = None
    for tr in _divisors(n_r, n_r):
        steps = n_r // tr
        for ti in _divisors(n_i, 1408):
            for tj in _divisors(n_j, 1664):
                vmem = 2 * (ti * tr * a_bytes + tr * tj * b_bytes + ti * tj * out_bytes) + (ti * tj * 4 if steps > 1 else 0)
                if vmem > MATMUL_VMEM_BUDGET or ti < min(n_i, 2 * LANES) or tj < min(n_j, 4 * LANES):
                    continue
                size_a, size_b = n_i * n_r * a_bytes, n_r * n_j * b_bytes
                if steps == 1:
                    moved = min(size_a + (n_i // ti) * size_b, size_b + (n_j // tj) * size_a)
                else:
                    moved = (n_j // tj) * size_a + (n_i // ti) * size_b + (steps - 1) * n_i * n_j * 8
                moved += n_i * n_j * out_bytes + (n_i // ti) * (n_j // tj) * steps * MATMUL_STEP_BYTES
                if best is None or moved < best[0]:
                    best = (moved, ti, tj, tr)
    return best[1:]


def matmul(a, b, mode, out_dtype, name, bias=None):
    if mode == "nn":
        (n_i, n_r), (_, n_j) = a.shape, b.shape
    elif mode == "nt":
        (n_i, n_r), (n_j, _) = a.shape, b.shape
    else:
        (n_r, n_i), (_, n_j) = a.shape, b.shape
    ti, tj, tr = _matmul_tiles(n_i, n_j, n_r, a.dtype.itemsize, b.dtype.itemsize, jnp.dtype(out_dtype).itemsize)
    steps = n_r // tr
    bytes_a, bytes_b = a.size * a.dtype.itemsize, b.size * b.dtype.itemsize
    j_outer = steps == 1 and bytes_b + (n_j // tj) * bytes_a < bytes_a + (n_i // ti) * bytes_b

    def at(fn):
        return (lambda j, i, r: fn(i, j, r)) if j_outer else fn

    a_spec = {"nn": pl.BlockSpec((ti, tr), at(lambda i, j, r: (i, r))), "nt": pl.BlockSpec((ti, tr), at(lambda i, j, r: (i, r))),
              "tn": pl.BlockSpec((tr, ti), at(lambda i, j, r: (r, i)))}[mode]
    b_spec = {"nn": pl.BlockSpec((tr, tj), at(lambda i, j, r: (r, j))), "nt": pl.BlockSpec((tj, tr), at(lambda i, j, r: (j, r))),
              "tn": pl.BlockSpec((tr, tj), at(lambda i, j, r: (r, j)))}[mode]
    dims = {"nn": (((1,), (0,)), ((), ())), "nt": (((1,), (1,)), ((), ())), "tn": (((0,), (0,)), ((), ()))}[mode]

    def body(*refs):
        a_ref, b_ref = refs[:2]
        bias_ref = refs[2] if bias is not None else None
        o_ref = refs[3] if bias is not None else refs[2]
        prod = lax.dot_general(a_ref[...].astype(BF16), b_ref[...].astype(BF16), dims, preferred_element_type=F32)

        def finish(total):
            if bias_ref is not None:
                total = total + bias_ref[...]
            o_ref[...] = total.astype(out_dtype)

        if steps == 1:
            finish(prod)
        else:
            acc = refs[-1]
            r = pl.program_id(2)

            @pl.when(r == 0)
            def _():
                acc[...] = prod

            @pl.when(r > 0)
            def _():
                acc[...] += prod

            @pl.when(r == steps - 1)
            def _():
                finish(acc[...])

    in_specs = [a_spec, b_spec] + ([pl.BlockSpec((1, tj), at(lambda i, j, r: (0, j)))] if bias is not None else [])
    args = (a, b) + ((bias,) if bias is not None else ())
    grid = (n_j // tj, n_i // ti, steps) if j_outer else (n_i // ti, n_j // tj, steps)
    return pl.pallas_call(
        body, name=name, grid=grid, in_specs=in_specs,
        out_specs=pl.BlockSpec((ti, tj), at(lambda i, j, r: (i, j))), out_shape=jax.ShapeDtypeStruct((n_i, n_j), out_dtype),
        scratch_shapes=[pltpu.VMEM((ti, tj), F32)] if steps > 1 else [],
        compiler_params=_params("arbitrary", "arbitrary", "arbitrary"),
    )(*args)


def matmul_swiglu(h, w_up, name):
    n_t, d = h.shape
    f = w_up.shape[1] // 2
    ti, tj = _pick(n_t, 768), _pick(f, 1408)
    n_j = f // tj

    def body(h_ref, wg_ref, wu_ref, g_ref, u_ref, act_ref):
        hv = h_ref[...]
        g = jnp.dot(hv, wg_ref[...], preferred_element_type=F32)
        u = jnp.dot(hv, wu_ref[...], preferred_element_type=F32)
        g_ref[...] = g.astype(BF16)
        u_ref[...] = u.astype(BF16)
        act_ref[...] = (_silu_and_grad(g)[0] * u).astype(BF16)

    out = pl.BlockSpec((ti, tj), lambda j, i: (i, j))
    shape = jax.ShapeDtypeStruct((n_t, f), BF16)
    return pl.pallas_call(
        body, name=name, grid=(n_j, n_t // ti),
        in_specs=[pl.BlockSpec((ti, d), lambda j, i: (i, 0)), pl.BlockSpec((d, tj), lambda j, i: (0, j)),
                  pl.BlockSpec((d, tj), lambda j, i: (0, j + n_j))],
        out_specs=[out, out, out], out_shape=[shape, shape, shape], compiler_params=_params("arbitrary", "arbitrary"),
    )(h, w_up, w_up)


def matmul_swiglu_bwd(df, w_down, g, u, name):
    n_t, d = df.shape
    f = w_down.shape[0]
    ti, tj = _pick(n_t, 768), _pick(f, 1408)

    def body(df_ref, w_ref, g_ref, u_ref, dg_ref, du_ref):
        dact = lax.dot_general(df_ref[...], w_ref[...], (((1,), (1,)), ((), ())), preferred_element_type=F32)
        act, dact_dg = _silu_and_grad(g_ref[...].astype(F32))
        dg_ref[...] = (dact * u_ref[...].astype(F32) * dact_dg).astype(BF16)
        du_ref[...] = (dact * act).astype(BF16)

    tile = pl.BlockSpec((ti, tj), lambda j, i: (i, j))
    shape = jax.ShapeDtypeStruct((n_t, f), BF16)
    return pl.pallas_call(
        body, name=name, grid=(f // tj, n_t // ti),
        in_specs=[pl.BlockSpec((ti, d), lambda j, i: (i, 0)), pl.BlockSpec((tj, d), lambda j, i: (j, 0)), tile, tile],
        out_specs=[tile, tile], out_shape=[shape, shape], compiler_params=_params("arbitrary", "arbitrary"))(df, w_down, g, u)


def matmul_groups_nt(parts, weights, name):
    n_g = len(parts)
    n_t, d = parts[0].shape[0], weights[0].shape[0]
    ti = _pick(n_t, 384)
    dims = (((1,), (1,)), ((), ()))

    def body(*refs):
        a_refs, b_refs, o_ref, sum_refs = refs[:n_g], refs[n_g:2 * n_g], refs[2 * n_g], refs[2 * n_g + 1:]
        i = pl.program_id(0)
        total = None
        for a_ref, b_ref, s_ref in zip(a_refs, b_refs, sum_refs):
            av = a_ref[...]
            prod = lax.dot_general(av, b_ref[...], dims, preferred_element_type=F32)
            total = prod if total is None else total + prod
            _accumulate(s_ref, _colsum(av.astype(F32)), i == 0)
        o_ref[...] = total

    widths = [p.shape[1] for p in parts]
    outs = pl.pallas_call(
        body, name=name, grid=(n_t // ti,),
        in_specs=[pl.BlockSpec((ti, w), lambda i: (i, 0)) for w in widths] + [pl.BlockSpec((d, w), lambda i: (0, 0)) for w in widths],
        out_specs=[pl.BlockSpec((ti, d), lambda i: (i, 0))] + [pl.BlockSpec((1, w), lambda i: (0, 0)) for w in widths],
        out_shape=[jax.ShapeDtypeStruct((n_t, d), F32)] + [jax.ShapeDtypeStruct((1, w), F32) for w in widths],
        compiler_params=_params("arbitrary"))(*parts, *weights)
    return outs[0], list(outs[1:])


def matmul_pair_nt(a1, a2, b, name):
    n_t, f = a1.shape
    d = b.shape[0]
    ti = _pick(n_t, 768)
    dims = (((1,), (1,)), ((), ()))

    def body(a1_ref, a2_ref, b1_ref, b2_ref, o_ref):
        o_ref[...] = (lax.dot_general(a1_ref[...], b1_ref[...], dims, preferred_element_type=F32)
                      + lax.dot_general(a2_ref[...], b2_ref[...], dims, preferred_element_type=F32))

    rows_in = pl.BlockSpec((ti, f), lambda i: (i, 0))
    return pl.pallas_call(
        body, name=name, grid=(n_t // ti,),
        in_specs=[rows_in, rows_in, pl.BlockSpec((d, f), lambda i: (0, 0)), pl.BlockSpec((d, f), lambda i: (0, 1))],
        out_specs=pl.BlockSpec((ti, d), lambda i: (i, 0)), out_shape=jax.ShapeDtypeStruct((n_t, d), F32),
        compiler_params=_params("arbitrary"))(a1, a2, b, b)


def norm_mod(rows, x, g, mv, shift_k, scale_k, name, resid=None):
    d = rows.d
    changed = resid is not None

    def body(*refs):
        refs = list(refs)
        x_ref, g_ref, mv_ref = refs[:3]
        rest = refs[3:]
        xv = x_ref[...]
        if resid is not None:
            xv = xv + resid[2] * mv_ref[resid[1]:resid[1] + 1, :] * rest.pop(0)[...]
        if changed:
            rest.pop(0)[...] = xv
        r = lax.rsqrt(jnp.mean(xv * xv, axis=-1, keepdims=True) + EPS)
        h = (xv * r) * g_ref[...]
        if shift_k is not None:
            h = h * (1.0 + mv_ref[scale_k:scale_k + 1, :]) + mv_ref[shift_k:shift_k + 1, :]
        rest.pop(0)[...] = h.astype(BF16)

    ins = [x, g, mv] + ([resid[0]] if resid is not None else [])
    in_specs = [rows.row(d), rows.vec(d), rows.mod()] + [rows.row(d)] * (len(ins) - 3)
    out_shape = ([jax.ShapeDtypeStruct((rows.t_all, d), F32)] if changed else []) + [jax.ShapeDtypeStruct((rows.t_all, d), BF16)]
    outs = rows.call(body, name, in_specs, [rows.row(d)] * len(out_shape), out_shape)(*ins)
    return (outs[0], outs[1]) if changed else (None, outs[0])


def embed_norm_mod(rows, x_lat, x_ctx, row_tab, col_tab, g, mv, shift_k, scale_k, name):
    d = rows.d
    half = d // 2
    per_tile = rows.tm // GRID_W
    assert rows.tm % GRID_W == 0 and per_tile <= SUBLANES

    def body(xl_ref, xc_ref, rt_ref, ct_ref, g_ref, mv_ref, x0_ref, h_ref):
        i = pl.program_id(0)
        by_row = jnp.concatenate([jnp.broadcast_to(rt_ref[r:r + 1, :], (GRID_W, half)) for r in range(per_tile)], axis=0)
        by_col = jnp.concatenate([ct_ref[...]] * per_tile, axis=0)
        xv = jnp.where(i < rows.n_lat, xl_ref[...] + jnp.concatenate([by_row, by_col], axis=1), xc_ref[...])
        x0_ref[...] = xv
        r = lax.rsqrt(jnp.mean(xv * xv, axis=-1, keepdims=True) + EPS)
        h = (xv * r) * g_ref[...]
        h_ref[...] = (h * (1.0 + mv_ref[scale_k:scale_k + 1, :]) + mv_ref[shift_k:shift_k + 1, :]).astype(BF16)

    last = rows.n_lat - 1
    in_specs = [pl.BlockSpec((rows.tm, d), lambda i: (jnp.minimum(i, last), 0)),
                pl.BlockSpec((rows.tm, d), lambda i: (jnp.maximum(i - rows.n_lat, 0), 0)),
                pl.BlockSpec((SUBLANES, half), lambda i: (jnp.minimum(i, last), 0)), rows.full(col_tab.shape), rows.vec(d), rows.mod()]
    return rows.call(body, name, in_specs, [rows.row(d), rows.row(d)],
                     [jax.ShapeDtypeStruct((rows.t_all, d), F32), jax.ShapeDtypeStruct((rows.t_all, d), BF16)])(
                         x_lat, x_ctx, row_tab, col_tab, g, mv)


def norm_mod_bwd(rows, x, dh, dxn, g, mv, shift_k, scale_k, name, latent_only=False, then_gate=None):
    d = rows.d
    n_dx = rows.n_lat if latent_only else rows.n_all

    def body(*refs):
        x_ref, dh_ref, dxn_ref, g_ref, mv_ref = refs[:5]
        dx_ref, dg_ref, dsh_ref, dsc_ref = refs[-4:] if then_gate is None else refs[6:10]
        i = pl.program_id(0)
        xv, dhv, gv = x_ref[...], dh_ref[...], g_ref[...]
        r = lax.rsqrt(jnp.mean(xv * xv, axis=-1, keepdims=True) + EPS)
        n = xv * r
        dy = dhv * (1.0 + mv_ref[scale_k:scale_k + 1, :])
        dn = dy * gv
        dx = dxn_ref[...] + r * (dn - n * jnp.mean(dn * n, axis=-1, keepdims=True))

        @pl.when(i < n_dx)
        def _():
            dx_ref[...] = dx

        _accumulate(dg_ref, _colsum(dy * n), i == 0)
        _accumulate(dsh_ref, _colsum(dhv), rows.seg_first(i))
        _accumulate(dsc_ref, _colsum(dhv * (n * gv)), rows.seg_first(i))
        if then_gate is not None:
            _gate_bwd(rows, i, dx, refs[5], mv_ref, then_gate[1], then_gate[2], refs[10], refs[11])

    dx_spec = pl.BlockSpec((rows.tm, d), lambda i: (jnp.minimum(i, n_dx - 1), 0))
    ins = [x, dh, dxn, g, mv] + ([then_gate[0]] if then_gate is not None else [])
    in_specs = [rows.row(d), rows.row(d), rows.row(d), rows.vec(d), rows.mod()] + ([rows.row(d)] if then_gate is not None else [])
    out_specs = [dx_spec, rows.vec(d), rows.segsum(), rows.segsum()]
    out_shape = [jax.ShapeDtypeStruct((n_dx * rows.tm, d), F32), jax.ShapeDtypeStruct((1, d), F32), rows.segsum_shape(), rows.segsum_shape()]
    if then_gate is not None:
        out_specs += [rows.row(d), rows.segsum()]
        out_shape += [jax.ShapeDtypeStruct((rows.t_all, d), BF16), rows.segsum_shape()]
    return rows.call(body, name, in_specs, out_specs, out_shape)(*ins)


def _gate_bwd(rows, i, dx, f_ref, mv_ref, gate_k, scale, df_ref, dgate_ref):
    df_ref[...] = (scale * mv_ref[gate_k:gate_k + 1, :] * dx).astype(BF16)
    _accumulate(dgate_ref, scale * _colsum(dx * f_ref[...]), rows.seg_first(i))


def glu(rows, pvg, name):
    w = pvg.shape[1] // 2

    def body(v_ref, t_ref, o_ref):
        for cs in _chunks(w):
            o_ref[:, cs] = v_ref[:, cs].astype(F32) * _sigmoid(t_ref[:, cs].astype(F32))

    return rows.call(body, name, [rows.row(w, 0), rows.row(w, 1)], rows.row(w), jax.ShapeDtypeStruct((rows.t_all, w), F32))(pvg, pvg)


def glu_bwd(rows, pvg, du, name):
    w = pvg.shape[1] // 2

    def body(v_ref, t_ref, du_ref, o_ref):
        for cs in _chunks(w):
            s = _sigmoid(t_ref[:, cs].astype(F32))
            duv = du_ref[:, cs]
            o_ref[:, cs] = (duv * s).astype(BF16)
            o_ref[:, slice(w + cs.start, w + cs.stop)] = (duv * v_ref[:, cs].astype(F32) * s * (1.0 - s)).astype(BF16)

    return rows.call(body, name, [rows.row(w, 0), rows.row(w, 1), rows.row(w)], rows.row(2 * w),
                     jax.ShapeDtypeStruct((rows.t_all, 2 * w), BF16))(pvg, pvg, du)


def ln_silu(rows, u, g, b, name):
    w = u.shape[1]

    def body(u_ref, g_ref, b_ref, o_ref):
        uv = u_ref[...]
        xc = uv - jnp.mean(uv, axis=-1, keepdims=True)
        n = xc * lax.rsqrt(jnp.mean(xc * xc, axis=-1, keepdims=True) + EPS)
        o_ref[...] = _silu_and_grad(n * g_ref[...] + b_ref[...])[0].astype(BF16)

    return rows.call(body, name, [rows.row(w), rows.vec(w), rows.vec(w)], rows.row(w),
                     jax.ShapeDtypeStruct((rows.t_all, w), BF16))(u, g, b)


def ln_silu_bwd(rows, u, ds, g, b, name):
    w = u.shape[1]

    def body(u_ref, ds_ref, g_ref, b_ref, du_ref, dg_ref, db_ref):
        i = pl.program_id(0)
        uv, gv = u_ref[...], g_ref[...]
        xc = uv - jnp.mean(uv, axis=-1, keepdims=True)
        r = lax.rsqrt(jnp.mean(xc * xc, axis=-1, keepdims=True) + EPS)
        n = xc * r
        dy = ds_ref[...].astype(F32) * _silu_and_grad(n * gv + b_ref[...])[1]
        dn = dy * gv
        du_ref[...] = r * (dn - jnp.mean(dn, axis=-1, keepdims=True) - n * jnp.mean(dn * n, axis=-1, keepdims=True))
        _accumulate(dg_ref, _colsum(dy * n), i == 0)
        _accumulate(db_ref, _colsum(dy), i == 0)

    return rows.call(body, name, [rows.row(w), rows.row(w), rows.vec(w), rows.vec(w)], [rows.row(w), rows.vec(w), rows.vec(w)],
                     [jax.ShapeDtypeStruct((rows.t_all, w), F32), jax.ShapeDtypeStruct((1, w), F32),
                      jax.ShapeDtypeStruct((1, w), F32)])(u, ds, g, b)


def lru_merge(rows, h, gl, name):
    w = gl.shape[1]

    def body(h_ref, gl_ref, o_ref):
        for cs in _chunks(w):
            h_sum = h_ref[0, :, cs].astype(F32) + h_ref[1, :, cs].astype(F32)
            o_ref[:, cs] = (h_sum * _gelu_and_grad(gl_ref[:, cs].astype(F32))[0]).astype(BF16)

    return rows.call(body, name, [pl.BlockSpec((2, rows.tm, w), lambda i: (0, i, 0)), rows.row(w)], rows.row(w),
                     jax.ShapeDtypeStruct((rows.t_all, w), BF16))(h, gl)


def lru_merge_bwd(rows, h, gl, dy, name):
    w = gl.shape[1]

    def body(h_ref, gl_ref, dy_ref, dh_ref, dgl_ref):
        for cs in _chunks(w):
            act, dact = _gelu_and_grad(gl_ref[:, cs].astype(F32))
            dyv = dy_ref[:, cs].astype(F32)
            dh_ref[:, cs] = (dyv * act).astype(BF16)
            dgl_ref[:, cs] = (dyv * (h_ref[0, :, cs].astype(F32) + h_ref[1, :, cs].astype(F32)) * dact).astype(BF16)

    return rows.call(body, name, [pl.BlockSpec((2, rows.tm, w), lambda i: (0, i, 0)), rows.row(w), rows.row(w)],
                     [rows.row(w), rows.row(w)],
                     [jax.ShapeDtypeStruct((rows.t_all, w), BF16), jax.ShapeDtypeStruct((rows.t_all, w), BF16)])(h, gl, dy)


def z_merge(rows, pgc, yc, yl, name):
    d = rows.d

    def body(gc_ref, gr_ref, yc_ref, yl_ref, o_ref):
        for cs in _chunks(d):
            o_ref[:, cs] = (_sigmoid(gc_ref[:, cs].astype(F32)) * yc_ref[:, cs].astype(F32)
                            + _sigmoid(gr_ref[:, cs].astype(F32)) * yl_ref[:, cs].astype(F32)).astype(BF16)

    return rows.call(body, name, [rows.row(d, 0), rows.row(d, 1), rows.row(d), rows.row(d)], rows.row(d),
                     jax.ShapeDtypeStruct((rows.t_all, d), BF16))(pgc, pgc, yc, yl)


def z_merge_bwd(rows, pgc, yc, yl, dz, name):
    d = rows.d

    def body(gc_ref, gr_ref, yc_ref, yl_ref, dz_ref, dyc_ref, dyl_ref, dp_ref):
        for cs in _chunks(d):
            sc, sr = _sigmoid(gc_ref[:, cs].astype(F32)), _sigmoid(gr_ref[:, cs].astype(F32))
            dzv = dz_ref[:, cs].astype(F32)
            dyc_ref[:, cs] = (dzv * sc).astype(BF16)
            dyl_ref[:, cs] = (dzv * sr).astype(BF16)
            dp_ref[:, cs] = (dzv * yc_ref[:, cs].astype(F32) * sc * (1.0 - sc)).astype(BF16)
            dp_ref[:, slice(d + cs.start, d + cs.stop)] = (dzv * yl_ref[:, cs].astype(F32) * sr * (1.0 - sr)).astype(BF16)

    return rows.call(body, name, [rows.row(d, 0), rows.row(d, 1), rows.row(d), rows.row(d), rows.row(d)],
                     [rows.row(d), rows.row(d), rows.row(2 * d)],
                     [jax.ShapeDtypeStruct((rows.t_all, d), BF16), jax.ShapeDtypeStruct((rows.t_all, d), BF16),
                      jax.ShapeDtypeStruct((rows.t_all, 2 * d), BF16)])(pgc, pgc, yc, yl, dz)


def loss_head(rows, x, f, mv, g, target, name):
    d = rows.d

    def body(x_ref, f_ref, mv_ref, g_ref, t_ref, loss_ref, dx_ref, dg_ref, df_ref, dgate_ref):
        i = pl.program_id(0)
        valid = jnp.where(i < rows.n_lat, 1.0, 0.0)
        xv = x_ref[...] + 0.5 * mv_ref[8:9, :] * f_ref[...]
        gv = g_ref[...]
        r = lax.rsqrt(jnp.mean(xv * xv, axis=-1, keepdims=True) + EPS)
        n = xv * r
        err = (n * gv - t_ref[...]) * valid
        part = 0.5 * jnp.sum(jnp.mean(err * err, axis=-1, keepdims=True), axis=0, keepdims=True)
        _accumulate(loss_ref, jnp.broadcast_to(part, (1, LANES)), i == 0)
        dy = err * (1.0 / d)
        dn = dy * gv
        dx = r * (dn - n * jnp.mean(dn * n, axis=-1, keepdims=True))
        dx_ref[...] = dx
        _accumulate(dg_ref, _colsum(dy * n), i == 0)
        _gate_bwd(rows, i, dx, f_ref, mv_ref, 8, 0.5, df_ref, dgate_ref)

    target_spec = pl.BlockSpec((rows.tm, d), lambda i: (jnp.minimum(i, rows.n_lat - 1), 0))
    return rows.call(body, name, [rows.row(d), rows.row(d), rows.mod(), rows.vec(d), target_spec],
                     [rows.vec(LANES), rows.row(d), rows.vec(d), rows.row(d), rows.segsum()],
                     [jax.ShapeDtypeStruct((1, LANES), F32), jax.ShapeDtypeStruct((rows.t_all, d), F32),
                      jax.ShapeDtypeStruct((1, d), F32), jax.ShapeDtypeStruct((rows.t_all, d), BF16), rows.segsum_shape()])(
                          x, f, mv, g, target)


def _extended(rows, i, prev_ref, cur_ref, next_ref, cs):
    prev = jnp.where(rows.seg_first(i), 0.0, prev_ref[:, cs].astype(F32))
    nxt = jnp.where(rows.seg_last(i), 0.0, next_ref[:, cs].astype(F32))
    return jnp.concatenate([prev, cur_ref[:, cs].astype(F32), nxt], axis=0)


class _Shifts:
    def __init__(self, ext, tm):
        self.ext, self.tm, self.rolled = ext, tm, {0: ext}

    def at(self, offset):
        residue = offset % SUBLANES
        if residue not in self.rolled:
            self.rolled[residue] = pltpu.roll(self.ext, self.ext.shape[0] - residue, 0)
        start = HALO + offset - residue
        return self.rolled[residue][start:start + self.tm]


def _pad_taps(w):
    k = w.shape[0]
    return jnp.pad(w, ((0, -k % SUBLANES), (0, 0)))


def dwconv(rows, u, w, b, pad_left, out_dtype, name):
    taps, width = w.shape
    wp = _pad_taps(w)

    def body(*refs):
        prev_ref, cur_ref, next_ref, w_ref = refs[:4]
        b_ref = refs[4] if b is not None else None
        o_ref = refs[-1]
        i = pl.program_id(0)
        for cs in _chunks(width, LANES):
            shifts = _Shifts(_extended(rows, i, prev_ref, cur_ref, next_ref, cs), rows.tm)
            acc = jnp.zeros((rows.tm, LANES), F32) if b_ref is None else jnp.broadcast_to(b_ref[:, cs], (rows.tm, LANES))
            for k in range(taps):
                acc = acc + w_ref[k:k + 1, cs] * shifts.at(k - pad_left)
            o_ref[:, cs] = acc.astype(out_dtype)

    ins = [u, u, u, wp] + ([b] if b is not None else [])
    in_specs = [rows.halo(width, -1), rows.row(width), rows.halo(width, 1), rows.full(wp.shape)] + ([rows.vec(width)] if b is not None else [])
    return rows.call(body, name, in_specs, rows.row(width), jax.ShapeDtypeStruct((rows.t_all, width), out_dtype))(*ins)


def dwconv_bwd_w(rows, u, dy, taps, pad_left, name):
    width = u.shape[1]
    taps_p = taps + (-taps % SUBLANES)

    def body(prev_ref, cur_ref, next_ref, dy_ref, dw_ref, db_ref):
        i = pl.program_id(0)
        tap_row = lax.broadcasted_iota(jnp.int32, (taps_p, LANES), 0)
        for cs in _chunks(width, LANES):
            shifts = _Shifts(_extended(rows, i, prev_ref, cur_ref, next_ref, cs), rows.tm)
            dyv = dy_ref[:, cs]
            total = jnp.zeros((taps_p, LANES), F32)
            for k in range(taps):
                total = total + jnp.where(tap_row == k, _colsum(dyv * shifts.at(k - pad_left)), 0.0)
            _accumulate(dw_ref.at[:, cs], total, i == 0)
            _accumulate(db_ref.at[:, cs], _colsum(dyv), i == 0)

    dw, db = rows.call(body, name, [rows.halo(width, -1), rows.row(width), rows.halo(width, 1), rows.row(width)],
                       [rows.full((taps_p, width)), rows.vec(width)],
                       [jax.ShapeDtypeStruct((taps_p, width), F32), jax.ShapeDtypeStruct((1, width), F32)])(u, u, u, dy)
    return dw[:taps], db


def _gate_values(xr, pre, lam, d):
    r = _sigmoid(pre[:, (2 * d) * LANES:(2 * d + 1) * LANES])
    ig = _sigmoid(pre[:, (2 * d + 1) * LANES:(2 * d + 2) * LANES])
    sp = _softplus(-lam[:, d * LANES:(d + 1) * LANES])
    log_a = -LRU_C * r * sp
    return r, ig, sp, jnp.exp(log_a), _neg_expm1(2.0 * log_a)


def lru_gates(rows, xr, wcat, bcat, lam, name):
    n_blk = wcat.shape[0]
    width = xr.shape[1]

    def body(xr_ref, w_ref, b_ref, lam_ref, a_ref, bb_ref):
        for h in range(n_blk):
            cs = slice(h * LANES, (h + 1) * LANES)
            xv = xr_ref[:, cs]
            pre = jnp.dot(xv.astype(BF16), w_ref[h], preferred_element_type=F32) + b_ref[h]
            for d in range(2):
                _, ig, _, a, one_minus_a2 = _gate_values(xv, pre, lam_ref[h], d)
                a_ref[d, :, cs] = a
                bb_ref[d, :, cs] = (jnp.sqrt(one_minus_a2) * (ig * xv)).astype(BF16)

    both = pl.BlockSpec((2, rows.tm, width), lambda i: (0, i, 0))
    return rows.call(body, name, [rows.row(width), rows.full(wcat.shape), rows.full(bcat.shape), rows.full(lam.shape)], [both, both],
                     [jax.ShapeDtypeStruct((2, rows.t_all, width), F32), jax.ShapeDtypeStruct((2, rows.t_all, width), BF16)])(
                         xr, wcat, bcat, lam)


def lru_gates_bwd(rows, xr, wcat, bcat, lam, da, dbb, name):
    n_blk = wcat.shape[0]
    width = xr.shape[1]

    def body(xr_ref, w_ref, b_ref, lam_ref, da_ref, dbb_ref, dxr_ref, dw_ref, db_ref, dlam_ref):
        i = pl.program_id(0)
        for h in range(n_blk):
            cs = slice(h * LANES, (h + 1) * LANES)
            xv = xr_ref[:, cs]
            xb = xv.astype(BF16)
            wv = w_ref[h]
            pre = jnp.dot(xb, wv, preferred_element_type=F32) + b_ref[h]
            dxr = jnp.zeros_like(xv)
            dpre, dlam = [], []
            for d in range(2):
                r, ig, sp, a, one_minus_a2 = _gate_values(xv, pre, lam_ref[h], d)
                inv_q = lax.rsqrt(one_minus_a2)
                dav, dbv = da_ref[d, :, cs].astype(F32), dbb_ref[d, :, cs].astype(F32)
                dbq = dbv * (one_minus_a2 * inv_q) * ig
                dlog_a = dav * a - dbv * (ig * xv) * ((a * a) * inv_q)
                dpre.append(dlog_a * (-LRU_C * sp) * (r - r * r))
                dpre.append(dbq * xv * (1.0 - ig))
                dxr = dxr + dbq
                dlam.append(_colsum(dlog_a * (-LRU_C * r)) * (-_sigmoid(-lam_ref[h][:, d * LANES:(d + 1) * LANES])))
            dpre = jnp.concatenate(dpre, axis=1)
            dpb = dpre.astype(BF16)
            dxr_ref[:, cs] = dxr + lax.dot_general(dpb, wv, (((1,), (1,)), ((), ())), preferred_element_type=F32)
            _accumulate(dw_ref.at[h], lax.dot_general(xb, dpb, (((0,), (0,)), ((), ())), preferred_element_type=F32), i == 0)
            _accumulate(db_ref.at[h], _colsum(dpre), i == 0)
            _accumulate(dlam_ref.at[h], jnp.concatenate(dlam, axis=1), i == 0)

    both = pl.BlockSpec((2, rows.tm, width), lambda i: (0, i, 0))
    return rows.call(
        body, name, [rows.row(width), rows.full(wcat.shape), rows.full(bcat.shape), rows.full(lam.shape), both, both],
        [rows.row(width), rows.full(wcat.shape), rows.full(bcat.shape), rows.full(lam.shape)],
        [jax.ShapeDtypeStruct((rows.t_all, width), F32), jax.ShapeDtypeStruct(wcat.shape, F32),
         jax.ShapeDtypeStruct(bcat.shape, F32), jax.ShapeDtypeStruct(lam.shape, F32)])(xr, wcat, bcat, lam, da, dbb)


def _tile_scan(a, b, reverse):
    n = a.shape[0]
    row = lax.broadcasted_iota(jnp.int32, a.shape, 0)
    k = 1
    while k < n:
        ok = (row < n - k) if reverse else (row >= k)
        shift = n - k if reverse else k
        b = b + a * jnp.where(ok, pltpu.roll(b, shift, 0), 0.0)
        a = a * jnp.where(ok, pltpu.roll(a, shift, 0), 1.0)
        k *= 2
    return a, b


def _chain_scan(a, b, h_in, reverse):
    n_blocks = a.shape[0] // SUBLANES
    out = [None] * n_blocks
    state = h_in
    for j in (range(n_blocks - 1, -1, -1) if reverse else range(n_blocks)):
        rows_j = slice(j * SUBLANES, (j + 1) * SUBLANES)
        cum, h0 = _tile_scan(a[rows_j], b[rows_j], reverse)
        out[j] = h0 + cum * state
        state = out[j][0:1] if reverse else out[j][SUBLANES - 1:SUBLANES]
    return jnp.concatenate(out, axis=0), state


def _neighbour(v, edge, reverse):
    n = v.shape[0]
    row = lax.broadcasted_iota(jnp.int32, v.shape, 0)
    if reverse:
        return jnp.where(row < n - 1, pltpu.roll(v, n - 1, 0), edge)
    return jnp.where(row >= 1, pltpu.roll(v, 1, 0), edge)


def _scan_call(rows, body, name, ins, in_specs, n_out, width, adjoint):
    n_all, n_lat = rows.n_all, rows.n_lat

    def tile(d, s):
        s = n_all - 1 - s if adjoint else s
        return jnp.where(d == 0, (s + n_lat) % n_all, n_all - 1 - s)

    def per_dir(d, s):
        return (d, tile(d, s), 0)

    specs = [pl.BlockSpec((None, rows.tm, width), per_dir) if kind == "dir" else
             pl.BlockSpec((rows.tm, width), lambda d, s: (tile(d, s), 0)) for kind in in_specs]
    shape = jax.ShapeDtypeStruct((2, rows.t_all, width), BF16)
    return pl.pallas_call(
        body, name=name, grid=(2, n_all), in_specs=specs, out_specs=[pl.BlockSpec((None, rows.tm, width), per_dir)] * n_out,
        out_shape=[shape] * n_out, scratch_shapes=[pltpu.VMEM((SUBLANES, width), F32)],
        compiler_params=_params("arbitrary", "arbitrary"))(*ins)


def lru_scan(rows, a, bb, name):
    width = a.shape[2]
    n = rows.tm

    def body(a_ref, bb_ref, h_ref, hp_ref, carry):
        d, s = pl.program_id(0), pl.program_id(1)

        @pl.when(s == 0)
        def _():
            carry[...] = jnp.zeros_like(carry)

        def run(reverse):
            for cs in _chunks(width, LANES):
                h_in = carry[0:1, cs]
                h, carry[0:1, cs] = _chain_scan(a_ref[:, cs], bb_ref[:, cs].astype(F32), h_in, reverse)
                h_ref[:, cs] = h.astype(BF16)
                hp_ref[:, cs] = _neighbour(h, h_in, reverse).astype(BF16)

        pl.when(d == 0)(lambda: run(False))
        pl.when(d == 1)(lambda: run(True))

    return _scan_call(rows, body, name, [a, bb], ["dir", "dir"], 2, width, adjoint=False)


def lru_scan_bwd(rows, dh, a, hp, name):
    width = a.shape[2]
    n = rows.tm

    def body(dh_ref, a_ref, hp_ref, da_ref, dbb_ref, carry):
        d, s = pl.program_id(0), pl.program_id(1)

        @pl.when(s == 0)
        def _():
            carry[...] = jnp.zeros_like(carry)

        def run(reverse):
            for cs in _chunks(width, LANES):
                av = a_ref[:, cs]
                g, _ = _chain_scan(_neighbour(av, 1.0, reverse), dh_ref[:, cs].astype(F32), carry[0:1, cs], reverse)
                da_ref[:, cs] = (g * hp_ref[:, cs].astype(F32)).astype(BF16)
                dbb_ref[:, cs] = g.astype(BF16)
                carry[0:1, cs] = (av * g)[0:1] if reverse else (av * g)[n - 1:n]

        pl.when(d == 0)(lambda: run(True))
        pl.when(d == 1)(lambda: run(False))

    return _scan_call(rows, body, name, [dh, a, hp], ["shared", "dir", "dir"], 2, width, adjoint=True)


def mod_matmul(c9, w_shard, b_shard, name):
    n = w_shard.shape[1]
    tn = _pick(n, 768)

    def body(c_ref, w_ref, b_ref, o_ref):
        act = _silu_and_grad(c_ref[...])[0]
        o_ref[...] = jnp.dot(act, w_ref[...], preferred_element_type=F32, precision=lax.Precision.HIGHEST) + b_ref[...]

    return pl.pallas_call(
        body, name=name, grid=(n // tn,),
        in_specs=[pl.BlockSpec(c9.shape, lambda j: (0, 0)), pl.BlockSpec((w_shard.shape[0], tn), lambda j: (0, j)),
                  pl.BlockSpec((1, tn), lambda j: (0, j))],
        out_specs=pl.BlockSpec((c9.shape[0], tn), lambda j: (0, j)), out_shape=jax.ShapeDtypeStruct((c9.shape[0], n), F32),
        compiler_params=_params("arbitrary"))(c9, w_shard, b_shard)


def mod_matmul_bwd(c9, d9, w_shard, name):
    n = w_shard.shape[1]
    tn = _pick(n, 768)
    steps = n // tn

    def body(c_ref, d_ref, w_ref, gw_ref, gc_ref):
        j = pl.program_id(0)
        act, dact = _silu_and_grad(c_ref[...])
        dv = d_ref[...]
        gw_ref[...] = lax.dot_general(act, dv, (((0,), (0,)), ((), ())), preferred_element_type=F32, precision=lax.Precision.HIGHEST)
        part = lax.dot_general(dv, w_ref[...], (((1,), (1,)), ((), ())), preferred_element_type=F32, precision=lax.Precision.HIGHEST)
        _accumulate(gc_ref, part * dact, j == 0)

    return pl.pallas_call(
        body, name=name, grid=(steps,),
        in_specs=[pl.BlockSpec(c9.shape, lambda j: (0, 0)), pl.BlockSpec((c9.shape[0], tn), lambda j: (0, j)),
                  pl.BlockSpec((w_shard.shape[0], tn), lambda j: (0, j))],
        out_specs=[pl.BlockSpec((w_shard.shape[0], tn), lambda j: (0, j)), pl.BlockSpec(c9.shape, lambda j: (0, 0))],
        out_shape=[jax.ShapeDtypeStruct(w_shard.shape, F32), jax.ShapeDtypeStruct(c9.shape, F32)],
        compiler_params=_params("arbitrary"))(c9, d9, w_shard)


def _row_tile(n_rows, n_cols):
    return _pick(n_rows, max(2 * SUBLANES, (256 * 1024) // n_cols), 2 * SUBLANES)


def sum_parts(parts, name):
    n, n_rows, n_cols = parts.shape
    tr = _row_tile(n_rows, n_cols)

    def body(p_ref, o_ref):
        total = p_ref[0].astype(F32)
        for k in range(1, n):
            total = total + p_ref[k].astype(F32)
        o_ref[...] = total

    return pl.pallas_call(
        body, name=name, grid=(n_rows // tr,), in_specs=[pl.BlockSpec((n, tr, n_cols), lambda i: (0, i, 0))],
        out_specs=pl.BlockSpec((tr, n_cols), lambda i: (i, 0)), out_shape=jax.ShapeDtypeStruct((n_rows, n_cols), F32),
        compiler_params=_params("arbitrary"))(parts)


def _adamw_update(g, w_ref, m_ref, v_ref, g_ref, d_ref, m2_ref, v2_ref):
    m2 = ADAM_B1 * m_ref[...] + (1.0 - ADAM_B1) * g
    v2 = ADAM_B2 * v_ref[...] + (1.0 - ADAM_B2) * (g * g)
    m_hat = m2 / (1.0 - ADAM_B1 ** ADAM_STEP)
    v_hat = v2 / (1.0 - ADAM_B2 ** ADAM_STEP)
    g_ref[...] = g
    d_ref[...] = -ADAM_LR * (m_hat / (jnp.sqrt(v_hat) + ADAM_EPS) + ADAM_WD * w_ref[...])
    m2_ref[...] = m2
    v2_ref[...] = v2


def adamw(parts, w, m, v, name):
    n, n_rows, n_cols = parts.shape
    tr = _row_tile(n_rows, n_cols)

    def body(p_ref, *refs):
        g = p_ref[0].astype(F32)
        for k in range(1, n):
            g = g + p_ref[k].astype(F32)
        _adamw_update(g, *refs)

    blk = pl.BlockSpec((tr, n_cols), lambda i: (i, 0))
    shape = jax.ShapeDtypeStruct((n_rows, n_cols), F32)
    return pl.pallas_call(
        body, name=name, grid=(n_rows // tr,), in_specs=[pl.BlockSpec((n, tr, n_cols), lambda i: (0, i, 0)), blk, blk, blk],
        out_specs=[blk] * 4, out_shape=[shape] * 4, compiler_params=_params("arbitrary"))(parts, w, m, v)


def adamw_pair(parts, w, m, v, name):
    n_parts, n_rows, n_cols = parts.shape
    tr = _row_tile(n_rows, n_cols)
    steps = n_rows // tr

    def body(p_ref, w_ref, m_ref, v_ref, g_ref, d_ref, m2_ref, v2_ref, outbox, inbox, send_sems, recv_sems, credits):
        i = pl.program_id(0)
        sibling = (lax.axis_index("x"), lax.axis_index("y"), 1 - lax.axis_index("c"))

        def copy(slot):
            return pltpu.make_async_remote_copy(src_ref=outbox.at[slot], dst_ref=inbox.at[slot], send_sem=send_sems.at[slot],
                                                recv_sem=recv_sems.at[slot], device_id=sibling, device_id_type=MESH)

        @pl.when(i < steps)
        def _():
            slot = i % 2
            total = p_ref[0].astype(F32)
            for k in range(1, n_parts):
                total = total + p_ref[k].astype(F32)
            outbox[slot] = total

            @pl.when(i >= 2)
            def _():
                pl.semaphore_wait(credits.at[slot], 1)

            copy(slot).start()

        @pl.when(i >= 1)
        def _():
            slot = (i - 1) % 2
            landed = copy(slot)
            landed.wait_recv()
            landed.wait_send()
            _adamw_update(outbox[slot] + inbox[slot], w_ref, m_ref, v_ref, g_ref, d_ref, m2_ref, v2_ref)

            @pl.when(i + 1 < steps)
            def _():
                pl.semaphore_signal(credits.at[slot], inc=1, device_id=sibling, device_id_type=MESH)

    blk = pl.BlockSpec((tr, n_cols), lambda i: (jnp.maximum(i - 1, 0), 0))
    shape = jax.ShapeDtypeStruct((n_rows, n_cols), F32)
    return pl.pallas_call(
        body, name=name, grid=(steps + 1,),
        in_specs=[pl.BlockSpec((n_parts, tr, n_cols), lambda i: (0, jnp.minimum(i, steps - 1), 0)), blk, blk, blk],
        out_specs=[blk] * 4, out_shape=[shape] * 4,
        scratch_shapes=[pltpu.VMEM((2, tr, n_cols), F32), pltpu.VMEM((2, tr, n_cols), F32), pltpu.SemaphoreType.DMA((2,)),
                        pltpu.SemaphoreType.DMA((2,)), pltpu.SemaphoreType.REGULAR((2,))],
        compiler_params=_params("arbitrary"))(parts, w, m, v)


def _pack(arrays, row_multiple=SUBLANES):
    flat = [jnp.pad(a.reshape(-1), (0, -a.size % LANES)) for a in arrays]
    buf = jnp.concatenate(flat)
    buf = jnp.pad(buf, (0, -buf.size % (row_multiple * LANES)))
    return buf.reshape(-1, LANES)


def _unpack(buf, shapes):
    lead = buf.shape[:-2]
    flat = buf.reshape(lead + (-1,))
    out, pos = [], 0
    for shape in shapes:
        size = 1
        for s in shape:
            size *= s
        out.append(flat[..., pos:pos + size].reshape(lead + tuple(shape)))
        pos += size + (-size % LANES)
    return out


def _pos_tables(seq_len, dim, tile_rows):
    grid_rows = seq_len // GRID_W
    per_tile = tile_rows // GRID_W
    q = dim // 4
    omega = 1.0 / (10000.0 ** (jnp.arange(q, dtype=F32) / q))
    er = jnp.arange(grid_rows).astype(F32)[:, None] * omega
    ec = jnp.arange(GRID_W).astype(F32)[:, None] * omega
    by_row = jnp.concatenate([jnp.sin(er), jnp.cos(er)], axis=-1).reshape(grid_rows // per_tile, per_tile, 2 * q)
    by_row = jnp.pad(by_row, ((0, 0), (0, SUBLANES - per_tile), (0, 0))).reshape(-1, 2 * q)
    return by_row, jnp.concatenate([jnp.sin(ec), jnp.cos(ec)], axis=-1)


def _cols_from_shards(g):
    return jnp.transpose(g, (1, 0, 2)).reshape(g.shape[1], -1)


def _shards_from_cols(w, n_shards=4):
    k, n = w.shape
    return jnp.transpose(w.reshape(k, n_shards, n // n_shards), (1, 0, 2))


def kernel(x, c, ctx, c_ctx, w_mod, b_mod, g_n1, w_ffn1_up, w_ffn1_down, g_n2, w_in, b_in, w_dw, b_dw, g_ln, b_ln, w_conf_out, w_lru_conv, b_lru_conv, w_rec_gate, b_rec_gate, w_in_gate, b_in_gate, lru_lambda, w_lru_out, w_out, g_n3, w_ffn2_up, w_ffn2_down, g_final, loss_target, m_c_ctx, m_w_mod, m_b_mod, m_g_n1, m_w_ffn1_up, m_w_ffn1_down, m_g_n2, m_w_in, m_b_in, m_w_dw, m_b_dw, m_g_ln, m_b_ln, m_w_conf_out, m_w_lru_conv, m_b_lru_conv, m_w_rec_gate, m_b_rec_gate, m_w_in_gate, m_b_in_gate, m_lru_lambda, m_w_lru_out, m_w_out, m_g_n3, m_w_ffn2_up, m_w_ffn2_down, m_g_final, v_c_ctx, v_w_mod, v_b_mod, v_g_n1, v_w_ffn1_up, v_w_ffn1_down, v_g_n2, v_w_in, v_b_in, v_w_dw, v_b_dw, v_g_ln, v_b_ln, v_w_conf_out, v_w_lru_conv, v_b_lru_conv, v_w_rec_gate, v_b_rec_gate, v_w_in_gate, v_b_in_gate, v_lru_lambda, v_w_lru_out, v_w_out, v_g_n3, v_w_ffn2_up, v_w_ffn2_down, v_g_final):
    given = dict(locals())
    wt = {n: given[n] for n in W_NAMES}
    mom = {n: given["m_" + n] for n in W_NAMES}
    var = {n: given["v_" + n] for n in W_NAMES}

    def flat2(a):
        if a.ndim == 1:
            return a.reshape(1, -1)
        a = a[0]
        return a if a.ndim == 2 else a.reshape(-1, a.shape[-1])

    t_lat, d = x.shape[1], x.shape[2]
    t_ctx = ctx.shape[1]
    rows = Rows(t_lat, t_ctx, d)
    xi, yi, ci = lax.axis_index("x"), lax.axis_index("y"), lax.axis_index("c")
    me, chip = 4 * xi + 2 * yi + ci, 2 * xi + yi
    lru_w = b_lru_conv.shape[-1]
    n_blk = lru_w // LANES
    taps = w_dw.shape[1]
    lru_taps = w_lru_conv.shape[1]
    ffn = w_ffn1_down.shape[1] * 4

    small_shapes = [(1, d)] + [flat2(wt[n]).shape for n in SMALL_SHARDED]
    small_all = exchange("all", [_pack([c] + [flat2(wt[n]) for n in SMALL_SHARDED])], False, "gather_small")[0]
    small_all = _unpack(small_all, small_shapes)
    c_all = small_all[0][:, 0, :]
    full = {n: jnp.concatenate([a[0], a[2], a[4], a[6]], axis=-1) for n, a in zip(SMALL_SHARDED, small_all[1:])}
    big = COL_SHARDED + ROW_SHARDED
    ffn1_names = ['w_ffn1_up', 'w_ffn1_down']
    ffn2_names = ['w_ffn2_up', 'w_ffn2_down']
    mixer_names = [n for n in big if n not in ffn1_names + ffn2_names + ['w_in']]
    wb = {}

    def take_weights(names, gathered):
        for n, g in zip(names, gathered):
            wb[n] = _cols_from_shards(g) if n in COL_SHARDED else g.reshape(-1, g.shape[-1])

    def shards_after(names, earlier):
        return lax.optimization_barrier(([wt[n][0].astype(BF16) for n in names], earlier))[0]

    c9 = jnp.concatenate([c_all, c_ctx.reshape(1, d), jnp.zeros((7, d), F32)], axis=0)
    mod_cols = mod_matmul(c9, w_mod[0], lax.dynamic_slice_in_dim(b_mod, chip * w_mod.shape[2], w_mod.shape[2], axis=1), "mod_matmul")
    mod_all = exchange("all", [mod_cols], False, "gather_mod")[0]
    mod9 = jnp.concatenate([mod_all[0], mod_all[2], mod_all[4], mod_all[6]], axis=-1)
    mv = jnp.stack([lax.dynamic_index_in_dim(mod9, me, 0, keepdims=False).reshape(N_MOD, d), mod9[8].reshape(N_MOD, d)])

    gathered_up1 = exchange_by_sequencer(shards_after(ffn1_names[:1], mod_all), False, "gather_weights_ffn1_up", 7)
    gathered_down1 = exchange_by_sequencer(shards_after(ffn1_names[1:], mod_all), False, "gather_weights_ffn1_down", 8)
    gathered_in = exchange_by_sequencer(shards_after(['w_in'], mod_all), False, "gather_weights_in", 1)
    gathered_mixer = exchange_by_sequencer(shards_after(mixer_names, mod_all), False, "gather_weights_mixer", 9)
    gathered_ffn2 = exchange_by_sequencer(shards_after(ffn2_names, mod_all), False, "gather_weights_ffn2", 5)

    row_tab, col_tab = _pos_tables(t_lat, d, rows.tm)

    def ffn_fwd(tag, h, names, after_up=None):
        g, u, act = matmul_swiglu(h, wb[names[0]], f"{tag}_up")
        if after_up is not None:
            act = after_up(act)
        return (g, u), act, matmul(act, wb[names[1]], "nn", F32, f"{tag}_down")

    def take_down1(act):
        landed, act = lax.optimization_barrier((gathered_down1, act))
        take_weights(ffn1_names[1:], landed)
        return act

    x0, h1 = embed_norm_mod(rows, x[0], ctx[0], row_tab, col_tab, g_n1, mv, 0, 1, "norm1")
    gathered_up1, h1 = lax.optimization_barrier((gathered_up1, h1))
    take_weights(ffn1_names[:1], gathered_up1)
    gu1, act1, f1 = ffn_fwd("ffn1", h1, ffn1_names, after_up=take_down1)
    x1, h2 = norm_mod(rows, x0, g_n2, mv, 3, 4, "norm2", resid=(f1, 2, 0.5))
    gathered_in, h2 = lax.optimization_barrier((gathered_in, h2))
    take_weights(['w_in'], gathered_in)

    conf_w = w_conf_out.shape[1] * 4
    col_groups = [(0, 2 * conf_w), (2 * conf_w, lru_w), (2 * conf_w + lru_w, lru_w), (2 * conf_w + 2 * lru_w, 2 * d)]
    w_in_groups = [wb["w_in"][:, s:s + n] for s, n in col_groups]
    pvg, ux, gl, pgc = [matmul(h2, w_g, "nn", BF16, f"in_proj{k}", bias=b_in[:, s:s + n])
                        for k, (w_g, (s, n)) in enumerate(zip(w_in_groups, col_groups))]
    u = glu(rows, pvg, "glu")
    u2 = dwconv(rows, u, full["w_dw"], b_dw, taps // 2, F32, "conf_conv")
    s_act = ln_silu(rows, u2, g_ln, b_ln, "conf_ln")
    gathered_mixer, s_act = lax.optimization_barrier((gathered_mixer, s_act))
    take_weights(mixer_names, gathered_mixer)
    yc = matmul(s_act, wb["w_conf_out"], "nn", BF16, "conf_out")

    xr = dwconv(rows, ux, full["w_lru_conv"], b_lru_conv, LRU_PAD_LEFT, F32, "lru_conv")
    w_rec, w_ing = w_rec_gate[0].astype(BF16), w_in_gate[0].astype(BF16)
    wcat = jnp.concatenate([w_rec[0], w_ing[0], w_rec[1], w_ing[1]], axis=-1)

    def per_block(a):
        return jnp.transpose(a.reshape(2, n_blk, LANES), (1, 0, 2)).reshape(n_blk, 1, 2 * LANES)

    b_rec, b_ing = full["b_rec_gate"].reshape(2, n_blk, LANES), full["b_in_gate"].reshape(2, n_blk, LANES)
    bcat = jnp.concatenate([b_rec[0], b_ing[0], b_rec[1], b_ing[1]], axis=-1).reshape(n_blk, 1, 4 * LANES)
    lam = per_block(full["lru_lambda"])
    a_gate, b_gate = lru_gates(rows, xr, wcat, bcat, lam, "lru_gates")
    h_scan, h_prev = lru_scan(rows, a_gate, b_gate, "lru_scan")
    yl_in = lru_merge(rows, h_scan, gl, "lru_merge")
    yl = matmul(yl_in, wb["w_lru_out"], "nn", BF16, "lru_out")
    z = z_merge(rows, pgc, yc, yl, "z_merge")
    y = matmul(z, wb["w_out"], "nn", F32, "mix_out")

    gathered_ffn2, y = lax.optimization_barrier((gathered_ffn2, y))
    take_weights(ffn2_names, gathered_ffn2)
    x2, h3 = norm_mod(rows, x1, g_n3, mv, 6, 7, "norm3", resid=(y, 5, 1.0))
    gu2, act2, f2 = ffn_fwd("ffn2", h3, ffn2_names)
    loss_part, dx3, dg_final, df2, dgate2 = loss_head(rows, x2, f2, mv, g_final.reshape(1, d), loss_target[0], "loss_head")

    grads = {"g_final": dg_final}
    dmv = [None] * N_MOD

    def ffn_bwd(tag, dxn, df, x_prev, h, gu, act, w_up, w_down, g, ks, collective_id, settle_first=(), **norm_bwd_options):
        k_shift, k_scale = ks
        grads[f"w_{tag}_down"] = matmul(act, df, "tn", BF16, f"{tag}_down_dw")
        if settle_first:
            df = settle(settle_first, df)
        dg_, du_ = matmul_swiglu_bwd(df, w_down, gu[0], gu[1], f"{tag}_down_dx")
        grads[f"w_{tag}_up"] = jnp.concatenate([matmul(h, dg_, "tn", BF16, f"{tag}_up_dw_g"), matmul(h, du_, "tn", BF16, f"{tag}_up_dw_u")], axis=1)
        dg_, du_ = scatter_behind([f"w_{tag}_up", f"w_{tag}_down"], tag, collective_id, (dg_, du_))
        dh = matmul_pair_nt(dg_, du_, w_up, f"{tag}_up_dx")
        dx, dg, dmv[k_shift], dmv[k_scale], *more = norm_mod_bwd(rows, x_prev, dh, dxn, g, mv, k_shift, k_scale, f"{tag}_norm_bwd",
                                                                 **norm_bwd_options)
        return (dx, dg, *more)

    def grad_pieces(names):
        return [(_shards_from_cols(grads[n]) if n in COL_SHARDED else grads[n].reshape(4, -1, grads[n].shape[-1])).astype(BF16)
                for n in names]

    from_chips = {}

    def scatter_behind(names, tag, collective_id, carry):
        ready, carry = lax.optimization_barrier(([grads[n] for n in names], carry))
        grads.update(zip(names, ready))
        from_chips.update(zip(names, exchange_by_sequencer(grad_pieces(names), True, f"scatter_grads_{tag}", collective_id)))
        return carry

    def settle(names, carry):
        landed, carry = lax.optimization_barrier(([from_chips[n] for n in names], carry))
        from_chips.update(zip(names, landed))
        return carry

    dmv[8] = dgate2
    dx2, grads["g_n3"], dy, dmv[5] = ffn_bwd("ffn2", dx3, df2, x2, h3, gu2, act2, wb["w_ffn2_up"], wb["w_ffn2_down"], g_n3, (6, 7), 2,
                                             then_gate=(y, 5, 1.0))

    grads["w_out"] = matmul(z, dy, "tn", BF16, "mix_out_dw")
    dz = matmul(dy, wb["w_out"], "nt", BF16, "mix_out_dx")
    dyc, dyl, dpgc = z_merge_bwd(rows, pgc, yc, yl, dz, "z_merge_bwd")

    grads["w_conf_out"] = matmul(s_act, dyc, "tn", BF16, "conf_out_dw")
    ds_act = matmul(dyc, wb["w_conf_out"], "nt", BF16, "conf_out_dx")
    du2, grads["g_ln"], grads["b_ln"] = ln_silu_bwd(rows, u2, ds_act, g_ln, b_ln, "conf_ln_bwd")
    grads["w_dw"], grads["b_dw"] = dwconv_bwd_w(rows, u, du2, taps, taps // 2, "conf_conv_dw")
    du = dwconv(rows, du2, full["w_dw"][::-1], None, taps - 1 - taps // 2, F32, "conf_conv_dx")
    dpvg = glu_bwd(rows, pvg, du, "glu_bwd")

    grads["w_lru_out"] = matmul(yl_in, dyl, "tn", BF16, "lru_out_dw")
    dyl = scatter_behind(['w_out', 'w_conf_out', 'w_lru_out'], "mixer", 3, settle(ffn2_names, dyl))
    dyl_in = matmul(dyl, wb["w_lru_out"], "nt", BF16, "lru_out_dx")
    dh_sum, dgl = lru_merge_bwd(rows, h_scan, gl, dyl_in, "lru_merge_bwd")
    da_gate, db_gate = lru_scan_bwd(rows, dh_sum, a_gate, h_prev, "lru_scan_bwd")
    dxr, dwcat, dbcat, dlam = lru_gates_bwd(rows, xr, wcat, bcat, lam, da_gate, db_gate, "lru_gates_bwd")
    grads["w_rec_gate"] = jnp.stack([dwcat[:, :, 0:LANES], dwcat[:, :, 2 * LANES:3 * LANES]])
    grads["w_in_gate"] = jnp.stack([dwcat[:, :, LANES:2 * LANES], dwcat[:, :, 3 * LANES:4 * LANES]])
    dbcat = dbcat.reshape(n_blk, 4, LANES)
    grads["b_rec_gate"] = jnp.stack([dbcat[:, 0], dbcat[:, 2]]).reshape(2, lru_w)
    grads["b_in_gate"] = jnp.stack([dbcat[:, 1], dbcat[:, 3]]).reshape(2, lru_w)
    grads["lru_lambda"] = jnp.transpose(dlam.reshape(n_blk, 2, LANES), (1, 0, 2)).reshape(2, lru_w)
    grads["w_lru_conv"], grads["b_lru_conv"] = dwconv_bwd_w(rows, ux, dxr, lru_taps, LRU_PAD_LEFT, "lru_conv_dw")
    dux = dwconv(rows, dxr, full["w_lru_conv"][::-1], None, lru_taps - 1 - LRU_PAD_LEFT, BF16, "lru_conv_dx")

    dproj = [dpvg, dux, dgl, dpgc]
    grads["w_in"] = jnp.concatenate([matmul(h2, dp, "tn", BF16, f"in_proj{k}_dw") for k, dp in enumerate(dproj)], axis=1)
    dproj = scatter_behind(['w_in'], "in_proj", 4, settle(['w_out', 'w_conf_out', 'w_lru_out'], dproj))
    dh2, db_in = matmul_groups_nt(dproj, w_in_groups, "in_proj_dx")
    grads["b_in"] = jnp.concatenate(db_in, axis=1)
    dx1, grads["g_n2"], dmv[3], dmv[4], df1, dmv[2] = norm_mod_bwd(rows, x1, dh2, dx2, g_n2, mv, 3, 4, "norm2_bwd", then_gate=(f1, 2, 0.5))

    dx0, grads["g_n1"] = ffn_bwd("ffn1", dx1, df1, x0, h1, gu1, act1, wb["w_ffn1_up"], wb["w_ffn1_down"], g_n1, (0, 1), 6,
                                 settle_first=['w_in'], latent_only=True)
    grad_x = dx0[None]

    out = {}
    for n in big:
        parts = from_chips[n].reshape(4, -1, from_chips[n].shape[-1])
        out[n] = adamw_pair(parts, flat2(wt[n]), flat2(mom[n]), flat2(var[n]), f"adamw_{n}")

    dmod = jnp.concatenate(dmv, axis=1)
    small_names = ['g_n1', 'g_n2', 'b_in', 'b_dw', 'g_ln', 'b_ln', 'b_lru_conv', 'g_n3', 'g_final'] + SMALL_SHARDED
    gate_names = ['w_rec_gate', 'w_in_gate']
    small_list = [loss_part] + [grads[n] for n in small_names] + [dmod[0], dmod[1]]
    small_buf = _pack(small_list, 8 * 2 * SUBLANES)
    eighths = [small_buf.reshape(8, -1, LANES)] + [grads[n].reshape(8, -1, LANES) for n in gate_names]
    *parts, dmod_all = exchange("all", eighths + [_pack([dmod[0]])], [True] * len(eighths) + [False], "scatter_small_grads")
    sums = exchange("all", [sum_parts(p, f"sum_small_grads{k}") for k, p in enumerate(parts)], False, "gather_small_grads")
    small_sum = _unpack(sums[0].reshape(small_buf.shape), [a.shape for a in small_list])
    loss = small_sum[0][0, 0]
    total = dict(zip(small_names, small_sum[1:]))
    total.update({n: s.reshape(grads[n].shape) for n, s in zip(gate_names, sums[1:])})
    dmod_all = _unpack(dmod_all, [dmod[0].shape])[0].reshape(8, N_MOD * d)
    dmc = small_sum[-1].reshape(1, N_MOD * d)
    total["b_mod"] = small_sum[-2].reshape(1, N_MOD * d) + dmc

    d9 = jnp.concatenate([dmod_all, dmc, jnp.zeros((7, N_MOD * d), F32)], axis=0)
    d9_cols = lax.dynamic_slice_in_dim(d9, chip * w_mod.shape[2], w_mod.shape[2], axis=1)
    g_wmod, dc9 = mod_matmul_bwd(c9, d9_cols, w_mod[0], "mod_matmul_bwd")
    dc_all = exchange("all", [dc9[8:16]], False, "gather_dc")[0]
    total["c_ctx"] = sum_parts(jnp.stack([dc_all[0], dc_all[2], dc_all[4], dc_all[6]]), "sum_dc")[0:1]
    out["w_mod"] = adamw(g_wmod[None], w_mod[0], m_w_mod[0], v_w_mod[0], "adamw_w_mod")

    for n in gate_names:
        out[n] = adamw(total[n].reshape(1, -1, LANES), flat2(wt[n]), flat2(mom[n]), flat2(var[n]), f"adamw_{n}")

    small_all_names = [n for n in W_NAMES if n not in out]
    g_local, shapes = [], []
    for n in small_all_names:
        g = total[n].reshape(flat2(wt[n]).shape[:-1] + (-1,)) if n in SMALL_SHARDED else total[n].reshape(flat2(wt[n]).shape)
        if n in SMALL_SHARDED:
            width = wt[n].shape[-1]
            g = lax.dynamic_slice_in_dim(g, chip * width, width, axis=1)
        g_local.append(g)
        shapes.append(g.shape)
    packed = adamw(_pack(g_local)[None], _pack([flat2(wt[n]) for n in small_all_names]),
                   _pack([flat2(mom[n]) for n in small_all_names]), _pack([flat2(var[n]) for n in small_all_names]), "adamw_small")
    for k, n in enumerate(small_all_names):
        out[n] = [_unpack(p, shapes)[k] for p in packed]

    results = [loss, grad_x]
    for k in range(4):
        results += [out[n][k].reshape(wt[n].shape) for n in W_NAMES]
    return tuple(results)
```

```python
import jax
import jax.numpy as jnp
from jax import lax
from jax.experimental import pallas as pl
from jax.experimental.pallas import tpu as pltpu
from jax.experimental.pallas import tpu_sc as plsc

F32 = jnp.float32
BF16 = jnp.bfloat16
MESH = pl.DeviceIdType.MESH

EPS = 1e-6
GRID_W = 64
N_MOD = 9
LRU_C = 8.0
LRU_PAD_LEFT = 2
ADAM_LR, ADAM_B1, ADAM_B2, ADAM_EPS, ADAM_WD, ADAM_STEP = 0.001, 0.9, 0.999, 1e-08, 0.01, 10

LANES = 128
SUBLANES = 8
HALO = 16
VMEM_LIMIT = 56 * 1024 * 1024

W_NAMES = ['c_ctx', 'w_mod', 'b_mod', 'g_n1', 'w_ffn1_up', 'w_ffn1_down', 'g_n2', 'w_in', 'b_in', 'w_dw', 'b_dw',
           'g_ln', 'b_ln', 'w_conf_out', 'w_lru_conv', 'b_lru_conv', 'w_rec_gate', 'b_rec_gate', 'w_in_gate',
           'b_in_gate', 'lru_lambda', 'w_lru_out', 'w_out', 'g_n3', 'w_ffn2_up', 'w_ffn2_down', 'g_final']
COL_SHARDED = ['w_ffn1_up', 'w_in', 'w_ffn2_up']
ROW_SHARDED = ['w_ffn1_down', 'w_conf_out', 'w_lru_out', 'w_out', 'w_ffn2_down']
SMALL_SHARDED = ['w_dw', 'w_lru_conv', 'b_rec_gate', 'b_in_gate', 'lru_lambda']


def _params(*semantics):
    return pltpu.CompilerParams(dimension_semantics=semantics, vmem_limit_bytes=VMEM_LIMIT)


def _pick(n, target, mult=LANES):
    best = None
    for t in range(mult, min(n, target) + 1, mult):
        if n % t == 0:
            best = t
    return best or n


def _chunks(width, target=512):
    w = _pick(width, target)
    return [slice(s, s + w) for s in range(0, width, w)]


def _sigmoid(x):
    return jax.nn.sigmoid(x)


def _silu_and_grad(x):
    s = _sigmoid(x)
    return x * s, s * (1.0 + x * (1.0 - s))


_GELU_K = 0.7978845608028654


def _gelu_and_grad(x):
    t = jnp.tanh(_GELU_K * (x + 0.044715 * (x * x * x)))
    return 0.5 * x * (1.0 + t), 0.5 * (1.0 + t) + 0.5 * x * (1.0 - t * t) * _GELU_K * (1.0 + 3 * 0.044715 * x * x)


def _neg_expm1(z):
    series = -(z * (1.0 + z * (0.5 + z * (1.0 / 6.0))))
    return jnp.where(z > -0.01, series, 1.0 - jnp.exp(z))


def _softplus(x):
    return jnp.maximum(x, 0.0) + jnp.log1p(jnp.exp(-jnp.abs(x)))


def _accumulate(ref, value, first):
    @pl.when(first)
    def _():
        ref[...] = value

    @pl.when(jnp.logical_not(first))
    def _():
        ref[...] += value


def _colsum(x):
    return jnp.sum(x, axis=0, keepdims=True)


class Rows:
    def __init__(self, t_lat, t_ctx, d_model):
        self.tm = _pick(t_ctx, 256, HALO)
        assert t_lat % self.tm == 0 and t_ctx % self.tm == 0
        self.n_lat = t_lat // self.tm
        self.n_all = (t_lat + t_ctx) // self.tm
        self.t_all = t_lat + t_ctx
        self.d = d_model

    def seg(self, i):
        return jnp.where(i >= self.n_lat, 1, 0)

    def seg_first(self, i):
        return jnp.logical_or(i == 0, i == self.n_lat)

    def seg_last(self, i):
        return jnp.logical_or(i == self.n_lat - 1, i == self.n_all - 1)

    def row(self, width, col=0):
        return pl.BlockSpec((self.tm, width), lambda i: (i, col))

    def vec(self, width):
        return pl.BlockSpec((1, width), lambda i: (0, 0))

    def full(self, shape):
        return pl.BlockSpec(shape, lambda i: (0,) * len(shape))

    def mod(self):
        return pl.BlockSpec((None, N_MOD, self.d), lambda i: (self.seg(i), 0, 0))

    def segsum(self):
        return pl.BlockSpec((None, 1, self.d), lambda i: (self.seg(i), 0, 0))

    def segsum_shape(self):
        return jax.ShapeDtypeStruct((2, 1, self.d), F32)

    def halo(self, width, side):
        per = self.tm // HALO
        last = self.t_all // HALO - 1
        if side < 0:
            return pl.BlockSpec((HALO, width), lambda i: (jnp.maximum(i * per - 1, 0), 0))
        return pl.BlockSpec((HALO, width), lambda i: (jnp.minimum((i + 1) * per, last), 0))

    def call(self, body, name, in_specs, out_specs, out_shape, scratch=()):
        return pl.pallas_call(body, name=name, grid=(self.n_all,), in_specs=in_specs, out_specs=out_specs,
                              out_shape=out_shape, scratch_shapes=list(scratch), compiler_params=_params("arbitrary"))


_PEER_FLIPS = {
    "chips": [(1, 0, 0), (0, 1, 0), (1, 1, 0)],
    "all": [(0, 0, 1), (0, 1, 0), (0, 1, 1), (1, 0, 0), (1, 0, 1), (1, 1, 0), (1, 1, 1)],
}


def _slot(kind, x, y, c):
    return {"chips": 2 * x + y, "all": 4 * x + 2 * y + c}[kind]


def exchange(kind, srcs, indexed, name):
    flips = _PEER_FLIPS[kind]
    n_slots = len(flips) + 1
    n_t = len(srcs)
    by_slot = list(indexed) if isinstance(indexed, (list, tuple)) else [indexed] * n_t
    out_shapes = [jax.ShapeDtypeStruct(s.shape if ix else (n_slots,) + s.shape, s.dtype) for s, ix in zip(srcs, by_slot)]

    def body(*refs):
        src_refs, dst_refs = refs[:n_t], refs[n_t:2 * n_t]
        send_sems, recv_sems, local_sems = refs[2 * n_t:]
        x, y, c = lax.axis_index("x"), lax.axis_index("y"), lax.axis_index("c")
        me = _slot(kind, x, y, c)

        def piece(t, k):
            return src_refs[t].at[k] if by_slot[t] else src_refs[t]

        peers = [(1 - x if fx else x, 1 - y if fy else y, 1 - c if fc else c) for fx, fy, fc in flips]
        local = [pltpu.make_async_copy(piece(t, me), dst_refs[t].at[me], local_sems.at[t]) for t in range(n_t)]
        for cp in local:
            cp.start()
        sends = []
        for t in range(n_t):
            for p, peer in enumerate(peers):
                sends.append(pltpu.make_async_remote_copy(
                    src_ref=piece(t, _slot(kind, *peer)), dst_ref=dst_refs[t].at[me],
                    send_sem=send_sems.at[t, p], recv_sem=recv_sems.at[t, p], device_id=peer, device_id_type=MESH))
        for cp in sends:
            cp.start()
        for t in range(n_t):
            for p, peer in enumerate(peers):
                theirs = _slot(kind, *peer)
                pltpu.make_async_remote_copy(
                    src_ref=piece(t, me), dst_ref=dst_refs[t].at[theirs],
                    send_sem=send_sems.at[t, p], recv_sem=recv_sems.at[t, p], device_id=peer, device_id_type=MESH).wait_recv()
        for cp in sends:
            cp.wait_send()
        for cp in local:
            cp.wait()

    any_spec = pl.BlockSpec(memory_space=pl.ANY)
    outs = pl.pallas_call(
        body, name=name, out_shape=out_shapes, in_specs=[any_spec] * n_t, out_specs=[any_spec] * n_t,
        scratch_shapes=[pltpu.SemaphoreType.DMA((n_t, len(flips))), pltpu.SemaphoreType.DMA((n_t, len(flips))),
                        pltpu.SemaphoreType.DMA((n_t,))],
        compiler_params=pltpu.CompilerParams(has_side_effects=True),
    )(*srcs)
    return list(outs)


def exchange_by_sequencer(srcs, indexed, name, collective_id):
    flips = _PEER_FLIPS["chips"]
    n_t, n_p = len(srcs), len(flips)
    src_refs = [jax.new_ref(s, memory_space=pltpu.MemorySpace.HBM) for s in srcs]
    land_refs = [jax.empty_ref(jax.ShapeDtypeStruct(s.shape if indexed else (n_p + 1,) + s.shape, s.dtype),
                               memory_space=pltpu.MemorySpace.HBM) for s in srcs]

    @pl.kernel(mesh=plsc.ScalarSubcoreMesh(axis_name="sequencer", num_cores=1), name=name,
               scratch_types=(pltpu.SemaphoreType.DMA((n_t, n_p)), pltpu.SemaphoreType.DMA((n_t, n_p)), pltpu.SemaphoreType.DMA((n_t,))),
               compiler_params=pltpu.CompilerParams(collective_id=collective_id))
    def launch(send_sems, recv_sems, local_sems):
        x, y, c = lax.axis_index("x"), lax.axis_index("y"), lax.axis_index("c")
        peers = [(1 - x if fx else x, 1 - y if fy else y, c) for fx, fy, _ in flips]
        barrier = pltpu.get_barrier_semaphore()
        for peer in peers:
            pl.semaphore_signal(barrier, inc=1, device_id=peer, device_id_type=MESH)
        pl.semaphore_wait(barrier, n_p)
        me = _slot("chips", x, y, c)

        def piece(t, k):
            return src_refs[t].at[k] if indexed else src_refs[t]

        local = [pltpu.make_async_copy(piece(t, me), land_refs[t].at[me], local_sems.at[t]) for t in range(n_t)]
        for cp in local:
            cp.start()
        sends = []
        for t in range(n_t):
            for p, peer in enumerate(peers):
                sends.append(pltpu.make_async_remote_copy(
                    src_ref=piece(t, _slot("chips", *peer)), dst_ref=land_refs[t].at[me],
                    send_sem=send_sems.at[t, p], recv_sem=recv_sems.at[t, p], device_id=peer, device_id_type=MESH))
        for cp in sends:
            cp.start()
        for t in range(n_t):
            for p, peer in enumerate(peers):
                pltpu.make_async_remote_copy(
                    src_ref=piece(t, me), dst_ref=land_refs[t].at[_slot("chips", *peer)],
                    send_sem=send_sems.at[t, p], recv_sem=recv_sems.at[t, p], device_id=peer, device_id_type=MESH).wait_recv()
        for cp in sends:
            cp.wait_send()
        for cp in local:
            cp.wait()

    launch()
    return [r[...] for r in land_refs]


MATMUL_VMEM_BUDGET = 44 * 1024 * 1024
MATMUL_STEP_BYTES = 1024 * 1024


def _divisors(n, most):
    return [t for t in range(LANES, min(n, most) + 1, LANES) if n % t == 0] or [n]


def _matmul_tiles(n_i, n_j, n_r, a_bytes, b_bytes, out_bytes):
    best = None
    for tr in _divisors(n_r, n_r):
        steps = n_r // tr
        for ti in _divisors(n_i, 1408):
            for tj in _divisors(n_j, 1664):
                vmem = 2 * (ti * tr * a_bytes + tr * tj * b_bytes + ti * tj * out_bytes) + (ti * tj * 4 if steps > 1 else 0)
                if vmem > MATMUL_VMEM_BUDGET or ti < min(n_i, 2 * LANES) or tj < min(n_j, 4 * LANES):
                    continue
                size_a, size_b = n_i * n_r * a_bytes, n_r * n_j * b_bytes
                if steps == 1:
                    moved = min(size_a + (n_i // ti) * size_b, size_b + (n_j // tj) * size_a)
                else:
                    moved = (n_j // tj) * size_a + (n_i // ti) * size_b + (steps - 1) * n_i * n_j * 8
                moved += n_i * n_j * out_bytes + (n_i // ti) * (n_j // tj) * steps * MATMUL_STEP_BYTES
                if best is None or moved < best[0]:
                    best = (moved, ti, tj, tr)
    return best[1:]


def matmul(a, b, mode, out_dtype, name, bias=None):
    if mode == "nn":
        (n_i, n_r), (_, n_j) = a.shape, b.shape
    elif mode == "nt":
        (n_i, n_r), (n_j, _) = a.shape, b.shape
    else:
        (n_r, n_i), (_, n_j) = a.shape, b.shape
    ti, tj, tr = _matmul_tiles(n_i, n_j, n_r, a.dtype.itemsize, b.dtype.itemsize, jnp.dtype(out_dtype).itemsize)
    steps = n_r // tr
    bytes_a, bytes_b = a.size * a.dtype.itemsize, b.size * b.dtype.itemsize
    j_outer = steps == 1 and bytes_b + (n_j // tj) * bytes_a < bytes_a + (n_i // ti) * bytes_b

    def at(fn):
        return (lambda j, i, r: fn(i, j, r)) if j_outer else fn

    a_spec = {"nn": pl.BlockSpec((ti, tr), at(lambda i, j, r: (i, r))), "nt": pl.BlockSpec((ti, tr), at(lambda i, j, r: (i, r))),
              "tn": pl.BlockSpec((tr, ti), at(lambda i, j, r: (r, i)))}[mode]
    b_spec = {"nn": pl.BlockSpec((tr, tj), at(lambda i, j, r: (r, j))), "nt": pl.BlockSpec((tj, tr), at(lambda i, j, r: (j, r))),
              "tn": pl.BlockSpec((tr, tj), at(lambda i, j, r: (r, j)))}[mode]
    dims = {"nn": (((1,), (0,)), ((), ())), "nt": (((1,), (1,)), ((), ())), "tn": (((0,), (0,)), ((), ()))}[mode]

    def body(*refs):
        a_ref, b_ref = refs[:2]
        bias_ref = refs[2] if bias is not None else None
        o_ref = refs[3] if bias is not None else refs[2]
        prod = lax.dot_general(a_ref[...].astype(BF16), b_ref[...].astype(BF16), dims, preferred_element_type=F32)

        def finish(total):
            if bias_ref is not None:
                total = total + bias_ref[...]
            o_ref[...] = total.astype(out_dtype)

        if steps == 1:
            finish(prod)
        else:
            acc = refs[-1]
            r = pl.program_id(2)

            @pl.when(r == 0)
            def _():
                acc[...] = prod

            @pl.when(r > 0)
            def _():
                acc[...] += prod

            @pl.when(r == steps - 1)
            def _():
                finish(acc[...])

    in_specs = [a_spec, b_spec] + ([pl.BlockSpec((1, tj), at(lambda i, j, r: (0, j)))] if bias is not None else [])
    args = (a, b) + ((bias,) if bias is not None else ())
    grid = (n_j // tj, n_i // ti, steps) if j_outer else (n_i // ti, n_j // tj, steps)
    return pl.pallas_call(
        body, name=name, grid=grid, in_specs=in_specs,
        out_specs=pl.BlockSpec((ti, tj), at(lambda i, j, r: (i, j))), out_shape=jax.ShapeDtypeStruct((n_i, n_j), out_dtype),
        scratch_shapes=[pltpu.VMEM((ti, tj), F32)] if steps > 1 else [],
        compiler_params=_params("arbitrary", "arbitrary", "arbitrary"),
    )(*args)


def matmul_swiglu(h, w_up, name):
    n_t, d = h.shape
    f = w_up.shape[1] // 2
    ti, tj = _pick(n_t, 768), _pick(f, 1408)
    n_j = f // tj

    def body(h_ref, wg_ref, wu_ref, g_ref, u_ref, act_ref):
        hv = h_ref[...]
        g = jnp.dot(hv, wg_ref[...], preferred_element_type=F32)
        u = jnp.dot(hv, wu_ref[...], preferred_element_type=F32)
        g_ref[...] = g.astype(BF16)
        u_ref[...] = u.astype(BF16)
        act_ref[...] = (_silu_and_grad(g)[0] * u).astype(BF16)

    out = pl.BlockSpec((ti, tj), lambda j, i: (i, j))
    shape = jax.ShapeDtypeStruct((n_t, f), BF16)
    return pl.pallas_call(
        body, name=name, grid=(n_j, n_t // ti),
        in_specs=[pl.BlockSpec((ti, d), lambda j, i: (i, 0)), pl.BlockSpec((d, tj), lambda j, i: (0, j)),
                  pl.BlockSpec((d, tj), lambda j, i: (0, j + n_j))],
        out_specs=[out, out, out], out_shape=[shape, shape, shape], compiler_params=_params("arbitrary", "arbitrary"),
    )(h, w_up, w_up)


def matmul_swiglu_bwd(df, w_down, g, u, name):
    n_t, d = df.shape
    f = w_down.shape[0]
    ti, tj = _pick(n_t, 768), _pick(f, 1408)

    def body(df_ref, w_ref, g_ref, u_ref, dg_ref, du_ref):
        dact = lax.dot_general(df_ref[...], w_ref[...], (((1,), (1,)), ((), ())), preferred_element_type=F32)
        act, dact_dg = _silu_and_grad(g_ref[...].astype(F32))
        dg_ref[...] = (dact * u_ref[...].astype(F32) * dact_dg).astype(BF16)
        du_ref[...] = (dact * act).astype(BF16)

    tile = pl.BlockSpec((ti, tj), lambda j, i: (i, j))
    shape = jax.ShapeDtypeStruct((n_t, f), BF16)
    return pl.pallas_call(
        body, name=name, grid=(f // tj, n_t // ti),
        in_specs=[pl.BlockSpec((ti, d), lambda j, i: (i, 0)), pl.BlockSpec((tj, d), lambda j, i: (j, 0)), tile, tile],
        out_specs=[tile, tile], out_shape=[shape, shape], compiler_params=_params("arbitrary", "arbitrary"))(df, w_down, g, u)


def matmul_groups_nn(a, weights, biases, name):
    n_g = len(weights)
    n_t, d = a.shape
    ti = _pick(n_t, 768)

    def body(*refs):
        av = refs[0][...]
        for w_ref, b_ref, o_ref in zip(refs[1:1 + n_g], refs[1 + n_g:1 + 2 * n_g], refs[1 + 2 * n_g:]):
            o_ref[...] = (jnp.dot(av, w_ref[...], preferred_element_type=F32) + b_ref[...]).astype(BF16)

    widths = [w.shape[1] for w in weights]
    once = pl.Buffered(1)
    return pl.pallas_call(
        body, name=name, grid=(n_t // ti,),
        in_specs=[pl.BlockSpec((ti, d), lambda i: (i, 0))] + [pl.BlockSpec((d, w), lambda i: (0, 0), pipeline_mode=once) for w in widths]
        + [pl.BlockSpec((1, w), lambda i: (0, 0), pipeline_mode=once) for w in widths],
        out_specs=[pl.BlockSpec((ti, w), lambda i: (i, 0)) for w in widths],
        out_shape=[jax.ShapeDtypeStruct((n_t, w), BF16) for w in widths], compiler_params=_params("arbitrary"))(a, *weights, *biases)


def matmul_groups_nt(parts, weights, name):
    n_g = len(parts)
    n_t, d = parts[0].shape[0], weights[0].shape[0]
    ti = _pick(n_t, 768)
    dims = (((1,), (1,)), ((), ()))

    def body(*refs):
        a_refs, b_refs, o_ref, sum_refs = refs[:n_g], refs[n_g:2 * n_g], refs[2 * n_g], refs[2 * n_g + 1:]
        i = pl.program_id(0)
        total = None
        for a_ref, b_ref, s_ref in zip(a_refs, b_refs, sum_refs):
            av = a_ref[...]
            prod = lax.dot_general(av, b_ref[...], dims, preferred_element_type=F32)
            total = prod if total is None else total + prod
            _accumulate(s_ref, _colsum(av.astype(F32)), i == 0)
        o_ref[...] = total

    widths = [p.shape[1] for p in parts]
    outs = pl.pallas_call(
        body, name=name, grid=(n_t // ti,),
        in_specs=[pl.BlockSpec((ti, w), lambda i: (i, 0)) for w in widths]
        + [pl.BlockSpec((d, w), lambda i: (0, 0), pipeline_mode=pl.Buffered(1)) for w in widths],
        out_specs=[pl.BlockSpec((ti, d), lambda i: (i, 0))] + [pl.BlockSpec((1, w), lambda i: (0, 0)) for w in widths],
        out_shape=[jax.ShapeDtypeStruct((n_t, d), F32)] + [jax.ShapeDtypeStruct((1, w), F32) for w in widths],
        compiler_params=_params("arbitrary"))(*parts, *weights)
    return outs[0], list(outs[1:])


def matmul_pair_nt(a1, a2, b, name):
    n_t, f = a1.shape
    d = b.shape[0]
    ti = _pick(n_t, 768)
    dims = (((1,), (1,)), ((), ()))

    def body(a1_ref, a2_ref, b1_ref, b2_ref, o_ref):
        o_ref[...] = (lax.dot_general(a1_ref[...], b1_ref[...], dims, preferred_element_type=F32)
                      + lax.dot_general(a2_ref[...], b2_ref[...], dims, preferred_element_type=F32))

    rows_in = pl.BlockSpec((ti, f), lambda i: (i, 0))
    return pl.pallas_call(
        body, name=name, grid=(n_t // ti,),
        in_specs=[rows_in, rows_in, pl.BlockSpec((d, f), lambda i: (0, 0)), pl.BlockSpec((d, f), lambda i: (0, 1))],
        out_specs=pl.BlockSpec((ti, d), lambda i: (i, 0)), out_shape=jax.ShapeDtypeStruct((n_t, d), F32),
        compiler_params=_params("arbitrary"))(a1, a2, b, b)


def norm_mod(rows, x, g, mv, shift_k, scale_k, name, resid=None):
    d = rows.d
    changed = resid is not None

    def body(*refs):
        refs = list(refs)
        x_ref, g_ref, mv_ref = refs[:3]
        rest = refs[3:]
        xv = x_ref[...]
        if resid is not None:
            xv = xv + resid[2] * mv_ref[resid[1]:resid[1] + 1, :] * rest.pop(0)[...]
        if changed:
            rest.pop(0)[...] = xv
        r = lax.rsqrt(jnp.mean(xv * xv, axis=-1, keepdims=True) + EPS)
        h = (xv * r) * g_ref[...]
        if shift_k is not None:
            h = h * (1.0 + mv_ref[scale_k:scale_k + 1, :]) + mv_ref[shift_k:shift_k + 1, :]
        rest.pop(0)[...] = h.astype(BF16)

    ins = [x, g, mv] + ([resid[0]] if resid is not None else [])
    in_specs = [rows.row(d), rows.vec(d), rows.mod()] + [rows.row(d)] * (len(ins) - 3)
    out_shape = ([jax.ShapeDtypeStruct((rows.t_all, d), F32)] if changed else []) + [jax.ShapeDtypeStruct((rows.t_all, d), BF16)]
    outs = rows.call(body, name, in_specs, [rows.row(d)] * len(out_shape), out_shape)(*ins)
    return (outs[0], outs[1]) if changed else (None, outs[0])


def embed_norm_mod(rows, x_lat, x_ctx, row_tab, col_tab, g, mv, shift_k, scale_k, name):
    d = rows.d
    half = d // 2
    per_tile = rows.tm // GRID_W
    assert rows.tm % GRID_W == 0 and per_tile <= SUBLANES

    def body(xl_ref, xc_ref, rt_ref, ct_ref, g_ref, mv_ref, x0_ref, h_ref):
        i = pl.program_id(0)
        by_row = jnp.concatenate([jnp.broadcast_to(rt_ref[r:r + 1, :], (GRID_W, half)) for r in range(per_tile)], axis=0)
        by_col = jnp.concatenate([ct_ref[...]] * per_tile, axis=0)
        xv = jnp.where(i < rows.n_lat, xl_ref[...] + jnp.concatenate([by_row, by_col], axis=1), xc_ref[...])
        x0_ref[...] = xv
        r = lax.rsqrt(jnp.mean(xv * xv, axis=-1, keepdims=True) + EPS)
        h = (xv * r) * g_ref[...]
        h_ref[...] = (h * (1.0 + mv_ref[scale_k:scale_k + 1, :]) + mv_ref[shift_k:shift_k + 1, :]).astype(BF16)

    last = rows.n_lat - 1
    in_specs = [pl.BlockSpec((rows.tm, d), lambda i: (jnp.minimum(i, last), 0)),
                pl.BlockSpec((rows.tm, d), lambda i: (jnp.maximum(i - rows.n_lat, 0), 0)),
                pl.BlockSpec((SUBLANES, half), lambda i: (jnp.minimum(i, last), 0)), rows.full(col_tab.shape), rows.vec(d), rows.mod()]
    return rows.call(body, name, in_specs, [rows.row(d), rows.row(d)],
                     [jax.ShapeDtypeStruct((rows.t_all, d), F32), jax.ShapeDtypeStruct((rows.t_all, d), BF16)])(
                         x_lat, x_ctx, row_tab, col_tab, g, mv)


def norm_mod_bwd(rows, x, dh, dxn, g, mv, shift_k, scale_k, name, latent_only=False, then_gate=None):
    d = rows.d
    n_dx = rows.n_lat if latent_only else rows.n_all

    def body(*refs):
        x_ref, dh_ref, dxn_ref, g_ref, mv_ref = refs[:5]
        dx_ref, dg_ref, dsh_ref, dsc_ref = refs[-4:] if then_gate is None else refs[6:10]
        i = pl.program_id(0)
        xv, dhv, gv = x_ref[...], dh_ref[...], g_ref[...]
        r = lax.rsqrt(jnp.mean(xv * xv, axis=-1, keepdims=True) + EPS)
        n = xv * r
        dy = dhv * (1.0 + mv_ref[scale_k:scale_k + 1, :])
        dn = dy * gv
        dx = dxn_ref[...] + r * (dn - n * jnp.mean(dn * n, axis=-1, keepdims=True))

        @pl.when(i < n_dx)
        def _():
            dx_ref[...] = dx

        _accumulate(dg_ref, _colsum(dy * n), i == 0)
        _accumulate(dsh_ref, _colsum(dhv), rows.seg_first(i))
        _accumulate(dsc_ref, _colsum(dhv * (n * gv)), rows.seg_first(i))
        if then_gate is not None:
            _gate_bwd(rows, i, dx, refs[5], mv_ref, then_gate[1], then_gate[2], refs[10], refs[11])

    dx_spec = pl.BlockSpec((rows.tm, d), lambda i: (jnp.minimum(i, n_dx - 1), 0))
    ins = [x, dh, dxn, g, mv] + ([then_gate[0]] if then_gate is not None else [])
    in_specs = [rows.row(d), rows.row(d), rows.row(d), rows.vec(d), rows.mod()] + ([rows.row(d)] if then_gate is not None else [])
    out_specs = [dx_spec, rows.vec(d), rows.segsum(), rows.segsum()]
    out_shape = [jax.ShapeDtypeStruct((n_dx * rows.tm, d), F32), jax.ShapeDtypeStruct((1, d), F32), rows.segsum_shape(), rows.segsum_shape()]
    if then_gate is not None:
        out_specs += [rows.row(d), rows.segsum()]
        out_shape += [jax.ShapeDtypeStruct((rows.t_all, d), BF16), rows.segsum_shape()]
    return rows.call(body, name, in_specs, out_specs, out_shape)(*ins)


def _gate_bwd(rows, i, dx, f_ref, mv_ref, gate_k, scale, df_ref, dgate_ref):
    df_ref[...] = (scale * mv_ref[gate_k:gate_k + 1, :] * dx).astype(BF16)
    _accumulate(dgate_ref, scale * _colsum(dx * f_ref[...]), rows.seg_first(i))


def glu(rows, pvg, name):
    w = pvg.shape[1] // 2

    def body(v_ref, t_ref, o_ref):
        for cs in _chunks(w):
            o_ref[:, cs] = v_ref[:, cs].astype(F32) * _sigmoid(t_ref[:, cs].astype(F32))

    return rows.call(body, name, [rows.row(w, 0), rows.row(w, 1)], rows.row(w), jax.ShapeDtypeStruct((rows.t_all, w), F32))(pvg, pvg)


def glu_bwd(rows, pvg, du, name):
    w = pvg.shape[1] // 2

    def body(v_ref, t_ref, du_ref, o_ref):
        for cs in _chunks(w):
            s = _sigmoid(t_ref[:, cs].astype(F32))
            duv = du_ref[:, cs]
            o_ref[:, cs] = (duv * s).astype(BF16)
            o_ref[:, slice(w + cs.start, w + cs.stop)] = (duv * v_ref[:, cs].astype(F32) * s * (1.0 - s)).astype(BF16)

    return rows.call(body, name, [rows.row(w, 0), rows.row(w, 1), rows.row(w)], rows.row(2 * w),
                     jax.ShapeDtypeStruct((rows.t_all, 2 * w), BF16))(pvg, pvg, du)


def ln_silu(rows, u, g, b, name):
    w = u.shape[1]

    def body(u_ref, g_ref, b_ref, o_ref):
        uv = u_ref[...]
        xc = uv - jnp.mean(uv, axis=-1, keepdims=True)
        n = xc * lax.rsqrt(jnp.mean(xc * xc, axis=-1, keepdims=True) + EPS)
        o_ref[...] = _silu_and_grad(n * g_ref[...] + b_ref[...])[0].astype(BF16)

    return rows.call(body, name, [rows.row(w), rows.vec(w), rows.vec(w)], rows.row(w),
                     jax.ShapeDtypeStruct((rows.t_all, w), BF16))(u, g, b)


def ln_silu_bwd(rows, u, ds, g, b, name):
    w = u.shape[1]

    def body(u_ref, ds_ref, g_ref, b_ref, du_ref, dg_ref, db_ref):
        i = pl.program_id(0)
        uv, gv = u_ref[...], g_ref[...]
        xc = uv - jnp.mean(uv, axis=-1, keepdims=True)
        r = lax.rsqrt(jnp.mean(xc * xc, axis=-1, keepdims=True) + EPS)
        n = xc * r
        dy = ds_ref[...].astype(F32) * _silu_and_grad(n * gv + b_ref[...])[1]
        dn = dy * gv
        du_ref[...] = r * (dn - jnp.mean(dn, axis=-1, keepdims=True) - n * jnp.mean(dn * n, axis=-1, keepdims=True))
        _accumulate(dg_ref, _colsum(dy * n), i == 0)
        _accumulate(db_ref, _colsum(dy), i == 0)

    return rows.call(body, name, [rows.row(w), rows.row(w), rows.vec(w), rows.vec(w)], [rows.row(w), rows.vec(w), rows.vec(w)],
                     [jax.ShapeDtypeStruct((rows.t_all, w), F32), jax.ShapeDtypeStruct((1, w), F32),
                      jax.ShapeDtypeStruct((1, w), F32)])(u, ds, g, b)


def lru_merge(rows, h, gl, name):
    w = gl.shape[1]

    def body(h_ref, gl_ref, o_ref):
        for cs in _chunks(w):
            h_sum = h_ref[0, :, cs].astype(F32) + h_ref[1, :, cs].astype(F32)
            o_ref[:, cs] = (h_sum * _gelu_and_grad(gl_ref[:, cs].astype(F32))[0]).astype(BF16)

    return rows.call(body, name, [pl.BlockSpec((2, rows.tm, w), lambda i: (0, i, 0)), rows.row(w)], rows.row(w),
                     jax.ShapeDtypeStruct((rows.t_all, w), BF16))(h, gl)


def lru_merge_bwd(rows, h, gl, dy, name):
    w = gl.shape[1]

    def body(h_ref, gl_ref, dy_ref, dh_ref, dgl_ref):
        for cs in _chunks(w):
            act, dact = _gelu_and_grad(gl_ref[:, cs].astype(F32))
            dyv = dy_ref[:, cs].astype(F32)
            dh_ref[:, cs] = (dyv * act).astype(BF16)
            dgl_ref[:, cs] = (dyv * (h_ref[0, :, cs].astype(F32) + h_ref[1, :, cs].astype(F32)) * dact).astype(BF16)

    return rows.call(body, name, [pl.BlockSpec((2, rows.tm, w), lambda i: (0, i, 0)), rows.row(w), rows.row(w)],
                     [rows.row(w), rows.row(w)],
                     [jax.ShapeDtypeStruct((rows.t_all, w), BF16), jax.ShapeDtypeStruct((rows.t_all, w), BF16)])(h, gl, dy)


def z_merge(rows, pgc, yc, yl, name):
    d = rows.d

    def body(gc_ref, gr_ref, yc_ref, yl_ref, o_ref):
        for cs in _chunks(d):
            o_ref[:, cs] = (_sigmoid(gc_ref[:, cs].astype(F32)) * yc_ref[:, cs].astype(F32)
                            + _sigmoid(gr_ref[:, cs].astype(F32)) * yl_ref[:, cs].astype(F32)).astype(BF16)

    return rows.call(body, name, [rows.row(d, 0), rows.row(d, 1), rows.row(d), rows.row(d)], rows.row(d),
                     jax.ShapeDtypeStruct((rows.t_all, d), BF16))(pgc, pgc, yc, yl)


def z_merge_bwd(rows, pgc, yc, yl, dz, name):
    d = rows.d

    def body(gc_ref, gr_ref, yc_ref, yl_ref, dz_ref, dyc_ref, dyl_ref, dp_ref):
        for cs in _chunks(d):
            sc, sr = _sigmoid(gc_ref[:, cs].astype(F32)), _sigmoid(gr_ref[:, cs].astype(F32))
            dzv = dz_ref[:, cs].astype(F32)
            dyc_ref[:, cs] = (dzv * sc).astype(BF16)
            dyl_ref[:, cs] = (dzv * sr).astype(BF16)
            dp_ref[:, cs] = (dzv * yc_ref[:, cs].astype(F32) * sc * (1.0 - sc)).astype(BF16)
            dp_ref[:, slice(d + cs.start, d + cs.stop)] = (dzv * yl_ref[:, cs].astype(F32) * sr * (1.0 - sr)).astype(BF16)

    return rows.call(body, name, [rows.row(d, 0), rows.row(d, 1), rows.row(d), rows.row(d), rows.row(d)],
                     [rows.row(d), rows.row(d), rows.row(2 * d)],
                     [jax.ShapeDtypeStruct((rows.t_all, d), BF16), jax.ShapeDtypeStruct((rows.t_all, d), BF16),
                      jax.ShapeDtypeStruct((rows.t_all, 2 * d), BF16)])(pgc, pgc, yc, yl, dz)


def loss_head(rows, x, f, mv, g, target, name):
    d = rows.d

    def body(x_ref, f_ref, mv_ref, g_ref, t_ref, loss_ref, dx_ref, dg_ref, df_ref, dgate_ref):
        i = pl.program_id(0)
        valid = jnp.where(i < rows.n_lat, 1.0, 0.0)
        xv = x_ref[...] + 0.5 * mv_ref[8:9, :] * f_ref[...]
        gv = g_ref[...]
        r = lax.rsqrt(jnp.mean(xv * xv, axis=-1, keepdims=True) + EPS)
        n = xv * r
        err = (n * gv - t_ref[...]) * valid
        part = 0.5 * jnp.sum(jnp.mean(err * err, axis=-1, keepdims=True), axis=0, keepdims=True)
        _accumulate(loss_ref, jnp.broadcast_to(part, (1, LANES)), i == 0)
        dy = err * (1.0 / d)
        dn = dy * gv
        dx = r * (dn - n * jnp.mean(dn * n, axis=-1, keepdims=True))
        dx_ref[...] = dx
        _accumulate(dg_ref, _colsum(dy * n), i == 0)
        _gate_bwd(rows, i, dx, f_ref, mv_ref, 8, 0.5, df_ref, dgate_ref)

    target_spec = pl.BlockSpec((rows.tm, d), lambda i: (jnp.minimum(i, rows.n_lat - 1), 0))
    return rows.call(body, name, [rows.row(d), rows.row(d), rows.mod(), rows.vec(d), target_spec],
                     [rows.vec(LANES), rows.row(d), rows.vec(d), rows.row(d), rows.segsum()],
                     [jax.ShapeDtypeStruct((1, LANES), F32), jax.ShapeDtypeStruct((rows.t_all, d), F32),
                      jax.ShapeDtypeStruct((1, d), F32), jax.ShapeDtypeStruct((rows.t_all, d), BF16), rows.segsum_shape()])(
                          x, f, mv, g, target)


def _extended(rows, i, prev_ref, cur_ref, next_ref, cs):
    prev = jnp.where(rows.seg_first(i), 0.0, prev_ref[:, cs].astype(F32))
    nxt = jnp.where(rows.seg_last(i), 0.0, next_ref[:, cs].astype(F32))
    return jnp.concatenate([prev, cur_ref[:, cs].astype(F32), nxt], axis=0)


class _Shifts:
    def __init__(self, ext, tm):
        self.ext, self.tm, self.rolled = ext, tm, {0: ext}

    def at(self, offset):
        residue = offset % SUBLANES
        if residue not in self.rolled:
            self.rolled[residue] = pltpu.roll(self.ext, self.ext.shape[0] - residue, 0)
        start = HALO + offset - residue
        return self.rolled[residue][start:start + self.tm]


def _pad_taps(w):
    k = w.shape[0]
    return jnp.pad(w, ((0, -k % SUBLANES), (0, 0)))


def dwconv(rows, u, w, b, pad_left, out_dtype, name):
    taps, width = w.shape
    wp = _pad_taps(w)

    def body(*refs):
        prev_ref, cur_ref, next_ref, w_ref = refs[:4]
        b_ref = refs[4] if b is not None else None
        o_ref = refs[-1]
        i = pl.program_id(0)
        for cs in _chunks(width, LANES):
            shifts = _Shifts(_extended(rows, i, prev_ref, cur_ref, next_ref, cs), rows.tm)
            acc = jnp.zeros((rows.tm, LANES), F32) if b_ref is None else jnp.broadcast_to(b_ref[:, cs], (rows.tm, LANES))
            for k in range(taps):
                acc = acc + w_ref[k:k + 1, cs] * shifts.at(k - pad_left)
            o_ref[:, cs] = acc.astype(out_dtype)

    ins = [u, u, u, wp] + ([b] if b is not None else [])
    in_specs = [rows.halo(width, -1), rows.row(width), rows.halo(width, 1), rows.full(wp.shape)] + ([rows.vec(width)] if b is not None else [])
    return rows.call(body, name, in_specs, rows.row(width), jax.ShapeDtypeStruct((rows.t_all, width), out_dtype))(*ins)


def dwconv_bwd_w(rows, u, dy, taps, pad_left, name):
    width = u.shape[1]
    taps_p = taps + (-taps % SUBLANES)

    def body(prev_ref, cur_ref, next_ref, dy_ref, dw_ref, db_ref):
        i = pl.program_id(0)
        tap_row = lax.broadcasted_iota(jnp.int32, (taps_p, LANES), 0)
        for cs in _chunks(width, LANES):
            shifts = _Shifts(_extended(rows, i, prev_ref, cur_ref, next_ref, cs), rows.tm)
            dyv = dy_ref[:, cs]
            total = jnp.zeros((taps_p, LANES), F32)
            for k in range(taps):
                total = total + jnp.where(tap_row == k, _colsum(dyv * shifts.at(k - pad_left)), 0.0)
            _accumulate(dw_ref.at[:, cs], total, i == 0)
            _accumulate(db_ref.at[:, cs], _colsum(dyv), i == 0)

    dw, db = rows.call(body, name, [rows.halo(width, -1), rows.row(width), rows.halo(width, 1), rows.row(width)],
                       [rows.full((taps_p, width)), rows.vec(width)],
                       [jax.ShapeDtypeStruct((taps_p, width), F32), jax.ShapeDtypeStruct((1, width), F32)])(u, u, u, dy)
    return dw[:taps], db


def _gate_values(xr, pre, lam, d):
    r = _sigmoid(pre[:, (2 * d) * LANES:(2 * d + 1) * LANES])
    ig = _sigmoid(pre[:, (2 * d + 1) * LANES:(2 * d + 2) * LANES])
    sp = _softplus(-lam[:, d * LANES:(d + 1) * LANES])
    log_a = -LRU_C * r * sp
    return r, ig, sp, jnp.exp(log_a), _neg_expm1(2.0 * log_a)


def lru_gates(rows, xr, wcat, bcat, lam, name):
    n_blk = wcat.shape[0]
    width = xr.shape[1]

    def body(xr_ref, w_ref, b_ref, lam_ref, a_ref, bb_ref):
        for h in range(n_blk):
            cs = slice(h * LANES, (h + 1) * LANES)
            xv = xr_ref[:, cs]
            pre = jnp.dot(xv.astype(BF16), w_ref[h], preferred_element_type=F32) + b_ref[h]
            for d in range(2):
                _, ig, _, a, one_minus_a2 = _gate_values(xv, pre, lam_ref[h], d)
                a_ref[d, :, cs] = a
                bb_ref[d, :, cs] = (jnp.sqrt(one_minus_a2) * (ig * xv)).astype(BF16)

    both = pl.BlockSpec((2, rows.tm, width), lambda i: (0, i, 0))
    return rows.call(body, name, [rows.row(width), rows.full(wcat.shape), rows.full(bcat.shape), rows.full(lam.shape)], [both, both],
                     [jax.ShapeDtypeStruct((2, rows.t_all, width), F32), jax.ShapeDtypeStruct((2, rows.t_all, width), BF16)])(
                         xr, wcat, bcat, lam)


def lru_gates_bwd(rows, xr, wcat, bcat, lam, da, dbb, name):
    n_blk = wcat.shape[0]
    width = xr.shape[1]

    def body(xr_ref, w_ref, b_ref, lam_ref, da_ref, dbb_ref, dxr_ref, dw_ref, db_ref, dlam_ref):
        i = pl.program_id(0)
        for h in range(n_blk):
            cs = slice(h * LANES, (h + 1) * LANES)
            xv = xr_ref[:, cs]
            xb = xv.astype(BF16)
            wv = w_ref[h]
            pre = jnp.dot(xb, wv, preferred_element_type=F32) + b_ref[h]
            dxr = jnp.zeros_like(xv)
            dpre, dlam = [], []
            for d in range(2):
                r, ig, sp, a, one_minus_a2 = _gate_values(xv, pre, lam_ref[h], d)
                inv_q = lax.rsqrt(one_minus_a2)
                dav, dbv = da_ref[d, :, cs].astype(F32), dbb_ref[d, :, cs].astype(F32)
                dbq = dbv * (one_minus_a2 * inv_q) * ig
                dlog_a = dav * a - dbv * (ig * xv) * ((a * a) * inv_q)
                dpre.append(dlog_a * (-LRU_C * sp) * (r - r * r))
                dpre.append(dbq * xv * (1.0 - ig))
                dxr = dxr + dbq
                dlam.append(_colsum(dlog_a * (-LRU_C * r)) * (-_sigmoid(-lam_ref[h][:, d * LANES:(d + 1) * LANES])))
            dpre = jnp.concatenate(dpre, axis=1)
            dpb = dpre.astype(BF16)
            dxr_ref[:, cs] = dxr + lax.dot_general(dpb, wv, (((1,), (1,)), ((), ())), preferred_element_type=F32)
            _accumulate(dw_ref.at[h], lax.dot_general(xb, dpb, (((0,), (0,)), ((), ())), preferred_element_type=F32), i == 0)
            _accumulate(db_ref.at[h], _colsum(dpre), i == 0)
            _accumulate(dlam_ref.at[h], jnp.concatenate(dlam, axis=1), i == 0)

    both = pl.BlockSpec((2, rows.tm, width), lambda i: (0, i, 0))
    return rows.call(
        body, name, [rows.row(width), rows.full(wcat.shape), rows.full(bcat.shape), rows.full(lam.shape), both, both],
        [rows.row(width), rows.full(wcat.shape), rows.full(bcat.shape), rows.full(lam.shape)],
        [jax.ShapeDtypeStruct((rows.t_all, width), F32), jax.ShapeDtypeStruct(wcat.shape, F32),
         jax.ShapeDtypeStruct(bcat.shape, F32), jax.ShapeDtypeStruct(lam.shape, F32)])(xr, wcat, bcat, lam, da, dbb)


def _tile_scan(a, b, reverse):
    n = a.shape[0]
    row = lax.broadcasted_iota(jnp.int32, a.shape, 0)
    k = 1
    while k < n:
        ok = (row < n - k) if reverse else (row >= k)
        shift = n - k if reverse else k
        b = b + a * jnp.where(ok, pltpu.roll(b, shift, 0), 0.0)
        a = a * jnp.where(ok, pltpu.roll(a, shift, 0), 1.0)
        k *= 2
    return a, b


def _chain_scan(a, b, h_in, reverse):
    n_blocks = a.shape[0] // SUBLANES
    out = [None] * n_blocks
    state = h_in
    for j in (range(n_blocks - 1, -1, -1) if reverse else range(n_blocks)):
        rows_j = slice(j * SUBLANES, (j + 1) * SUBLANES)
        cum, h0 = _tile_scan(a[rows_j], b[rows_j], reverse)
        out[j] = h0 + cum * state
        state = out[j][0:1] if reverse else out[j][SUBLANES - 1:SUBLANES]
    return jnp.concatenate(out, axis=0), state


def _neighbour(v, edge, reverse):
    n = v.shape[0]
    row = lax.broadcasted_iota(jnp.int32, v.shape, 0)
    if reverse:
        return jnp.where(row < n - 1, pltpu.roll(v, n - 1, 0), edge)
    return jnp.where(row >= 1, pltpu.roll(v, 1, 0), edge)


def _scan_call(rows, body, name, ins, in_specs, n_out, width, adjoint):
    n_all, n_lat = rows.n_all, rows.n_lat

    def tile(d, s):
        s = n_all - 1 - s if adjoint else s
        return jnp.where(d == 0, (s + n_lat) % n_all, n_all - 1 - s)

    def per_dir(d, s):
        return (d, tile(d, s), 0)

    specs = [pl.BlockSpec((None, rows.tm, width), per_dir) if kind == "dir" else
             pl.BlockSpec((rows.tm, width), lambda d, s: (tile(d, s), 0)) for kind in in_specs]
    shape = jax.ShapeDtypeStruct((2, rows.t_all, width), BF16)
    return pl.pallas_call(
        body, name=name, grid=(2, n_all), in_specs=specs, out_specs=[pl.BlockSpec((None, rows.tm, width), per_dir)] * n_out,
        out_shape=[shape] * n_out, scratch_shapes=[pltpu.VMEM((SUBLANES, width), F32)],
        compiler_params=_params("arbitrary", "arbitrary"))(*ins)


def lru_scan(rows, a, bb, name):
    width = a.shape[2]

    def body(a_ref, bb_ref, h_ref, hp_ref, carry):
        d, s = pl.program_id(0), pl.program_id(1)

        @pl.when(s == 0)
        def _():
            carry[...] = jnp.zeros_like(carry)

        def run(reverse):
            for cs in _chunks(width, LANES):
                h_in = carry[0:1, cs]
                h, carry[0:1, cs] = _chain_scan(a_ref[:, cs], bb_ref[:, cs].astype(F32), h_in, reverse)
                h_ref[:, cs] = h.astype(BF16)
                hp_ref[:, cs] = _neighbour(h, h_in, reverse).astype(BF16)

        pl.when(d == 0)(lambda: run(False))
        pl.when(d == 1)(lambda: run(True))

    return _scan_call(rows, body, name, [a, bb], ["dir", "dir"], 2, width, adjoint=False)


def lru_scan_bwd(rows, dh, a, hp, name):
    width = a.shape[2]
    n = rows.tm

    def body(dh_ref, a_ref, hp_ref, da_ref, dbb_ref, carry):
        d, s = pl.program_id(0), pl.program_id(1)

        @pl.when(s == 0)
        def _():
            carry[...] = jnp.zeros_like(carry)

        def run(reverse):
            for cs in _chunks(width, LANES):
                av = a_ref[:, cs]
                g, _ = _chain_scan(_neighbour(av, 1.0, reverse), dh_ref[:, cs].astype(F32), carry[0:1, cs], reverse)
                da_ref[:, cs] = (g * hp_ref[:, cs].astype(F32)).astype(BF16)
                dbb_ref[:, cs] = g.astype(BF16)
                carry[0:1, cs] = (av * g)[0:1] if reverse else (av * g)[n - 1:n]

        pl.when(d == 0)(lambda: run(True))
        pl.when(d == 1)(lambda: run(False))

    return _scan_call(rows, body, name, [dh, a, hp], ["shared", "dir", "dir"], 2, width, adjoint=True)


def mod_matmul(c9, w_shard, b_shard, name):
    n = w_shard.shape[1]
    tn = _pick(n, 768)

    def body(c_ref, w_ref, b_ref, o_ref):
        act = _silu_and_grad(c_ref[...])[0]
        o_ref[...] = jnp.dot(act, w_ref[...], preferred_element_type=F32, precision=lax.Precision.HIGHEST) + b_ref[...]

    return pl.pallas_call(
        body, name=name, grid=(n // tn,),
        in_specs=[pl.BlockSpec(c9.shape, lambda j: (0, 0)), pl.BlockSpec((w_shard.shape[0], tn), lambda j: (0, j)),
                  pl.BlockSpec((1, tn), lambda j: (0, j))],
        out_specs=pl.BlockSpec((c9.shape[0], tn), lambda j: (0, j)), out_shape=jax.ShapeDtypeStruct((c9.shape[0], n), F32),
        compiler_params=_params("arbitrary"))(c9, w_shard, b_shard)


def mod_matmul_bwd(c9, d9, w_shard, name):
    n = w_shard.shape[1]
    tn = _pick(n, 768)
    steps = n // tn

    def body(c_ref, d_ref, w_ref, gw_ref, gc_ref):
        j = pl.program_id(0)
        act, dact = _silu_and_grad(c_ref[...])
        dv = d_ref[...]
        gw_ref[...] = lax.dot_general(act, dv, (((0,), (0,)), ((), ())), preferred_element_type=F32, precision=lax.Precision.HIGHEST)
        part = lax.dot_general(dv, w_ref[...], (((1,), (1,)), ((), ())), preferred_element_type=F32, precision=lax.Precision.HIGHEST)
        _accumulate(gc_ref, part * dact, j == 0)

    return pl.pallas_call(
        body, name=name, grid=(steps,),
        in_specs=[pl.BlockSpec(c9.shape, lambda j: (0, 0)), pl.BlockSpec((c9.shape[0], tn), lambda j: (0, j)),
                  pl.BlockSpec((w_shard.shape[0], tn), lambda j: (0, j))],
        out_specs=[pl.BlockSpec((w_shard.shape[0], tn), lambda j: (0, j)), pl.BlockSpec(c9.shape, lambda j: (0, 0))],
        out_shape=[jax.ShapeDtypeStruct(w_shard.shape, F32), jax.ShapeDtypeStruct(c9.shape, F32)],
        compiler_params=_params("arbitrary"))(c9, d9, w_shard)


def _row_tile(n_rows, n_cols):
    return _pick(n_rows, max(2 * SUBLANES, (256 * 1024) // n_cols), 2 * SUBLANES)


def sum_parts(parts, name):
    n, n_rows, n_cols = parts.shape
    tr = _row_tile(n_rows, n_cols)

    def body(p_ref, o_ref):
        total = p_ref[0].astype(F32)
        for k in range(1, n):
            total = total + p_ref[k].astype(F32)
        o_ref[...] = total

    return pl.pallas_call(
        body, name=name, grid=(n_rows // tr,), in_specs=[pl.BlockSpec((n, tr, n_cols), lambda i: (0, i, 0))],
        out_specs=pl.BlockSpec((tr, n_cols), lambda i: (i, 0)), out_shape=jax.ShapeDtypeStruct((n_rows, n_cols), F32),
        compiler_params=_params("arbitrary"))(parts)


def _adamw_update(g, w_ref, m_ref, v_ref, g_ref, d_ref, m2_ref, v2_ref):
    m2 = ADAM_B1 * m_ref[...] + (1.0 - ADAM_B1) * g
    v2 = ADAM_B2 * v_ref[...] + (1.0 - ADAM_B2) * (g * g)
    m_hat = m2 / (1.0 - ADAM_B1 ** ADAM_STEP)
    v_hat = v2 / (1.0 - ADAM_B2 ** ADAM_STEP)
    g_ref[...] = g
    d_ref[...] = -ADAM_LR * (m_hat / (jnp.sqrt(v_hat) + ADAM_EPS) + ADAM_WD * w_ref[...])
    m2_ref[...] = m2
    v2_ref[...] = v2


def adamw(parts, w, m, v, name):
    n, n_rows, n_cols = parts.shape
    tr = _row_tile(n_rows, n_cols)

    def body(p_ref, *refs):
        g = p_ref[0].astype(F32)
        for k in range(1, n):
            g = g + p_ref[k].astype(F32)
        _adamw_update(g, *refs)

    blk = pl.BlockSpec((tr, n_cols), lambda i: (i, 0))
    shape = jax.ShapeDtypeStruct((n_rows, n_cols), F32)
    return pl.pallas_call(
        body, name=name, grid=(n_rows // tr,), in_specs=[pl.BlockSpec((n, tr, n_cols), lambda i: (0, i, 0)), blk, blk, blk],
        out_specs=[blk] * 4, out_shape=[shape] * 4, compiler_params=_params("arbitrary"))(parts, w, m, v)


def adamw_pair(parts, w, m, v, name):
    n_parts, n_rows, n_cols = parts.shape
    tr = _row_tile(n_rows, n_cols)
    steps = n_rows // tr

    def body(p_ref, w_ref, m_ref, v_ref, g_ref, d_ref, m2_ref, v2_ref, outbox, inbox, send_sems, recv_sems, credits):
        i = pl.program_id(0)
        sibling = (lax.axis_index("x"), lax.axis_index("y"), 1 - lax.axis_index("c"))

        def copy(slot):
            return pltpu.make_async_remote_copy(src_ref=outbox.at[slot], dst_ref=inbox.at[slot], send_sem=send_sems.at[slot],
                                                recv_sem=recv_sems.at[slot], device_id=sibling, device_id_type=MESH)

        @pl.when(i < steps)
        def _():
            slot = i % 2
            total = p_ref[0].astype(F32)
            for k in range(1, n_parts):
                total = total + p_ref[k].astype(F32)
            outbox[slot] = total

            @pl.when(i >= 2)
            def _():
                pl.semaphore_wait(credits.at[slot], 1)

            copy(slot).start()

        @pl.when(i >= 1)
        def _():
            slot = (i - 1) % 2
            landed = copy(slot)
            landed.wait_recv()
            landed.wait_send()
            _adamw_update(outbox[slot] + inbox[slot], w_ref, m_ref, v_ref, g_ref, d_ref, m2_ref, v2_ref)

            @pl.when(i + 1 < steps)
            def _():
                pl.semaphore_signal(credits.at[slot], inc=1, device_id=sibling, device_id_type=MESH)

    blk = pl.BlockSpec((tr, n_cols), lambda i: (jnp.maximum(i - 1, 0), 0))
    shape = jax.ShapeDtypeStruct((n_rows, n_cols), F32)
    return pl.pallas_call(
        body, name=name, grid=(steps + 1,),
        in_specs=[pl.BlockSpec((n_parts, tr, n_cols), lambda i: (0, jnp.minimum(i, steps - 1), 0)), blk, blk, blk],
        out_specs=[blk] * 4, out_shape=[shape] * 4,
        scratch_shapes=[pltpu.VMEM((2, tr, n_cols), F32), pltpu.VMEM((2, tr, n_cols), F32), pltpu.SemaphoreType.DMA((2,)),
                        pltpu.SemaphoreType.DMA((2,)), pltpu.SemaphoreType.REGULAR((2,))],
        compiler_params=_params("arbitrary"))(parts, w, m, v)


def _pack(arrays, row_multiple=SUBLANES):
    flat = [jnp.pad(a.reshape(-1), (0, -a.size % LANES)) for a in arrays]
    buf = jnp.concatenate(flat)
    buf = jnp.pad(buf, (0, -buf.size % (row_multiple * LANES)))
    return buf.reshape(-1, LANES)


def _unpack(buf, shapes):
    lead = buf.shape[:-2]
    flat = buf.reshape(lead + (-1,))
    out, pos = [], 0
    for shape in shapes:
        size = 1
        for s in shape:
            size *= s
        out.append(flat[..., pos:pos + size].reshape(lead + tuple(shape)))
        pos += size + (-size % LANES)
    return out


def _pos_tables(seq_len, dim, tile_rows):
    grid_rows = seq_len // GRID_W
    per_tile = tile_rows // GRID_W
    q = dim // 4
    omega = 1.0 / (10000.0 ** (jnp.arange(q, dtype=F32) / q))
    er = jnp.arange(grid_rows).astype(F32)[:, None] * omega
    ec = jnp.arange(GRID_W).astype(F32)[:, None] * omega
    by_row = jnp.concatenate([jnp.sin(er), jnp.cos(er)], axis=-1).reshape(grid_rows // per_tile, per_tile, 2 * q)
    by_row = jnp.pad(by_row, ((0, 0), (0, SUBLANES - per_tile), (0, 0))).reshape(-1, 2 * q)
    return by_row, jnp.concatenate([jnp.sin(ec), jnp.cos(ec)], axis=-1)


def _cols_from_shards(g):
    return jnp.transpose(g, (1, 0, 2)).reshape(g.shape[1], -1)


def _shards_from_cols(w, n_shards=4):
    k, n = w.shape
    return jnp.transpose(w.reshape(k, n_shards, n // n_shards), (1, 0, 2))


def kernel(x, c, ctx, c_ctx, w_mod, b_mod, g_n1, w_ffn1_up, w_ffn1_down, g_n2, w_in, b_in, w_dw, b_dw, g_ln, b_ln, w_conf_out, w_lru_conv, b_lru_conv, w_rec_gate, b_rec_gate, w_in_gate, b_in_gate, lru_lambda, w_lru_out, w_out, g_n3, w_ffn2_up, w_ffn2_down, g_final, loss_target, m_c_ctx, m_w_mod, m_b_mod, m_g_n1, m_w_ffn1_up, m_w_ffn1_down, m_g_n2, m_w_in, m_b_in, m_w_dw, m_b_dw, m_g_ln, m_b_ln, m_w_conf_out, m_w_lru_conv, m_b_lru_conv, m_w_rec_gate, m_b_rec_gate, m_w_in_gate, m_b_in_gate, m_lru_lambda, m_w_lru_out, m_w_out, m_g_n3, m_w_ffn2_up, m_w_ffn2_down, m_g_final, v_c_ctx, v_w_mod, v_b_mod, v_g_n1, v_w_ffn1_up, v_w_ffn1_down, v_g_n2, v_w_in, v_b_in, v_w_dw, v_b_dw, v_g_ln, v_b_ln, v_w_conf_out, v_w_lru_conv, v_b_lru_conv, v_w_rec_gate, v_b_rec_gate, v_w_in_gate, v_b_in_gate, v_lru_lambda, v_w_lru_out, v_w_out, v_g_n3, v_w_ffn2_up, v_w_ffn2_down, v_g_final):
    given = dict(locals())
    wt = {n: given[n] for n in W_NAMES}
    mom = {n: given["m_" + n] for n in W_NAMES}
    var = {n: given["v_" + n] for n in W_NAMES}

    def flat2(a):
        if a.ndim == 1:
            return a.reshape(1, -1)
        a = a[0]
        return a if a.ndim == 2 else a.reshape(-1, a.shape[-1])

    t_lat, d = x.shape[1], x.shape[2]
    t_ctx = ctx.shape[1]
    rows = Rows(t_lat, t_ctx, d)
    xi, yi, ci = lax.axis_index("x"), lax.axis_index("y"), lax.axis_index("c")
    me, chip = 4 * xi + 2 * yi + ci, 2 * xi + yi
    lru_w = b_lru_conv.shape[-1]
    n_blk = lru_w // LANES
    taps = w_dw.shape[1]
    lru_taps = w_lru_conv.shape[1]
    ffn = w_ffn1_down.shape[1] * 4

    small_shapes = [(1, d)] + [flat2(wt[n]).shape for n in SMALL_SHARDED]
    small_all = exchange("all", [_pack([c] + [flat2(wt[n]) for n in SMALL_SHARDED])], False, "gather_small")[0]
    small_all = _unpack(small_all, small_shapes)
    c_all = small_all[0][:, 0, :]
    full = {n: jnp.concatenate([a[0], a[2], a[4], a[6]], axis=-1) for n, a in zip(SMALL_SHARDED, small_all[1:])}
    big = COL_SHARDED + ROW_SHARDED
    ffn1_names = ['w_ffn1_up', 'w_ffn1_down']
    ffn2_names = ['w_ffn2_up', 'w_ffn2_down']
    mixer_names = [n for n in big if n not in ffn1_names + ffn2_names + ['w_in']]
    wb = {}

    def take_weights(names, gathered):
        for n, g in zip(names, gathered):
            wb[n] = _cols_from_shards(g) if n in COL_SHARDED else g.reshape(-1, g.shape[-1])

    def shards_after(names, earlier):
        return lax.optimization_barrier(([wt[n][0].astype(BF16) for n in names], earlier))[0]

    c9 = jnp.concatenate([c_all, c_ctx.reshape(1, d), jnp.zeros((7, d), F32)], axis=0)
    mod_cols = mod_matmul(c9, w_mod[0], lax.dynamic_slice_in_dim(b_mod, chip * w_mod.shape[2], w_mod.shape[2], axis=1), "mod_matmul")
    mod_all = exchange("all", [mod_cols], False, "gather_mod")[0]
    mod9 = jnp.concatenate([mod_all[0], mod_all[2], mod_all[4], mod_all[6]], axis=-1)
    mv = jnp.stack([lax.dynamic_index_in_dim(mod9, me, 0, keepdims=False).reshape(N_MOD, d), mod9[8].reshape(N_MOD, d)])

    gathered_up1 = exchange_by_sequencer(shards_after(ffn1_names[:1], mod_all), False, "gather_weights_ffn1_up", 7)
    gathered_down1 = exchange_by_sequencer(shards_after(ffn1_names[1:], mod_all), False, "gather_weights_ffn1_down", 8)
    gathered_in = exchange_by_sequencer(shards_after(['w_in'], mod_all), False, "gather_weights_in", 1)
    gathered_mixer = exchange_by_sequencer(shards_after(mixer_names, mod_all), False, "gather_weights_mixer", 9)
    gathered_ffn2 = exchange_by_sequencer(shards_after(ffn2_names, mod_all), False, "gather_weights_ffn2", 5)

    row_tab, col_tab = _pos_tables(t_lat, d, rows.tm)

    def ffn_fwd(tag, h, names, after_up=None):
        g, u, act = matmul_swiglu(h, wb[names[0]], f"{tag}_up")
        if after_up is not None:
            act = after_up(act)
        return (g, u), act, matmul(act, wb[names[1]], "nn", F32, f"{tag}_down")

    def take_down1(act):
        landed, act = lax.optimization_barrier((gathered_down1, act))
        take_weights(ffn1_names[1:], landed)
        return act

    x0, h1 = embed_norm_mod(rows, x[0], ctx[0], row_tab, col_tab, g_n1, mv, 0, 1, "norm1")
    gathered_up1, h1 = lax.optimization_barrier((gathered_up1, h1))
    take_weights(ffn1_names[:1], gathered_up1)
    gu1, act1, f1 = ffn_fwd("ffn1", h1, ffn1_names, after_up=take_down1)
    x1, h2 = norm_mod(rows, x0, g_n2, mv, 3, 4, "norm2", resid=(f1, 2, 0.5))
    gathered_in, h2 = lax.optimization_barrier((gathered_in, h2))
    take_weights(['w_in'], gathered_in)

    conf_w = w_conf_out.shape[1] * 4
    col_groups = [(0, 2 * conf_w), (2 * conf_w, lru_w), (2 * conf_w + lru_w, lru_w), (2 * conf_w + 2 * lru_w, 2 * d)]
    w_in_groups = [wb["w_in"][:, s:s + n] for s, n in col_groups]
    pvg, ux, gl, pgc = matmul_groups_nn(h2, w_in_groups, [b_in[:, s:s + n] for s, n in col_groups], "in_proj")
    u = glu(rows, pvg, "glu")
    u2 = dwconv(rows, u, full["w_dw"], b_dw, taps // 2, F32, "conf_conv")
    s_act = ln_silu(rows, u2, g_ln, b_ln, "conf_ln")
    gathered_mixer, s_act = lax.optimization_barrier((gathered_mixer, s_act))
    take_weights(mixer_names, gathered_mixer)
    yc = matmul(s_act, wb["w_conf_out"], "nn", BF16, "conf_out")

    xr = dwconv(rows, ux, full["w_lru_conv"], b_lru_conv, LRU_PAD_LEFT, F32, "lru_conv")
    w_rec, w_ing = w_rec_gate[0].astype(BF16), w_in_gate[0].astype(BF16)
    wcat = jnp.concatenate([w_rec[0], w_ing[0], w_rec[1], w_ing[1]], axis=-1)

    def per_block(a):
        return jnp.transpose(a.reshape(2, n_blk, LANES), (1, 0, 2)).reshape(n_blk, 1, 2 * LANES)

    b_rec, b_ing = full["b_rec_gate"].reshape(2, n_blk, LANES), full["b_in_gate"].reshape(2, n_blk, LANES)
    bcat = jnp.concatenate([b_rec[0], b_ing[0], b_rec[1], b_ing[1]], axis=-1).reshape(n_blk, 1, 4 * LANES)
    lam = per_block(full["lru_lambda"])
    a_gate, b_gate = lru_gates(rows, xr, wcat, bcat, lam, "lru_gates")
    h_scan, h_prev = lru_scan(rows, a_gate, b_gate, "lru_scan")
    yl_in = lru_merge(rows, h_scan, gl, "lru_merge")
    yl = matmul(yl_in, wb["w_lru_out"], "nn", BF16, "lru_out")
    z = z_merge(rows, pgc, yc, yl, "z_merge")
    y = matmul(z, wb["w_out"], "nn", F32, "mix_out")

    gathered_ffn2, y = lax.optimization_barrier((gathered_ffn2, y))
    take_weights(ffn2_names, gathered_ffn2)
    x2, h3 = norm_mod(rows, x1, g_n3, mv, 6, 7, "norm3", resid=(y, 5, 1.0))
    gu2, act2, f2 = ffn_fwd("ffn2", h3, ffn2_names)
    loss_part, dx3, dg_final, df2, dgate2 = loss_head(rows, x2, f2, mv, g_final.reshape(1, d), loss_target[0], "loss_head")

    grads = {"g_final": dg_final}
    dmv = [None] * N_MOD

    def ffn_bwd(tag, dxn, df, x_prev, h, gu, act, w_up, w_down, g, ks, collective_id, settle_first=(), **norm_bwd_options):
        k_shift, k_scale = ks
        grads[f"w_{tag}_down"] = matmul(act, df, "tn", BF16, f"{tag}_down_dw")
        if settle_first:
            df = settle(settle_first, df)
        dg_, du_ = matmul_swiglu_bwd(df, w_down, gu[0], gu[1], f"{tag}_down_dx")
        grads[f"w_{tag}_up"] = jnp.concatenate([matmul(h, dg_, "tn", BF16, f"{tag}_up_dw_g"), matmul(h, du_, "tn", BF16, f"{tag}_up_dw_u")], axis=1)
        dg_, du_ = scatter_behind([f"w_{tag}_up", f"w_{tag}_down"], tag, collective_id, (dg_, du_))
        dh = matmul_pair_nt(dg_, du_, w_up, f"{tag}_up_dx")
        dx, dg, dmv[k_shift], dmv[k_scale], *more = norm_mod_bwd(rows, x_prev, dh, dxn, g, mv, k_shift, k_scale, f"{tag}_norm_bwd",
                                                                 **norm_bwd_options)
        return (dx, dg, *more)

    def grad_pieces(names):
        return [(_shards_from_cols(grads[n]) if n in COL_SHARDED else grads[n].reshape(4, -1, grads[n].shape[-1])).astype(BF16)
                for n in names]

    from_chips = {}

    def scatter_behind(names, tag, collective_id, carry):
        ready, carry = lax.optimization_barrier(([grads[n] for n in names], carry))
        grads.update(zip(names, ready))
        from_chips.update(zip(names, exchange_by_sequencer(grad_pieces(names), True, f"scatter_grads_{tag}", collective_id)))
        return carry

    def settle(names, carry):
        landed, carry = lax.optimization_barrier(([from_chips[n] for n in names], carry))
        from_chips.update(zip(names, landed))
        return carry

    dmv[8] = dgate2
    dx2, grads["g_n3"], dy, dmv[5] = ffn_bwd("ffn2", dx3, df2, x2, h3, gu2, act2, wb["w_ffn2_up"], wb["w_ffn2_down"], g_n3, (6, 7), 2,
                                             then_gate=(y, 5, 1.0))

    grads["w_out"] = matmul(z, dy, "tn", BF16, "mix_out_dw")
    dz = matmul(dy, wb["w_out"], "nt", BF16, "mix_out_dx")
    dyc, dyl, dpgc = z_merge_bwd(rows, pgc, yc, yl, dz, "z_merge_bwd")

    grads["w_conf_out"] = matmul(s_act, dyc, "tn", BF16, "conf_out_dw")
    ds_act = matmul(dyc, wb["w_conf_out"], "nt", BF16, "conf_out_dx")
    du2, grads["g_ln"], grads["b_ln"] = ln_silu_bwd(rows, u2, ds_act, g_ln, b_ln, "conf_ln_bwd")
    grads["w_dw"], grads["b_dw"] = dwconv_bwd_w(rows, u, du2, taps, taps // 2, "conf_conv_dw")
    du = dwconv(rows, du2, full["w_dw"][::-1], None, taps - 1 - taps // 2, F32, "conf_conv_dx")
    dpvg = glu_bwd(rows, pvg, du, "glu_bwd")

    grads["w_lru_out"] = matmul(yl_in, dyl, "tn", BF16, "lru_out_dw")
    dyl = scatter_behind(['w_out', 'w_conf_out', 'w_lru_out'], "mixer", 3, settle(ffn2_names, dyl))
    dyl_in = matmul(dyl, wb["w_lru_out"], "nt", BF16, "lru_out_dx")
    dh_sum, dgl = lru_merge_bwd(rows, h_scan, gl, dyl_in, "lru_merge_bwd")
    da_gate, db_gate = lru_scan_bwd(rows, dh_sum, a_gate, h_prev, "lru_scan_bwd")
    dxr, dwcat, dbcat, dlam = lru_gates_bwd(rows, xr, wcat, bcat, lam, da_gate, db_gate, "lru_gates_bwd")
    grads["w_rec_gate"] = jnp.stack([dwcat[:, :, 0:LANES], dwcat[:, :, 2 * LANES:3 * LANES]])
    grads["w_in_gate"] = jnp.stack([dwcat[:, :, LANES:2 * LANES], dwcat[:, :, 3 * LANES:4 * LANES]])
    dbcat = dbcat.reshape(n_blk, 4, LANES)
    grads["b_rec_gate"] = jnp.stack([dbcat[:, 0], dbcat[:, 2]]).reshape(2, lru_w)
    grads["b_in_gate"] = jnp.stack([dbcat[:, 1], dbcat[:, 3]]).reshape(2, lru_w)
    grads["lru_lambda"] = jnp.transpose(dlam.reshape(n_blk, 2, LANES), (1, 0, 2)).reshape(2, lru_w)
    grads["w_lru_conv"], grads["b_lru_conv"] = dwconv_bwd_w(rows, ux, dxr, lru_taps, LRU_PAD_LEFT, "lru_conv_dw")
    dux = dwconv(rows, dxr, full["w_lru_conv"][::-1], None, lru_taps - 1 - LRU_PAD_LEFT, BF16, "lru_conv_dx")

    dproj = [dpvg, dux, dgl, dpgc]
    grads["w_in"] = jnp.concatenate([matmul(h2, dp, "tn", BF16, f"in_proj{k}_dw") for k, dp in enumerate(dproj)], axis=1)
    dproj = scatter_behind(['w_in'], "in_proj", 4, settle(['w_out', 'w_conf_out', 'w_lru_out'], dproj))
    dh2, db_in = matmul_groups_nt(dproj, w_in_groups, "in_proj_dx")
    grads["b_in"] = jnp.concatenate(db_in, axis=1)
    dx1, grads["g_n2"], dmv[3], dmv[4], df1, dmv[2] = norm_mod_bwd(rows, x1, dh2, dx2, g_n2, mv, 3, 4, "norm2_bwd", then_gate=(f1, 2, 0.5))

    dx0, grads["g_n1"] = ffn_bwd("ffn1", dx1, df1, x0, h1, gu1, act1, wb["w_ffn1_up"], wb["w_ffn1_down"], g_n1, (0, 1), 6,
                                 settle_first=['w_in'], latent_only=True)
    grad_x = dx0[None]

    out = {}
    for n in big:
        parts = from_chips[n].reshape(4, -1, from_chips[n].shape[-1])
        out[n] = adamw_pair(parts, flat2(wt[n]), flat2(mom[n]), flat2(var[n]), f"adamw_{n}")

    dmod = jnp.concatenate(dmv, axis=1)
    small_names = ['g_n1', 'g_n2', 'b_in', 'b_dw', 'g_ln', 'b_ln', 'b_lru_conv', 'g_n3', 'g_final'] + SMALL_SHARDED
    gate_names = ['w_rec_gate', 'w_in_gate']
    small_list = [loss_part] + [grads[n] for n in small_names] + [dmod[0], dmod[1]]
    small_buf = _pack(small_list, 8 * 2 * SUBLANES)
    eighths = [small_buf.reshape(8, -1, LANES)] + [grads[n].reshape(8, -1, LANES) for n in gate_names]
    *parts, dmod_all = exchange("all", eighths + [_pack([dmod[0]])], [True] * len(eighths) + [False], "scatter_small_grads")
    sums = exchange("all", [sum_parts(p, f"sum_small_grads{k}") for k, p in enumerate(parts)], False, "gather_small_grads")
    small_sum = _unpack(sums[0].reshape(small_buf.shape), [a.shape for a in small_list])
    loss = small_sum[0][0, 0]
    total = dict(zip(small_names, small_sum[1:]))
    total.update({n: s.reshape(grads[n].shape) for n, s in zip(gate_names, sums[1:])})
    dmod_all = _unpack(dmod_all, [dmod[0].shape])[0].reshape(8, N_MOD * d)
    dmc = small_sum[-1].reshape(1, N_MOD * d)
    total["b_mod"] = small_sum[-2].reshape(1, N_MOD * d) + dmc

    d9 = jnp.concatenate([dmod_all, dmc, jnp.zeros((7, N_MOD * d), F32)], axis=0)
    d9_cols = lax.dynamic_slice_in_dim(d9, chip * w_mod.shape[2], w_mod.shape[2], axis=1)
    g_wmod, dc9 = mod_matmul_bwd(c9, d9_cols, w_mod[0], "mod_matmul_bwd")
    dc_all = exchange("all", [dc9[8:16]], False, "gather_dc")[0]
    total["c_ctx"] = sum_parts(jnp.stack([dc_all[0], dc_all[2], dc_all[4], dc_all[6]]), "sum_dc")[0:1]
    out["w_mod"] = adamw(g_wmod[None], w_mod[0], m_w_mod[0], v_w_mod[0], "adamw_w_mod")

    for n in gate_names:
        out[n] = adamw(total[n].reshape(1, -1, LANES), flat2(wt[n]), flat2(mom[n]), flat2(var[n]), f"adamw_{n}")

    small_all_names = [n for n in W_NAMES if n not in out]
    g_local, shapes = [], []
    for n in small_all_names:
        g = total[n].reshape(flat2(wt[n]).shape[:-1] + (-1,)) if n in SMALL_SHARDED else total[n].reshape(flat2(wt[n]).shape)
        if n in SMALL_SHARDED:
            width = wt[n].shape[-1]
            g = lax.dynamic_slice_in_dim(g, chip * width, width, axis=1)
        g_local.append(g)
        shapes.append(g.shape)
    packed = adamw(_pack(g_local)[None], _pack([flat2(wt[n]) for n in small_all_names]),
                   _pack([flat2(mom[n]) for n in small_all_names]), _pack([flat2(var[n]) for n in small_all_names]), "adamw_small")
    for k, n in enumerate(small_all_names):
        out[n] = [_unpack(p, shapes)[k] for p in packed]

    results = [loss, grad_x]
    for k in range(4):
        results += [out[n][k].reshape(wt[n].shape) for n in W_NAMES]
    return tuple(results)
```

```python
import jax
import jax.numpy as jnp
from jax import lax
from jax.experimental import pallas as pl
from jax.experimental.pallas import tpu as pltpu
from jax.experimental.pallas import tpu_sc as plsc

F32 = jnp.float32
BF16 = jnp.bfloat16
MESH = pl.DeviceIdType.MESH

EPS = 1e-6
GRID_W = 64
N_MOD = 9
LRU_C = 8.0
LRU_PAD_LEFT = 2
ADAM_LR, ADAM_B1, ADAM_B2, ADAM_EPS, ADAM_WD, ADAM_STEP = 0.001, 0.9, 0.999, 1e-08, 0.01, 10

LANES = 128
SUBLANES = 8
HALO = 16
VMEM_LIMIT = 56 * 1024 * 1024

W_NAMES = ['c_ctx', 'w_mod', 'b_mod', 'g_n1', 'w_ffn1_up', 'w_ffn1_down', 'g_n2', 'w_in', 'b_in', 'w_dw', 'b_dw',
           'g_ln', 'b_ln', 'w_conf_out', 'w_lru_conv', 'b_lru_conv', 'w_rec_gate', 'b_rec_gate', 'w_in_gate',
           'b_in_gate', 'lru_lambda', 'w_lru_out', 'w_out', 'g_n3', 'w_ffn2_up', 'w_ffn2_down', 'g_final']
COL_SHARDED = ['w_ffn1_up', 'w_in', 'w_ffn2_up']
ROW_SHARDED = ['w_ffn1_down', 'w_conf_out', 'w_lru_out', 'w_out', 'w_ffn2_down']
SMALL_SHARDED = ['w_dw', 'w_lru_conv', 'b_rec_gate', 'b_in_gate', 'lru_lambda']


def _params(*semantics):
    return pltpu.CompilerParams(dimension_semantics=semantics, vmem_limit_bytes=VMEM_LIMIT)


def _pick(n, target, mult=LANES):
    best = None
    for t in range(mult, min(n, target) + 1, mult):
        if n % t == 0:
            best = t
    return best or n


def _chunks(width, target=512):
    w = _pick(width, target)
    return [slice(s, s + w) for s in range(0, width, w)]


def _sigmoid(x):
    return jax.nn.sigmoid(x)


def _silu_and_grad(x):
    s = _sigmoid(x)
    return x * s, s * (1.0 + x * (1.0 - s))


_GELU_K = 0.7978845608028654


def _gelu_and_grad(x):
    t = jnp.tanh(_GELU_K * (x + 0.044715 * (x * x * x)))
    return 0.5 * x * (1.0 + t), 0.5 * (1.0 + t) + 0.5 * x * (1.0 - t * t) * _GELU_K * (1.0 + 3 * 0.044715 * x * x)


def _neg_expm1(z):
    series = -(z * (1.0 + z * (0.5 + z * (1.0 / 6.0))))
    return jnp.where(z > -0.01, series, 1.0 - jnp.exp(z))


def _softplus(x):
    return jnp.maximum(x, 0.0) + jnp.log1p(jnp.exp(-jnp.abs(x)))


def _accumulate(ref, value, first):
    @pl.when(first)
    def _():
        ref[...] = value

    @pl.when(jnp.logical_not(first))
    def _():
        ref[...] += value


def _colsum(x):
    return jnp.sum(x, axis=0, keepdims=True)


class Rows:
    def __init__(self, t_lat, t_ctx, d_model):
        self.tm = _pick(t_ctx, 256, HALO)
        assert t_lat % self.tm == 0 and t_ctx % self.tm == 0
        self.n_lat = t_lat // self.tm
        self.n_all = (t_lat + t_ctx) // self.tm
        self.t_all = t_lat + t_ctx
        self.d = d_model

    def seg(self, i):
        return jnp.where(i >= self.n_lat, 1, 0)

    def seg_first(self, i):
        return jnp.logical_or(i == 0, i == self.n_lat)

    def seg_last(self, i):
        return jnp.logical_or(i == self.n_lat - 1, i == self.n_all - 1)

    def row(self, width, col=0):
        return pl.BlockSpec((self.tm, width), lambda i: (i, col))

    def vec(self, width):
        return pl.BlockSpec((1, width), lambda i: (0, 0))

    def full(self, shape):
        return pl.BlockSpec(shape, lambda i: (0,) * len(shape))

    def mod(self):
        return pl.BlockSpec((None, N_MOD, self.d), lambda i: (self.seg(i), 0, 0))

    def segsum(self):
        return pl.BlockSpec((None, 1, self.d), lambda i: (self.seg(i), 0, 0))

    def segsum_shape(self):
        return jax.ShapeDtypeStruct((2, 1, self.d), F32)

    def halo(self, width, side):
        per = self.tm // HALO
        last = self.t_all // HALO - 1
        if side < 0:
            return pl.BlockSpec((HALO, width), lambda i: (jnp.maximum(i * per - 1, 0), 0))
        return pl.BlockSpec((HALO, width), lambda i: (jnp.minimum((i + 1) * per, last), 0))

    def call(self, body, name, in_specs, out_specs, out_shape, scratch=()):
        return pl.pallas_call(body, name=name, grid=(self.n_all,), in_specs=in_specs, out_specs=out_specs,
                              out_shape=out_shape, scratch_shapes=list(scratch), compiler_params=_params("arbitrary"))


_PEER_FLIPS = {
    "chips": [(1, 0, 0), (0, 1, 0), (1, 1, 0)],
    "all": [(0, 0, 1), (0, 1, 0), (0, 1, 1), (1, 0, 0), (1, 0, 1), (1, 1, 0), (1, 1, 1)],
}


def _slot(kind, x, y, c):
    return {"chips": 2 * x + y, "all": 4 * x + 2 * y + c}[kind]


def exchange(kind, srcs, indexed, name):
    flips = _PEER_FLIPS[kind]
    n_slots = len(flips) + 1
    n_t = len(srcs)
    by_slot = list(indexed) if isinstance(indexed, (list, tuple)) else [indexed] * n_t
    out_shapes = [jax.ShapeDtypeStruct(s.shape if ix else (n_slots,) + s.shape, s.dtype) for s, ix in zip(srcs, by_slot)]

    def body(*refs):
        src_refs, dst_refs = refs[:n_t], refs[n_t:2 * n_t]
        send_sems, recv_sems, local_sems = refs[2 * n_t:]
        x, y, c = lax.axis_index("x"), lax.axis_index("y"), lax.axis_index("c")
        me = _slot(kind, x, y, c)

        def piece(t, k):
            return src_refs[t].at[k] if by_slot[t] else src_refs[t]

        peers = [(1 - x if fx else x, 1 - y if fy else y, 1 - c if fc else c) for fx, fy, fc in flips]
        local = [pltpu.make_async_copy(piece(t, me), dst_refs[t].at[me], local_sems.at[t]) for t in range(n_t)]
        for cp in local:
            cp.start()
        sends = []
        for t in range(n_t):
            for p, peer in enumerate(peers):
                sends.append(pltpu.make_async_remote_copy(
                    src_ref=piece(t, _slot(kind, *peer)), dst_ref=dst_refs[t].at[me],
                    send_sem=send_sems.at[t, p], recv_sem=recv_sems.at[t, p], device_id=peer, device_id_type=MESH))
        for cp in sends:
            cp.start()
        for t in range(n_t):
            for p, peer in enumerate(peers):
                theirs = _slot(kind, *peer)
                pltpu.make_async_remote_copy(
                    src_ref=piece(t, me), dst_ref=dst_refs[t].at[theirs],
                    send_sem=send_sems.at[t, p], recv_sem=recv_sems.at[t, p], device_id=peer, device_id_type=MESH).wait_recv()
        for cp in sends:
            cp.wait_send()
        for cp in local:
            cp.wait()

    any_spec = pl.BlockSpec(memory_space=pl.ANY)
    outs = pl.pallas_call(
        body, name=name, out_shape=out_shapes, in_specs=[any_spec] * n_t, out_specs=[any_spec] * n_t,
        scratch_shapes=[pltpu.SemaphoreType.DMA((n_t, len(flips))), pltpu.SemaphoreType.DMA((n_t, len(flips))),
                        pltpu.SemaphoreType.DMA((n_t,))],
        compiler_params=pltpu.CompilerParams(has_side_effects=True),
    )(*srcs)
    return list(outs)


def exchange_by_sequencer(srcs, indexed, name, collective_id, kind="chips"):
    flips = _PEER_FLIPS[kind]
    n_t, n_p = len(srcs), len(flips)
    src_refs = [jax.new_ref(s, memory_space=pltpu.MemorySpace.HBM) for s in srcs]
    land_refs = [jax.empty_ref(jax.ShapeDtypeStruct(s.shape if indexed else (n_p + 1,) + s.shape, s.dtype),
                               memory_space=pltpu.MemorySpace.HBM) for s in srcs]

    @pl.kernel(mesh=plsc.ScalarSubcoreMesh(axis_name="sequencer", num_cores=1), name=name,
               scratch_types=(pltpu.SemaphoreType.DMA((n_t, n_p)), pltpu.SemaphoreType.DMA((n_t, n_p)), pltpu.SemaphoreType.DMA((n_t,))),
               compiler_params=pltpu.CompilerParams(collective_id=collective_id))
    def launch(send_sems, recv_sems, local_sems):
        x, y, c = lax.axis_index("x"), lax.axis_index("y"), lax.axis_index("c")
        peers = [(1 - x if fx else x, 1 - y if fy else y, 1 - c if fc else c) for fx, fy, fc in flips]
        barrier = pltpu.get_barrier_semaphore()
        for peer in peers:
            pl.semaphore_signal(barrier, inc=1, device_id=peer, device_id_type=MESH)
        pl.semaphore_wait(barrier, n_p)
        me = _slot(kind, x, y, c)

        def piece(t, k):
            return src_refs[t].at[k] if indexed else src_refs[t]

        local = [pltpu.make_async_copy(piece(t, me), land_refs[t].at[me], local_sems.at[t]) for t in range(n_t)]
        for cp in local:
            cp.start()
        sends = []
        for t in range(n_t):
            for p, peer in enumerate(peers):
                sends.append(pltpu.make_async_remote_copy(
                    src_ref=piece(t, _slot(kind, *peer)), dst_ref=land_refs[t].at[me],
                    send_sem=send_sems.at[t, p], recv_sem=recv_sems.at[t, p], device_id=peer, device_id_type=MESH))
        for cp in sends:
            cp.start()
        for t in range(n_t):
            for p, peer in enumerate(peers):
                pltpu.make_async_remote_copy(
                    src_ref=piece(t, me), dst_ref=land_refs[t].at[_slot(kind, *peer)],
                    send_sem=send_sems.at[t, p], recv_sem=recv_sems.at[t, p], device_id=peer, device_id_type=MESH).wait_recv()
        for cp in sends:
            cp.wait_send()
        for cp in local:
            cp.wait()

    launch()
    return [r[...] for r in land_refs]


MATMUL_VMEM_BUDGET = 44 * 1024 * 1024
MATMUL_STEP_BYTES = 1024 * 1024


def _divisors(n, most):
    return [t for t in range(LANES, min(n, most) + 1, LANES) if n % t == 0] or [n]


def _matmul_tiles(n_i, n_j, n_r, a_bytes, b_bytes, out_bytes):
    best = None
    for tr in _divisors(n_r, n_r):
        steps = n_r // tr
        for ti in _divisors(n_i, 1408):
            for tj in _divisors(n_j, 1664):
                vmem = 2 * (ti * tr * a_bytes + tr * tj * b_bytes + ti * tj * out_bytes) + (ti * tj * 4 if steps > 1 else 0)
                if vmem > MATMUL_VMEM_BUDGET or ti < min(n_i, 2 * LANES) or tj < min(n_j, 4 * LANES):
                    continue
                size_a, size_b = n_i * n_r * a_bytes, n_r * n_j * b_bytes
                if steps == 1:
                    moved = min(size_a + (n_i // ti) * size_b, size_b + (n_j // tj) * size_a)
                else:
                    moved = (n_j // tj) * size_a + (n_i // ti) * size_b + (steps - 1) * n_i * n_j * 8
                moved += n_i * n_j * out_bytes + (n_i // ti) * (n_j // tj) * steps * MATMUL_STEP_BYTES
                if best is None or moved < best[0]:
                    best = (moved, ti, tj, tr)
    return best[1:]


def matmul(a, b, mode, out_dtype, name):
    if mode == "nn":
        (n_i, n_r), (_, n_j) = a.shape, b.shape
    elif mode == "nt":
        (n_i, n_r), (n_j, _) = a.shape, b.shape
    else:
        (n_r, n_i), (_, n_j) = a.shape, b.shape
    ti, tj, tr = _matmul_tiles(n_i, n_j, n_r, a.dtype.itemsize, b.dtype.itemsize, jnp.dtype(out_dtype).itemsize)
    steps = n_r // tr
    bytes_a, bytes_b = a.size * a.dtype.itemsize, b.size * b.dtype.itemsize
    j_outer = steps == 1 and bytes_b + (n_j // tj) * bytes_a < bytes_a + (n_i // ti) * bytes_b

    def at(fn):
        return (lambda j, i, r: fn(i, j, r)) if j_outer else fn

    a_spec = {"nn": pl.BlockSpec((ti, tr), at(lambda i, j, r: (i, r))), "nt": pl.BlockSpec((ti, tr), at(lambda i, j, r: (i, r))),
              "tn": pl.BlockSpec((tr, ti), at(lambda i, j, r: (r, i)))}[mode]
    b_spec = {"nn": pl.BlockSpec((tr, tj), at(lambda i, j, r: (r, j))), "nt": pl.BlockSpec((tj, tr), at(lambda i, j, r: (j, r))),
              "tn": pl.BlockSpec((tr, tj), at(lambda i, j, r: (r, j)))}[mode]
    dims = {"nn": (((1,), (0,)), ((), ())), "nt": (((1,), (1,)), ((), ())), "tn": (((0,), (0,)), ((), ()))}[mode]

    def body(a_ref, b_ref, o_ref, *scratch):
        prod = lax.dot_general(a_ref[...].astype(BF16), b_ref[...].astype(BF16), dims, preferred_element_type=F32)
        if steps == 1:
            o_ref[...] = prod.astype(out_dtype)
        else:
            acc = scratch[0]
            r = pl.program_id(2)

            @pl.when(r == 0)
            def _():
                acc[...] = prod

            @pl.when(r > 0)
            def _():
                acc[...] += prod

            @pl.when(r == steps - 1)
            def _():
                o_ref[...] = acc[...].astype(out_dtype)

    grid = (n_j // tj, n_i // ti, steps) if j_outer else (n_i // ti, n_j // tj, steps)
    return pl.pallas_call(
        body, name=name, grid=grid, in_specs=[a_spec, b_spec],
        out_specs=pl.BlockSpec((ti, tj), at(lambda i, j, r: (i, j))), out_shape=jax.ShapeDtypeStruct((n_i, n_j), out_dtype),
        scratch_shapes=[pltpu.VMEM((ti, tj), F32)] if steps > 1 else [],
        compiler_params=_params("arbitrary", "arbitrary", "arbitrary"),
    )(a, b)


def matmul_swiglu(h, w_up, name):
    n_t, d = h.shape
    f = w_up.shape[1] // 2
    ti, tj = _pick(n_t, 768), _pick(f, 1408)
    n_j = f // tj

    def body(h_ref, wg_ref, wu_ref, g_ref, u_ref, act_ref):
        hv = h_ref[...]
        g = jnp.dot(hv, wg_ref[...], preferred_element_type=F32)
        u = jnp.dot(hv, wu_ref[...], preferred_element_type=F32)
        g_ref[...] = g.astype(BF16)
        u_ref[...] = u.astype(BF16)
        act_ref[...] = (_silu_and_grad(g)[0] * u).astype(BF16)

    out = pl.BlockSpec((ti, tj), lambda j, i: (i, j))
    shape = jax.ShapeDtypeStruct((n_t, f), BF16)
    return pl.pallas_call(
        body, name=name, grid=(n_j, n_t // ti),
        in_specs=[pl.BlockSpec((ti, d), lambda j, i: (i, 0)), pl.BlockSpec((d, tj), lambda j, i: (0, j)),
                  pl.BlockSpec((d, tj), lambda j, i: (0, j + n_j))],
        out_specs=[out, out, out], out_shape=[shape, shape, shape], compiler_params=_params("arbitrary", "arbitrary"),
    )(h, w_up, w_up)


def matmul_swiglu_bwd(df, w_down, g, u, name):
    n_t, d = df.shape
    f = w_down.shape[0]
    ti, tj = _pick(n_t, 768), _pick(f, 1408)

    def body(df_ref, w_ref, g_ref, u_ref, dg_ref, du_ref):
        dact = lax.dot_general(df_ref[...], w_ref[...], (((1,), (1,)), ((), ())), preferred_element_type=F32)
        act, dact_dg = _silu_and_grad(g_ref[...].astype(F32))
        dg_ref[...] = (dact * u_ref[...].astype(F32) * dact_dg).astype(BF16)
        du_ref[...] = (dact * act).astype(BF16)

    tile = pl.BlockSpec((ti, tj), lambda j, i: (i, j))
    shape = jax.ShapeDtypeStruct((n_t, f), BF16)
    return pl.pallas_call(
        body, name=name, grid=(f // tj, n_t // ti),
        in_specs=[pl.BlockSpec((ti, d), lambda j, i: (i, 0)), pl.BlockSpec((tj, d), lambda j, i: (j, 0)), tile, tile],
        out_specs=[tile, tile], out_shape=[shape, shape], compiler_params=_params("arbitrary", "arbitrary"))(df, w_down, g, u)


def matmul_groups_nn(a, weights, biases, name):
    n_g = len(weights)
    n_t, d = a.shape
    ti = _pick(n_t, 768)

    def body(*refs):
        av = refs[0][...]
        for w_ref, b_ref, o_ref in zip(refs[1:1 + n_g], refs[1 + n_g:1 + 2 * n_g], refs[1 + 2 * n_g:]):
            o_ref[...] = (jnp.dot(av, w_ref[...], preferred_element_type=F32) + b_ref[...]).astype(BF16)

    widths = [w.shape[1] for w in weights]
    once = pl.Buffered(1)
    return pl.pallas_call(
        body, name=name, grid=(n_t // ti,),
        in_specs=[pl.BlockSpec((ti, d), lambda i: (i, 0))] + [pl.BlockSpec((d, w), lambda i: (0, 0), pipeline_mode=once) for w in widths]
        + [pl.BlockSpec((1, w), lambda i: (0, 0), pipeline_mode=once) for w in widths],
        out_specs=[pl.BlockSpec((ti, w), lambda i: (i, 0)) for w in widths],
        out_shape=[jax.ShapeDtypeStruct((n_t, w), BF16) for w in widths], compiler_params=_params("arbitrary"))(a, *weights, *biases)


def matmul_groups_nt(parts, weights, name):
    n_g = len(parts)
    n_t, d = parts[0].shape[0], weights[0].shape[0]
    ti = _pick(n_t, 768)
    dims = (((1,), (1,)), ((), ()))

    def body(*refs):
        a_refs, b_refs, o_ref, sum_refs = refs[:n_g], refs[n_g:2 * n_g], refs[2 * n_g], refs[2 * n_g + 1:]
        i = pl.program_id(0)
        total = None
        for a_ref, b_ref, s_ref in zip(a_refs, b_refs, sum_refs):
            av = a_ref[...]
            prod = lax.dot_general(av, b_ref[...], dims, preferred_element_type=F32)
            total = prod if total is None else total + prod
            _accumulate(s_ref, _colsum(av.astype(F32)), i == 0)
        o_ref[...] = total

    widths = [p.shape[1] for p in parts]
    outs = pl.pallas_call(
        body, name=name, grid=(n_t // ti,),
        in_specs=[pl.BlockSpec((ti, w), lambda i: (i, 0)) for w in widths]
        + [pl.BlockSpec((d, w), lambda i: (0, 0), pipeline_mode=pl.Buffered(1)) for w in widths],
        out_specs=[pl.BlockSpec((ti, d), lambda i: (i, 0))] + [pl.BlockSpec((1, w), lambda i: (0, 0)) for w in widths],
        out_shape=[jax.ShapeDtypeStruct((n_t, d), F32)] + [jax.ShapeDtypeStruct((1, w), F32) for w in widths],
        compiler_params=_params("arbitrary"))(*parts, *weights)
    return outs[0], list(outs[1:])


def matmul_pair_nt(a1, a2, b, name):
    n_t, f = a1.shape
    d = b.shape[0]
    ti = _pick(n_t, 768)
    dims = (((1,), (1,)), ((), ()))

    def body(a1_ref, a2_ref, b1_ref, b2_ref, o_ref):
        o_ref[...] = (lax.dot_general(a1_ref[...], b1_ref[...], dims, preferred_element_type=F32)
                      + lax.dot_general(a2_ref[...], b2_ref[...], dims, preferred_element_type=F32))

    rows_in = pl.BlockSpec((ti, f), lambda i: (i, 0))
    return pl.pallas_call(
        body, name=name, grid=(n_t // ti,),
        in_specs=[rows_in, rows_in, pl.BlockSpec((d, f), lambda i: (0, 0)), pl.BlockSpec((d, f), lambda i: (0, 1))],
        out_specs=pl.BlockSpec((ti, d), lambda i: (i, 0)), out_shape=jax.ShapeDtypeStruct((n_t, d), F32),
        compiler_params=_params("arbitrary"))(a1, a2, b, b)


def norm_mod(rows, x, g, mv, shift_k, scale_k, name, resid=None):
    d = rows.d
    changed = resid is not None

    def body(*refs):
        refs = list(refs)
        x_ref, g_ref, mv_ref = refs[:3]
        rest = refs[3:]
        xv = x_ref[...]
        if resid is not None:
            xv = xv + resid[2] * mv_ref[resid[1]:resid[1] + 1, :] * rest.pop(0)[...]
        if changed:
            rest.pop(0)[...] = xv
        r = lax.rsqrt(jnp.mean(xv * xv, axis=-1, keepdims=True) + EPS)
        h = (xv * r) * g_ref[...]
        if shift_k is not None:
            h = h * (1.0 + mv_ref[scale_k:scale_k + 1, :]) + mv_ref[shift_k:shift_k + 1, :]
        rest.pop(0)[...] = h.astype(BF16)

    ins = [x, g, mv] + ([resid[0]] if resid is not None else [])
    in_specs = [rows.row(d), rows.vec(d), rows.mod()] + [rows.row(d)] * (len(ins) - 3)
    out_shape = ([jax.ShapeDtypeStruct((rows.t_all, d), F32)] if changed else []) + [jax.ShapeDtypeStruct((rows.t_all, d), BF16)]
    outs = rows.call(body, name, in_specs, [rows.row(d)] * len(out_shape), out_shape)(*ins)
    return (outs[0], outs[1]) if changed else (None, outs[0])


def embed_norm_mod(rows, x_lat, x_ctx, row_tab, col_tab, g, mv, shift_k, scale_k, name):
    d = rows.d
    half = d // 2
    per_tile = rows.tm // GRID_W
    assert rows.tm % GRID_W == 0 and per_tile <= SUBLANES

    def body(xl_ref, xc_ref, rt_ref, ct_ref, g_ref, mv_ref, x0_ref, h_ref):
        i = pl.program_id(0)
        by_row = jnp.concatenate([jnp.broadcast_to(rt_ref[r:r + 1, :], (GRID_W, half)) for r in range(per_tile)], axis=0)
        by_col = jnp.concatenate([ct_ref[...]] * per_tile, axis=0)
        xv = jnp.where(i < rows.n_lat, xl_ref[...] + jnp.concatenate([by_row, by_col], axis=1), xc_ref[...])
        x0_ref[...] = xv
        r = lax.rsqrt(jnp.mean(xv * xv, axis=-1, keepdims=True) + EPS)
        h = (xv * r) * g_ref[...]
        h_ref[...] = (h * (1.0 + mv_ref[scale_k:scale_k + 1, :]) + mv_ref[shift_k:shift_k + 1, :]).astype(BF16)

    last = rows.n_lat - 1
    in_specs = [pl.BlockSpec((rows.tm, d), lambda i: (jnp.minimum(i, last), 0)),
                pl.BlockSpec((rows.tm, d), lambda i: (jnp.maximum(i - rows.n_lat, 0), 0)),
                pl.BlockSpec((SUBLANES, half), lambda i: (jnp.minimum(i, last), 0)), rows.full(col_tab.shape), rows.vec(d), rows.mod()]
    return rows.call(body, name, in_specs, [rows.row(d), rows.row(d)],
                     [jax.ShapeDtypeStruct((rows.t_all, d), F32), jax.ShapeDtypeStruct((rows.t_all, d), BF16)])(
                         x_lat, x_ctx, row_tab, col_tab, g, mv)


def norm_mod_bwd(rows, x, dh, dxn, g, mv, shift_k, scale_k, name, latent_only=False, then_gate=None):
    d = rows.d
    n_dx = rows.n_lat if latent_only else rows.n_all

    def body(*refs):
        x_ref, dh_ref, dxn_ref, g_ref, mv_ref = refs[:5]
        dx_ref, dg_ref, dsh_ref, dsc_ref = refs[-4:] if then_gate is None else refs[6:10]
        i = pl.program_id(0)
        xv, dhv, gv = x_ref[...], dh_ref[...], g_ref[...]
        r = lax.rsqrt(jnp.mean(xv * xv, axis=-1, keepdims=True) + EPS)
        n = xv * r
        dy = dhv * (1.0 + mv_ref[scale_k:scale_k + 1, :])
        dn = dy * gv
        dx = dxn_ref[...] + r * (dn - n * jnp.mean(dn * n, axis=-1, keepdims=True))

        @pl.when(i < n_dx)
        def _():
            dx_ref[...] = dx

        _accumulate(dg_ref, _colsum(dy * n), i == 0)
        _accumulate(dsh_ref, _colsum(dhv), rows.seg_first(i))
        _accumulate(dsc_ref, _colsum(dhv * (n * gv)), rows.seg_first(i))
        if then_gate is not None:
            _gate_bwd(rows, i, dx, refs[5], mv_ref, then_gate[1], then_gate[2], refs[10], refs[11])

    dx_spec = pl.BlockSpec((rows.tm, d), lambda i: (jnp.minimum(i, n_dx - 1), 0))
    ins = [x, dh, dxn, g, mv] + ([then_gate[0]] if then_gate is not None else [])
    in_specs = [rows.row(d), rows.row(d), rows.row(d), rows.vec(d), rows.mod()] + ([rows.row(d)] if then_gate is not None else [])
    out_specs = [dx_spec, rows.vec(d), rows.segsum(), rows.segsum()]
    out_shape = [jax.ShapeDtypeStruct((n_dx * rows.tm, d), F32), jax.ShapeDtypeStruct((1, d), F32), rows.segsum_shape(), rows.segsum_shape()]
    if then_gate is not None:
        out_specs += [rows.row(d), rows.segsum()]
        out_shape += [jax.ShapeDtypeStruct((rows.t_all, d), BF16), rows.segsum_shape()]
    return rows.call(body, name, in_specs, out_specs, out_shape)(*ins)


def _gate_bwd(rows, i, dx, f_ref, mv_ref, gate_k, scale, df_ref, dgate_ref):
    df_ref[...] = (scale * mv_ref[gate_k:gate_k + 1, :] * dx).astype(BF16)
    _accumulate(dgate_ref, scale * _colsum(dx * f_ref[...]), rows.seg_first(i))


def glu(rows, pvg, name):
    w = pvg.shape[1] // 2

    def body(v_ref, t_ref, o_ref):
        for cs in _chunks(w):
            o_ref[:, cs] = v_ref[:, cs].astype(F32) * _sigmoid(t_ref[:, cs].astype(F32))

    return rows.call(body, name, [rows.row(w, 0), rows.row(w, 1)], rows.row(w), jax.ShapeDtypeStruct((rows.t_all, w), F32))(pvg, pvg)


def glu_bwd(rows, pvg, du, name):
    w = pvg.shape[1] // 2

    def body(v_ref, t_ref, du_ref, o_ref):
        for cs in _chunks(w):
            s = _sigmoid(t_ref[:, cs].astype(F32))
            duv = du_ref[:, cs]
            o_ref[:, cs] = (duv * s).astype(BF16)
            o_ref[:, slice(w + cs.start, w + cs.stop)] = (duv * v_ref[:, cs].astype(F32) * s * (1.0 - s)).astype(BF16)

    return rows.call(body, name, [rows.row(w, 0), rows.row(w, 1), rows.row(w)], rows.row(2 * w),
                     jax.ShapeDtypeStruct((rows.t_all, 2 * w), BF16))(pvg, pvg, du)


def ln_silu(rows, u, g, b, name):
    w = u.shape[1]

    def body(u_ref, g_ref, b_ref, o_ref):
        uv = u_ref[...]
        xc = uv - jnp.mean(uv, axis=-1, keepdims=True)
        n = xc * lax.rsqrt(jnp.mean(xc * xc, axis=-1, keepdims=True) + EPS)
        o_ref[...] = _silu_and_grad(n * g_ref[...] + b_ref[...])[0].astype(BF16)

    return rows.call(body, name, [rows.row(w), rows.vec(w), rows.vec(w)], rows.row(w),
                     jax.ShapeDtypeStruct((rows.t_all, w), BF16))(u, g, b)


def ln_silu_bwd(rows, u, ds, g, b, name):
    w = u.shape[1]

    def body(u_ref, ds_ref, g_ref, b_ref, du_ref, dg_ref, db_ref):
        i = pl.program_id(0)
        uv, gv = u_ref[...], g_ref[...]
        xc = uv - jnp.mean(uv, axis=-1, keepdims=True)
        r = lax.rsqrt(jnp.mean(xc * xc, axis=-1, keepdims=True) + EPS)
        n = xc * r
        dy = ds_ref[...].astype(F32) * _silu_and_grad(n * gv + b_ref[...])[1]
        dn = dy * gv
        du_ref[...] = r * (dn - jnp.mean(dn, axis=-1, keepdims=True) - n * jnp.mean(dn * n, axis=-1, keepdims=True))
        _accumulate(dg_ref, _colsum(dy * n), i == 0)
        _accumulate(db_ref, _colsum(dy), i == 0)

    return rows.call(body, name, [rows.row(w), rows.row(w), rows.vec(w), rows.vec(w)], [rows.row(w), rows.vec(w), rows.vec(w)],
                     [jax.ShapeDtypeStruct((rows.t_all, w), F32), jax.ShapeDtypeStruct((1, w), F32),
                      jax.ShapeDtypeStruct((1, w), F32)])(u, ds, g, b)


def lru_merge(rows, h, gl, name):
    w = gl.shape[1]

    def body(h_ref, gl_ref, o_ref):
        for cs in _chunks(w):
            h_sum = h_ref[0, :, cs].astype(F32) + h_ref[1, :, cs].astype(F32)
            o_ref[:, cs] = (h_sum * _gelu_and_grad(gl_ref[:, cs].astype(F32))[0]).astype(BF16)

    return rows.call(body, name, [pl.BlockSpec((2, rows.tm, w), lambda i: (0, i, 0)), rows.row(w)], rows.row(w),
                     jax.ShapeDtypeStruct((rows.t_all, w), BF16))(h, gl)


def lru_merge_bwd(rows, h, gl, dy, name):
    w = gl.shape[1]

    def body(h_ref, gl_ref, dy_ref, dh_ref, dgl_ref):
        for cs in _chunks(w):
            act, dact = _gelu_and_grad(gl_ref[:, cs].astype(F32))
            dyv = dy_ref[:, cs].astype(F32)
            dh_ref[:, cs] = (dyv * act).astype(BF16)
            dgl_ref[:, cs] = (dyv * (h_ref[0, :, cs].astype(F32) + h_ref[1, :, cs].astype(F32)) * dact).astype(BF16)

    return rows.call(body, name, [pl.BlockSpec((2, rows.tm, w), lambda i: (0, i, 0)), rows.row(w), rows.row(w)],
                     [rows.row(w), rows.row(w)],
                     [jax.ShapeDtypeStruct((rows.t_all, w), BF16), jax.ShapeDtypeStruct((rows.t_all, w), BF16)])(h, gl, dy)


def z_merge(rows, pgc, yc, yl, name):
    d = rows.d

    def body(gc_ref, gr_ref, yc_ref, yl_ref, o_ref):
        for cs in _chunks(d):
            o_ref[:, cs] = (_sigmoid(gc_ref[:, cs].astype(F32)) * yc_ref[:, cs].astype(F32)
                            + _sigmoid(gr_ref[:, cs].astype(F32)) * yl_ref[:, cs].astype(F32)).astype(BF16)

    return rows.call(body, name, [rows.row(d, 0), rows.row(d, 1), rows.row(d), rows.row(d)], rows.row(d),
                     jax.ShapeDtypeStruct((rows.t_all, d), BF16))(pgc, pgc, yc, yl)


def z_merge_bwd(rows, pgc, yc, yl, dz, name):
    d = rows.d

    def body(gc_ref, gr_ref, yc_ref, yl_ref, dz_ref, dyc_ref, dyl_ref, dp_ref):
        for cs in _chunks(d):
            sc, sr = _sigmoid(gc_ref[:, cs].astype(F32)), _sigmoid(gr_ref[:, cs].astype(F32))
            dzv = dz_ref[:, cs].astype(F32)
            dyc_ref[:, cs] = (dzv * sc).astype(BF16)
            dyl_ref[:, cs] = (dzv * sr).astype(BF16)
            dp_ref[:, cs] = (dzv * yc_ref[:, cs].astype(F32) * sc * (1.0 - sc)).astype(BF16)
            dp_ref[:, slice(d + cs.start, d + cs.stop)] = (dzv * yl_ref[:, cs].astype(F32) * sr * (1.0 - sr)).astype(BF16)

    return rows.call(body, name, [rows.row(d, 0), rows.row(d, 1), rows.row(d), rows.row(d), rows.row(d)],
                     [rows.row(d), rows.row(d), rows.row(2 * d)],
                     [jax.ShapeDtypeStruct((rows.t_all, d), BF16), jax.ShapeDtypeStruct((rows.t_all, d), BF16),
                      jax.ShapeDtypeStruct((rows.t_all, 2 * d), BF16)])(pgc, pgc, yc, yl, dz)


def loss_head(rows, x, f, mv, g, target, name):
    d = rows.d

    def body(x_ref, f_ref, mv_ref, g_ref, t_ref, loss_ref, dx_ref, dg_ref, df_ref, dgate_ref):
        i = pl.program_id(0)
        valid = jnp.where(i < rows.n_lat, 1.0, 0.0)
        xv = x_ref[...] + 0.5 * mv_ref[8:9, :] * f_ref[...]
        gv = g_ref[...]
        r = lax.rsqrt(jnp.mean(xv * xv, axis=-1, keepdims=True) + EPS)
        n = xv * r
        err = (n * gv - t_ref[...]) * valid
        part = 0.5 * jnp.sum(jnp.mean(err * err, axis=-1, keepdims=True), axis=0, keepdims=True)
        _accumulate(loss_ref, jnp.broadcast_to(part, (1, LANES)), i == 0)
        dy = err * (1.0 / d)
        dn = dy * gv
        dx = r * (dn - n * jnp.mean(dn * n, axis=-1, keepdims=True))
        dx_ref[...] = dx
        _accumulate(dg_ref, _colsum(dy * n), i == 0)
        _gate_bwd(rows, i, dx, f_ref, mv_ref, 8, 0.5, df_ref, dgate_ref)

    target_spec = pl.BlockSpec((rows.tm, d), lambda i: (jnp.minimum(i, rows.n_lat - 1), 0))
    return rows.call(body, name, [rows.row(d), rows.row(d), rows.mod(), rows.vec(d), target_spec],
                     [rows.vec(LANES), rows.row(d), rows.vec(d), rows.row(d), rows.segsum()],
                     [jax.ShapeDtypeStruct((1, LANES), F32), jax.ShapeDtypeStruct((rows.t_all, d), F32),
                      jax.ShapeDtypeStruct((1, d), F32), jax.ShapeDtypeStruct((rows.t_all, d), BF16), rows.segsum_shape()])(
                          x, f, mv, g, target)


def _extended(rows, i, prev_ref, cur_ref, next_ref, cs):
    prev = jnp.where(rows.seg_first(i), 0.0, prev_ref[:, cs].astype(F32))
    nxt = jnp.where(rows.seg_last(i), 0.0, next_ref[:, cs].astype(F32))
    return jnp.concatenate([prev, cur_ref[:, cs].astype(F32), nxt], axis=0)


class _Shifts:
    def __init__(self, ext, tm):
        self.ext, self.tm, self.rolled = ext, tm, {0: ext}

    def at(self, offset):
        residue = offset % SUBLANES
        if residue not in self.rolled:
            self.rolled[residue] = pltpu.roll(self.ext, self.ext.shape[0] - residue, 0)
        start = HALO + offset - residue
        return self.rolled[residue][start:start + self.tm]


def _pad_taps(w):
    k = w.shape[0]
    return jnp.pad(w, ((0, -k % SUBLANES), (0, 0)))


def dwconv(rows, u, w, b, pad_left, out_dtype, name):
    taps, width = w.shape
    wp = _pad_taps(w)

    def body(*refs):
        prev_ref, cur_ref, next_ref, w_ref = refs[:4]
        b_ref = refs[4] if b is not None else None
        o_ref = refs[-1]
        i = pl.program_id(0)
        for cs in _chunks(width, LANES):
            shifts = _Shifts(_extended(rows, i, prev_ref, cur_ref, next_ref, cs), rows.tm)
            acc = jnp.zeros((rows.tm, LANES), F32) if b_ref is None else jnp.broadcast_to(b_ref[:, cs], (rows.tm, LANES))
            for k in range(taps):
                acc = acc + w_ref[k:k + 1, cs] * shifts.at(k - pad_left)
            o_ref[:, cs] = acc.astype(out_dtype)

    ins = [u, u, u, wp] + ([b] if b is not None else [])
    in_specs = [rows.halo(width, -1), rows.row(width), rows.halo(width, 1), rows.full(wp.shape)] + ([rows.vec(width)] if b is not None else [])
    return rows.call(body, name, in_specs, rows.row(width), jax.ShapeDtypeStruct((rows.t_all, width), out_dtype))(*ins)


def dwconv_bwd_w(rows, u, dy, taps, pad_left, name):
    width = u.shape[1]
    taps_p = taps + (-taps % SUBLANES)

    def body(prev_ref, cur_ref, next_ref, dy_ref, dw_ref, db_ref):
        i = pl.program_id(0)
        tap_row = lax.broadcasted_iota(jnp.int32, (taps_p, LANES), 0)
        for cs in _chunks(width, LANES):
            shifts = _Shifts(_extended(rows, i, prev_ref, cur_ref, next_ref, cs), rows.tm)
            dyv = dy_ref[:, cs]
            total = jnp.zeros((taps_p, LANES), F32)
            for k in range(taps):
                total = total + jnp.where(tap_row == k, _colsum(dyv * shifts.at(k - pad_left)), 0.0)
            _accumulate(dw_ref.at[:, cs], total, i == 0)
            _accumulate(db_ref.at[:, cs], _colsum(dyv), i == 0)

    dw, db = rows.call(body, name, [rows.halo(width, -1), rows.row(width), rows.halo(width, 1), rows.row(width)],
                       [rows.full((taps_p, width)), rows.vec(width)],
                       [jax.ShapeDtypeStruct((taps_p, width), F32), jax.ShapeDtypeStruct((1, width), F32)])(u, u, u, dy)
    return dw[:taps], db


def _gate_values(xr, pre, lam, d):
    r = _sigmoid(pre[:, (2 * d) * LANES:(2 * d + 1) * LANES])
    ig = _sigmoid(pre[:, (2 * d + 1) * LANES:(2 * d + 2) * LANES])
    sp = _softplus(-lam[:, d * LANES:(d + 1) * LANES])
    log_a = -LRU_C * r * sp
    return r, ig, sp, jnp.exp(log_a), _neg_expm1(2.0 * log_a)


def lru_gates(rows, xr, wcat, bcat, lam, name):
    n_blk = wcat.shape[0]
    width = xr.shape[1]

    def body(xr_ref, w_ref, b_ref, lam_ref, a_ref, bb_ref):
        for h in range(n_blk):
            cs = slice(h * LANES, (h + 1) * LANES)
            xv = xr_ref[:, cs]
            pre = jnp.dot(xv.astype(BF16), w_ref[h], preferred_element_type=F32) + b_ref[h]
            for d in range(2):
                _, ig, _, a, one_minus_a2 = _gate_values(xv, pre, lam_ref[h], d)
                a_ref[d, :, cs] = a
                bb_ref[d, :, cs] = (jnp.sqrt(one_minus_a2) * (ig * xv)).astype(BF16)

    both = pl.BlockSpec((2, rows.tm, width), lambda i: (0, i, 0))
    return rows.call(body, name, [rows.row(width), rows.full(wcat.shape), rows.full(bcat.shape), rows.full(lam.shape)], [both, both],
                     [jax.ShapeDtypeStruct((2, rows.t_all, width), F32), jax.ShapeDtypeStruct((2, rows.t_all, width), BF16)])(
                         xr, wcat, bcat, lam)


def lru_gates_bwd(rows, xr, wcat, bcat, lam, da, dbb, name):
    n_blk = wcat.shape[0]
    width = xr.shape[1]

    def body(xr_ref, w_ref, b_ref, lam_ref, da_ref, dbb_ref, dxr_ref, dw_ref, db_ref, dlam_ref):
        i = pl.program_id(0)
        for h in range(n_blk):
            cs = slice(h * LANES, (h + 1) * LANES)
            xv = xr_ref[:, cs]
            xb = xv.astype(BF16)
            wv = w_ref[h]
            pre = jnp.dot(xb, wv, preferred_element_type=F32) + b_ref[h]
            dxr = jnp.zeros_like(xv)
            dpre, dlam = [], []
            for d in range(2):
                r, ig, sp, a, one_minus_a2 = _gate_values(xv, pre, lam_ref[h], d)
                inv_q = lax.rsqrt(one_minus_a2)
                dav, dbv = da_ref[d, :, cs].astype(F32), dbb_ref[d, :, cs].astype(F32)
                dbq = dbv * (one_minus_a2 * inv_q) * ig
                dlog_a = dav * a - dbv * (ig * xv) * ((a * a) * inv_q)
                dpre.append(dlog_a * (-LRU_C * sp) * (r - r * r))
                dpre.append(dbq * xv * (1.0 - ig))
                dxr = dxr + dbq
                dlam.append(_colsum(dlog_a * (-LRU_C * r)) * (-_sigmoid(-lam_ref[h][:, d * LANES:(d + 1) * LANES])))
            dpre = jnp.concatenate(dpre, axis=1)
            dpb = dpre.astype(BF16)
            dxr_ref[:, cs] = dxr + lax.dot_general(dpb, wv, (((1,), (1,)), ((), ())), preferred_element_type=F32)
            _accumulate(dw_ref.at[h], lax.dot_general(xb, dpb, (((0,), (0,)), ((), ())), preferred_element_type=F32), i == 0)
            _accumulate(db_ref.at[h], _colsum(dpre), i == 0)
            _accumulate(dlam_ref.at[h], jnp.concatenate(dlam, axis=1), i == 0)

    both = pl.BlockSpec((2, rows.tm, width), lambda i: (0, i, 0))
    return rows.call(
        body, name, [rows.row(width), rows.full(wcat.shape), rows.full(bcat.shape), rows.full(lam.shape), both, both],
        [rows.row(width), rows.full(wcat.shape), rows.full(bcat.shape), rows.full(lam.shape)],
        [jax.ShapeDtypeStruct((rows.t_all, width), F32), jax.ShapeDtypeStruct(wcat.shape, F32),
         jax.ShapeDtypeStruct(bcat.shape, F32), jax.ShapeDtypeStruct(lam.shape, F32)])(xr, wcat, bcat, lam, da, dbb)


def _tile_scan(a, b, reverse):
    n = a.shape[0]
    row = lax.broadcasted_iota(jnp.int32, a.shape, 0)
    k = 1
    while k < n:
        ok = (row < n - k) if reverse else (row >= k)
        shift = n - k if reverse else k
        b = b + a * jnp.where(ok, pltpu.roll(b, shift, 0), 0.0)
        a = a * jnp.where(ok, pltpu.roll(a, shift, 0), 1.0)
        k *= 2
    return a, b


def _chain_scan(a, b, h_in, reverse):
    n_blocks = a.shape[0] // SUBLANES
    out = [None] * n_blocks
    state = h_in
    for j in (range(n_blocks - 1, -1, -1) if reverse else range(n_blocks)):
        rows_j = slice(j * SUBLANES, (j + 1) * SUBLANES)
        cum, h0 = _tile_scan(a[rows_j], b[rows_j], reverse)
        out[j] = h0 + cum * state
        state = out[j][0:1] if reverse else out[j][SUBLANES - 1:SUBLANES]
    return jnp.concatenate(out, axis=0), state


def _neighbour(v, edge, reverse):
    n = v.shape[0]
    row = lax.broadcasted_iota(jnp.int32, v.shape, 0)
    if reverse:
        return jnp.where(row < n - 1, pltpu.roll(v, n - 1, 0), edge)
    return jnp.where(row >= 1, pltpu.roll(v, 1, 0), edge)


def _scan_call(rows, body, name, ins, in_specs, n_out, width, adjoint):
    n_all, n_lat = rows.n_all, rows.n_lat

    def tile(d, s):
        s = n_all - 1 - s if adjoint else s
        return jnp.where(d == 0, (s + n_lat) % n_all, n_all - 1 - s)

    def per_dir(d, s):
        return (d, tile(d, s), 0)

    specs = [pl.BlockSpec((None, rows.tm, width), per_dir) if kind == "dir" else
             pl.BlockSpec((rows.tm, width), lambda d, s: (tile(d, s), 0)) for kind in in_specs]
    shape = jax.ShapeDtypeStruct((2, rows.t_all, width), BF16)
    return pl.pallas_call(
        body, name=name, grid=(2, n_all), in_specs=specs, out_specs=[pl.BlockSpec((None, rows.tm, width), per_dir)] * n_out,
        out_shape=[shape] * n_out, scratch_shapes=[pltpu.VMEM((SUBLANES, width), F32)],
        compiler_params=_params("arbitrary", "arbitrary"))(*ins)


def lru_scan(rows, a, bb, name):
    width = a.shape[2]

    def body(a_ref, bb_ref, h_ref, hp_ref, carry):
        d, s = pl.program_id(0), pl.program_id(1)

        @pl.when(s == 0)
        def _():
            carry[...] = jnp.zeros_like(carry)

        def run(reverse):
            for cs in _chunks(width, LANES):
                h_in = carry[0:1, cs]
                h, carry[0:1, cs] = _chain_scan(a_ref[:, cs], bb_ref[:, cs].astype(F32), h_in, reverse)
                h_ref[:, cs] = h.astype(BF16)
                hp_ref[:, cs] = _neighbour(h, h_in, reverse).astype(BF16)

        pl.when(d == 0)(lambda: run(False))
        pl.when(d == 1)(lambda: run(True))

    return _scan_call(rows, body, name, [a, bb], ["dir", "dir"], 2, width, adjoint=False)


def lru_scan_bwd(rows, dh, a, hp, name):
    width = a.shape[2]
    n = rows.tm

    def body(dh_ref, a_ref, hp_ref, da_ref, dbb_ref, carry):
        d, s = pl.program_id(0), pl.program_id(1)

        @pl.when(s == 0)
        def _():
            carry[...] = jnp.zeros_like(carry)

        def run(reverse):
            for cs in _chunks(width, LANES):
                av = a_ref[:, cs]
                g, _ = _chain_scan(_neighbour(av, 1.0, reverse), dh_ref[:, cs].astype(F32), carry[0:1, cs], reverse)
                da_ref[:, cs] = (g * hp_ref[:, cs].astype(F32)).astype(BF16)
                dbb_ref[:, cs] = g.astype(BF16)
                carry[0:1, cs] = (av * g)[0:1] if reverse else (av * g)[n - 1:n]

        pl.when(d == 0)(lambda: run(True))
        pl.when(d == 1)(lambda: run(False))

    return _scan_call(rows, body, name, [dh, a, hp], ["shared", "dir", "dir"], 2, width, adjoint=True)


def mod_matmul(c9, w_shard, b_shard, name):
    n = w_shard.shape[1]
    tn = _pick(n, 768)

    def body(c_ref, w_ref, b_ref, o_ref):
        act = _silu_and_grad(c_ref[...])[0]
        o_ref[...] = jnp.dot(act, w_ref[...], preferred_element_type=F32, precision=lax.Precision.HIGHEST) + b_ref[...]

    return pl.pallas_call(
        body, name=name, grid=(n // tn,),
        in_specs=[pl.BlockSpec(c9.shape, lambda j: (0, 0)), pl.BlockSpec((w_shard.shape[0], tn), lambda j: (0, j)),
                  pl.BlockSpec((1, tn), lambda j: (0, j))],
        out_specs=pl.BlockSpec((c9.shape[0], tn), lambda j: (0, j)), out_shape=jax.ShapeDtypeStruct((c9.shape[0], n), F32),
        compiler_params=_params("arbitrary"))(c9, w_shard, b_shard)


def mod_matmul_bwd(c9, d9, w_shard, name):
    n = w_shard.shape[1]
    tn = _pick(n, 768)
    steps = n // tn

    def body(c_ref, d_ref, w_ref, gw_ref, gc_ref):
        j = pl.program_id(0)
        act, dact = _silu_and_grad(c_ref[...])
        dv = d_ref[...]
        gw_ref[...] = lax.dot_general(act, dv, (((0,), (0,)), ((), ())), preferred_element_type=F32, precision=lax.Precision.HIGHEST)
        part = lax.dot_general(dv, w_ref[...], (((1,), (1,)), ((), ())), preferred_element_type=F32, precision=lax.Precision.HIGHEST)
        _accumulate(gc_ref, part * dact, j == 0)

    return pl.pallas_call(
        body, name=name, grid=(steps,),
        in_specs=[pl.BlockSpec(c9.shape, lambda j: (0, 0)), pl.BlockSpec((c9.shape[0], tn), lambda j: (0, j)),
                  pl.BlockSpec((w_shard.shape[0], tn), lambda j: (0, j))],
        out_specs=[pl.BlockSpec((w_shard.shape[0], tn), lambda j: (0, j)), pl.BlockSpec(c9.shape, lambda j: (0, 0))],
        out_shape=[jax.ShapeDtypeStruct(w_shard.shape, F32), jax.ShapeDtypeStruct(c9.shape, F32)],
        compiler_params=_params("arbitrary"))(c9, d9, w_shard)


def _row_tile(n_rows, n_cols):
    return _pick(n_rows, max(2 * SUBLANES, (256 * 1024) // n_cols), 2 * SUBLANES)


def sum_parts(parts, name):
    n, n_rows, n_cols = parts.shape
    tr = _row_tile(n_rows, n_cols)

    def body(p_ref, o_ref):
        total = p_ref[0].astype(F32)
        for k in range(1, n):
            total = total + p_ref[k].astype(F32)
        o_ref[...] = total

    return pl.pallas_call(
        body, name=name, grid=(n_rows // tr,), in_specs=[pl.BlockSpec((n, tr, n_cols), lambda i: (0, i, 0))],
        out_specs=pl.BlockSpec((tr, n_cols), lambda i: (i, 0)), out_shape=jax.ShapeDtypeStruct((n_rows, n_cols), F32),
        compiler_params=_params("arbitrary"))(parts)


def _adamw_update(g, w_ref, m_ref, v_ref, g_ref, d_ref, m2_ref, v2_ref):
    m2 = ADAM_B1 * m_ref[...] + (1.0 - ADAM_B1) * g
    v2 = ADAM_B2 * v_ref[...] + (1.0 - ADAM_B2) * (g * g)
    m_hat = m2 / (1.0 - ADAM_B1 ** ADAM_STEP)
    v_hat = v2 / (1.0 - ADAM_B2 ** ADAM_STEP)
    g_ref[...] = g
    d_ref[...] = -ADAM_LR * (m_hat / (jnp.sqrt(v_hat) + ADAM_EPS) + ADAM_WD * w_ref[...])
    m2_ref[...] = m2
    v2_ref[...] = v2


def adamw(parts, w, m, v, name):
    n, n_rows, n_cols = parts.shape
    tr = _row_tile(n_rows, n_cols)

    def body(p_ref, *refs):
        g = p_ref[0].astype(F32)
        for k in range(1, n):
            g = g + p_ref[k].astype(F32)
        _adamw_update(g, *refs)

    blk = pl.BlockSpec((tr, n_cols), lambda i: (i, 0))
    shape = jax.ShapeDtypeStruct((n_rows, n_cols), F32)
    return pl.pallas_call(
        body, name=name, grid=(n_rows // tr,), in_specs=[pl.BlockSpec((n, tr, n_cols), lambda i: (0, i, 0)), blk, blk, blk],
        out_specs=[blk] * 4, out_shape=[shape] * 4, compiler_params=_params("arbitrary"))(parts, w, m, v)


def adamw_pair(parts, w, m, v, name):
    n_parts, n_rows, n_cols = parts.shape
    tr = _row_tile(n_rows, n_cols)
    steps = n_rows // tr

    def body(p_ref, w_ref, m_ref, v_ref, g_ref, d_ref, m2_ref, v2_ref, outbox, inbox, send_sems, recv_sems, credits):
        i = pl.program_id(0)
        sibling = (lax.axis_index("x"), lax.axis_index("y"), 1 - lax.axis_index("c"))

        def copy(slot):
            return pltpu.make_async_remote_copy(src_ref=outbox.at[slot], dst_ref=inbox.at[slot], send_sem=send_sems.at[slot],
                                                recv_sem=recv_sems.at[slot], device_id=sibling, device_id_type=MESH)

        @pl.when(i < steps)
        def _():
            slot = i % 2
            total = p_ref[0].astype(F32)
            for k in range(1, n_parts):
                total = total + p_ref[k].astype(F32)
            outbox[slot] = total

            @pl.when(i >= 2)
            def _():
                pl.semaphore_wait(credits.at[slot], 1)

            copy(slot).start()

        @pl.when(i >= 1)
        def _():
            slot = (i - 1) % 2
            landed = copy(slot)
            landed.wait_recv()
            landed.wait_send()
            _adamw_update(outbox[slot] + inbox[slot], w_ref, m_ref, v_ref, g_ref, d_ref, m2_ref, v2_ref)

            @pl.when(i + 1 < steps)
            def _():
                pl.semaphore_signal(credits.at[slot], inc=1, device_id=sibling, device_id_type=MESH)

    blk = pl.BlockSpec((tr, n_cols), lambda i: (jnp.maximum(i - 1, 0), 0))
    shape = jax.ShapeDtypeStruct((n_rows, n_cols), F32)
    return pl.pallas_call(
        body, name=name, grid=(steps + 1,),
        in_specs=[pl.BlockSpec((n_parts, tr, n_cols), lambda i: (0, jnp.minimum(i, steps - 1), 0)), blk, blk, blk],
        out_specs=[blk] * 4, out_shape=[shape] * 4,
        scratch_shapes=[pltpu.VMEM((2, tr, n_cols), F32), pltpu.VMEM((2, tr, n_cols), F32), pltpu.SemaphoreType.DMA((2,)),
                        pltpu.SemaphoreType.DMA((2,)), pltpu.SemaphoreType.REGULAR((2,))],
        compiler_params=_params("arbitrary"))(parts, w, m, v)


def _pack(arrays, row_multiple=SUBLANES):
    flat = [jnp.pad(a.reshape(-1), (0, -a.size % LANES)) for a in arrays]
    buf = jnp.concatenate(flat)
    buf = jnp.pad(buf, (0, -buf.size % (row_multiple * LANES)))
    return buf.reshape(-1, LANES)


def _unpack(buf, shapes):
    lead = buf.shape[:-2]
    flat = buf.reshape(lead + (-1,))
    out, pos = [], 0
    for shape in shapes:
        size = 1
        for s in shape:
            size *= s
        out.append(flat[..., pos:pos + size].reshape(lead + tuple(shape)))
        pos += size + (-size % LANES)
    return out


def _pos_tables(seq_len, dim, tile_rows):
    grid_rows = seq_len // GRID_W
    per_tile = tile_rows // GRID_W
    q = dim // 4
    omega = 1.0 / (10000.0 ** (jnp.arange(q, dtype=F32) / q))
    er = jnp.arange(grid_rows).astype(F32)[:, None] * omega
    ec = jnp.arange(GRID_W).astype(F32)[:, None] * omega
    by_row = jnp.concatenate([jnp.sin(er), jnp.cos(er)], axis=-1).reshape(grid_rows // per_tile, per_tile, 2 * q)
    by_row = jnp.pad(by_row, ((0, 0), (0, SUBLANES - per_tile), (0, 0))).reshape(-1, 2 * q)
    return by_row, jnp.concatenate([jnp.sin(ec), jnp.cos(ec)], axis=-1)


def _cols_from_shards(g):
    return jnp.transpose(g, (1, 0, 2)).reshape(g.shape[1], -1)


def _shards_from_cols(w, n_shards=4):
    k, n = w.shape
    return jnp.transpose(w.reshape(k, n_shards, n // n_shards), (1, 0, 2))


def kernel(x, c, ctx, c_ctx, w_mod, b_mod, g_n1, w_ffn1_up, w_ffn1_down, g_n2, w_in, b_in, w_dw, b_dw, g_ln, b_ln, w_conf_out, w_lru_conv, b_lru_conv, w_rec_gate, b_rec_gate, w_in_gate, b_in_gate, lru_lambda, w_lru_out, w_out, g_n3, w_ffn2_up, w_ffn2_down, g_final, loss_target, m_c_ctx, m_w_mod, m_b_mod, m_g_n1, m_w_ffn1_up, m_w_ffn1_down, m_g_n2, m_w_in, m_b_in, m_w_dw, m_b_dw, m_g_ln, m_b_ln, m_w_conf_out, m_w_lru_conv, m_b_lru_conv, m_w_rec_gate, m_b_rec_gate, m_w_in_gate, m_b_in_gate, m_lru_lambda, m_w_lru_out, m_w_out, m_g_n3, m_w_ffn2_up, m_w_ffn2_down, m_g_final, v_c_ctx, v_w_mod, v_b_mod, v_g_n1, v_w_ffn1_up, v_w_ffn1_down, v_g_n2, v_w_in, v_b_in, v_w_dw, v_b_dw, v_g_ln, v_b_ln, v_w_conf_out, v_w_lru_conv, v_b_lru_conv, v_w_rec_gate, v_b_rec_gate, v_w_in_gate, v_b_in_gate, v_lru_lambda, v_w_lru_out, v_w_out, v_g_n3, v_w_ffn2_up, v_w_ffn2_down, v_g_final):
    given = dict(locals())
    wt = {n: given[n] for n in W_NAMES}
    mom = {n: given["m_" + n] for n in W_NAMES}
    var = {n: given["v_" + n] for n in W_NAMES}

    def flat2(a):
        if a.ndim == 1:
            return a.reshape(1, -1)
        a = a[0]
        return a if a.ndim == 2 else a.reshape(-1, a.shape[-1])

    t_lat, d = x.shape[1], x.shape[2]
    t_ctx = ctx.shape[1]
    rows = Rows(t_lat, t_ctx, d)
    xi, yi, ci = lax.axis_index("x"), lax.axis_index("y"), lax.axis_index("c")
    me, chip = 4 * xi + 2 * yi + ci, 2 * xi + yi
    lru_w = b_lru_conv.shape[-1]
    n_blk = lru_w // LANES
    taps = w_dw.shape[1]
    lru_taps = w_lru_conv.shape[1]
    ffn = w_ffn1_down.shape[1] * 4

    small_shapes = [(1, d)] + [flat2(wt[n]).shape for n in SMALL_SHARDED]
    small_all = exchange("all", [_pack([c] + [flat2(wt[n]) for n in SMALL_SHARDED])], False, "gather_small")[0]
    small_all = _unpack(small_all, small_shapes)
    c_all = small_all[0][:, 0, :]
    full = {n: jnp.concatenate([a[0], a[2], a[4], a[6]], axis=-1) for n, a in zip(SMALL_SHARDED, small_all[1:])}
    big = COL_SHARDED + ROW_SHARDED
    ffn1_names = ['w_ffn1_up', 'w_ffn1_down']
    ffn2_names = ['w_ffn2_up', 'w_ffn2_down']
    mixer_names = [n for n in big if n not in ffn1_names + ffn2_names + ['w_in']]
    wb = {}

    def take_weights(names, gathered):
        for n, g in zip(names, gathered):
            wb[n] = _cols_from_shards(g) if n in COL_SHARDED else g.reshape(-1, g.shape[-1])

    def shards_after(names, earlier):
        return lax.optimization_barrier(([wt[n][0].astype(BF16) for n in names], earlier))[0]

    c9 = jnp.concatenate([c_all, c_ctx.reshape(1, d), jnp.zeros((7, d), F32)], axis=0)
    mod_cols = mod_matmul(c9, w_mod[0], lax.dynamic_slice_in_dim(b_mod, chip * w_mod.shape[2], w_mod.shape[2], axis=1), "mod_matmul")
    mod_all = exchange("all", [mod_cols], False, "gather_mod")[0]
    mod9 = jnp.concatenate([mod_all[0], mod_all[2], mod_all[4], mod_all[6]], axis=-1)
    mv = jnp.stack([lax.dynamic_index_in_dim(mod9, me, 0, keepdims=False).reshape(N_MOD, d), mod9[8].reshape(N_MOD, d)])

    gathered_up1 = exchange_by_sequencer(shards_after(ffn1_names[:1], mod_all), False, "gather_weights_ffn1_up", 7)
    gathered_down1 = exchange_by_sequencer(shards_after(ffn1_names[1:], mod_all), False, "gather_weights_ffn1_down", 8)
    gathered_in = exchange_by_sequencer(shards_after(['w_in'], mod_all), False, "gather_weights_in", 1)
    gathered_mixer = exchange_by_sequencer(shards_after(mixer_names, mod_all), False, "gather_weights_mixer", 9)
    gathered_ffn2 = exchange_by_sequencer(shards_after(ffn2_names, mod_all), False, "gather_weights_ffn2", 5)

    row_tab, col_tab = _pos_tables(t_lat, d, rows.tm)

    def ffn_fwd(tag, h, names, after_up=None):
        g, u, act = matmul_swiglu(h, wb[names[0]], f"{tag}_up")
        if after_up is not None:
            act = after_up(act)
        return (g, u), act, matmul(act, wb[names[1]], "nn", F32, f"{tag}_down")

    def take_down1(act):
        landed, act = lax.optimization_barrier((gathered_down1, act))
        take_weights(ffn1_names[1:], landed)
        return act

    x0, h1 = embed_norm_mod(rows, x[0], ctx[0], row_tab, col_tab, g_n1, mv, 0, 1, "norm1")
    gathered_up1, h1 = lax.optimization_barrier((gathered_up1, h1))
    take_weights(ffn1_names[:1], gathered_up1)
    gu1, act1, f1 = ffn_fwd("ffn1", h1, ffn1_names, after_up=take_down1)
    x1, h2 = norm_mod(rows, x0, g_n2, mv, 3, 4, "norm2", resid=(f1, 2, 0.5))
    gathered_in, h2 = lax.optimization_barrier((gathered_in, h2))
    take_weights(['w_in'], gathered_in)

    conf_w = w_conf_out.shape[1] * 4
    col_groups = [(0, 2 * conf_w), (2 * conf_w, lru_w), (2 * conf_w + lru_w, lru_w), (2 * conf_w + 2 * lru_w, 2 * d)]
    w_in_groups = [wb["w_in"][:, s:s + n] for s, n in col_groups]
    pvg, ux, gl, pgc = matmul_groups_nn(h2, w_in_groups, [b_in[:, s:s + n] for s, n in col_groups], "in_proj")
    u = glu(rows, pvg, "glu")
    u2 = dwconv(rows, u, full["w_dw"], b_dw, taps // 2, F32, "conf_conv")
    s_act = ln_silu(rows, u2, g_ln, b_ln, "conf_ln")
    gathered_mixer, s_act = lax.optimization_barrier((gathered_mixer, s_act))
    take_weights(mixer_names, gathered_mixer)
    yc = matmul(s_act, wb["w_conf_out"], "nn", BF16, "conf_out")

    xr = dwconv(rows, ux, full["w_lru_conv"], b_lru_conv, LRU_PAD_LEFT, F32, "lru_conv")
    w_rec, w_ing = w_rec_gate[0].astype(BF16), w_in_gate[0].astype(BF16)
    wcat = jnp.concatenate([w_rec[0], w_ing[0], w_rec[1], w_ing[1]], axis=-1)

    def per_block(a):
        return jnp.transpose(a.reshape(2, n_blk, LANES), (1, 0, 2)).reshape(n_blk, 1, 2 * LANES)

    b_rec, b_ing = full["b_rec_gate"].reshape(2, n_blk, LANES), full["b_in_gate"].reshape(2, n_blk, LANES)
    bcat = jnp.concatenate([b_rec[0], b_ing[0], b_rec[1], b_ing[1]], axis=-1).reshape(n_blk, 1, 4 * LANES)
    lam = per_block(full["lru_lambda"])
    a_gate, b_gate = lru_gates(rows, xr, wcat, bcat, lam, "lru_gates")
    h_scan, h_prev = lru_scan(rows, a_gate, b_gate, "lru_scan")
    yl_in = lru_merge(rows, h_scan, gl, "lru_merge")
    yl = matmul(yl_in, wb["w_lru_out"], "nn", BF16, "lru_out")
    z = z_merge(rows, pgc, yc, yl, "z_merge")
    y = matmul(z, wb["w_out"], "nn", F32, "mix_out")

    gathered_ffn2, y = lax.optimization_barrier((gathered_ffn2, y))
    take_weights(ffn2_names, gathered_ffn2)
    x2, h3 = norm_mod(rows, x1, g_n3, mv, 6, 7, "norm3", resid=(y, 5, 1.0))
    gu2, act2, f2 = ffn_fwd("ffn2", h3, ffn2_names)
    loss_part, dx3, dg_final, df2, dgate2 = loss_head(rows, x2, f2, mv, g_final.reshape(1, d), loss_target[0], "loss_head")

    grads = {"g_final": dg_final}
    dmv = [None] * N_MOD

    def ffn_bwd(tag, dxn, df, x_prev, h, gu, act, w_up, w_down, g, ks, collective_id, settle_first=(), **norm_bwd_options):
        k_shift, k_scale = ks
        grads[f"w_{tag}_down"] = matmul(act, df, "tn", BF16, f"{tag}_down_dw")
        if settle_first:
            df = settle(settle_first, df)
        dg_, du_ = matmul_swiglu_bwd(df, w_down, gu[0], gu[1], f"{tag}_down_dx")
        grads[f"w_{tag}_up"] = jnp.concatenate([matmul(h, dg_, "tn", BF16, f"{tag}_up_dw_g"), matmul(h, du_, "tn", BF16, f"{tag}_up_dw_u")], axis=1)
        dg_, du_ = scatter_behind([f"w_{tag}_up", f"w_{tag}_down"], tag, collective_id, (dg_, du_))
        dh = matmul_pair_nt(dg_, du_, w_up, f"{tag}_up_dx")
        dx, dg, dmv[k_shift], dmv[k_scale], *more = norm_mod_bwd(rows, x_prev, dh, dxn, g, mv, k_shift, k_scale, f"{tag}_norm_bwd",
                                                                 **norm_bwd_options)
        return (dx, dg, *more)

    def grad_pieces(names):
        return [(_shards_from_cols(grads[n]) if n in COL_SHARDED else grads[n].reshape(4, -1, grads[n].shape[-1])).astype(BF16)
                for n in names]

    from_chips = {}

    def scatter_behind(names, tag, collective_id, carry):
        ready, carry = lax.optimization_barrier(([grads[n] for n in names], carry))
        grads.update(zip(names, ready))
        from_chips.update(zip(names, exchange_by_sequencer(grad_pieces(names), True, f"scatter_grads_{tag}", collective_id)))
        return carry

    def settle(names, carry):
        landed, carry = lax.optimization_barrier(([from_chips[n] for n in names], carry))
        from_chips.update(zip(names, landed))
        return carry

    dmv[8] = dgate2
    dx2, grads["g_n3"], dy, dmv[5] = ffn_bwd("ffn2", dx3, df2, x2, h3, gu2, act2, wb["w_ffn2_up"], wb["w_ffn2_down"], g_n3, (6, 7), 2,
                                             then_gate=(y, 5, 1.0))

    grads["w_out"] = matmul(z, dy, "tn", BF16, "mix_out_dw")
    dz = matmul(dy, wb["w_out"], "nt", BF16, "mix_out_dx")
    dyc, dyl, dpgc = z_merge_bwd(rows, pgc, yc, yl, dz, "z_merge_bwd")

    grads["w_conf_out"] = matmul(s_act, dyc, "tn", BF16, "conf_out_dw")
    ds_act = matmul(dyc, wb["w_conf_out"], "nt", BF16, "conf_out_dx")
    du2, grads["g_ln"], grads["b_ln"] = ln_silu_bwd(rows, u2, ds_act, g_ln, b_ln, "conf_ln_bwd")
    grads["w_dw"], grads["b_dw"] = dwconv_bwd_w(rows, u, du2, taps, taps // 2, "conf_conv_dw")
    du = dwconv(rows, du2, full["w_dw"][::-1], None, taps - 1 - taps // 2, F32, "conf_conv_dx")
    dpvg = glu_bwd(rows, pvg, du, "glu_bwd")

    grads["w_lru_out"] = matmul(yl_in, dyl, "tn", BF16, "lru_out_dw")
    dyl = scatter_behind(['w_out', 'w_conf_out', 'w_lru_out'], "mixer", 3, settle(ffn2_names, dyl))
    dyl_in = matmul(dyl, wb["w_lru_out"], "nt", BF16, "lru_out_dx")
    dh_sum, dgl = lru_merge_bwd(rows, h_scan, gl, dyl_in, "lru_merge_bwd")
    da_gate, db_gate = lru_scan_bwd(rows, dh_sum, a_gate, h_prev, "lru_scan_bwd")
    dxr, dwcat, dbcat, dlam = lru_gates_bwd(rows, xr, wcat, bcat, lam, da_gate, db_gate, "lru_gates_bwd")
    grads["w_rec_gate"] = jnp.stack([dwcat[:, :, 0:LANES], dwcat[:, :, 2 * LANES:3 * LANES]])
    grads["w_in_gate"] = jnp.stack([dwcat[:, :, LANES:2 * LANES], dwcat[:, :, 3 * LANES:4 * LANES]])
    gate_names = ['w_rec_gate', 'w_in_gate']
    gate_eighths, dxr = lax.optimization_barrier(([grads[n].reshape(8, -1, LANES) for n in gate_names], dxr))
    gate_parts = exchange_by_sequencer(gate_eighths, True, "scatter_gate_grads", 10, kind="all")
    dbcat = dbcat.reshape(n_blk, 4, LANES)
    grads["b_rec_gate"] = jnp.stack([dbcat[:, 0], dbcat[:, 2]]).reshape(2, lru_w)
    grads["b_in_gate"] = jnp.stack([dbcat[:, 1], dbcat[:, 3]]).reshape(2, lru_w)
    grads["lru_lambda"] = jnp.transpose(dlam.reshape(n_blk, 2, LANES), (1, 0, 2)).reshape(2, lru_w)
    grads["w_lru_conv"], grads["b_lru_conv"] = dwconv_bwd_w(rows, ux, dxr, lru_taps, LRU_PAD_LEFT, "lru_conv_dw")
    dux = dwconv(rows, dxr, full["w_lru_conv"][::-1], None, lru_taps - 1 - LRU_PAD_LEFT, BF16, "lru_conv_dx")

    dproj = [dpvg, dux, dgl, dpgc]
    grads["w_in"] = jnp.concatenate([matmul(h2, dp, "tn", BF16, f"in_proj{k}_dw") for k, dp in enumerate(dproj)], axis=1)
    dproj = scatter_behind(['w_in'], "in_proj", 4, settle(['w_out', 'w_conf_out', 'w_lru_out'], dproj))
    dh2, db_in = matmul_groups_nt(dproj, w_in_groups, "in_proj_dx")
    grads["b_in"] = jnp.concatenate(db_in, axis=1)
    dx1, grads["g_n2"], dmv[3], dmv[4], df1, dmv[2] = norm_mod_bwd(rows, x1, dh2, dx2, g_n2, mv, 3, 4, "norm2_bwd", then_gate=(f1, 2, 0.5))
    gate_parts, dx1 = lax.optimization_barrier((gate_parts, dx1))
    gate_sums = exchange_by_sequencer([sum_parts(p, f"sum_gate_grads{k}") for k, p in enumerate(gate_parts)], False,
                                      "gather_gate_grads", 11, kind="all")

    dx0, grads["g_n1"] = ffn_bwd("ffn1", dx1, df1, x0, h1, gu1, act1, wb["w_ffn1_up"], wb["w_ffn1_down"], g_n1, (0, 1), 6,
                                 settle_first=['w_in'], latent_only=True)
    grad_x = dx0[None]

    out = {}
    for n in big:
        parts = from_chips[n].reshape(4, -1, from_chips[n].shape[-1])
        out[n] = adamw_pair(parts, flat2(wt[n]), flat2(mom[n]), flat2(var[n]), f"adamw_{n}")

    dmod = jnp.concatenate(dmv, axis=1)
    small_names = ['g_n1', 'g_n2', 'b_in', 'b_dw', 'g_ln', 'b_ln', 'b_lru_conv', 'g_n3', 'g_final'] + SMALL_SHARDED
    small_list = [loss_part] + [grads[n] for n in small_names] + [dmod[0], dmod[1]]
    small_buf = _pack(small_list, 8 * 2 * SUBLANES)
    small_parts, dmod_all = exchange("all", [small_buf.reshape(8, -1, LANES), _pack([dmod[0]])], [True, False], "scatter_small_grads")
    small_sum = exchange("all", [sum_parts(small_parts, "sum_small_grads")], False, "gather_small_grads")[0]
    small_sum = _unpack(small_sum.reshape(small_buf.shape), [a.shape for a in small_list])
    loss = small_sum[0][0, 0]
    total = dict(zip(small_names, small_sum[1:]))
    total.update({n: s.reshape(grads[n].shape) for n, s in zip(gate_names, gate_sums)})
    dmod_all = _unpack(dmod_all, [dmod[0].shape])[0].reshape(8, N_MOD * d)
    dmc = small_sum[-1].reshape(1, N_MOD * d)
    total["b_mod"] = small_sum[-2].reshape(1, N_MOD * d) + dmc

    d9 = jnp.concatenate([dmod_all, dmc, jnp.zeros((7, N_MOD * d), F32)], axis=0)
    d9_cols = lax.dynamic_slice_in_dim(d9, chip * w_mod.shape[2], w_mod.shape[2], axis=1)
    g_wmod, dc9 = mod_matmul_bwd(c9, d9_cols, w_mod[0], "mod_matmul_bwd")
    dc_all = exchange("all", [dc9[8:16]], False, "gather_dc")[0]
    total["c_ctx"] = sum_parts(jnp.stack([dc_all[0], dc_all[2], dc_all[4], dc_all[6]]), "sum_dc")[0:1]
    out["w_mod"] = adamw(g_wmod[None], w_mod[0], m_w_mod[0], v_w_mod[0], "adamw_w_mod")

    for n in gate_names:
        out[n] = adamw(total[n].reshape(1, -1, LANES), flat2(wt[n]), flat2(mom[n]), flat2(var[n]), f"adamw_{n}")

    small_all_names = [n for n in W_NAMES if n not in out]
    g_local, shapes = [], []
    for n in small_all_names:
        g = total[n].reshape(flat2(wt[n]).shape[:-1] + (-1,)) if n in SMALL_SHARDED else total[n].reshape(flat2(wt[n]).shape)
        if n in SMALL_SHARDED:
            width = wt[n].shape[-1]
            g = lax.dynamic_slice_in_dim(g, chip * width, width, axis=1)
        g_local.append(g)
        shapes.append(g.shape)
    packed = adamw(_pack(g_local)[None], _pack([flat2(wt[n]) for n in small_all_names]),
                   _pack([flat2(mom[n]) for n in small_all_names]), _pack([flat2(var[n]) for n in small_all_names]), "adamw_small")
    for k, n in enumerate(small_all_names):
        out[n] = [_unpack(p, shapes)[k] for p in packed]

    results = [loss, grad_x]
    for k in range(4):
        results += [out[n][k].reshape(wt[n].shape) for n in W_NAMES]
    return tuple(results)
```

```python
import jax
import jax.numpy as jnp
from jax import lax
from jax.experimental import pallas as pl
from jax.experimental.pallas import tpu as pltpu
from jax.experimental.pallas import tpu_sc as plsc

F32 = jnp.float32
BF16 = jnp.bfloat16
MESH = pl.DeviceIdType.MESH

EPS = 1e-6
GRID_W = 64
N_MOD = 9
LRU_C = 8.0
LRU_PAD_LEFT = 2
ADAM_LR, ADAM_B1, ADAM_B2, ADAM_EPS, ADAM_WD, ADAM_STEP = 0.001, 0.9, 0.999, 1e-08, 0.01, 10

LANES = 128
SUBLANES = 8
HALO = 16
VMEM_LIMIT = 56 * 1024 * 1024

W_NAMES = ['c_ctx', 'w_mod', 'b_mod', 'g_n1', 'w_ffn1_up', 'w_ffn1_down', 'g_n2', 'w_in', 'b_in', 'w_dw', 'b_dw',
           'g_ln', 'b_ln', 'w_conf_out', 'w_lru_conv', 'b_lru_conv', 'w_rec_gate', 'b_rec_gate', 'w_in_gate',
           'b_in_gate', 'lru_lambda', 'w_lru_out', 'w_out', 'g_n3', 'w_ffn2_up', 'w_ffn2_down', 'g_final']
COL_SHARDED = ['w_ffn1_up', 'w_in', 'w_ffn2_up']
ROW_SHARDED = ['w_ffn1_down', 'w_conf_out', 'w_lru_out', 'w_out', 'w_ffn2_down']
SMALL_SHARDED = ['w_dw', 'w_lru_conv', 'b_rec_gate', 'b_in_gate', 'lru_lambda']


def _params(*semantics):
    return pltpu.CompilerParams(dimension_semantics=semantics, vmem_limit_bytes=VMEM_LIMIT)


def _pick(n, target, mult=LANES):
    best = None
    for t in range(mult, min(n, target) + 1, mult):
        if n % t == 0:
            best = t
    return best or n


def _chunks(width, target=512):
    w = _pick(width, target)
    return [slice(s, s + w) for s in range(0, width, w)]


def _sigmoid(x):
    return jax.nn.sigmoid(x)


def _silu_and_grad(x):
    s = _sigmoid(x)
    return x * s, s * (1.0 + x * (1.0 - s))


_GELU_K = 0.7978845608028654


def _gelu_and_grad(x):
    t = jnp.tanh(_GELU_K * (x + 0.044715 * (x * x * x)))
    return 0.5 * x * (1.0 + t), 0.5 * (1.0 + t) + 0.5 * x * (1.0 - t * t) * _GELU_K * (1.0 + 3 * 0.044715 * x * x)


def _neg_expm1(z):
    series = -(z * (1.0 + z * (0.5 + z * (1.0 / 6.0))))
    return jnp.where(z > -0.01, series, 1.0 - jnp.exp(z))


def _softplus(x):
    return jnp.maximum(x, 0.0) + jnp.log1p(jnp.exp(-jnp.abs(x)))


def _accumulate(ref, value, first):
    @pl.when(first)
    def _():
        ref[...] = value

    @pl.when(jnp.logical_not(first))
    def _():
        ref[...] += value


def _colsum(x):
    return jnp.sum(x, axis=0, keepdims=True)


class Rows:
    def __init__(self, t_lat, t_ctx, d_model):
        self.tm = _pick(t_ctx, 256, HALO)
        assert t_lat % self.tm == 0 and t_ctx % self.tm == 0
        self.n_lat = t_lat // self.tm
        self.n_all = (t_lat + t_ctx) // self.tm
        self.t_all = t_lat + t_ctx
        self.d = d_model

    def seg(self, i):
        return jnp.where(i >= self.n_lat, 1, 0)

    def seg_first(self, i):
        return jnp.logical_or(i == 0, i == self.n_lat)

    def seg_last(self, i):
        return jnp.logical_or(i == self.n_lat - 1, i == self.n_all - 1)

    def row(self, width, col=0):
        return pl.BlockSpec((self.tm, width), lambda i: (i, col))

    def vec(self, width):
        return pl.BlockSpec((1, width), lambda i: (0, 0))

    def full(self, shape):
        return pl.BlockSpec(shape, lambda i: (0,) * len(shape))

    def mod(self):
        return pl.BlockSpec((None, N_MOD, self.d), lambda i: (self.seg(i), 0, 0))

    def segsum(self):
        return pl.BlockSpec((None, 1, self.d), lambda i: (self.seg(i), 0, 0))

    def segsum_shape(self):
        return jax.ShapeDtypeStruct((2, 1, self.d), F32)

    def halo(self, width, side):
        per = self.tm // HALO
        last = self.t_all // HALO - 1
        if side < 0:
            return pl.BlockSpec((HALO, width), lambda i: (jnp.maximum(i * per - 1, 0), 0))
        return pl.BlockSpec((HALO, width), lambda i: (jnp.minimum((i + 1) * per, last), 0))

    def call(self, body, name, in_specs, out_specs, out_shape, scratch=()):
        return pl.pallas_call(body, name=name, grid=(self.n_all,), in_specs=in_specs, out_specs=out_specs,
                              out_shape=out_shape, scratch_shapes=list(scratch), compiler_params=_params("arbitrary"))


_PEER_FLIPS = {
    "chips": [(1, 0, 0), (0, 1, 0), (1, 1, 0)],
    "all": [(0, 0, 1), (0, 1, 0), (0, 1, 1), (1, 0, 0), (1, 0, 1), (1, 1, 0), (1, 1, 1)],
}


def _slot(kind, x, y, c):
    return {"chips": 2 * x + y, "all": 4 * x + 2 * y + c}[kind]


def exchange(kind, srcs, indexed, name):
    flips = _PEER_FLIPS[kind]
    n_slots = len(flips) + 1
    n_t = len(srcs)
    by_slot = list(indexed) if isinstance(indexed, (list, tuple)) else [indexed] * n_t
    out_shapes = [jax.ShapeDtypeStruct(s.shape if ix else (n_slots,) + s.shape, s.dtype) for s, ix in zip(srcs, by_slot)]

    def body(*refs):
        src_refs, dst_refs = refs[:n_t], refs[n_t:2 * n_t]
        send_sems, recv_sems, local_sems = refs[2 * n_t:]
        x, y, c = lax.axis_index("x"), lax.axis_index("y"), lax.axis_index("c")
        me = _slot(kind, x, y, c)

        def piece(t, k):
            return src_refs[t].at[k] if by_slot[t] else src_refs[t]

        peers = [(1 - x if fx else x, 1 - y if fy else y, 1 - c if fc else c) for fx, fy, fc in flips]
        local = [pltpu.make_async_copy(piece(t, me), dst_refs[t].at[me], local_sems.at[t]) for t in range(n_t)]
        for cp in local:
            cp.start()
        sends = []
        for t in range(n_t):
            for p, peer in enumerate(peers):
                sends.append(pltpu.make_async_remote_copy(
                    src_ref=piece(t, _slot(kind, *peer)), dst_ref=dst_refs[t].at[me],
                    send_sem=send_sems.at[t, p], recv_sem=recv_sems.at[t, p], device_id=peer, device_id_type=MESH))
        for cp in sends:
            cp.start()
        for t in range(n_t):
            for p, peer in enumerate(peers):
                theirs = _slot(kind, *peer)
                pltpu.make_async_remote_copy(
                    src_ref=piece(t, me), dst_ref=dst_refs[t].at[theirs],
                    send_sem=send_sems.at[t, p], recv_sem=recv_sems.at[t, p], device_id=peer, device_id_type=MESH).wait_recv()
        for cp in sends:
            cp.wait_send()
        for cp in local:
            cp.wait()

    any_spec = pl.BlockSpec(memory_space=pl.ANY)
    outs = pl.pallas_call(
        body, name=name, out_shape=out_shapes, in_specs=[any_spec] * n_t, out_specs=[any_spec] * n_t,
        scratch_shapes=[pltpu.SemaphoreType.DMA((n_t, len(flips))), pltpu.SemaphoreType.DMA((n_t, len(flips))),
                        pltpu.SemaphoreType.DMA((n_t,))],
        compiler_params=pltpu.CompilerParams(has_side_effects=True),
    )(*srcs)
    return list(outs)


def exchange_by_sequencer(srcs, indexed, name, collective_id, kind="chips"):
    flips = _PEER_FLIPS[kind]
    n_t, n_p = len(srcs), len(flips)
    src_refs = [jax.new_ref(s, memory_space=pltpu.MemorySpace.HBM) for s in srcs]
    land_refs = [jax.empty_ref(jax.ShapeDtypeStruct(s.shape if indexed else (n_p + 1,) + s.shape, s.dtype),
                               memory_space=pltpu.MemorySpace.HBM) for s in srcs]

    @pl.kernel(mesh=plsc.ScalarSubcoreMesh(axis_name="sequencer", num_cores=1), name=name,
               scratch_types=(pltpu.SemaphoreType.DMA((n_t, n_p)), pltpu.SemaphoreType.DMA((n_t, n_p)), pltpu.SemaphoreType.DMA((n_t,))),
               compiler_params=pltpu.CompilerParams(collective_id=collective_id))
    def launch(send_sems, recv_sems, local_sems):
        x, y, c = lax.axis_index("x"), lax.axis_index("y"), lax.axis_index("c")
        peers = [(1 - x if fx else x, 1 - y if fy else y, 1 - c if fc else c) for fx, fy, fc in flips]
        barrier = pltpu.get_barrier_semaphore()
        for peer in peers:
            pl.semaphore_signal(barrier, inc=1, device_id=peer, device_id_type=MESH)
        pl.semaphore_wait(barrier, n_p)
        me = _slot(kind, x, y, c)

        def piece(t, k):
            return src_refs[t].at[k] if indexed else src_refs[t]

        local = [pltpu.make_async_copy(piece(t, me), land_refs[t].at[me], local_sems.at[t]) for t in range(n_t)]
        for cp in local:
            cp.start()
        sends = []
        for t in range(n_t):
            for p, peer in enumerate(peers):
                sends.append(pltpu.make_async_remote_copy(
                    src_ref=piece(t, _slot(kind, *peer)), dst_ref=land_refs[t].at[me],
                    send_sem=send_sems.at[t, p], recv_sem=recv_sems.at[t, p], device_id=peer, device_id_type=MESH))
        for cp in sends:
            cp.start()
        for t in range(n_t):
            for p, peer in enumerate(peers):
                pltpu.make_async_remote_copy(
                    src_ref=piece(t, me), dst_ref=land_refs[t].at[_slot(kind, *peer)],
                    send_sem=send_sems.at[t, p], recv_sem=recv_sems.at[t, p], device_id=peer, device_id_type=MESH).wait_recv()
        for cp in sends:
            cp.wait_send()
        for cp in local:
            cp.wait()

    launch()
    return [r[...] for r in land_refs]


MATMUL_VMEM_BUDGET = 44 * 1024 * 1024
MATMUL_STEP_BYTES = 1024 * 1024


def _divisors(n, most):
    return [t for t in range(LANES, min(n, most) + 1, LANES) if n % t == 0] or [n]


def _matmul_tiles(n_i, n_j, n_r, a_bytes, b_bytes, out_bytes):
    best = None
    for tr in _divisors(n_r, n_r):
        steps = n_r // tr
        for ti in _divisors(n_i, 1408):
            for tj in _divisors(n_j, 1664):
                vmem = 2 * (ti * tr * a_bytes + tr * tj * b_bytes + ti * tj * out_bytes) + (ti * tj * 4 if steps > 1 else 0)
                if vmem > MATMUL_VMEM_BUDGET or ti < min(n_i, 2 * LANES) or tj < min(n_j, 4 * LANES):
                    continue
                size_a, size_b = n_i * n_r * a_bytes, n_r * n_j * b_bytes
                if steps == 1:
                    moved = min(size_a + (n_i // ti) * size_b, size_b + (n_j // tj) * size_a)
                else:
                    moved = (n_j // tj) * size_a + (n_i // ti) * size_b + (steps - 1) * n_i * n_j * 8
                moved += n_i * n_j * out_bytes + (n_i // ti) * (n_j // tj) * steps * MATMUL_STEP_BYTES
                if best is None or moved < best[0]:
                    best = (moved, ti, tj, tr)
    return best[1:]


def matmul(a, b, mode, out_dtype, name):
    if mode == "nn":
        (n_i, n_r), (_, n_j) = a.shape, b.shape
    elif mode == "nt":
        (n_i, n_r), (n_j, _) = a.shape, b.shape
    else:
        (n_r, n_i), (_, n_j) = a.shape, b.shape
    ti, tj, tr = _matmul_tiles(n_i, n_j, n_r, a.dtype.itemsize, b.dtype.itemsize, jnp.dtype(out_dtype).itemsize)
    steps = n_r // tr
    bytes_a, bytes_b = a.size * a.dtype.itemsize, b.size * b.dtype.itemsize
    j_outer = steps == 1 and bytes_b + (n_j // tj) * bytes_a < bytes_a + (n_i // ti) * bytes_b

    def at(fn):
        return (lambda j, i, r: fn(i, j, r)) if j_outer else fn

    a_spec = {"nn": pl.BlockSpec((ti, tr), at(lambda i, j, r: (i, r))), "nt": pl.BlockSpec((ti, tr), at(lambda i, j, r: (i, r))),
              "tn": pl.BlockSpec((tr, ti), at(lambda i, j, r: (r, i)))}[mode]
    b_spec = {"nn": pl.BlockSpec((tr, tj), at(lambda i, j, r: (r, j))), "nt": pl.BlockSpec((tj, tr), at(lambda i, j, r: (j, r))),
              "tn": pl.BlockSpec((tr, tj), at(lambda i, j, r: (r, j)))}[mode]
    dims = {"nn": (((1,), (0,)), ((), ())), "nt": (((1,), (1,)), ((), ())), "tn": (((0,), (0,)), ((), ()))}[mode]

    def body(a_ref, b_ref, o_ref, *scratch):
        prod = lax.dot_general(a_ref[...].astype(BF16), b_ref[...].astype(BF16), dims, preferred_element_type=F32)
        if steps == 1:
            o_ref[...] = prod.astype(out_dtype)
        else:
            acc = scratch[0]
            r = pl.program_id(2)

            @pl.when(r == 0)
            def _():
                acc[...] = prod

            @pl.when(r > 0)
            def _():
                acc[...] += prod

            @pl.when(r == steps - 1)
            def _():
                o_ref[...] = acc[...].astype(out_dtype)

    grid = (n_j // tj, n_i // ti, steps) if j_outer else (n_i // ti, n_j // tj, steps)
    return pl.pallas_call(
        body, name=name, grid=grid, in_specs=[a_spec, b_spec],
        out_specs=pl.BlockSpec((ti, tj), at(lambda i, j, r: (i, j))), out_shape=jax.ShapeDtypeStruct((n_i, n_j), out_dtype),
        scratch_shapes=[pltpu.VMEM((ti, tj), F32)] if steps > 1 else [],
        compiler_params=_params("arbitrary", "arbitrary", "arbitrary"),
    )(a, b)


def matmul_swiglu(h, w_up, name):
    n_t, d = h.shape
    f = w_up.shape[1] // 2
    ti, tj = _pick(n_t, 768), _pick(f, 1408)
    n_j = f // tj

    def body(h_ref, wg_ref, wu_ref, g_ref, u_ref, act_ref):
        hv = h_ref[...]
        g = jnp.dot(hv, wg_ref[...], preferred_element_type=F32)
        u = jnp.dot(hv, wu_ref[...], preferred_element_type=F32)
        g_ref[...] = g.astype(BF16)
        u_ref[...] = u.astype(BF16)
        act_ref[...] = (_silu_and_grad(g)[0] * u).astype(BF16)

    out = pl.BlockSpec((ti, tj), lambda j, i: (i, j))
    shape = jax.ShapeDtypeStruct((n_t, f), BF16)
    return pl.pallas_call(
        body, name=name, grid=(n_j, n_t // ti),
        in_specs=[pl.BlockSpec((ti, d), lambda j, i: (i, 0)), pl.BlockSpec((d, tj), lambda j, i: (0, j)),
                  pl.BlockSpec((d, tj), lambda j, i: (0, j + n_j))],
        out_specs=[out, out, out], out_shape=[shape, shape, shape], compiler_params=_params("arbitrary", "arbitrary"),
    )(h, w_up, w_up)


def matmul_swiglu_bwd(df, w_down, g, u, name):
    n_t, d = df.shape
    f = w_down.shape[0]
    ti, tj = _pick(n_t, 768), _pick(f, 1408)

    def body(df_ref, w_ref, g_ref, u_ref, dg_ref, du_ref):
        dact = lax.dot_general(df_ref[...], w_ref[...], (((1,), (1,)), ((), ())), preferred_element_type=F32)
        act, dact_dg = _silu_and_grad(g_ref[...].astype(F32))
        dg_ref[...] = (dact * u_ref[...].astype(F32) * dact_dg).astype(BF16)
        du_ref[...] = (dact * act).astype(BF16)

    tile = pl.BlockSpec((ti, tj), lambda j, i: (i, j))
    shape = jax.ShapeDtypeStruct((n_t, f), BF16)
    return pl.pallas_call(
        body, name=name, grid=(f // tj, n_t // ti),
        in_specs=[pl.BlockSpec((ti, d), lambda j, i: (i, 0)), pl.BlockSpec((tj, d), lambda j, i: (j, 0)), tile, tile],
        out_specs=[tile, tile], out_shape=[shape, shape], compiler_params=_params("arbitrary", "arbitrary"))(df, w_down, g, u)


def matmul_groups_nn(a, weights, biases, name):
    n_g = len(weights)
    n_t, d = a.shape
    ti = _pick(n_t, 768)

    def body(*refs):
        av = refs[0][...]
        for w_ref, b_ref, o_ref in zip(refs[1:1 + n_g], refs[1 + n_g:1 + 2 * n_g], refs[1 + 2 * n_g:]):
            o_ref[...] = (jnp.dot(av, w_ref[...], preferred_element_type=F32) + b_ref[...]).astype(BF16)

    widths = [w.shape[1] for w in weights]
    once = pl.Buffered(1)
    return pl.pallas_call(
        body, name=name, grid=(n_t // ti,),
        in_specs=[pl.BlockSpec((ti, d), lambda i: (i, 0))] + [pl.BlockSpec((d, w), lambda i: (0, 0), pipeline_mode=once) for w in widths]
        + [pl.BlockSpec((1, w), lambda i: (0, 0), pipeline_mode=once) for w in widths],
        out_specs=[pl.BlockSpec((ti, w), lambda i: (i, 0)) for w in widths],
        out_shape=[jax.ShapeDtypeStruct((n_t, w), BF16) for w in widths], compiler_params=_params("arbitrary"))(a, *weights, *biases)


def matmul_groups_nt(parts, weights, name):
    n_g = len(parts)
    n_t, d = parts[0].shape[0], weights[0].shape[0]
    ti = _pick(n_t, 768)
    dims = (((1,), (1,)), ((), ()))

    def body(*refs):
        a_refs, b_refs, o_ref, sum_refs = refs[:n_g], refs[n_g:2 * n_g], refs[2 * n_g], refs[2 * n_g + 1:]
        i = pl.program_id(0)
        total = None
        for a_ref, b_ref, s_ref in zip(a_refs, b_refs, sum_refs):
            av = a_ref[...]
            prod = lax.dot_general(av, b_ref[...], dims, preferred_element_type=F32)
            total = prod if total is None else total + prod
            _accumulate(s_ref, _colsum(av.astype(F32)), i == 0)
        o_ref[...] = total

    widths = [p.shape[1] for p in parts]
    outs = pl.pallas_call(
        body, name=name, grid=(n_t // ti,),
        in_specs=[pl.BlockSpec((ti, w), lambda i: (i, 0)) for w in widths]
        + [pl.BlockSpec((d, w), lambda i: (0, 0), pipeline_mode=pl.Buffered(1)) for w in widths],
        out_specs=[pl.BlockSpec((ti, d), lambda i: (i, 0))] + [pl.BlockSpec((1, w), lambda i: (0, 0)) for w in widths],
        out_shape=[jax.ShapeDtypeStruct((n_t, d), F32)] + [jax.ShapeDtypeStruct((1, w), F32) for w in widths],
        compiler_params=_params("arbitrary"))(*parts, *weights)
    return outs[0], list(outs[1:])


def matmul_pair_nt(a1, a2, b, name):
    n_t, f = a1.shape
    d = b.shape[0]
    ti = _pick(n_t, 768)
    dims = (((1,), (1,)), ((), ()))

    def body(a1_ref, a2_ref, b1_ref, b2_ref, o_ref):
        o_ref[...] = (lax.dot_general(a1_ref[...], b1_ref[...], dims, preferred_element_type=F32)
                      + lax.dot_general(a2_ref[...], b2_ref[...], dims, preferred_element_type=F32))

    rows_in = pl.BlockSpec((ti, f), lambda i: (i, 0))
    return pl.pallas_call(
        body, name=name, grid=(n_t // ti,),
        in_specs=[rows_in, rows_in, pl.BlockSpec((d, f), lambda i: (0, 0)), pl.BlockSpec((d, f), lambda i: (0, 1))],
        out_specs=pl.BlockSpec((ti, d), lambda i: (i, 0)), out_shape=jax.ShapeDtypeStruct((n_t, d), F32),
        compiler_params=_params("arbitrary"))(a1, a2, b, b)


def norm_mod(rows, x, g, mv, shift_k, scale_k, name, resid=None):
    d = rows.d
    changed = resid is not None

    def body(*refs):
        refs = list(refs)
        x_ref, g_ref, mv_ref = refs[:3]
        rest = refs[3:]
        xv = x_ref[...]
        if resid is not None:
            xv = xv + resid[2] * mv_ref[resid[1]:resid[1] + 1, :] * rest.pop(0)[...]
        if changed:
            rest.pop(0)[...] = xv
        r = lax.rsqrt(jnp.mean(xv * xv, axis=-1, keepdims=True) + EPS)
        h = (xv * r) * g_ref[...]
        if shift_k is not None:
            h = h * (1.0 + mv_ref[scale_k:scale_k + 1, :]) + mv_ref[shift_k:shift_k + 1, :]
        rest.pop(0)[...] = h.astype(BF16)

    ins = [x, g, mv] + ([resid[0]] if resid is not None else [])
    in_specs = [rows.row(d), rows.vec(d), rows.mod()] + [rows.row(d)] * (len(ins) - 3)
    out_shape = ([jax.ShapeDtypeStruct((rows.t_all, d), F32)] if changed else []) + [jax.ShapeDtypeStruct((rows.t_all, d), BF16)]
    outs = rows.call(body, name, in_specs, [rows.row(d)] * len(out_shape), out_shape)(*ins)
    return (outs[0], outs[1]) if changed else (None, outs[0])


def embed_norm_mod(rows, x_lat, x_ctx, row_tab, col_tab, g, mv, shift_k, scale_k, name):
    d = rows.d
    half = d // 2
    per_tile = rows.tm // GRID_W
    assert rows.tm % GRID_W == 0 and per_tile <= SUBLANES

    def body(xl_ref, xc_ref, rt_ref, ct_ref, g_ref, mv_ref, x0_ref, h_ref):
        i = pl.program_id(0)
        by_row = jnp.concatenate([jnp.broadcast_to(rt_ref[r:r + 1, :], (GRID_W, half)) for r in range(per_tile)], axis=0)
        by_col = jnp.concatenate([ct_ref[...]] * per_tile, axis=0)
        xv = jnp.where(i < rows.n_lat, xl_ref[...] + jnp.concatenate([by_row, by_col], axis=1), xc_ref[...])
        x0_ref[...] = xv
        r = lax.rsqrt(jnp.mean(xv * xv, axis=-1, keepdims=True) + EPS)
        h = (xv * r) * g_ref[...]
        h_ref[...] = (h * (1.0 + mv_ref[scale_k:scale_k + 1, :]) + mv_ref[shift_k:shift_k + 1, :]).astype(BF16)

    last = rows.n_lat - 1
    in_specs = [pl.BlockSpec((rows.tm, d), lambda i: (jnp.minimum(i, last), 0)),
                pl.BlockSpec((rows.tm, d), lambda i: (jnp.maximum(i - rows.n_lat, 0), 0)),
                pl.BlockSpec((SUBLANES, half), lambda i: (jnp.minimum(i, last), 0)), rows.full(col_tab.shape), rows.vec(d), rows.mod()]
    return rows.call(body, name, in_specs, [rows.row(d), rows.row(d)],
                     [jax.ShapeDtypeStruct((rows.t_all, d), F32), jax.ShapeDtypeStruct((rows.t_all, d), BF16)])(
                         x_lat, x_ctx, row_tab, col_tab, g, mv)


def norm_mod_bwd(rows, x, dh, dxn, g, mv, shift_k, scale_k, name, latent_only=False, then_gate=None):
    d = rows.d
    n_dx = rows.n_lat if latent_only else rows.n_all

    def body(*refs):
        x_ref, dh_ref, dxn_ref, g_ref, mv_ref = refs[:5]
        dx_ref, dg_ref, dsh_ref, dsc_ref = refs[-4:] if then_gate is None else refs[6:10]
        i = pl.program_id(0)
        xv, dhv, gv = x_ref[...], dh_ref[...], g_ref[...]
        r = lax.rsqrt(jnp.mean(xv * xv, axis=-1, keepdims=True) + EPS)
        n = xv * r
        dy = dhv * (1.0 + mv_ref[scale_k:scale_k + 1, :])
        dn = dy * gv
        dx = dxn_ref[...] + r * (dn - n * jnp.mean(dn * n, axis=-1, keepdims=True))

        @pl.when(i < n_dx)
        def _():
            dx_ref[...] = dx

        _accumulate(dg_ref, _colsum(dy * n), i == 0)
        _accumulate(dsh_ref, _colsum(dhv), rows.seg_first(i))
        _accumulate(dsc_ref, _colsum(dhv * (n * gv)), rows.seg_first(i))
        if then_gate is not None:
            _gate_bwd(rows, i, dx, refs[5], mv_ref, then_gate[1], then_gate[2], refs[10], refs[11])

    dx_spec = pl.BlockSpec((rows.tm, d), lambda i: (jnp.minimum(i, n_dx - 1), 0))
    ins = [x, dh, dxn, g, mv] + ([then_gate[0]] if then_gate is not None else [])
    in_specs = [rows.row(d), rows.row(d), rows.row(d), rows.vec(d), rows.mod()] + ([rows.row(d)] if then_gate is not None else [])
    out_specs = [dx_spec, rows.vec(d), rows.segsum(), rows.segsum()]
    out_shape = [jax.ShapeDtypeStruct((n_dx * rows.tm, d), F32), jax.ShapeDtypeStruct((1, d), F32), rows.segsum_shape(), rows.segsum_shape()]
    if then_gate is not None:
        out_specs += [rows.row(d), rows.segsum()]
        out_shape += [jax.ShapeDtypeStruct((rows.t_all, d), BF16), rows.segsum_shape()]
    return rows.call(body, name, in_specs, out_specs, out_shape)(*ins)


def _gate_bwd(rows, i, dx, f_ref, mv_ref, gate_k, scale, df_ref, dgate_ref):
    df_ref[...] = (scale * mv_ref[gate_k:gate_k + 1, :] * dx).astype(BF16)
    _accumulate(dgate_ref, scale * _colsum(dx * f_ref[...]), rows.seg_first(i))


def glu(rows, pvg, name):
    w = pvg.shape[1] // 2

    def body(v_ref, t_ref, o_ref):
        for cs in _chunks(w):
            o_ref[:, cs] = v_ref[:, cs].astype(F32) * _sigmoid(t_ref[:, cs].astype(F32))

    return rows.call(body, name, [rows.row(w, 0), rows.row(w, 1)], rows.row(w), jax.ShapeDtypeStruct((rows.t_all, w), F32))(pvg, pvg)


def glu_bwd(rows, pvg, du, name):
    w = pvg.shape[1] // 2

    def body(v_ref, t_ref, du_ref, o_ref):
        for cs in _chunks(w):
            s = _sigmoid(t_ref[:, cs].astype(F32))
            duv = du_ref[:, cs]
            o_ref[:, cs] = (duv * s).astype(BF16)
            o_ref[:, slice(w + cs.start, w + cs.stop)] = (duv * v_ref[:, cs].astype(F32) * s * (1.0 - s)).astype(BF16)

    return rows.call(body, name, [rows.row(w, 0), rows.row(w, 1), rows.row(w)], rows.row(2 * w),
                     jax.ShapeDtypeStruct((rows.t_all, 2 * w), BF16))(pvg, pvg, du)


def ln_silu(rows, u, g, b, name):
    w = u.shape[1]

    def body(u_ref, g_ref, b_ref, o_ref):
        uv = u_ref[...]
        xc = uv - jnp.mean(uv, axis=-1, keepdims=True)
        n = xc * lax.rsqrt(jnp.mean(xc * xc, axis=-1, keepdims=True) + EPS)
        o_ref[...] = _silu_and_grad(n * g_ref[...] + b_ref[...])[0].astype(BF16)

    return rows.call(body, name, [rows.row(w), rows.vec(w), rows.vec(w)], rows.row(w),
                     jax.ShapeDtypeStruct((rows.t_all, w), BF16))(u, g, b)


def ln_silu_bwd(rows, u, ds, g, b, name):
    w = u.shape[1]

    def body(u_ref, ds_ref, g_ref, b_ref, du_ref, dg_ref, db_ref):
        i = pl.program_id(0)
        uv, gv = u_ref[...], g_ref[...]
        xc = uv - jnp.mean(uv, axis=-1, keepdims=True)
        r = lax.rsqrt(jnp.mean(xc * xc, axis=-1, keepdims=True) + EPS)
        n = xc * r
        dy = ds_ref[...].astype(F32) * _silu_and_grad(n * gv + b_ref[...])[1]
        dn = dy * gv
        du_ref[...] = r * (dn - jnp.mean(dn, axis=-1, keepdims=True) - n * jnp.mean(dn * n, axis=-1, keepdims=True))
        _accumulate(dg_ref, _colsum(dy * n), i == 0)
        _accumulate(db_ref, _colsum(dy), i == 0)

    return rows.call(body, name, [rows.row(w), rows.row(w), rows.vec(w), rows.vec(w)], [rows.row(w), rows.vec(w), rows.vec(w)],
                     [jax.ShapeDtypeStruct((rows.t_all, w), F32), jax.ShapeDtypeStruct((1, w), F32),
                      jax.ShapeDtypeStruct((1, w), F32)])(u, ds, g, b)


def lru_merge(rows, h, gl, name):
    w = gl.shape[1]

    def body(h_ref, gl_ref, o_ref):
        for cs in _chunks(w):
            h_sum = h_ref[0, :, cs].astype(F32) + h_ref[1, :, cs].astype(F32)
            o_ref[:, cs] = (h_sum * _gelu_and_grad(gl_ref[:, cs].astype(F32))[0]).astype(BF16)

    return rows.call(body, name, [pl.BlockSpec((2, rows.tm, w), lambda i: (0, i, 0)), rows.row(w)], rows.row(w),
                     jax.ShapeDtypeStruct((rows.t_all, w), BF16))(h, gl)


def lru_merge_bwd(rows, h, gl, dy, name):
    w = gl.shape[1]

    def body(h_ref, gl_ref, dy_ref, dh_ref, dgl_ref):
        for cs in _chunks(w):
            act, dact = _gelu_and_grad(gl_ref[:, cs].astype(F32))
            dyv = dy_ref[:, cs].astype(F32)
            dh_ref[:, cs] = (dyv * act).astype(BF16)
            dgl_ref[:, cs] = (dyv * (h_ref[0, :, cs].astype(F32) + h_ref[1, :, cs].astype(F32)) * dact).astype(BF16)

    return rows.call(body, name, [pl.BlockSpec((2, rows.tm, w), lambda i: (0, i, 0)), rows.row(w), rows.row(w)],
                     [rows.row(w), rows.row(w)],
                     [jax.ShapeDtypeStruct((rows.t_all, w), BF16), jax.ShapeDtypeStruct((rows.t_all, w), BF16)])(h, gl, dy)


def z_merge(rows, pgc, yc, yl, name):
    d = rows.d

    def body(gc_ref, gr_ref, yc_ref, yl_ref, o_ref):
        for cs in _chunks(d):
            o_ref[:, cs] = (_sigmoid(gc_ref[:, cs].astype(F32)) * yc_ref[:, cs].astype(F32)
                            + _sigmoid(gr_ref[:, cs].astype(F32)) * yl_ref[:, cs].astype(F32)).astype(BF16)

    return rows.call(body, name, [rows.row(d, 0), rows.row(d, 1), rows.row(d), rows.row(d)], rows.row(d),
                     jax.ShapeDtypeStruct((rows.t_all, d), BF16))(pgc, pgc, yc, yl)


def z_merge_bwd(rows, pgc, yc, yl, dz, name):
    d = rows.d

    def body(gc_ref, gr_ref, yc_ref, yl_ref, dz_ref, dyc_ref, dyl_ref, dp_ref):
        for cs in _chunks(d):
            sc, sr = _sigmoid(gc_ref[:, cs].astype(F32)), _sigmoid(gr_ref[:, cs].astype(F32))
            dzv = dz_ref[:, cs].astype(F32)
            dyc_ref[:, cs] = (dzv * sc).astype(BF16)
            dyl_ref[:, cs] = (dzv * sr).astype(BF16)
            dp_ref[:, cs] = (dzv * yc_ref[:, cs].astype(F32) * sc * (1.0 - sc)).astype(BF16)
            dp_ref[:, slice(d + cs.start, d + cs.stop)] = (dzv * yl_ref[:, cs].astype(F32) * sr * (1.0 - sr)).astype(BF16)

    return rows.call(body, name, [rows.row(d, 0), rows.row(d, 1), rows.row(d), rows.row(d), rows.row(d)],
                     [rows.row(d), rows.row(d), rows.row(2 * d)],
                     [jax.ShapeDtypeStruct((rows.t_all, d), BF16), jax.ShapeDtypeStruct((rows.t_all, d), BF16),
                      jax.ShapeDtypeStruct((rows.t_all, 2 * d), BF16)])(pgc, pgc, yc, yl, dz)


def loss_head(rows, x, f, mv, g, target, name):
    d = rows.d

    def body(x_ref, f_ref, mv_ref, g_ref, t_ref, loss_ref, dx_ref, dg_ref, df_ref, dgate_ref):
        i = pl.program_id(0)
        valid = jnp.where(i < rows.n_lat, 1.0, 0.0)
        xv = x_ref[...] + 0.5 * mv_ref[8:9, :] * f_ref[...]
        gv = g_ref[...]
        r = lax.rsqrt(jnp.mean(xv * xv, axis=-1, keepdims=True) + EPS)
        n = xv * r
        err = (n * gv - t_ref[...]) * valid
        part = 0.5 * jnp.sum(jnp.mean(err * err, axis=-1, keepdims=True), axis=0, keepdims=True)
        _accumulate(loss_ref, jnp.broadcast_to(part, (1, LANES)), i == 0)
        dy = err * (1.0 / d)
        dn = dy * gv
        dx = r * (dn - n * jnp.mean(dn * n, axis=-1, keepdims=True))
        dx_ref[...] = dx
        _accumulate(dg_ref, _colsum(dy * n), i == 0)
        _gate_bwd(rows, i, dx, f_ref, mv_ref, 8, 0.5, df_ref, dgate_ref)

    target_spec = pl.BlockSpec((rows.tm, d), lambda i: (jnp.minimum(i, rows.n_lat - 1), 0))
    return rows.call(body, name, [rows.row(d), rows.row(d), rows.mod(), rows.vec(d), target_spec],
                     [rows.vec(LANES), rows.row(d), rows.vec(d), rows.row(d), rows.segsum()],
                     [jax.ShapeDtypeStruct((1, LANES), F32), jax.ShapeDtypeStruct((rows.t_all, d), F32),
                      jax.ShapeDtypeStruct((1, d), F32), jax.ShapeDtypeStruct((rows.t_all, d), BF16), rows.segsum_shape()])(
                          x, f, mv, g, target)


def _extended(rows, i, prev_ref, cur_ref, next_ref, cs):
    prev = jnp.where(rows.seg_first(i), 0.0, prev_ref[:, cs].astype(F32))
    nxt = jnp.where(rows.seg_last(i), 0.0, next_ref[:, cs].astype(F32))
    return jnp.concatenate([prev, cur_ref[:, cs].astype(F32), nxt], axis=0)


class _Shifts:
    def __init__(self, ext, tm):
        self.ext, self.tm, self.rolled = ext, tm, {0: ext}

    def at(self, offset):
        residue = offset % SUBLANES
        if residue not in self.rolled:
            self.rolled[residue] = pltpu.roll(self.ext, self.ext.shape[0] - residue, 0)
        start = HALO + offset - residue
        return self.rolled[residue][start:start + self.tm]


def _pad_taps(w):
    k = w.shape[0]
    return jnp.pad(w, ((0, -k % SUBLANES), (0, 0)))


def dwconv(rows, u, w, b, pad_left, out_dtype, name):
    taps, width = w.shape
    wp = _pad_taps(w)

    def body(*refs):
        prev_ref, cur_ref, next_ref, w_ref = refs[:4]
        b_ref = refs[4] if b is not None else None
        o_ref = refs[-1]
        i = pl.program_id(0)
        for cs in _chunks(width, LANES):
            shifts = _Shifts(_extended(rows, i, prev_ref, cur_ref, next_ref, cs), rows.tm)
            acc = jnp.zeros((rows.tm, LANES), F32) if b_ref is None else jnp.broadcast_to(b_ref[:, cs], (rows.tm, LANES))
            for k in range(taps):
                acc = acc + w_ref[k:k + 1, cs] * shifts.at(k - pad_left)
            o_ref[:, cs] = acc.astype(out_dtype)

    ins = [u, u, u, wp] + ([b] if b is not None else [])
    in_specs = [rows.halo(width, -1), rows.row(width), rows.halo(width, 1), rows.full(wp.shape)] + ([rows.vec(width)] if b is not None else [])
    return rows.call(body, name, in_specs, rows.row(width), jax.ShapeDtypeStruct((rows.t_all, width), out_dtype))(*ins)


def dwconv_bwd_w(rows, u, dy, taps, pad_left, name):
    width = u.shape[1]
    taps_p = taps + (-taps % SUBLANES)

    def body(prev_ref, cur_ref, next_ref, dy_ref, dw_ref, db_ref):
        i = pl.program_id(0)
        tap_row = lax.broadcasted_iota(jnp.int32, (taps_p, LANES), 0)
        for cs in _chunks(width, LANES):
            shifts = _Shifts(_extended(rows, i, prev_ref, cur_ref, next_ref, cs), rows.tm)
            dyv = dy_ref[:, cs]
            total = jnp.zeros((taps_p, LANES), F32)
            for k in range(taps):
                total = total + jnp.where(tap_row == k, _colsum(dyv * shifts.at(k - pad_left)), 0.0)
            _accumulate(dw_ref.at[:, cs], total, i == 0)
            _accumulate(db_ref.at[:, cs], _colsum(dyv), i == 0)

    dw, db = rows.call(body, name, [rows.halo(width, -1), rows.row(width), rows.halo(width, 1), rows.row(width)],
                       [rows.full((taps_p, width)), rows.vec(width)],
                       [jax.ShapeDtypeStruct((taps_p, width), F32), jax.ShapeDtypeStruct((1, width), F32)])(u, u, u, dy)
    return dw[:taps], db


def _gate_values(xr, pre, lam, d):
    r = _sigmoid(pre[:, (2 * d) * LANES:(2 * d + 1) * LANES])
    ig = _sigmoid(pre[:, (2 * d + 1) * LANES:(2 * d + 2) * LANES])
    sp = _softplus(-lam[:, d * LANES:(d + 1) * LANES])
    log_a = -LRU_C * r * sp
    return r, ig, sp, jnp.exp(log_a), _neg_expm1(2.0 * log_a)


def lru_gates(rows, xr, wcat, bcat, lam, name):
    n_blk = wcat.shape[0]
    width = xr.shape[1]

    def body(xr_ref, w_ref, b_ref, lam_ref, a_ref, bb_ref):
        for h in range(n_blk):
            cs = slice(h * LANES, (h + 1) * LANES)
            xv = xr_ref[:, cs]
            pre = jnp.dot(xv.astype(BF16), w_ref[h], preferred_element_type=F32) + b_ref[h]
            for d in range(2):
                _, ig, _, a, one_minus_a2 = _gate_values(xv, pre, lam_ref[h], d)
                a_ref[d, :, cs] = a
                bb_ref[d, :, cs] = (jnp.sqrt(one_minus_a2) * (ig * xv)).astype(BF16)

    both = pl.BlockSpec((2, rows.tm, width), lambda i: (0, i, 0))
    return rows.call(body, name, [rows.row(width), rows.full(wcat.shape), rows.full(bcat.shape), rows.full(lam.shape)], [both, both],
                     [jax.ShapeDtypeStruct((2, rows.t_all, width), F32), jax.ShapeDtypeStruct((2, rows.t_all, width), BF16)])(
                         xr, wcat, bcat, lam)


def lru_gates_bwd(rows, xr, wcat, bcat, lam, da, dbb, name):
    n_blk = wcat.shape[0]
    width = xr.shape[1]

    def body(xr_ref, w_ref, b_ref, lam_ref, da_ref, dbb_ref, dxr_ref, dw_ref, db_ref, dlam_ref):
        i = pl.program_id(0)
        for h in range(n_blk):
            cs = slice(h * LANES, (h + 1) * LANES)
            xv = xr_ref[:, cs]
            xb = xv.astype(BF16)
            wv = w_ref[h]
            pre = jnp.dot(xb, wv, preferred_element_type=F32) + b_ref[h]
            dxr = jnp.zeros_like(xv)
            dpre, dlam = [], []
            for d in range(2):
                r, ig, sp, a, one_minus_a2 = _gate_values(xv, pre, lam_ref[h], d)
                inv_q = lax.rsqrt(one_minus_a2)
                dav, dbv = da_ref[d, :, cs].astype(F32), dbb_ref[d, :, cs].astype(F32)
                dbq = dbv * (one_minus_a2 * inv_q) * ig
                dlog_a = dav * a - dbv * (ig * xv) * ((a * a) * inv_q)
                dpre.append(dlog_a * (-LRU_C * sp) * (r - r * r))
                dpre.append(dbq * xv * (1.0 - ig))
                dxr = dxr + dbq
                dlam.append(_colsum(dlog_a * (-LRU_C * r)) * (-_sigmoid(-lam_ref[h][:, d * LANES:(d + 1) * LANES])))
            dpre = jnp.concatenate(dpre, axis=1)
            dpb = dpre.astype(BF16)
            dxr_ref[:, cs] = dxr + lax.dot_general(dpb, wv, (((1,), (1,)), ((), ())), preferred_element_type=F32)
            _accumulate(dw_ref.at[h], lax.dot_general(xb, dpb, (((0,), (0,)), ((), ())), preferred_element_type=F32), i == 0)
            _accumulate(db_ref.at[h], _colsum(dpre), i == 0)
            _accumulate(dlam_ref.at[h], jnp.concatenate(dlam, axis=1), i == 0)

    both = pl.BlockSpec((2, rows.tm, width), lambda i: (0, i, 0))
    return rows.call(
        body, name, [rows.row(width), rows.full(wcat.shape), rows.full(bcat.shape), rows.full(lam.shape), both, both],
        [rows.row(width), rows.full(wcat.shape), rows.full(bcat.shape), rows.full(lam.shape)],
        [jax.ShapeDtypeStruct((rows.t_all, width), F32), jax.ShapeDtypeStruct(wcat.shape, F32),
         jax.ShapeDtypeStruct(bcat.shape, F32), jax.ShapeDtypeStruct(lam.shape, F32)])(xr, wcat, bcat, lam, da, dbb)


def _tile_scan(a, b, reverse):
    n = a.shape[0]
    row = lax.broadcasted_iota(jnp.int32, a.shape, 0)
    k = 1
    while k < n:
        ok = (row < n - k) if reverse else (row >= k)
        shift = n - k if reverse else k
        b = b + a * jnp.where(ok, pltpu.roll(b, shift, 0), 0.0)
        a = a * jnp.where(ok, pltpu.roll(a, shift, 0), 1.0)
        k *= 2
    return a, b


def _chain_scan(a, b, h_in, reverse):
    n_blocks = a.shape[0] // SUBLANES
    out = [None] * n_blocks
    state = h_in
    for j in (range(n_blocks - 1, -1, -1) if reverse else range(n_blocks)):
        rows_j = slice(j * SUBLANES, (j + 1) * SUBLANES)
        cum, h0 = _tile_scan(a[rows_j], b[rows_j], reverse)
        out[j] = h0 + cum * state
        state = out[j][0:1] if reverse else out[j][SUBLANES - 1:SUBLANES]
    return jnp.concatenate(out, axis=0), state


def _neighbour(v, edge, reverse):
    n = v.shape[0]
    row = lax.broadcasted_iota(jnp.int32, v.shape, 0)
    if reverse:
        return jnp.where(row < n - 1, pltpu.roll(v, n - 1, 0), edge)
    return jnp.where(row >= 1, pltpu.roll(v, 1, 0), edge)


def _scan_call(rows, body, name, ins, in_specs, n_out, width, adjoint):
    n_all, n_lat = rows.n_all, rows.n_lat

    def tile(d, s):
        s = n_all - 1 - s if adjoint else s
        return jnp.where(d == 0, (s + n_lat) % n_all, n_all - 1 - s)

    def per_dir(d, s):
        return (d, tile(d, s), 0)

    specs = [pl.BlockSpec((None, rows.tm, width), per_dir) if kind == "dir" else
             pl.BlockSpec((rows.tm, width), lambda d, s: (tile(d, s), 0)) for kind in in_specs]
    shape = jax.ShapeDtypeStruct((2, rows.t_all, width), BF16)
    return pl.pallas_call(
        body, name=name, grid=(2, n_all), in_specs=specs, out_specs=[pl.BlockSpec((None, rows.tm, width), per_dir)] * n_out,
        out_shape=[shape] * n_out, scratch_shapes=[pltpu.VMEM((SUBLANES, width), F32)],
        compiler_params=_params("arbitrary", "arbitrary"))(*ins)


def lru_scan(rows, a, bb, name):
    width = a.shape[2]

    def body(a_ref, bb_ref, h_ref, hp_ref, carry):
        d, s = pl.program_id(0), pl.program_id(1)

        @pl.when(s == 0)
        def _():
            carry[...] = jnp.zeros_like(carry)

        def run(reverse):
            for cs in _chunks(width, LANES):
                h_in = carry[0:1, cs]
                h, carry[0:1, cs] = _chain_scan(a_ref[:, cs], bb_ref[:, cs].astype(F32), h_in, reverse)
                h_ref[:, cs] = h.astype(BF16)
                hp_ref[:, cs] = _neighbour(h, h_in, reverse).astype(BF16)

        pl.when(d == 0)(lambda: run(False))
        pl.when(d == 1)(lambda: run(True))

    return _scan_call(rows, body, name, [a, bb], ["dir", "dir"], 2, width, adjoint=False)


def lru_scan_bwd(rows, dh, a, hp, name):
    width = a.shape[2]
    n = rows.tm

    def body(dh_ref, a_ref, hp_ref, da_ref, dbb_ref, carry):
        d, s = pl.program_id(0), pl.program_id(1)

        @pl.when(s == 0)
        def _():
            carry[...] = jnp.zeros_like(carry)

        def run(reverse):
            for cs in _chunks(width, LANES):
                av = a_ref[:, cs]
                g, _ = _chain_scan(_neighbour(av, 1.0, reverse), dh_ref[:, cs].astype(F32), carry[0:1, cs], reverse)
                da_ref[:, cs] = (g * hp_ref[:, cs].astype(F32)).astype(BF16)
                dbb_ref[:, cs] = g.astype(BF16)
                carry[0:1, cs] = (av * g)[0:1] if reverse else (av * g)[n - 1:n]

        pl.when(d == 0)(lambda: run(True))
        pl.when(d == 1)(lambda: run(False))

    return _scan_call(rows, body, name, [dh, a, hp], ["shared", "dir", "dir"], 2, width, adjoint=True)


def mod_matmul(c9, w_shard, b_shard, name):
    n = w_shard.shape[1]
    tn = _pick(n, 768)

    def body(c_ref, w_ref, b_ref, o_ref):
        act = _silu_and_grad(c_ref[...])[0]
        o_ref[...] = jnp.dot(act, w_ref[...], preferred_element_type=F32, precision=lax.Precision.HIGHEST) + b_ref[...]

    return pl.pallas_call(
        body, name=name, grid=(n // tn,),
        in_specs=[pl.BlockSpec(c9.shape, lambda j: (0, 0)), pl.BlockSpec((w_shard.shape[0], tn), lambda j: (0, j)),
                  pl.BlockSpec((1, tn), lambda j: (0, j))],
        out_specs=pl.BlockSpec((c9.shape[0], tn), lambda j: (0, j)), out_shape=jax.ShapeDtypeStruct((c9.shape[0], n), F32),
        compiler_params=_params("arbitrary"))(c9, w_shard, b_shard)


def mod_matmul_bwd(c9, d9, w_shard, name):
    n = w_shard.shape[1]
    tn = _pick(n, 768)
    steps = n // tn

    def body(c_ref, d_ref, w_ref, gw_ref, gc_ref):
        j = pl.program_id(0)
        act, dact = _silu_and_grad(c_ref[...])
        dv = d_ref[...]
        gw_ref[...] = lax.dot_general(act, dv, (((0,), (0,)), ((), ())), preferred_element_type=F32, precision=lax.Precision.HIGHEST)
        part = lax.dot_general(dv, w_ref[...], (((1,), (1,)), ((), ())), preferred_element_type=F32, precision=lax.Precision.HIGHEST)
        _accumulate(gc_ref, part * dact, j == 0)

    return pl.pallas_call(
        body, name=name, grid=(steps,),
        in_specs=[pl.BlockSpec(c9.shape, lambda j: (0, 0)), pl.BlockSpec((c9.shape[0], tn), lambda j: (0, j)),
                  pl.BlockSpec((w_shard.shape[0], tn), lambda j: (0, j))],
        out_specs=[pl.BlockSpec((w_shard.shape[0], tn), lambda j: (0, j)), pl.BlockSpec(c9.shape, lambda j: (0, 0))],
        out_shape=[jax.ShapeDtypeStruct(w_shard.shape, F32), jax.ShapeDtypeStruct(c9.shape, F32)],
        compiler_params=_params("arbitrary"))(c9, d9, w_shard)


def _row_tile(n_rows, n_cols):
    return _pick(n_rows, max(2 * SUBLANES, (256 * 1024) // n_cols), 2 * SUBLANES)


def sum_parts(parts, name):
    n, n_rows, n_cols = parts.shape
    tr = _row_tile(n_rows, n_cols)

    def body(p_ref, o_ref):
        total = p_ref[0].astype(F32)
        for k in range(1, n):
            total = total + p_ref[k].astype(F32)
        o_ref[...] = total

    return pl.pallas_call(
        body, name=name, grid=(n_rows // tr,), in_specs=[pl.BlockSpec((n, tr, n_cols), lambda i: (0, i, 0))],
        out_specs=pl.BlockSpec((tr, n_cols), lambda i: (i, 0)), out_shape=jax.ShapeDtypeStruct((n_rows, n_cols), F32),
        compiler_params=_params("arbitrary"))(parts)


def _adamw_update(g, w_ref, m_ref, v_ref, g_ref, d_ref, m2_ref, v2_ref):
    m2 = ADAM_B1 * m_ref[...] + (1.0 - ADAM_B1) * g
    v2 = ADAM_B2 * v_ref[...] + (1.0 - ADAM_B2) * (g * g)
    m_hat = m2 / (1.0 - ADAM_B1 ** ADAM_STEP)
    v_hat = v2 / (1.0 - ADAM_B2 ** ADAM_STEP)
    g_ref[...] = g
    d_ref[...] = -ADAM_LR * (m_hat / (jnp.sqrt(v_hat) + ADAM_EPS) + ADAM_WD * w_ref[...])
    m2_ref[...] = m2
    v2_ref[...] = v2


def adamw(parts, w, m, v, name):
    n, n_rows, n_cols = parts.shape
    tr = _row_tile(n_rows, n_cols)

    def body(p_ref, *refs):
        g = p_ref[0].astype(F32)
        for k in range(1, n):
            g = g + p_ref[k].astype(F32)
        _adamw_update(g, *refs)

    blk = pl.BlockSpec((tr, n_cols), lambda i: (i, 0))
    shape = jax.ShapeDtypeStruct((n_rows, n_cols), F32)
    return pl.pallas_call(
        body, name=name, grid=(n_rows // tr,), in_specs=[pl.BlockSpec((n, tr, n_cols), lambda i: (0, i, 0)), blk, blk, blk],
        out_specs=[blk] * 4, out_shape=[shape] * 4, compiler_params=_params("arbitrary"))(parts, w, m, v)


def adamw_pair(parts, w, m, v, name):
    n_parts, n_rows, n_cols = parts.shape
    tr = _row_tile(n_rows, n_cols)
    steps = n_rows // tr

    def body(p_ref, w_ref, m_ref, v_ref, g_ref, d_ref, m2_ref, v2_ref, outbox, inbox, send_sems, recv_sems, credits):
        i = pl.program_id(0)
        sibling = (lax.axis_index("x"), lax.axis_index("y"), 1 - lax.axis_index("c"))

        def copy(slot):
            return pltpu.make_async_remote_copy(src_ref=outbox.at[slot], dst_ref=inbox.at[slot], send_sem=send_sems.at[slot],
                                                recv_sem=recv_sems.at[slot], device_id=sibling, device_id_type=MESH)

        @pl.when(i < steps)
        def _():
            slot = i % 2
            total = p_ref[0].astype(F32)
            for k in range(1, n_parts):
                total = total + p_ref[k].astype(F32)
            outbox[slot] = total

            @pl.when(i >= 2)
            def _():
                pl.semaphore_wait(credits.at[slot], 1)

            copy(slot).start()

        @pl.when(i >= 1)
        def _():
            slot = (i - 1) % 2
            landed = copy(slot)
            landed.wait_recv()
            landed.wait_send()
            _adamw_update(outbox[slot] + inbox[slot], w_ref, m_ref, v_ref, g_ref, d_ref, m2_ref, v2_ref)

            @pl.when(i + 1 < steps)
            def _():
                pl.semaphore_signal(credits.at[slot], inc=1, device_id=sibling, device_id_type=MESH)

    blk = pl.BlockSpec((tr, n_cols), lambda i: (jnp.maximum(i - 1, 0), 0))
    shape = jax.ShapeDtypeStruct((n_rows, n_cols), F32)
    return pl.pallas_call(
        body, name=name, grid=(steps + 1,),
        in_specs=[pl.BlockSpec((n_parts, tr, n_cols), lambda i: (0, jnp.minimum(i, steps - 1), 0)), blk, blk, blk],
        out_specs=[blk] * 4, out_shape=[shape] * 4,
        scratch_shapes=[pltpu.VMEM((2, tr, n_cols), F32), pltpu.VMEM((2, tr, n_cols), F32), pltpu.SemaphoreType.DMA((2,)),
                        pltpu.SemaphoreType.DMA((2,)), pltpu.SemaphoreType.REGULAR((2,))],
        compiler_params=_params("arbitrary"))(parts, w, m, v)


def pair_swap(x, name):
    n, n_rows, n_cols = x.shape

    def body(x_ref, o_ref, inbox, send_sems, recv_sems, credits):
        i = pl.program_id(0)
        slot = i % 2
        sibling = (lax.axis_index("x"), lax.axis_index("y"), 1 - lax.axis_index("c"))

        @pl.when(i >= 2)
        def _():
            pl.semaphore_wait(credits.at[slot], 1)

        copy = pltpu.make_async_remote_copy(src_ref=x_ref, dst_ref=inbox.at[slot], send_sem=send_sems.at[slot],
                                            recv_sem=recv_sems.at[slot], device_id=sibling, device_id_type=MESH)
        copy.start()
        copy.wait_recv()
        copy.wait_send()
        o_ref[...] = inbox[slot]

        @pl.when(i + 2 < n)
        def _():
            pl.semaphore_signal(credits.at[slot], inc=1, device_id=sibling, device_id_type=MESH)

    blk = pl.BlockSpec((1, n_rows, n_cols), lambda i: (i, 0, 0))
    return pl.pallas_call(
        body, name=name, grid=(n,), in_specs=[blk], out_specs=blk, out_shape=jax.ShapeDtypeStruct(x.shape, x.dtype),
        scratch_shapes=[pltpu.VMEM((2, 1, n_rows, n_cols), x.dtype), pltpu.SemaphoreType.DMA((2,)), pltpu.SemaphoreType.DMA((2,)),
                        pltpu.SemaphoreType.REGULAR((2,))],
        compiler_params=_params("arbitrary"))(x)


def _pack(arrays, row_multiple=SUBLANES):
    flat = [jnp.pad(a.reshape(-1), (0, -a.size % LANES)) for a in arrays]
    buf = jnp.concatenate(flat)
    buf = jnp.pad(buf, (0, -buf.size % (row_multiple * LANES)))
    return buf.reshape(-1, LANES)


def _unpack(buf, shapes):
    lead = buf.shape[:-2]
    flat = buf.reshape(lead + (-1,))
    out, pos = [], 0
    for shape in shapes:
        size = 1
        for s in shape:
            size *= s
        out.append(flat[..., pos:pos + size].reshape(lead + tuple(shape)))
        pos += size + (-size % LANES)
    return out


def _pos_tables(seq_len, dim, tile_rows):
    grid_rows = seq_len // GRID_W
    per_tile = tile_rows // GRID_W
    q = dim // 4
    omega = 1.0 / (10000.0 ** (jnp.arange(q, dtype=F32) / q))
    er = jnp.arange(grid_rows).astype(F32)[:, None] * omega
    ec = jnp.arange(GRID_W).astype(F32)[:, None] * omega
    by_row = jnp.concatenate([jnp.sin(er), jnp.cos(er)], axis=-1).reshape(grid_rows // per_tile, per_tile, 2 * q)
    by_row = jnp.pad(by_row, ((0, 0), (0, SUBLANES - per_tile), (0, 0))).reshape(-1, 2 * q)
    return by_row, jnp.concatenate([jnp.sin(ec), jnp.cos(ec)], axis=-1)


def _cols_from_shards(g):
    return jnp.transpose(g, (1, 0, 2)).reshape(g.shape[1], -1)


def _shards_from_cols(w, n_shards=4):
    k, n = w.shape
    return jnp.transpose(w.reshape(k, n_shards, n // n_shards), (1, 0, 2))


def kernel(x, c, ctx, c_ctx, w_mod, b_mod, g_n1, w_ffn1_up, w_ffn1_down, g_n2, w_in, b_in, w_dw, b_dw, g_ln, b_ln, w_conf_out, w_lru_conv, b_lru_conv, w_rec_gate, b_rec_gate, w_in_gate, b_in_gate, lru_lambda, w_lru_out, w_out, g_n3, w_ffn2_up, w_ffn2_down, g_final, loss_target, m_c_ctx, m_w_mod, m_b_mod, m_g_n1, m_w_ffn1_up, m_w_ffn1_down, m_g_n2, m_w_in, m_b_in, m_w_dw, m_b_dw, m_g_ln, m_b_ln, m_w_conf_out, m_w_lru_conv, m_b_lru_conv, m_w_rec_gate, m_b_rec_gate, m_w_in_gate, m_b_in_gate, m_lru_lambda, m_w_lru_out, m_w_out, m_g_n3, m_w_ffn2_up, m_w_ffn2_down, m_g_final, v_c_ctx, v_w_mod, v_b_mod, v_g_n1, v_w_ffn1_up, v_w_ffn1_down, v_g_n2, v_w_in, v_b_in, v_w_dw, v_b_dw, v_g_ln, v_b_ln, v_w_conf_out, v_w_lru_conv, v_b_lru_conv, v_w_rec_gate, v_b_rec_gate, v_w_in_gate, v_b_in_gate, v_lru_lambda, v_w_lru_out, v_w_out, v_g_n3, v_w_ffn2_up, v_w_ffn2_down, v_g_final):
    given = dict(locals())
    wt = {n: given[n] for n in W_NAMES}
    mom = {n: given["m_" + n] for n in W_NAMES}
    var = {n: given["v_" + n] for n in W_NAMES}

    def flat2(a):
        if a.ndim == 1:
            return a.reshape(1, -1)
        a = a[0]
        return a if a.ndim == 2 else a.reshape(-1, a.shape[-1])

    t_lat, d = x.shape[1], x.shape[2]
    t_ctx = ctx.shape[1]
    rows = Rows(t_lat, t_ctx, d)
    xi, yi, ci = lax.axis_index("x"), lax.axis_index("y"), lax.axis_index("c")
    me, chip = 4 * xi + 2 * yi + ci, 2 * xi + yi
    lru_w = b_lru_conv.shape[-1]
    n_blk = lru_w // LANES
    taps = w_dw.shape[1]
    lru_taps = w_lru_conv.shape[1]
    ffn = w_ffn1_down.shape[1] * 4

    small_shapes = [(1, d)] + [flat2(wt[n]).shape for n in SMALL_SHARDED]
    small_all = exchange("all", [_pack([c] + [flat2(wt[n]) for n in SMALL_SHARDED])], False, "gather_small")[0]
    small_all = _unpack(small_all, small_shapes)
    c_all = small_all[0][:, 0, :]
    full = {n: jnp.concatenate([a[0], a[2], a[4], a[6]], axis=-1) for n, a in zip(SMALL_SHARDED, small_all[1:])}
    big = COL_SHARDED + ROW_SHARDED
    ffn1_names = ['w_ffn1_up', 'w_ffn1_down']
    ffn2_names = ['w_ffn2_up', 'w_ffn2_down']
    mixer_names = [n for n in big if n not in ffn1_names + ffn2_names + ['w_in']]
    wb = {}

    def take_weights(names, gathered):
        for n, g in zip(names, gathered):
            wb[n] = _cols_from_shards(g) if n in COL_SHARDED else g.reshape(-1, g.shape[-1])

    def shards_after(names, earlier):
        return lax.optimization_barrier(([wt[n][0].astype(BF16) for n in names], earlier))[0]

    c9 = jnp.concatenate([c_all, c_ctx.reshape(1, d), jnp.zeros((7, d), F32)], axis=0)
    mod_cols = mod_matmul(c9, w_mod[0], lax.dynamic_slice_in_dim(b_mod, chip * w_mod.shape[2], w_mod.shape[2], axis=1), "mod_matmul")
    mod_all = exchange("all", [mod_cols], False, "gather_mod")[0]
    mod9 = jnp.concatenate([mod_all[0], mod_all[2], mod_all[4], mod_all[6]], axis=-1)
    mv = jnp.stack([lax.dynamic_index_in_dim(mod9, me, 0, keepdims=False).reshape(N_MOD, d), mod9[8].reshape(N_MOD, d)])

    up1_shard = shards_after(ffn1_names[:1], mod_all)[0]
    half_rows = up1_shard.shape[0] // 2
    gathered_up1 = exchange_by_sequencer([lax.dynamic_slice_in_dim(up1_shard, ci * half_rows, half_rows, axis=0)], False,
                                         "gather_weights_ffn1_up", 7)
    gathered_down1 = exchange_by_sequencer(shards_after(ffn1_names[1:], mod_all), False, "gather_weights_ffn1_down", 8)
    gathered_in = exchange_by_sequencer(shards_after(['w_in'], mod_all), False, "gather_weights_in", 1)
    gathered_mixer = exchange_by_sequencer(shards_after(mixer_names, mod_all), False, "gather_weights_mixer", 9)
    gathered_ffn2 = exchange_by_sequencer(shards_after(ffn2_names, mod_all), False, "gather_weights_ffn2", 5)

    row_tab, col_tab = _pos_tables(t_lat, d, rows.tm)

    def ffn_fwd(tag, h, names, after_up=None):
        g, u, act = matmul_swiglu(h, wb[names[0]], f"{tag}_up")
        if after_up is not None:
            act = after_up(act)
        return (g, u), act, matmul(act, wb[names[1]], "nn", F32, f"{tag}_down")

    def take_down1(act):
        landed, act = lax.optimization_barrier((gathered_down1, act))
        take_weights(ffn1_names[1:], landed)
        return act

    x0, h1 = embed_norm_mod(rows, x[0], ctx[0], row_tab, col_tab, g_n1, mv, 0, 1, "norm1")
    gathered_up1, h1 = lax.optimization_barrier((gathered_up1, h1))
    mine = gathered_up1[0]
    theirs = pair_swap(mine, "swap_weights_ffn1_up")
    halves = [jnp.where(ci == 0, mine, theirs), jnp.where(ci == 0, theirs, mine)]
    take_weights(ffn1_names[:1], [jnp.concatenate(halves, axis=1)])
    gu1, act1, f1 = ffn_fwd("ffn1", h1, ffn1_names, after_up=take_down1)
    x1, h2 = norm_mod(rows, x0, g_n2, mv, 3, 4, "norm2", resid=(f1, 2, 0.5))
    gathered_in, h2 = lax.optimization_barrier((gathered_in, h2))
    take_weights(['w_in'], gathered_in)

    conf_w = w_conf_out.shape[1] * 4
    col_groups = [(0, 2 * conf_w), (2 * conf_w, lru_w), (2 * conf_w + lru_w, lru_w), (2 * conf_w + 2 * lru_w, 2 * d)]
    w_in_groups = [wb["w_in"][:, s:s + n] for s, n in col_groups]
    pvg, ux, gl, pgc = matmul_groups_nn(h2, w_in_groups, [b_in[:, s:s + n] for s, n in col_groups], "in_proj")
    u = glu(rows, pvg, "glu")
    u2 = dwconv(rows, u, full["w_dw"], b_dw, taps // 2, F32, "conf_conv")
    s_act = ln_silu(rows, u2, g_ln, b_ln, "conf_ln")
    gathered_mixer, s_act = lax.optimization_barrier((gathered_mixer, s_act))
    take_weights(mixer_names, gathered_mixer)
    yc = matmul(s_act, wb["w_conf_out"], "nn", BF16, "conf_out")

    xr = dwconv(rows, ux, full["w_lru_conv"], b_lru_conv, LRU_PAD_LEFT, F32, "lru_conv")
    w_rec, w_ing = w_rec_gate[0].astype(BF16), w_in_gate[0].astype(BF16)
    wcat = jnp.concatenate([w_rec[0], w_ing[0], w_rec[1], w_ing[1]], axis=-1)

    def per_block(a):
        return jnp.transpose(a.reshape(2, n_blk, LANES), (1, 0, 2)).reshape(n_blk, 1, 2 * LANES)

    b_rec, b_ing = full["b_rec_gate"].reshape(2, n_blk, LANES), full["b_in_gate"].reshape(2, n_blk, LANES)
    bcat = jnp.concatenate([b_rec[0], b_ing[0], b_rec[1], b_ing[1]], axis=-1).reshape(n_blk, 1, 4 * LANES)
    lam = per_block(full["lru_lambda"])
    a_gate, b_gate = lru_gates(rows, xr, wcat, bcat, lam, "lru_gates")
    h_scan, h_prev = lru_scan(rows, a_gate, b_gate, "lru_scan")
    yl_in = lru_merge(rows, h_scan, gl, "lru_merge")
    yl = matmul(yl_in, wb["w_lru_out"], "nn", BF16, "lru_out")
    z = z_merge(rows, pgc, yc, yl, "z_merge")
    y = matmul(z, wb["w_out"], "nn", F32, "mix_out")

    gathered_ffn2, y = lax.optimization_barrier((gathered_ffn2, y))
    take_weights(ffn2_names, gathered_ffn2)
    x2, h3 = norm_mod(rows, x1, g_n3, mv, 6, 7, "norm3", resid=(y, 5, 1.0))
    gu2, act2, f2 = ffn_fwd("ffn2", h3, ffn2_names)
    loss_part, dx3, dg_final, df2, dgate2 = loss_head(rows, x2, f2, mv, g_final.reshape(1, d), loss_target[0], "loss_head")

    grads = {"g_final": dg_final}
    dmv = [None] * N_MOD

    def ffn_bwd(tag, dxn, df, x_prev, h, gu, act, w_up, w_down, g, ks, collective_id, settle_first=(), **norm_bwd_options):
        k_shift, k_scale = ks
        grads[f"w_{tag}_down"] = matmul(act, df, "tn", BF16, f"{tag}_down_dw")
        if settle_first:
            df = settle(settle_first, df)
        dg_, du_ = matmul_swiglu_bwd(df, w_down, gu[0], gu[1], f"{tag}_down_dx")
        grads[f"w_{tag}_up"] = jnp.concatenate([matmul(h, dg_, "tn", BF16, f"{tag}_up_dw_g"), matmul(h, du_, "tn", BF16, f"{tag}_up_dw_u")], axis=1)
        dg_, du_ = scatter_behind([f"w_{tag}_up", f"w_{tag}_down"], tag, collective_id, (dg_, du_))
        dh = matmul_pair_nt(dg_, du_, w_up, f"{tag}_up_dx")
        dx, dg, dmv[k_shift], dmv[k_scale], *more = norm_mod_bwd(rows, x_prev, dh, dxn, g, mv, k_shift, k_scale, f"{tag}_norm_bwd",
                                                                 **norm_bwd_options)
        return (dx, dg, *more)

    def grad_pieces(names):
        return [(_shards_from_cols(grads[n]) if n in COL_SHARDED else grads[n].reshape(4, -1, grads[n].shape[-1])).astype(BF16)
                for n in names]

    from_chips = {}

    def scatter_behind(names, tag, collective_id, carry):
        ready, carry = lax.optimization_barrier(([grads[n] for n in names], carry))
        grads.update(zip(names, ready))
        from_chips.update(zip(names, exchange_by_sequencer(grad_pieces(names), True, f"scatter_grads_{tag}", collective_id)))
        return carry

    def settle(names, carry):
        landed, carry = lax.optimization_barrier(([from_chips[n] for n in names], carry))
        from_chips.update(zip(names, landed))
        return carry

    dmv[8] = dgate2
    dx2, grads["g_n3"], dy, dmv[5] = ffn_bwd("ffn2", dx3, df2, x2, h3, gu2, act2, wb["w_ffn2_up"], wb["w_ffn2_down"], g_n3, (6, 7), 2,
                                             then_gate=(y, 5, 1.0))

    grads["w_out"] = matmul(z, dy, "tn", BF16, "mix_out_dw")
    dz = matmul(dy, wb["w_out"], "nt", BF16, "mix_out_dx")
    dyc, dyl, dpgc = z_merge_bwd(rows, pgc, yc, yl, dz, "z_merge_bwd")

    grads["w_conf_out"] = matmul(s_act, dyc, "tn", BF16, "conf_out_dw")
    ds_act = matmul(dyc, wb["w_conf_out"], "nt", BF16, "conf_out_dx")
    du2, grads["g_ln"], grads["b_ln"] = ln_silu_bwd(rows, u2, ds_act, g_ln, b_ln, "conf_ln_bwd")
    grads["w_dw"], grads["b_dw"] = dwconv_bwd_w(rows, u, du2, taps, taps // 2, "conf_conv_dw")
    du = dwconv(rows, du2, full["w_dw"][::-1], None, taps - 1 - taps // 2, F32, "conf_conv_dx")
    dpvg = glu_bwd(rows, pvg, du, "glu_bwd")

    grads["w_lru_out"] = matmul(yl_in, dyl, "tn", BF16, "lru_out_dw")
    dyl = scatter_behind(['w_out', 'w_conf_out', 'w_lru_out'], "mixer", 3, settle(ffn2_names, dyl))
    dyl_in = matmul(dyl, wb["w_lru_out"], "nt", BF16, "lru_out_dx")
    dh_sum, dgl = lru_merge_bwd(rows, h_scan, gl, dyl_in, "lru_merge_bwd")
    da_gate, db_gate = lru_scan_bwd(rows, dh_sum, a_gate, h_prev, "lru_scan_bwd")
    dxr, dwcat, dbcat, dlam = lru_gates_bwd(rows, xr, wcat, bcat, lam, da_gate, db_gate, "lru_gates_bwd")
    grads["w_rec_gate"] = jnp.stack([dwcat[:, :, 0:LANES], dwcat[:, :, 2 * LANES:3 * LANES]])
    grads["w_in_gate"] = jnp.stack([dwcat[:, :, LANES:2 * LANES], dwcat[:, :, 3 * LANES:4 * LANES]])
    gate_names = ['w_rec_gate', 'w_in_gate']
    gate_eighths, dxr = lax.optimization_barrier(([grads[n].reshape(8, -1, LANES) for n in gate_names], dxr))
    gate_parts = exchange_by_sequencer(gate_eighths, True, "scatter_gate_grads", 10, kind="all")
    dbcat = dbcat.reshape(n_blk, 4, LANES)
    grads["b_rec_gate"] = jnp.stack([dbcat[:, 0], dbcat[:, 2]]).reshape(2, lru_w)
    grads["b_in_gate"] = jnp.stack([dbcat[:, 1], dbcat[:, 3]]).reshape(2, lru_w)
    grads["lru_lambda"] = jnp.transpose(dlam.reshape(n_blk, 2, LANES), (1, 0, 2)).reshape(2, lru_w)
    grads["w_lru_conv"], grads["b_lru_conv"] = dwconv_bwd_w(rows, ux, dxr, lru_taps, LRU_PAD_LEFT, "lru_conv_dw")
    dux = dwconv(rows, dxr, full["w_lru_conv"][::-1], None, lru_taps - 1 - LRU_PAD_LEFT, BF16, "lru_conv_dx")

    dproj = [dpvg, dux, dgl, dpgc]
    grads["w_in"] = jnp.concatenate([matmul(h2, dp, "tn", BF16, f"in_proj{k}_dw") for k, dp in enumerate(dproj)], axis=1)
    dproj = scatter_behind(['w_in'], "in_proj", 4, settle(['w_out', 'w_conf_out', 'w_lru_out'], dproj))
    dh2, db_in = matmul_groups_nt(dproj, w_in_groups, "in_proj_dx")
    grads["b_in"] = jnp.concatenate(db_in, axis=1)
    dx1, grads["g_n2"], dmv[3], dmv[4], df1, dmv[2] = norm_mod_bwd(rows, x1, dh2, dx2, g_n2, mv, 3, 4, "norm2_bwd", then_gate=(f1, 2, 0.5))
    gate_parts, dx1 = lax.optimization_barrier((gate_parts, dx1))
    gate_sums = exchange_by_sequencer([sum_parts(p, f"sum_gate_grads{k}") for k, p in enumerate(gate_parts)], False,
                                      "gather_gate_grads", 11, kind="all")

    dx0, grads["g_n1"] = ffn_bwd("ffn1", dx1, df1, x0, h1, gu1, act1, wb["w_ffn1_up"], wb["w_ffn1_down"], g_n1, (0, 1), 6,
                                 settle_first=['w_in'], latent_only=True)
    grad_x = dx0[None]

    out = {}
    for n in big:
        parts = from_chips[n].reshape(4, -1, from_chips[n].shape[-1])
        out[n] = adamw_pair(parts, flat2(wt[n]), flat2(mom[n]), flat2(var[n]), f"adamw_{n}")

    dmod = jnp.concatenate(dmv, axis=1)
    small_names = ['g_n1', 'g_n2', 'b_in', 'b_dw', 'g_ln', 'b_ln', 'b_lru_conv', 'g_n3', 'g_final'] + SMALL_SHARDED
    small_list = [loss_part] + [grads[n] for n in small_names] + [dmod[0], dmod[1]]
    small_buf = _pack(small_list, 8 * 2 * SUBLANES)
    small_parts, dmod_all = exchange("all", [small_buf.reshape(8, -1, LANES), _pack([dmod[0]])], [True, False], "scatter_small_grads")
    small_sum = exchange("all", [sum_parts(small_parts, "sum_small_grads")], False, "gather_small_grads")[0]
    small_sum = _unpack(small_sum.reshape(small_buf.shape), [a.shape for a in small_list])
    loss = small_sum[0][0, 0]
    total = dict(zip(small_names, small_sum[1:]))
    total.update({n: s.reshape(grads[n].shape) for n, s in zip(gate_names, gate_sums)})
    dmod_all = _unpack(dmod_all, [dmod[0].shape])[0].reshape(8, N_MOD * d)
    dmc = small_sum[-1].reshape(1, N_MOD * d)
    total["b_mod"] = small_sum[-2].reshape(1, N_MOD * d) + dmc

    d9 = jnp.concatenate([dmod_all, dmc, jnp.zeros((7, N_MOD * d), F32)], axis=0)
    d9_cols = lax.dynamic_slice_in_dim(d9, chip * w_mod.shape[2], w_mod.shape[2], axis=1)
    g_wmod, dc9 = mod_matmul_bwd(c9, d9_cols, w_mod[0], "mod_matmul_bwd")
    dc_all = exchange("all", [dc9[8:16]], False, "gather_dc")[0]
    total["c_ctx"] = sum_parts(jnp.stack([dc_all[0], dc_all[2], dc_all[4], dc_all[6]]), "sum_dc")[0:1]
    out["w_mod"] = adamw(g_wmod[None], w_mod[0], m_w_mod[0], v_w_mod[0], "adamw_w_mod")

    for n in gate_names:
        out[n] = adamw(total[n].reshape(1, -1, LANES), flat2(wt[n]), flat2(mom[n]), flat2(var[n]), f"adamw_{n}")

    small_all_names = [n for n in W_NAMES if n not in out]
    g_local, shapes = [], []
    for n in small_all_names:
        g = total[n].reshape(flat2(wt[n]).shape[:-1] + (-1,)) if n in SMALL_SHARDED else total[n].reshape(flat2(wt[n]).shape)
        if n in SMALL_SHARDED:
            width = wt[n].shape[-1]
            g = lax.dynamic_slice_in_dim(g, chip * width, width, axis=1)
        g_local.append(g)
        shapes.append(g.shape)
    packed = adamw(_pack(g_local)[None], _pack([flat2(wt[n]) for n in small_all_names]),
                   _pack([flat2(mom[n]) for n in small_all_names]), _pack([flat2(var[n]) for n in small_all_names]), "adamw_small")
    for k, n in enumerate(small_all_names):
        out[n] = [_unpack(p, shapes)[k] for p in packed]

    results = [loss, grad_x]
    for k in range(4):
        results += [out[n][k].reshape(wt[n].shape) for n in W_NAMES]
    return tuple(results)
```
